```python
import math
import jax, jax.numpy as jnp
from jax import lax
import numpy as np

D_MODEL = 1024
BATCH = 2
SEQ = 16384
DEPTH = 1

CTX_LEN = 256
GRID_W = 64
MIX_W = D_MODEL
GDN_W = MIX_W // 2
SGU_W = MIX_W - GDN_W
HEAD_DIM = 128
GDN_HEADS = GDN_W // HEAD_DIM
SGU_GROUP = 128
SGU_GROUPS = SGU_W // SGU_GROUP
SGU_CHUNK = 128
DELTA_CHUNK = 64
CONV_W = 3
N_EXPERTS = 32
TOP_K = 4
D_FF = D_MODEL
SWIGLU_LIMIT = 7.0
SWIGLU_ALPHA = 1.702
MOE_BLOCK = 128
NORM_EPS = 1e-6

QKV_COLS = 3 * GDN_W
STATE_COLS = QKV_COLS + 4 * GDN_HEADS
Z_END = STATE_COLS + GDN_W
U_END = Z_END + SGU_W
PROJ_COLS = U_END + SGU_W

kernel_name = "hybrid_gdn_sgu_moe_prefix_dit"


def _rmsnorm(x, g):
    x32 = x.astype(jnp.float32)
    y = x32 * lax.rsqrt(jnp.mean(x32 * x32, -1, keepdims=True) + NORM_EPS)
    return (y * g.astype(jnp.float32)).astype(x.dtype)


def _layernorm(x, g, b):
    x32 = x.astype(jnp.float32)
    xc = x32 - jnp.mean(x32, -1, keepdims=True)
    y = xc * lax.rsqrt(jnp.mean(xc * xc, -1, keepdims=True) + NORM_EPS)
    return (y * g + b).astype(x.dtype)


def _l2norm(x):
    x32 = x.astype(jnp.float32)
    return x32 * lax.rsqrt(jnp.sum(x32 * x32, -1, keepdims=True) + NORM_EPS)


def _dwconv_centred(x, w):
    pad = CONV_W // 2
    L = x.shape[-2]
    xp = jnp.pad(x, [(0, 0)] * (x.ndim - 2) + [(pad, pad), (0, 0)])
    return sum(xp[..., j:j + L, :] * w[j] for j in range(CONV_W))


def _state_step(s, u_i, w_i, kd_i, gl_i):
    v_new = u_i - jnp.einsum('bhck,bhkv->bhcv', w_i, s)
    s_next = s * gl_i[..., None, None] + jnp.einsum('bhck,bhcv->bhkv', kd_i, v_new)
    return s_next, v_new


def _chunk_gated_delta(q, k, v, g, beta, s0, need_out):
    B, H, L, Dk = q.shape
    Dv = v.shape[-1]
    C = DELTA_CHUNK
    N = L // C
    q = q.reshape(B, H, N, C, Dk)
    k = k.reshape(B, H, N, C, Dk)
    v = v.reshape(B, H, N, C, Dv)
    g = jnp.cumsum(g.reshape(B, H, N, C), -1)
    beta = beta.reshape(B, H, N, C)
    incl = jnp.tril(jnp.ones((C, C), bool))
    strict = jnp.tril(jnp.ones((C, C), bool), -1)
    decay = jnp.where(incl, jnp.exp(jnp.where(incl, g[..., :, None] - g[..., None, :], 0.0)), 0.0)
    kb = k * beta[..., None]
    a = jnp.where(strict, jnp.einsum('bhnid,bhnjd->bhnij', kb, k) * decay, 0.0)
    eye = jnp.eye(C, dtype=q.dtype)
    t_inv = lax.linalg.triangular_solve(eye + a, jnp.broadcast_to(eye, a.shape), left_side=True,
                                        lower=True, unit_diagonal=True)
    u = jnp.einsum('bhnij,bhnjd->bhnid', t_inv, v * beta[..., None])
    w = jnp.einsum('bhnij,bhnjd->bhnid', t_inv, kb * jnp.exp(g)[..., None])
    k_dec = k * jnp.exp(g[..., -1:] - g)[..., None]
    g_last = jnp.exp(g[..., -1])
    to_scan = lambda t: jnp.moveaxis(t, 2, 0)
    if need_out:
        attn = jnp.where(incl, jnp.einsum('bhnid,bhnjd->bhnij', q, k) * decay, 0.0)
        q_dec = q * jnp.exp(g)[..., None]

        def step(s, xs):
            u_i, w_i, kd_i, gl_i, qd_i, at_i = xs
            s_next, v_new = _state_step(s, u_i, w_i, kd_i, gl_i)
            o = jnp.einsum('bhck,bhkv->bhcv', qd_i, s) + jnp.einsum('bhij,bhjv->bhiv', at_i, v_new)
            return s_next, o

        s, o = lax.scan(step, s0, tuple(map(to_scan, (u, w, k_dec, g_last, q_dec, attn))))
        return jnp.moveaxis(o, 0, 2).reshape(B, H, L, Dv), s

    def step_state(s, xs):
        s_next, _ = _state_step(s, *xs)
        return s_next, None

    s, _ = lax.scan(step_state, s0, tuple(map(to_scan, (u, w, k_dec, g_last))))
    return None, s


def _gdn_prepare(p, conv_w, a_log, dt_bias, rows):
    B, L, _ = p.shape
    qkv = p[..., :QKV_COLS].reshape(B, rows, L // rows, QKV_COLS)
    qkv = jax.nn.silu(_dwconv_centred(qkv, conv_w)).astype(jnp.float32)
    qkv = qkv.reshape(B, L, 3, GDN_HEADS, HEAD_DIM).transpose(2, 0, 3, 1, 4)
    q = _l2norm(qkv[0]) * (HEAD_DIM ** -0.5)
    k = _l2norm(qkv[1])
    v = qkv[2]
    ba = p[..., QKV_COLS:STATE_COLS].astype(jnp.float32).reshape(B, L, 2, 2, GDN_HEADS)
    ba = ba.transpose(2, 3, 0, 4, 1)
    beta = jax.nn.sigmoid(ba[0])
    g = -jnp.exp(a_log)[:, None, :, None] * jax.nn.softplus(ba[1] + dt_bias[:, None, :, None])
    return q, k, v, g, beta


def _gdn_bidir(q, k, v, g, beta, s0, need_out):
    flip = lambda t: jnp.flip(t, axis=2)
    o_f, s_f = _chunk_gated_delta(q, k, v, g[0], beta[0], s0[0], need_out)
    o_b, s_b = _chunk_gated_delta(flip(q), flip(k), flip(v), flip(g[1]), flip(beta[1]), s0[1], need_out)
    o = o_f + flip(o_b) if need_out else None
    return o, jnp.stack([s_f, s_b])


def _gdn_output(o, z, norm_g):
    B, H, L, Dv = o.shape
    o = o.transpose(0, 2, 1, 3)
    zz = z.reshape(B, L, H, Dv).astype(jnp.float32)
    o = o * lax.rsqrt(jnp.mean(o * o, -1, keepdims=True) + NORM_EPS) * norm_g * jax.nn.silu(zz)
    return o.reshape(B, L, GDN_W).astype(z.dtype)


def _sgu(p, ln_g, ln_b, w_s, b_s):
    B, L, _ = p.shape
    shp = (B, L // SGU_CHUNK, SGU_CHUNK, SGU_GROUPS, SGU_GROUP)
    u = jax.nn.gelu(p[..., Z_END:U_END]).reshape(shp)
    v = _layernorm(jax.nn.gelu(p[..., U_END:PROJ_COLS]).reshape(shp), ln_g, ln_b)
    s = jnp.einsum('gpq,bnqgc->bnpgc', w_s, v) + b_s.T[:, :, None]
    return (u * s).reshape(B, L, SGU_W)


def _moe(h, router_w, router_b, w_gu, b_gu, w_down, b_down):
    T, D = h.shape
    logits = (h @ router_w + router_b).astype(jnp.float32)
    top_v, top_i = lax.top_k(logits, TOP_K)
    gates = jax.nn.softmax(top_v, axis=-1).astype(h.dtype)
    n_assign = T * TOP_K
    flat_e = top_i.reshape(-1)
    order = jnp.argsort(flat_e)
    e_sorted = flat_e[order]
    tok_sorted = (order // TOP_K).astype(jnp.int32)
    gate_sorted = gates.reshape(-1)[order]
    counts = jnp.bincount(flat_e, length=N_EXPERTS)
    padded = (counts + MOE_BLOCK - 1) // MOE_BLOCK * MOE_BLOCK
    pad_end = jnp.cumsum(padded)
    dest = ((pad_end - padded)[e_sorted] + jnp.arange(n_assign)
            - (jnp.cumsum(counts) - counts)[e_sorted])
    n_blocks = -(-n_assign // MOE_BLOCK) + N_EXPERTS
    slot_tok = jnp.full((n_blocks * MOE_BLOCK,), T, jnp.int32).at[dest].set(tok_sorted)
    block_e = jnp.minimum(jnp.searchsorted(pad_end, jnp.arange(n_blocks) * MOE_BLOCK, side='right'),
                          N_EXPERTS - 1)
    xb = jnp.concatenate([h, jnp.zeros((1, D), h.dtype)])[slot_tok].reshape(n_blocks, MOE_BLOCK, D)

    def expert_block(args):
        xblk, e = args
        gu = xblk @ w_gu[e] + b_gu[e]
        gate = jnp.minimum(gu[:, ::2], SWIGLU_LIMIT)
        up = jnp.clip(gu[:, 1::2], -SWIGLU_LIMIT, SWIGLU_LIMIT)
        act = (up + 1.0) * gate * jax.nn.sigmoid(SWIGLU_ALPHA * gate)
        return act @ w_down[e] + b_down[e]

    yb = lax.map(expert_block, (xb, block_e))
    y = yb.reshape(-1, D)[dest] * gate_sorted[:, None]
    return jnp.zeros_like(h).at[tok_sorted].add(y)


def setup_inputs(seed: int = 0) -> dict:
    key = jax.random.key(seed)
    ks = jax.random.split(key, 24)
    D = D_MODEL
    nrm = lambda k, shape, s: jax.random.normal(k, shape, jnp.float32) * s
    dt = jnp.exp(jax.random.uniform(ks[10], (DEPTH, 2, GDN_HEADS), jnp.float32,
                                    minval=math.log(1e-3), maxval=math.log(1e-1)))
    return {
        "x": nrm(ks[0], (BATCH, SEQ, D), 1.0),
        "c": nrm(ks[1], (BATCH, D), 1.0),
        "ctx": nrm(ks[2], (BATCH, CTX_LEN, D), 1.0),
        "c_ctx": nrm(ks[3], (D,), 1.0),
        "w_ada": nrm(ks[4], (DEPTH, D, 6 * D), D ** -0.5),
        "b_ada": nrm(ks[5], (DEPTH, 6 * D), 0.02),
        "norm_g": 1.0 + nrm(ks[6], (DEPTH, 4, D), 0.05),
        "w_in": nrm(ks[7], (DEPTH, D, PROJ_COLS), D ** -0.5),
        "conv_w": nrm(ks[8], (DEPTH, CONV_W, QKV_COLS), CONV_W ** -0.5),
        "a_log": jnp.log(jax.random.uniform(ks[9], (DEPTH, 2, GDN_HEADS), jnp.float32, minval=1.0, maxval=16.0)),
        "dt_bias": dt + jnp.log(-jnp.expm1(-dt)),
        "gdn_norm_g": 1.0 + nrm(ks[11], (DEPTH, HEAD_DIM), 0.05),
        "sgu_ln_g": 1.0 + nrm(ks[12], (DEPTH, SGU_GROUPS, SGU_GROUP), 0.05),
        "sgu_ln_b": nrm(ks[13], (DEPTH, SGU_GROUPS, SGU_GROUP), 0.02),
        "sgu_w": nrm(ks[14], (DEPTH, SGU_GROUPS, SGU_CHUNK, SGU_CHUNK), SGU_CHUNK ** -0.5),
        "sgu_b": 1.0 + nrm(ks[15], (DEPTH, SGU_GROUPS, SGU_CHUNK), 0.1),
        "w_out": nrm(ks[16], (DEPTH, MIX_W, D), MIX_W ** -0.5),
        "router_w": nrm(ks[17], (DEPTH, D, N_EXPERTS), D ** -0.5),
        "router_b": nrm(ks[18], (DEPTH, N_EXPERTS), 0.01),
        "w_gu": nrm(ks[19], (DEPTH, N_EXPERTS, D, 2 * D_FF), D ** -0.5),
        "b_gu": nrm(ks[20], (DEPTH, N_EXPERTS, 2 * D_FF), 0.02),
        "w_down": nrm(ks[21], (DEPTH, N_EXPERTS, D_FF, D), D_FF ** -0.5),
        "b_down": nrm(ks[22], (DEPTH, N_EXPERTS, D), 0.02),
    }


def reference(x, c, ctx, c_ctx, w_ada, b_ada, norm_g, w_in, conv_w, a_log, dt_bias, gdn_norm_g,
              sgu_ln_g, sgu_ln_b, sgu_w, sgu_b, w_out, router_w, router_b, w_gu, b_gu, w_down, b_down):
    B, L, D = x.shape
    rows = L // GRID_W
    xc = ctx
    for layer in range(DEPTH):
        last = layer == DEPTH - 1
        mod = (jax.nn.silu(c) @ w_ada[layer] + b_ada[layer]).reshape(B, 6, 1, D)
        mod_c = (jax.nn.silu(c_ctx) @ w_ada[layer] + b_ada[layer]).reshape(6, 1, 1, D)
        ng = norm_g[layer]

        h = _rmsnorm(x, ng[0]) * (1.0 + mod[:, 1]) + mod[:, 0]
        hc = _rmsnorm(xc, ng[0]) * (1.0 + mod_c[1]) + mod_c[0]
        p = h @ w_in[layer]
        pc = hc @ (w_in[layer][:, :STATE_COLS] if last else w_in[layer])

        qc, kc, vc, gcx, bcx = _gdn_prepare(pc, conv_w[layer], a_log[layer], dt_bias[layer], 1)
        s_zero = jnp.zeros((2, B, GDN_HEADS, HEAD_DIM, HEAD_DIM), jnp.float32)
        oc, s_ctx = _gdn_bidir(qc, kc, vc, gcx, bcx, s_zero, not last)

        q, k, v, g, beta = _gdn_prepare(p, conv_w[layer], a_log[layer], dt_bias[layer], rows)
        o, _ = _gdn_bidir(q, k, v, g, beta, s_ctx, True)
        mix = jnp.concatenate([
            _gdn_output(o, p[..., STATE_COLS:Z_END], gdn_norm_g[layer]),
            _sgu(p, sgu_ln_g[layer], sgu_ln_b[layer], sgu_w[layer], sgu_b[layer])], axis=-1)
        x = x + mod[:, 2] * _rmsnorm(mix @ w_out[layer], ng[1])
        if not last:
            mix_c = jnp.concatenate([
                _gdn_output(oc, pc[..., STATE_COLS:Z_END], gdn_norm_g[layer]),
                _sgu(pc, sgu_ln_g[layer], sgu_ln_b[layer], sgu_w[layer], sgu_b[layer])], axis=-1)
            xc = xc + mod_c[2] * _rmsnorm(mix_c @ w_out[layer], ng[1])

        h = (_rmsnorm(x, ng[2]) * (1.0 + mod[:, 4]) + mod[:, 3]).reshape(-1, D)
        if not last:
            hc = (_rmsnorm(xc, ng[2]) * (1.0 + mod_c[4]) + mod_c[3]).reshape(-1, D)
            h = jnp.concatenate([h, hc], axis=0)
        y = _moe(h, router_w[layer], router_b[layer], w_gu[layer], b_gu[layer], w_down[layer], b_down[layer])
        x = x + mod[:, 5] * _rmsnorm(y[:B * L].reshape(B, L, D), ng[3])
        if not last:
            xc = xc + mod_c[5] * _rmsnorm(y[B * L:].reshape(xc.shape), ng[3])
    return x
```

```python
import functools
import math

import jax
import jax.numpy as jnp
from jax import lax
from jax.experimental import pallas as pl
from jax.experimental.pallas import tpu as pltpu

F32 = jnp.float32
BF16 = jnp.bfloat16

D_MODEL = 1024
GDN_HEADS = 4
HEAD_DIM = 128
GDN_W = GDN_HEADS * HEAD_DIM
SGU_GROUPS = 4
SGU_GROUP = 128
SGU_W = SGU_GROUPS * SGU_GROUP
SGU_CHUNK = 128
DELTA_CHUNK = 64
GRID_W = 64
N_EXPERTS = 32
TOP_K = 4
D_FF = 1024
SWIGLU_LIMIT = 7.0
SWIGLU_ALPHA = 1.702
NORM_EPS = 1e-6
QKV_COLS = 3 * GDN_W
N_CHAINS = 2 * GDN_HEADS
N_GATE_COLS = 2 * N_CHAINS

ROW_TILE = 512
GDN_BLOCK = 256
CHUNKS_PER_BLOCK = GDN_BLOCK // DELTA_CHUNK
MOE_ROWS = 256
VMEM_LIMIT = 56 * 1024 * 1024


def _params(*sem):
    return pltpu.CompilerParams(dimension_semantics=sem, vmem_limit_bytes=VMEM_LIMIT)


def _dot(a, b):
    return jnp.dot(a, b, preferred_element_type=F32)


def _dot_nt(a, b):
    return lax.dot_general(a, b, (((1,), (1,)), ((), ())), preferred_element_type=F32)


def _dot_tn(a, b):
    return lax.dot_general(a, b, (((0,), (0,)), ((), ())), preferred_element_type=F32)


def _split2(a):
    hi = a.astype(BF16)
    lo = (a - hi.astype(F32)).astype(BF16)
    return hi, lo


def _split3(a):
    hi = a.astype(BF16)
    r = a - hi.astype(F32)
    mid = r.astype(BF16)
    lo = (r - mid.astype(F32)).astype(BF16)
    return hi, mid, lo


def _rms(x32, g):
    return x32 * lax.rsqrt(jnp.mean(x32 * x32, -1, keepdims=True) + NORM_EPS) * g


def _gelu_tanh(x):
    c = math.sqrt(2.0 / math.pi)
    return 0.5 * x * (1.0 + jnp.tanh(c * (x + 0.044715 * (x * x * x))))


def _sigmoid(x):
    return 1.0 / (1.0 + jnp.exp(-x))


def _softplus(x):
    return jnp.maximum(x, 0.0) + jnp.log(1.0 + jnp.exp(-jnp.abs(x)))


def _ada_kernel(c_ref, w_ref, b_ref, o_ref):
    c = c_ref[...]
    s = c * _sigmoid(c)
    s_hi, s_lo = _split2(s)
    w_hi, w_lo = _split2(w_ref[...])
    o_ref[...] = _dot(s_hi, w_hi) + _dot(s_lo, w_hi) + _dot(s_hi, w_lo) + b_ref[...]


def _ada(cs, w_ada, b_ada):
    n = w_ada.shape[1]
    bn = D_MODEL
    return pl.pallas_call(
        _ada_kernel,
        out_shape=jax.ShapeDtypeStruct((cs.shape[0], n), F32),
        grid=(n // bn,),
        in_specs=[pl.BlockSpec(cs.shape, lambda j: (0, 0)),
                  pl.BlockSpec((D_MODEL, bn), lambda j: (0, j)),
                  pl.BlockSpec((1, bn), lambda j: (0, j))],
        out_specs=pl.BlockSpec((cs.shape[0], bn), lambda j: (0, j)),
        compiler_params=_params("parallel"),
        name="ada_mod",
    )(cs, w_ada, b_ada)


def _inproj_kernel(x_ref, mod_ref, g_ref, wqkv_ref, wzuv_ref, wba_ref, wbat_ref,
                   qkv_ref, z_ref, u_ref, v_ref, ba_ref, bat_ref):
    x = x_ref[...]
    mod = mod_ref[0]
    h = _rms(x, g_ref[...]) * (1.0 + mod[1:2]) + mod[0:1]
    h_hi, h_lo = _split2(h)
    qkv_ref[...] = _dot(h_hi, wqkv_ref[...]).astype(BF16)
    zuv = _dot(h_hi, wzuv_ref[...])
    z_ref[...] = zuv[:, :GDN_W].astype(BF16)
    u_ref[...] = _gelu_tanh(zuv[:, GDN_W:GDN_W + SGU_W]).astype(BF16)
    v_ref[...] = _gelu_tanh(zuv[:, GDN_W + SGU_W:]).astype(BF16)
    w_hi, w_lo = _split2(wba_ref[...])
    ba_ref[...] = _dot(h_hi, w_hi) + _dot(h_lo, w_hi) + _dot(h_hi, w_lo)
    wt_hi, wt_lo = _split2(wbat_ref[...])
    bat_ref[...] = _dot_nt(wt_hi, h_hi) + _dot_nt(wt_lo, h_hi) + _dot_nt(wt_hi, h_lo)


def _inproj(x2d, mod, rows_per_mod, ng0, wqkv, wzuv, wba, wbat):
    t = x2d.shape[0]
    tm = min(ROW_TILE, t)
    tiles_per_mod = rows_per_mod // tm
    row = lambda i: (i, 0)
    const = lambda i: (0, 0)
    return pl.pallas_call(
        _inproj_kernel,
        out_shape=(jax.ShapeDtypeStruct((t, QKV_COLS), BF16),
                   jax.ShapeDtypeStruct((t, GDN_W), BF16),
                   jax.ShapeDtypeStruct((t, SGU_W), BF16),
                   jax.ShapeDtypeStruct((t, SGU_W), BF16),
                   jax.ShapeDtypeStruct((t, N_GATE_COLS), F32),
                   jax.ShapeDtypeStruct((N_GATE_COLS, t), F32)),
        grid=(t // tm,),
        in_specs=[pl.BlockSpec((tm, D_MODEL), row),
                  pl.BlockSpec((1, 6, D_MODEL), lambda i: (i // tiles_per_mod, 0, 0)),
                  pl.BlockSpec((1, D_MODEL), const),
                  pl.BlockSpec(wqkv.shape, const),
                  pl.BlockSpec(wzuv.shape, const),
                  pl.BlockSpec(wba.shape, const),
                  pl.BlockSpec(wbat.shape, const)],
        out_specs=(pl.BlockSpec((tm, QKV_COLS), row),
                   pl.BlockSpec((tm, GDN_W), row),
                   pl.BlockSpec((tm, SGU_W), row),
                   pl.BlockSpec((tm, SGU_W), row),
                   pl.BlockSpec((tm, N_GATE_COLS), row),
                   pl.BlockSpec((N_GATE_COLS, tm), lambda i: (0, i))),
        compiler_params=_params("parallel"),
        name="in_proj",
    )(x2d, mod, ng0, wqkv, wzuv, wba, wbat)


def _gdn_prep_kernel(row_len, qkv_ref, cw_ref, ba_ref, bat_ref, alog_r_ref, dtb_r_ref,
                     alog_c_ref, dtb_c_ref, u_ref, w_ref, qd_ref, kd_ref, at_ref, gl_ref):
    n = GDN_BLOCK
    c = DELTA_CHUNK
    ri = lax.broadcasted_iota(jnp.int32, (n, n), 0)
    ci = lax.broadcasted_iota(jnp.int32, (n, n), 1)
    same = (ri // c) == (ci // c)
    lower = same & (ri >= ci)
    upper = same & (ri <= ci)
    diag = ri == ci
    def mask01(m):
        return jnp.where(m, 1.0, 0.0).astype(BF16)

    lower_b = mask01(lower)
    upper_b = mask01(upper)
    same_b = mask01(same)
    eye = jnp.where(diag, 1.0, 0.0)
    pair_masks = []
    s = 1
    while s < c:
        pair_masks.append(mask01(((ri // (2 * s)) == (ci // (2 * s))) & ((ri // s) != (ci // s))))
        s *= 2

    ba = ba_ref[0]
    bat = bat_ref[...]
    beta_c = _sigmoid(ba[:, :N_CHAINS])
    g_c = -jnp.exp(alog_r_ref[...]) * _softplus(ba[:, N_CHAINS:] + dtb_r_ref[...])
    g_r = -jnp.exp(alog_c_ref[...]) * _softplus(bat[N_CHAINS:] + dtb_c_ref[...])
    gc3 = _split3(g_c)
    gr3 = jnp.concatenate(_split3(g_r), axis=0)

    def sum3_r(m):
        return m[:N_CHAINS] + m[N_CHAINS:2 * N_CHAINS] + m[2 * N_CHAINS:]

    cum_f_c = _dot(lower_b, gc3[0]) + _dot(lower_b, gc3[1]) + _dot(lower_b, gc3[2])
    tot_c = _dot(same_b, gc3[0]) + _dot(same_b, gc3[1]) + _dot(same_b, gc3[2])
    cum_b_c = tot_c - cum_f_c + g_c
    cum_f_r = sum3_r(_dot(gr3, upper_b))
    cum_b_r = sum3_r(_dot(gr3, lower_b))
    ei = lax.broadcasted_iota(jnp.int32, (n, CHUNKS_PER_BLOCK * HEAD_DIM), 0) // c
    ej = lax.broadcasted_iota(jnp.int32, (n, CHUNKS_PER_BLOCK * HEAD_DIM), 1) // HEAD_DIM
    g_last = jnp.exp(sum3_r(_dot(gr3, mask01(ei == ej))))
    gl_ref[0, 0, 0] = g_last[:GDN_HEADS]
    gl_ref[0, 0, 1] = g_last[GDN_HEADS:]

    pos = lax.broadcasted_iota(jnp.int32, (n, HEAD_DIM), 0) % row_len
    first = pos == 0
    last = pos == row_len - 1

    def conv_silu(col):
        x = qkv_ref[0, :, col * HEAD_DIM:(col + 1) * HEAD_DIM].astype(F32)
        cw = cw_ref[:, col * HEAD_DIM:(col + 1) * HEAD_DIM]
        xp = jnp.where(first, 0.0, pltpu.roll(x, 1, 0))
        xn = jnp.where(last, 0.0, pltpu.roll(x, n - 1, 0))
        y = xp * cw[0:1] + x * cw[1:2] + xn * cw[2:3]
        return y * _sigmoid(y)

    def l2n(x):
        return x * lax.rsqrt(jnp.sum(x * x, -1, keepdims=True) + NORM_EPS)

    for h in range(GDN_HEADS):
        q = l2n(conv_silu(h)) * (HEAD_DIM ** -0.5)
        k = l2n(conv_silu(GDN_HEADS + h))
        v = conv_silu(2 * GDN_HEADS + h)
        k_b = k.astype(BF16)
        qk_kk = _dot_nt(jnp.concatenate([q.astype(BF16), k_b], axis=0), k_b)
        qk = qk_kk[:n]
        kk = qk_kk[n:]
        for d in range(2):
            j = d * GDN_HEADS + h
            mask = lower if d == 0 else upper
            cum_c = (cum_f_c if d == 0 else cum_b_c)[:, j:j + 1]
            cum_r = (cum_f_r if d == 0 else cum_b_r)[j:j + 1, :]
            b_c = beta_c[:, j:j + 1]
            decay = jnp.where(mask, jnp.exp(jnp.where(mask, cum_c - cum_r, 0.0)), 0.0)
            amat = jnp.where(diag, 0.0, kk * decay * b_c)
            a_b = amat.astype(BF16)
            p = eye - amat * pair_masks[0].astype(F32)
            for pm in pair_masks[1:]:
                p_b = p.astype(BF16)
                y = _dot(a_b * pm, p_b)
                p = p - _dot(p_b, y.astype(BF16))
            e_c = jnp.exp(cum_c)
            rhs = jnp.concatenate([(v * b_c).astype(BF16), (k * (b_c * e_c)).astype(BF16)], axis=1)
            uw = _dot(p.astype(BF16), rhs)
            cols = slice(j * HEAD_DIM, (j + 1) * HEAD_DIM)
            u_ref[0, :, cols] = uw[:, :HEAD_DIM].astype(BF16)
            w_ref[0, :, cols] = uw[:, HEAD_DIM:].astype(BF16)
            qd_ref[0, :, cols] = (q * e_c).astype(BF16)
            kd_ref[0, :, cols] = (k * jnp.exp(tot_c[:, j:j + 1] - cum_c)).astype(BF16)
            attn = qk * decay
            for cc in range(CHUNKS_PER_BLOCK):
                at_ref[0, cc * c:(cc + 1) * c, j * c:(j + 1) * c] = (
                    attn[cc * c:(cc + 1) * c, cc * c:(cc + 1) * c].astype(BF16))


def _gdn_prep(qkv, conv_w, ba, bat, alog, dtb, row_len):
    b, l, _ = qkv.shape
    nblk = l // GDN_BLOCK
    wide = N_CHAINS * HEAD_DIM
    blk = lambda bi, i: (bi, i, 0)
    const = lambda bi, i: (0, 0)
    alog_r, dtb_r = alog.reshape(1, N_CHAINS), dtb.reshape(1, N_CHAINS)
    alog_c, dtb_c = alog.reshape(N_CHAINS, 1), dtb.reshape(N_CHAINS, 1)
    return pl.pallas_call(
        functools.partial(_gdn_prep_kernel, row_len),
        out_shape=(jax.ShapeDtypeStruct((b, l, wide), BF16),) * 4 + (
            jax.ShapeDtypeStruct((b, l, N_CHAINS * DELTA_CHUNK), BF16),
            jax.ShapeDtypeStruct((b, nblk, 2, GDN_HEADS, CHUNKS_PER_BLOCK * HEAD_DIM), F32)),
        grid=(b, nblk),
        in_specs=[pl.BlockSpec((1, GDN_BLOCK, QKV_COLS), blk),
                  pl.BlockSpec((3, QKV_COLS), const),
                  pl.BlockSpec((1, GDN_BLOCK, N_GATE_COLS), blk),
                  pl.BlockSpec((N_GATE_COLS, GDN_BLOCK), lambda bi, i: (0, bi * nblk + i)),
                  pl.BlockSpec((1, N_CHAINS), const),
                  pl.BlockSpec((1, N_CHAINS), const),
                  pl.BlockSpec((N_CHAINS, 1), const),
                  pl.BlockSpec((N_CHAINS, 1), const)],
        out_specs=(pl.BlockSpec((1, GDN_BLOCK, wide), blk),) * 4 + (
            pl.BlockSpec((1, GDN_BLOCK, N_CHAINS * DELTA_CHUNK), blk),
            pl.BlockSpec((1, 1, 2, GDN_HEADS, CHUNKS_PER_BLOCK * HEAD_DIM),
                         lambda bi, i: (bi, i, 0, 0, 0))),
        compiler_params=_params("parallel", "parallel"),
        name="gdn_prep",
    )(qkv, conv_w, ba, bat, alog_r, dtb_r, alog_c, dtb_c)


def _gdn_scan_kernel(uf, wf, qf, kf, af, gf, ub, wb, qb, kb, ab, gb, s0_ref,
                     of_ref, ob_ref, sfin_ref, s_scr):
    i = pl.program_id(1)
    c = DELTA_CHUNK

    @pl.when(i == 0)
    def _():
        s_scr[...] = s0_ref[0]

    ops = ((uf, wf, qf, kf, af, gf, of_ref), (ub, wb, qb, kb, ab, gb, ob_ref))
    for d in range(2):
        u_ref, w_ref, q_ref, k_ref, a_ref, g_ref, o_ref = ops[d]
        for h in range(GDN_HEADS):
            j = d * GDN_HEADS + h
            cols = slice(h * HEAD_DIM, (h + 1) * HEAD_DIM)
            s = s_scr[j]
            for step in range(CHUNKS_PER_BLOCK):
                cc = step if d == 0 else CHUNKS_PER_BLOCK - 1 - step
                rows = slice(cc * c, (cc + 1) * c)
                wq = jnp.concatenate([w_ref[0, rows, cols], q_ref[0, rows, cols]], axis=0)
                x = _dot(wq, s.astype(BF16))
                v_new = (u_ref[0, rows, cols].astype(F32) - x[:c]).astype(BF16)
                o_ref[0, rows, cols] = x[c:] + _dot(a_ref[0, rows, h * c:(h + 1) * c], v_new)
                ds = _dot_tn(k_ref[0, rows, cols], v_new)
                s = s * g_ref[0, 0, 0, h:h + 1, cc * HEAD_DIM:(cc + 1) * HEAD_DIM] + ds
            s_scr[j] = s

    @pl.when(i == pl.num_programs(1) - 1)
    def _():
        sfin_ref[0] = s_scr[...]


def _gdn_scan(u, w, qd, kd, at, gl, s0):
    b, l, _ = u.shape
    nblk = l // GDN_BLOCK
    half = GDN_HEADS * HEAD_DIM
    fwd = lambda bi, i: (bi, i, 0)
    bwd = lambda bi, i: (bi, nblk - 1 - i, 1)
    big = lambda m: pl.BlockSpec((1, GDN_BLOCK, half), m)
    att = lambda m: pl.BlockSpec((1, GDN_BLOCK, GDN_HEADS * DELTA_CHUNK), m)
    gl_shape = (1, 1, 1, GDN_HEADS, CHUNKS_PER_BLOCK * HEAD_DIM)
    glf = pl.BlockSpec(gl_shape, lambda bi, i: (bi, i, 0, 0, 0))
    glb = pl.BlockSpec(gl_shape, lambda bi, i: (bi, nblk - 1 - i, 1, 0, 0))
    state = pl.BlockSpec((1, N_CHAINS, HEAD_DIM, HEAD_DIM), lambda bi, i: (bi, 0, 0, 0))
    return pl.pallas_call(
        _gdn_scan_kernel,
        out_shape=(jax.ShapeDtypeStruct((b, l, half), F32),
                   jax.ShapeDtypeStruct((b, l, half), F32),
                   jax.ShapeDtypeStruct((b, N_CHAINS, HEAD_DIM, HEAD_DIM), F32)),
        grid=(b, nblk),
        in_specs=[big(fwd), big(fwd), big(fwd), big(fwd), att(fwd), glf,
                  big(bwd), big(bwd), big(bwd), big(bwd), att(bwd), glb, state],
        out_specs=(pl.BlockSpec((1, GDN_BLOCK, half), fwd),
                   pl.BlockSpec((1, GDN_BLOCK, half), lambda bi, i: (bi, nblk - 1 - i, 0)),
                   state),
        scratch_shapes=[pltpu.VMEM((N_CHAINS, HEAD_DIM, HEAD_DIM), F32)],
        compiler_params=_params("parallel", "arbitrary"),
        name="gdn_scan",
    )(u, w, qd, kd, at, gl, u, w, qd, kd, at, gl, s0)


def _mix_kernel(of_ref, ob_ref, z_ref, u_ref, v_ref, gng_ref, lng_ref, lnb_ref, ws_ref, bs_ref,
                mix_ref):
    tm = of_ref.shape[0]
    o = of_ref[...] + ob_ref[...]
    z = z_ref[...].astype(F32)
    for h in range(GDN_HEADS):
        cols = slice(h * HEAD_DIM, (h + 1) * HEAD_DIM)
        oh = o[:, cols]
        zh = z[:, cols]
        r = lax.rsqrt(jnp.mean(oh * oh, -1, keepdims=True) + NORM_EPS)
        mix_ref[:, cols] = (oh * r * gng_ref[...] * (zh * _sigmoid(zh))).astype(BF16)
    for g in range(SGU_GROUPS):
        cols = slice(g * SGU_GROUP, (g + 1) * SGU_GROUP)
        vg = v_ref[:, cols].astype(F32)
        vc = vg - jnp.mean(vg, -1, keepdims=True)
        vn = vc * lax.rsqrt(jnp.mean(vc * vc, -1, keepdims=True) + NORM_EPS)
        vn = (vn * lng_ref[g:g + 1] + lnb_ref[g:g + 1]).astype(BF16)
        wsg = ws_ref[g].astype(BF16)
        for n in range(tm // SGU_CHUNK):
            rows = slice(n * SGU_CHUNK, (n + 1) * SGU_CHUNK)
            s = _dot(wsg, vn[rows]) + bs_ref[g]
            mix_ref[rows, GDN_W + g * SGU_GROUP:GDN_W + (g + 1) * SGU_GROUP] = (
                u_ref[rows, cols].astype(F32) * s).astype(BF16)


def _mix(o_f, o_b, z, ug, vg, gdn_norm_g, ln_g, ln_b, w_s, b_s_full):
    t = z.shape[0]
    tm = min(ROW_TILE, t)
    row = lambda i: (i, 0)
    c2 = lambda i: (0, 0)
    c3 = lambda i: (0, 0, 0)
    return pl.pallas_call(
        _mix_kernel,
        out_shape=jax.ShapeDtypeStruct((t, D_MODEL), BF16),
        grid=(t // tm,),
        in_specs=[pl.BlockSpec((tm, GDN_W), row), pl.BlockSpec((tm, GDN_W), row),
                  pl.BlockSpec((tm, GDN_W), row), pl.BlockSpec((tm, SGU_W), row),
                  pl.BlockSpec((tm, SGU_W), row),
                  pl.BlockSpec((1, HEAD_DIM), c2),
                  pl.BlockSpec((SGU_GROUPS, SGU_GROUP), c2),
                  pl.BlockSpec((SGU_GROUPS, SGU_GROUP), c2),
                  pl.BlockSpec((SGU_GROUPS, SGU_CHUNK, SGU_CHUNK), c3),
                  pl.BlockSpec((SGU_GROUPS, SGU_CHUNK, SGU_GROUP), c3)],
        out_specs=pl.BlockSpec((tm, D_MODEL), row),
        compiler_params=_params("parallel"),
        name="mixer_out",
    )(o_f, o_b, z, ug, vg, gdn_norm_g, ln_g, ln_b, w_s, b_s_full)


def _outproj_kernel(mix_ref, x_ref, mod_ref, ng_ref, wout_ref, rwt_ref, rb_ref,
                    x1_ref, h2_ref, lt_ref):
    mod = mod_ref[0]
    y = _dot(mix_ref[...], wout_ref[...])
    x1 = x_ref[...] + mod[2:3] * _rms(y, ng_ref[1:2])
    x1_ref[...] = x1
    h2 = _rms(x1, ng_ref[2:3]) * (1.0 + mod[4:5]) + mod[3:4]
    h_hi, h_lo = _split2(h2)
    h2_ref[...] = h_hi
    w_hi, w_lo = _split2(rwt_ref[...])
    lt_ref[...] = _dot_nt(w_hi, h_hi) + _dot_nt(w_lo, h_hi) + _dot_nt(w_hi, h_lo) + rb_ref[...]


def _outproj(mix, x2d, mod, rows_per_mod, ng, wout, rwt, rb_col):
    t = x2d.shape[0]
    tm = min(ROW_TILE, t)
    tiles_per_mod = rows_per_mod // tm
    row = lambda i: (i, 0)
    const = lambda i: (0, 0)
    return pl.pallas_call(
        _outproj_kernel,
        out_shape=(jax.ShapeDtypeStruct((t, D_MODEL), F32),
                   jax.ShapeDtypeStruct((t, D_MODEL), BF16),
                   jax.ShapeDtypeStruct((N_EXPERTS, t), F32)),
        grid=(t // tm,),
        in_specs=[pl.BlockSpec((tm, D_MODEL), row),
                  pl.BlockSpec((tm, D_MODEL), row),
                  pl.BlockSpec((1, 6, D_MODEL), lambda i: (i // tiles_per_mod, 0, 0)),
                  pl.BlockSpec((4, D_MODEL), const),
                  pl.BlockSpec((D_MODEL, D_MODEL), const),
                  pl.BlockSpec((N_EXPERTS, D_MODEL), const),
                  pl.BlockSpec((N_EXPERTS, 1), const)],
        out_specs=(pl.BlockSpec((tm, D_MODEL), row),
                   pl.BlockSpec((tm, D_MODEL), row),
                   pl.BlockSpec((N_EXPERTS, tm), lambda i: (0, i))),
        compiler_params=_params("parallel"),
        name="out_proj_router",
    )(mix, x2d, mod, ng, wout, rwt, rb_col)


def _moe_kernel(be_ref, nb_ref, xb_ref, wg_ref, wu_ref, bg_ref, bu_ref, wd_ref, bd_ref, y_ref):
    i = pl.program_id(0)

    @pl.when(i < nb_ref[0])
    def _():
        xb = xb_ref[...]
        gate = jnp.minimum(_dot(xb, wg_ref[0]) + bg_ref[0], SWIGLU_LIMIT)
        up = jnp.clip(_dot(xb, wu_ref[0]) + bu_ref[0], -SWIGLU_LIMIT, SWIGLU_LIMIT)
        act = (up + 1.0) * gate * _sigmoid(SWIGLU_ALPHA * gate)
        y_ref[...] = (_dot(act.astype(BF16), wd_ref[0]) + bd_ref[0]).astype(y_ref.dtype)

    @pl.when(i >= nb_ref[0])
    def _():
        y_ref[...] = jnp.zeros_like(y_ref)


def _moe_ffn(block_e, n_used, xb, wg, wu, bg, bu, wd, bd):
    n_rows = xb.shape[0]
    n_blocks = n_rows // MOE_ROWS
    row = lambda i, be, nb: (i, 0)
    ex3 = lambda i, be, nb: (be[i], 0, 0)
    grid_spec = pltpu.PrefetchScalarGridSpec(
        num_scalar_prefetch=2,
        grid=(n_blocks,),
        in_specs=[pl.BlockSpec((MOE_ROWS, D_MODEL), row),
                  pl.BlockSpec((1, D_MODEL, D_FF), ex3),
                  pl.BlockSpec((1, D_MODEL, D_FF), ex3),
                  pl.BlockSpec((1, 1, D_FF), ex3),
                  pl.BlockSpec((1, 1, D_FF), ex3),
                  pl.BlockSpec((1, D_FF, D_MODEL), ex3),
                  pl.BlockSpec((1, 1, D_MODEL), ex3)],
        out_specs=pl.BlockSpec((MOE_ROWS, D_MODEL), row),
    )
    return pl.pallas_call(
        _moe_kernel,
        out_shape=jax.ShapeDtypeStruct((n_rows, D_MODEL), BF16),
        grid_spec=grid_spec,
        compiler_params=_params("arbitrary"),
        name="moe_ffn",
    )(block_e, n_used, xb, wg, wu, bg, bu, wd, bd)


def _combine_kernel(yg_ref, gt_ref, x1_ref, mod_ref, ng_ref, o_ref):
    mod = mod_ref[0]
    gt = gt_ref[...]
    y = yg_ref[:, :D_MODEL].astype(F32) * gt[:, 0:1]
    for k in range(1, TOP_K):
        y = y + yg_ref[:, k * D_MODEL:(k + 1) * D_MODEL].astype(F32) * gt[:, k:k + 1]
    o_ref[...] = x1_ref[...] + mod[5:6] * _rms(y, ng_ref[3:4])


def _combine(yg, gates, x1, mod, rows_per_mod, ng):
    t = x1.shape[0]
    tm = min(ROW_TILE, t)
    tiles_per_mod = rows_per_mod // tm
    row = lambda i: (i, 0)
    return pl.pallas_call(
        _combine_kernel,
        out_shape=jax.ShapeDtypeStruct((t, D_MODEL), F32),
        grid=(t // tm,),
        in_specs=[pl.BlockSpec((tm, TOP_K * D_MODEL), row),
                  pl.BlockSpec((tm, TOP_K), row),
                  pl.BlockSpec((tm, D_MODEL), row),
                  pl.BlockSpec((1, 6, D_MODEL), lambda i: (i // tiles_per_mod, 0, 0)),
                  pl.BlockSpec((4, D_MODEL), lambda i: (0, 0))],
        out_specs=pl.BlockSpec((tm, D_MODEL), row),
        compiler_params=_params("parallel"),
        name="moe_combine",
    )(yg, gates, x1, mod, ng)


def _route(logits):
    t = logits.shape[0]
    top_v, top_i = lax.top_k(logits, TOP_K)
    gates = jax.nn.softmax(top_v, axis=-1)
    n_assign = t * TOP_K
    flat_e = top_i.reshape(-1)
    order = jnp.argsort(flat_e)
    e_sorted = flat_e[order]
    tok_sorted = (order // TOP_K).astype(jnp.int32)
    counts = jnp.bincount(flat_e, length=N_EXPERTS)
    padded = (counts + MOE_ROWS - 1) // MOE_ROWS * MOE_ROWS
    pad_end = jnp.cumsum(padded)
    dest = ((pad_end - padded)[e_sorted] + jnp.arange(n_assign)
            - (jnp.cumsum(counts) - counts)[e_sorted]).astype(jnp.int32)
    n_blocks = -(-n_assign // MOE_ROWS) + N_EXPERTS
    slot_tok = jnp.zeros((n_blocks * MOE_ROWS,), jnp.int32).at[dest].set(tok_sorted)
    block_e = jnp.minimum(jnp.searchsorted(pad_end, jnp.arange(n_blocks) * MOE_ROWS, side='right'),
                          N_EXPERTS - 1).astype(jnp.int32)
    n_used = (pad_end[-1:] // MOE_ROWS).astype(jnp.int32)
    dest_by_assign = jnp.zeros((n_assign,), jnp.int32).at[order].set(dest).reshape(t, TOP_K)
    return gates, slot_tok, block_e, n_used, dest_by_assign


def kernel(x, c, ctx, c_ctx, w_ada, b_ada, norm_g, w_in, conv_w, a_log, dt_bias, gdn_norm_g,
           sgu_ln_g, sgu_ln_b, sgu_w, sgu_b, w_out, router_w, router_b, w_gu, b_gu, w_down, b_down):
    b, l, d = x.shape
    lc = ctx.shape[1]
    t = b * l
    assert d == D_MODEL and l % ROW_TILE == 0 and l % GDN_BLOCK == 0 and lc % GDN_BLOCK == 0
    assert w_ada.shape[0] == 1, "single-layer block"

    cs = jnp.concatenate([c, c_ctx[None], jnp.zeros((8 - b - 1, d), F32)], axis=0)
    mod_all = _ada(cs, w_ada[0], b_ada[0][None])
    mod = mod_all[:b].reshape(b, 6, d)
    mod_c = mod_all[b:b + 1].reshape(1, 6, d)
    ng = norm_g[0]

    w = w_in[0]
    wqkv = w[:, :QKV_COLS].astype(BF16)
    wba = w[:, QKV_COLS:QKV_COLS + N_GATE_COLS]
    wzuv = w[:, QKV_COLS + N_GATE_COLS:].astype(BF16)
    wbat = wba.T

    x2d = x.reshape(t, d)
    qkv, z, ug, vg, ba, bat = _inproj(x2d, mod, l, ng[0:1], wqkv, wzuv, wba, wbat)
    ctx2d = ctx.reshape(b * lc, d)
    qkv_c, _, _, _, ba_c, bat_c = _inproj(ctx2d, mod_c, b * lc, ng[0:1], wqkv, wzuv, wba, wbat)

    alog = a_log[0].reshape(-1)
    dtb = dt_bias[0].reshape(-1)
    pc = _gdn_prep(qkv_c.reshape(b, lc, QKV_COLS), conv_w[0], ba_c.reshape(b, lc, N_GATE_COLS),
                   bat_c, alog, dtb, lc)
    s_zero = jnp.zeros((b, N_CHAINS, HEAD_DIM, HEAD_DIM), F32)
    _, _, s_ctx = _gdn_scan(*pc, s_zero)
    pp = _gdn_prep(qkv.reshape(b, l, QKV_COLS), conv_w[0], ba.reshape(b, l, N_GATE_COLS),
                   bat, alog, dtb, GRID_W)
    o_f, o_b, _ = _gdn_scan(*pp, s_ctx)

    b_s_full = jnp.broadcast_to(sgu_b[0][:, :, None], (SGU_GROUPS, SGU_CHUNK, SGU_GROUP))
    mix = _mix(o_f.reshape(t, GDN_W), o_b.reshape(t, GDN_W), z, ug, vg, gdn_norm_g,
               sgu_ln_g[0], sgu_ln_b[0], sgu_w[0], b_s_full)

    x1, h2, logits_t = _outproj(mix, x2d, mod, l, ng, w_out[0].astype(BF16),
                                router_w[0].T, router_b[0][:, None])

    gates, slot_tok, block_e, n_used, dest_by_assign = _route(logits_t.T)
    xb = h2[slot_tok]
    wgu = w_gu[0].reshape(N_EXPERTS, d, D_FF, 2)
    bgu = b_gu[0].reshape(N_EXPERTS, 1, D_FF, 2)
    yb = _moe_ffn(block_e, n_used, xb,
                  wgu[..., 0].astype(BF16), wgu[..., 1].astype(BF16), bgu[..., 0], bgu[..., 1],
                  w_down[0].astype(BF16), b_down[0][:, None, :])
    yg = yb[dest_by_assign].reshape(t, TOP_K * d)
    out = _combine(yg, gates, x1, mod, l, ng)
    return out.reshape(b, l, d)
```

```python
import functools
import math

import jax
import jax.numpy as jnp
from jax import lax
from jax.experimental import pallas as pl
from jax.experimental.pallas import tpu as pltpu

F32 = jnp.float32
BF16 = jnp.bfloat16

D_MODEL = 1024
GDN_HEADS = 4
HEAD_DIM = 128
GDN_W = GDN_HEADS * HEAD_DIM
SGU_GROUPS = 4
SGU_GROUP = 128
SGU_W = SGU_GROUPS * SGU_GROUP
SGU_CHUNK = 128
DELTA_CHUNK = 64
GRID_W = 64
N_EXPERTS = 32
TOP_K = 4
D_FF = 1024
SWIGLU_LIMIT = 7.0
SWIGLU_ALPHA = 1.702
NORM_EPS = 1e-6
QKV_COLS = 3 * GDN_W
N_CHAINS = 2 * GDN_HEADS
N_GATE_COLS = 2 * N_CHAINS

ROW_TILE = 512
GDN_BLOCK = 256
CHUNKS_PER_BLOCK = GDN_BLOCK // DELTA_CHUNK
MOE_ROWS = 256
VMEM_LIMIT = 56 * 1024 * 1024


def _params(*sem):
    return pltpu.CompilerParams(dimension_semantics=sem, vmem_limit_bytes=VMEM_LIMIT)


def _dot(a, b):
    return jnp.dot(a, b, preferred_element_type=F32)


def _dot_nt(a, b):
    return lax.dot_general(a, b, (((1,), (1,)), ((), ())), preferred_element_type=F32)


def _dot_tn(a, b):
    return lax.dot_general(a, b, (((0,), (0,)), ((), ())), preferred_element_type=F32)


def _split2(a):
    hi = a.astype(BF16)
    lo = (a - hi.astype(F32)).astype(BF16)
    return hi, lo


def _split3(a):
    hi = a.astype(BF16)
    r = a - hi.astype(F32)
    mid = r.astype(BF16)
    lo = (r - mid.astype(F32)).astype(BF16)
    return hi, mid, lo


def _rms(x32, g):
    return x32 * lax.rsqrt(jnp.mean(x32 * x32, -1, keepdims=True) + NORM_EPS) * g


def _gelu_tanh(x):
    c = math.sqrt(2.0 / math.pi)
    return 0.5 * x * (1.0 + jnp.tanh(c * (x + 0.044715 * (x * x * x))))


def _sigmoid(x):
    return 1.0 / (1.0 + jnp.exp(-x))


def _softplus(x):
    return jnp.maximum(x, 0.0) + jnp.log(1.0 + jnp.exp(-jnp.abs(x)))


def _ada_kernel(c_ref, w_ref, b_ref, o_ref):
    c = c_ref[...]
    s = c * _sigmoid(c)
    s_hi, s_lo = _split2(s)
    w_hi, w_lo = _split2(w_ref[...])
    o_ref[...] = _dot(s_hi, w_hi) + _dot(s_lo, w_hi) + _dot(s_hi, w_lo) + b_ref[...]


def _ada(cs, w_ada, b_ada):
    n = w_ada.shape[1]
    bn = D_MODEL
    return pl.pallas_call(
        _ada_kernel,
        out_shape=jax.ShapeDtypeStruct((cs.shape[0], n), F32),
        grid=(n // bn,),
        in_specs=[pl.BlockSpec(cs.shape, lambda j: (0, 0)),
                  pl.BlockSpec((D_MODEL, bn), lambda j: (0, j)),
                  pl.BlockSpec((1, bn), lambda j: (0, j))],
        out_specs=pl.BlockSpec((cs.shape[0], bn), lambda j: (0, j)),
        compiler_params=_params("parallel"),
        name="ada_mod",
    )(cs, w_ada, b_ada)


def _inproj_kernel(x_ref, mod_ref, g_ref, wqkv_ref, wzuv_ref, wba_ref, wbat_ref,
                   qkv_ref, z_ref, u_ref, v_ref, ba_ref, bat_ref):
    x = x_ref[...]
    mod = mod_ref[0]
    h = _rms(x, g_ref[...]) * (1.0 + mod[1:2]) + mod[0:1]
    h_hi, h_lo = _split2(h)
    qkv_ref[...] = _dot(h_hi, wqkv_ref[...]).astype(BF16)
    zuv = _dot(h_hi, wzuv_ref[...])
    z_ref[...] = zuv[:, :GDN_W].astype(BF16)
    u_ref[...] = _gelu_tanh(zuv[:, GDN_W:GDN_W + SGU_W]).astype(BF16)
    v_ref[...] = _gelu_tanh(zuv[:, GDN_W + SGU_W:]).astype(BF16)
    w_hi, w_lo = _split2(wba_ref[...])
    ba_ref[...] = _dot(h_hi, w_hi) + _dot(h_lo, w_hi) + _dot(h_hi, w_lo)
    wt_hi, wt_lo = _split2(wbat_ref[...])
    bat_ref[...] = _dot_nt(wt_hi, h_hi) + _dot_nt(wt_lo, h_hi) + _dot_nt(wt_hi, h_lo)


def _inproj(x2d, mod, rows_per_mod, ng0, wqkv, wzuv, wba, wbat):
    t = x2d.shape[0]
    tm = min(ROW_TILE, t)
    tiles_per_mod = rows_per_mod // tm
    row = lambda i: (i, 0)
    const = lambda i: (0, 0)
    return pl.pallas_call(
        _inproj_kernel,
        out_shape=(jax.ShapeDtypeStruct((t, QKV_COLS), BF16),
                   jax.ShapeDtypeStruct((t, GDN_W), BF16),
                   jax.ShapeDtypeStruct((t, SGU_W), BF16),
                   jax.ShapeDtypeStruct((t, SGU_W), BF16),
                   jax.ShapeDtypeStruct((t, N_GATE_COLS), F32),
                   jax.ShapeDtypeStruct((N_GATE_COLS, t), F32)),
        grid=(t // tm,),
        in_specs=[pl.BlockSpec((tm, D_MODEL), row),
                  pl.BlockSpec((1, 6, D_MODEL), lambda i: (i // tiles_per_mod, 0, 0)),
                  pl.BlockSpec((1, D_MODEL), const),
                  pl.BlockSpec(wqkv.shape, const),
                  pl.BlockSpec(wzuv.shape, const),
                  pl.BlockSpec(wba.shape, const),
                  pl.BlockSpec(wbat.shape, const)],
        out_specs=(pl.BlockSpec((tm, QKV_COLS), row),
                   pl.BlockSpec((tm, GDN_W), row),
                   pl.BlockSpec((tm, SGU_W), row),
                   pl.BlockSpec((tm, SGU_W), row),
                   pl.BlockSpec((tm, N_GATE_COLS), row),
                   pl.BlockSpec((N_GATE_COLS, tm), lambda i: (0, i))),
        compiler_params=_params("parallel"),
        name="in_proj",
    )(x2d, mod, ng0, wqkv, wzuv, wba, wbat)


def _gdn_prep_kernel(row_len, qkv_ref, cw_ref, ba_ref, bat_ref, alog_r_ref, dtb_r_ref,
                     alog_c_ref, dtb_c_ref, u_ref, w_ref, qd_ref, kd_ref, at_ref, gl_ref):
    n = GDN_BLOCK
    c = DELTA_CHUNK
    ri = lax.broadcasted_iota(jnp.int32, (n, n), 0)
    ci = lax.broadcasted_iota(jnp.int32, (n, n), 1)
    same = (ri // c) == (ci // c)
    lower = same & (ri >= ci)
    upper = same & (ri <= ci)

    def mask01(m):
        return jnp.where(m, 1.0, 0.0).astype(BF16)

    lower_b = mask01(lower)
    upper_b = mask01(upper)
    same_b = mask01(same)

    wi = lax.broadcasted_iota(jnp.int32, (c, n), 0)
    wl = lax.broadcasted_iota(jnp.int32, (c, n), 1)
    wchunk = wl // c
    wj = wl % c
    lower_w = wi >= wj
    upper_w = wi <= wj
    diag_w = wi == wj
    eye_w = jnp.where(diag_w, 1.0, 0.0)
    pair_masks = []
    s = 1
    while s < c:
        pair_masks.append(mask01(((wi // (2 * s)) == (wj // (2 * s))) & ((wi // s) != (wj // s))))
        s *= 2

    def to_wide(full):
        out = full[:c]
        for k in range(1, CHUNKS_PER_BLOCK):
            out = jnp.where(wchunk == k, full[k * c:(k + 1) * c], out)
        return out

    def col_wide(col):
        out = jnp.broadcast_to(col[:c], (c, n))
        for k in range(1, CHUNKS_PER_BLOCK):
            out = jnp.where(wchunk == k, jnp.broadcast_to(col[k * c:(k + 1) * c], (c, n)), out)
        return out

    def block_diag(x_w):
        return jnp.concatenate([x_w] * CHUNKS_PER_BLOCK, axis=0) * same_b

    ba = ba_ref[0]
    bat = bat_ref[...]
    beta_c = _sigmoid(ba[:, :N_CHAINS])
    g_c = -jnp.exp(alog_r_ref[...]) * _softplus(ba[:, N_CHAINS:] + dtb_r_ref[...])
    g_r = -jnp.exp(alog_c_ref[...]) * _softplus(bat[N_CHAINS:] + dtb_c_ref[...])
    gc3 = _split3(g_c)
    gr3 = jnp.concatenate(_split3(g_r), axis=0)

    def sum3_r(m):
        return m[:N_CHAINS] + m[N_CHAINS:2 * N_CHAINS] + m[2 * N_CHAINS:]

    cum_f_c = _dot(lower_b, gc3[0]) + _dot(lower_b, gc3[1]) + _dot(lower_b, gc3[2])
    tot_c = _dot(same_b, gc3[0]) + _dot(same_b, gc3[1]) + _dot(same_b, gc3[2])
    cum_b_c = tot_c - cum_f_c + g_c
    cum_f_r = sum3_r(_dot(gr3, upper_b))
    cum_b_r = sum3_r(_dot(gr3, lower_b))
    ei = lax.broadcasted_iota(jnp.int32, (n, CHUNKS_PER_BLOCK * HEAD_DIM), 0) // c
    ej = lax.broadcasted_iota(jnp.int32, (n, CHUNKS_PER_BLOCK * HEAD_DIM), 1) // HEAD_DIM
    g_last = jnp.exp(sum3_r(_dot(gr3, mask01(ei == ej))))
    gl_ref[0, 0, 0] = g_last[:GDN_HEADS]
    gl_ref[0, 0, 1] = g_last[GDN_HEADS:]

    pos = lax.broadcasted_iota(jnp.int32, (n, HEAD_DIM), 0) % row_len
    first = pos == 0
    last = pos == row_len - 1

    def conv_silu(col):
        x = qkv_ref[0, :, col * HEAD_DIM:(col + 1) * HEAD_DIM].astype(F32)
        cw = cw_ref[:, col * HEAD_DIM:(col + 1) * HEAD_DIM]
        xp = jnp.where(first, 0.0, pltpu.roll(x, 1, 0))
        xn = jnp.where(last, 0.0, pltpu.roll(x, n - 1, 0))
        y = xp * cw[0:1] + x * cw[1:2] + xn * cw[2:3]
        return y * _sigmoid(y)

    def l2n(x):
        return x * lax.rsqrt(jnp.sum(x * x, -1, keepdims=True) + NORM_EPS)

    a_bs, ps, rhss = [None] * N_CHAINS, [None] * N_CHAINS, [None] * N_CHAINS
    for h in range(GDN_HEADS):
        q = l2n(conv_silu(h)) * (HEAD_DIM ** -0.5)
        k = l2n(conv_silu(GDN_HEADS + h))
        v = conv_silu(2 * GDN_HEADS + h)
        k_b = k.astype(BF16)
        qk_kk = _dot_nt(jnp.concatenate([q.astype(BF16), k_b], axis=0), k_b)
        qk_w = to_wide(qk_kk[:n])
        kk_w = to_wide(qk_kk[n:])
        for d in range(2):
            j = d * GDN_HEADS + h
            mask_w = lower_w if d == 0 else upper_w
            cum_c = (cum_f_c if d == 0 else cum_b_c)[:, j:j + 1]
            cum_r = (cum_f_r if d == 0 else cum_b_r)[j:j + 1, :]
            b_c = beta_c[:, j:j + 1]
            decay_w = jnp.where(mask_w, jnp.exp(jnp.where(mask_w, col_wide(cum_c) - cum_r, 0.0)), 0.0)
            amat_w = jnp.where(diag_w, 0.0, kk_w * decay_w * col_wide(b_c))
            a_bs[j] = amat_w.astype(BF16)
            ps[j] = eye_w - amat_w * pair_masks[0].astype(F32)
            e_c = jnp.exp(cum_c)
            rhss[j] = jnp.concatenate([(v * b_c).astype(BF16), (k * (b_c * e_c)).astype(BF16)], axis=1)
            cols = slice(j * HEAD_DIM, (j + 1) * HEAD_DIM)
            qd_ref[0, :, cols] = (q * e_c).astype(BF16)
            kd_ref[0, :, cols] = (k * jnp.exp(tot_c[:, j:j + 1] - cum_c)).astype(BF16)
            at_ref[0, 0, j * c:(j + 1) * c, :] = (qk_w * decay_w).astype(BF16)

    for pm in pair_masks[1:]:
        p_bs = [p.astype(BF16) for p in ps]
        ys = [_dot(a_bs[j] * pm, block_diag(p_bs[j])) for j in range(N_CHAINS)]
        ps = [ps[j] - _dot(p_bs[j], block_diag(ys[j].astype(BF16))) for j in range(N_CHAINS)]

    for j in range(N_CHAINS):
        uw = _dot(block_diag(ps[j].astype(BF16)), rhss[j])
        cols = slice(j * HEAD_DIM, (j + 1) * HEAD_DIM)
        u_ref[0, :, cols] = uw[:, :HEAD_DIM].astype(BF16)
        w_ref[0, :, cols] = uw[:, HEAD_DIM:].astype(BF16)


def _gdn_prep(qkv, conv_w, ba, bat, alog, dtb, row_len):
    b, l, _ = qkv.shape
    nblk = l // GDN_BLOCK
    wide = N_CHAINS * HEAD_DIM
    blk = lambda bi, i: (bi, i, 0)
    const = lambda bi, i: (0, 0)
    alog_r, dtb_r = alog.reshape(1, N_CHAINS), dtb.reshape(1, N_CHAINS)
    alog_c, dtb_c = alog.reshape(N_CHAINS, 1), dtb.reshape(N_CHAINS, 1)
    return pl.pallas_call(
        functools.partial(_gdn_prep_kernel, row_len),
        out_shape=(jax.ShapeDtypeStruct((b, l, wide), BF16),) * 4 + (
            jax.ShapeDtypeStruct((b, nblk, N_CHAINS * DELTA_CHUNK, GDN_BLOCK), BF16),
            jax.ShapeDtypeStruct((b, nblk, 2, GDN_HEADS, CHUNKS_PER_BLOCK * HEAD_DIM), F32)),
        grid=(b, nblk),
        in_specs=[pl.BlockSpec((1, GDN_BLOCK, QKV_COLS), blk),
                  pl.BlockSpec((3, QKV_COLS), const),
                  pl.BlockSpec((1, GDN_BLOCK, N_GATE_COLS), blk),
                  pl.BlockSpec((N_GATE_COLS, GDN_BLOCK), lambda bi, i: (0, bi * nblk + i)),
                  pl.BlockSpec((1, N_CHAINS), const),
                  pl.BlockSpec((1, N_CHAINS), const),
                  pl.BlockSpec((N_CHAINS, 1), const),
                  pl.BlockSpec((N_CHAINS, 1), const)],
        out_specs=(pl.BlockSpec((1, GDN_BLOCK, wide), blk),) * 4 + (
            pl.BlockSpec((1, 1, N_CHAINS * DELTA_CHUNK, GDN_BLOCK), lambda bi, i: (bi, i, 0, 0)),
            pl.BlockSpec((1, 1, 2, GDN_HEADS, CHUNKS_PER_BLOCK * HEAD_DIM),
                         lambda bi, i: (bi, i, 0, 0, 0))),
        compiler_params=_params("parallel", "parallel"),
        name="gdn_prep",
    )(qkv, conv_w, ba, bat, alog_r, dtb_r, alog_c, dtb_c)


def _gdn_scan_kernel(uf, wf, qf, kf, af, gf, ub, wb, qb, kb, ab, gb, s0_ref,
                     of_ref, ob_ref, sfin_ref, s_scr):
    i = pl.program_id(1)
    c = DELTA_CHUNK

    @pl.when(i == 0)
    def _():
        s_scr[...] = s0_ref[0]

    ops = ((uf, wf, qf, kf, af, gf, of_ref), (ub, wb, qb, kb, ab, gb, ob_ref))
    chains = [(d, h) for d in range(2) for h in range(GDN_HEADS)]
    states = [s_scr[j] for j in range(N_CHAINS)]
    for step in range(CHUNKS_PER_BLOCK):
        def chunk(d):
            cc = step if d == 0 else CHUNKS_PER_BLOCK - 1 - step
            return cc, slice(cc * c, (cc + 1) * c)

        xs = []
        for j, (d, h) in enumerate(chains):
            _, rows = chunk(d)
            cols = slice(h * HEAD_DIM, (h + 1) * HEAD_DIM)
            wq = jnp.concatenate([ops[d][1][0, rows, cols], ops[d][2][0, rows, cols]], axis=0)
            xs.append(_dot(wq, states[j].astype(BF16)))
        v_news = []
        for j, (d, h) in enumerate(chains):
            _, rows = chunk(d)
            cols = slice(h * HEAD_DIM, (h + 1) * HEAD_DIM)
            v_news.append((ops[d][0][0, rows, cols].astype(F32) - xs[j][:c]).astype(BF16))
        for j, (d, h) in enumerate(chains):
            cc, rows = chunk(d)
            cols = slice(h * HEAD_DIM, (h + 1) * HEAD_DIM)
            a_c = ops[d][4][0, 0, h * c:(h + 1) * c, cc * c:(cc + 1) * c]
            ops[d][6][0, rows, cols] = xs[j][c:] + _dot(a_c, v_news[j])
            ds = _dot_tn(ops[d][3][0, rows, cols], v_news[j])
            g_last = ops[d][5][0, 0, 0, h:h + 1, cc * HEAD_DIM:(cc + 1) * HEAD_DIM]
            states[j] = states[j] * g_last + ds
    for j in range(N_CHAINS):
        s_scr[j] = states[j]

    @pl.when(i == pl.num_programs(1) - 1)
    def _():
        sfin_ref[0] = s_scr[...]


def _gdn_scan(u, w, qd, kd, at, gl, s0):
    b, l, _ = u.shape
    nblk = l // GDN_BLOCK
    half = GDN_HEADS * HEAD_DIM
    fwd = lambda bi, i: (bi, i, 0)
    bwd = lambda bi, i: (bi, nblk - 1 - i, 1)
    big = lambda m: pl.BlockSpec((1, GDN_BLOCK, half), m)
    att_shape = (1, 1, GDN_HEADS * DELTA_CHUNK, GDN_BLOCK)
    attf = pl.BlockSpec(att_shape, lambda bi, i: (bi, i, 0, 0))
    attb = pl.BlockSpec(att_shape, lambda bi, i: (bi, nblk - 1 - i, 1, 0))
    gl_shape = (1, 1, 1, GDN_HEADS, CHUNKS_PER_BLOCK * HEAD_DIM)
    glf = pl.BlockSpec(gl_shape, lambda bi, i: (bi, i, 0, 0, 0))
    glb = pl.BlockSpec(gl_shape, lambda bi, i: (bi, nblk - 1 - i, 1, 0, 0))
    state = pl.BlockSpec((1, N_CHAINS, HEAD_DIM, HEAD_DIM), lambda bi, i: (bi, 0, 0, 0))
    return pl.pallas_call(
        _gdn_scan_kernel,
        out_shape=(jax.ShapeDtypeStruct((b, l, half), F32),
                   jax.ShapeDtypeStruct((b, l, half), F32),
                   jax.ShapeDtypeStruct((b, N_CHAINS, HEAD_DIM, HEAD_DIM), F32)),
        grid=(b, nblk),
        in_specs=[big(fwd), big(fwd), big(fwd), big(fwd), attf, glf,
                  big(bwd), big(bwd), big(bwd), big(bwd), attb, glb, state],
        out_specs=(pl.BlockSpec((1, GDN_BLOCK, half), fwd),
                   pl.BlockSpec((1, GDN_BLOCK, half), lambda bi, i: (bi, nblk - 1 - i, 0)),
                   state),
        scratch_shapes=[pltpu.VMEM((N_CHAINS, HEAD_DIM, HEAD_DIM), F32)],
        compiler_params=_params("parallel", "arbitrary"),
        name="gdn_scan",
    )(u, w, qd, kd, at, gl, u, w, qd, kd, at, gl, s0)


def _mix_kernel(of_ref, ob_ref, z_ref, u_ref, v_ref, gng_ref, lng_ref, lnb_ref, ws_ref, bs_ref,
                mix_ref):
    tm = of_ref.shape[0]
    o = of_ref[...] + ob_ref[...]
    z = z_ref[...].astype(F32)
    for h in range(GDN_HEADS):
        cols = slice(h * HEAD_DIM, (h + 1) * HEAD_DIM)
        oh = o[:, cols]
        zh = z[:, cols]
        r = lax.rsqrt(jnp.mean(oh * oh, -1, keepdims=True) + NORM_EPS)
        mix_ref[:, cols] = (oh * r * gng_ref[...] * (zh * _sigmoid(zh))).astype(BF16)
    for g in range(SGU_GROUPS):
        cols = slice(g * SGU_GROUP, (g + 1) * SGU_GROUP)
        vg = v_ref[:, cols].astype(F32)
        vc = vg - jnp.mean(vg, -1, keepdims=True)
        vn = vc * lax.rsqrt(jnp.mean(vc * vc, -1, keepdims=True) + NORM_EPS)
        vn = (vn * lng_ref[g:g + 1] + lnb_ref[g:g + 1]).astype(BF16)
        wsg = ws_ref[g].astype(BF16)
        for n in range(tm // SGU_CHUNK):
            rows = slice(n * SGU_CHUNK, (n + 1) * SGU_CHUNK)
            s = _dot(wsg, vn[rows]) + bs_ref[g]
            mix_ref[rows, GDN_W + g * SGU_GROUP:GDN_W + (g + 1) * SGU_GROUP] = (
                u_ref[rows, cols].astype(F32) * s).astype(BF16)


def _mix(o_f, o_b, z, ug, vg, gdn_norm_g, ln_g, ln_b, w_s, b_s_full):
    t = z.shape[0]
    tm = min(ROW_TILE, t)
    row = lambda i: (i, 0)
    c2 = lambda i: (0, 0)
    c3 = lambda i: (0, 0, 0)
    return pl.pallas_call(
        _mix_kernel,
        out_shape=jax.ShapeDtypeStruct((t, D_MODEL), BF16),
        grid=(t // tm,),
        in_specs=[pl.BlockSpec((tm, GDN_W), row), pl.BlockSpec((tm, GDN_W), row),
                  pl.BlockSpec((tm, GDN_W), row), pl.BlockSpec((tm, SGU_W), row),
                  pl.BlockSpec((tm, SGU_W), row),
                  pl.BlockSpec((1, HEAD_DIM), c2),
                  pl.BlockSpec((SGU_GROUPS, SGU_GROUP), c2),
                  pl.BlockSpec((SGU_GROUPS, SGU_GROUP), c2),
                  pl.BlockSpec((SGU_GROUPS, SGU_CHUNK, SGU_CHUNK), c3),
                  pl.BlockSpec((SGU_GROUPS, SGU_CHUNK, SGU_GROUP), c3)],
        out_specs=pl.BlockSpec((tm, D_MODEL), row),
        compiler_params=_params("parallel"),
        name="mixer_out",
    )(o_f, o_b, z, ug, vg, gdn_norm_g, ln_g, ln_b, w_s, b_s_full)


def _outproj_kernel(mix_ref, x_ref, mod_ref, ng_ref, wout_ref, rwt_ref, rb_ref,
                    x1_ref, h2_ref, lt_ref):
    mod = mod_ref[0]
    y = _dot(mix_ref[...], wout_ref[...])
    x1 = x_ref[...] + mod[2:3] * _rms(y, ng_ref[1:2])
    x1_ref[...] = x1
    h2 = _rms(x1, ng_ref[2:3]) * (1.0 + mod[4:5]) + mod[3:4]
    h_hi, h_lo = _split2(h2)
    h2_ref[...] = h_hi
    w_hi, w_lo = _split2(rwt_ref[...])
    lt_ref[...] = _dot_nt(w_hi, h_hi) + _dot_nt(w_lo, h_hi) + _dot_nt(w_hi, h_lo) + rb_ref[...]


def _outproj(mix, x2d, mod, rows_per_mod, ng, wout, rwt, rb_col):
    t = x2d.shape[0]
    tm = min(ROW_TILE, t)
    tiles_per_mod = rows_per_mod // tm
    row = lambda i: (i, 0)
    const = lambda i: (0, 0)
    return pl.pallas_call(
        _outproj_kernel,
        out_shape=(jax.ShapeDtypeStruct((t, D_MODEL), F32),
                   jax.ShapeDtypeStruct((t, D_MODEL), BF16),
                   jax.ShapeDtypeStruct((N_EXPERTS, t), F32)),
        grid=(t // tm,),
        in_specs=[pl.BlockSpec((tm, D_MODEL), row),
                  pl.BlockSpec((tm, D_MODEL), row),
                  pl.BlockSpec((1, 6, D_MODEL), lambda i: (i // tiles_per_mod, 0, 0)),
                  pl.BlockSpec((4, D_MODEL), const),
                  pl.BlockSpec((D_MODEL, D_MODEL), const),
                  pl.BlockSpec((N_EXPERTS, D_MODEL), const),
                  pl.BlockSpec((N_EXPERTS, 1), const)],
        out_specs=(pl.BlockSpec((tm, D_MODEL), row),
                   pl.BlockSpec((tm, D_MODEL), row),
                   pl.BlockSpec((N_EXPERTS, tm), lambda i: (0, i))),
        compiler_params=_params("parallel"),
        name="out_proj_router",
    )(mix, x2d, mod, ng, wout, rwt, rb_col)


def _moe_kernel(be_ref, nb_ref, xb_ref, wg_ref, wu_ref, bg_ref, bu_ref, wd_ref, bd_ref, y_ref):
    i = pl.program_id(0)

    @pl.when(i < nb_ref[0])
    def _():
        xb = xb_ref[...]
        gate = jnp.minimum(_dot(xb, wg_ref[0]) + bg_ref[0], SWIGLU_LIMIT)
        up = jnp.clip(_dot(xb, wu_ref[0]) + bu_ref[0], -SWIGLU_LIMIT, SWIGLU_LIMIT)
        act = (up + 1.0) * gate * _sigmoid(SWIGLU_ALPHA * gate)
        y_ref[...] = (_dot(act.astype(BF16), wd_ref[0]) + bd_ref[0]).astype(y_ref.dtype)

    @pl.when(i >= nb_ref[0])
    def _():
        y_ref[...] = jnp.zeros_like(y_ref)


def _moe_ffn(block_e, n_used, xb, wg, wu, bg, bu, wd, bd):
    n_rows = xb.shape[0]
    n_blocks = n_rows // MOE_ROWS
    row = lambda i, be, nb: (i, 0)
    ex3 = lambda i, be, nb: (be[i], 0, 0)
    grid_spec = pltpu.PrefetchScalarGridSpec(
        num_scalar_prefetch=2,
        grid=(n_blocks,),
        in_specs=[pl.BlockSpec((MOE_ROWS, D_MODEL), row),
                  pl.BlockSpec((1, D_MODEL, D_FF), ex3),
                  pl.BlockSpec((1, D_MODEL, D_FF), ex3),
                  pl.BlockSpec((1, 1, D_FF), ex3),
                  pl.BlockSpec((1, 1, D_FF), ex3),
                  pl.BlockSpec((1, D_FF, D_MODEL), ex3),
                  pl.BlockSpec((1, 1, D_MODEL), ex3)],
        out_specs=pl.BlockSpec((MOE_ROWS, D_MODEL), row),
    )
    return pl.pallas_call(
        _moe_kernel,
        out_shape=jax.ShapeDtypeStruct((n_rows, D_MODEL), BF16),
        grid_spec=grid_spec,
        compiler_params=_params("arbitrary"),
        name="moe_ffn",
    )(block_e, n_used, xb, wg, wu, bg, bu, wd, bd)


def _combine_kernel(yg_ref, gt_ref, x1_ref, mod_ref, ng_ref, o_ref):
    mod = mod_ref[0]
    gt = gt_ref[...]
    y = yg_ref[:, :D_MODEL].astype(F32) * gt[:, 0:1]
    for k in range(1, TOP_K):
        y = y + yg_ref[:, k * D_MODEL:(k + 1) * D_MODEL].astype(F32) * gt[:, k:k + 1]
    o_ref[...] = x1_ref[...] + mod[5:6] * _rms(y, ng_ref[3:4])


def _combine(yg, gates, x1, mod, rows_per_mod, ng):
    t = x1.shape[0]
    tm = min(ROW_TILE, t)
    tiles_per_mod = rows_per_mod // tm
    row = lambda i: (i, 0)
    return pl.pallas_call(
        _combine_kernel,
        out_shape=jax.ShapeDtypeStruct((t, D_MODEL), F32),
        grid=(t // tm,),
        in_specs=[pl.BlockSpec((tm, TOP_K * D_MODEL), row),
                  pl.BlockSpec((tm, TOP_K), row),
                  pl.BlockSpec((tm, D_MODEL), row),
                  pl.BlockSpec((1, 6, D_MODEL), lambda i: (i // tiles_per_mod, 0, 0)),
                  pl.BlockSpec((4, D_MODEL), lambda i: (0, 0))],
        out_specs=pl.BlockSpec((tm, D_MODEL), row),
        compiler_params=_params("parallel"),
        name="moe_combine",
    )(yg, gates, x1, mod, ng)


def _route(logits):
    t = logits.shape[0]
    top_v, top_i = lax.top_k(logits, TOP_K)
    gates = jax.nn.softmax(top_v, axis=-1)
    n_assign = t * TOP_K
    flat_e = top_i.reshape(-1)
    order = jnp.argsort(flat_e)
    e_sorted = flat_e[order]
    tok_sorted = (order // TOP_K).astype(jnp.int32)
    counts = jnp.bincount(flat_e, length=N_EXPERTS)
    padded = (counts + MOE_ROWS - 1) // MOE_ROWS * MOE_ROWS
    pad_end = jnp.cumsum(padded)
    dest = ((pad_end - padded)[e_sorted] + jnp.arange(n_assign)
            - (jnp.cumsum(counts) - counts)[e_sorted]).astype(jnp.int32)
    n_blocks = -(-n_assign // MOE_ROWS) + N_EXPERTS
    slot_tok = jnp.zeros((n_blocks * MOE_ROWS,), jnp.int32).at[dest].set(tok_sorted)
    block_e = jnp.minimum(jnp.searchsorted(pad_end, jnp.arange(n_blocks) * MOE_ROWS, side='right'),
                          N_EXPERTS - 1).astype(jnp.int32)
    n_used = (pad_end[-1:] // MOE_ROWS).astype(jnp.int32)
    dest_by_assign = jnp.zeros((n_assign,), jnp.int32).at[order].set(dest).reshape(t, TOP_K)
    return gates, slot_tok, block_e, n_used, dest_by_assign


def kernel(x, c, ctx, c_ctx, w_ada, b_ada, norm_g, w_in, conv_w, a_log, dt_bias, gdn_norm_g,
           sgu_ln_g, sgu_ln_b, sgu_w, sgu_b, w_out, router_w, router_b, w_gu, b_gu, w_down, b_down):
    b, l, d = x.shape
    lc = ctx.shape[1]
    t = b * l
    assert d == D_MODEL and l % ROW_TILE == 0 and l % GDN_BLOCK == 0 and lc % GDN_BLOCK == 0
    assert w_ada.shape[0] == 1, "single-layer block"

    cs = jnp.concatenate([c, c_ctx[None], jnp.zeros((8 - b - 1, d), F32)], axis=0)
    mod_all = _ada(cs, w_ada[0], b_ada[0][None])
    mod = mod_all[:b].reshape(b, 6, d)
    mod_c = mod_all[b:b + 1].reshape(1, 6, d)
    ng = norm_g[0]

    w = w_in[0]
    wqkv = w[:, :QKV_COLS].astype(BF16)
    wba = w[:, QKV_COLS:QKV_COLS + N_GATE_COLS]
    wzuv = w[:, QKV_COLS + N_GATE_COLS:].astype(BF16)
    wbat = wba.T

    x2d = x.reshape(t, d)
    qkv, z, ug, vg, ba, bat = _inproj(x2d, mod, l, ng[0:1], wqkv, wzuv, wba, wbat)
    ctx2d = ctx.reshape(b * lc, d)
    qkv_c, _, _, _, ba_c, bat_c = _inproj(ctx2d, mod_c, b * lc, ng[0:1], wqkv, wzuv, wba, wbat)

    alog = a_log[0].reshape(-1)
    dtb = dt_bias[0].reshape(-1)
    pc = _gdn_prep(qkv_c.reshape(b, lc, QKV_COLS), conv_w[0], ba_c.reshape(b, lc, N_GATE_COLS),
                   bat_c, alog, dtb, lc)
    s_zero = jnp.zeros((b, N_CHAINS, HEAD_DIM, HEAD_DIM), F32)
    _, _, s_ctx = _gdn_scan(*pc, s_zero)
    pp = _gdn_prep(qkv.reshape(b, l, QKV_COLS), conv_w[0], ba.reshape(b, l, N_GATE_COLS),
                   bat, alog, dtb, GRID_W)
    o_f, o_b, _ = _gdn_scan(*pp, s_ctx)

    b_s_full = jnp.broadcast_to(sgu_b[0][:, :, None], (SGU_GROUPS, SGU_CHUNK, SGU_GROUP))
    mix = _mix(o_f.reshape(t, GDN_W), o_b.reshape(t, GDN_W), z, ug, vg, gdn_norm_g,
               sgu_ln_g[0], sgu_ln_b[0], sgu_w[0], b_s_full)

    x1, h2, logits_t = _outproj(mix, x2d, mod, l, ng, w_out[0].astype(BF16),
                                router_w[0].T, router_b[0][:, None])

    gates, slot_tok, block_e, n_used, dest_by_assign = _route(logits_t.T)
    xb = h2[slot_tok]
    wgu = w_gu[0].reshape(N_EXPERTS, d, D_FF, 2)
    bgu = b_gu[0].reshape(N_EXPERTS, 1, D_FF, 2)
    yb = _moe_ffn(block_e, n_used, xb,
                  wgu[..., 0].astype(BF16), wgu[..., 1].astype(BF16), bgu[..., 0], bgu[..., 1],
                  w_down[0].astype(BF16), b_down[0][:, None, :])
    yg = yb[dest_by_assign].reshape(t, TOP_K * d)
    out = _combine(yg, gates, x1, mod, l, ng)
    return out.reshape(b, l, d)
```

```python
import functools
import math

import jax
import jax.numpy as jnp
from jax import lax
from jax.experimental import pallas as pl
from jax.experimental.pallas import tpu as pltpu

F32 = jnp.float32
BF16 = jnp.bfloat16

D_MODEL = 1024
GDN_HEADS = 4
HEAD_DIM = 128
GDN_W = GDN_HEADS * HEAD_DIM
SGU_GROUPS = 4
SGU_GROUP = 128
SGU_W = SGU_GROUPS * SGU_GROUP
SGU_CHUNK = 128
DELTA_CHUNK = 64
GRID_W = 64
N_EXPERTS = 32
TOP_K = 4
D_FF = 1024
SWIGLU_LIMIT = 7.0
SWIGLU_ALPHA = 1.702
NORM_EPS = 1e-6
QKV_COLS = 3 * GDN_W
N_CHAINS = 2 * GDN_HEADS
N_GATE_COLS = 2 * N_CHAINS

ROW_TILE = 512
GDN_BLOCK = 256
CHUNKS_PER_BLOCK = GDN_BLOCK // DELTA_CHUNK
MOE_ROWS = 256
U32 = jnp.uint32
LANES = 128
PACKED_W = D_MODEL // 2
ROUTE_TILE = 512
COMBINE_TILE = 256
WAIT_UNROLL = 8
VMEM_LIMIT = 56 * 1024 * 1024


def _params(*sem):
    return pltpu.CompilerParams(dimension_semantics=sem, vmem_limit_bytes=VMEM_LIMIT)


def _dot(a, b):
    return jnp.dot(a, b, preferred_element_type=F32)


def _dot_nt(a, b):
    return lax.dot_general(a, b, (((1,), (1,)), ((), ())), preferred_element_type=F32)


def _dot_tn(a, b):
    return lax.dot_general(a, b, (((0,), (0,)), ((), ())), preferred_element_type=F32)


def _split2(a):
    hi = a.astype(BF16)
    lo = (a - hi.astype(F32)).astype(BF16)
    return hi, lo


def _split3(a):
    hi = a.astype(BF16)
    r = a - hi.astype(F32)
    mid = r.astype(BF16)
    lo = (r - mid.astype(F32)).astype(BF16)
    return hi, mid, lo


def _pack_rows(x32):
    xb = x32.astype(BF16).astype(F32)
    hi = lax.bitcast_convert_type(xb[:, :PACKED_W], U32)
    lo = lax.bitcast_convert_type(xb[:, PACKED_W:], U32)
    return hi | (lo >> 16)


def _unpack_rows(w):
    hi = lax.bitcast_convert_type(w & jnp.uint32(0xFFFF0000), F32)
    lo = lax.bitcast_convert_type(w << 16, F32)
    return jnp.concatenate([hi, lo], axis=1)


def _rms(x32, g):
    return x32 * lax.rsqrt(jnp.mean(x32 * x32, -1, keepdims=True) + NORM_EPS) * g


def _gelu_tanh(x):
    c = math.sqrt(2.0 / math.pi)
    return 0.5 * x * (1.0 + jnp.tanh(c * (x + 0.044715 * (x * x * x))))


def _sigmoid(x):
    return 1.0 / (1.0 + jnp.exp(-x))


def _softplus(x):
    return jnp.maximum(x, 0.0) + jnp.log(1.0 + jnp.exp(-jnp.abs(x)))


def _ada_kernel(c_ref, w_ref, b_ref, o_ref):
    c = c_ref[...]
    s = c * _sigmoid(c)
    s_hi, s_lo = _split2(s)
    w_hi, w_lo = _split2(w_ref[...])
    o_ref[...] = _dot(s_hi, w_hi) + _dot(s_lo, w_hi) + _dot(s_hi, w_lo) + b_ref[...]


def _ada(cs, w_ada, b_ada):
    n = w_ada.shape[1]
    bn = D_MODEL
    return pl.pallas_call(
        _ada_kernel,
        out_shape=jax.ShapeDtypeStruct((cs.shape[0], n), F32),
        grid=(n // bn,),
        in_specs=[pl.BlockSpec(cs.shape, lambda j: (0, 0)),
                  pl.BlockSpec((D_MODEL, bn), lambda j: (0, j)),
                  pl.BlockSpec((1, bn), lambda j: (0, j))],
        out_specs=pl.BlockSpec((cs.shape[0], bn), lambda j: (0, j)),
        compiler_params=_params("parallel"),
        name="ada_mod",
    )(cs, w_ada, b_ada)


def _inproj_kernel(x_ref, mod_ref, g_ref, wqkv_ref, wzuv_ref, wba_ref, wbat_ref,
                   qkv_ref, z_ref, u_ref, v_ref, ba_ref, bat_ref):
    x = x_ref[...]
    mod = mod_ref[0]
    h = _rms(x, g_ref[...]) * (1.0 + mod[1:2]) + mod[0:1]
    h_hi, h_lo = _split2(h)
    qkv_ref[...] = _dot(h_hi, wqkv_ref[...]).astype(BF16)
    zuv = _dot(h_hi, wzuv_ref[...])
    z_ref[...] = zuv[:, :GDN_W].astype(BF16)
    u_ref[...] = _gelu_tanh(zuv[:, GDN_W:GDN_W + SGU_W]).astype(BF16)
    v_ref[...] = _gelu_tanh(zuv[:, GDN_W + SGU_W:]).astype(BF16)
    w_hi, w_lo = _split2(wba_ref[...])
    ba_ref[...] = _dot(h_hi, w_hi) + _dot(h_lo, w_hi) + _dot(h_hi, w_lo)
    wt_hi, wt_lo = _split2(wbat_ref[...])
    bat_ref[...] = _dot_nt(wt_hi, h_hi) + _dot_nt(wt_lo, h_hi) + _dot_nt(wt_hi, h_lo)


def _inproj(x2d, mod, rows_per_mod, ng0, wqkv, wzuv, wba, wbat):
    t = x2d.shape[0]
    tm = min(ROW_TILE, t)
    tiles_per_mod = rows_per_mod // tm
    row = lambda i: (i, 0)
    const = lambda i: (0, 0)
    return pl.pallas_call(
        _inproj_kernel,
        out_shape=(jax.ShapeDtypeStruct((t, QKV_COLS), BF16),
                   jax.ShapeDtypeStruct((t, GDN_W), BF16),
                   jax.ShapeDtypeStruct((t, SGU_W), BF16),
                   jax.ShapeDtypeStruct((t, SGU_W), BF16),
                   jax.ShapeDtypeStruct((t, N_GATE_COLS), F32),
                   jax.ShapeDtypeStruct((N_GATE_COLS, t), F32)),
        grid=(t // tm,),
        in_specs=[pl.BlockSpec((tm, D_MODEL), row),
                  pl.BlockSpec((1, 6, D_MODEL), lambda i: (i // tiles_per_mod, 0, 0)),
                  pl.BlockSpec((1, D_MODEL), const),
                  pl.BlockSpec(wqkv.shape, const),
                  pl.BlockSpec(wzuv.shape, const),
                  pl.BlockSpec(wba.shape, const),
                  pl.BlockSpec(wbat.shape, const)],
        out_specs=(pl.BlockSpec((tm, QKV_COLS), row),
                   pl.BlockSpec((tm, GDN_W), row),
                   pl.BlockSpec((tm, SGU_W), row),
                   pl.BlockSpec((tm, SGU_W), row),
                   pl.BlockSpec((tm, N_GATE_COLS), row),
                   pl.BlockSpec((N_GATE_COLS, tm), lambda i: (0, i))),
        compiler_params=_params("parallel"),
        name="in_proj",
    )(x2d, mod, ng0, wqkv, wzuv, wba, wbat)


def _gdn_prep_kernel(row_len, qkv_ref, cw_ref, ba_ref, bat_ref, alog_r_ref, dtb_r_ref,
                     alog_c_ref, dtb_c_ref, u_ref, w_ref, qd_ref, kd_ref, at_ref, gl_ref):
    n = GDN_BLOCK
    c = DELTA_CHUNK
    ri = lax.broadcasted_iota(jnp.int32, (n, n), 0)
    ci = lax.broadcasted_iota(jnp.int32, (n, n), 1)
    same = (ri // c) == (ci // c)
    lower = same & (ri >= ci)
    upper = same & (ri <= ci)

    def mask01(m):
        return jnp.where(m, 1.0, 0.0).astype(BF16)

    lower_b = mask01(lower)
    upper_b = mask01(upper)
    same_b = mask01(same)

    wi = lax.broadcasted_iota(jnp.int32, (c, n), 0)
    wl = lax.broadcasted_iota(jnp.int32, (c, n), 1)
    wchunk = wl // c
    wj = wl % c
    lower_w = wi >= wj
    upper_w = wi <= wj
    diag_w = wi == wj
    eye_w = jnp.where(diag_w, 1.0, 0.0)
    pair_masks = []
    s = 1
    while s < c:
        pair_masks.append(mask01(((wi // (2 * s)) == (wj // (2 * s))) & ((wi // s) != (wj // s))))
        s *= 2

    def to_wide(full):
        out = full[:c]
        for k in range(1, CHUNKS_PER_BLOCK):
            out = jnp.where(wchunk == k, full[k * c:(k + 1) * c], out)
        return out

    def col_wide(col):
        out = jnp.broadcast_to(col[:c], (c, n))
        for k in range(1, CHUNKS_PER_BLOCK):
            out = jnp.where(wchunk == k, jnp.broadcast_to(col[k * c:(k + 1) * c], (c, n)), out)
        return out

    def block_diag(x_w):
        return jnp.concatenate([x_w] * CHUNKS_PER_BLOCK, axis=0) * same_b

    ba = ba_ref[0]
    bat = bat_ref[...]
    beta_c = _sigmoid(ba[:, :N_CHAINS])
    g_c = -jnp.exp(alog_r_ref[...]) * _softplus(ba[:, N_CHAINS:] + dtb_r_ref[...])
    g_r = -jnp.exp(alog_c_ref[...]) * _softplus(bat[N_CHAINS:] + dtb_c_ref[...])
    gc3 = _split3(g_c)
    gr3 = jnp.concatenate(_split3(g_r), axis=0)

    def sum3_r(m):
        return m[:N_CHAINS] + m[N_CHAINS:2 * N_CHAINS] + m[2 * N_CHAINS:]

    cum_f_c = _dot(lower_b, gc3[0]) + _dot(lower_b, gc3[1]) + _dot(lower_b, gc3[2])
    tot_c = _dot(same_b, gc3[0]) + _dot(same_b, gc3[1]) + _dot(same_b, gc3[2])
    cum_b_c = tot_c - cum_f_c + g_c
    cum_f_r = sum3_r(_dot(gr3, upper_b))
    cum_b_r = sum3_r(_dot(gr3, lower_b))
    ei = lax.broadcasted_iota(jnp.int32, (n, CHUNKS_PER_BLOCK * HEAD_DIM), 0) // c
    ej = lax.broadcasted_iota(jnp.int32, (n, CHUNKS_PER_BLOCK * HEAD_DIM), 1) // HEAD_DIM
    g_last = jnp.exp(sum3_r(_dot(gr3, mask01(ei == ej))))
    gl_ref[0, 0, 0] = g_last[:GDN_HEADS]
    gl_ref[0, 0, 1] = g_last[GDN_HEADS:]

    pos = lax.broadcasted_iota(jnp.int32, (n, HEAD_DIM), 0) % row_len
    first = pos == 0
    last = pos == row_len - 1

    def conv_silu(col):
        x = qkv_ref[0, :, col * HEAD_DIM:(col + 1) * HEAD_DIM].astype(F32)
        cw = cw_ref[:, col * HEAD_DIM:(col + 1) * HEAD_DIM]
        xp = jnp.where(first, 0.0, pltpu.roll(x, 1, 0))
        xn = jnp.where(last, 0.0, pltpu.roll(x, n - 1, 0))
        y = xp * cw[0:1] + x * cw[1:2] + xn * cw[2:3]
        return y * _sigmoid(y)

    def l2n(x):
        return x * lax.rsqrt(jnp.sum(x * x, -1, keepdims=True) + NORM_EPS)

    a_bs, ps, rhss = [None] * N_CHAINS, [None] * N_CHAINS, [None] * N_CHAINS
    for h in range(GDN_HEADS):
        q = l2n(conv_silu(h)) * (HEAD_DIM ** -0.5)
        k = l2n(conv_silu(GDN_HEADS + h))
        v = conv_silu(2 * GDN_HEADS + h)
        k_b = k.astype(BF16)
        qk_kk = _dot_nt(jnp.concatenate([q.astype(BF16), k_b], axis=0), k_b)
        qk_w = to_wide(qk_kk[:n])
        kk_w = to_wide(qk_kk[n:])
        for d in range(2):
            j = d * GDN_HEADS + h
            mask_w = lower_w if d == 0 else upper_w
            cum_c = (cum_f_c if d == 0 else cum_b_c)[:, j:j + 1]
            cum_r = (cum_f_r if d == 0 else cum_b_r)[j:j + 1, :]
            b_c = beta_c[:, j:j + 1]
            decay_w = jnp.where(mask_w, jnp.exp(jnp.where(mask_w, col_wide(cum_c) - cum_r, 0.0)), 0.0)
            amat_w = jnp.where(diag_w, 0.0, kk_w * decay_w * col_wide(b_c))
            a_bs[j] = amat_w.astype(BF16)
            ps[j] = eye_w - amat_w * pair_masks[0].astype(F32)
            e_c = jnp.exp(cum_c)
            rhss[j] = jnp.concatenate([(v * b_c).astype(BF16), (k * (b_c * e_c)).astype(BF16)], axis=1)
            cols = slice(j * HEAD_DIM, (j + 1) * HEAD_DIM)
            qd_ref[0, :, cols] = (q * e_c).astype(BF16)
            kd_ref[0, :, cols] = (k * jnp.exp(tot_c[:, j:j + 1] - cum_c)).astype(BF16)
            at_ref[0, 0, j * c:(j + 1) * c, :] = (qk_w * decay_w).astype(BF16)

    for pm in pair_masks[1:]:
        p_bs = [p.astype(BF16) for p in ps]
        ys = [_dot(a_bs[j] * pm, block_diag(p_bs[j])) for j in range(N_CHAINS)]
        ps = [ps[j] - _dot(p_bs[j], block_diag(ys[j].astype(BF16))) for j in range(N_CHAINS)]

    for j in range(N_CHAINS):
        uw = _dot(block_diag(ps[j].astype(BF16)), rhss[j])
        cols = slice(j * HEAD_DIM, (j + 1) * HEAD_DIM)
        u_ref[0, :, cols] = uw[:, :HEAD_DIM].astype(BF16)
        w_ref[0, :, cols] = uw[:, HEAD_DIM:].astype(BF16)


def _gdn_prep(qkv, conv_w, ba, bat, alog, dtb, row_len):
    b, l, _ = qkv.shape
    nblk = l // GDN_BLOCK
    wide = N_CHAINS * HEAD_DIM
    blk = lambda bi, i: (bi, i, 0)
    const = lambda bi, i: (0, 0)
    alog_r, dtb_r = alog.reshape(1, N_CHAINS), dtb.reshape(1, N_CHAINS)
    alog_c, dtb_c = alog.reshape(N_CHAINS, 1), dtb.reshape(N_CHAINS, 1)
    return pl.pallas_call(
        functools.partial(_gdn_prep_kernel, row_len),
        out_shape=(jax.ShapeDtypeStruct((b, l, wide), BF16),) * 4 + (
            jax.ShapeDtypeStruct((b, nblk, N_CHAINS * DELTA_CHUNK, GDN_BLOCK), BF16),
            jax.ShapeDtypeStruct((b, nblk, 2, GDN_HEADS, CHUNKS_PER_BLOCK * HEAD_DIM), F32)),
        grid=(b, nblk),
        in_specs=[pl.BlockSpec((1, GDN_BLOCK, QKV_COLS), blk),
                  pl.BlockSpec((3, QKV_COLS), const),
                  pl.BlockSpec((1, GDN_BLOCK, N_GATE_COLS), blk),
                  pl.BlockSpec((N_GATE_COLS, GDN_BLOCK), lambda bi, i: (0, bi * nblk + i)),
                  pl.BlockSpec((1, N_CHAINS), const),
                  pl.BlockSpec((1, N_CHAINS), const),
                  pl.BlockSpec((N_CHAINS, 1), const),
                  pl.BlockSpec((N_CHAINS, 1), const)],
        out_specs=(pl.BlockSpec((1, GDN_BLOCK, wide), blk),) * 4 + (
            pl.BlockSpec((1, 1, N_CHAINS * DELTA_CHUNK, GDN_BLOCK), lambda bi, i: (bi, i, 0, 0)),
            pl.BlockSpec((1, 1, 2, GDN_HEADS, CHUNKS_PER_BLOCK * HEAD_DIM),
                         lambda bi, i: (bi, i, 0, 0, 0))),
        compiler_params=_params("parallel", "parallel"),
        name="gdn_prep",
    )(qkv, conv_w, ba, bat, alog_r, dtb_r, alog_c, dtb_c)


def _gdn_scan_kernel(uf, wf, qf, kf, af, gf, ub, wb, qb, kb, ab, gb, s0_ref,
                     of_ref, ob_ref, sfin_ref, s_scr):
    i = pl.program_id(1)
    c = DELTA_CHUNK

    @pl.when(i == 0)
    def _():
        s_scr[...] = s0_ref[0]

    ops = ((uf, wf, qf, kf, af, gf, of_ref), (ub, wb, qb, kb, ab, gb, ob_ref))
    chains = [(d, h) for d in range(2) for h in range(GDN_HEADS)]
    states = [s_scr[j] for j in range(N_CHAINS)]
    for step in range(CHUNKS_PER_BLOCK):
        def chunk(d):
            cc = step if d == 0 else CHUNKS_PER_BLOCK - 1 - step
            return cc, slice(cc * c, (cc + 1) * c)

        xs = []
        for j, (d, h) in enumerate(chains):
            _, rows = chunk(d)
            cols = slice(h * HEAD_DIM, (h + 1) * HEAD_DIM)
            wq = jnp.concatenate([ops[d][1][0, rows, cols], ops[d][2][0, rows, cols]], axis=0)
            xs.append(_dot(wq, states[j].astype(BF16)))
        v_news = []
        for j, (d, h) in enumerate(chains):
            _, rows = chunk(d)
            cols = slice(h * HEAD_DIM, (h + 1) * HEAD_DIM)
            v_news.append((ops[d][0][0, rows, cols].astype(F32) - xs[j][:c]).astype(BF16))
        for j, (d, h) in enumerate(chains):
            cc, rows = chunk(d)
            cols = slice(h * HEAD_DIM, (h + 1) * HEAD_DIM)
            a_c = ops[d][4][0, 0, h * c:(h + 1) * c, cc * c:(cc + 1) * c]
            ops[d][6][0, rows, cols] = xs[j][c:] + _dot(a_c, v_news[j])
            ds = _dot_tn(ops[d][3][0, rows, cols], v_news[j])
            g_last = ops[d][5][0, 0, 0, h:h + 1, cc * HEAD_DIM:(cc + 1) * HEAD_DIM]
            states[j] = states[j] * g_last + ds
    for j in range(N_CHAINS):
        s_scr[j] = states[j]

    @pl.when(i == pl.num_programs(1) - 1)
    def _():
        sfin_ref[0] = s_scr[...]


def _gdn_scan(u, w, qd, kd, at, gl, s0):
    b, l, _ = u.shape
    nblk = l // GDN_BLOCK
    half = GDN_HEADS * HEAD_DIM
    fwd = lambda bi, i: (bi, i, 0)
    bwd = lambda bi, i: (bi, nblk - 1 - i, 1)
    big = lambda m: pl.BlockSpec((1, GDN_BLOCK, half), m)
    att_shape = (1, 1, GDN_HEADS * DELTA_CHUNK, GDN_BLOCK)
    attf = pl.BlockSpec(att_shape, lambda bi, i: (bi, i, 0, 0))
    attb = pl.BlockSpec(att_shape, lambda bi, i: (bi, nblk - 1 - i, 1, 0))
    gl_shape = (1, 1, 1, GDN_HEADS, CHUNKS_PER_BLOCK * HEAD_DIM)
    glf = pl.BlockSpec(gl_shape, lambda bi, i: (bi, i, 0, 0, 0))
    glb = pl.BlockSpec(gl_shape, lambda bi, i: (bi, nblk - 1 - i, 1, 0, 0))
    state = pl.BlockSpec((1, N_CHAINS, HEAD_DIM, HEAD_DIM), lambda bi, i: (bi, 0, 0, 0))
    return pl.pallas_call(
        _gdn_scan_kernel,
        out_shape=(jax.ShapeDtypeStruct((b, l, half), F32),
                   jax.ShapeDtypeStruct((b, l, half), F32),
                   jax.ShapeDtypeStruct((b, N_CHAINS, HEAD_DIM, HEAD_DIM), F32)),
        grid=(b, nblk),
        in_specs=[big(fwd), big(fwd), big(fwd), big(fwd), attf, glf,
                  big(bwd), big(bwd), big(bwd), big(bwd), attb, glb, state],
        out_specs=(pl.BlockSpec((1, GDN_BLOCK, half), fwd),
                   pl.BlockSpec((1, GDN_BLOCK, half), lambda bi, i: (bi, nblk - 1 - i, 0)),
                   state),
        scratch_shapes=[pltpu.VMEM((N_CHAINS, HEAD_DIM, HEAD_DIM), F32)],
        compiler_params=_params("parallel", "arbitrary"),
        name="gdn_scan",
    )(u, w, qd, kd, at, gl, u, w, qd, kd, at, gl, s0)


def _mix_kernel(of_ref, ob_ref, z_ref, u_ref, v_ref, gng_ref, lng_ref, lnb_ref, ws_ref, bs_ref,
                mix_ref):
    tm = of_ref.shape[0]
    o = of_ref[...] + ob_ref[...]
    z = z_ref[...].astype(F32)
    for h in range(GDN_HEADS):
        cols = slice(h * HEAD_DIM, (h + 1) * HEAD_DIM)
        oh = o[:, cols]
        zh = z[:, cols]
        r = lax.rsqrt(jnp.mean(oh * oh, -1, keepdims=True) + NORM_EPS)
        mix_ref[:, cols] = (oh * r * gng_ref[...] * (zh * _sigmoid(zh))).astype(BF16)
    for g in range(SGU_GROUPS):
        cols = slice(g * SGU_GROUP, (g + 1) * SGU_GROUP)
        vg = v_ref[:, cols].astype(F32)
        vc = vg - jnp.mean(vg, -1, keepdims=True)
        vn = vc * lax.rsqrt(jnp.mean(vc * vc, -1, keepdims=True) + NORM_EPS)
        vn = (vn * lng_ref[g:g + 1] + lnb_ref[g:g + 1]).astype(BF16)
        wsg = ws_ref[g].astype(BF16)
        for n in range(tm // SGU_CHUNK):
            rows = slice(n * SGU_CHUNK, (n + 1) * SGU_CHUNK)
            s = _dot(wsg, vn[rows]) + bs_ref[g]
            mix_ref[rows, GDN_W + g * SGU_GROUP:GDN_W + (g + 1) * SGU_GROUP] = (
                u_ref[rows, cols].astype(F32) * s).astype(BF16)


def _mix(o_f, o_b, z, ug, vg, gdn_norm_g, ln_g, ln_b, w_s, b_s_full):
    t = z.shape[0]
    tm = min(ROW_TILE, t)
    row = lambda i: (i, 0)
    c2 = lambda i: (0, 0)
    c3 = lambda i: (0, 0, 0)
    return pl.pallas_call(
        _mix_kernel,
        out_shape=jax.ShapeDtypeStruct((t, D_MODEL), BF16),
        grid=(t // tm,),
        in_specs=[pl.BlockSpec((tm, GDN_W), row), pl.BlockSpec((tm, GDN_W), row),
                  pl.BlockSpec((tm, GDN_W), row), pl.BlockSpec((tm, SGU_W), row),
                  pl.BlockSpec((tm, SGU_W), row),
                  pl.BlockSpec((1, HEAD_DIM), c2),
                  pl.BlockSpec((SGU_GROUPS, SGU_GROUP), c2),
                  pl.BlockSpec((SGU_GROUPS, SGU_GROUP), c2),
                  pl.BlockSpec((SGU_GROUPS, SGU_CHUNK, SGU_CHUNK), c3),
                  pl.BlockSpec((SGU_GROUPS, SGU_CHUNK, SGU_GROUP), c3)],
        out_specs=pl.BlockSpec((tm, D_MODEL), row),
        compiler_params=_params("parallel"),
        name="mixer_out",
    )(o_f, o_b, z, ug, vg, gdn_norm_g, ln_g, ln_b, w_s, b_s_full)


def _outproj_kernel(mix_ref, x_ref, mod_ref, ng_ref, wout_ref, rwt_ref, rb_ref,
                    x1_ref, h2_ref, lt_ref):
    mod = mod_ref[0]
    y = _dot(mix_ref[...], wout_ref[...])
    x1 = x_ref[...] + mod[2:3] * _rms(y, ng_ref[1:2])
    x1_ref[...] = x1
    h2 = _rms(x1, ng_ref[2:3]) * (1.0 + mod[4:5]) + mod[3:4]
    h_hi, h_lo = _split2(h2)
    h2_ref[...] = _pack_rows(h2)
    w_hi, w_lo = _split2(rwt_ref[...])
    lt_ref[...] = _dot_nt(w_hi, h_hi) + _dot_nt(w_lo, h_hi) + _dot_nt(w_hi, h_lo) + rb_ref[...]


def _outproj(mix, x2d, mod, rows_per_mod, ng, wout, rwt, rb_col):
    t = x2d.shape[0]
    tm = min(ROW_TILE, t)
    tiles_per_mod = rows_per_mod // tm
    row = lambda i: (i, 0)
    const = lambda i: (0, 0)
    return pl.pallas_call(
        _outproj_kernel,
        out_shape=(jax.ShapeDtypeStruct((t, D_MODEL), F32),
                   jax.ShapeDtypeStruct((t, PACKED_W), U32),
                   jax.ShapeDtypeStruct((N_EXPERTS, t), F32)),
        grid=(t // tm,),
        in_specs=[pl.BlockSpec((tm, D_MODEL), row),
                  pl.BlockSpec((tm, D_MODEL), row),
                  pl.BlockSpec((1, 6, D_MODEL), lambda i: (i // tiles_per_mod, 0, 0)),
                  pl.BlockSpec((4, D_MODEL), const),
                  pl.BlockSpec((D_MODEL, D_MODEL), const),
                  pl.BlockSpec((N_EXPERTS, D_MODEL), const),
                  pl.BlockSpec((N_EXPERTS, 1), const)],
        out_specs=(pl.BlockSpec((tm, D_MODEL), row),
                   pl.BlockSpec((tm, PACKED_W), row),
                   pl.BlockSpec((N_EXPERTS, tm), lambda i: (0, i))),
        compiler_params=_params("parallel"),
        name="out_proj_router",
    )(mix, x2d, mod, ng, wout, rwt, rb_col)


def _moe_kernel(be_ref, nb_ref, xb_ref, wgu_ref, bgu_ref, wd_ref, bd_ref, y_ref,
                wgu_b, wd_b, gut_scr):
    i = pl.program_id(0)
    live = i < nb_ref[0]
    new_expert = jnp.logical_or(i == 0, be_ref[i] != be_ref[jnp.maximum(i - 1, 0)])

    @pl.when(jnp.logical_and(live, new_expert))
    def _():
        wgu_b[...] = wgu_ref[0].astype(BF16)
        wd_b[...] = wd_ref[0].astype(BF16)

    @pl.when(live)
    def _():
        xb = _unpack_rows(xb_ref[...]).astype(BF16)
        gu = _dot(xb, wgu_b[...]) + bgu_ref[0]
        gu_t = gu.T
        acts = []
        for part in range(MOE_ROWS // LANES):
            part_ref = gut_scr.at[part]
            part_ref[...] = gu_t[:, part * LANES:(part + 1) * LANES]
            gate = jnp.minimum(part_ref[pl.ds(0, D_FF, stride=2), :], SWIGLU_LIMIT)
            up = jnp.clip(part_ref[pl.ds(1, D_FF, stride=2), :], -SWIGLU_LIMIT, SWIGLU_LIMIT)
            acts.append(((up + 1.0) * gate * _sigmoid(SWIGLU_ALPHA * gate)).astype(BF16))
        act_t = jnp.concatenate(acts, axis=1)
        y_ref[...] = _pack_rows(_dot_tn(act_t, wd_b[...]) + bd_ref[0])

    @pl.when(jnp.logical_not(live))
    def _():
        y_ref[...] = jnp.zeros_like(y_ref)


def _moe_ffn(block_e, n_used, xb, w_gu, b_gu, w_down, b_down):
    n_rows = xb.shape[0]
    n_blocks = n_rows // MOE_ROWS
    row = lambda i, be, nb: (i, 0)
    ex3 = lambda i, be, nb: (be[i], 0, 0)
    live_row = lambda i, be, nb: (jnp.minimum(i, nb[0] - 1), 0)
    grid_spec = pltpu.PrefetchScalarGridSpec(
        num_scalar_prefetch=2,
        grid=(n_blocks,),
        in_specs=[pl.BlockSpec((MOE_ROWS, PACKED_W), live_row),
                  pl.BlockSpec((1, D_MODEL, 2 * D_FF), ex3),
                  pl.BlockSpec((1, 1, 2 * D_FF), ex3),
                  pl.BlockSpec((1, D_FF, D_MODEL), ex3),
                  pl.BlockSpec((1, 1, D_MODEL), ex3)],
        out_specs=pl.BlockSpec((MOE_ROWS, PACKED_W), row),
        scratch_shapes=[pltpu.VMEM((D_MODEL, 2 * D_FF), BF16),
                        pltpu.VMEM((D_FF, D_MODEL), BF16),
                        pltpu.VMEM((MOE_ROWS // LANES, 2 * D_FF, LANES), F32)],
    )
    return pl.pallas_call(
        _moe_kernel,
        out_shape=jax.ShapeDtypeStruct((n_rows, PACKED_W), U32),
        grid_spec=grid_spec,
        compiler_params=_params("arbitrary"),
        name="moe_ffn",
    )(block_e, n_used, xb, w_gu, b_gu, w_down, b_down)


def _row_copy(src_ref, src_row, dst_ref, dst_row, sem):
    return pltpu.make_async_copy(src_ref.at[pl.ds(src_row, 1)], dst_ref.at[pl.ds(dst_row, 1)], sem)


def _wait_row_copies(n_tokens, make_copy):
    def body(g, carry):
        for _ in range(WAIT_UNROLL * TOP_K):
            make_copy().wait()
        return carry
    lax.fori_loop(0, n_tokens // WAIT_UNROLL, body, 0)


def _combine_kernel(dest_ref, gt_ref, x1_ref, mod_ref, ng_ref, yb_ref, o_ref, buf, sem):
    tm = o_ref.shape[0]

    def issue(tok, carry):
        for k in range(TOP_K):
            _row_copy(yb_ref, dest_ref[k, tok], buf.at[k], tok, sem).start()
        return carry
    lax.fori_loop(0, tm, issue, 0)
    _wait_row_copies(tm, lambda: _row_copy(yb_ref, 0, buf.at[0], 0, sem))

    mod = mod_ref[0]
    gt = gt_ref[...]
    y = _unpack_rows(buf[0]) * gt[:, 0:1]
    for k in range(1, TOP_K):
        y = y + _unpack_rows(buf[k]) * gt[:, k:k + 1]
    o_ref[...] = x1_ref[...] + mod[5:6] * _rms(y, ng_ref[3:4])


def _combine(dest, gates, x1, mod, rows_per_mod, ng, yb):
    t = x1.shape[0]
    tm = min(COMBINE_TILE, t)
    tiles_per_mod = rows_per_mod // tm
    row = lambda i: (i, 0)
    return pl.pallas_call(
        _combine_kernel,
        out_shape=jax.ShapeDtypeStruct((t, D_MODEL), F32),
        grid=(t // tm,),
        in_specs=[pl.BlockSpec((TOP_K, tm), lambda i: (0, i), memory_space=pltpu.SMEM),
                  pl.BlockSpec((tm, TOP_K), row),
                  pl.BlockSpec((tm, D_MODEL), row),
                  pl.BlockSpec((1, 6, D_MODEL), lambda i: (i // tiles_per_mod, 0, 0)),
                  pl.BlockSpec((4, D_MODEL), lambda i: (0, 0)),
                  pl.BlockSpec(memory_space=pl.ANY)],
        out_specs=pl.BlockSpec((tm, D_MODEL), row),
        scratch_shapes=[pltpu.VMEM((TOP_K, tm, PACKED_W), U32), pltpu.SemaphoreType.DMA],
        compiler_params=_params("arbitrary"),
        name="moe_combine",
    )(dest, gates, x1, mod, ng, yb)


def _route_kernel(lt_ref, eidx_ref, gate_ref, rank_ref, cnt_ref, carry):
    i = pl.program_id(0)
    tile = lt_ref.shape[1]

    @pl.when(i == 0)
    def _():
        carry[...] = jnp.zeros_like(carry)

    logits = lt_ref[...]
    eio = lax.broadcasted_iota(jnp.int32, (N_EXPERTS, tile), 0).astype(F32)
    vals, sels = [], []
    for k in range(TOP_K):
        m = jnp.max(logits, axis=0, keepdims=True)
        idx = jnp.min(jnp.where(logits == m, eio, float(N_EXPERTS)), axis=0, keepdims=True)
        sel = eio == idx
        logits = jnp.where(sel, -jnp.inf, logits)
        vals.append(m)
        sels.append(sel)
        eidx_ref[k:k + 1, :] = idx.astype(jnp.int32)
    exps = [jnp.exp(v - vals[0]) for v in vals]
    denom = exps[0] + exps[1] + exps[2] + exps[3]
    for k in range(TOP_K):
        gate_ref[k:k + 1, :] = exps[k] / denom

    member = jnp.where(sels[0] | sels[1] | sels[2] | sels[3], 1.0, 0.0)
    ti = lax.broadcasted_iota(jnp.int32, (tile, tile), 0)
    tj = lax.broadcasted_iota(jnp.int32, (tile, tile), 1)
    earlier = jnp.where(ti < tj, 1.0, 0.0).astype(BF16)
    before = _dot(member.astype(BF16), earlier) + carry[...]
    for k in range(TOP_K):
        rank_ref[k:k + 1, :] = jnp.sum(jnp.where(sels[k], before, 0.0), axis=0,
                                       keepdims=True).astype(jnp.int32)
    carry[...] = carry[...] + jnp.sum(member, axis=1, keepdims=True)
    cnt_ref[...] = carry[...].astype(jnp.int32)


def _route(logits_t):
    t = logits_t.shape[1]
    tile = min(ROUTE_TILE, t)
    blk = lambda i: (0, i)
    return pl.pallas_call(
        _route_kernel,
        out_shape=(jax.ShapeDtypeStruct((TOP_K, t), jnp.int32),
                   jax.ShapeDtypeStruct((TOP_K, t), F32),
                   jax.ShapeDtypeStruct((TOP_K, t), jnp.int32),
                   jax.ShapeDtypeStruct((N_EXPERTS, 1), jnp.int32)),
        grid=(t // tile,),
        in_specs=[pl.BlockSpec((N_EXPERTS, tile), blk)],
        out_specs=(pl.BlockSpec((TOP_K, tile), blk), pl.BlockSpec((TOP_K, tile), blk),
                   pl.BlockSpec((TOP_K, tile), blk),
                   pl.BlockSpec((N_EXPERTS, 1), lambda i: (0, 0))),
        scratch_shapes=[pltpu.VMEM((N_EXPERTS, 1), F32)],
        compiler_params=_params("arbitrary"),
        name="moe_route",
    )(logits_t)


def _slot_tables(eidx, rank, counts, n_blocks):
    padded = (counts + MOE_ROWS - 1) // MOE_ROWS * MOE_ROWS
    pad_end = jnp.cumsum(padded)
    pad_start = pad_end - padded
    experts = jnp.arange(N_EXPERTS, dtype=jnp.int32)
    dest = rank + jnp.sum(jnp.where(eidx[..., None] == experts, pad_start, 0), axis=-1)
    first_row = jnp.arange(n_blocks, dtype=jnp.int32)[:, None] * MOE_ROWS
    block_e = jnp.minimum(jnp.sum((pad_end[None, :] <= first_row).astype(jnp.int32), axis=1),
                          N_EXPERTS - 1)
    n_used = pad_end[-1:] // MOE_ROWS
    return dest.astype(jnp.int32), pad_start.astype(jnp.int32), block_e, n_used.astype(jnp.int32)


def _dispatch_kernel(cnt_ref, start_ref, dest_ref, h_ref, xb_ref, zero_scr, sem, zsem):
    i = pl.program_id(0)
    tile = h_ref.shape[0]

    @pl.when(i == 0)
    def _():
        zero_scr[...] = jnp.zeros_like(zero_scr)

        def per_expert(e, carry):
            n_real = cnt_ref[e]
            first_pad = start_ref[e] + n_real
            n_pad = (MOE_ROWS - n_real % MOE_ROWS) % MOE_ROWS

            def issue(r, c):
                _row_copy(zero_scr, 0, xb_ref, first_pad + r, zsem).start()
                return c

            def drain(r, c):
                _row_copy(zero_scr, 0, xb_ref, first_pad, zsem).wait()
                return c
            lax.fori_loop(0, n_pad, issue, 0)
            lax.fori_loop(0, n_pad, drain, 0)
            return carry
        lax.fori_loop(0, N_EXPERTS, per_expert, 0)

    def issue(tok, carry):
        for k in range(TOP_K):
            _row_copy(h_ref, tok, xb_ref, dest_ref[k, tok], sem).start()
        return carry
    lax.fori_loop(0, tile, issue, 0)
    _wait_row_copies(tile, lambda: _row_copy(h_ref, 0, xb_ref, 0, sem))


def _dispatch(counts, pad_start, dest, h2p, n_rows):
    t = h2p.shape[0]
    tile = min(ROUTE_TILE, t)
    grid_spec = pltpu.PrefetchScalarGridSpec(
        num_scalar_prefetch=2,
        grid=(t // tile,),
        in_specs=[pl.BlockSpec((TOP_K, tile), lambda i, c, s: (0, i), memory_space=pltpu.SMEM),
                  pl.BlockSpec((tile, PACKED_W), lambda i, c, s: (i, 0))],
        out_specs=pl.BlockSpec(memory_space=pl.ANY),
        scratch_shapes=[pltpu.VMEM((8, PACKED_W), U32), pltpu.SemaphoreType.DMA,
                        pltpu.SemaphoreType.DMA],
    )
    return pl.pallas_call(
        _dispatch_kernel,
        out_shape=jax.ShapeDtypeStruct((n_rows, PACKED_W), U32),
        grid_spec=grid_spec,
        compiler_params=_params("arbitrary"),
        name="moe_dispatch",
    )(counts, pad_start, dest, h2p)


def kernel(x, c, ctx, c_ctx, w_ada, b_ada, norm_g, w_in, conv_w, a_log, dt_bias, gdn_norm_g,
           sgu_ln_g, sgu_ln_b, sgu_w, sgu_b, w_out, router_w, router_b, w_gu, b_gu, w_down, b_down):
    b, l, d = x.shape
    lc = ctx.shape[1]
    t = b * l
    assert d == D_MODEL and l % ROW_TILE == 0 and l % GDN_BLOCK == 0 and lc % GDN_BLOCK == 0
    assert w_ada.shape[0] == 1, "single-layer block"

    cs = jnp.concatenate([c, c_ctx[None], jnp.zeros((8 - b - 1, d), F32)], axis=0)
    mod_all = _ada(cs, w_ada[0], b_ada[0][None])
    mod = mod_all[:b].reshape(b, 6, d)
    mod_c = mod_all[b:b + 1].reshape(1, 6, d)
    ng = norm_g[0]

    w = w_in[0]
    wqkv = w[:, :QKV_COLS].astype(BF16)
    wba = w[:, QKV_COLS:QKV_COLS + N_GATE_COLS]
    wzuv = w[:, QKV_COLS + N_GATE_COLS:].astype(BF16)
    wbat = wba.T

    x2d = x.reshape(t, d)
    qkv, z, ug, vg, ba, bat = _inproj(x2d, mod, l, ng[0:1], wqkv, wzuv, wba, wbat)
    ctx2d = ctx.reshape(b * lc, d)
    qkv_c, _, _, _, ba_c, bat_c = _inproj(ctx2d, mod_c, b * lc, ng[0:1], wqkv, wzuv, wba, wbat)

    alog = a_log[0].reshape(-1)
    dtb = dt_bias[0].reshape(-1)
    pc = _gdn_prep(qkv_c.reshape(b, lc, QKV_COLS), conv_w[0], ba_c.reshape(b, lc, N_GATE_COLS),
                   bat_c, alog, dtb, lc)
    s_zero = jnp.zeros((b, N_CHAINS, HEAD_DIM, HEAD_DIM), F32)
    _, _, s_ctx = _gdn_scan(*pc, s_zero)
    pp = _gdn_prep(qkv.reshape(b, l, QKV_COLS), conv_w[0], ba.reshape(b, l, N_GATE_COLS),
                   bat, alog, dtb, GRID_W)
    o_f, o_b, _ = _gdn_scan(*pp, s_ctx)

    b_s_full = jnp.broadcast_to(sgu_b[0][:, :, None], (SGU_GROUPS, SGU_CHUNK, SGU_GROUP))
    mix = _mix(o_f.reshape(t, GDN_W), o_b.reshape(t, GDN_W), z, ug, vg, gdn_norm_g,
               sgu_ln_g[0], sgu_ln_b[0], sgu_w[0], b_s_full)

    x1, h2p, logits_t = _outproj(mix, x2d, mod, l, ng, w_out[0].astype(BF16),
                                 router_w[0].T, router_b[0][:, None])

    eidx, gates_t, rank, counts = _route(logits_t)
    counts = counts[:, 0]
    n_blocks = -(-(t * TOP_K) // MOE_ROWS) + N_EXPERTS
    dest, pad_start, block_e, n_used = _slot_tables(eidx, rank, counts, n_blocks)
    xb = _dispatch(counts, pad_start, dest, h2p, n_blocks * MOE_ROWS)
    yb = _moe_ffn(block_e, n_used, xb, w_gu[0], b_gu[0][:, None, :], w_down[0],
                  b_down[0][:, None, :])
    out = _combine(dest, gates_t.T, x1, mod, l, ng, yb)
    return out.reshape(b, l, d)
```

```python
import functools
import math

import jax
import jax.numpy as jnp
from jax import lax
from jax.experimental import pallas as pl
from jax.experimental.pallas import tpu as pltpu
from jax.experimental.pallas import tpu_sc as plsc

F32 = jnp.float32
BF16 = jnp.bfloat16

D_MODEL = 1024
GDN_HEADS = 4
HEAD_DIM = 128
GDN_W = GDN_HEADS * HEAD_DIM
SGU_GROUPS = 4
SGU_GROUP = 128
SGU_W = SGU_GROUPS * SGU_GROUP
SGU_CHUNK = 128
DELTA_CHUNK = 64
GRID_W = 64
N_EXPERTS = 32
TOP_K = 4
D_FF = 1024
SWIGLU_LIMIT = 7.0
SWIGLU_ALPHA = 1.702
NORM_EPS = 1e-6
QKV_COLS = 3 * GDN_W
N_CHAINS = 2 * GDN_HEADS
N_GATE_COLS = 2 * N_CHAINS

ROW_TILE = 512
GDN_BLOCK = 256
CHUNKS_PER_BLOCK = GDN_BLOCK // DELTA_CHUNK
MOE_ROWS = 256
U32 = jnp.uint32
LANES = 128
PACKED_W = D_MODEL // 2
ROUTE_TILE = 512
SC_CORES = 2
SC_SUBCORES = 16
SC_WINDOW = 128
SC_ROW_W = PACKED_W // 2
VMEM_LIMIT = 56 * 1024 * 1024


def _params(*sem):
    return pltpu.CompilerParams(dimension_semantics=sem, vmem_limit_bytes=VMEM_LIMIT)


def _dot(a, b):
    return jnp.dot(a, b, preferred_element_type=F32)


def _dot_nt(a, b):
    return lax.dot_general(a, b, (((1,), (1,)), ((), ())), preferred_element_type=F32)


def _dot_tn(a, b):
    return lax.dot_general(a, b, (((0,), (0,)), ((), ())), preferred_element_type=F32)


def _split2(a):
    hi = a.astype(BF16)
    lo = (a - hi.astype(F32)).astype(BF16)
    return hi, lo


def _split3(a):
    hi = a.astype(BF16)
    r = a - hi.astype(F32)
    mid = r.astype(BF16)
    lo = (r - mid.astype(F32)).astype(BF16)
    return hi, mid, lo


def _pack_rows(x32):
    xb = x32.astype(BF16).astype(F32)
    hi = lax.bitcast_convert_type(xb[:, :PACKED_W], U32)
    lo = lax.bitcast_convert_type(xb[:, PACKED_W:], U32)
    return hi | (lo >> 16)


def _unpack_rows(w):
    hi = lax.bitcast_convert_type(w & jnp.uint32(0xFFFF0000), F32)
    lo = lax.bitcast_convert_type(w << 16, F32)
    return jnp.concatenate([hi, lo], axis=1)


def _rms(x32, g):
    return x32 * lax.rsqrt(jnp.mean(x32 * x32, -1, keepdims=True) + NORM_EPS) * g


def _gelu_tanh(x):
    c = math.sqrt(2.0 / math.pi)
    return 0.5 * x * (1.0 + jnp.tanh(c * (x + 0.044715 * (x * x * x))))


def _sigmoid(x):
    return 1.0 / (1.0 + jnp.exp(-x))


def _softplus(x):
    return jnp.maximum(x, 0.0) + jnp.log(1.0 + jnp.exp(-jnp.abs(x)))


def _ada_kernel(c_ref, w_ref, b_ref, o_ref):
    c = c_ref[...]
    s = c * _sigmoid(c)
    s_hi, s_lo = _split2(s)
    w_hi, w_lo = _split2(w_ref[...])
    o_ref[...] = _dot(s_hi, w_hi) + _dot(s_lo, w_hi) + _dot(s_hi, w_lo) + b_ref[...]


def _ada(cs, w_ada, b_ada):
    n = w_ada.shape[1]
    bn = D_MODEL
    return pl.pallas_call(
        _ada_kernel,
        out_shape=jax.ShapeDtypeStruct((cs.shape[0], n), F32),
        grid=(n // bn,),
        in_specs=[pl.BlockSpec(cs.shape, lambda j: (0, 0)),
                  pl.BlockSpec((D_MODEL, bn), lambda j: (0, j)),
                  pl.BlockSpec((1, bn), lambda j: (0, j))],
        out_specs=pl.BlockSpec((cs.shape[0], bn), lambda j: (0, j)),
        compiler_params=_params("parallel"),
        name="ada_mod",
    )(cs, w_ada, b_ada)


def _inproj_kernel(x_ref, mod_ref, g_ref, wqkv_ref, wzuv_ref, wba_ref, wbat_ref,
                   qkv_ref, z_ref, u_ref, v_ref, ba_ref, bat_ref):
    x = x_ref[...]
    mod = mod_ref[0]
    h = _rms(x, g_ref[...]) * (1.0 + mod[1:2]) + mod[0:1]
    h_hi, h_lo = _split2(h)
    qkv_ref[...] = _dot(h_hi, wqkv_ref[...]).astype(BF16)
    zuv = _dot(h_hi, wzuv_ref[...])
    z_ref[...] = zuv[:, :GDN_W].astype(BF16)
    u_ref[...] = _gelu_tanh(zuv[:, GDN_W:GDN_W + SGU_W]).astype(BF16)
    v_ref[...] = _gelu_tanh(zuv[:, GDN_W + SGU_W:]).astype(BF16)
    w_hi, w_lo = _split2(wba_ref[...])
    ba_ref[...] = _dot(h_hi, w_hi) + _dot(h_lo, w_hi) + _dot(h_hi, w_lo)
    wt_hi, wt_lo = _split2(wbat_ref[...])
    bat_ref[...] = _dot_nt(wt_hi, h_hi) + _dot_nt(wt_lo, h_hi) + _dot_nt(wt_hi, h_lo)


def _inproj(x2d, mod, rows_per_mod, ng0, wqkv, wzuv, wba, wbat):
    t = x2d.shape[0]
    tm = min(ROW_TILE, t)
    tiles_per_mod = rows_per_mod // tm
    row = lambda i: (i, 0)
    const = lambda i: (0, 0)
    return pl.pallas_call(
        _inproj_kernel,
        out_shape=(jax.ShapeDtypeStruct((t, QKV_COLS), BF16),
                   jax.ShapeDtypeStruct((t, GDN_W), BF16),
                   jax.ShapeDtypeStruct((t, SGU_W), BF16),
                   jax.ShapeDtypeStruct((t, SGU_W), BF16),
                   jax.ShapeDtypeStruct((t, N_GATE_COLS), F32),
                   jax.ShapeDtypeStruct((N_GATE_COLS, t), F32)),
        grid=(t // tm,),
        in_specs=[pl.BlockSpec((tm, D_MODEL), row),
                  pl.BlockSpec((1, 6, D_MODEL), lambda i: (i // tiles_per_mod, 0, 0)),
                  pl.BlockSpec((1, D_MODEL), const),
                  pl.BlockSpec(wqkv.shape, const),
                  pl.BlockSpec(wzuv.shape, const),
                  pl.BlockSpec(wba.shape, const),
                  pl.BlockSpec(wbat.shape, const)],
        out_specs=(pl.BlockSpec((tm, QKV_COLS), row),
                   pl.BlockSpec((tm, GDN_W), row),
                   pl.BlockSpec((tm, SGU_W), row),
                   pl.BlockSpec((tm, SGU_W), row),
                   pl.BlockSpec((tm, N_GATE_COLS), row),
                   pl.BlockSpec((N_GATE_COLS, tm), lambda i: (0, i))),
        compiler_params=_params("parallel"),
        name="in_proj",
    )(x2d, mod, ng0, wqkv, wzuv, wba, wbat)


def _gdn_prep_kernel(row_len, qkv_ref, cw_ref, ba_ref, bat_ref, alog_r_ref, dtb_r_ref,
                     alog_c_ref, dtb_c_ref, u_ref, w_ref, qd_ref, kd_ref, at_ref, gl_ref):
    n = GDN_BLOCK
    c = DELTA_CHUNK
    ri = lax.broadcasted_iota(jnp.int32, (n, n), 0)
    ci = lax.broadcasted_iota(jnp.int32, (n, n), 1)
    same = (ri // c) == (ci // c)
    lower = same & (ri >= ci)
    upper = same & (ri <= ci)

    def mask01(m):
        return jnp.where(m, 1.0, 0.0).astype(BF16)

    lower_b = mask01(lower)
    upper_b = mask01(upper)
    same_b = mask01(same)

    wi = lax.broadcasted_iota(jnp.int32, (c, n), 0)
    wl = lax.broadcasted_iota(jnp.int32, (c, n), 1)
    wchunk = wl // c
    wj = wl % c
    lower_w = wi >= wj
    upper_w = wi <= wj
    diag_w = wi == wj
    eye_w = jnp.where(diag_w, 1.0, 0.0)
    pair_masks = []
    s = 1
    while s < c:
        pair_masks.append(mask01(((wi // (2 * s)) == (wj // (2 * s))) & ((wi // s) != (wj // s))))
        s *= 2

    def to_wide(full):
        out = full[:c]
        for k in range(1, CHUNKS_PER_BLOCK):
            out = jnp.where(wchunk == k, full[k * c:(k + 1) * c], out)
        return out

    def col_wide(col):
        out = jnp.broadcast_to(col[:c], (c, n))
        for k in range(1, CHUNKS_PER_BLOCK):
            out = jnp.where(wchunk == k, jnp.broadcast_to(col[k * c:(k + 1) * c], (c, n)), out)
        return out

    def block_diag(x_w):
        return jnp.concatenate([x_w] * CHUNKS_PER_BLOCK, axis=0) * same_b

    ba = ba_ref[0]
    bat = bat_ref[...]
    beta_c = _sigmoid(ba[:, :N_CHAINS])
    g_c = -jnp.exp(alog_r_ref[...]) * _softplus(ba[:, N_CHAINS:] + dtb_r_ref[...])
    g_r = -jnp.exp(alog_c_ref[...]) * _softplus(bat[N_CHAINS:] + dtb_c_ref[...])
    gc3 = _split3(g_c)
    gr3 = jnp.concatenate(_split3(g_r), axis=0)

    def sum3_r(m):
        return m[:N_CHAINS] + m[N_CHAINS:2 * N_CHAINS] + m[2 * N_CHAINS:]

    cum_f_c = _dot(lower_b, gc3[0]) + _dot(lower_b, gc3[1]) + _dot(lower_b, gc3[2])
    tot_c = _dot(same_b, gc3[0]) + _dot(same_b, gc3[1]) + _dot(same_b, gc3[2])
    cum_b_c = tot_c - cum_f_c + g_c
    cum_f_r = sum3_r(_dot(gr3, upper_b))
    cum_b_r = sum3_r(_dot(gr3, lower_b))
    ei = lax.broadcasted_iota(jnp.int32, (n, CHUNKS_PER_BLOCK * HEAD_DIM), 0) // c
    ej = lax.broadcasted_iota(jnp.int32, (n, CHUNKS_PER_BLOCK * HEAD_DIM), 1) // HEAD_DIM
    g_last = jnp.exp(sum3_r(_dot(gr3, mask01(ei == ej))))
    gl_ref[0, 0, 0] = g_last[:GDN_HEADS]
    gl_ref[0, 0, 1] = g_last[GDN_HEADS:]

    pos = lax.broadcasted_iota(jnp.int32, (n, HEAD_DIM), 0) % row_len
    first = pos == 0
    last = pos == row_len - 1

    def conv_silu(col):
        x = qkv_ref[0, :, col * HEAD_DIM:(col + 1) * HEAD_DIM].astype(F32)
        cw = cw_ref[:, col * HEAD_DIM:(col + 1) * HEAD_DIM]
        xp = jnp.where(first, 0.0, pltpu.roll(x, 1, 0))
        xn = jnp.where(last, 0.0, pltpu.roll(x, n - 1, 0))
        y = xp * cw[0:1] + x * cw[1:2] + xn * cw[2:3]
        return y * _sigmoid(y)

    def l2n(x):
        return x * lax.rsqrt(jnp.sum(x * x, -1, keepdims=True) + NORM_EPS)

    a_bs, ps, rhss = [None] * N_CHAINS, [None] * N_CHAINS, [None] * N_CHAINS
    for h in range(GDN_HEADS):
        q = l2n(conv_silu(h)) * (HEAD_DIM ** -0.5)
        k = l2n(conv_silu(GDN_HEADS + h))
        v = conv_silu(2 * GDN_HEADS + h)
        k_b = k.astype(BF16)
        qk_kk = _dot_nt(jnp.concatenate([q.astype(BF16), k_b], axis=0), k_b)
        qk_w = to_wide(qk_kk[:n])
        kk_w = to_wide(qk_kk[n:])
        for d in range(2):
            j = d * GDN_HEADS + h
            mask_w = lower_w if d == 0 else upper_w
            cum_c = (cum_f_c if d == 0 else cum_b_c)[:, j:j + 1]
            cum_r = (cum_f_r if d == 0 else cum_b_r)[j:j + 1, :]
            b_c = beta_c[:, j:j + 1]
            decay_w = jnp.where(mask_w, jnp.exp(jnp.where(mask_w, col_wide(cum_c) - cum_r, 0.0)), 0.0)
            amat_w = jnp.where(diag_w, 0.0, kk_w * decay_w * col_wide(b_c))
            a_bs[j] = amat_w.astype(BF16)
            ps[j] = eye_w - amat_w * pair_masks[0].astype(F32)
            e_c = jnp.exp(cum_c)
            rhss[j] = jnp.concatenate([(v * b_c).astype(BF16), (k * (b_c * e_c)).astype(BF16)], axis=1)
            cols = slice(j * HEAD_DIM, (j + 1) * HEAD_DIM)
            qd_ref[0, :, cols] = (q * e_c).astype(BF16)
            kd_ref[0, :, cols] = (k * jnp.exp(tot_c[:, j:j + 1] - cum_c)).astype(BF16)
            at_ref[0, 0, j * c:(j + 1) * c, :] = (qk_w * decay_w).astype(BF16)

    for pm in pair_masks[1:]:
        p_bs = [p.astype(BF16) for p in ps]
        ys = [_dot(a_bs[j] * pm, block_diag(p_bs[j])) for j in range(N_CHAINS)]
        ps = [ps[j] - _dot(p_bs[j], block_diag(ys[j].astype(BF16))) for j in range(N_CHAINS)]

    for j in range(N_CHAINS):
        uw = _dot(block_diag(ps[j].astype(BF16)), rhss[j])
        cols = slice(j * HEAD_DIM, (j + 1) * HEAD_DIM)
        u_ref[0, :, cols] = uw[:, :HEAD_DIM].astype(BF16)
        w_ref[0, :, cols] = uw[:, HEAD_DIM:].astype(BF16)


def _gdn_prep(qkv, conv_w, ba, bat, alog, dtb, row_len):
    b, l, _ = qkv.shape
    nblk = l // GDN_BLOCK
    wide = N_CHAINS * HEAD_DIM
    blk = lambda bi, i: (bi, i, 0)
    const = lambda bi, i: (0, 0)
    alog_r, dtb_r = alog.reshape(1, N_CHAINS), dtb.reshape(1, N_CHAINS)
    alog_c, dtb_c = alog.reshape(N_CHAINS, 1), dtb.reshape(N_CHAINS, 1)
    return pl.pallas_call(
        functools.partial(_gdn_prep_kernel, row_len),
        out_shape=(jax.ShapeDtypeStruct((b, l, wide), BF16),) * 4 + (
            jax.ShapeDtypeStruct((b, nblk, N_CHAINS * DELTA_CHUNK, GDN_BLOCK), BF16),
            jax.ShapeDtypeStruct((b, nblk, 2, GDN_HEADS, CHUNKS_PER_BLOCK * HEAD_DIM), F32)),
        grid=(b, nblk),
        in_specs=[pl.BlockSpec((1, GDN_BLOCK, QKV_COLS), blk),
                  pl.BlockSpec((3, QKV_COLS), const),
                  pl.BlockSpec((1, GDN_BLOCK, N_GATE_COLS), blk),
                  pl.BlockSpec((N_GATE_COLS, GDN_BLOCK), lambda bi, i: (0, bi * nblk + i)),
                  pl.BlockSpec((1, N_CHAINS), const),
                  pl.BlockSpec((1, N_CHAINS), const),
                  pl.BlockSpec((N_CHAINS, 1), const),
                  pl.BlockSpec((N_CHAINS, 1), const)],
        out_specs=(pl.BlockSpec((1, GDN_BLOCK, wide), blk),) * 4 + (
            pl.BlockSpec((1, 1, N_CHAINS * DELTA_CHUNK, GDN_BLOCK), lambda bi, i: (bi, i, 0, 0)),
            pl.BlockSpec((1, 1, 2, GDN_HEADS, CHUNKS_PER_BLOCK * HEAD_DIM),
                         lambda bi, i: (bi, i, 0, 0, 0))),
        compiler_params=_params("parallel", "parallel"),
        name="gdn_prep",
    )(qkv, conv_w, ba, bat, alog_r, dtb_r, alog_c, dtb_c)


def _gdn_scan_kernel(uf, wf, qf, kf, af, gf, ub, wb, qb, kb, ab, gb, s0_ref,
                     of_ref, ob_ref, sfin_ref, s_scr):
    i = pl.program_id(1)
    c = DELTA_CHUNK

    @pl.when(i == 0)
    def _():
        s_scr[...] = s0_ref[0]

    ops = ((uf, wf, qf, kf, af, gf, of_ref), (ub, wb, qb, kb, ab, gb, ob_ref))
    chains = [(d, h) for d in range(2) for h in range(GDN_HEADS)]
    states = [s_scr[j] for j in range(N_CHAINS)]
    for step in range(CHUNKS_PER_BLOCK):
        def chunk(d):
            cc = step if d == 0 else CHUNKS_PER_BLOCK - 1 - step
            return cc, slice(cc * c, (cc + 1) * c)

        xs = []
        for j, (d, h) in enumerate(chains):
            _, rows = chunk(d)
            cols = slice(h * HEAD_DIM, (h + 1) * HEAD_DIM)
            wq = jnp.concatenate([ops[d][1][0, rows, cols], ops[d][2][0, rows, cols]], axis=0)
            xs.append(_dot(wq, states[j].astype(BF16)))
        v_news = []
        for j, (d, h) in enumerate(chains):
            _, rows = chunk(d)
            cols = slice(h * HEAD_DIM, (h + 1) * HEAD_DIM)
            v_news.append((ops[d][0][0, rows, cols].astype(F32) - xs[j][:c]).astype(BF16))
        for j, (d, h) in enumerate(chains):
            cc, rows = chunk(d)
            cols = slice(h * HEAD_DIM, (h + 1) * HEAD_DIM)
            a_c = ops[d][4][0, 0, h * c:(h + 1) * c, cc * c:(cc + 1) * c]
            ops[d][6][0, rows, cols] = xs[j][c:] + _dot(a_c, v_news[j])
            ds = _dot_tn(ops[d][3][0, rows, cols], v_news[j])
            g_last = ops[d][5][0, 0, 0, h:h + 1, cc * HEAD_DIM:(cc + 1) * HEAD_DIM]
            states[j] = states[j] * g_last + ds
    for j in range(N_CHAINS):
        s_scr[j] = states[j]

    @pl.when(i == pl.num_programs(1) - 1)
    def _():
        sfin_ref[0] = s_scr[...]


def _gdn_scan(u, w, qd, kd, at, gl, s0):
    b, l, _ = u.shape
    nblk = l // GDN_BLOCK
    half = GDN_HEADS * HEAD_DIM
    fwd = lambda bi, i: (bi, i, 0)
    bwd = lambda bi, i: (bi, nblk - 1 - i, 1)
    big = lambda m: pl.BlockSpec((1, GDN_BLOCK, half), m)
    att_shape = (1, 1, GDN_HEADS * DELTA_CHUNK, GDN_BLOCK)
    attf = pl.BlockSpec(att_shape, lambda bi, i: (bi, i, 0, 0))
    attb = pl.BlockSpec(att_shape, lambda bi, i: (bi, nblk - 1 - i, 1, 0))
    gl_shape = (1, 1, 1, GDN_HEADS, CHUNKS_PER_BLOCK * HEAD_DIM)
    glf = pl.BlockSpec(gl_shape, lambda bi, i: (bi, i, 0, 0, 0))
    glb = pl.BlockSpec(gl_shape, lambda bi, i: (bi, nblk - 1 - i, 1, 0, 0))
    state = pl.BlockSpec((1, N_CHAINS, HEAD_DIM, HEAD_DIM), lambda bi, i: (bi, 0, 0, 0))
    return pl.pallas_call(
        _gdn_scan_kernel,
        out_shape=(jax.ShapeDtypeStruct((b, l, half), F32),
                   jax.ShapeDtypeStruct((b, l, half), F32),
                   jax.ShapeDtypeStruct((b, N_CHAINS, HEAD_DIM, HEAD_DIM), F32)),
        grid=(b, nblk),
        in_specs=[big(fwd), big(fwd), big(fwd), big(fwd), attf, glf,
                  big(bwd), big(bwd), big(bwd), big(bwd), attb, glb, state],
        out_specs=(pl.BlockSpec((1, GDN_BLOCK, half), fwd),
                   pl.BlockSpec((1, GDN_BLOCK, half), lambda bi, i: (bi, nblk - 1 - i, 0)),
                   state),
        scratch_shapes=[pltpu.VMEM((N_CHAINS, HEAD_DIM, HEAD_DIM), F32)],
        compiler_params=_params("parallel", "arbitrary"),
        name="gdn_scan",
    )(u, w, qd, kd, at, gl, u, w, qd, kd, at, gl, s0)


def _mix_kernel(of_ref, ob_ref, z_ref, u_ref, v_ref, gng_ref, lng_ref, lnb_ref, ws_ref, bs_ref,
                mix_ref):
    tm = of_ref.shape[0]
    o = of_ref[...] + ob_ref[...]
    z = z_ref[...].astype(F32)
    for h in range(GDN_HEADS):
        cols = slice(h * HEAD_DIM, (h + 1) * HEAD_DIM)
        oh = o[:, cols]
        zh = z[:, cols]
        r = lax.rsqrt(jnp.mean(oh * oh, -1, keepdims=True) + NORM_EPS)
        mix_ref[:, cols] = (oh * r * gng_ref[...] * (zh * _sigmoid(zh))).astype(BF16)
    for g in range(SGU_GROUPS):
        cols = slice(g * SGU_GROUP, (g + 1) * SGU_GROUP)
        vg = v_ref[:, cols].astype(F32)
        vc = vg - jnp.mean(vg, -1, keepdims=True)
        vn = vc * lax.rsqrt(jnp.mean(vc * vc, -1, keepdims=True) + NORM_EPS)
        vn = (vn * lng_ref[g:g + 1] + lnb_ref[g:g + 1]).astype(BF16)
        wsg = ws_ref[g].astype(BF16)
        for n in range(tm // SGU_CHUNK):
            rows = slice(n * SGU_CHUNK, (n + 1) * SGU_CHUNK)
            s = _dot(wsg, vn[rows]) + bs_ref[g]
            mix_ref[rows, GDN_W + g * SGU_GROUP:GDN_W + (g + 1) * SGU_GROUP] = (
                u_ref[rows, cols].astype(F32) * s).astype(BF16)


def _mix(o_f, o_b, z, ug, vg, gdn_norm_g, ln_g, ln_b, w_s, b_s_full):
    t = z.shape[0]
    tm = min(ROW_TILE, t)
    row = lambda i: (i, 0)
    c2 = lambda i: (0, 0)
    c3 = lambda i: (0, 0, 0)
    return pl.pallas_call(
        _mix_kernel,
        out_shape=jax.ShapeDtypeStruct((t, D_MODEL), BF16),
        grid=(t // tm,),
        in_specs=[pl.BlockSpec((tm, GDN_W), row), pl.BlockSpec((tm, GDN_W), row),
                  pl.BlockSpec((tm, GDN_W), row), pl.BlockSpec((tm, SGU_W), row),
                  pl.BlockSpec((tm, SGU_W), row),
                  pl.BlockSpec((1, HEAD_DIM), c2),
                  pl.BlockSpec((SGU_GROUPS, SGU_GROUP), c2),
                  pl.BlockSpec((SGU_GROUPS, SGU_GROUP), c2),
                  pl.BlockSpec((SGU_GROUPS, SGU_CHUNK, SGU_CHUNK), c3),
                  pl.BlockSpec((SGU_GROUPS, SGU_CHUNK, SGU_GROUP), c3)],
        out_specs=pl.BlockSpec((tm, D_MODEL), row),
        compiler_params=_params("parallel"),
        name="mixer_out",
    )(o_f, o_b, z, ug, vg, gdn_norm_g, ln_g, ln_b, w_s, b_s_full)


def _outproj_kernel(mix_ref, x_ref, mod_ref, ng_ref, wout_ref, rwt_ref, rb_ref,
                    x1_ref, h2_ref, lt_ref):
    mod = mod_ref[0]
    y = _dot(mix_ref[...], wout_ref[...])
    x1 = x_ref[...] + mod[2:3] * _rms(y, ng_ref[1:2])
    x1_ref[...] = x1
    h2 = _rms(x1, ng_ref[2:3]) * (1.0 + mod[4:5]) + mod[3:4]
    h_hi, h_lo = _split2(h2)
    h2_ref[...] = _pack_rows(h2)
    w_hi, w_lo = _split2(rwt_ref[...])
    lt_ref[...] = _dot_nt(w_hi, h_hi) + _dot_nt(w_lo, h_hi) + _dot_nt(w_hi, h_lo) + rb_ref[...]


def _outproj(mix, x2d, mod, rows_per_mod, ng, wout, rwt, rb_col):
    t = x2d.shape[0]
    tm = min(ROW_TILE, t)
    tiles_per_mod = rows_per_mod // tm
    row = lambda i: (i, 0)
    const = lambda i: (0, 0)
    return pl.pallas_call(
        _outproj_kernel,
        out_shape=(jax.ShapeDtypeStruct((t, D_MODEL), F32),
                   jax.ShapeDtypeStruct((t, PACKED_W), U32),
                   jax.ShapeDtypeStruct((N_EXPERTS, t), F32)),
        grid=(t // tm,),
        in_specs=[pl.BlockSpec((tm, D_MODEL), row),
                  pl.BlockSpec((tm, D_MODEL), row),
                  pl.BlockSpec((1, 6, D_MODEL), lambda i: (i // tiles_per_mod, 0, 0)),
                  pl.BlockSpec((4, D_MODEL), const),
                  pl.BlockSpec((D_MODEL, D_MODEL), const),
                  pl.BlockSpec((N_EXPERTS, D_MODEL), const),
                  pl.BlockSpec((N_EXPERTS, 1), const)],
        out_specs=(pl.BlockSpec((tm, D_MODEL), row),
                   pl.BlockSpec((tm, PACKED_W), row),
                   pl.BlockSpec((N_EXPERTS, tm), lambda i: (0, i))),
        compiler_params=_params("parallel"),
        name="out_proj_router",
    )(mix, x2d, mod, ng, wout, rwt, rb_col)


def _moe_kernel(be_ref, nb_ref, xb_ref, wgu_ref, bgu_ref, wd_ref, bd_ref, y_ref,
                wgu_b, wd_b, gut_scr):
    i = pl.program_id(0)
    live = i < nb_ref[0]
    new_expert = jnp.logical_or(i == 0, be_ref[i] != be_ref[jnp.maximum(i - 1, 0)])

    @pl.when(jnp.logical_and(live, new_expert))
    def _():
        wgu_b[...] = wgu_ref[0].astype(BF16)
        wd_b[...] = wd_ref[0].astype(BF16)

    @pl.when(live)
    def _():
        xb = _unpack_rows(xb_ref[...]).astype(BF16)
        gu = _dot(xb, wgu_b[...]) + bgu_ref[0]
        gu_t = gu.T
        acts = []
        for part in range(MOE_ROWS // LANES):
            part_ref = gut_scr.at[part]
            part_ref[...] = gu_t[:, part * LANES:(part + 1) * LANES]
            gate = jnp.minimum(part_ref[pl.ds(0, D_FF, stride=2), :], SWIGLU_LIMIT)
            up = jnp.clip(part_ref[pl.ds(1, D_FF, stride=2), :], -SWIGLU_LIMIT, SWIGLU_LIMIT)
            acts.append(((up + 1.0) * gate * _sigmoid(SWIGLU_ALPHA * gate)).astype(BF16))
        act_t = jnp.concatenate(acts, axis=1)
        y_ref[...] = _pack_rows(_dot_tn(act_t, wd_b[...]) + bd_ref[0])

    @pl.when(jnp.logical_not(live))
    def _():
        y_ref[...] = jnp.zeros_like(y_ref)


def _moe_ffn(block_e, n_used, xb, w_gu, b_gu, w_down, b_down):
    n_rows = xb.shape[0]
    n_blocks = n_rows // MOE_ROWS
    row = lambda i, be, nb: (i, 0)
    ex3 = lambda i, be, nb: (be[i], 0, 0)
    live_row = lambda i, be, nb: (jnp.minimum(i, nb[0] - 1), 0)
    grid_spec = pltpu.PrefetchScalarGridSpec(
        num_scalar_prefetch=2,
        grid=(n_blocks,),
        in_specs=[pl.BlockSpec((MOE_ROWS, PACKED_W), live_row),
                  pl.BlockSpec((1, D_MODEL, 2 * D_FF), ex3),
                  pl.BlockSpec((1, 1, 2 * D_FF), ex3),
                  pl.BlockSpec((1, D_FF, D_MODEL), ex3),
                  pl.BlockSpec((1, 1, D_MODEL), ex3)],
        out_specs=pl.BlockSpec((MOE_ROWS, PACKED_W), row),
        scratch_shapes=[pltpu.VMEM((D_MODEL, 2 * D_FF), BF16),
                        pltpu.VMEM((D_FF, D_MODEL), BF16),
                        pltpu.VMEM((MOE_ROWS // LANES, 2 * D_FF, LANES), F32)],
    )
    return pl.pallas_call(
        _moe_kernel,
        out_shape=jax.ShapeDtypeStruct((n_rows, PACKED_W), U32),
        grid_spec=grid_spec,
        compiler_params=_params("arbitrary"),
        name="moe_ffn",
    )(block_e, n_used, xb, w_gu, b_gu, w_down, b_down)


def _combine_kernel(y0_ref, y1_ref, y2_ref, y3_ref, gt_ref, x1_ref, mod_ref, ng_ref, o_ref):
    mod = mod_ref[0]
    gt = gt_ref[...]
    y = _unpack_rows(y0_ref[...]) * gt[:, 0:1]
    for k, y_ref in ((1, y1_ref), (2, y2_ref), (3, y3_ref)):
        y = y + _unpack_rows(y_ref[...]) * gt[:, k:k + 1]
    o_ref[...] = x1_ref[...] + mod[5:6] * _rms(y, ng_ref[3:4])


def _combine(yg, gates, x1, mod, rows_per_mod, ng):
    t = x1.shape[0]
    tm = min(ROW_TILE, t)
    tiles_per_mod = rows_per_mod // tm
    n_tiles = t // tm
    row = lambda i: (i, 0)
    choice = lambda k: pl.BlockSpec((tm, PACKED_W), lambda i: (k * n_tiles + i, 0))
    return pl.pallas_call(
        _combine_kernel,
        out_shape=jax.ShapeDtypeStruct((t, D_MODEL), F32),
        grid=(n_tiles,),
        in_specs=[choice(0), choice(1), choice(2), choice(3),
                  pl.BlockSpec((tm, TOP_K), row),
                  pl.BlockSpec((tm, D_MODEL), row),
                  pl.BlockSpec((1, 6, D_MODEL), lambda i: (i // tiles_per_mod, 0, 0)),
                  pl.BlockSpec((4, D_MODEL), lambda i: (0, 0))],
        out_specs=pl.BlockSpec((tm, D_MODEL), row),
        compiler_params=_params("parallel"),
        name="moe_combine",
    )(yg, yg, yg, yg, gates, x1, mod, ng)


def _route_kernel(lt_ref, eidx_ref, gate_ref, rank_ref, cnt_ref, carry):
    i = pl.program_id(0)
    tile = lt_ref.shape[1]

    @pl.when(i == 0)
    def _():
        carry[...] = jnp.zeros_like(carry)

    logits = lt_ref[...]
    eio = lax.broadcasted_iota(jnp.int32, (N_EXPERTS, tile), 0).astype(F32)
    vals, sels = [], []
    for k in range(TOP_K):
        m = jnp.max(logits, axis=0, keepdims=True)
        idx = jnp.min(jnp.where(logits == m, eio, float(N_EXPERTS)), axis=0, keepdims=True)
        sel = eio == idx
        logits = jnp.where(sel, -jnp.inf, logits)
        vals.append(m)
        sels.append(sel)
        eidx_ref[k:k + 1, :] = idx.astype(jnp.int32)
    exps = [jnp.exp(v - vals[0]) for v in vals]
    denom = exps[0] + exps[1] + exps[2] + exps[3]
    for k in range(TOP_K):
        gate_ref[k:k + 1, :] = exps[k] / denom

    member = jnp.where(sels[0] | sels[1] | sels[2] | sels[3], 1.0, 0.0)
    ti = lax.broadcasted_iota(jnp.int32, (tile, tile), 0)
    tj = lax.broadcasted_iota(jnp.int32, (tile, tile), 1)
    earlier = jnp.where(ti < tj, 1.0, 0.0).astype(BF16)
    before = _dot(member.astype(BF16), earlier) + carry[...]
    for k in range(TOP_K):
        rank_ref[k:k + 1, :] = jnp.sum(jnp.where(sels[k], before, 0.0), axis=0,
                                       keepdims=True).astype(jnp.int32)
    carry[...] = carry[...] + jnp.sum(member, axis=1, keepdims=True)
    cnt_ref[...] = carry[...].astype(jnp.int32)


def _route(logits_t):
    t = logits_t.shape[1]
    tile = min(ROUTE_TILE, t)
    blk = lambda i: (0, i)
    return pl.pallas_call(
        _route_kernel,
        out_shape=(jax.ShapeDtypeStruct((TOP_K, t), jnp.int32),
                   jax.ShapeDtypeStruct((TOP_K, t), F32),
                   jax.ShapeDtypeStruct((TOP_K, t), jnp.int32),
                   jax.ShapeDtypeStruct((N_EXPERTS, 1), jnp.int32)),
        grid=(t // tile,),
        in_specs=[pl.BlockSpec((N_EXPERTS, tile), blk)],
        out_specs=(pl.BlockSpec((TOP_K, tile), blk), pl.BlockSpec((TOP_K, tile), blk),
                   pl.BlockSpec((TOP_K, tile), blk),
                   pl.BlockSpec((N_EXPERTS, 1), lambda i: (0, 0))),
        scratch_shapes=[pltpu.VMEM((N_EXPERTS, 1), F32)],
        compiler_params=_params("arbitrary"),
        name="moe_route",
    )(logits_t)


def _slot_tables(eidx, rank, counts, n_blocks):
    padded = (counts + MOE_ROWS - 1) // MOE_ROWS * MOE_ROWS
    pad_end = jnp.cumsum(padded)
    pad_start = pad_end - padded
    experts = jnp.arange(N_EXPERTS, dtype=jnp.int32)
    dest = rank + jnp.sum(jnp.where(eidx[..., None] == experts, pad_start, 0), axis=-1)
    first_row = jnp.arange(n_blocks, dtype=jnp.int32)[:, None] * MOE_ROWS
    block_e = jnp.minimum(jnp.sum((pad_end[None, :] <= first_row).astype(jnp.int32), axis=1),
                          N_EXPERTS - 1)
    n_used = pad_end[-1:] // MOE_ROWS
    return dest.astype(jnp.int32), pad_start.astype(jnp.int32), block_e, n_used.astype(jnp.int32)


def _sc_mesh():
    return plsc.VectorSubcoreMesh(core_axis_name="c", subcore_axis_name="s",
                                  num_cores=SC_CORES, num_subcores=SC_SUBCORES)


def _half_row_ids(rows):
    return jnp.stack([2 * rows, 2 * rows + 1], axis=-1).reshape(*rows.shape[:-1], -1)


def _sc_gather_rows(table, rows):
    idx = _half_row_ids(rows)[None]
    n_half = idx.shape[1]

    @functools.partial(pl.kernel, mesh=_sc_mesh(), name="moe_gather_rows",
                       out_type=jax.ShapeDtypeStruct((n_half, SC_ROW_W), U32))
    def gather(x_hbm, i_hbm, o_hbm):
        def body(i_vmem, o_vmem):
            pltpu.sync_copy(x_hbm.at[i_vmem.at[0]], o_vmem)

        pltpu.emit_pipeline(
            body, grid=(n_half // SC_WINDOW,),
            in_specs=[pl.BlockSpec((1, SC_WINDOW), lambda i: (0, i))],
            out_specs=[pl.BlockSpec((SC_WINDOW, SC_ROW_W), lambda i: (i, 0))],
            core_axis_name=("c", "s"), dimension_semantics=(pltpu.PARALLEL,),
        )(i_hbm, o_hbm)

    return gather(table.reshape(-1, SC_ROW_W), idx).reshape(-1, PACKED_W)


def _sc_scatter_rows(rows, dest, n_out):
    idx = _half_row_ids(dest)
    n_half = idx.shape[1]

    @functools.partial(pl.kernel, mesh=_sc_mesh(), name="moe_scatter_rows", scratch_types=[],
                       out_type=jax.ShapeDtypeStruct((2 * n_out, SC_ROW_W), U32))
    def scatter(x_hbm, i_hbm, o_hbm):
        def body(x_vmem, i_vmem):
            for k in range(TOP_K):
                pltpu.sync_copy(x_vmem, o_hbm.at[i_vmem.at[k]])

        pltpu.emit_pipeline(
            body, grid=(n_half // SC_WINDOW,),
            in_specs=[pl.BlockSpec((SC_WINDOW, SC_ROW_W), lambda i: (i, 0)),
                      pl.BlockSpec((TOP_K, SC_WINDOW), lambda i: (0, i))],
            out_specs=[],
            core_axis_name=("c", "s"), dimension_semantics=(pltpu.PARALLEL,),
        )(x_hbm, i_hbm)

    return scatter(rows.reshape(-1, SC_ROW_W), idx).reshape(n_out, PACKED_W)


def _zero_pad_kernel(cnt_ref, start_ref, xb_in_ref, xb_ref, zero_scr, sem):
    del xb_in_ref
    zero_scr[...] = jnp.zeros_like(zero_scr)

    def zero_row_copy(row):
        return pltpu.make_async_copy(zero_scr.at[pl.ds(0, 1)], xb_ref.at[pl.ds(row, 1)], sem)

    def per_expert(e, carry):
        n_real = cnt_ref[e]
        first_pad = start_ref[e] + n_real
        n_pad = (MOE_ROWS - n_real % MOE_ROWS) % MOE_ROWS

        def issue(r, c):
            zero_row_copy(first_pad + r).start()
            return c

        def drain(r, c):
            zero_row_copy(first_pad).wait()
            return c
        lax.fori_loop(0, n_pad, issue, 0)
        lax.fori_loop(0, n_pad, drain, 0)
        return carry
    lax.fori_loop(0, N_EXPERTS, per_expert, 0)


def _zero_pad_slots(counts, pad_start, xb):
    grid_spec = pltpu.PrefetchScalarGridSpec(
        num_scalar_prefetch=2,
        grid=(1,),
        in_specs=[pl.BlockSpec(memory_space=pl.ANY)],
        out_specs=pl.BlockSpec(memory_space=pl.ANY),
        scratch_shapes=[pltpu.VMEM((8, PACKED_W), U32), pltpu.SemaphoreType.DMA],
    )
    return pl.pallas_call(
        _zero_pad_kernel,
        out_shape=jax.ShapeDtypeStruct(xb.shape, xb.dtype),
        grid_spec=grid_spec,
        input_output_aliases={2: 0},
        compiler_params=_params("arbitrary"),
        name="moe_zero_pad",
    )(counts, pad_start, xb)


def kernel(x, c, ctx, c_ctx, w_ada, b_ada, norm_g, w_in, conv_w, a_log, dt_bias, gdn_norm_g,
           sgu_ln_g, sgu_ln_b, sgu_w, sgu_b, w_out, router_w, router_b, w_gu, b_gu, w_down, b_down):
    b, l, d = x.shape
    lc = ctx.shape[1]
    t = b * l
    assert d == D_MODEL and l % ROW_TILE == 0 and l % GDN_BLOCK == 0 and lc % GDN_BLOCK == 0
    assert w_ada.shape[0] == 1, "single-layer block"

    cs = jnp.concatenate([c, c_ctx[None], jnp.zeros((8 - b - 1, d), F32)], axis=0)
    mod_all = _ada(cs, w_ada[0], b_ada[0][None])
    mod = mod_all[:b].reshape(b, 6, d)
    mod_c = mod_all[b:b + 1].reshape(1, 6, d)
    ng = norm_g[0]

    w = w_in[0]
    wqkv = w[:, :QKV_COLS].astype(BF16)
    wba = w[:, QKV_COLS:QKV_COLS + N_GATE_COLS]
    wzuv = w[:, QKV_COLS + N_GATE_COLS:].astype(BF16)
    wbat = wba.T

    x2d = x.reshape(t, d)
    qkv, z, ug, vg, ba, bat = _inproj(x2d, mod, l, ng[0:1], wqkv, wzuv, wba, wbat)
    ctx2d = ctx.reshape(b * lc, d)
    qkv_c, _, _, _, ba_c, bat_c = _inproj(ctx2d, mod_c, b * lc, ng[0:1], wqkv, wzuv, wba, wbat)

    alog = a_log[0].reshape(-1)
    dtb = dt_bias[0].reshape(-1)
    pc = _gdn_prep(qkv_c.reshape(b, lc, QKV_COLS), conv_w[0], ba_c.reshape(b, lc, N_GATE_COLS),
                   bat_c, alog, dtb, lc)
    s_zero = jnp.zeros((b, N_CHAINS, HEAD_DIM, HEAD_DIM), F32)
    _, _, s_ctx = _gdn_scan(*pc, s_zero)
    pp = _gdn_prep(qkv.reshape(b, l, QKV_COLS), conv_w[0], ba.reshape(b, l, N_GATE_COLS),
                   bat, alog, dtb, GRID_W)
    o_f, o_b, _ = _gdn_scan(*pp, s_ctx)

    b_s_full = jnp.broadcast_to(sgu_b[0][:, :, None], (SGU_GROUPS, SGU_CHUNK, SGU_GROUP))
    mix = _mix(o_f.reshape(t, GDN_W), o_b.reshape(t, GDN_W), z, ug, vg, gdn_norm_g,
               sgu_ln_g[0], sgu_ln_b[0], sgu_w[0], b_s_full)

    x1, h2p, logits_t = _outproj(mix, x2d, mod, l, ng, w_out[0].astype(BF16),
                                 router_w[0].T, router_b[0][:, None])

    eidx, gates_t, rank, counts = _route(logits_t)
    counts = counts[:, 0]
    n_blocks = -(-(t * TOP_K) // MOE_ROWS) + N_EXPERTS
    dest, pad_start, block_e, n_used = _slot_tables(eidx, rank, counts, n_blocks)
    xb = _sc_scatter_rows(h2p, dest, n_blocks * MOE_ROWS)
    xb = _zero_pad_slots(counts, pad_start, xb)
    yb = _moe_ffn(block_e, n_used, xb, w_gu[0], b_gu[0][:, None, :], w_down[0],
                  b_down[0][:, None, :])
    yg = _sc_gather_rows(yb, dest.reshape(-1))
    out = _combine(yg, gates_t.T, x1, mod, l, ng)
    return out.reshape(b, l, d)
```

```python
import functools
import math

import jax
import jax.numpy as jnp
from jax import lax
from jax.experimental import pallas as pl
from jax.experimental.pallas import tpu as pltpu
from jax.experimental.pallas import tpu_sc as plsc

F32 = jnp.float32
BF16 = jnp.bfloat16

D_MODEL = 1024
GDN_HEADS = 4
HEAD_DIM = 128
GDN_W = GDN_HEADS * HEAD_DIM
SGU_GROUPS = 4
SGU_GROUP = 128
SGU_W = SGU_GROUPS * SGU_GROUP
SGU_CHUNK = 128
DELTA_CHUNK = 64
GRID_W = 64
N_EXPERTS = 32
TOP_K = 4
D_FF = 1024
SWIGLU_LIMIT = 7.0
SWIGLU_ALPHA = 1.702
NORM_EPS = 1e-6
QKV_COLS = 3 * GDN_W
N_CHAINS = 2 * GDN_HEADS
N_GATE_COLS = 2 * N_CHAINS

ROW_TILE = 512
GDN_BLOCK = 256
CHUNKS_PER_BLOCK = GDN_BLOCK // DELTA_CHUNK
MOE_ROWS = 256
U32 = jnp.uint32
LANES = 128
PACKED_W = D_MODEL // 2
ROUTE_TILE = 512
SC_CORES = 2
SC_SUBCORES = 16
SC_WINDOW = 128
N_PLANES = 2
SC_ROW_W = PACKED_W // N_PLANES
VMEM_LIMIT = 56 * 1024 * 1024


def _params(*sem):
    return pltpu.CompilerParams(dimension_semantics=sem, vmem_limit_bytes=VMEM_LIMIT)


def _dot(a, b):
    return jnp.dot(a, b, preferred_element_type=F32)


def _dot_nt(a, b):
    return lax.dot_general(a, b, (((1,), (1,)), ((), ())), preferred_element_type=F32)


def _dot_tn(a, b):
    return lax.dot_general(a, b, (((0,), (0,)), ((), ())), preferred_element_type=F32)


def _split2(a):
    hi = a.astype(BF16)
    lo = (a - hi.astype(F32)).astype(BF16)
    return hi, lo


def _split3(a):
    hi = a.astype(BF16)
    r = a - hi.astype(F32)
    mid = r.astype(BF16)
    lo = (r - mid.astype(F32)).astype(BF16)
    return hi, mid, lo


def _pack_rows(x32):
    xb = x32.astype(BF16).astype(F32)
    hi = lax.bitcast_convert_type(xb[:, :PACKED_W], U32)
    lo = lax.bitcast_convert_type(xb[:, PACKED_W:], U32)
    return hi | (lo >> 16)


def _store_planes(ref, packed):
    for p in range(N_PLANES):
        ref[p] = packed[:, p * SC_ROW_W:(p + 1) * SC_ROW_W]


def _load_planes(ref):
    return jnp.concatenate([ref[p] for p in range(N_PLANES)], axis=1)


def _unpack_rows(w):
    hi = lax.bitcast_convert_type(w & jnp.uint32(0xFFFF0000), F32)
    lo = lax.bitcast_convert_type(w << 16, F32)
    return jnp.concatenate([hi, lo], axis=1)


def _rms(x32, g):
    return x32 * lax.rsqrt(jnp.mean(x32 * x32, -1, keepdims=True) + NORM_EPS) * g


def _gelu_tanh(x):
    c = math.sqrt(2.0 / math.pi)
    return 0.5 * x * (1.0 + jnp.tanh(c * (x + 0.044715 * (x * x * x))))


def _sigmoid(x):
    return 1.0 / (1.0 + jnp.exp(-x))


def _softplus(x):
    return jnp.maximum(x, 0.0) + jnp.log(1.0 + jnp.exp(-jnp.abs(x)))


def _ada_kernel(c_ref, w_ref, b_ref, o_ref):
    c = c_ref[...]
    s = c * _sigmoid(c)
    s_hi, s_lo = _split2(s)
    w_hi, w_lo = _split2(w_ref[...])
    o_ref[...] = _dot(s_hi, w_hi) + _dot(s_lo, w_hi) + _dot(s_hi, w_lo) + b_ref[...]


def _ada(cs, w_ada, b_ada):
    n = w_ada.shape[1]
    bn = D_MODEL
    return pl.pallas_call(
        _ada_kernel,
        out_shape=jax.ShapeDtypeStruct((cs.shape[0], n), F32),
        grid=(n // bn,),
        in_specs=[pl.BlockSpec(cs.shape, lambda j: (0, 0)),
                  pl.BlockSpec((D_MODEL, bn), lambda j: (0, j)),
                  pl.BlockSpec((1, bn), lambda j: (0, j))],
        out_specs=pl.BlockSpec((cs.shape[0], bn), lambda j: (0, j)),
        compiler_params=_params("parallel"),
        name="ada_mod",
    )(cs, w_ada, b_ada)


def _inproj_kernel(x_ref, mod_ref, g_ref, wqkv_ref, wzuv_ref, wba_ref, wbat_ref,
                   qkv_ref, z_ref, u_ref, v_ref, ba_ref, bat_ref):
    x = x_ref[...]
    mod = mod_ref[0]
    h = _rms(x, g_ref[...]) * (1.0 + mod[1:2]) + mod[0:1]
    h_hi, h_lo = _split2(h)
    qkv_ref[...] = _dot(h_hi, wqkv_ref[...]).astype(BF16)
    zuv = _dot(h_hi, wzuv_ref[...])
    z_ref[...] = zuv[:, :GDN_W].astype(BF16)
    u_ref[...] = _gelu_tanh(zuv[:, GDN_W:GDN_W + SGU_W]).astype(BF16)
    v_ref[...] = _gelu_tanh(zuv[:, GDN_W + SGU_W:]).astype(BF16)
    w_hi, w_lo = _split2(wba_ref[...])
    ba_ref[...] = _dot(h_hi, w_hi) + _dot(h_lo, w_hi) + _dot(h_hi, w_lo)
    wt_hi, wt_lo = _split2(wbat_ref[...])
    bat_ref[...] = _dot_nt(wt_hi, h_hi) + _dot_nt(wt_lo, h_hi) + _dot_nt(wt_hi, h_lo)


def _inproj(x2d, mod, rows_per_mod, ng0, wqkv, wzuv, wba, wbat):
    t = x2d.shape[0]
    tm = min(ROW_TILE, t)
    tiles_per_mod = rows_per_mod // tm
    row = lambda i: (i, 0)
    const = lambda i: (0, 0)
    return pl.pallas_call(
        _inproj_kernel,
        out_shape=(jax.ShapeDtypeStruct((t, QKV_COLS), BF16),
                   jax.ShapeDtypeStruct((t, GDN_W), BF16),
                   jax.ShapeDtypeStruct((t, SGU_W), BF16),
                   jax.ShapeDtypeStruct((t, SGU_W), BF16),
                   jax.ShapeDtypeStruct((t, N_GATE_COLS), F32),
                   jax.ShapeDtypeStruct((N_GATE_COLS, t), F32)),
        grid=(t // tm,),
        in_specs=[pl.BlockSpec((tm, D_MODEL), row),
                  pl.BlockSpec((1, 6, D_MODEL), lambda i: (i // tiles_per_mod, 0, 0)),
                  pl.BlockSpec((1, D_MODEL), const),
                  pl.BlockSpec(wqkv.shape, const),
                  pl.BlockSpec(wzuv.shape, const),
                  pl.BlockSpec(wba.shape, const),
                  pl.BlockSpec(wbat.shape, const)],
        out_specs=(pl.BlockSpec((tm, QKV_COLS), row),
                   pl.BlockSpec((tm, GDN_W), row),
                   pl.BlockSpec((tm, SGU_W), row),
                   pl.BlockSpec((tm, SGU_W), row),
                   pl.BlockSpec((tm, N_GATE_COLS), row),
                   pl.BlockSpec((N_GATE_COLS, tm), lambda i: (0, i))),
        compiler_params=_params("parallel"),
        name="in_proj",
    )(x2d, mod, ng0, wqkv, wzuv, wba, wbat)


def _gdn_prep_kernel(row_len, qkv_ref, cw_ref, ba_ref, bat_ref, alog_r_ref, dtb_r_ref,
                     alog_c_ref, dtb_c_ref, u_ref, w_ref, qd_ref, kd_ref, at_ref, gl_ref):
    n = GDN_BLOCK
    c = DELTA_CHUNK
    ri = lax.broadcasted_iota(jnp.int32, (n, n), 0)
    ci = lax.broadcasted_iota(jnp.int32, (n, n), 1)
    same = (ri // c) == (ci // c)
    lower = same & (ri >= ci)
    upper = same & (ri <= ci)

    def mask01(m):
        return jnp.where(m, 1.0, 0.0).astype(BF16)

    lower_b = mask01(lower)
    upper_b = mask01(upper)
    same_b = mask01(same)

    wi = lax.broadcasted_iota(jnp.int32, (c, n), 0)
    wl = lax.broadcasted_iota(jnp.int32, (c, n), 1)
    wchunk = wl // c
    wj = wl % c
    lower_w = wi >= wj
    upper_w = wi <= wj
    diag_w = wi == wj
    eye_w = jnp.where(diag_w, 1.0, 0.0)
    pair_masks = []
    s = 1
    while s < c:
        pair_masks.append(mask01(((wi // (2 * s)) == (wj // (2 * s))) & ((wi // s) != (wj // s))))
        s *= 2

    def to_wide(full):
        out = full[:c]
        for k in range(1, CHUNKS_PER_BLOCK):
            out = jnp.where(wchunk == k, full[k * c:(k + 1) * c], out)
        return out

    def col_wide(col):
        out = jnp.broadcast_to(col[:c], (c, n))
        for k in range(1, CHUNKS_PER_BLOCK):
            out = jnp.where(wchunk == k, jnp.broadcast_to(col[k * c:(k + 1) * c], (c, n)), out)
        return out

    def block_diag(x_w):
        return jnp.concatenate([x_w] * CHUNKS_PER_BLOCK, axis=0) * same_b

    ba = ba_ref[0]
    bat = bat_ref[...]
    beta_c = _sigmoid(ba[:, :N_CHAINS])
    g_c = -jnp.exp(alog_r_ref[...]) * _softplus(ba[:, N_CHAINS:] + dtb_r_ref[...])
    g_r = -jnp.exp(alog_c_ref[...]) * _softplus(bat[N_CHAINS:] + dtb_c_ref[...])
    gc3 = _split3(g_c)
    gr3 = jnp.concatenate(_split3(g_r), axis=0)

    def sum3_r(m):
        return m[:N_CHAINS] + m[N_CHAINS:2 * N_CHAINS] + m[2 * N_CHAINS:]

    cum_f_c = _dot(lower_b, gc3[0]) + _dot(lower_b, gc3[1]) + _dot(lower_b, gc3[2])
    tot_c = _dot(same_b, gc3[0]) + _dot(same_b, gc3[1]) + _dot(same_b, gc3[2])
    cum_b_c = tot_c - cum_f_c + g_c
    cum_f_r = sum3_r(_dot(gr3, upper_b))
    cum_b_r = sum3_r(_dot(gr3, lower_b))
    ei = lax.broadcasted_iota(jnp.int32, (n, CHUNKS_PER_BLOCK * HEAD_DIM), 0) // c
    ej = lax.broadcasted_iota(jnp.int32, (n, CHUNKS_PER_BLOCK * HEAD_DIM), 1) // HEAD_DIM
    g_last = jnp.exp(sum3_r(_dot(gr3, mask01(ei == ej))))
    gl_ref[0, 0, 0] = g_last[:GDN_HEADS]
    gl_ref[0, 0, 1] = g_last[GDN_HEADS:]

    pos = lax.broadcasted_iota(jnp.int32, (n, HEAD_DIM), 0) % row_len
    first = pos == 0
    last = pos == row_len - 1

    def conv_silu(col):
        x = qkv_ref[0, :, col * HEAD_DIM:(col + 1) * HEAD_DIM].astype(F32)
        cw = cw_ref[:, col * HEAD_DIM:(col + 1) * HEAD_DIM]
        xp = jnp.where(first, 0.0, pltpu.roll(x, 1, 0))
        xn = jnp.where(last, 0.0, pltpu.roll(x, n - 1, 0))
        y = xp * cw[0:1] + x * cw[1:2] + xn * cw[2:3]
        return y * _sigmoid(y)

    def l2n(x):
        return x * lax.rsqrt(jnp.sum(x * x, -1, keepdims=True) + NORM_EPS)

    a_bs, ps, rhss = [None] * N_CHAINS, [None] * N_CHAINS, [None] * N_CHAINS
    for h in range(GDN_HEADS):
        q = l2n(conv_silu(h)) * (HEAD_DIM ** -0.5)
        k = l2n(conv_silu(GDN_HEADS + h))
        v = conv_silu(2 * GDN_HEADS + h)
        k_b = k.astype(BF16)
        qk_kk = _dot_nt(jnp.concatenate([q.astype(BF16), k_b], axis=0), k_b)
        qk_w = to_wide(qk_kk[:n])
        kk_w = to_wide(qk_kk[n:])
        for d in range(2):
            j = d * GDN_HEADS + h
            mask_w = lower_w if d == 0 else upper_w
            cum_c = (cum_f_c if d == 0 else cum_b_c)[:, j:j + 1]
            cum_r = (cum_f_r if d == 0 else cum_b_r)[j:j + 1, :]
            b_c = beta_c[:, j:j + 1]
            decay_w = jnp.where(mask_w, jnp.exp(jnp.where(mask_w, col_wide(cum_c) - cum_r, 0.0)), 0.0)
            amat_w = jnp.where(diag_w, 0.0, kk_w * decay_w * col_wide(b_c))
            a_bs[j] = amat_w.astype(BF16)
            ps[j] = eye_w - amat_w * pair_masks[0].astype(F32)
            e_c = jnp.exp(cum_c)
            rhss[j] = jnp.concatenate([(v * b_c).astype(BF16), (k * (b_c * e_c)).astype(BF16)], axis=1)
            cols = slice(j * HEAD_DIM, (j + 1) * HEAD_DIM)
            qd_ref[0, :, cols] = (q * e_c).astype(BF16)
            kd_ref[0, :, cols] = (k * jnp.exp(tot_c[:, j:j + 1] - cum_c)).astype(BF16)
            at_ref[0, 0, j * c:(j + 1) * c, :] = (qk_w * decay_w).astype(BF16)

    for pm in pair_masks[1:]:
        p_bs = [p.astype(BF16) for p in ps]
        ys = [_dot(a_bs[j] * pm, block_diag(p_bs[j])) for j in range(N_CHAINS)]
        ps = [ps[j] - _dot(p_bs[j], block_diag(ys[j].astype(BF16))) for j in range(N_CHAINS)]

    for j in range(N_CHAINS):
        uw = _dot(block_diag(ps[j].astype(BF16)), rhss[j])
        cols = slice(j * HEAD_DIM, (j + 1) * HEAD_DIM)
        u_ref[0, :, cols] = uw[:, :HEAD_DIM].astype(BF16)
        w_ref[0, :, cols] = uw[:, HEAD_DIM:].astype(BF16)


def _gdn_prep(qkv, conv_w, ba, bat, alog, dtb, row_len):
    b, l, _ = qkv.shape
    nblk = l // GDN_BLOCK
    wide = N_CHAINS * HEAD_DIM
    blk = lambda bi, i: (bi, i, 0)
    const = lambda bi, i: (0, 0)
    alog_r, dtb_r = alog.reshape(1, N_CHAINS), dtb.reshape(1, N_CHAINS)
    alog_c, dtb_c = alog.reshape(N_CHAINS, 1), dtb.reshape(N_CHAINS, 1)
    return pl.pallas_call(
        functools.partial(_gdn_prep_kernel, row_len),
        out_shape=(jax.ShapeDtypeStruct((b, l, wide), BF16),) * 4 + (
            jax.ShapeDtypeStruct((b, nblk, N_CHAINS * DELTA_CHUNK, GDN_BLOCK), BF16),
            jax.ShapeDtypeStruct((b, nblk, 2, GDN_HEADS, CHUNKS_PER_BLOCK * HEAD_DIM), F32)),
        grid=(b, nblk),
        in_specs=[pl.BlockSpec((1, GDN_BLOCK, QKV_COLS), blk),
                  pl.BlockSpec((3, QKV_COLS), const),
                  pl.BlockSpec((1, GDN_BLOCK, N_GATE_COLS), blk),
                  pl.BlockSpec((N_GATE_COLS, GDN_BLOCK), lambda bi, i: (0, bi * nblk + i)),
                  pl.BlockSpec((1, N_CHAINS), const),
                  pl.BlockSpec((1, N_CHAINS), const),
                  pl.BlockSpec((N_CHAINS, 1), const),
                  pl.BlockSpec((N_CHAINS, 1), const)],
        out_specs=(pl.BlockSpec((1, GDN_BLOCK, wide), blk),) * 4 + (
            pl.BlockSpec((1, 1, N_CHAINS * DELTA_CHUNK, GDN_BLOCK), lambda bi, i: (bi, i, 0, 0)),
            pl.BlockSpec((1, 1, 2, GDN_HEADS, CHUNKS_PER_BLOCK * HEAD_DIM),
                         lambda bi, i: (bi, i, 0, 0, 0))),
        compiler_params=_params("parallel", "parallel"),
        name="gdn_prep",
    )(qkv, conv_w, ba, bat, alog_r, dtb_r, alog_c, dtb_c)


def _gdn_scan_kernel(uf, wf, qf, kf, af, gf, ub, wb, qb, kb, ab, gb, s0_ref,
                     of_ref, ob_ref, sfin_ref, s_scr):
    i = pl.program_id(1)
    c = DELTA_CHUNK

    @pl.when(i == 0)
    def _():
        s_scr[...] = s0_ref[0]

    ops = ((uf, wf, qf, kf, af, gf, of_ref), (ub, wb, qb, kb, ab, gb, ob_ref))
    chains = [(d, h) for d in range(2) for h in range(GDN_HEADS)]
    states = [s_scr[j] for j in range(N_CHAINS)]
    for step in range(CHUNKS_PER_BLOCK):
        def chunk(d):
            cc = step if d == 0 else CHUNKS_PER_BLOCK - 1 - step
            return cc, slice(cc * c, (cc + 1) * c)

        xs = []
        for j, (d, h) in enumerate(chains):
            _, rows = chunk(d)
            cols = slice(h * HEAD_DIM, (h + 1) * HEAD_DIM)
            wq = jnp.concatenate([ops[d][1][0, rows, cols], ops[d][2][0, rows, cols]], axis=0)
            xs.append(_dot(wq, states[j].astype(BF16)))
        v_news = []
        for j, (d, h) in enumerate(chains):
            _, rows = chunk(d)
            cols = slice(h * HEAD_DIM, (h + 1) * HEAD_DIM)
            v_news.append((ops[d][0][0, rows, cols].astype(F32) - xs[j][:c]).astype(BF16))
        for j, (d, h) in enumerate(chains):
            cc, rows = chunk(d)
            cols = slice(h * HEAD_DIM, (h + 1) * HEAD_DIM)
            a_c = ops[d][4][0, 0, h * c:(h + 1) * c, cc * c:(cc + 1) * c]
            ops[d][6][0, rows, cols] = xs[j][c:] + _dot(a_c, v_news[j])
            ds = _dot_tn(ops[d][3][0, rows, cols], v_news[j])
            g_last = ops[d][5][0, 0, 0, h:h + 1, cc * HEAD_DIM:(cc + 1) * HEAD_DIM]
            states[j] = states[j] * g_last + ds
    for j in range(N_CHAINS):
        s_scr[j] = states[j]

    @pl.when(i == pl.num_programs(1) - 1)
    def _():
        sfin_ref[0] = s_scr[...]


def _gdn_scan(u, w, qd, kd, at, gl, s0):
    b, l, _ = u.shape
    nblk = l // GDN_BLOCK
    half = GDN_HEADS * HEAD_DIM
    fwd = lambda bi, i: (bi, i, 0)
    bwd = lambda bi, i: (bi, nblk - 1 - i, 1)
    big = lambda m: pl.BlockSpec((1, GDN_BLOCK, half), m)
    att_shape = (1, 1, GDN_HEADS * DELTA_CHUNK, GDN_BLOCK)
    attf = pl.BlockSpec(att_shape, lambda bi, i: (bi, i, 0, 0))
    attb = pl.BlockSpec(att_shape, lambda bi, i: (bi, nblk - 1 - i, 1, 0))
    gl_shape = (1, 1, 1, GDN_HEADS, CHUNKS_PER_BLOCK * HEAD_DIM)
    glf = pl.BlockSpec(gl_shape, lambda bi, i: (bi, i, 0, 0, 0))
    glb = pl.BlockSpec(gl_shape, lambda bi, i: (bi, nblk - 1 - i, 1, 0, 0))
    state = pl.BlockSpec((1, N_CHAINS, HEAD_DIM, HEAD_DIM), lambda bi, i: (bi, 0, 0, 0))
    return pl.pallas_call(
        _gdn_scan_kernel,
        out_shape=(jax.ShapeDtypeStruct((b, l, half), F32),
                   jax.ShapeDtypeStruct((b, l, half), F32),
                   jax.ShapeDtypeStruct((b, N_CHAINS, HEAD_DIM, HEAD_DIM), F32)),
        grid=(b, nblk),
        in_specs=[big(fwd), big(fwd), big(fwd), big(fwd), attf, glf,
                  big(bwd), big(bwd), big(bwd), big(bwd), attb, glb, state],
        out_specs=(pl.BlockSpec((1, GDN_BLOCK, half), fwd),
                   pl.BlockSpec((1, GDN_BLOCK, half), lambda bi, i: (bi, nblk - 1 - i, 0)),
                   state),
        scratch_shapes=[pltpu.VMEM((N_CHAINS, HEAD_DIM, HEAD_DIM), F32)],
        compiler_params=_params("parallel", "arbitrary"),
        name="gdn_scan",
    )(u, w, qd, kd, at, gl, u, w, qd, kd, at, gl, s0)


def _mix_kernel(of_ref, ob_ref, z_ref, u_ref, v_ref, gng_ref, lng_ref, lnb_ref, ws_ref, bs_ref,
                mix_ref):
    tm = of_ref.shape[0]
    o = of_ref[...] + ob_ref[...]
    z = z_ref[...].astype(F32)
    for h in range(GDN_HEADS):
        cols = slice(h * HEAD_DIM, (h + 1) * HEAD_DIM)
        oh = o[:, cols]
        zh = z[:, cols]
        r = lax.rsqrt(jnp.mean(oh * oh, -1, keepdims=True) + NORM_EPS)
        mix_ref[:, cols] = (oh * r * gng_ref[...] * (zh * _sigmoid(zh))).astype(BF16)
    for g in range(SGU_GROUPS):
        cols = slice(g * SGU_GROUP, (g + 1) * SGU_GROUP)
        vg = v_ref[:, cols].astype(F32)
        vc = vg - jnp.mean(vg, -1, keepdims=True)
        vn = vc * lax.rsqrt(jnp.mean(vc * vc, -1, keepdims=True) + NORM_EPS)
        vn = (vn * lng_ref[g:g + 1] + lnb_ref[g:g + 1]).astype(BF16)
        wsg = ws_ref[g].astype(BF16)
        for n in range(tm // SGU_CHUNK):
            rows = slice(n * SGU_CHUNK, (n + 1) * SGU_CHUNK)
            s = _dot(wsg, vn[rows]) + bs_ref[g]
            mix_ref[rows, GDN_W + g * SGU_GROUP:GDN_W + (g + 1) * SGU_GROUP] = (
                u_ref[rows, cols].astype(F32) * s).astype(BF16)


def _mix(o_f, o_b, z, ug, vg, gdn_norm_g, ln_g, ln_b, w_s, b_s_full):
    t = z.shape[0]
    tm = min(ROW_TILE, t)
    row = lambda i: (i, 0)
    c2 = lambda i: (0, 0)
    c3 = lambda i: (0, 0, 0)
    return pl.pallas_call(
        _mix_kernel,
        out_shape=jax.ShapeDtypeStruct((t, D_MODEL), BF16),
        grid=(t // tm,),
        in_specs=[pl.BlockSpec((tm, GDN_W), row), pl.BlockSpec((tm, GDN_W), row),
                  pl.BlockSpec((tm, GDN_W), row), pl.BlockSpec((tm, SGU_W), row),
                  pl.BlockSpec((tm, SGU_W), row),
                  pl.BlockSpec((1, HEAD_DIM), c2),
                  pl.BlockSpec((SGU_GROUPS, SGU_GROUP), c2),
                  pl.BlockSpec((SGU_GROUPS, SGU_GROUP), c2),
                  pl.BlockSpec((SGU_GROUPS, SGU_CHUNK, SGU_CHUNK), c3),
                  pl.BlockSpec((SGU_GROUPS, SGU_CHUNK, SGU_GROUP), c3)],
        out_specs=pl.BlockSpec((tm, D_MODEL), row),
        compiler_params=_params("parallel"),
        name="mixer_out",
    )(o_f, o_b, z, ug, vg, gdn_norm_g, ln_g, ln_b, w_s, b_s_full)


def _outproj_kernel(mix_ref, x_ref, mod_ref, ng_ref, wout_ref, rwt_ref, rb_ref,
                    x1_ref, h2_ref, lt_ref):
    mod = mod_ref[0]
    y = _dot(mix_ref[...], wout_ref[...])
    x1 = x_ref[...] + mod[2:3] * _rms(y, ng_ref[1:2])
    x1_ref[...] = x1
    h2 = _rms(x1, ng_ref[2:3]) * (1.0 + mod[4:5]) + mod[3:4]
    h_hi, h_lo = _split2(h2)
    _store_planes(h2_ref, _pack_rows(h2))
    w_hi, w_lo = _split2(rwt_ref[...])
    lt_ref[...] = _dot_nt(w_hi, h_hi) + _dot_nt(w_lo, h_hi) + _dot_nt(w_hi, h_lo) + rb_ref[...]


def _outproj(mix, x2d, mod, rows_per_mod, ng, wout, rwt, rb_col):
    t = x2d.shape[0]
    tm = min(ROW_TILE, t)
    tiles_per_mod = rows_per_mod // tm
    row = lambda i: (i, 0)
    const = lambda i: (0, 0)
    return pl.pallas_call(
        _outproj_kernel,
        out_shape=(jax.ShapeDtypeStruct((t, D_MODEL), F32),
                   jax.ShapeDtypeStruct((N_PLANES, t, SC_ROW_W), U32),
                   jax.ShapeDtypeStruct((N_EXPERTS, t), F32)),
        grid=(t // tm,),
        in_specs=[pl.BlockSpec((tm, D_MODEL), row),
                  pl.BlockSpec((tm, D_MODEL), row),
                  pl.BlockSpec((1, 6, D_MODEL), lambda i: (i // tiles_per_mod, 0, 0)),
                  pl.BlockSpec((4, D_MODEL), const),
                  pl.BlockSpec((D_MODEL, D_MODEL), const),
                  pl.BlockSpec((N_EXPERTS, D_MODEL), const),
                  pl.BlockSpec((N_EXPERTS, 1), const)],
        out_specs=(pl.BlockSpec((tm, D_MODEL), row),
                   pl.BlockSpec((N_PLANES, tm, SC_ROW_W), lambda i: (0, i, 0)),
                   pl.BlockSpec((N_EXPERTS, tm), lambda i: (0, i))),
        compiler_params=_params("parallel"),
        name="out_proj_router",
    )(mix, x2d, mod, ng, wout, rwt, rb_col)


def _moe_kernel(be_ref, nb_ref, xb_ref, wgu_ref, bgu_ref, wd_ref, bd_ref, y_ref,
                wgu_b, wd_b, gut_scr):
    i = pl.program_id(0)
    live = i < nb_ref[0]
    new_expert = jnp.logical_or(i == 0, be_ref[i] != be_ref[jnp.maximum(i - 1, 0)])

    @pl.when(jnp.logical_and(live, new_expert))
    def _():
        wgu_b[...] = wgu_ref[0].astype(BF16)
        wd_b[...] = wd_ref[0].astype(BF16)

    @pl.when(live)
    def _():
        xb = _unpack_rows(_load_planes(xb_ref)).astype(BF16)
        gu = _dot(xb, wgu_b[...]) + bgu_ref[0]
        gu_t = gu.T
        acts = []
        for part in range(MOE_ROWS // LANES):
            part_ref = gut_scr.at[part]
            part_ref[...] = gu_t[:, part * LANES:(part + 1) * LANES]
            gate = jnp.minimum(part_ref[pl.ds(0, D_FF, stride=2), :], SWIGLU_LIMIT)
            up = jnp.clip(part_ref[pl.ds(1, D_FF, stride=2), :], -SWIGLU_LIMIT, SWIGLU_LIMIT)
            acts.append(((up + 1.0) * gate * _sigmoid(SWIGLU_ALPHA * gate)).astype(BF16))
        act_t = jnp.concatenate(acts, axis=1)
        _store_planes(y_ref, _pack_rows(_dot_tn(act_t, wd_b[...]) + bd_ref[0]))

    @pl.when(jnp.logical_not(live))
    def _():
        y_ref[...] = jnp.zeros_like(y_ref)


def _moe_ffn(block_e, n_used, xb, w_gu, b_gu, w_down, b_down):
    n_rows = xb.shape[1]
    n_blocks = n_rows // MOE_ROWS
    row = lambda i, be, nb: (0, i, 0)
    ex3 = lambda i, be, nb: (be[i], 0, 0)
    live_row = lambda i, be, nb: (0, jnp.minimum(i, nb[0] - 1), 0)
    planes = (N_PLANES, MOE_ROWS, SC_ROW_W)
    grid_spec = pltpu.PrefetchScalarGridSpec(
        num_scalar_prefetch=2,
        grid=(n_blocks,),
        in_specs=[pl.BlockSpec(planes, live_row),
                  pl.BlockSpec((1, D_MODEL, 2 * D_FF), ex3),
                  pl.BlockSpec((1, 1, 2 * D_FF), ex3),
                  pl.BlockSpec((1, D_FF, D_MODEL), ex3),
                  pl.BlockSpec((1, 1, D_MODEL), ex3)],
        out_specs=pl.BlockSpec(planes, row),
        scratch_shapes=[pltpu.VMEM((D_MODEL, 2 * D_FF), BF16),
                        pltpu.VMEM((D_FF, D_MODEL), BF16),
                        pltpu.VMEM((MOE_ROWS // LANES, 2 * D_FF, LANES), F32)],
    )
    return pl.pallas_call(
        _moe_kernel,
        out_shape=jax.ShapeDtypeStruct((N_PLANES, n_rows, SC_ROW_W), U32),
        grid_spec=grid_spec,
        compiler_params=_params("arbitrary"),
        name="moe_ffn",
    )(block_e, n_used, xb, w_gu, b_gu, w_down, b_down)


def _combine_kernel(y0_ref, y1_ref, y2_ref, y3_ref, gt_ref, x1_ref, mod_ref, ng_ref, o_ref):
    mod = mod_ref[0]
    gt = gt_ref[...]
    y = _unpack_rows(_load_planes(y0_ref)) * gt[:, 0:1]
    for k, y_ref in ((1, y1_ref), (2, y2_ref), (3, y3_ref)):
        y = y + _unpack_rows(_load_planes(y_ref)) * gt[:, k:k + 1]
    o_ref[...] = x1_ref[...] + mod[5:6] * _rms(y, ng_ref[3:4])


def _combine(yg, gates, x1, mod, rows_per_mod, ng):
    t = x1.shape[0]
    tm = min(ROW_TILE, t)
    tiles_per_mod = rows_per_mod // tm
    n_tiles = t // tm
    row = lambda i: (i, 0)
    choice = lambda k: pl.BlockSpec((N_PLANES, tm, SC_ROW_W), lambda i: (0, k * n_tiles + i, 0))
    return pl.pallas_call(
        _combine_kernel,
        out_shape=jax.ShapeDtypeStruct((t, D_MODEL), F32),
        grid=(n_tiles,),
        in_specs=[choice(0), choice(1), choice(2), choice(3),
                  pl.BlockSpec((tm, TOP_K), row),
                  pl.BlockSpec((tm, D_MODEL), row),
                  pl.BlockSpec((1, 6, D_MODEL), lambda i: (i // tiles_per_mod, 0, 0)),
                  pl.BlockSpec((4, D_MODEL), lambda i: (0, 0))],
        out_specs=pl.BlockSpec((tm, D_MODEL), row),
        compiler_params=_params("parallel"),
        name="moe_combine",
    )(yg, yg, yg, yg, gates, x1, mod, ng)


def _route_kernel(lt_ref, eidx_ref, gate_ref, rank_ref, cnt_ref, carry):
    i = pl.program_id(0)
    tile = lt_ref.shape[1]

    @pl.when(i == 0)
    def _():
        carry[...] = jnp.zeros_like(carry)

    logits = lt_ref[...]
    eio = lax.broadcasted_iota(jnp.int32, (N_EXPERTS, tile), 0).astype(F32)
    vals, sels = [], []
    for k in range(TOP_K):
        m = jnp.max(logits, axis=0, keepdims=True)
        idx = jnp.min(jnp.where(logits == m, eio, float(N_EXPERTS)), axis=0, keepdims=True)
        sel = eio == idx
        logits = jnp.where(sel, -jnp.inf, logits)
        vals.append(m)
        sels.append(sel)
        eidx_ref[k:k + 1, :] = idx.astype(jnp.int32)
    exps = [jnp.exp(v - vals[0]) for v in vals]
    denom = exps[0] + exps[1] + exps[2] + exps[3]
    for k in range(TOP_K):
        gate_ref[k:k + 1, :] = exps[k] / denom

    member = jnp.where(sels[0] | sels[1] | sels[2] | sels[3], 1.0, 0.0)
    ti = lax.broadcasted_iota(jnp.int32, (tile, tile), 0)
    tj = lax.broadcasted_iota(jnp.int32, (tile, tile), 1)
    earlier = jnp.where(ti < tj, 1.0, 0.0).astype(BF16)
    before = _dot(member.astype(BF16), earlier) + carry[...]
    for k in range(TOP_K):
        rank_ref[k:k + 1, :] = jnp.sum(jnp.where(sels[k], before, 0.0), axis=0,
                                       keepdims=True).astype(jnp.int32)
    carry[...] = carry[...] + jnp.sum(member, axis=1, keepdims=True)
    cnt_ref[...] = carry[...].astype(jnp.int32)


def _route(logits_t):
    t = logits_t.shape[1]
    tile = min(ROUTE_TILE, t)
    blk = lambda i: (0, i)
    return pl.pallas_call(
        _route_kernel,
        out_shape=(jax.ShapeDtypeStruct((TOP_K, t), jnp.int32),
                   jax.ShapeDtypeStruct((TOP_K, t), F32),
                   jax.ShapeDtypeStruct((TOP_K, t), jnp.int32),
                   jax.ShapeDtypeStruct((N_EXPERTS, 1), jnp.int32)),
        grid=(t // tile,),
        in_specs=[pl.BlockSpec((N_EXPERTS, tile), blk)],
        out_specs=(pl.BlockSpec((TOP_K, tile), blk), pl.BlockSpec((TOP_K, tile), blk),
                   pl.BlockSpec((TOP_K, tile), blk),
                   pl.BlockSpec((N_EXPERTS, 1), lambda i: (0, 0))),
        scratch_shapes=[pltpu.VMEM((N_EXPERTS, 1), F32)],
        compiler_params=_params("arbitrary"),
        name="moe_route",
    )(logits_t)


def _slot_tables(eidx, rank, counts, n_blocks):
    padded = (counts + MOE_ROWS - 1) // MOE_ROWS * MOE_ROWS
    pad_end = jnp.cumsum(padded)
    pad_start = pad_end - padded
    experts = jnp.arange(N_EXPERTS, dtype=jnp.int32)
    dest = rank + jnp.sum(jnp.where(eidx[..., None] == experts, pad_start, 0), axis=-1)
    first_row = jnp.arange(n_blocks, dtype=jnp.int32)[:, None] * MOE_ROWS
    block_e = jnp.minimum(jnp.sum((pad_end[None, :] <= first_row).astype(jnp.int32), axis=1),
                          N_EXPERTS - 1)
    n_used = pad_end[-1:] // MOE_ROWS
    return dest.astype(jnp.int32), pad_start.astype(jnp.int32), block_e, n_used.astype(jnp.int32)


def _sc_mesh():
    return plsc.VectorSubcoreMesh(core_axis_name="c", subcore_axis_name="s",
                                  num_cores=SC_CORES, num_subcores=SC_SUBCORES)


def _plane_row_ids(rows, rows_per_plane):
    return jnp.concatenate([rows + p * rows_per_plane for p in range(N_PLANES)], axis=-1)


def _sc_gather_rows(table, rows):
    v = table.shape[1]
    idx = _plane_row_ids(rows, v)[None]
    n_all = idx.shape[1]

    @functools.partial(pl.kernel, mesh=_sc_mesh(), name="moe_gather_rows",
                       out_type=jax.ShapeDtypeStruct((n_all, SC_ROW_W), U32))
    def gather(x_hbm, i_hbm, o_hbm):
        def body(i_vmem, o_vmem):
            pltpu.sync_copy(x_hbm.at[i_vmem.at[0]], o_vmem)

        pltpu.emit_pipeline(
            body, grid=(n_all // SC_WINDOW,),
            in_specs=[pl.BlockSpec((1, SC_WINDOW), lambda i: (0, i))],
            out_specs=[pl.BlockSpec((SC_WINDOW, SC_ROW_W), lambda i: (i, 0))],
            core_axis_name=("c", "s"), dimension_semantics=(pltpu.PARALLEL,),
        )(i_hbm, o_hbm)

    return gather(table.reshape(N_PLANES * v, SC_ROW_W), idx).reshape(N_PLANES, -1, SC_ROW_W)


def _sc_scatter_rows(rows, dest, n_out):
    t = rows.shape[1]
    idx = _plane_row_ids(dest, n_out)

    @functools.partial(pl.kernel, mesh=_sc_mesh(), name="moe_scatter_rows", scratch_types=[],
                       out_type=jax.ShapeDtypeStruct((N_PLANES * n_out, SC_ROW_W), U32))
    def scatter(x_hbm, i_hbm, o_hbm):
        def body(x_vmem, i_vmem):
            for k in range(TOP_K):
                pltpu.sync_copy(x_vmem, o_hbm.at[i_vmem.at[k]])

        pltpu.emit_pipeline(
            body, grid=(N_PLANES * t // SC_WINDOW,),
            in_specs=[pl.BlockSpec((SC_WINDOW, SC_ROW_W), lambda i: (i, 0)),
                      pl.BlockSpec((TOP_K, SC_WINDOW), lambda i: (0, i))],
            out_specs=[],
            core_axis_name=("c", "s"), dimension_semantics=(pltpu.PARALLEL,),
        )(x_hbm, i_hbm)

    return scatter(rows.reshape(N_PLANES * t, SC_ROW_W), idx).reshape(N_PLANES, n_out, SC_ROW_W)


def _zero_pad_kernel(cnt_ref, start_ref, xb_in_ref, xb_ref, zero_scr, sem):
    del xb_in_ref
    zero_scr[...] = jnp.zeros_like(zero_scr)

    def zero_row_copy(p, row):
        return pltpu.make_async_copy(zero_scr.at[pl.ds(0, 1)], xb_ref.at[p, pl.ds(row, 1)], sem)

    def per_expert(e, carry):
        n_real = cnt_ref[e]
        first_pad = start_ref[e] + n_real
        n_pad = (MOE_ROWS - n_real % MOE_ROWS) % MOE_ROWS

        def issue(r, c):
            for p in range(N_PLANES):
                zero_row_copy(p, first_pad + r).start()
            return c

        def drain(r, c):
            for p in range(N_PLANES):
                zero_row_copy(p, first_pad).wait()
            return c
        lax.fori_loop(0, n_pad, issue, 0)
        lax.fori_loop(0, n_pad, drain, 0)
        return carry
    lax.fori_loop(0, N_EXPERTS, per_expert, 0)


def _zero_pad_slots(counts, pad_start, xb):
    grid_spec = pltpu.PrefetchScalarGridSpec(
        num_scalar_prefetch=2,
        grid=(1,),
        in_specs=[pl.BlockSpec(memory_space=pl.ANY)],
        out_specs=pl.BlockSpec(memory_space=pl.ANY),
        scratch_shapes=[pltpu.VMEM((8, SC_ROW_W), U32), pltpu.SemaphoreType.DMA],
    )
    return pl.pallas_call(
        _zero_pad_kernel,
        out_shape=jax.ShapeDtypeStruct(xb.shape, xb.dtype),
        grid_spec=grid_spec,
        input_output_aliases={2: 0},
        compiler_params=_params("arbitrary"),
        name="moe_zero_pad",
    )(counts, pad_start, xb)


def kernel(x, c, ctx, c_ctx, w_ada, b_ada, norm_g, w_in, conv_w, a_log, dt_bias, gdn_norm_g,
           sgu_ln_g, sgu_ln_b, sgu_w, sgu_b, w_out, router_w, router_b, w_gu, b_gu, w_down, b_down):
    b, l, d = x.shape
    lc = ctx.shape[1]
    t = b * l
    assert d == D_MODEL and l % ROW_TILE == 0 and l % GDN_BLOCK == 0 and lc % GDN_BLOCK == 0
    assert w_ada.shape[0] == 1, "single-layer block"

    cs = jnp.concatenate([c, c_ctx[None], jnp.zeros((8 - b - 1, d), F32)], axis=0)
    mod_all = _ada(cs, w_ada[0], b_ada[0][None])
    mod = mod_all[:b].reshape(b, 6, d)
    mod_c = mod_all[b:b + 1].reshape(1, 6, d)
    ng = norm_g[0]

    w = w_in[0]
    wqkv = w[:, :QKV_COLS].astype(BF16)
    wba = w[:, QKV_COLS:QKV_COLS + N_GATE_COLS]
    wzuv = w[:, QKV_COLS + N_GATE_COLS:].astype(BF16)
    wbat = wba.T

    x2d = x.reshape(t, d)
    qkv, z, ug, vg, ba, bat = _inproj(x2d, mod, l, ng[0:1], wqkv, wzuv, wba, wbat)
    ctx2d = ctx.reshape(b * lc, d)
    qkv_c, _, _, _, ba_c, bat_c = _inproj(ctx2d, mod_c, b * lc, ng[0:1], wqkv, wzuv, wba, wbat)

    alog = a_log[0].reshape(-1)
    dtb = dt_bias[0].reshape(-1)
    pc = _gdn_prep(qkv_c.reshape(b, lc, QKV_COLS), conv_w[0], ba_c.reshape(b, lc, N_GATE_COLS),
                   bat_c, alog, dtb, lc)
    s_zero = jnp.zeros((b, N_CHAINS, HEAD_DIM, HEAD_DIM), F32)
    _, _, s_ctx = _gdn_scan(*pc, s_zero)
    pp = _gdn_prep(qkv.reshape(b, l, QKV_COLS), conv_w[0], ba.reshape(b, l, N_GATE_COLS),
                   bat, alog, dtb, GRID_W)
    o_f, o_b, _ = _gdn_scan(*pp, s_ctx)

    b_s_full = jnp.broadcast_to(sgu_b[0][:, :, None], (SGU_GROUPS, SGU_CHUNK, SGU_GROUP))
    mix = _mix(o_f.reshape(t, GDN_W), o_b.reshape(t, GDN_W), z, ug, vg, gdn_norm_g,
               sgu_ln_g[0], sgu_ln_b[0], sgu_w[0], b_s_full)

    x1, h2p, logits_t = _outproj(mix, x2d, mod, l, ng, w_out[0].astype(BF16),
                                 router_w[0].T, router_b[0][:, None])

    eidx, gates_t, rank, counts = _route(logits_t)
    counts = counts[:, 0]
    n_blocks = -(-(t * TOP_K) // MOE_ROWS) + N_EXPERTS
    dest, pad_start, block_e, n_used = _slot_tables(eidx, rank, counts, n_blocks)
    xb = _sc_scatter_rows(h2p, dest, n_blocks * MOE_ROWS)
    xb = _zero_pad_slots(counts, pad_start, xb)
    yb = _moe_ffn(block_e, n_used, xb, w_gu[0], b_gu[0][:, None, :], w_down[0],
                  b_down[0][:, None, :])
    yg = _sc_gather_rows(yb, dest.reshape(-1))
    out = _combine(yg, gates_t.T, x1, mod, l, ng)
    return out.reshape(b, l, d)
```

```python
import functools
import math

import jax
import jax.numpy as jnp
from jax import lax
from jax.experimental import pallas as pl
from jax.experimental.pallas import tpu as pltpu
from jax.experimental.pallas import tpu_sc as plsc

F32 = jnp.float32
BF16 = jnp.bfloat16

D_MODEL = 1024
GDN_HEADS = 4
HEAD_DIM = 128
GDN_W = GDN_HEADS * HEAD_DIM
SGU_GROUPS = 4
SGU_GROUP = 128
SGU_W = SGU_GROUPS * SGU_GROUP
SGU_CHUNK = 128
DELTA_CHUNK = 64
GRID_W = 64
N_EXPERTS = 32
TOP_K = 4
D_FF = 1024
SWIGLU_LIMIT = 7.0
SWIGLU_ALPHA = 1.702
NORM_EPS = 1e-6
QKV_COLS = 3 * GDN_W
N_CHAINS = 2 * GDN_HEADS
N_GATE_COLS = 2 * N_CHAINS

ROW_TILE = 512
GDN_BLOCK = 256
CHUNKS_PER_BLOCK = GDN_BLOCK // DELTA_CHUNK
MOE_ROWS = 256
U32 = jnp.uint32
LANES = 128
PACKED_W = D_MODEL // 2
ROUTE_TILE = 512
SC_CORES = 2
SC_SUBCORES = 16
SC_WINDOW = 128
N_PLANES = 2
SC_ROW_W = PACKED_W // N_PLANES
VMEM_LIMIT = 56 * 1024 * 1024


def _params(*sem):
    return pltpu.CompilerParams(dimension_semantics=sem, vmem_limit_bytes=VMEM_LIMIT)


def _dot(a, b):
    return jnp.dot(a, b, preferred_element_type=F32)


def _dot_nt(a, b):
    return lax.dot_general(a, b, (((1,), (1,)), ((), ())), preferred_element_type=F32)


def _dot_tn(a, b):
    return lax.dot_general(a, b, (((0,), (0,)), ((), ())), preferred_element_type=F32)


def _split2(a):
    hi = a.astype(BF16)
    lo = (a - hi.astype(F32)).astype(BF16)
    return hi, lo


def _split3(a):
    hi = a.astype(BF16)
    r = a - hi.astype(F32)
    mid = r.astype(BF16)
    lo = (r - mid.astype(F32)).astype(BF16)
    return hi, mid, lo


def _pack_rows(x32):
    xb = x32.astype(BF16).astype(F32)
    hi = lax.bitcast_convert_type(xb[:, :PACKED_W], U32)
    lo = lax.bitcast_convert_type(xb[:, PACKED_W:], U32)
    return hi | (lo >> 16)


def _store_planes(ref, packed):
    for p in range(N_PLANES):
        ref[p] = packed[:, p * SC_ROW_W:(p + 1) * SC_ROW_W]


def _load_planes(ref):
    return jnp.concatenate([ref[p] for p in range(N_PLANES)], axis=1)


def _unpack_rows(w):
    hi = lax.bitcast_convert_type(w & jnp.uint32(0xFFFF0000), F32)
    lo = lax.bitcast_convert_type(w << 16, F32)
    return jnp.concatenate([hi, lo], axis=1)


def _rms(x32, g):
    return x32 * lax.rsqrt(jnp.mean(x32 * x32, -1, keepdims=True) + NORM_EPS) * g


def _gelu_tanh(x):
    c = math.sqrt(2.0 / math.pi)
    return 0.5 * x * (1.0 + jnp.tanh(c * (x + 0.044715 * (x * x * x))))


def _sigmoid(x):
    return 1.0 / (1.0 + jnp.exp(-x))


def _softplus(x):
    return jnp.maximum(x, 0.0) + jnp.log(1.0 + jnp.exp(-jnp.abs(x)))


def _ada_kernel(c_ref, w_ref, b_ref, o_ref):
    c = c_ref[...]
    s = c * _sigmoid(c)
    s_hi, s_lo = _split2(s)
    w_hi, w_lo = _split2(w_ref[...])
    o_ref[...] = _dot(s_hi, w_hi) + _dot(s_lo, w_hi) + _dot(s_hi, w_lo) + b_ref[...]


def _ada(cs, w_ada, b_ada):
    n = w_ada.shape[1]
    bn = D_MODEL
    return pl.pallas_call(
        _ada_kernel,
        out_shape=jax.ShapeDtypeStruct((cs.shape[0], n), F32),
        grid=(n // bn,),
        in_specs=[pl.BlockSpec(cs.shape, lambda j: (0, 0)),
                  pl.BlockSpec((D_MODEL, bn), lambda j: (0, j)),
                  pl.BlockSpec((1, bn), lambda j: (0, j))],
        out_specs=pl.BlockSpec((cs.shape[0], bn), lambda j: (0, j)),
        compiler_params=_params("parallel"),
        name="ada_mod",
    )(cs, w_ada, b_ada)


def _inproj_kernel(x_ref, mod_ref, g_ref, wqkv_ref, wzuv_ref, wba_ref,
                   qkv_ref, z_ref, u_ref, v_ref, ba_ref, bat_ref):
    x = x_ref[...]
    mod = mod_ref[0]
    h = _rms(x, g_ref[...]) * (1.0 + mod[1:2]) + mod[0:1]
    h_hi, h_lo = _split2(h)
    qkv_ref[...] = _dot(h_hi, wqkv_ref[...]).astype(BF16)
    zuv = _dot(h_hi, wzuv_ref[...])
    z_ref[...] = zuv[:, :GDN_W].astype(BF16)
    u_ref[...] = _gelu_tanh(zuv[:, GDN_W:GDN_W + SGU_W]).astype(BF16)
    v_ref[...] = _gelu_tanh(zuv[:, GDN_W + SGU_W:]).astype(BF16)
    w_hi, w_lo = _split2(wba_ref[...])
    ba = _dot(h_hi, w_hi) + _dot(h_lo, w_hi) + _dot(h_hi, w_lo)
    ba_ref[...] = ba[:, :N_GATE_COLS]
    bat_ref[...] = ba.T[:N_GATE_COLS]


def _inproj(x2d, mod, rows_per_mod, ng0, wqkv, wzuv, wba):
    t = x2d.shape[0]
    tm = min(ROW_TILE, t)
    tiles_per_mod = rows_per_mod // tm
    row = lambda i: (i, 0)
    const = lambda i: (0, 0)
    return pl.pallas_call(
        _inproj_kernel,
        out_shape=(jax.ShapeDtypeStruct((t, QKV_COLS), BF16),
                   jax.ShapeDtypeStruct((t, GDN_W), BF16),
                   jax.ShapeDtypeStruct((t, SGU_W), BF16),
                   jax.ShapeDtypeStruct((t, SGU_W), BF16),
                   jax.ShapeDtypeStruct((t, N_GATE_COLS), F32),
                   jax.ShapeDtypeStruct((N_GATE_COLS, t), F32)),
        grid=(t // tm,),
        in_specs=[pl.BlockSpec((tm, D_MODEL), row),
                  pl.BlockSpec((1, 6, D_MODEL), lambda i: (i // tiles_per_mod, 0, 0)),
                  pl.BlockSpec((1, D_MODEL), const),
                  pl.BlockSpec(wqkv.shape, const),
                  pl.BlockSpec(wzuv.shape, const),
                  pl.BlockSpec(wba.shape, const)],
        out_specs=(pl.BlockSpec((tm, QKV_COLS), row),
                   pl.BlockSpec((tm, GDN_W), row),
                   pl.BlockSpec((tm, SGU_W), row),
                   pl.BlockSpec((tm, SGU_W), row),
                   pl.BlockSpec((tm, N_GATE_COLS), row),
                   pl.BlockSpec((N_GATE_COLS, tm), lambda i: (0, i))),
        compiler_params=_params("parallel"),
        name="in_proj",
    )(x2d, mod, ng0, wqkv, wzuv, wba)


def _gdn_prep_kernel(row_len, qkv_ref, cw_ref, ba_ref, bat_ref, alog_r_ref, dtb_r_ref,
                     alog_c_ref, dtb_c_ref, u_ref, w_ref, qd_ref, kd_ref, at_ref, gl_ref):
    n = GDN_BLOCK
    c = DELTA_CHUNK
    ri = lax.broadcasted_iota(jnp.int32, (n, n), 0)
    ci = lax.broadcasted_iota(jnp.int32, (n, n), 1)
    same = (ri // c) == (ci // c)
    lower = same & (ri >= ci)
    upper = same & (ri <= ci)

    def mask01(m):
        return jnp.where(m, 1.0, 0.0).astype(BF16)

    lower_b = mask01(lower)
    upper_b = mask01(upper)
    same_b = mask01(same)

    wi = lax.broadcasted_iota(jnp.int32, (c, n), 0)
    wl = lax.broadcasted_iota(jnp.int32, (c, n), 1)
    wchunk = wl // c
    wj = wl % c
    lower_w = wi >= wj
    upper_w = wi <= wj
    diag_w = wi == wj
    eye_w = jnp.where(diag_w, 1.0, 0.0)
    pair_masks = []
    s = 1
    while s < c:
        pair_masks.append(mask01(((wi // (2 * s)) == (wj // (2 * s))) & ((wi // s) != (wj // s))))
        s *= 2

    def to_wide(full):
        out = full[:c]
        for k in range(1, CHUNKS_PER_BLOCK):
            out = jnp.where(wchunk == k, full[k * c:(k + 1) * c], out)
        return out

    def col_wide(col):
        out = jnp.broadcast_to(col[:c], (c, n))
        for k in range(1, CHUNKS_PER_BLOCK):
            out = jnp.where(wchunk == k, jnp.broadcast_to(col[k * c:(k + 1) * c], (c, n)), out)
        return out

    def block_diag(x_w):
        return jnp.concatenate([x_w] * CHUNKS_PER_BLOCK, axis=0) * same_b

    ba = ba_ref[0]
    bat = bat_ref[...]
    beta_c = _sigmoid(ba[:, :N_CHAINS])
    g_c = -jnp.exp(alog_r_ref[...]) * _softplus(ba[:, N_CHAINS:] + dtb_r_ref[...])
    g_r = -jnp.exp(alog_c_ref[...]) * _softplus(bat[N_CHAINS:] + dtb_c_ref[...])
    gc3 = _split3(g_c)
    gr3 = jnp.concatenate(_split3(g_r), axis=0)

    def sum3_r(m):
        return m[:N_CHAINS] + m[N_CHAINS:2 * N_CHAINS] + m[2 * N_CHAINS:]

    cum_f_c = _dot(lower_b, gc3[0]) + _dot(lower_b, gc3[1]) + _dot(lower_b, gc3[2])
    tot_c = _dot(same_b, gc3[0]) + _dot(same_b, gc3[1]) + _dot(same_b, gc3[2])
    cum_b_c = tot_c - cum_f_c + g_c
    cum_f_r = sum3_r(_dot(gr3, upper_b))
    cum_b_r = sum3_r(_dot(gr3, lower_b))
    ei = lax.broadcasted_iota(jnp.int32, (n, CHUNKS_PER_BLOCK * HEAD_DIM), 0) // c
    ej = lax.broadcasted_iota(jnp.int32, (n, CHUNKS_PER_BLOCK * HEAD_DIM), 1) // HEAD_DIM
    g_last = jnp.exp(sum3_r(_dot(gr3, mask01(ei == ej))))
    gl_ref[0, 0, 0] = g_last[:GDN_HEADS]
    gl_ref[0, 0, 1] = g_last[GDN_HEADS:]

    pos = lax.broadcasted_iota(jnp.int32, (n, HEAD_DIM), 0) % row_len
    first = pos == 0
    last = pos == row_len - 1

    def conv_silu(col):
        x = qkv_ref[0, :, col * HEAD_DIM:(col + 1) * HEAD_DIM].astype(F32)
        cw = cw_ref[:, col * HEAD_DIM:(col + 1) * HEAD_DIM]
        xp = jnp.where(first, 0.0, pltpu.roll(x, 1, 0))
        xn = jnp.where(last, 0.0, pltpu.roll(x, n - 1, 0))
        y = xp * cw[0:1] + x * cw[1:2] + xn * cw[2:3]
        return y * _sigmoid(y)

    def l2n(x):
        return x * lax.rsqrt(jnp.sum(x * x, -1, keepdims=True) + NORM_EPS)

    a_bs, ps, rhss = [None] * N_CHAINS, [None] * N_CHAINS, [None] * N_CHAINS
    for h in range(GDN_HEADS):
        q = l2n(conv_silu(h)) * (HEAD_DIM ** -0.5)
        k = l2n(conv_silu(GDN_HEADS + h))
        v = conv_silu(2 * GDN_HEADS + h)
        k_b = k.astype(BF16)
        qk_kk = _dot_nt(jnp.concatenate([q.astype(BF16), k_b], axis=0), k_b)
        qk_w = to_wide(qk_kk[:n])
        kk_w = to_wide(qk_kk[n:])
        for d in range(2):
            j = d * GDN_HEADS + h
            mask_w = lower_w if d == 0 else upper_w
            cum_c = (cum_f_c if d == 0 else cum_b_c)[:, j:j + 1]
            cum_r = (cum_f_r if d == 0 else cum_b_r)[j:j + 1, :]
            b_c = beta_c[:, j:j + 1]
            decay_w = jnp.where(mask_w, jnp.exp(jnp.where(mask_w, col_wide(cum_c) - cum_r, 0.0)), 0.0)
            amat_w = jnp.where(diag_w, 0.0, kk_w * decay_w * col_wide(b_c))
            a_bs[j] = amat_w.astype(BF16)
            ps[j] = eye_w - amat_w * pair_masks[0].astype(F32)
            e_c = jnp.exp(cum_c)
            rhss[j] = jnp.concatenate([(v * b_c).astype(BF16), (k * (b_c * e_c)).astype(BF16)], axis=1)
            cols = slice(j * HEAD_DIM, (j + 1) * HEAD_DIM)
            qd_ref[0, :, cols] = (q * e_c).astype(BF16)
            kd_ref[0, :, cols] = (k * jnp.exp(tot_c[:, j:j + 1] - cum_c)).astype(BF16)
            at_ref[0, 0, j * c:(j + 1) * c, :] = (qk_w * decay_w).astype(BF16)

    for pm in pair_masks[1:]:
        p_bs = [p.astype(BF16) for p in ps]
        ys = [_dot(a_bs[j] * pm, block_diag(p_bs[j])) for j in range(N_CHAINS)]
        ps = [ps[j] - _dot(p_bs[j], block_diag(ys[j].astype(BF16))) for j in range(N_CHAINS)]

    for j in range(N_CHAINS):
        uw = _dot(block_diag(ps[j].astype(BF16)), rhss[j])
        cols = slice(j * HEAD_DIM, (j + 1) * HEAD_DIM)
        u_ref[0, :, cols] = uw[:, :HEAD_DIM].astype(BF16)
        w_ref[0, :, cols] = uw[:, HEAD_DIM:].astype(BF16)


def _gdn_prep(qkv, conv_w, ba, bat, alog, dtb, row_len):
    b, l, _ = qkv.shape
    nblk = l // GDN_BLOCK
    wide = N_CHAINS * HEAD_DIM
    blk = lambda bi, i: (bi, i, 0)
    const = lambda bi, i: (0, 0)
    alog_r, dtb_r = alog.reshape(1, N_CHAINS), dtb.reshape(1, N_CHAINS)
    alog_c, dtb_c = alog.reshape(N_CHAINS, 1), dtb.reshape(N_CHAINS, 1)
    return pl.pallas_call(
        functools.partial(_gdn_prep_kernel, row_len),
        out_shape=(jax.ShapeDtypeStruct((b, l, wide), BF16),) * 4 + (
            jax.ShapeDtypeStruct((b, nblk, N_CHAINS * DELTA_CHUNK, GDN_BLOCK), BF16),
            jax.ShapeDtypeStruct((b, nblk, 2, GDN_HEADS, CHUNKS_PER_BLOCK * HEAD_DIM), F32)),
        grid=(b, nblk),
        in_specs=[pl.BlockSpec((1, GDN_BLOCK, QKV_COLS), blk),
                  pl.BlockSpec((3, QKV_COLS), const),
                  pl.BlockSpec((1, GDN_BLOCK, N_GATE_COLS), blk),
                  pl.BlockSpec((N_GATE_COLS, GDN_BLOCK), lambda bi, i: (0, bi * nblk + i)),
                  pl.BlockSpec((1, N_CHAINS), const),
                  pl.BlockSpec((1, N_CHAINS), const),
                  pl.BlockSpec((N_CHAINS, 1), const),
                  pl.BlockSpec((N_CHAINS, 1), const)],
        out_specs=(pl.BlockSpec((1, GDN_BLOCK, wide), blk),) * 4 + (
            pl.BlockSpec((1, 1, N_CHAINS * DELTA_CHUNK, GDN_BLOCK), lambda bi, i: (bi, i, 0, 0)),
            pl.BlockSpec((1, 1, 2, GDN_HEADS, CHUNKS_PER_BLOCK * HEAD_DIM),
                         lambda bi, i: (bi, i, 0, 0, 0))),
        compiler_params=_params("parallel", "parallel"),
        name="gdn_prep",
    )(qkv, conv_w, ba, bat, alog_r, dtb_r, alog_c, dtb_c)


def _gdn_scan_kernel(uf, wf, qf, kf, af, gf, ub, wb, qb, kb, ab, gb, s0_ref,
                     of_ref, ob_ref, sfin_ref, s_scr):
    i = pl.program_id(0)
    c = DELTA_CHUNK
    n_batch = s0_ref.shape[0]

    @pl.when(i == 0)
    def _():
        s_scr[...] = s0_ref[...]

    ops = ((uf, wf, qf, kf, af, gf, of_ref), (ub, wb, qb, kb, ab, gb, ob_ref))
    chains = [(bi, d, h) for bi in range(n_batch) for d in range(2) for h in range(GDN_HEADS)]
    states = [s_scr[bi, d * GDN_HEADS + h] for bi, d, h in chains]
    for step in range(CHUNKS_PER_BLOCK):
        def chunk(d):
            cc = step if d == 0 else CHUNKS_PER_BLOCK - 1 - step
            return cc, slice(cc * c, (cc + 1) * c)

        xs = []
        for j, (bi, d, h) in enumerate(chains):
            _, rows = chunk(d)
            cols = slice(h * HEAD_DIM, (h + 1) * HEAD_DIM)
            wq = jnp.concatenate([ops[d][1][bi, rows, cols], ops[d][2][bi, rows, cols]], axis=0)
            xs.append(_dot(wq, states[j].astype(BF16)))
        v_news = []
        for j, (bi, d, h) in enumerate(chains):
            _, rows = chunk(d)
            cols = slice(h * HEAD_DIM, (h + 1) * HEAD_DIM)
            v_news.append((ops[d][0][bi, rows, cols].astype(F32) - xs[j][:c]).astype(BF16))
        for j, (bi, d, h) in enumerate(chains):
            cc, rows = chunk(d)
            cols = slice(h * HEAD_DIM, (h + 1) * HEAD_DIM)
            a_c = ops[d][4][bi, 0, h * c:(h + 1) * c, cc * c:(cc + 1) * c]
            ops[d][6][bi, rows, cols] = xs[j][c:] + _dot(a_c, v_news[j])
            ds = _dot_tn(ops[d][3][bi, rows, cols], v_news[j])
            g_last = ops[d][5][bi, 0, 0, h:h + 1, cc * HEAD_DIM:(cc + 1) * HEAD_DIM]
            states[j] = states[j] * g_last + ds
    for j, (bi, d, h) in enumerate(chains):
        s_scr[bi, d * GDN_HEADS + h] = states[j]

    @pl.when(i == pl.num_programs(0) - 1)
    def _():
        sfin_ref[...] = s_scr[...]


def _gdn_scan(u, w, qd, kd, at, gl, s0):
    b, l, _ = u.shape
    nblk = l // GDN_BLOCK
    half = GDN_HEADS * HEAD_DIM
    fwd = lambda i: (0, i, 0)
    bwd = lambda i: (0, nblk - 1 - i, 1)
    big = lambda m: pl.BlockSpec((b, GDN_BLOCK, half), m)
    att_shape = (b, 1, GDN_HEADS * DELTA_CHUNK, GDN_BLOCK)
    attf = pl.BlockSpec(att_shape, lambda i: (0, i, 0, 0))
    attb = pl.BlockSpec(att_shape, lambda i: (0, nblk - 1 - i, 1, 0))
    gl_shape = (b, 1, 1, GDN_HEADS, CHUNKS_PER_BLOCK * HEAD_DIM)
    glf = pl.BlockSpec(gl_shape, lambda i: (0, i, 0, 0, 0))
    glb = pl.BlockSpec(gl_shape, lambda i: (0, nblk - 1 - i, 1, 0, 0))
    state = pl.BlockSpec((b, N_CHAINS, HEAD_DIM, HEAD_DIM), lambda i: (0, 0, 0, 0))
    return pl.pallas_call(
        _gdn_scan_kernel,
        out_shape=(jax.ShapeDtypeStruct((b, l, half), F32),
                   jax.ShapeDtypeStruct((b, l, half), F32),
                   jax.ShapeDtypeStruct((b, N_CHAINS, HEAD_DIM, HEAD_DIM), F32)),
        grid=(nblk,),
        in_specs=[big(fwd), big(fwd), big(fwd), big(fwd), attf, glf,
                  big(bwd), big(bwd), big(bwd), big(bwd), attb, glb, state],
        out_specs=(pl.BlockSpec((b, GDN_BLOCK, half), fwd),
                   pl.BlockSpec((b, GDN_BLOCK, half), lambda i: (0, nblk - 1 - i, 0)),
                   state),
        scratch_shapes=[pltpu.VMEM((b, N_CHAINS, HEAD_DIM, HEAD_DIM), F32)],
        compiler_params=_params("arbitrary"),
        name="gdn_scan",
    )(u, w, qd, kd, at, gl, u, w, qd, kd, at, gl, s0)


def _mix_into(mix_ref, of_ref, ob_ref, z_ref, u_ref, v_ref, gng_ref, lng_ref, lnb_ref, ws_ref,
              bs_ref):
    tm = of_ref.shape[0]
    o = of_ref[...] + ob_ref[...]
    z = z_ref[...].astype(F32)
    for h in range(GDN_HEADS):
        cols = slice(h * HEAD_DIM, (h + 1) * HEAD_DIM)
        oh = o[:, cols]
        zh = z[:, cols]
        r = lax.rsqrt(jnp.mean(oh * oh, -1, keepdims=True) + NORM_EPS)
        mix_ref[:, cols] = (oh * r * gng_ref[...] * (zh * _sigmoid(zh))).astype(BF16)
    for g in range(SGU_GROUPS):
        cols = slice(g * SGU_GROUP, (g + 1) * SGU_GROUP)
        vg = v_ref[:, cols].astype(F32)
        vc = vg - jnp.mean(vg, -1, keepdims=True)
        vn = vc * lax.rsqrt(jnp.mean(vc * vc, -1, keepdims=True) + NORM_EPS)
        vn = (vn * lng_ref[g:g + 1] + lnb_ref[g:g + 1]).astype(BF16)
        wsg = ws_ref[g].astype(BF16)
        for n in range(tm // SGU_CHUNK):
            rows = slice(n * SGU_CHUNK, (n + 1) * SGU_CHUNK)
            s = _dot(wsg, vn[rows]) + bs_ref[g]
            mix_ref[rows, GDN_W + g * SGU_GROUP:GDN_W + (g + 1) * SGU_GROUP] = (
                u_ref[rows, cols].astype(F32) * s).astype(BF16)


def _outproj_kernel(of_ref, ob_ref, z_ref, u_ref, v_ref, gng_ref, lng_ref, lnb_ref, ws_ref, bs_ref,
                    x_ref, mod_ref, ng_ref, wout_ref, rwt_ref, rb_ref,
                    x1_ref, h2_ref, lt_ref, mix_scr):
    _mix_into(mix_scr, of_ref, ob_ref, z_ref, u_ref, v_ref, gng_ref, lng_ref, lnb_ref, ws_ref, bs_ref)
    mod = mod_ref[0]
    y = _dot(mix_scr[...], wout_ref[...])
    x1 = x_ref[...] + mod[2:3] * _rms(y, ng_ref[1:2])
    x1_ref[...] = x1
    h2 = _rms(x1, ng_ref[2:3]) * (1.0 + mod[4:5]) + mod[3:4]
    h_hi, h_lo = _split2(h2)
    _store_planes(h2_ref, _pack_rows(h2))
    w_hi, w_lo = _split2(rwt_ref[...])
    lt_ref[...] = _dot_nt(w_hi, h_hi) + _dot_nt(w_lo, h_hi) + _dot_nt(w_hi, h_lo) + rb_ref[...]


def _outproj(o_f, o_b, z, ug, vg, gdn_norm_g, ln_g, ln_b, w_s, b_s_full,
             x2d, mod, rows_per_mod, ng, wout, rwt, rb_col):
    t = x2d.shape[0]
    tm = min(ROW_TILE, t)
    tiles_per_mod = rows_per_mod // tm
    row = lambda i: (i, 0)
    const = lambda i: (0, 0)
    c3 = lambda i: (0, 0, 0)
    return pl.pallas_call(
        _outproj_kernel,
        out_shape=(jax.ShapeDtypeStruct((t, D_MODEL), F32),
                   jax.ShapeDtypeStruct((N_PLANES, t, SC_ROW_W), U32),
                   jax.ShapeDtypeStruct((N_EXPERTS, t), F32)),
        grid=(t // tm,),
        in_specs=[pl.BlockSpec((tm, GDN_W), row), pl.BlockSpec((tm, GDN_W), row),
                  pl.BlockSpec((tm, GDN_W), row), pl.BlockSpec((tm, SGU_W), row),
                  pl.BlockSpec((tm, SGU_W), row),
                  pl.BlockSpec((1, HEAD_DIM), const),
                  pl.BlockSpec((SGU_GROUPS, SGU_GROUP), const),
                  pl.BlockSpec((SGU_GROUPS, SGU_GROUP), const),
                  pl.BlockSpec((SGU_GROUPS, SGU_CHUNK, SGU_CHUNK), c3),
                  pl.BlockSpec((SGU_GROUPS, SGU_CHUNK, SGU_GROUP), c3),
                  pl.BlockSpec((tm, D_MODEL), row),
                  pl.BlockSpec((1, 6, D_MODEL), lambda i: (i // tiles_per_mod, 0, 0)),
                  pl.BlockSpec((4, D_MODEL), const),
                  pl.BlockSpec((D_MODEL, D_MODEL), const),
                  pl.BlockSpec((N_EXPERTS, D_MODEL), const),
                  pl.BlockSpec((N_EXPERTS, 1), const)],
        out_specs=(pl.BlockSpec((tm, D_MODEL), row),
                   pl.BlockSpec((N_PLANES, tm, SC_ROW_W), lambda i: (0, i, 0)),
                   pl.BlockSpec((N_EXPERTS, tm), lambda i: (0, i))),
        compiler_params=_params("parallel"),
        scratch_shapes=[pltpu.VMEM((tm, D_MODEL), BF16)],
        name="out_proj_router",
    )(o_f, o_b, z, ug, vg, gdn_norm_g, ln_g, ln_b, w_s, b_s_full, x2d, mod, ng, wout, rwt, rb_col)


def _moe_kernel(be_ref, nb_ref, xb_ref, wgu_ref, bgu_ref, wd_ref, bd_ref, y_ref,
                wgu_b, wd_b, gut_scr):
    i = pl.program_id(0)
    live = i < nb_ref[0]
    new_expert = jnp.logical_or(i == 0, be_ref[i] != be_ref[jnp.maximum(i - 1, 0)])

    @pl.when(jnp.logical_and(live, new_expert))
    def _():
        wgu_b[...] = wgu_ref[0].astype(BF16)
        wd_b[...] = wd_ref[0].astype(BF16)

    @pl.when(live)
    def _():
        xb = _unpack_rows(_load_planes(xb_ref)).astype(BF16)
        gu = _dot(xb, wgu_b[...]) + bgu_ref[0]
        gu_t = gu.T
        acts = []
        for part in range(MOE_ROWS // LANES):
            part_ref = gut_scr.at[part]
            part_ref[...] = gu_t[:, part * LANES:(part + 1) * LANES]
            gate = jnp.minimum(part_ref[pl.ds(0, D_FF, stride=2), :], SWIGLU_LIMIT)
            up = jnp.clip(part_ref[pl.ds(1, D_FF, stride=2), :], -SWIGLU_LIMIT, SWIGLU_LIMIT)
            acts.append(((up + 1.0) * gate * _sigmoid(SWIGLU_ALPHA * gate)).astype(BF16))
        act_t = jnp.concatenate(acts, axis=1)
        _store_planes(y_ref, _pack_rows(_dot_tn(act_t, wd_b[...]) + bd_ref[0]))

    @pl.when(jnp.logical_not(live))
    def _():
        y_ref[...] = jnp.zeros_like(y_ref)


def _moe_ffn(block_e, n_used, xb, w_gu, b_gu, w_down, b_down):
    n_rows = xb.shape[1]
    n_blocks = n_rows // MOE_ROWS
    row = lambda i, be, nb: (0, i, 0)
    ex3 = lambda i, be, nb: (be[i], 0, 0)
    live_row = lambda i, be, nb: (0, jnp.minimum(i, nb[0] - 1), 0)
    planes = (N_PLANES, MOE_ROWS, SC_ROW_W)
    grid_spec = pltpu.PrefetchScalarGridSpec(
        num_scalar_prefetch=2,
        grid=(n_blocks,),
        in_specs=[pl.BlockSpec(planes, live_row),
                  pl.BlockSpec((1, D_MODEL, 2 * D_FF), ex3),
                  pl.BlockSpec((1, 1, 2 * D_FF), ex3),
                  pl.BlockSpec((1, D_FF, D_MODEL), ex3),
                  pl.BlockSpec((1, 1, D_MODEL), ex3)],
        out_specs=pl.BlockSpec(planes, row),
        scratch_shapes=[pltpu.VMEM((D_MODEL, 2 * D_FF), BF16),
                        pltpu.VMEM((D_FF, D_MODEL), BF16),
                        pltpu.VMEM((MOE_ROWS // LANES, 2 * D_FF, LANES), F32)],
    )
    return pl.pallas_call(
        _moe_kernel,
        out_shape=jax.ShapeDtypeStruct((N_PLANES, n_rows, SC_ROW_W), U32),
        grid_spec=grid_spec,
        compiler_params=_params("arbitrary"),
        name="moe_ffn",
    )(block_e, n_used, xb, w_gu, b_gu, w_down, b_down)


def _combine_kernel(y0_ref, y1_ref, y2_ref, y3_ref, gt_ref, x1_ref, mod_ref, ng_ref, o_ref):
    mod = mod_ref[0]
    gt = gt_ref[...]
    y = _unpack_rows(_load_planes(y0_ref)) * gt[:, 0:1]
    for k, y_ref in ((1, y1_ref), (2, y2_ref), (3, y3_ref)):
        y = y + _unpack_rows(_load_planes(y_ref)) * gt[:, k:k + 1]
    o_ref[...] = x1_ref[...] + mod[5:6] * _rms(y, ng_ref[3:4])


def _combine(yg, gates, x1, mod, rows_per_mod, ng):
    t = x1.shape[0]
    tm = min(ROW_TILE, t)
    tiles_per_mod = rows_per_mod // tm
    n_tiles = t // tm
    row = lambda i: (i, 0)
    choice = lambda k: pl.BlockSpec((N_PLANES, tm, SC_ROW_W), lambda i: (0, k * n_tiles + i, 0))
    return pl.pallas_call(
        _combine_kernel,
        out_shape=jax.ShapeDtypeStruct((t, D_MODEL), F32),
        grid=(n_tiles,),
        in_specs=[choice(0), choice(1), choice(2), choice(3),
                  pl.BlockSpec((tm, TOP_K), row),
                  pl.BlockSpec((tm, D_MODEL), row),
                  pl.BlockSpec((1, 6, D_MODEL), lambda i: (i // tiles_per_mod, 0, 0)),
                  pl.BlockSpec((4, D_MODEL), lambda i: (0, 0))],
        out_specs=pl.BlockSpec((tm, D_MODEL), row),
        compiler_params=_params("parallel"),
        name="moe_combine",
    )(yg, yg, yg, yg, gates, x1, mod, ng)


def _route_kernel(lt_ref, eidx_ref, gate_ref, rank_ref, cnt_ref, carry):
    i = pl.program_id(0)
    tile = lt_ref.shape[1]

    @pl.when(i == 0)
    def _():
        carry[...] = jnp.zeros_like(carry)

    logits = lt_ref[...]
    eio = lax.broadcasted_iota(jnp.int32, (N_EXPERTS, tile), 0).astype(F32)
    vals, sels = [], []
    for k in range(TOP_K):
        m = jnp.max(logits, axis=0, keepdims=True)
        idx = jnp.min(jnp.where(logits == m, eio, float(N_EXPERTS)), axis=0, keepdims=True)
        sel = eio == idx
        logits = jnp.where(sel, -jnp.inf, logits)
        vals.append(m)
        sels.append(sel)
        eidx_ref[k:k + 1, :] = idx.astype(jnp.int32)
    exps = [jnp.exp(v - vals[0]) for v in vals]
    denom = exps[0] + exps[1] + exps[2] + exps[3]
    for k in range(TOP_K):
        gate_ref[k:k + 1, :] = exps[k] / denom

    member = jnp.where(sels[0] | sels[1] | sels[2] | sels[3], 1.0, 0.0)
    ti = lax.broadcasted_iota(jnp.int32, (tile, tile), 0)
    tj = lax.broadcasted_iota(jnp.int32, (tile, tile), 1)
    earlier = jnp.where(ti < tj, 1.0, 0.0).astype(BF16)
    before = _dot(member.astype(BF16), earlier) + carry[...]
    for k in range(TOP_K):
        rank_ref[k:k + 1, :] = jnp.sum(jnp.where(sels[k], before, 0.0), axis=0,
                                       keepdims=True).astype(jnp.int32)
    carry[...] = carry[...] + jnp.sum(member, axis=1, keepdims=True)
    cnt_ref[...] = carry[...].astype(jnp.int32)


def _route(logits_t):
    t = logits_t.shape[1]
    tile = min(ROUTE_TILE, t)
    blk = lambda i: (0, i)
    return pl.pallas_call(
        _route_kernel,
        out_shape=(jax.ShapeDtypeStruct((TOP_K, t), jnp.int32),
                   jax.ShapeDtypeStruct((TOP_K, t), F32),
                   jax.ShapeDtypeStruct((TOP_K, t), jnp.int32),
                   jax.ShapeDtypeStruct((N_EXPERTS, 1), jnp.int32)),
        grid=(t // tile,),
        in_specs=[pl.BlockSpec((N_EXPERTS, tile), blk)],
        out_specs=(pl.BlockSpec((TOP_K, tile), blk), pl.BlockSpec((TOP_K, tile), blk),
                   pl.BlockSpec((TOP_K, tile), blk),
                   pl.BlockSpec((N_EXPERTS, 1), lambda i: (0, 0))),
        scratch_shapes=[pltpu.VMEM((N_EXPERTS, 1), F32)],
        compiler_params=_params("arbitrary"),
        name="moe_route",
    )(logits_t)


def _slot_tables(eidx, rank, counts, n_blocks):
    padded = (counts + MOE_ROWS - 1) // MOE_ROWS * MOE_ROWS
    pad_end = jnp.cumsum(padded)
    pad_start = pad_end - padded
    experts = jnp.arange(N_EXPERTS, dtype=jnp.int32)
    dest = rank + jnp.sum(jnp.where(eidx[..., None] == experts, pad_start, 0), axis=-1)
    first_row = jnp.arange(n_blocks, dtype=jnp.int32)[:, None] * MOE_ROWS
    block_e = jnp.minimum(jnp.sum((pad_end[None, :] <= first_row).astype(jnp.int32), axis=1),
                          N_EXPERTS - 1)
    n_used = pad_end[-1:] // MOE_ROWS
    return dest.astype(jnp.int32), pad_start.astype(jnp.int32), block_e, n_used.astype(jnp.int32)


def _sc_mesh():
    return plsc.VectorSubcoreMesh(core_axis_name="c", subcore_axis_name="s",
                                  num_cores=SC_CORES, num_subcores=SC_SUBCORES)


def _plane_row_ids(rows, rows_per_plane):
    return jnp.concatenate([rows + p * rows_per_plane for p in range(N_PLANES)], axis=-1)


def _sc_gather_rows(table, rows):
    v = table.shape[1]
    idx = _plane_row_ids(rows, v)[None]
    n_all = idx.shape[1]

    @functools.partial(pl.kernel, mesh=_sc_mesh(), name="moe_gather_rows",
                       out_type=jax.ShapeDtypeStruct((n_all, SC_ROW_W), U32))
    def gather(x_hbm, i_hbm, o_hbm):
        def body(i_vmem, o_vmem):
            pltpu.sync_copy(x_hbm.at[i_vmem.at[0]], o_vmem)

        pltpu.emit_pipeline(
            body, grid=(n_all // SC_WINDOW,),
            in_specs=[pl.BlockSpec((1, SC_WINDOW), lambda i: (0, i))],
            out_specs=[pl.BlockSpec((SC_WINDOW, SC_ROW_W), lambda i: (i, 0))],
            core_axis_name=("c", "s"), dimension_semantics=(pltpu.PARALLEL,),
        )(i_hbm, o_hbm)

    return gather(table.reshape(N_PLANES * v, SC_ROW_W), idx).reshape(N_PLANES, -1, SC_ROW_W)


def _sc_scatter_rows(rows, dest, n_out):
    t = rows.shape[1]
    idx = _plane_row_ids(dest, n_out)

    @functools.partial(pl.kernel, mesh=_sc_mesh(), name="moe_scatter_rows", scratch_types=[],
                       out_type=jax.ShapeDtypeStruct((N_PLANES * n_out, SC_ROW_W), U32))
    def scatter(x_hbm, i_hbm, o_hbm):
        def body(x_vmem, i_vmem):
            for k in range(TOP_K):
                pltpu.sync_copy(x_vmem, o_hbm.at[i_vmem.at[k]])

        pltpu.emit_pipeline(
            body, grid=(N_PLANES * t // SC_WINDOW,),
            in_specs=[pl.BlockSpec((SC_WINDOW, SC_ROW_W), lambda i: (i, 0)),
                      pl.BlockSpec((TOP_K, SC_WINDOW), lambda i: (0, i))],
            out_specs=[],
            core_axis_name=("c", "s"), dimension_semantics=(pltpu.PARALLEL,),
        )(x_hbm, i_hbm)

    return scatter(rows.reshape(N_PLANES * t, SC_ROW_W), idx).reshape(N_PLANES, n_out, SC_ROW_W)


def _zero_pad_kernel(cnt_ref, start_ref, xb_in_ref, xb_ref, zero_scr, sem):
    del xb_in_ref
    zero_scr[...] = jnp.zeros_like(zero_scr)

    def zero_row_copy(p, row):
        return pltpu.make_async_copy(zero_scr.at[pl.ds(0, 1)], xb_ref.at[p, pl.ds(row, 1)], sem)

    def for_each_pad_row(fn):
        def per_expert(e, carry):
            n_real = cnt_ref[e]
            first_pad = start_ref[e] + n_real
            n_pad = (MOE_ROWS - n_real % MOE_ROWS) % MOE_ROWS

            def per_row(r, c):
                for p in range(N_PLANES):
                    fn(zero_row_copy(p, first_pad + r))
                return c
            lax.fori_loop(0, n_pad, per_row, 0)
            return carry
        lax.fori_loop(0, N_EXPERTS, per_expert, 0)

    for_each_pad_row(lambda copy: copy.start())
    for_each_pad_row(lambda copy: copy.wait())


def _zero_pad_slots(counts, pad_start, xb):
    grid_spec = pltpu.PrefetchScalarGridSpec(
        num_scalar_prefetch=2,
        grid=(1,),
        in_specs=[pl.BlockSpec(memory_space=pl.ANY)],
        out_specs=pl.BlockSpec(memory_space=pl.ANY),
        scratch_shapes=[pltpu.VMEM((8, SC_ROW_W), U32), pltpu.SemaphoreType.DMA],
    )
    return pl.pallas_call(
        _zero_pad_kernel,
        out_shape=jax.ShapeDtypeStruct(xb.shape, xb.dtype),
        grid_spec=grid_spec,
        input_output_aliases={2: 0},
        compiler_params=_params("arbitrary"),
        name="moe_zero_pad",
    )(counts, pad_start, xb)


def kernel(x, c, ctx, c_ctx, w_ada, b_ada, norm_g, w_in, conv_w, a_log, dt_bias, gdn_norm_g,
           sgu_ln_g, sgu_ln_b, sgu_w, sgu_b, w_out, router_w, router_b, w_gu, b_gu, w_down, b_down):
    b, l, d = x.shape
    lc = ctx.shape[1]
    t = b * l
    assert d == D_MODEL and l % ROW_TILE == 0 and l % GDN_BLOCK == 0 and lc % GDN_BLOCK == 0
    assert w_ada.shape[0] == 1, "single-layer block"

    cs = jnp.concatenate([c, c_ctx[None], jnp.zeros((8 - b - 1, d), F32)], axis=0)
    mod_all = _ada(cs, w_ada[0], b_ada[0][None])
    mod = mod_all[:b].reshape(b, 6, d)
    mod_c = mod_all[b:b + 1].reshape(1, 6, d)
    ng = norm_g[0]

    w = w_in[0]
    wqkv = w[:, :QKV_COLS].astype(BF16)
    wba = jnp.pad(w[:, QKV_COLS:QKV_COLS + N_GATE_COLS], ((0, 0), (0, LANES - N_GATE_COLS)))
    wzuv = w[:, QKV_COLS + N_GATE_COLS:].astype(BF16)

    x2d = x.reshape(t, d)
    qkv, z, ug, vg, ba, bat = _inproj(x2d, mod, l, ng[0:1], wqkv, wzuv, wba)
    ctx2d = ctx.reshape(b * lc, d)
    qkv_c, _, _, _, ba_c, bat_c = _inproj(ctx2d, mod_c, b * lc, ng[0:1], wqkv, wzuv, wba)

    alog = a_log[0].reshape(-1)
    dtb = dt_bias[0].reshape(-1)
    pc = _gdn_prep(qkv_c.reshape(b, lc, QKV_COLS), conv_w[0], ba_c.reshape(b, lc, N_GATE_COLS),
                   bat_c, alog, dtb, lc)
    s_zero = jnp.zeros((b, N_CHAINS, HEAD_DIM, HEAD_DIM), F32)
    _, _, s_ctx = _gdn_scan(*pc, s_zero)
    pp = _gdn_prep(qkv.reshape(b, l, QKV_COLS), conv_w[0], ba.reshape(b, l, N_GATE_COLS),
                   bat, alog, dtb, GRID_W)
    o_f, o_b, _ = _gdn_scan(*pp, s_ctx)

    b_s_full = jnp.broadcast_to(sgu_b[0][:, :, None], (SGU_GROUPS, SGU_CHUNK, SGU_GROUP))
    x1, h2p, logits_t = _outproj(o_f.reshape(t, GDN_W), o_b.reshape(t, GDN_W), z, ug, vg,
                                 gdn_norm_g, sgu_ln_g[0], sgu_ln_b[0], sgu_w[0], b_s_full,
                                 x2d, mod, l, ng, w_out[0].astype(BF16),
                                 router_w[0].T, router_b[0][:, None])

    eidx, gates_t, rank, counts = _route(logits_t)
    counts = counts[:, 0]
    n_blocks = -(-(t * TOP_K) // MOE_ROWS) + N_EXPERTS
    dest, pad_start, block_e, n_used = _slot_tables(eidx, rank, counts, n_blocks)
    xb = _sc_scatter_rows(h2p, dest, n_blocks * MOE_ROWS)
    xb = _zero_pad_slots(counts, pad_start, xb)
    yb = _moe_ffn(block_e, n_used, xb, w_gu[0], b_gu[0][:, None, :], w_down[0],
                  b_down[0][:, None, :])
    yg = _sc_gather_rows(yb, dest.reshape(-1))
    out = _combine(yg, gates_t.T, x1, mod, l, ng)
    return out.reshape(b, l, d)
```

```python
import functools
import math

import jax
import jax.numpy as jnp
from jax import lax
from jax.experimental import pallas as pl
from jax.experimental.pallas import tpu as pltpu
from jax.experimental.pallas import tpu_sc as plsc

F32 = jnp.float32
BF16 = jnp.bfloat16

D_MODEL = 1024
GDN_HEADS = 4
HEAD_DIM = 128
GDN_W = GDN_HEADS * HEAD_DIM
SGU_GROUPS = 4
SGU_GROUP = 128
SGU_W = SGU_GROUPS * SGU_GROUP
SGU_CHUNK = 128
DELTA_CHUNK = 64
GRID_W = 64
N_EXPERTS = 32
TOP_K = 4
D_FF = 1024
SWIGLU_LIMIT = 7.0
SWIGLU_ALPHA = 1.702
NORM_EPS = 1e-6
QKV_COLS = 3 * GDN_W
N_CHAINS = 2 * GDN_HEADS
N_GATE_COLS = 2 * N_CHAINS

ROW_TILE = 512
GDN_BLOCK = 256
CHUNKS_PER_BLOCK = GDN_BLOCK // DELTA_CHUNK
MOE_ROWS = 512
U32 = jnp.uint32
LANES = 128
SUBLANES = 8
PACKED_W = D_MODEL // 2
ROUTE_TILE = 512
SC_CORES = 2
SC_SUBCORES = 16
SC_WINDOW = 128
N_PLANES = 2
SC_ROW_W = PACKED_W // N_PLANES
VMEM_LIMIT = 56 * 1024 * 1024


def _params(*sem):
    return pltpu.CompilerParams(dimension_semantics=sem, vmem_limit_bytes=VMEM_LIMIT)


def _dot(a, b):
    return jnp.dot(a, b, preferred_element_type=F32)


def _dot_nt(a, b):
    return lax.dot_general(a, b, (((1,), (1,)), ((), ())), preferred_element_type=F32)


def _dot_tn(a, b):
    return lax.dot_general(a, b, (((0,), (0,)), ((), ())), preferred_element_type=F32)


def _split2(a):
    hi = a.astype(BF16)
    lo = (a - hi.astype(F32)).astype(BF16)
    return hi, lo


def _split3(a):
    hi = a.astype(BF16)
    r = a - hi.astype(F32)
    mid = r.astype(BF16)
    lo = (r - mid.astype(F32)).astype(BF16)
    return hi, mid, lo


def _pack_rows(x32):
    xb = x32.astype(BF16).astype(F32)
    hi = lax.bitcast_convert_type(xb[:, :PACKED_W], U32)
    lo = lax.bitcast_convert_type(xb[:, PACKED_W:], U32)
    return hi | (lo >> 16)


def _store_planes(ref, packed):
    for p in range(N_PLANES):
        ref[p] = packed[:, p * SC_ROW_W:(p + 1) * SC_ROW_W]


def _load_planes(ref):
    return jnp.concatenate([ref[p] for p in range(N_PLANES)], axis=1)


def _unpack_rows(w):
    hi = lax.bitcast_convert_type(w & jnp.uint32(0xFFFF0000), F32)
    lo = lax.bitcast_convert_type(w << 16, F32)
    return jnp.concatenate([hi, lo], axis=1)


def _rms(x32, g):
    return x32 * lax.rsqrt(jnp.mean(x32 * x32, -1, keepdims=True) + NORM_EPS) * g


def _gelu_tanh(x):
    c = math.sqrt(2.0 / math.pi)
    return 0.5 * x * (1.0 + jnp.tanh(c * (x + 0.044715 * (x * x * x))))


def _sigmoid(x):
    return 1.0 / (1.0 + jnp.exp(-x))


def _softplus(x):
    return jnp.maximum(x, 0.0) + jnp.log(1.0 + jnp.exp(-jnp.abs(x)))


def _ada_kernel(c_ref, w_ref, b_ref, o_ref):
    c = c_ref[...]
    s = c * _sigmoid(c)
    s_hi, s_lo = _split2(s)
    w_hi, w_lo = _split2(w_ref[...])
    o_ref[...] = _dot(s_hi, w_hi) + _dot(s_lo, w_hi) + _dot(s_hi, w_lo) + b_ref[...]


def _ada(cs, w_ada, b_ada):
    n = w_ada.shape[1]
    bn = D_MODEL
    return pl.pallas_call(
        _ada_kernel,
        out_shape=jax.ShapeDtypeStruct((cs.shape[0], n), F32),
        grid=(n // bn,),
        in_specs=[pl.BlockSpec(cs.shape, lambda j: (0, 0)),
                  pl.BlockSpec((D_MODEL, bn), lambda j: (0, j)),
                  pl.BlockSpec((1, bn), lambda j: (0, j))],
        out_specs=pl.BlockSpec((cs.shape[0], bn), lambda j: (0, j)),
        compiler_params=_params("parallel"),
        name="ada_mod",
    )(cs, w_ada, b_ada)


def _inproj_kernel(x_ref, mod_ref, g_ref, wqkv_ref, wzuv_ref, wba_ref,
                   qkv_ref, z_ref, u_ref, v_ref, ba_ref, bat_ref):
    x = x_ref[...]
    mod = mod_ref[0]
    h = _rms(x, g_ref[...]) * (1.0 + mod[1:2]) + mod[0:1]
    h_hi, h_lo = _split2(h)
    qkv_ref[...] = _dot(h_hi, wqkv_ref[...]).astype(BF16)
    zuv = _dot(h_hi, wzuv_ref[...])
    z_ref[...] = zuv[:, :GDN_W].astype(BF16)
    u_ref[...] = _gelu_tanh(zuv[:, GDN_W:GDN_W + SGU_W]).astype(BF16)
    v_ref[...] = _gelu_tanh(zuv[:, GDN_W + SGU_W:]).astype(BF16)
    w_hi, w_lo = _split2(wba_ref[...])
    ba = _dot(h_hi, w_hi) + _dot(h_lo, w_hi) + _dot(h_hi, w_lo)
    ba_ref[...] = ba[:, :N_GATE_COLS]
    bat_ref[...] = ba.T[:N_GATE_COLS]


def _inproj(x2d, mod, rows_per_mod, ng0, wqkv, wzuv, wba):
    t = x2d.shape[0]
    tm = min(ROW_TILE, t)
    tiles_per_mod = rows_per_mod // tm
    row = lambda i: (i, 0)
    const = lambda i: (0, 0)
    return pl.pallas_call(
        _inproj_kernel,
        out_shape=(jax.ShapeDtypeStruct((t, QKV_COLS), BF16),
                   jax.ShapeDtypeStruct((t, GDN_W), BF16),
                   jax.ShapeDtypeStruct((t, SGU_W), BF16),
                   jax.ShapeDtypeStruct((t, SGU_W), BF16),
                   jax.ShapeDtypeStruct((t, N_GATE_COLS), F32),
                   jax.ShapeDtypeStruct((N_GATE_COLS, t), F32)),
        grid=(t // tm,),
        in_specs=[pl.BlockSpec((tm, D_MODEL), row),
                  pl.BlockSpec((1, 6, D_MODEL), lambda i: (i // tiles_per_mod, 0, 0)),
                  pl.BlockSpec((1, D_MODEL), const),
                  pl.BlockSpec(wqkv.shape, const),
                  pl.BlockSpec(wzuv.shape, const),
                  pl.BlockSpec(wba.shape, const)],
        out_specs=(pl.BlockSpec((tm, QKV_COLS), row),
                   pl.BlockSpec((tm, GDN_W), row),
                   pl.BlockSpec((tm, SGU_W), row),
                   pl.BlockSpec((tm, SGU_W), row),
                   pl.BlockSpec((tm, N_GATE_COLS), row),
                   pl.BlockSpec((N_GATE_COLS, tm), lambda i: (0, i))),
        compiler_params=_params("parallel"),
        name="in_proj",
    )(x2d, mod, ng0, wqkv, wzuv, wba)


def _gdn_prep_kernel(row_len, qkv_ref, cw_ref, ba_ref, bat_ref, alog_r_ref, dtb_r_ref,
                     alog_c_ref, dtb_c_ref, u_ref, w_ref, qd_ref, kd_ref, at_ref, gl_ref):
    n = GDN_BLOCK
    c = DELTA_CHUNK
    ri = lax.broadcasted_iota(jnp.int32, (n, n), 0)
    ci = lax.broadcasted_iota(jnp.int32, (n, n), 1)
    same = (ri // c) == (ci // c)
    lower = same & (ri >= ci)
    upper = same & (ri <= ci)

    def mask01(m):
        return jnp.where(m, 1.0, 0.0).astype(BF16)

    lower_b = mask01(lower)
    upper_b = mask01(upper)
    same_b = mask01(same)

    wi = lax.broadcasted_iota(jnp.int32, (c, n), 0)
    wl = lax.broadcasted_iota(jnp.int32, (c, n), 1)
    wchunk = wl // c
    wj = wl % c
    lower_w = wi >= wj
    upper_w = wi <= wj
    diag_w = wi == wj
    eye_w = jnp.where(diag_w, 1.0, 0.0)
    pair_masks = []
    s = 1
    while s < c:
        pair_masks.append(mask01(((wi // (2 * s)) == (wj // (2 * s))) & ((wi // s) != (wj // s))))
        s *= 2

    def to_wide(full):
        out = full[:c]
        for k in range(1, CHUNKS_PER_BLOCK):
            out = jnp.where(wchunk == k, full[k * c:(k + 1) * c], out)
        return out

    def col_wide(col):
        out = jnp.broadcast_to(col[:c], (c, n))
        for k in range(1, CHUNKS_PER_BLOCK):
            out = jnp.where(wchunk == k, jnp.broadcast_to(col[k * c:(k + 1) * c], (c, n)), out)
        return out

    def block_diag(x_w):
        return jnp.concatenate([x_w] * CHUNKS_PER_BLOCK, axis=0) * same_b

    ba = ba_ref[0]
    bat = bat_ref[...]
    beta_c = _sigmoid(ba[:, :N_CHAINS])
    g_c = -jnp.exp(alog_r_ref[...]) * _softplus(ba[:, N_CHAINS:] + dtb_r_ref[...])
    g_r = -jnp.exp(alog_c_ref[...]) * _softplus(bat[N_CHAINS:] + dtb_c_ref[...])
    gc3 = _split3(g_c)
    gr3 = jnp.concatenate(_split3(g_r), axis=0)

    def sum3_r(m):
        return m[:N_CHAINS] + m[N_CHAINS:2 * N_CHAINS] + m[2 * N_CHAINS:]

    cum_f_c = _dot(lower_b, gc3[0]) + _dot(lower_b, gc3[1]) + _dot(lower_b, gc3[2])
    tot_c = _dot(same_b, gc3[0]) + _dot(same_b, gc3[1]) + _dot(same_b, gc3[2])
    cum_b_c = tot_c - cum_f_c + g_c
    cum_f_r = sum3_r(_dot(gr3, upper_b))
    cum_b_r = sum3_r(_dot(gr3, lower_b))
    ei = lax.broadcasted_iota(jnp.int32, (n, CHUNKS_PER_BLOCK * HEAD_DIM), 0) // c
    ej = lax.broadcasted_iota(jnp.int32, (n, CHUNKS_PER_BLOCK * HEAD_DIM), 1) // HEAD_DIM
    g_last = jnp.exp(sum3_r(_dot(gr3, mask01(ei == ej))))
    gl_ref[0, 0, 0] = g_last[:GDN_HEADS]
    gl_ref[0, 0, 1] = g_last[GDN_HEADS:]

    pos = lax.broadcasted_iota(jnp.int32, (n, HEAD_DIM), 0) % row_len
    first = pos == 0
    last = pos == row_len - 1

    def conv_silu(col):
        x = qkv_ref[0, :, col * HEAD_DIM:(col + 1) * HEAD_DIM].astype(F32)
        cw = cw_ref[:, col * HEAD_DIM:(col + 1) * HEAD_DIM]
        xp = jnp.where(first, 0.0, pltpu.roll(x, 1, 0))
        xn = jnp.where(last, 0.0, pltpu.roll(x, n - 1, 0))
        y = xp * cw[0:1] + x * cw[1:2] + xn * cw[2:3]
        return y * _sigmoid(y)

    def l2n(x):
        return x * lax.rsqrt(jnp.sum(x * x, -1, keepdims=True) + NORM_EPS)

    a_bs, ps, rhss = [None] * N_CHAINS, [None] * N_CHAINS, [None] * N_CHAINS
    for h in range(GDN_HEADS):
        q = l2n(conv_silu(h)) * (HEAD_DIM ** -0.5)
        k = l2n(conv_silu(GDN_HEADS + h))
        v = conv_silu(2 * GDN_HEADS + h)
        k_b = k.astype(BF16)
        qk_kk = _dot_nt(jnp.concatenate([q.astype(BF16), k_b], axis=0), k_b)
        qk_w = to_wide(qk_kk[:n])
        kk_w = to_wide(qk_kk[n:])
        for d in range(2):
            j = d * GDN_HEADS + h
            mask_w = lower_w if d == 0 else upper_w
            cum_c = (cum_f_c if d == 0 else cum_b_c)[:, j:j + 1]
            cum_r = (cum_f_r if d == 0 else cum_b_r)[j:j + 1, :]
            b_c = beta_c[:, j:j + 1]
            decay_w = jnp.where(mask_w, jnp.exp(jnp.where(mask_w, col_wide(cum_c) - cum_r, 0.0)), 0.0)
            amat_w = jnp.where(diag_w, 0.0, kk_w * decay_w * col_wide(b_c))
            a_bs[j] = amat_w.astype(BF16)
            ps[j] = eye_w - amat_w * pair_masks[0].astype(F32)
            e_c = jnp.exp(cum_c)
            rhss[j] = jnp.concatenate([(v * b_c).astype(BF16), (k * (b_c * e_c)).astype(BF16)], axis=1)
            cols = slice(j * HEAD_DIM, (j + 1) * HEAD_DIM)
            qd_ref[0, :, cols] = (q * e_c).astype(BF16)
            kd_ref[0, :, cols] = (k * jnp.exp(tot_c[:, j:j + 1] - cum_c)).astype(BF16)
            at_ref[0, 0, j * c:(j + 1) * c, :] = (qk_w * decay_w).astype(BF16)

    for pm in pair_masks[1:]:
        p_bs = [p.astype(BF16) for p in ps]
        ys = [_dot(a_bs[j] * pm, block_diag(p_bs[j])) for j in range(N_CHAINS)]
        ps = [ps[j] - _dot(p_bs[j], block_diag(ys[j].astype(BF16))) for j in range(N_CHAINS)]

    for j in range(N_CHAINS):
        uw = _dot(block_diag(ps[j].astype(BF16)), rhss[j])
        cols = slice(j * HEAD_DIM, (j + 1) * HEAD_DIM)
        u_ref[0, :, cols] = uw[:, :HEAD_DIM].astype(BF16)
        w_ref[0, :, cols] = uw[:, HEAD_DIM:].astype(BF16)


def _gdn_prep(qkv, conv_w, ba, bat, alog, dtb, row_len):
    b, l, _ = qkv.shape
    nblk = l // GDN_BLOCK
    wide = N_CHAINS * HEAD_DIM
    blk = lambda bi, i: (bi, i, 0)
    const = lambda bi, i: (0, 0)
    alog_r, dtb_r = alog.reshape(1, N_CHAINS), dtb.reshape(1, N_CHAINS)
    alog_c, dtb_c = alog.reshape(N_CHAINS, 1), dtb.reshape(N_CHAINS, 1)
    return pl.pallas_call(
        functools.partial(_gdn_prep_kernel, row_len),
        out_shape=(jax.ShapeDtypeStruct((b, l, wide), BF16),) * 4 + (
            jax.ShapeDtypeStruct((b, nblk, N_CHAINS * DELTA_CHUNK, GDN_BLOCK), BF16),
            jax.ShapeDtypeStruct((b, nblk, 2, GDN_HEADS, CHUNKS_PER_BLOCK * HEAD_DIM), F32)),
        grid=(b, nblk),
        in_specs=[pl.BlockSpec((1, GDN_BLOCK, QKV_COLS), blk),
                  pl.BlockSpec((3, QKV_COLS), const),
                  pl.BlockSpec((1, GDN_BLOCK, N_GATE_COLS), blk),
                  pl.BlockSpec((N_GATE_COLS, GDN_BLOCK), lambda bi, i: (0, bi * nblk + i)),
                  pl.BlockSpec((1, N_CHAINS), const),
                  pl.BlockSpec((1, N_CHAINS), const),
                  pl.BlockSpec((N_CHAINS, 1), const),
                  pl.BlockSpec((N_CHAINS, 1), const)],
        out_specs=(pl.BlockSpec((1, GDN_BLOCK, wide), blk),) * 4 + (
            pl.BlockSpec((1, 1, N_CHAINS * DELTA_CHUNK, GDN_BLOCK), lambda bi, i: (bi, i, 0, 0)),
            pl.BlockSpec((1, 1, 2, GDN_HEADS, CHUNKS_PER_BLOCK * HEAD_DIM),
                         lambda bi, i: (bi, i, 0, 0, 0))),
        compiler_params=_params("parallel", "parallel"),
        name="gdn_prep",
    )(qkv, conv_w, ba, bat, alog_r, dtb_r, alog_c, dtb_c)


def _gdn_scan_kernel(uf, wf, qf, kf, af, gf, ub, wb, qb, kb, ab, gb, s0_ref,
                     of_ref, ob_ref, sfin_ref, s_scr):
    i = pl.program_id(0)
    c = DELTA_CHUNK
    n_batch = s0_ref.shape[0]

    @pl.when(i == 0)
    def _():
        s_scr[...] = s0_ref[...]

    ops = ((uf, wf, qf, kf, af, gf, of_ref), (ub, wb, qb, kb, ab, gb, ob_ref))
    chains = [(bi, d, h) for bi in range(n_batch) for d in range(2) for h in range(GDN_HEADS)]
    states = [s_scr[bi, d * GDN_HEADS + h] for bi, d, h in chains]
    for step in range(CHUNKS_PER_BLOCK):
        def chunk(d):
            cc = step if d == 0 else CHUNKS_PER_BLOCK - 1 - step
            return cc, slice(cc * c, (cc + 1) * c)

        xs = []
        for j, (bi, d, h) in enumerate(chains):
            _, rows = chunk(d)
            cols = slice(h * HEAD_DIM, (h + 1) * HEAD_DIM)
            wq = jnp.concatenate([ops[d][1][bi, rows, cols], ops[d][2][bi, rows, cols]], axis=0)
            xs.append(_dot(wq, states[j].astype(BF16)))
        v_news = []
        for j, (bi, d, h) in enumerate(chains):
            _, rows = chunk(d)
            cols = slice(h * HEAD_DIM, (h + 1) * HEAD_DIM)
            v_news.append((ops[d][0][bi, rows, cols].astype(F32) - xs[j][:c]).astype(BF16))
        for j, (bi, d, h) in enumerate(chains):
            cc, rows = chunk(d)
            cols = slice(h * HEAD_DIM, (h + 1) * HEAD_DIM)
            a_c = ops[d][4][bi, 0, h * c:(h + 1) * c, cc * c:(cc + 1) * c]
            ops[d][6][bi, rows, cols] = xs[j][c:] + _dot(a_c, v_news[j])
            ds = _dot_tn(ops[d][3][bi, rows, cols], v_news[j])
            g_last = ops[d][5][bi, 0, 0, h:h + 1, cc * HEAD_DIM:(cc + 1) * HEAD_DIM]
            states[j] = states[j] * g_last + ds
    for j, (bi, d, h) in enumerate(chains):
        s_scr[bi, d * GDN_HEADS + h] = states[j]

    @pl.when(i == pl.num_programs(0) - 1)
    def _():
        sfin_ref[...] = s_scr[...]


def _gdn_scan(u, w, qd, kd, at, gl, s0):
    b, l, _ = u.shape
    nblk = l // GDN_BLOCK
    half = GDN_HEADS * HEAD_DIM
    fwd = lambda i: (0, i, 0)
    bwd = lambda i: (0, nblk - 1 - i, 1)
    big = lambda m: pl.BlockSpec((b, GDN_BLOCK, half), m)
    att_shape = (b, 1, GDN_HEADS * DELTA_CHUNK, GDN_BLOCK)
    attf = pl.BlockSpec(att_shape, lambda i: (0, i, 0, 0))
    attb = pl.BlockSpec(att_shape, lambda i: (0, nblk - 1 - i, 1, 0))
    gl_shape = (b, 1, 1, GDN_HEADS, CHUNKS_PER_BLOCK * HEAD_DIM)
    glf = pl.BlockSpec(gl_shape, lambda i: (0, i, 0, 0, 0))
    glb = pl.BlockSpec(gl_shape, lambda i: (0, nblk - 1 - i, 1, 0, 0))
    state = pl.BlockSpec((b, N_CHAINS, HEAD_DIM, HEAD_DIM), lambda i: (0, 0, 0, 0))
    return pl.pallas_call(
        _gdn_scan_kernel,
        out_shape=(jax.ShapeDtypeStruct((b, l, half), F32),
                   jax.ShapeDtypeStruct((b, l, half), F32),
                   jax.ShapeDtypeStruct((b, N_CHAINS, HEAD_DIM, HEAD_DIM), F32)),
        grid=(nblk,),
        in_specs=[big(fwd), big(fwd), big(fwd), big(fwd), attf, glf,
                  big(bwd), big(bwd), big(bwd), big(bwd), attb, glb, state],
        out_specs=(pl.BlockSpec((b, GDN_BLOCK, half), fwd),
                   pl.BlockSpec((b, GDN_BLOCK, half), lambda i: (0, nblk - 1 - i, 0)),
                   state),
        scratch_shapes=[pltpu.VMEM((b, N_CHAINS, HEAD_DIM, HEAD_DIM), F32)],
        compiler_params=_params("arbitrary"),
        name="gdn_scan",
    )(u, w, qd, kd, at, gl, u, w, qd, kd, at, gl, s0)


def _mix_into(mix_ref, of_ref, ob_ref, z_ref, u_ref, v_ref, gng_ref, lng_ref, lnb_ref, ws_ref,
              bs_ref):
    tm = of_ref.shape[0]
    o = of_ref[...] + ob_ref[...]
    z = z_ref[...].astype(F32)
    for h in range(GDN_HEADS):
        cols = slice(h * HEAD_DIM, (h + 1) * HEAD_DIM)
        oh = o[:, cols]
        zh = z[:, cols]
        r = lax.rsqrt(jnp.mean(oh * oh, -1, keepdims=True) + NORM_EPS)
        mix_ref[:, cols] = (oh * r * gng_ref[...] * (zh * _sigmoid(zh))).astype(BF16)
    for g in range(SGU_GROUPS):
        cols = slice(g * SGU_GROUP, (g + 1) * SGU_GROUP)
        vg = v_ref[:, cols].astype(F32)
        vc = vg - jnp.mean(vg, -1, keepdims=True)
        vn = vc * lax.rsqrt(jnp.mean(vc * vc, -1, keepdims=True) + NORM_EPS)
        vn = (vn * lng_ref[g:g + 1] + lnb_ref[g:g + 1]).astype(BF16)
        wsg = ws_ref[g].astype(BF16)
        for n in range(tm // SGU_CHUNK):
            rows = slice(n * SGU_CHUNK, (n + 1) * SGU_CHUNK)
            s = _dot(wsg, vn[rows]) + bs_ref[g]
            mix_ref[rows, GDN_W + g * SGU_GROUP:GDN_W + (g + 1) * SGU_GROUP] = (
                u_ref[rows, cols].astype(F32) * s).astype(BF16)


def _outproj_kernel(of_ref, ob_ref, z_ref, u_ref, v_ref, gng_ref, lng_ref, lnb_ref, ws_ref, bs_ref,
                    x_ref, mod_ref, ng_ref, wout_ref, rwt_ref, rb_ref,
                    x1_ref, h2_ref, lt_ref, mix_scr):
    _mix_into(mix_scr, of_ref, ob_ref, z_ref, u_ref, v_ref, gng_ref, lng_ref, lnb_ref, ws_ref, bs_ref)
    mod = mod_ref[0]
    y = _dot(mix_scr[...], wout_ref[...])
    x1 = x_ref[...] + mod[2:3] * _rms(y, ng_ref[1:2])
    x1_ref[...] = x1
    h2 = _rms(x1, ng_ref[2:3]) * (1.0 + mod[4:5]) + mod[3:4]
    h_hi, h_lo = _split2(h2)
    _store_planes(h2_ref, _pack_rows(h2))
    w_hi, w_lo = _split2(rwt_ref[...])
    lt_ref[...] = _dot_nt(w_hi, h_hi) + _dot_nt(w_lo, h_hi) + _dot_nt(w_hi, h_lo) + rb_ref[...]


def _outproj(o_f, o_b, z, ug, vg, gdn_norm_g, ln_g, ln_b, w_s, b_s_full,
             x2d, mod, rows_per_mod, ng, wout, rwt, rb_col):
    t = x2d.shape[0]
    tm = min(ROW_TILE, t)
    tiles_per_mod = rows_per_mod // tm
    row = lambda i: (i, 0)
    const = lambda i: (0, 0)
    c3 = lambda i: (0, 0, 0)
    return pl.pallas_call(
        _outproj_kernel,
        out_shape=(jax.ShapeDtypeStruct((t, D_MODEL), F32),
                   jax.ShapeDtypeStruct((N_PLANES, t, SC_ROW_W), U32),
                   jax.ShapeDtypeStruct((N_EXPERTS, t), F32)),
        grid=(t // tm,),
        in_specs=[pl.BlockSpec((tm, GDN_W), row), pl.BlockSpec((tm, GDN_W), row),
                  pl.BlockSpec((tm, GDN_W), row), pl.BlockSpec((tm, SGU_W), row),
                  pl.BlockSpec((tm, SGU_W), row),
                  pl.BlockSpec((1, HEAD_DIM), const),
                  pl.BlockSpec((SGU_GROUPS, SGU_GROUP), const),
                  pl.BlockSpec((SGU_GROUPS, SGU_GROUP), const),
                  pl.BlockSpec((SGU_GROUPS, SGU_CHUNK, SGU_CHUNK), c3),
                  pl.BlockSpec((SGU_GROUPS, SGU_CHUNK, SGU_GROUP), c3),
                  pl.BlockSpec((tm, D_MODEL), row),
                  pl.BlockSpec((1, 6, D_MODEL), lambda i: (i // tiles_per_mod, 0, 0)),
                  pl.BlockSpec((4, D_MODEL), const),
                  pl.BlockSpec((D_MODEL, D_MODEL), const),
                  pl.BlockSpec((N_EXPERTS, D_MODEL), const),
                  pl.BlockSpec((N_EXPERTS, 1), const)],
        out_specs=(pl.BlockSpec((tm, D_MODEL), row),
                   pl.BlockSpec((N_PLANES, tm, SC_ROW_W), lambda i: (0, i, 0)),
                   pl.BlockSpec((N_EXPERTS, tm), lambda i: (0, i))),
        compiler_params=_params("parallel"),
        scratch_shapes=[pltpu.VMEM((tm, D_MODEL), BF16)],
        name="out_proj_router",
    )(o_f, o_b, z, ug, vg, gdn_norm_g, ln_g, ln_b, w_s, b_s_full, x2d, mod, ng, wout, rwt, rb_col)


def _moe_kernel(be_ref, nb_ref, xb_ref, wgu_ref, bgu_ref, wd_ref, bd_ref, y_ref,
                wgu_b, wd_b, gut_scr):
    i = pl.program_id(0)
    live = i < nb_ref[0]
    new_expert = jnp.logical_or(i == 0, be_ref[i] != be_ref[jnp.maximum(i - 1, 0)])

    @pl.when(jnp.logical_and(live, new_expert))
    def _():
        wgu_b[...] = wgu_ref[0].astype(BF16)
        wd_b[...] = wd_ref[0].astype(BF16)

    @pl.when(live)
    def _():
        xb = _unpack_rows(_load_planes(xb_ref)).astype(BF16)
        gu = _dot(xb, wgu_b[...]) + bgu_ref[0]
        gu_t = gu.T
        acts = []
        for part in range(MOE_ROWS // LANES):
            part_ref = gut_scr.at[part]
            part_ref[...] = gu_t[:, part * LANES:(part + 1) * LANES]
            gate = jnp.minimum(part_ref[pl.ds(0, D_FF, stride=2), :], SWIGLU_LIMIT)
            up = jnp.clip(part_ref[pl.ds(1, D_FF, stride=2), :], -SWIGLU_LIMIT, SWIGLU_LIMIT)
            acts.append(((up + 1.0) * gate * _sigmoid(SWIGLU_ALPHA * gate)).astype(BF16))
        act_t = jnp.concatenate(acts, axis=1)
        _store_planes(y_ref, _pack_rows(_dot_tn(act_t, wd_b[...]) + bd_ref[0]))

    @pl.when(jnp.logical_not(live))
    def _():
        y_ref[...] = jnp.zeros_like(y_ref)


def _moe_ffn(block_e, n_used, xb, w_gu, b_gu, w_down, b_down):
    n_rows = xb.shape[1]
    n_blocks = n_rows // MOE_ROWS
    row = lambda i, be, nb: (0, i, 0)
    ex3 = lambda i, be, nb: (be[i], 0, 0)
    live_row = lambda i, be, nb: (0, jnp.minimum(i, nb[0] - 1), 0)
    planes = (N_PLANES, MOE_ROWS, SC_ROW_W)
    grid_spec = pltpu.PrefetchScalarGridSpec(
        num_scalar_prefetch=2,
        grid=(n_blocks,),
        in_specs=[pl.BlockSpec(planes, live_row),
                  pl.BlockSpec((1, D_MODEL, 2 * D_FF), ex3),
                  pl.BlockSpec((1, 1, 2 * D_FF), ex3),
                  pl.BlockSpec((1, D_FF, D_MODEL), ex3),
                  pl.BlockSpec((1, 1, D_MODEL), ex3)],
        out_specs=pl.BlockSpec(planes, row),
        scratch_shapes=[pltpu.VMEM((D_MODEL, 2 * D_FF), BF16),
                        pltpu.VMEM((D_FF, D_MODEL), BF16),
                        pltpu.VMEM((MOE_ROWS // LANES, 2 * D_FF, LANES), F32)],
    )
    return pl.pallas_call(
        _moe_kernel,
        out_shape=jax.ShapeDtypeStruct((N_PLANES, n_rows, SC_ROW_W), U32),
        grid_spec=grid_spec,
        compiler_params=_params("arbitrary"),
        name="moe_ffn",
    )(block_e, n_used, xb, w_gu, b_gu, w_down, b_down)


def _combine_kernel(y0_ref, y1_ref, y2_ref, y3_ref, gt_ref, x1_ref, mod_ref, ng_ref, o_ref):
    mod = mod_ref[0]
    gt = gt_ref[...]
    y = _unpack_rows(_load_planes(y0_ref)) * gt[:, 0:1]
    for k, y_ref in ((1, y1_ref), (2, y2_ref), (3, y3_ref)):
        y = y + _unpack_rows(_load_planes(y_ref)) * gt[:, k:k + 1]
    o_ref[...] = x1_ref[...] + mod[5:6] * _rms(y, ng_ref[3:4])


def _combine(yg, gates, x1, mod, rows_per_mod, ng):
    t = x1.shape[0]
    tm = min(ROW_TILE, t)
    tiles_per_mod = rows_per_mod // tm
    n_tiles = t // tm
    row = lambda i: (i, 0)
    choice = lambda k: pl.BlockSpec((N_PLANES, tm, SC_ROW_W), lambda i: (0, k * n_tiles + i, 0))
    return pl.pallas_call(
        _combine_kernel,
        out_shape=jax.ShapeDtypeStruct((t, D_MODEL), F32),
        grid=(n_tiles,),
        in_specs=[choice(0), choice(1), choice(2), choice(3),
                  pl.BlockSpec((tm, TOP_K), row),
                  pl.BlockSpec((tm, D_MODEL), row),
                  pl.BlockSpec((1, 6, D_MODEL), lambda i: (i // tiles_per_mod, 0, 0)),
                  pl.BlockSpec((4, D_MODEL), lambda i: (0, 0))],
        out_specs=pl.BlockSpec((tm, D_MODEL), row),
        compiler_params=_params("parallel"),
        name="moe_combine",
    )(yg, yg, yg, yg, gates, x1, mod, ng)


def _route_kernel(lt_ref, eidx_ref, gate_ref, rank_ref, cnt_ref, carry):
    i = pl.program_id(0)
    tile = lt_ref.shape[1]

    @pl.when(i == 0)
    def _():
        carry[...] = jnp.zeros_like(carry)

    logits = lt_ref[...]
    eio = lax.broadcasted_iota(jnp.int32, (N_EXPERTS, tile), 0).astype(F32)
    vals, sels = [], []
    for k in range(TOP_K):
        m = jnp.max(logits, axis=0, keepdims=True)
        idx = jnp.min(jnp.where(logits == m, eio, float(N_EXPERTS)), axis=0, keepdims=True)
        sel = eio == idx
        logits = jnp.where(sel, -jnp.inf, logits)
        vals.append(m)
        sels.append(sel)
        eidx_ref[k:k + 1, :] = idx.astype(jnp.int32)
    exps = [jnp.exp(v - vals[0]) for v in vals]
    denom = exps[0] + exps[1] + exps[2] + exps[3]
    for k in range(TOP_K):
        gate_ref[k:k + 1, :] = exps[k] / denom

    member = jnp.where(sels[0] | sels[1] | sels[2] | sels[3], 1.0, 0.0)
    ti = lax.broadcasted_iota(jnp.int32, (tile, tile), 0)
    tj = lax.broadcasted_iota(jnp.int32, (tile, tile), 1)
    earlier = jnp.where(ti < tj, 1.0, 0.0).astype(BF16)
    before = _dot(member.astype(BF16), earlier) + carry[...]
    for k in range(TOP_K):
        rank_ref[k:k + 1, :] = jnp.sum(jnp.where(sels[k], before, 0.0), axis=0,
                                       keepdims=True).astype(jnp.int32)
    carry[...] = carry[...] + jnp.sum(member, axis=1, keepdims=True)
    cnt_ref[...] = carry[...].astype(jnp.int32)


def _route(logits_t):
    t = logits_t.shape[1]
    tile = min(ROUTE_TILE, t)
    blk = lambda i: (0, i)
    return pl.pallas_call(
        _route_kernel,
        out_shape=(jax.ShapeDtypeStruct((TOP_K, t), jnp.int32),
                   jax.ShapeDtypeStruct((TOP_K, t), F32),
                   jax.ShapeDtypeStruct((TOP_K, t), jnp.int32),
                   jax.ShapeDtypeStruct((N_EXPERTS, 1), jnp.int32)),
        grid=(t // tile,),
        in_specs=[pl.BlockSpec((N_EXPERTS, tile), blk)],
        out_specs=(pl.BlockSpec((TOP_K, tile), blk), pl.BlockSpec((TOP_K, tile), blk),
                   pl.BlockSpec((TOP_K, tile), blk),
                   pl.BlockSpec((N_EXPERTS, 1), lambda i: (0, 0))),
        scratch_shapes=[pltpu.VMEM((N_EXPERTS, 1), F32)],
        compiler_params=_params("arbitrary"),
        name="moe_route",
    )(logits_t)


def _slot_tables(eidx, rank, counts, n_blocks):
    padded = (counts + MOE_ROWS - 1) // MOE_ROWS * MOE_ROWS
    pad_end = jnp.cumsum(padded)
    pad_start = pad_end - padded
    experts = jnp.arange(N_EXPERTS, dtype=jnp.int32)
    dest = rank + jnp.sum(jnp.where(eidx[..., None] == experts, pad_start, 0), axis=-1)
    first_row = jnp.arange(n_blocks, dtype=jnp.int32)[:, None] * MOE_ROWS
    block_e = jnp.minimum(jnp.sum((pad_end[None, :] <= first_row).astype(jnp.int32), axis=1),
                          N_EXPERTS - 1)
    n_used = pad_end[-1:] // MOE_ROWS
    return dest.astype(jnp.int32), pad_start.astype(jnp.int32), block_e, n_used.astype(jnp.int32)


def _sc_mesh():
    return plsc.VectorSubcoreMesh(core_axis_name="c", subcore_axis_name="s",
                                  num_cores=SC_CORES, num_subcores=SC_SUBCORES)


def _plane_row_ids(rows, rows_per_plane):
    return jnp.concatenate([rows + p * rows_per_plane for p in range(N_PLANES)], axis=-1)


def _sc_gather_rows(table, rows):
    v = table.shape[1]
    idx = _plane_row_ids(rows, v)[None]
    n_all = idx.shape[1]

    @functools.partial(pl.kernel, mesh=_sc_mesh(), name="moe_gather_rows",
                       out_type=jax.ShapeDtypeStruct((n_all, SC_ROW_W), U32))
    def gather(x_hbm, i_hbm, o_hbm):
        def body(i_vmem, o_vmem):
            pltpu.sync_copy(x_hbm.at[i_vmem.at[0]], o_vmem)

        pltpu.emit_pipeline(
            body, grid=(n_all // SC_WINDOW,),
            in_specs=[pl.BlockSpec((1, SC_WINDOW), lambda i: (0, i))],
            out_specs=[pl.BlockSpec((SC_WINDOW, SC_ROW_W), lambda i: (i, 0))],
            core_axis_name=("c", "s"), dimension_semantics=(pltpu.PARALLEL,),
        )(i_hbm, o_hbm)

    return gather(table.reshape(N_PLANES * v, SC_ROW_W), idx).reshape(N_PLANES, -1, SC_ROW_W)


def _sc_scatter_rows(rows, dest, n_out):
    t = rows.shape[1]
    idx = _plane_row_ids(dest, n_out)

    @functools.partial(pl.kernel, mesh=_sc_mesh(), name="moe_scatter_rows", scratch_types=[],
                       out_type=jax.ShapeDtypeStruct((N_PLANES * n_out, SC_ROW_W), U32))
    def scatter(x_hbm, i_hbm, o_hbm):
        def body(x_vmem, i_vmem):
            for k in range(TOP_K):
                pltpu.sync_copy(x_vmem, o_hbm.at[i_vmem.at[k]])

        pltpu.emit_pipeline(
            body, grid=(N_PLANES * t // SC_WINDOW,),
            in_specs=[pl.BlockSpec((SC_WINDOW, SC_ROW_W), lambda i: (i, 0)),
                      pl.BlockSpec((TOP_K, SC_WINDOW), lambda i: (0, i))],
            out_specs=[],
            core_axis_name=("c", "s"), dimension_semantics=(pltpu.PARALLEL,),
        )(x_hbm, i_hbm)

    return scatter(rows.reshape(N_PLANES * t, SC_ROW_W), idx).reshape(N_PLANES, n_out, SC_ROW_W)


def _zero_pad_kernel(cnt_ref, start_ref, xb_in_ref, xb_ref, zero_scr, sem):
    del xb_in_ref
    zero_scr[...] = jnp.zeros_like(zero_scr)

    pieces = [SUBLANES << bit for bit in range((MOE_ROWS // SUBLANES - 1).bit_length())]

    def zero_copy(p, row, size):
        return pltpu.make_async_copy(zero_scr.at[pl.ds(0, size)], xb_ref.at[p, pl.ds(row, size)], sem)

    def for_each_piece(fn):
        def per_expert(e, carry):
            n_real = cnt_ref[e]
            n_pad = (MOE_ROWS - n_real % MOE_ROWS) % MOE_ROWS
            first = start_ref[e] + n_real
            n_single = n_pad % SUBLANES
            for j in range(SUBLANES - 1):
                @pl.when(j < n_single)
                def _():
                    for p in range(N_PLANES):
                        fn(zero_copy(p, first + j, 1))
            row = first + n_single
            for size in pieces:
                @pl.when((n_pad & size) != 0)
                def _():
                    for p in range(N_PLANES):
                        fn(zero_copy(p, pl.multiple_of(row, SUBLANES), size))
                row = row + (n_pad & size)
            return carry
        lax.fori_loop(0, N_EXPERTS, per_expert, 0)

    for_each_piece(lambda copy: copy.start())
    for_each_piece(lambda copy: copy.wait())


def _zero_pad_slots(counts, pad_start, xb):
    grid_spec = pltpu.PrefetchScalarGridSpec(
        num_scalar_prefetch=2,
        grid=(1,),
        in_specs=[pl.BlockSpec(memory_space=pl.ANY)],
        out_specs=pl.BlockSpec(memory_space=pl.ANY),
        scratch_shapes=[pltpu.VMEM((MOE_ROWS // 2, SC_ROW_W), U32), pltpu.SemaphoreType.DMA],
    )
    return pl.pallas_call(
        _zero_pad_kernel,
        out_shape=jax.ShapeDtypeStruct(xb.shape, xb.dtype),
        grid_spec=grid_spec,
        input_output_aliases={2: 0},
        compiler_params=_params("arbitrary"),
        name="moe_zero_pad",
    )(counts, pad_start, xb)


def kernel(x, c, ctx, c_ctx, w_ada, b_ada, norm_g, w_in, conv_w, a_log, dt_bias, gdn_norm_g,
           sgu_ln_g, sgu_ln_b, sgu_w, sgu_b, w_out, router_w, router_b, w_gu, b_gu, w_down, b_down):
    b, l, d = x.shape
    lc = ctx.shape[1]
    t = b * l
    assert d == D_MODEL and l % ROW_TILE == 0 and l % GDN_BLOCK == 0 and lc % GDN_BLOCK == 0
    assert w_ada.shape[0] == 1, "single-layer block"

    cs = jnp.concatenate([c, c_ctx[None], jnp.zeros((8 - b - 1, d), F32)], axis=0)
    mod_all = _ada(cs, w_ada[0], b_ada[0][None])
    mod = mod_all[:b].reshape(b, 6, d)
    mod_c = mod_all[b:b + 1].reshape(1, 6, d)
    ng = norm_g[0]

    w = w_in[0]
    wqkv = w[:, :QKV_COLS].astype(BF16)
    wba = jnp.pad(w[:, QKV_COLS:QKV_COLS + N_GATE_COLS], ((0, 0), (0, LANES - N_GATE_COLS)))
    wzuv = w[:, QKV_COLS + N_GATE_COLS:].astype(BF16)

    x2d = x.reshape(t, d)
    qkv, z, ug, vg, ba, bat = _inproj(x2d, mod, l, ng[0:1], wqkv, wzuv, wba)
    ctx2d = ctx.reshape(b * lc, d)
    qkv_c, _, _, _, ba_c, bat_c = _inproj(ctx2d, mod_c, b * lc, ng[0:1], wqkv, wzuv, wba)

    alog = a_log[0].reshape(-1)
    dtb = dt_bias[0].reshape(-1)
    pc = _gdn_prep(qkv_c.reshape(b, lc, QKV_COLS), conv_w[0], ba_c.reshape(b, lc, N_GATE_COLS),
                   bat_c, alog, dtb, lc)
    s_zero = jnp.zeros((b, N_CHAINS, HEAD_DIM, HEAD_DIM), F32)
    _, _, s_ctx = _gdn_scan(*pc, s_zero)
    pp = _gdn_prep(qkv.reshape(b, l, QKV_COLS), conv_w[0], ba.reshape(b, l, N_GATE_COLS),
                   bat, alog, dtb, GRID_W)
    o_f, o_b, _ = _gdn_scan(*pp, s_ctx)

    b_s_full = jnp.broadcast_to(sgu_b[0][:, :, None], (SGU_GROUPS, SGU_CHUNK, SGU_GROUP))
    x1, h2p, logits_t = _outproj(o_f.reshape(t, GDN_W), o_b.reshape(t, GDN_W), z, ug, vg,
                                 gdn_norm_g, sgu_ln_g[0], sgu_ln_b[0], sgu_w[0], b_s_full,
                                 x2d, mod, l, ng, w_out[0].astype(BF16),
                                 router_w[0].T, router_b[0][:, None])

    eidx, gates_t, rank, counts = _route(logits_t)
    counts = counts[:, 0]
    n_blocks = -(-(t * TOP_K) // MOE_ROWS) + N_EXPERTS
    dest, pad_start, block_e, n_used = _slot_tables(eidx, rank, counts, n_blocks)
    xb = _sc_scatter_rows(h2p, dest, n_blocks * MOE_ROWS)
    xb = _zero_pad_slots(counts, pad_start, xb)
    yb = _moe_ffn(block_e, n_used, xb, w_gu[0], b_gu[0][:, None, :], w_down[0],
                  b_down[0][:, None, :])
    yg = _sc_gather_rows(yb, dest.reshape(-1))
    out = _combine(yg, gates_t.T, x1, mod, l, ng)
    return out.reshape(b, l, d)
```

```python
import functools
import math

import jax
import jax.numpy as jnp
from jax import lax
from jax.experimental import pallas as pl
from jax.experimental.pallas import tpu as pltpu
from jax.experimental.pallas import tpu_sc as plsc

F32 = jnp.float32
BF16 = jnp.bfloat16

D_MODEL = 1024
GDN_HEADS = 4
HEAD_DIM = 128
GDN_W = GDN_HEADS * HEAD_DIM
SGU_GROUPS = 4
SGU_GROUP = 128
SGU_W = SGU_GROUPS * SGU_GROUP
SGU_CHUNK = 128
DELTA_CHUNK = 64
GRID_W = 64
N_EXPERTS = 32
TOP_K = 4
D_FF = 1024
SWIGLU_LIMIT = 7.0
SWIGLU_ALPHA = 1.702
NORM_EPS = 1e-6
QKV_COLS = 3 * GDN_W
N_CHAINS = 2 * GDN_HEADS
N_GATE_COLS = 2 * N_CHAINS

ROW_TILE = 1024
SUB_ROWS = 256
COMBINE_TILE = 512
GDN_BLOCK = 256
CHUNKS_PER_BLOCK = GDN_BLOCK // DELTA_CHUNK
MOE_ROWS = 512
U32 = jnp.uint32
LANES = 128
SUBLANES = 8
PACKED_W = D_MODEL // 2
ROUTE_TILE = 512
SC_CORES = 2
SC_SUBCORES = 16
SC_WINDOW = 128
N_PLANES = 2
SC_ROW_W = PACKED_W // N_PLANES
VMEM_LIMIT = 56 * 1024 * 1024


def _params(*sem):
    return pltpu.CompilerParams(dimension_semantics=sem, vmem_limit_bytes=VMEM_LIMIT)


def _dot(a, b):
    return jnp.dot(a, b, preferred_element_type=F32)


def _dot_nt(a, b):
    return lax.dot_general(a, b, (((1,), (1,)), ((), ())), preferred_element_type=F32)


def _dot_tn(a, b):
    return lax.dot_general(a, b, (((0,), (0,)), ((), ())), preferred_element_type=F32)


def _split2(a):
    hi = a.astype(BF16)
    lo = (a - hi.astype(F32)).astype(BF16)
    return hi, lo


def _split3(a):
    hi = a.astype(BF16)
    r = a - hi.astype(F32)
    mid = r.astype(BF16)
    lo = (r - mid.astype(F32)).astype(BF16)
    return hi, mid, lo


def _pack_rows(x32):
    xb = x32.astype(BF16).astype(F32)
    hi = lax.bitcast_convert_type(xb[:, :PACKED_W], U32)
    lo = lax.bitcast_convert_type(xb[:, PACKED_W:], U32)
    return hi | (lo >> 16)


def _store_planes(ref, packed, rows=slice(None)):
    for p in range(N_PLANES):
        ref[p, rows] = packed[:, p * SC_ROW_W:(p + 1) * SC_ROW_W]


def _load_planes(ref, rows=slice(None)):
    return jnp.concatenate([ref[p, rows] for p in range(N_PLANES)], axis=1)


def _sub_tiles(n_rows):
    return [slice(r, r + SUB_ROWS) for r in range(0, n_rows, SUB_ROWS)]


def _unpack_rows(w):
    hi = lax.bitcast_convert_type(w & jnp.uint32(0xFFFF0000), F32)
    lo = lax.bitcast_convert_type(w << 16, F32)
    return jnp.concatenate([hi, lo], axis=1)


def _rms(x32, g):
    return x32 * lax.rsqrt(jnp.mean(x32 * x32, -1, keepdims=True) + NORM_EPS) * g


def _gelu_tanh(x):
    c = math.sqrt(2.0 / math.pi)
    return 0.5 * x * (1.0 + jnp.tanh(c * (x + 0.044715 * (x * x * x))))


def _sigmoid(x):
    return 1.0 / (1.0 + jnp.exp(-x))


def _softplus(x):
    return jnp.maximum(x, 0.0) + jnp.log(1.0 + jnp.exp(-jnp.abs(x)))


def _ada_kernel(c_ref, w_ref, b_ref, o_ref):
    c = c_ref[...]
    s = c * _sigmoid(c)
    s_hi, s_lo = _split2(s)
    w_hi, w_lo = _split2(w_ref[...])
    o_ref[...] = _dot(s_hi, w_hi) + _dot(s_lo, w_hi) + _dot(s_hi, w_lo) + b_ref[...]


def _ada(cs, w_ada, b_ada):
    n = w_ada.shape[1]
    bn = D_MODEL
    return pl.pallas_call(
        _ada_kernel,
        out_shape=jax.ShapeDtypeStruct((cs.shape[0], n), F32),
        grid=(n // bn,),
        in_specs=[pl.BlockSpec(cs.shape, lambda j: (0, 0)),
                  pl.BlockSpec((D_MODEL, bn), lambda j: (0, j)),
                  pl.BlockSpec((1, bn), lambda j: (0, j))],
        out_specs=pl.BlockSpec((cs.shape[0], bn), lambda j: (0, j)),
        compiler_params=_params("parallel"),
        name="ada_mod",
    )(cs, w_ada, b_ada)


def _inproj_kernel(x_ref, mod_ref, g_ref, wqkv_ref, wzuv_ref, wba_ref,
                   qkv_ref, z_ref, u_ref, v_ref, ba_ref, bat_ref):
    mod = mod_ref[0]
    subs = _sub_tiles(x_ref.shape[0])
    hs = [_split2(_rms(x_ref[sl], g_ref[...]) * (1.0 + mod[1:2]) + mod[0:1]) for sl in subs]
    qkvs = [_dot(h_hi, wqkv_ref[...]) for h_hi, _ in hs]
    zuvs = [_dot(h_hi, wzuv_ref[...]) for h_hi, _ in hs]
    w_hi, w_lo = _split2(wba_ref[...])
    bas = [_dot(h_hi, w_hi) + _dot(h_lo, w_hi) + _dot(h_hi, w_lo) for h_hi, h_lo in hs]
    for sl, qkv, zuv, ba in zip(subs, qkvs, zuvs, bas):
        qkv_ref[sl] = qkv.astype(BF16)
        z_ref[sl] = zuv[:, :GDN_W].astype(BF16)
        u_ref[sl] = _gelu_tanh(zuv[:, GDN_W:GDN_W + SGU_W]).astype(BF16)
        v_ref[sl] = _gelu_tanh(zuv[:, GDN_W + SGU_W:]).astype(BF16)
        ba_ref[sl] = ba[:, :N_GATE_COLS]
        bat_ref[:, sl] = ba.T[:N_GATE_COLS]


def _inproj(x2d, mod, rows_per_mod, ng0, wqkv, wzuv, wba):
    t = x2d.shape[0]
    tm = min(ROW_TILE, t)
    tiles_per_mod = rows_per_mod // tm
    row = lambda i: (i, 0)
    const = lambda i: (0, 0)
    return pl.pallas_call(
        _inproj_kernel,
        out_shape=(jax.ShapeDtypeStruct((t, QKV_COLS), BF16),
                   jax.ShapeDtypeStruct((t, GDN_W), BF16),
                   jax.ShapeDtypeStruct((t, SGU_W), BF16),
                   jax.ShapeDtypeStruct((t, SGU_W), BF16),
                   jax.ShapeDtypeStruct((t, N_GATE_COLS), F32),
                   jax.ShapeDtypeStruct((N_GATE_COLS, t), F32)),
        grid=(t // tm,),
        in_specs=[pl.BlockSpec((tm, D_MODEL), row),
                  pl.BlockSpec((1, 6, D_MODEL), lambda i: (i // tiles_per_mod, 0, 0)),
                  pl.BlockSpec((1, D_MODEL), const),
                  pl.BlockSpec(wqkv.shape, const),
                  pl.BlockSpec(wzuv.shape, const),
                  pl.BlockSpec(wba.shape, const)],
        out_specs=(pl.BlockSpec((tm, QKV_COLS), row),
                   pl.BlockSpec((tm, GDN_W), row),
                   pl.BlockSpec((tm, SGU_W), row),
                   pl.BlockSpec((tm, SGU_W), row),
                   pl.BlockSpec((tm, N_GATE_COLS), row),
                   pl.BlockSpec((N_GATE_COLS, tm), lambda i: (0, i))),
        compiler_params=_params("parallel"),
        name="in_proj",
    )(x2d, mod, ng0, wqkv, wzuv, wba)


def _gdn_prep_kernel(row_len, qkv_ref, cw_ref, ba_ref, bat_ref, alog_r_ref, dtb_r_ref,
                     alog_c_ref, dtb_c_ref, u_ref, w_ref, qd_ref, kd_ref, at_ref, gl_ref):
    n = GDN_BLOCK
    c = DELTA_CHUNK
    ri = lax.broadcasted_iota(jnp.int32, (n, n), 0)
    ci = lax.broadcasted_iota(jnp.int32, (n, n), 1)
    same = (ri // c) == (ci // c)
    lower = same & (ri >= ci)
    upper = same & (ri <= ci)

    def mask01(m):
        return jnp.where(m, 1.0, 0.0).astype(BF16)

    lower_b = mask01(lower)
    upper_b = mask01(upper)
    same_b = mask01(same)

    wi = lax.broadcasted_iota(jnp.int32, (c, n), 0)
    wl = lax.broadcasted_iota(jnp.int32, (c, n), 1)
    wchunk = wl // c
    wj = wl % c
    lower_w = wi >= wj
    upper_w = wi <= wj
    diag_w = wi == wj
    eye_w = jnp.where(diag_w, 1.0, 0.0)
    pair_masks = []
    s = 1
    while s < c:
        pair_masks.append(mask01(((wi // (2 * s)) == (wj // (2 * s))) & ((wi // s) != (wj // s))))
        s *= 2

    def to_wide(full):
        out = full[:c]
        for k in range(1, CHUNKS_PER_BLOCK):
            out = jnp.where(wchunk == k, full[k * c:(k + 1) * c], out)
        return out

    def col_wide(col):
        out = jnp.broadcast_to(col[:c], (c, n))
        for k in range(1, CHUNKS_PER_BLOCK):
            out = jnp.where(wchunk == k, jnp.broadcast_to(col[k * c:(k + 1) * c], (c, n)), out)
        return out

    def block_diag(x_w):
        return jnp.concatenate([x_w] * CHUNKS_PER_BLOCK, axis=0) * same_b

    ba = ba_ref[0]
    bat = bat_ref[...]
    beta_c = _sigmoid(ba[:, :N_CHAINS])
    g_c = -jnp.exp(alog_r_ref[...]) * _softplus(ba[:, N_CHAINS:] + dtb_r_ref[...])
    g_r = -jnp.exp(alog_c_ref[...]) * _softplus(bat[N_CHAINS:] + dtb_c_ref[...])
    gc3 = _split3(g_c)
    gr3 = jnp.concatenate(_split3(g_r), axis=0)

    def sum3_r(m):
        return m[:N_CHAINS] + m[N_CHAINS:2 * N_CHAINS] + m[2 * N_CHAINS:]

    cum_f_c = _dot(lower_b, gc3[0]) + _dot(lower_b, gc3[1]) + _dot(lower_b, gc3[2])
    tot_c = _dot(same_b, gc3[0]) + _dot(same_b, gc3[1]) + _dot(same_b, gc3[2])
    cum_b_c = tot_c - cum_f_c + g_c
    cum_f_r = sum3_r(_dot(gr3, upper_b))
    cum_b_r = sum3_r(_dot(gr3, lower_b))
    ei = lax.broadcasted_iota(jnp.int32, (n, CHUNKS_PER_BLOCK * HEAD_DIM), 0) // c
    ej = lax.broadcasted_iota(jnp.int32, (n, CHUNKS_PER_BLOCK * HEAD_DIM), 1) // HEAD_DIM
    g_last = jnp.exp(sum3_r(_dot(gr3, mask01(ei == ej))))
    gl_ref[0, 0, 0] = g_last[:GDN_HEADS]
    gl_ref[0, 0, 1] = g_last[GDN_HEADS:]

    pos = lax.broadcasted_iota(jnp.int32, (n, HEAD_DIM), 0) % row_len
    first = pos == 0
    last = pos == row_len - 1

    def conv_silu(col):
        x = qkv_ref[0, :, col * HEAD_DIM:(col + 1) * HEAD_DIM].astype(F32)
        cw = cw_ref[:, col * HEAD_DIM:(col + 1) * HEAD_DIM]
        xp = jnp.where(first, 0.0, pltpu.roll(x, 1, 0))
        xn = jnp.where(last, 0.0, pltpu.roll(x, n - 1, 0))
        y = xp * cw[0:1] + x * cw[1:2] + xn * cw[2:3]
        return y * _sigmoid(y)

    def l2n(x):
        return x * lax.rsqrt(jnp.sum(x * x, -1, keepdims=True) + NORM_EPS)

    a_bs, ps, rhss = [None] * N_CHAINS, [None] * N_CHAINS, [None] * N_CHAINS
    for h in range(GDN_HEADS):
        q = l2n(conv_silu(h)) * (HEAD_DIM ** -0.5)
        k = l2n(conv_silu(GDN_HEADS + h))
        v = conv_silu(2 * GDN_HEADS + h)
        k_b = k.astype(BF16)
        qk_kk = _dot_nt(jnp.concatenate([q.astype(BF16), k_b], axis=0), k_b)
        qk_w = to_wide(qk_kk[:n])
        kk_w = to_wide(qk_kk[n:])
        for d in range(2):
            j = d * GDN_HEADS + h
            mask_w = lower_w if d == 0 else upper_w
            cum_c = (cum_f_c if d == 0 else cum_b_c)[:, j:j + 1]
            cum_r = (cum_f_r if d == 0 else cum_b_r)[j:j + 1, :]
            b_c = beta_c[:, j:j + 1]
            decay_w = jnp.where(mask_w, jnp.exp(jnp.where(mask_w, col_wide(cum_c) - cum_r, 0.0)), 0.0)
            amat_w = jnp.where(diag_w, 0.0, kk_w * decay_w * col_wide(b_c))
            a_bs[j] = amat_w.astype(BF16)
            ps[j] = eye_w - amat_w * pair_masks[0].astype(F32)
            e_c = jnp.exp(cum_c)
            rhss[j] = jnp.concatenate([(v * b_c).astype(BF16), (k * (b_c * e_c)).astype(BF16)], axis=1)
            cols = slice(j * HEAD_DIM, (j + 1) * HEAD_DIM)
            qd_ref[0, :, cols] = (q * e_c).astype(BF16)
            kd_ref[0, :, cols] = (k * jnp.exp(tot_c[:, j:j + 1] - cum_c)).astype(BF16)
            at_ref[0, 0, j * c:(j + 1) * c, :] = (qk_w * decay_w).astype(BF16)

    for pm in pair_masks[1:]:
        p_bs = [p.astype(BF16) for p in ps]
        ys = [_dot(a_bs[j] * pm, block_diag(p_bs[j])) for j in range(N_CHAINS)]
        ps = [ps[j] - _dot(p_bs[j], block_diag(ys[j].astype(BF16))) for j in range(N_CHAINS)]

    for j in range(N_CHAINS):
        uw = _dot(block_diag(ps[j].astype(BF16)), rhss[j])
        cols = slice(j * HEAD_DIM, (j + 1) * HEAD_DIM)
        u_ref[0, :, cols] = uw[:, :HEAD_DIM].astype(BF16)
        w_ref[0, :, cols] = uw[:, HEAD_DIM:].astype(BF16)


def _gdn_prep(qkv, conv_w, ba, bat, alog, dtb, row_len):
    b, l, _ = qkv.shape
    nblk = l // GDN_BLOCK
    wide = N_CHAINS * HEAD_DIM
    blk = lambda bi, i: (bi, i, 0)
    const = lambda bi, i: (0, 0)
    alog_r, dtb_r = alog.reshape(1, N_CHAINS), dtb.reshape(1, N_CHAINS)
    alog_c, dtb_c = alog.reshape(N_CHAINS, 1), dtb.reshape(N_CHAINS, 1)
    return pl.pallas_call(
        functools.partial(_gdn_prep_kernel, row_len),
        out_shape=(jax.ShapeDtypeStruct((b, l, wide), BF16),) * 4 + (
            jax.ShapeDtypeStruct((b, nblk, N_CHAINS * DELTA_CHUNK, GDN_BLOCK), BF16),
            jax.ShapeDtypeStruct((b, nblk, 2, GDN_HEADS, CHUNKS_PER_BLOCK * HEAD_DIM), F32)),
        grid=(b, nblk),
        in_specs=[pl.BlockSpec((1, GDN_BLOCK, QKV_COLS), blk),
                  pl.BlockSpec((3, QKV_COLS), const),
                  pl.BlockSpec((1, GDN_BLOCK, N_GATE_COLS), blk),
                  pl.BlockSpec((N_GATE_COLS, GDN_BLOCK), lambda bi, i: (0, bi * nblk + i)),
                  pl.BlockSpec((1, N_CHAINS), const),
                  pl.BlockSpec((1, N_CHAINS), const),
                  pl.BlockSpec((N_CHAINS, 1), const),
                  pl.BlockSpec((N_CHAINS, 1), const)],
        out_specs=(pl.BlockSpec((1, GDN_BLOCK, wide), blk),) * 4 + (
            pl.BlockSpec((1, 1, N_CHAINS * DELTA_CHUNK, GDN_BLOCK), lambda bi, i: (bi, i, 0, 0)),
            pl.BlockSpec((1, 1, 2, GDN_HEADS, CHUNKS_PER_BLOCK * HEAD_DIM),
                         lambda bi, i: (bi, i, 0, 0, 0))),
        compiler_params=_params("parallel", "parallel"),
        name="gdn_prep",
    )(qkv, conv_w, ba, bat, alog_r, dtb_r, alog_c, dtb_c)


def _gdn_scan_kernel(uf, wf, qf, kf, af, gf, ub, wb, qb, kb, ab, gb, s0_ref,
                     of_ref, ob_ref, sfin_ref, s_scr):
    i = pl.program_id(0)
    c = DELTA_CHUNK
    n_batch = s0_ref.shape[0]

    @pl.when(i == 0)
    def _():
        s_scr[...] = s0_ref[...]

    ops = ((uf, wf, qf, kf, af, gf, of_ref), (ub, wb, qb, kb, ab, gb, ob_ref))
    chains = [(bi, d, h) for bi in range(n_batch) for d in range(2) for h in range(GDN_HEADS)]
    states = [s_scr[bi, d * GDN_HEADS + h] for bi, d, h in chains]
    for step in range(CHUNKS_PER_BLOCK):
        def chunk(d):
            cc = step if d == 0 else CHUNKS_PER_BLOCK - 1 - step
            return cc, slice(cc * c, (cc + 1) * c)

        xs = []
        for j, (bi, d, h) in enumerate(chains):
            _, rows = chunk(d)
            cols = slice(h * HEAD_DIM, (h + 1) * HEAD_DIM)
            wq = jnp.concatenate([ops[d][1][bi, rows, cols], ops[d][2][bi, rows, cols]], axis=0)
            xs.append(_dot(wq, states[j].astype(BF16)))
        v_news = []
        for j, (bi, d, h) in enumerate(chains):
            _, rows = chunk(d)
            cols = slice(h * HEAD_DIM, (h + 1) * HEAD_DIM)
            v_news.append((ops[d][0][bi, rows, cols].astype(F32) - xs[j][:c]).astype(BF16))
        for j, (bi, d, h) in enumerate(chains):
            cc, rows = chunk(d)
            cols = slice(h * HEAD_DIM, (h + 1) * HEAD_DIM)
            a_c = ops[d][4][bi, 0, h * c:(h + 1) * c, cc * c:(cc + 1) * c]
            ops[d][6][bi, rows, cols] = xs[j][c:] + _dot(a_c, v_news[j])
            ds = _dot_tn(ops[d][3][bi, rows, cols], v_news[j])
            g_last = ops[d][5][bi, 0, 0, h:h + 1, cc * HEAD_DIM:(cc + 1) * HEAD_DIM]
            states[j] = states[j] * g_last + ds
    for j, (bi, d, h) in enumerate(chains):
        s_scr[bi, d * GDN_HEADS + h] = states[j]

    @pl.when(i == pl.num_programs(0) - 1)
    def _():
        sfin_ref[...] = s_scr[...]


def _gdn_scan(u, w, qd, kd, at, gl, s0):
    b, l, _ = u.shape
    nblk = l // GDN_BLOCK
    half = GDN_HEADS * HEAD_DIM
    fwd = lambda i: (0, i, 0)
    bwd = lambda i: (0, nblk - 1 - i, 1)
    big = lambda m: pl.BlockSpec((b, GDN_BLOCK, half), m)
    att_shape = (b, 1, GDN_HEADS * DELTA_CHUNK, GDN_BLOCK)
    attf = pl.BlockSpec(att_shape, lambda i: (0, i, 0, 0))
    attb = pl.BlockSpec(att_shape, lambda i: (0, nblk - 1 - i, 1, 0))
    gl_shape = (b, 1, 1, GDN_HEADS, CHUNKS_PER_BLOCK * HEAD_DIM)
    glf = pl.BlockSpec(gl_shape, lambda i: (0, i, 0, 0, 0))
    glb = pl.BlockSpec(gl_shape, lambda i: (0, nblk - 1 - i, 1, 0, 0))
    state = pl.BlockSpec((b, N_CHAINS, HEAD_DIM, HEAD_DIM), lambda i: (0, 0, 0, 0))
    return pl.pallas_call(
        _gdn_scan_kernel,
        out_shape=(jax.ShapeDtypeStruct((b, l, half), F32),
                   jax.ShapeDtypeStruct((b, l, half), F32),
                   jax.ShapeDtypeStruct((b, N_CHAINS, HEAD_DIM, HEAD_DIM), F32)),
        grid=(nblk,),
        in_specs=[big(fwd), big(fwd), big(fwd), big(fwd), attf, glf,
                  big(bwd), big(bwd), big(bwd), big(bwd), attb, glb, state],
        out_specs=(pl.BlockSpec((b, GDN_BLOCK, half), fwd),
                   pl.BlockSpec((b, GDN_BLOCK, half), lambda i: (0, nblk - 1 - i, 0)),
                   state),
        scratch_shapes=[pltpu.VMEM((b, N_CHAINS, HEAD_DIM, HEAD_DIM), F32)],
        compiler_params=_params("arbitrary"),
        name="gdn_scan",
    )(u, w, qd, kd, at, gl, u, w, qd, kd, at, gl, s0)


def _mix_into(mix_ref, of_ref, ob_ref, z_ref, u_ref, v_ref, gng_ref, lng_ref, lnb_ref, ws_ref,
              bs_ref):
    tm = of_ref.shape[0]
    o = of_ref[...] + ob_ref[...]
    z = z_ref[...].astype(F32)
    for h in range(GDN_HEADS):
        cols = slice(h * HEAD_DIM, (h + 1) * HEAD_DIM)
        oh = o[:, cols]
        zh = z[:, cols]
        r = lax.rsqrt(jnp.mean(oh * oh, -1, keepdims=True) + NORM_EPS)
        mix_ref[:, cols] = (oh * r * gng_ref[...] * (zh * _sigmoid(zh))).astype(BF16)
    for g in range(SGU_GROUPS):
        cols = slice(g * SGU_GROUP, (g + 1) * SGU_GROUP)
        vg = v_ref[:, cols].astype(F32)
        vc = vg - jnp.mean(vg, -1, keepdims=True)
        vn = vc * lax.rsqrt(jnp.mean(vc * vc, -1, keepdims=True) + NORM_EPS)
        vn = (vn * lng_ref[g:g + 1] + lnb_ref[g:g + 1]).astype(BF16)
        wsg = ws_ref[g].astype(BF16)
        for n in range(tm // SGU_CHUNK):
            rows = slice(n * SGU_CHUNK, (n + 1) * SGU_CHUNK)
            s = _dot(wsg, vn[rows]) + bs_ref[g]
            mix_ref[rows, GDN_W + g * SGU_GROUP:GDN_W + (g + 1) * SGU_GROUP] = (
                u_ref[rows, cols].astype(F32) * s).astype(BF16)


def _outproj_kernel(of_ref, ob_ref, z_ref, u_ref, v_ref, gng_ref, lng_ref, lnb_ref, ws_ref, bs_ref,
                    x_ref, mod_ref, ng_ref, wout_ref, rwt_ref, rb_ref,
                    x1_ref, h2_ref, lt_ref, mix_scr):
    subs = _sub_tiles(x_ref.shape[0])
    for sl in subs:
        _mix_into(mix_scr.at[sl], of_ref.at[sl], ob_ref.at[sl], z_ref.at[sl], u_ref.at[sl],
                  v_ref.at[sl], gng_ref, lng_ref, lnb_ref, ws_ref, bs_ref)
    mod = mod_ref[0]
    ys = [_dot(mix_scr[sl], wout_ref[...]) for sl in subs]
    w_hi, w_lo = _split2(rwt_ref[...])
    for sl, y in zip(subs, ys):
        x1 = x_ref[sl] + mod[2:3] * _rms(y, ng_ref[1:2])
        x1_ref[sl] = x1
        h2 = _rms(x1, ng_ref[2:3]) * (1.0 + mod[4:5]) + mod[3:4]
        h_hi, h_lo = _split2(h2)
        _store_planes(h2_ref, _pack_rows(h2), sl)
        lt_ref[:, sl] = (_dot_nt(w_hi, h_hi) + _dot_nt(w_lo, h_hi) + _dot_nt(w_hi, h_lo)
                         + rb_ref[...])


def _outproj(o_f, o_b, z, ug, vg, gdn_norm_g, ln_g, ln_b, w_s, b_s_full,
             x2d, mod, rows_per_mod, ng, wout, rwt, rb_col):
    t = x2d.shape[0]
    tm = min(ROW_TILE, t)
    tiles_per_mod = rows_per_mod // tm
    row = lambda i: (i, 0)
    const = lambda i: (0, 0)
    c3 = lambda i: (0, 0, 0)
    return pl.pallas_call(
        _outproj_kernel,
        out_shape=(jax.ShapeDtypeStruct((t, D_MODEL), F32),
                   jax.ShapeDtypeStruct((N_PLANES, t, SC_ROW_W), U32),
                   jax.ShapeDtypeStruct((N_EXPERTS, t), F32)),
        grid=(t // tm,),
        in_specs=[pl.BlockSpec((tm, GDN_W), row), pl.BlockSpec((tm, GDN_W), row),
                  pl.BlockSpec((tm, GDN_W), row), pl.BlockSpec((tm, SGU_W), row),
                  pl.BlockSpec((tm, SGU_W), row),
                  pl.BlockSpec((1, HEAD_DIM), const),
                  pl.BlockSpec((SGU_GROUPS, SGU_GROUP), const),
                  pl.BlockSpec((SGU_GROUPS, SGU_GROUP), const),
                  pl.BlockSpec((SGU_GROUPS, SGU_CHUNK, SGU_CHUNK), c3),
                  pl.BlockSpec((SGU_GROUPS, SGU_CHUNK, SGU_GROUP), c3),
                  pl.BlockSpec((tm, D_MODEL), row),
                  pl.BlockSpec((1, 6, D_MODEL), lambda i: (i // tiles_per_mod, 0, 0)),
                  pl.BlockSpec((4, D_MODEL), const),
                  pl.BlockSpec((D_MODEL, D_MODEL), const),
                  pl.BlockSpec((N_EXPERTS, D_MODEL), const),
                  pl.BlockSpec((N_EXPERTS, 1), const)],
        out_specs=(pl.BlockSpec((tm, D_MODEL), row),
                   pl.BlockSpec((N_PLANES, tm, SC_ROW_W), lambda i: (0, i, 0)),
                   pl.BlockSpec((N_EXPERTS, tm), lambda i: (0, i))),
        compiler_params=_params("parallel"),
        scratch_shapes=[pltpu.VMEM((tm, D_MODEL), BF16)],
        name="out_proj_router",
    )(o_f, o_b, z, ug, vg, gdn_norm_g, ln_g, ln_b, w_s, b_s_full, x2d, mod, ng, wout, rwt, rb_col)


def _moe_kernel(be_ref, nb_ref, xb_ref, wgu_ref, bgu_ref, wd_ref, bd_ref, y_ref,
                wgu_b, wd_b, gut_scr):
    i = pl.program_id(0)
    live = i < nb_ref[0]
    new_expert = jnp.logical_or(i == 0, be_ref[i] != be_ref[jnp.maximum(i - 1, 0)])

    @pl.when(jnp.logical_and(live, new_expert))
    def _():
        wgu_b[...] = wgu_ref[0].astype(BF16)
        wd_b[...] = wd_ref[0].astype(BF16)

    @pl.when(live)
    def _():
        subs = _sub_tiles(MOE_ROWS)
        xbs = [_unpack_rows(_load_planes(xb_ref, sl)).astype(BF16) for sl in subs]
        gu_ts = [(_dot(xb, wgu_b[...]) + bgu_ref[0]).T for xb in xbs]
        act_ts = []
        parts_per_sub = SUB_ROWS // LANES
        for s, gu_t in enumerate(gu_ts):
            acts = []
            for part in range(parts_per_sub):
                part_ref = gut_scr.at[s * parts_per_sub + part]
                part_ref[...] = gu_t[:, part * LANES:(part + 1) * LANES]
                gate = jnp.minimum(part_ref[pl.ds(0, D_FF, stride=2), :], SWIGLU_LIMIT)
                up = jnp.clip(part_ref[pl.ds(1, D_FF, stride=2), :], -SWIGLU_LIMIT, SWIGLU_LIMIT)
                acts.append(((up + 1.0) * gate * _sigmoid(SWIGLU_ALPHA * gate)).astype(BF16))
            act_ts.append(jnp.concatenate(acts, axis=1))
        for sl, act_t in zip(subs, act_ts):
            _store_planes(y_ref, _pack_rows(_dot_tn(act_t, wd_b[...]) + bd_ref[0]), sl)

    @pl.when(jnp.logical_not(live))
    def _():
        y_ref[...] = jnp.zeros_like(y_ref)


def _moe_ffn(block_e, n_used, xb, w_gu, b_gu, w_down, b_down):
    n_rows = xb.shape[1]
    n_blocks = n_rows // MOE_ROWS
    row = lambda i, be, nb: (0, i, 0)
    ex3 = lambda i, be, nb: (be[i], 0, 0)
    live_row = lambda i, be, nb: (0, jnp.minimum(i, nb[0] - 1), 0)
    planes = (N_PLANES, MOE_ROWS, SC_ROW_W)
    grid_spec = pltpu.PrefetchScalarGridSpec(
        num_scalar_prefetch=2,
        grid=(n_blocks,),
        in_specs=[pl.BlockSpec(planes, live_row),
                  pl.BlockSpec((1, D_MODEL, 2 * D_FF), ex3),
                  pl.BlockSpec((1, 1, 2 * D_FF), ex3),
                  pl.BlockSpec((1, D_FF, D_MODEL), ex3),
                  pl.BlockSpec((1, 1, D_MODEL), ex3)],
        out_specs=pl.BlockSpec(planes, row),
        scratch_shapes=[pltpu.VMEM((D_MODEL, 2 * D_FF), BF16),
                        pltpu.VMEM((D_FF, D_MODEL), BF16),
                        pltpu.VMEM((MOE_ROWS // LANES, 2 * D_FF, LANES), F32)],
    )
    return pl.pallas_call(
        _moe_kernel,
        out_shape=jax.ShapeDtypeStruct((N_PLANES, n_rows, SC_ROW_W), U32),
        grid_spec=grid_spec,
        compiler_params=_params("arbitrary"),
        name="moe_ffn",
    )(block_e, n_used, xb, w_gu, b_gu, w_down, b_down)


def _combine_kernel(y0_ref, y1_ref, y2_ref, y3_ref, gt_ref, x1_ref, mod_ref, ng_ref, o_ref):
    mod = mod_ref[0]
    gt = gt_ref[...]
    y = _unpack_rows(_load_planes(y0_ref)) * gt[:, 0:1]
    for k, y_ref in ((1, y1_ref), (2, y2_ref), (3, y3_ref)):
        y = y + _unpack_rows(_load_planes(y_ref)) * gt[:, k:k + 1]
    o_ref[...] = x1_ref[...] + mod[5:6] * _rms(y, ng_ref[3:4])


def _combine(yg, gates, x1, mod, rows_per_mod, ng):
    t = x1.shape[0]
    tm = min(COMBINE_TILE, t)
    tiles_per_mod = rows_per_mod // tm
    n_tiles = t // tm
    row = lambda i: (i, 0)
    choice = lambda k: pl.BlockSpec((N_PLANES, tm, SC_ROW_W), lambda i: (0, k * n_tiles + i, 0))
    return pl.pallas_call(
        _combine_kernel,
        out_shape=jax.ShapeDtypeStruct((t, D_MODEL), F32),
        grid=(n_tiles,),
        in_specs=[choice(0), choice(1), choice(2), choice(3),
                  pl.BlockSpec((tm, TOP_K), row),
                  pl.BlockSpec((tm, D_MODEL), row),
                  pl.BlockSpec((1, 6, D_MODEL), lambda i: (i // tiles_per_mod, 0, 0)),
                  pl.BlockSpec((4, D_MODEL), lambda i: (0, 0))],
        out_specs=pl.BlockSpec((tm, D_MODEL), row),
        compiler_params=_params("parallel"),
        name="moe_combine",
    )(yg, yg, yg, yg, gates, x1, mod, ng)


def _route_kernel(lt_ref, eidx_ref, gate_ref, rank_ref, cnt_ref, carry):
    i = pl.program_id(0)
    tile = lt_ref.shape[1]

    @pl.when(i == 0)
    def _():
        carry[...] = jnp.zeros_like(carry)

    logits = lt_ref[...]
    eio = lax.broadcasted_iota(jnp.int32, (N_EXPERTS, tile), 0).astype(F32)
    vals, sels = [], []
    for k in range(TOP_K):
        m = jnp.max(logits, axis=0, keepdims=True)
        idx = jnp.min(jnp.where(logits == m, eio, float(N_EXPERTS)), axis=0, keepdims=True)
        sel = eio == idx
        logits = jnp.where(sel, -jnp.inf, logits)
        vals.append(m)
        sels.append(sel)
        eidx_ref[k:k + 1, :] = idx.astype(jnp.int32)
    exps = [jnp.exp(v - vals[0]) for v in vals]
    denom = exps[0] + exps[1] + exps[2] + exps[3]
    for k in range(TOP_K):
        gate_ref[k:k + 1, :] = exps[k] / denom

    member = jnp.where(sels[0] | sels[1] | sels[2] | sels[3], 1.0, 0.0)
    ti = lax.broadcasted_iota(jnp.int32, (tile, tile), 0)
    tj = lax.broadcasted_iota(jnp.int32, (tile, tile), 1)
    earlier = jnp.where(ti < tj, 1.0, 0.0).astype(BF16)
    before = _dot(member.astype(BF16), earlier) + carry[...]
    for k in range(TOP_K):
        rank_ref[k:k + 1, :] = jnp.sum(jnp.where(sels[k], before, 0.0), axis=0,
                                       keepdims=True).astype(jnp.int32)
    carry[...] = carry[...] + jnp.sum(member, axis=1, keepdims=True)
    cnt_ref[...] = carry[...].astype(jnp.int32)


def _route(logits_t):
    t = logits_t.shape[1]
    tile = min(ROUTE_TILE, t)
    blk = lambda i: (0, i)
    return pl.pallas_call(
        _route_kernel,
        out_shape=(jax.ShapeDtypeStruct((TOP_K, t), jnp.int32),
                   jax.ShapeDtypeStruct((TOP_K, t), F32),
                   jax.ShapeDtypeStruct((TOP_K, t), jnp.int32),
                   jax.ShapeDtypeStruct((N_EXPERTS, 1), jnp.int32)),
        grid=(t // tile,),
        in_specs=[pl.BlockSpec((N_EXPERTS, tile), blk)],
        out_specs=(pl.BlockSpec((TOP_K, tile), blk), pl.BlockSpec((TOP_K, tile), blk),
                   pl.BlockSpec((TOP_K, tile), blk),
                   pl.BlockSpec((N_EXPERTS, 1), lambda i: (0, 0))),
        scratch_shapes=[pltpu.VMEM((N_EXPERTS, 1), F32)],
        compiler_params=_params("arbitrary"),
        name="moe_route",
    )(logits_t)


def _slot_tables(eidx, rank, counts, n_blocks):
    padded = (counts + MOE_ROWS - 1) // MOE_ROWS * MOE_ROWS
    pad_end = jnp.cumsum(padded)
    pad_start = pad_end - padded
    experts = jnp.arange(N_EXPERTS, dtype=jnp.int32)
    dest = rank + jnp.sum(jnp.where(eidx[..., None] == experts, pad_start, 0), axis=-1)
    first_row = jnp.arange(n_blocks, dtype=jnp.int32)[:, None] * MOE_ROWS
    block_e = jnp.minimum(jnp.sum((pad_end[None, :] <= first_row).astype(jnp.int32), axis=1),
                          N_EXPERTS - 1)
    n_used = pad_end[-1:] // MOE_ROWS
    return dest.astype(jnp.int32), pad_start.astype(jnp.int32), block_e, n_used.astype(jnp.int32)


def _sc_mesh():
    return plsc.VectorSubcoreMesh(core_axis_name="c", subcore_axis_name="s",
                                  num_cores=SC_CORES, num_subcores=SC_SUBCORES)


def _plane_row_ids(rows, rows_per_plane):
    return jnp.concatenate([rows + p * rows_per_plane for p in range(N_PLANES)], axis=-1)


def _sc_gather_rows(table, rows):
    v = table.shape[1]
    idx = _plane_row_ids(rows, v)[None]
    n_all = idx.shape[1]

    @functools.partial(pl.kernel, mesh=_sc_mesh(), name="moe_gather_rows",
                       out_type=jax.ShapeDtypeStruct((n_all, SC_ROW_W), U32))
    def gather(x_hbm, i_hbm, o_hbm):
        def body(i_vmem, o_vmem):
            pltpu.sync_copy(x_hbm.at[i_vmem.at[0]], o_vmem)

        pltpu.emit_pipeline(
            body, grid=(n_all // SC_WINDOW,),
            in_specs=[pl.BlockSpec((1, SC_WINDOW), lambda i: (0, i))],
            out_specs=[pl.BlockSpec((SC_WINDOW, SC_ROW_W), lambda i: (i, 0))],
            core_axis_name=("c", "s"), dimension_semantics=(pltpu.PARALLEL,),
        )(i_hbm, o_hbm)

    return gather(table.reshape(N_PLANES * v, SC_ROW_W), idx).reshape(N_PLANES, -1, SC_ROW_W)


def _sc_scatter_rows(rows, dest, n_out):
    t = rows.shape[1]
    idx = _plane_row_ids(dest, n_out)

    @functools.partial(pl.kernel, mesh=_sc_mesh(), name="moe_scatter_rows", scratch_types=[],
                       out_type=jax.ShapeDtypeStruct((N_PLANES * n_out, SC_ROW_W), U32))
    def scatter(x_hbm, i_hbm, o_hbm):
        def body(x_vmem, i_vmem):
            for k in range(TOP_K):
                pltpu.sync_copy(x_vmem, o_hbm.at[i_vmem.at[k]])

        pltpu.emit_pipeline(
            body, grid=(N_PLANES * t // SC_WINDOW,),
            in_specs=[pl.BlockSpec((SC_WINDOW, SC_ROW_W), lambda i: (i, 0)),
                      pl.BlockSpec((TOP_K, SC_WINDOW), lambda i: (0, i))],
            out_specs=[],
            core_axis_name=("c", "s"), dimension_semantics=(pltpu.PARALLEL,),
        )(x_hbm, i_hbm)

    return scatter(rows.reshape(N_PLANES * t, SC_ROW_W), idx).reshape(N_PLANES, n_out, SC_ROW_W)


def _zero_pad_kernel(cnt_ref, start_ref, xb_in_ref, xb_ref, zero_scr, sem):
    del xb_in_ref
    zero_scr[...] = jnp.zeros_like(zero_scr)

    pieces = [SUBLANES << bit for bit in range((MOE_ROWS // SUBLANES - 1).bit_length())]

    def zero_copy(p, row, size):
        return pltpu.make_async_copy(zero_scr.at[pl.ds(0, size)], xb_ref.at[p, pl.ds(row, size)], sem)

    def for_each_piece(fn):
        def per_expert(e, carry):
            n_real = cnt_ref[e]
            n_pad = (MOE_ROWS - n_real % MOE_ROWS) % MOE_ROWS
            first = start_ref[e] + n_real
            n_single = n_pad % SUBLANES
            for j in range(SUBLANES - 1):
                @pl.when(j < n_single)
                def _():
                    for p in range(N_PLANES):
                        fn(zero_copy(p, first + j, 1))
            row = first + n_single
            for size in pieces:
                @pl.when((n_pad & size) != 0)
                def _():
                    for p in range(N_PLANES):
                        fn(zero_copy(p, pl.multiple_of(row, SUBLANES), size))
                row = row + (n_pad & size)
            return carry
        lax.fori_loop(0, N_EXPERTS, per_expert, 0)

    for_each_piece(lambda copy: copy.start())
    for_each_piece(lambda copy: copy.wait())


def _zero_pad_slots(counts, pad_start, xb):
    grid_spec = pltpu.PrefetchScalarGridSpec(
        num_scalar_prefetch=2,
        grid=(1,),
        in_specs=[pl.BlockSpec(memory_space=pl.ANY)],
        out_specs=pl.BlockSpec(memory_space=pl.ANY),
        scratch_shapes=[pltpu.VMEM((MOE_ROWS // 2, SC_ROW_W), U32), pltpu.SemaphoreType.DMA],
    )
    return pl.pallas_call(
        _zero_pad_kernel,
        out_shape=jax.ShapeDtypeStruct(xb.shape, xb.dtype),
        grid_spec=grid_spec,
        input_output_aliases={2: 0},
        compiler_params=_params("arbitrary"),
        name="moe_zero_pad",
    )(counts, pad_start, xb)


def kernel(x, c, ctx, c_ctx, w_ada, b_ada, norm_g, w_in, conv_w, a_log, dt_bias, gdn_norm_g,
           sgu_ln_g, sgu_ln_b, sgu_w, sgu_b, w_out, router_w, router_b, w_gu, b_gu, w_down, b_down):
    b, l, d = x.shape
    lc = ctx.shape[1]
    t = b * l
    assert d == D_MODEL and l % ROW_TILE == 0 and l % GDN_BLOCK == 0 and lc % GDN_BLOCK == 0
    assert w_ada.shape[0] == 1, "single-layer block"

    cs = jnp.concatenate([c, c_ctx[None], jnp.zeros((8 - b - 1, d), F32)], axis=0)
    mod_all = _ada(cs, w_ada[0], b_ada[0][None])
    mod = mod_all[:b].reshape(b, 6, d)
    mod_c = mod_all[b:b + 1].reshape(1, 6, d)
    ng = norm_g[0]

    w = w_in[0]
    wqkv = w[:, :QKV_COLS].astype(BF16)
    wba = jnp.pad(w[:, QKV_COLS:QKV_COLS + N_GATE_COLS], ((0, 0), (0, LANES - N_GATE_COLS)))
    wzuv = w[:, QKV_COLS + N_GATE_COLS:].astype(BF16)

    x2d = x.reshape(t, d)
    qkv, z, ug, vg, ba, bat = _inproj(x2d, mod, l, ng[0:1], wqkv, wzuv, wba)
    ctx2d = ctx.reshape(b * lc, d)
    qkv_c, _, _, _, ba_c, bat_c = _inproj(ctx2d, mod_c, b * lc, ng[0:1], wqkv, wzuv, wba)

    alog = a_log[0].reshape(-1)
    dtb = dt_bias[0].reshape(-1)
    pc = _gdn_prep(qkv_c.reshape(b, lc, QKV_COLS), conv_w[0], ba_c.reshape(b, lc, N_GATE_COLS),
                   bat_c, alog, dtb, lc)
    s_zero = jnp.zeros((b, N_CHAINS, HEAD_DIM, HEAD_DIM), F32)
    _, _, s_ctx = _gdn_scan(*pc, s_zero)
    pp = _gdn_prep(qkv.reshape(b, l, QKV_COLS), conv_w[0], ba.reshape(b, l, N_GATE_COLS),
                   bat, alog, dtb, GRID_W)
    o_f, o_b, _ = _gdn_scan(*pp, s_ctx)

    b_s_full = jnp.broadcast_to(sgu_b[0][:, :, None], (SGU_GROUPS, SGU_CHUNK, SGU_GROUP))
    x1, h2p, logits_t = _outproj(o_f.reshape(t, GDN_W), o_b.reshape(t, GDN_W), z, ug, vg,
                                 gdn_norm_g, sgu_ln_g[0], sgu_ln_b[0], sgu_w[0], b_s_full,
                                 x2d, mod, l, ng, w_out[0].astype(BF16),
                                 router_w[0].T, router_b[0][:, None])

    eidx, gates_t, rank, counts = _route(logits_t)
    counts = counts[:, 0]
    n_blocks = -(-(t * TOP_K) // MOE_ROWS) + N_EXPERTS
    dest, pad_start, block_e, n_used = _slot_tables(eidx, rank, counts, n_blocks)
    xb = _sc_scatter_rows(h2p, dest, n_blocks * MOE_ROWS)
    xb = _zero_pad_slots(counts, pad_start, xb)
    yb = _moe_ffn(block_e, n_used, xb, w_gu[0], b_gu[0][:, None, :], w_down[0],
                  b_down[0][:, None, :])
    yg = _sc_gather_rows(yb, dest.reshape(-1))
    out = _combine(yg, gates_t.T, x1, mod, l, ng)
    return out.reshape(b, l, d)
```

```python
import functools
import math

import jax
import jax.numpy as jnp
from jax import lax
from jax.experimental import pallas as pl
from jax.experimental.pallas import tpu as pltpu
from jax.experimental.pallas import tpu_sc as plsc

F32 = jnp.float32
BF16 = jnp.bfloat16

D_MODEL = 1024
GDN_HEADS = 4
HEAD_DIM = 128
GDN_W = GDN_HEADS * HEAD_DIM
SGU_GROUPS = 4
SGU_GROUP = 128
SGU_W = SGU_GROUPS * SGU_GROUP
SGU_CHUNK = 128
DELTA_CHUNK = 64
GRID_W = 64
N_EXPERTS = 32
TOP_K = 4
D_FF = 1024
SWIGLU_LIMIT = 7.0
SWIGLU_ALPHA = 1.702
NORM_EPS = 1e-6
QKV_COLS = 3 * GDN_W
N_CHAINS = 2 * GDN_HEADS
N_GATE_COLS = 2 * N_CHAINS

ROW_TILE = 1024
SUB_ROWS = 256
COMBINE_TILE = 512
GDN_BLOCK = 256
CHUNKS_PER_BLOCK = GDN_BLOCK // DELTA_CHUNK
MOE_ROWS = 512
U32 = jnp.uint32
LANES = 128
SUBLANES = 8
PACKED_W = D_MODEL // 2
ROUTE_TILE = 512
SC_CORES = 2
SC_SUBCORES = 16
SC_WINDOW = 128
N_PLANES = 2
SC_ROW_W = PACKED_W // N_PLANES
VMEM_LIMIT = 56 * 1024 * 1024


def _params(*sem):
    return pltpu.CompilerParams(dimension_semantics=sem, vmem_limit_bytes=VMEM_LIMIT)


def _dot(a, b):
    return jnp.dot(a, b, preferred_element_type=F32)


def _dot_nt(a, b):
    return lax.dot_general(a, b, (((1,), (1,)), ((), ())), preferred_element_type=F32)


def _dot_tn(a, b):
    return lax.dot_general(a, b, (((0,), (0,)), ((), ())), preferred_element_type=F32)


def _split2(a):
    hi = a.astype(BF16)
    lo = (a - hi.astype(F32)).astype(BF16)
    return hi, lo


def _split3(a):
    hi = a.astype(BF16)
    r = a - hi.astype(F32)
    mid = r.astype(BF16)
    lo = (r - mid.astype(F32)).astype(BF16)
    return hi, mid, lo


def _pack_rows(x32):
    xb = x32.astype(BF16).astype(F32)
    hi = lax.bitcast_convert_type(xb[:, :PACKED_W], U32)
    lo = lax.bitcast_convert_type(xb[:, PACKED_W:], U32)
    return hi | (lo >> 16)


def _store_planes(ref, packed, rows=slice(None)):
    for p in range(N_PLANES):
        ref[p, rows] = packed[:, p * SC_ROW_W:(p + 1) * SC_ROW_W]


def _load_planes(ref, rows=slice(None)):
    return jnp.concatenate([ref[p, rows] for p in range(N_PLANES)], axis=1)


def _sub_tiles(n_rows):
    return [slice(r, r + SUB_ROWS) for r in range(0, n_rows, SUB_ROWS)]


def _unpack_rows(w):
    hi = lax.bitcast_convert_type(w & jnp.uint32(0xFFFF0000), F32)
    lo = lax.bitcast_convert_type(w << 16, F32)
    return jnp.concatenate([hi, lo], axis=1)


def _rms(x32, g):
    return x32 * lax.rsqrt(jnp.mean(x32 * x32, -1, keepdims=True) + NORM_EPS) * g


def _gelu_tanh(x):
    c = math.sqrt(2.0 / math.pi)
    return 0.5 * x * (1.0 + jnp.tanh(c * (x + 0.044715 * (x * x * x))))


def _sigmoid(x):
    return 1.0 / (1.0 + jnp.exp(-x))


def _softplus(x):
    return jnp.maximum(x, 0.0) + jnp.log(1.0 + jnp.exp(-jnp.abs(x)))


def _ada_kernel(c_ref, w_ref, b_ref, o_ref):
    c = c_ref[...]
    s = c * _sigmoid(c)
    s_hi, s_lo = _split2(s)
    w_hi, w_lo = _split2(w_ref[...])
    o_ref[...] = _dot(s_hi, w_hi) + _dot(s_lo, w_hi) + _dot(s_hi, w_lo) + b_ref[...]


def _ada(cs, w_ada, b_ada):
    n = w_ada.shape[1]
    bn = D_MODEL
    return pl.pallas_call(
        _ada_kernel,
        out_shape=jax.ShapeDtypeStruct((cs.shape[0], n), F32),
        grid=(n // bn,),
        in_specs=[pl.BlockSpec(cs.shape, lambda j: (0, 0)),
                  pl.BlockSpec((D_MODEL, bn), lambda j: (0, j)),
                  pl.BlockSpec((1, bn), lambda j: (0, j))],
        out_specs=pl.BlockSpec((cs.shape[0], bn), lambda j: (0, j)),
        compiler_params=_params("parallel"),
        name="ada_mod",
    )(cs, w_ada, b_ada)


def _inproj_kernel(x_ref, mod_ref, g_ref, wqkv_ref, wzuv_ref, wba_ref,
                   qkv_ref, z_ref, u_ref, v_ref, ba_ref, bat_ref):
    mod = mod_ref[0]
    subs = _sub_tiles(x_ref.shape[0])
    hs = [_split2(_rms(x_ref[sl], g_ref[...]) * (1.0 + mod[1:2]) + mod[0:1]) for sl in subs]
    qkvs = [_dot(h_hi, wqkv_ref[...]) for h_hi, _ in hs]
    zuvs = [_dot(h_hi, wzuv_ref[...]) for h_hi, _ in hs]
    w_hi, w_lo = _split2(wba_ref[...])
    bas = [_dot(h_hi, w_hi) + _dot(h_lo, w_hi) + _dot(h_hi, w_lo) for h_hi, h_lo in hs]
    for sl, qkv, zuv, ba in zip(subs, qkvs, zuvs, bas):
        qkv_ref[sl] = qkv.astype(BF16)
        z_ref[sl] = zuv[:, :GDN_W].astype(BF16)
        u_ref[sl] = _gelu_tanh(zuv[:, GDN_W:GDN_W + SGU_W]).astype(BF16)
        v_ref[sl] = _gelu_tanh(zuv[:, GDN_W + SGU_W:]).astype(BF16)
        ba_ref[sl] = ba[:, :N_GATE_COLS]
        bat_ref[:, sl] = ba.T[:N_GATE_COLS]


def _inproj(x2d, mod, rows_per_mod, ng0, wqkv, wzuv, wba):
    t = x2d.shape[0]
    tm = min(ROW_TILE, t)
    tiles_per_mod = rows_per_mod // tm
    row = lambda i: (i, 0)
    const = lambda i: (0, 0)
    return pl.pallas_call(
        _inproj_kernel,
        out_shape=(jax.ShapeDtypeStruct((t, QKV_COLS), BF16),
                   jax.ShapeDtypeStruct((t, GDN_W), BF16),
                   jax.ShapeDtypeStruct((t, SGU_W), BF16),
                   jax.ShapeDtypeStruct((t, SGU_W), BF16),
                   jax.ShapeDtypeStruct((t, N_GATE_COLS), F32),
                   jax.ShapeDtypeStruct((N_GATE_COLS, t), F32)),
        grid=(t // tm,),
        in_specs=[pl.BlockSpec((tm, D_MODEL), row),
                  pl.BlockSpec((1, 6, D_MODEL), lambda i: (i // tiles_per_mod, 0, 0)),
                  pl.BlockSpec((1, D_MODEL), const),
                  pl.BlockSpec(wqkv.shape, const),
                  pl.BlockSpec(wzuv.shape, const),
                  pl.BlockSpec(wba.shape, const)],
        out_specs=(pl.BlockSpec((tm, QKV_COLS), row),
                   pl.BlockSpec((tm, GDN_W), row),
                   pl.BlockSpec((tm, SGU_W), row),
                   pl.BlockSpec((tm, SGU_W), row),
                   pl.BlockSpec((tm, N_GATE_COLS), row),
                   pl.BlockSpec((N_GATE_COLS, tm), lambda i: (0, i))),
        compiler_params=_params("parallel"),
        name="in_proj",
    )(x2d, mod, ng0, wqkv, wzuv, wba)


def _gdn_prep_kernel(row_len, qkv_ref, cw_ref, ba_ref, bat_ref, alog_r_ref, dtb_r_ref,
                     alog_c_ref, dtb_c_ref, u_ref, w_ref, qd_ref, kd_ref, at_ref, gl_ref):
    n = GDN_BLOCK
    c = DELTA_CHUNK
    ri = lax.broadcasted_iota(jnp.int32, (n, n), 0)
    ci = lax.broadcasted_iota(jnp.int32, (n, n), 1)
    same = (ri // c) == (ci // c)
    lower = same & (ri >= ci)
    upper = same & (ri <= ci)

    def mask01(m):
        return jnp.where(m, 1.0, 0.0).astype(BF16)

    lower_b = mask01(lower)
    upper_b = mask01(upper)
    same_b = mask01(same)

    wi = lax.broadcasted_iota(jnp.int32, (c, n), 0)
    wl = lax.broadcasted_iota(jnp.int32, (c, n), 1)
    wchunk = wl // c
    wj = wl % c
    lower_w = wi >= wj
    upper_w = wi <= wj
    diag_w = wi == wj
    eye_w = jnp.where(diag_w, 1.0, 0.0)
    pair_masks = []
    s = 1
    while s < c:
        pair_masks.append(mask01(((wi // (2 * s)) == (wj // (2 * s))) & ((wi // s) != (wj // s))))
        s *= 2

    def to_wide(full):
        out = full[:c]
        for k in range(1, CHUNKS_PER_BLOCK):
            out = jnp.where(wchunk == k, full[k * c:(k + 1) * c], out)
        return out

    def col_wide(col):
        out = jnp.broadcast_to(col[:c], (c, n))
        for k in range(1, CHUNKS_PER_BLOCK):
            out = jnp.where(wchunk == k, jnp.broadcast_to(col[k * c:(k + 1) * c], (c, n)), out)
        return out

    def block_diag(x_w):
        return jnp.concatenate([x_w] * CHUNKS_PER_BLOCK, axis=0) * same_b

    ba = ba_ref[0]
    bat = bat_ref[...]
    beta_c = _sigmoid(ba[:, :N_CHAINS])
    g_c = -jnp.exp(alog_r_ref[...]) * _softplus(ba[:, N_CHAINS:] + dtb_r_ref[...])
    g_r = -jnp.exp(alog_c_ref[...]) * _softplus(bat[N_CHAINS:] + dtb_c_ref[...])
    gc3 = _split3(g_c)
    gr3 = jnp.concatenate(_split3(g_r), axis=0)

    def sum3_r(m):
        return m[:N_CHAINS] + m[N_CHAINS:2 * N_CHAINS] + m[2 * N_CHAINS:]

    cum_f_c = _dot(lower_b, gc3[0]) + _dot(lower_b, gc3[1]) + _dot(lower_b, gc3[2])
    tot_c = _dot(same_b, gc3[0]) + _dot(same_b, gc3[1]) + _dot(same_b, gc3[2])
    cum_b_c = tot_c - cum_f_c + g_c
    cum_f_r = sum3_r(_dot(gr3, upper_b))
    cum_b_r = sum3_r(_dot(gr3, lower_b))
    ei = lax.broadcasted_iota(jnp.int32, (n, CHUNKS_PER_BLOCK * HEAD_DIM), 0) // c
    ej = lax.broadcasted_iota(jnp.int32, (n, CHUNKS_PER_BLOCK * HEAD_DIM), 1) // HEAD_DIM
    g_last = jnp.exp(sum3_r(_dot(gr3, mask01(ei == ej))))
    gl_ref[0, 0, 0] = g_last[:GDN_HEADS]
    gl_ref[0, 0, 1] = g_last[GDN_HEADS:]

    pos = lax.broadcasted_iota(jnp.int32, (n, HEAD_DIM), 0) % row_len
    first = pos == 0
    last = pos == row_len - 1

    def conv_silu(col):
        x = qkv_ref[0, :, col * HEAD_DIM:(col + 1) * HEAD_DIM].astype(F32)
        cw = cw_ref[:, col * HEAD_DIM:(col + 1) * HEAD_DIM]
        xp = jnp.where(first, 0.0, pltpu.roll(x, 1, 0))
        xn = jnp.where(last, 0.0, pltpu.roll(x, n - 1, 0))
        y = xp * cw[0:1] + x * cw[1:2] + xn * cw[2:3]
        return y * _sigmoid(y)

    def l2n(x):
        return x * lax.rsqrt(jnp.sum(x * x, -1, keepdims=True) + NORM_EPS)

    a_bs, ps, rhss = [None] * N_CHAINS, [None] * N_CHAINS, [None] * N_CHAINS
    for h in range(GDN_HEADS):
        q = l2n(conv_silu(h)) * (HEAD_DIM ** -0.5)
        k = l2n(conv_silu(GDN_HEADS + h))
        v = conv_silu(2 * GDN_HEADS + h)
        k_b = k.astype(BF16)
        qk_kk = _dot_nt(jnp.concatenate([q.astype(BF16), k_b], axis=0), k_b)
        qk_w = to_wide(qk_kk[:n])
        kk_w = to_wide(qk_kk[n:])
        for d in range(2):
            j = d * GDN_HEADS + h
            mask_w = lower_w if d == 0 else upper_w
            cum_c = (cum_f_c if d == 0 else cum_b_c)[:, j:j + 1]
            cum_r = (cum_f_r if d == 0 else cum_b_r)[j:j + 1, :]
            b_c = beta_c[:, j:j + 1]
            decay_w = jnp.where(mask_w, jnp.exp(jnp.where(mask_w, col_wide(cum_c) - cum_r, 0.0)), 0.0)
            amat_w = jnp.where(diag_w, 0.0, kk_w * decay_w * col_wide(b_c))
            a_bs[j] = amat_w.astype(BF16)
            ps[j] = eye_w - amat_w * pair_masks[0].astype(F32)
            e_c = jnp.exp(cum_c)
            rhss[j] = jnp.concatenate([(v * b_c).astype(BF16), (k * (b_c * e_c)).astype(BF16)], axis=1)
            cols = slice(j * HEAD_DIM, (j + 1) * HEAD_DIM)
            qd_ref[0, :, cols] = (q * e_c).astype(BF16)
            kd_ref[0, :, cols] = (k * jnp.exp(tot_c[:, j:j + 1] - cum_c)).astype(BF16)
            at_ref[0, 0, j * c:(j + 1) * c, :] = (qk_w * decay_w).astype(BF16)

    for pm in pair_masks[1:]:
        p_bs = [p.astype(BF16) for p in ps]
        ys = [_dot(a_bs[j] * pm, block_diag(p_bs[j])) for j in range(N_CHAINS)]
        ps = [ps[j] - _dot(p_bs[j], block_diag(ys[j].astype(BF16))) for j in range(N_CHAINS)]

    for j in range(N_CHAINS):
        uw = _dot(block_diag(ps[j].astype(BF16)), rhss[j])
        cols = slice(j * HEAD_DIM, (j + 1) * HEAD_DIM)
        u_ref[0, :, cols] = uw[:, :HEAD_DIM].astype(BF16)
        w_ref[0, :, cols] = uw[:, HEAD_DIM:].astype(BF16)


def _gdn_prep(qkv, conv_w, ba, bat, alog, dtb, row_len):
    b, l, _ = qkv.shape
    nblk = l // GDN_BLOCK
    wide = N_CHAINS * HEAD_DIM
    blk = lambda bi, i: (bi, i, 0)
    const = lambda bi, i: (0, 0)
    alog_r, dtb_r = alog.reshape(1, N_CHAINS), dtb.reshape(1, N_CHAINS)
    alog_c, dtb_c = alog.reshape(N_CHAINS, 1), dtb.reshape(N_CHAINS, 1)
    return pl.pallas_call(
        functools.partial(_gdn_prep_kernel, row_len),
        out_shape=(jax.ShapeDtypeStruct((b, l, wide), BF16),) * 4 + (
            jax.ShapeDtypeStruct((b, nblk, N_CHAINS * DELTA_CHUNK, GDN_BLOCK), BF16),
            jax.ShapeDtypeStruct((b, nblk, 2, GDN_HEADS, CHUNKS_PER_BLOCK * HEAD_DIM), F32)),
        grid=(b, nblk),
        in_specs=[pl.BlockSpec((1, GDN_BLOCK, QKV_COLS), blk),
                  pl.BlockSpec((3, QKV_COLS), const),
                  pl.BlockSpec((1, GDN_BLOCK, N_GATE_COLS), blk),
                  pl.BlockSpec((N_GATE_COLS, GDN_BLOCK), lambda bi, i: (0, bi * nblk + i)),
                  pl.BlockSpec((1, N_CHAINS), const),
                  pl.BlockSpec((1, N_CHAINS), const),
                  pl.BlockSpec((N_CHAINS, 1), const),
                  pl.BlockSpec((N_CHAINS, 1), const)],
        out_specs=(pl.BlockSpec((1, GDN_BLOCK, wide), blk),) * 4 + (
            pl.BlockSpec((1, 1, N_CHAINS * DELTA_CHUNK, GDN_BLOCK), lambda bi, i: (bi, i, 0, 0)),
            pl.BlockSpec((1, 1, 2, GDN_HEADS, CHUNKS_PER_BLOCK * HEAD_DIM),
                         lambda bi, i: (bi, i, 0, 0, 0))),
        compiler_params=_params("parallel", "parallel"),
        name="gdn_prep",
    )(qkv, conv_w, ba, bat, alog_r, dtb_r, alog_c, dtb_c)


def _gdn_scan_kernel(uf, wf, qf, kf, af, gf, ub, wb, qb, kb, ab, gb, s0_ref,
                     of_ref, ob_ref, sfin_ref, s_scr):
    i = pl.program_id(0)
    c = DELTA_CHUNK
    n_batch = s0_ref.shape[0]

    @pl.when(i == 0)
    def _():
        s_scr[...] = s0_ref[...]

    ops = ((uf, wf, qf, kf, af, gf, of_ref), (ub, wb, qb, kb, ab, gb, ob_ref))
    chains = [(bi, d, h) for bi in range(n_batch) for d in range(2) for h in range(GDN_HEADS)]
    states = [s_scr[bi, d * GDN_HEADS + h] for bi, d, h in chains]
    for step in range(CHUNKS_PER_BLOCK):
        def chunk(d):
            cc = step if d == 0 else CHUNKS_PER_BLOCK - 1 - step
            return cc, slice(cc * c, (cc + 1) * c)

        xs = []
        for j, (bi, d, h) in enumerate(chains):
            _, rows = chunk(d)
            cols = slice(h * HEAD_DIM, (h + 1) * HEAD_DIM)
            wq = jnp.concatenate([ops[d][1][bi, rows, cols], ops[d][2][bi, rows, cols]], axis=0)
            xs.append(_dot(wq, states[j].astype(BF16)))
        v_news = []
        for j, (bi, d, h) in enumerate(chains):
            _, rows = chunk(d)
            cols = slice(h * HEAD_DIM, (h + 1) * HEAD_DIM)
            v_news.append((ops[d][0][bi, rows, cols].astype(F32) - xs[j][:c]).astype(BF16))
        for j, (bi, d, h) in enumerate(chains):
            cc, rows = chunk(d)
            cols = slice(h * HEAD_DIM, (h + 1) * HEAD_DIM)
            a_c = ops[d][4][bi, 0, h * c:(h + 1) * c, cc * c:(cc + 1) * c]
            ops[d][6][bi, rows, cols] = xs[j][c:] + _dot(a_c, v_news[j])
            ds = _dot_tn(ops[d][3][bi, rows, cols], v_news[j])
            g_last = ops[d][5][bi, 0, 0, h:h + 1, cc * HEAD_DIM:(cc + 1) * HEAD_DIM]
            states[j] = states[j] * g_last + ds
    for j, (bi, d, h) in enumerate(chains):
        s_scr[bi, d * GDN_HEADS + h] = states[j]

    @pl.when(i == pl.num_programs(0) - 1)
    def _():
        sfin_ref[...] = s_scr[...]


def _gdn_scan(u, w, qd, kd, at, gl, s0):
    b, l, _ = u.shape
    nblk = l // GDN_BLOCK
    half = GDN_HEADS * HEAD_DIM
    fwd = lambda i: (0, i, 0)
    bwd = lambda i: (0, nblk - 1 - i, 1)
    big = lambda m: pl.BlockSpec((b, GDN_BLOCK, half), m)
    att_shape = (b, 1, GDN_HEADS * DELTA_CHUNK, GDN_BLOCK)
    attf = pl.BlockSpec(att_shape, lambda i: (0, i, 0, 0))
    attb = pl.BlockSpec(att_shape, lambda i: (0, nblk - 1 - i, 1, 0))
    gl_shape = (b, 1, 1, GDN_HEADS, CHUNKS_PER_BLOCK * HEAD_DIM)
    glf = pl.BlockSpec(gl_shape, lambda i: (0, i, 0, 0, 0))
    glb = pl.BlockSpec(gl_shape, lambda i: (0, nblk - 1 - i, 1, 0, 0))
    state = pl.BlockSpec((b, N_CHAINS, HEAD_DIM, HEAD_DIM), lambda i: (0, 0, 0, 0))
    return pl.pallas_call(
        _gdn_scan_kernel,
        out_shape=(jax.ShapeDtypeStruct((b, l, half), F32),
                   jax.ShapeDtypeStruct((b, l, half), F32),
                   jax.ShapeDtypeStruct((b, N_CHAINS, HEAD_DIM, HEAD_DIM), F32)),
        grid=(nblk,),
        in_specs=[big(fwd), big(fwd), big(fwd), big(fwd), attf, glf,
                  big(bwd), big(bwd), big(bwd), big(bwd), attb, glb, state],
        out_specs=(pl.BlockSpec((b, GDN_BLOCK, half), fwd),
                   pl.BlockSpec((b, GDN_BLOCK, half), lambda i: (0, nblk - 1 - i, 0)),
                   state),
        scratch_shapes=[pltpu.VMEM((b, N_CHAINS, HEAD_DIM, HEAD_DIM), F32)],
        compiler_params=_params("arbitrary"),
        name="gdn_scan",
    )(u, w, qd, kd, at, gl, u, w, qd, kd, at, gl, s0)


def _mix_into(mix_ref, of_ref, ob_ref, z_ref, u_ref, v_ref, gng_ref, lng_ref, lnb_ref, ws_ref,
              bs_ref):
    tm = of_ref.shape[0]
    o = of_ref[...] + ob_ref[...]
    z = z_ref[...].astype(F32)
    for h in range(GDN_HEADS):
        cols = slice(h * HEAD_DIM, (h + 1) * HEAD_DIM)
        oh = o[:, cols]
        zh = z[:, cols]
        r = lax.rsqrt(jnp.mean(oh * oh, -1, keepdims=True) + NORM_EPS)
        mix_ref[:, cols] = (oh * r * gng_ref[...] * (zh * _sigmoid(zh))).astype(BF16)
    for g in range(SGU_GROUPS):
        cols = slice(g * SGU_GROUP, (g + 1) * SGU_GROUP)
        vg = v_ref[:, cols].astype(F32)
        vc = vg - jnp.mean(vg, -1, keepdims=True)
        vn = vc * lax.rsqrt(jnp.mean(vc * vc, -1, keepdims=True) + NORM_EPS)
        vn = (vn * lng_ref[g:g + 1] + lnb_ref[g:g + 1]).astype(BF16)
        wsg = ws_ref[g].astype(BF16)
        for n in range(tm // SGU_CHUNK):
            rows = slice(n * SGU_CHUNK, (n + 1) * SGU_CHUNK)
            s = _dot(wsg, vn[rows]) + bs_ref[g]
            mix_ref[rows, GDN_W + g * SGU_GROUP:GDN_W + (g + 1) * SGU_GROUP] = (
                u_ref[rows, cols].astype(F32) * s).astype(BF16)


def _outproj_kernel(of_ref, ob_ref, z_ref, u_ref, v_ref, gng_ref, lng_ref, lnb_ref, ws_ref, bs_ref,
                    x_ref, mod_ref, ng_ref, wout_ref, rwt_ref, rb_ref,
                    x1_ref, h2_ref, lt_ref, mix_scr):
    subs = _sub_tiles(x_ref.shape[0])
    for sl in subs:
        _mix_into(mix_scr.at[sl], of_ref.at[sl], ob_ref.at[sl], z_ref.at[sl], u_ref.at[sl],
                  v_ref.at[sl], gng_ref, lng_ref, lnb_ref, ws_ref, bs_ref)
    mod = mod_ref[0]
    ys = [_dot(mix_scr[sl], wout_ref[...]) for sl in subs]
    w_hi, w_lo = _split2(rwt_ref[...])
    for sl, y in zip(subs, ys):
        x1 = x_ref[sl] + mod[2:3] * _rms(y, ng_ref[1:2])
        x1_ref[sl] = x1
        h2 = _rms(x1, ng_ref[2:3]) * (1.0 + mod[4:5]) + mod[3:4]
        h_hi, h_lo = _split2(h2)
        _store_planes(h2_ref, _pack_rows(h2), sl)
        lt_ref[:, sl] = (_dot_nt(w_hi, h_hi) + _dot_nt(w_lo, h_hi) + _dot_nt(w_hi, h_lo)
                         + rb_ref[...])


def _outproj(o_f, o_b, z, ug, vg, gdn_norm_g, ln_g, ln_b, w_s, b_s_full,
             x2d, mod, rows_per_mod, ng, wout, rwt, rb_col):
    t = x2d.shape[0]
    tm = min(ROW_TILE, t)
    tiles_per_mod = rows_per_mod // tm
    row = lambda i: (i, 0)
    const = lambda i: (0, 0)
    c3 = lambda i: (0, 0, 0)
    return pl.pallas_call(
        _outproj_kernel,
        out_shape=(jax.ShapeDtypeStruct((t, D_MODEL), F32),
                   jax.ShapeDtypeStruct((N_PLANES, t, SC_ROW_W), U32),
                   jax.ShapeDtypeStruct((N_EXPERTS, t), F32)),
        grid=(t // tm,),
        in_specs=[pl.BlockSpec((tm, GDN_W), row), pl.BlockSpec((tm, GDN_W), row),
                  pl.BlockSpec((tm, GDN_W), row), pl.BlockSpec((tm, SGU_W), row),
                  pl.BlockSpec((tm, SGU_W), row),
                  pl.BlockSpec((1, HEAD_DIM), const),
                  pl.BlockSpec((SGU_GROUPS, SGU_GROUP), const),
                  pl.BlockSpec((SGU_GROUPS, SGU_GROUP), const),
                  pl.BlockSpec((SGU_GROUPS, SGU_CHUNK, SGU_CHUNK), c3),
                  pl.BlockSpec((SGU_GROUPS, SGU_CHUNK, SGU_GROUP), c3),
                  pl.BlockSpec((tm, D_MODEL), row),
                  pl.BlockSpec((1, 6, D_MODEL), lambda i: (i // tiles_per_mod, 0, 0)),
                  pl.BlockSpec((4, D_MODEL), const),
                  pl.BlockSpec((D_MODEL, D_MODEL), const),
                  pl.BlockSpec((N_EXPERTS, D_MODEL), const),
                  pl.BlockSpec((N_EXPERTS, 1), const)],
        out_specs=(pl.BlockSpec((tm, D_MODEL), row),
                   pl.BlockSpec((N_PLANES, tm, SC_ROW_W), lambda i: (0, i, 0)),
                   pl.BlockSpec((N_EXPERTS, tm), lambda i: (0, i))),
        compiler_params=_params("parallel"),
        scratch_shapes=[pltpu.VMEM((tm, D_MODEL), BF16)],
        name="out_proj_router",
    )(o_f, o_b, z, ug, vg, gdn_norm_g, ln_g, ln_b, w_s, b_s_full, x2d, mod, ng, wout, rwt, rb_col)


def _moe_kernel(be_ref, slot_ref, next_ref, nb_ref, xb_ref, wgu_hbm, bgu_ref, wd_hbm, bd_ref, y_ref,
                wgu_f, wd_f, wgu_b, wd_b, gut_scr, sems):
    i = pl.program_id(0)
    live = i < nb_ref[0]
    new_expert = jnp.logical_or(i == 0, be_ref[i] != be_ref[jnp.maximum(i - 1, 0)])

    def weight_copies(expert, slot):
        return (pltpu.make_async_copy(wgu_hbm.at[expert], wgu_f.at[slot], sems.at[slot, 0]),
                pltpu.make_async_copy(wd_hbm.at[expert], wd_f.at[slot], sems.at[slot, 1]))

    @pl.when(i == 0)
    def _():
        for copy in weight_copies(be_ref[0], 0):
            copy.start()

    @pl.when(jnp.logical_and(live, new_expert))
    def _():
        slot = slot_ref[i]
        for copy in weight_copies(be_ref[i], slot):
            copy.wait()

        @pl.when(next_ref[i] >= 0)
        def _():
            for copy in weight_copies(next_ref[i], 1 - slot):
                copy.start()
        wgu_b[...] = wgu_f[slot].astype(BF16)
        wd_b[...] = wd_f[slot].astype(BF16)

    @pl.when(live)
    def _():
        xb = _unpack_rows(_load_planes(xb_ref)).astype(BF16)
        gu = _dot(xb, wgu_b[...]) + bgu_ref[0]
        gu_t = gu.T
        acts = []
        for part in range(MOE_ROWS // LANES):
            part_ref = gut_scr.at[part]
            part_ref[...] = gu_t[:, part * LANES:(part + 1) * LANES]
            gate = jnp.minimum(part_ref[pl.ds(0, D_FF, stride=2), :], SWIGLU_LIMIT)
            up = jnp.clip(part_ref[pl.ds(1, D_FF, stride=2), :], -SWIGLU_LIMIT, SWIGLU_LIMIT)
            acts.append(((up + 1.0) * gate * _sigmoid(SWIGLU_ALPHA * gate)).astype(BF16))
        act_t = jnp.concatenate(acts, axis=1)
        _store_planes(y_ref, _pack_rows(_dot_tn(act_t, wd_b[...]) + bd_ref[0]))

    @pl.when(jnp.logical_not(live))
    def _():
        y_ref[...] = jnp.zeros_like(y_ref)


def _moe_ffn(block_e, n_used, xb, w_gu, b_gu, w_down, b_down):
    n_rows = xb.shape[1]
    n_blocks = n_rows // MOE_ROWS
    idx = jnp.arange(n_blocks, dtype=jnp.int32)
    live = idx < n_used[0]
    changed = jnp.concatenate([jnp.ones((1,), bool), block_e[1:] != block_e[:-1]]) & live
    slot = ((jnp.cumsum(changed.astype(jnp.int32)) - 1) % 2).astype(jnp.int32)
    change_at = jnp.where(changed, idx, n_blocks)
    next_change = lax.cummin(jnp.concatenate([change_at[1:], jnp.full((1,), n_blocks, jnp.int32)]),
                             reverse=True)
    next_e = jnp.where(next_change < n_blocks,
                       block_e[jnp.minimum(next_change, n_blocks - 1)], -1).astype(jnp.int32)

    row = lambda i, be, sl, nx, nb: (0, i, 0)
    ex3 = lambda i, be, sl, nx, nb: (be[i], 0, 0)
    live_row = lambda i, be, sl, nx, nb: (0, jnp.minimum(i, nb[0] - 1), 0)
    planes = (N_PLANES, MOE_ROWS, SC_ROW_W)
    grid_spec = pltpu.PrefetchScalarGridSpec(
        num_scalar_prefetch=4,
        grid=(n_blocks,),
        in_specs=[pl.BlockSpec(planes, live_row),
                  pl.BlockSpec(memory_space=pl.ANY),
                  pl.BlockSpec((1, 1, 2 * D_FF), ex3),
                  pl.BlockSpec(memory_space=pl.ANY),
                  pl.BlockSpec((1, 1, D_MODEL), ex3)],
        out_specs=pl.BlockSpec(planes, row),
        scratch_shapes=[pltpu.VMEM((2, D_MODEL, 2 * D_FF), F32),
                        pltpu.VMEM((2, D_FF, D_MODEL), F32),
                        pltpu.VMEM((D_MODEL, 2 * D_FF), BF16),
                        pltpu.VMEM((D_FF, D_MODEL), BF16),
                        pltpu.VMEM((MOE_ROWS // LANES, 2 * D_FF, LANES), F32),
                        pltpu.SemaphoreType.DMA((2, 2))],
    )
    return pl.pallas_call(
        _moe_kernel,
        out_shape=jax.ShapeDtypeStruct((N_PLANES, n_rows, SC_ROW_W), U32),
        grid_spec=grid_spec,
        compiler_params=_params("arbitrary"),
        name="moe_ffn",
    )(block_e, slot, next_e, n_used, xb, w_gu, b_gu, w_down, b_down)


def _combine_kernel(y0_ref, y1_ref, y2_ref, y3_ref, gt_ref, x1_ref, mod_ref, ng_ref, o_ref):
    mod = mod_ref[0]
    gt = gt_ref[...]
    y = _unpack_rows(_load_planes(y0_ref)) * gt[:, 0:1]
    for k, y_ref in ((1, y1_ref), (2, y2_ref), (3, y3_ref)):
        y = y + _unpack_rows(_load_planes(y_ref)) * gt[:, k:k + 1]
    o_ref[...] = x1_ref[...] + mod[5:6] * _rms(y, ng_ref[3:4])


def _combine(yg, gates, x1, mod, rows_per_mod, ng):
    t = x1.shape[0]
    tm = min(COMBINE_TILE, t)
    tiles_per_mod = rows_per_mod // tm
    n_tiles = t // tm
    row = lambda i: (i, 0)
    choice = lambda k: pl.BlockSpec((N_PLANES, tm, SC_ROW_W), lambda i: (0, k * n_tiles + i, 0))
    return pl.pallas_call(
        _combine_kernel,
        out_shape=jax.ShapeDtypeStruct((t, D_MODEL), F32),
        grid=(n_tiles,),
        in_specs=[choice(0), choice(1), choice(2), choice(3),
                  pl.BlockSpec((tm, TOP_K), row),
                  pl.BlockSpec((tm, D_MODEL), row),
                  pl.BlockSpec((1, 6, D_MODEL), lambda i: (i // tiles_per_mod, 0, 0)),
                  pl.BlockSpec((4, D_MODEL), lambda i: (0, 0))],
        out_specs=pl.BlockSpec((tm, D_MODEL), row),
        compiler_params=_params("parallel"),
        name="moe_combine",
    )(yg, yg, yg, yg, gates, x1, mod, ng)


def _route_kernel(lt_ref, eidx_ref, gate_ref, rank_ref, cnt_ref, carry):
    i = pl.program_id(0)
    tile = lt_ref.shape[1]

    @pl.when(i == 0)
    def _():
        carry[...] = jnp.zeros_like(carry)

    logits = lt_ref[...]
    eio = lax.broadcasted_iota(jnp.int32, (N_EXPERTS, tile), 0).astype(F32)
    vals, sels = [], []
    for k in range(TOP_K):
        m = jnp.max(logits, axis=0, keepdims=True)
        idx = jnp.min(jnp.where(logits == m, eio, float(N_EXPERTS)), axis=0, keepdims=True)
        sel = eio == idx
        logits = jnp.where(sel, -jnp.inf, logits)
        vals.append(m)
        sels.append(sel)
        eidx_ref[k:k + 1, :] = idx.astype(jnp.int32)
    exps = [jnp.exp(v - vals[0]) for v in vals]
    denom = exps[0] + exps[1] + exps[2] + exps[3]
    for k in range(TOP_K):
        gate_ref[k:k + 1, :] = exps[k] / denom

    member = jnp.where(sels[0] | sels[1] | sels[2] | sels[3], 1.0, 0.0)
    ti = lax.broadcasted_iota(jnp.int32, (tile, tile), 0)
    tj = lax.broadcasted_iota(jnp.int32, (tile, tile), 1)
    earlier = jnp.where(ti < tj, 1.0, 0.0).astype(BF16)
    before = _dot(member.astype(BF16), earlier) + carry[...]
    for k in range(TOP_K):
        rank_ref[k:k + 1, :] = jnp.sum(jnp.where(sels[k], before, 0.0), axis=0,
                                       keepdims=True).astype(jnp.int32)
    carry[...] = carry[...] + jnp.sum(member, axis=1, keepdims=True)
    cnt_ref[...] = carry[...].astype(jnp.int32)


def _route(logits_t):
    t = logits_t.shape[1]
    tile = min(ROUTE_TILE, t)
    blk = lambda i: (0, i)
    return pl.pallas_call(
        _route_kernel,
        out_shape=(jax.ShapeDtypeStruct((TOP_K, t), jnp.int32),
                   jax.ShapeDtypeStruct((TOP_K, t), F32),
                   jax.ShapeDtypeStruct((TOP_K, t), jnp.int32),
                   jax.ShapeDtypeStruct((N_EXPERTS, 1), jnp.int32)),
        grid=(t // tile,),
        in_specs=[pl.BlockSpec((N_EXPERTS, tile), blk)],
        out_specs=(pl.BlockSpec((TOP_K, tile), blk), pl.BlockSpec((TOP_K, tile), blk),
                   pl.BlockSpec((TOP_K, tile), blk),
                   pl.BlockSpec((N_EXPERTS, 1), lambda i: (0, 0))),
        scratch_shapes=[pltpu.VMEM((N_EXPERTS, 1), F32)],
        compiler_params=_params("arbitrary"),
        name="moe_route",
    )(logits_t)


def _slot_tables(eidx, rank, counts, n_blocks):
    padded = (counts + MOE_ROWS - 1) // MOE_ROWS * MOE_ROWS
    pad_end = jnp.cumsum(padded)
    pad_start = pad_end - padded
    experts = jnp.arange(N_EXPERTS, dtype=jnp.int32)
    dest = rank + jnp.sum(jnp.where(eidx[..., None] == experts, pad_start, 0), axis=-1)
    first_row = jnp.arange(n_blocks, dtype=jnp.int32)[:, None] * MOE_ROWS
    block_e = jnp.minimum(jnp.sum((pad_end[None, :] <= first_row).astype(jnp.int32), axis=1),
                          N_EXPERTS - 1)
    n_used = pad_end[-1:] // MOE_ROWS
    return dest.astype(jnp.int32), pad_start.astype(jnp.int32), block_e, n_used.astype(jnp.int32)


def _sc_mesh():
    return plsc.VectorSubcoreMesh(core_axis_name="c", subcore_axis_name="s",
                                  num_cores=SC_CORES, num_subcores=SC_SUBCORES)


def _plane_row_ids(rows, rows_per_plane):
    return jnp.concatenate([rows + p * rows_per_plane for p in range(N_PLANES)], axis=-1)


def _sc_gather_rows(table, rows):
    v = table.shape[1]
    idx = _plane_row_ids(rows, v)[None]
    n_all = idx.shape[1]

    @functools.partial(pl.kernel, mesh=_sc_mesh(), name="moe_gather_rows",
                       out_type=jax.ShapeDtypeStruct((n_all, SC_ROW_W), U32))
    def gather(x_hbm, i_hbm, o_hbm):
        def body(i_vmem, o_vmem):
            pltpu.sync_copy(x_hbm.at[i_vmem.at[0]], o_vmem)

        pltpu.emit_pipeline(
            body, grid=(n_all // SC_WINDOW,),
            in_specs=[pl.BlockSpec((1, SC_WINDOW), lambda i: (0, i))],
            out_specs=[pl.BlockSpec((SC_WINDOW, SC_ROW_W), lambda i: (i, 0))],
            core_axis_name=("c", "s"), dimension_semantics=(pltpu.PARALLEL,),
        )(i_hbm, o_hbm)

    return gather(table.reshape(N_PLANES * v, SC_ROW_W), idx).reshape(N_PLANES, -1, SC_ROW_W)


def _sc_scatter_rows(rows, dest, n_out):
    t = rows.shape[1]
    idx = _plane_row_ids(dest, n_out)

    @functools.partial(pl.kernel, mesh=_sc_mesh(), name="moe_scatter_rows", scratch_types=[],
                       out_type=jax.ShapeDtypeStruct((N_PLANES * n_out, SC_ROW_W), U32))
    def scatter(x_hbm, i_hbm, o_hbm):
        def body(x_vmem, i_vmem):
            for k in range(TOP_K):
                pltpu.sync_copy(x_vmem, o_hbm.at[i_vmem.at[k]])

        pltpu.emit_pipeline(
            body, grid=(N_PLANES * t // SC_WINDOW,),
            in_specs=[pl.BlockSpec((SC_WINDOW, SC_ROW_W), lambda i: (i, 0)),
                      pl.BlockSpec((TOP_K, SC_WINDOW), lambda i: (0, i))],
            out_specs=[],
            core_axis_name=("c", "s"), dimension_semantics=(pltpu.PARALLEL,),
        )(x_hbm, i_hbm)

    return scatter(rows.reshape(N_PLANES * t, SC_ROW_W), idx).reshape(N_PLANES, n_out, SC_ROW_W)


def _zero_pad_kernel(cnt_ref, start_ref, xb_in_ref, xb_ref, zero_scr, sem):
    del xb_in_ref
    zero_scr[...] = jnp.zeros_like(zero_scr)

    pieces = [SUBLANES << bit for bit in range((MOE_ROWS // SUBLANES - 1).bit_length())]

    def zero_copy(p, row, size):
        return pltpu.make_async_copy(zero_scr.at[pl.ds(0, size)], xb_ref.at[p, pl.ds(row, size)], sem)

    def for_each_piece(fn):
        def per_expert(e, carry):
            n_real = cnt_ref[e]
            n_pad = (MOE_ROWS - n_real % MOE_ROWS) % MOE_ROWS
            first = start_ref[e] + n_real
            n_single = n_pad % SUBLANES
            for j in range(SUBLANES - 1):
                @pl.when(j < n_single)
                def _():
                    for p in range(N_PLANES):
                        fn(zero_copy(p, first + j, 1))
            row = first + n_single
            for size in pieces:
                @pl.when((n_pad & size) != 0)
                def _():
                    for p in range(N_PLANES):
                        fn(zero_copy(p, pl.multiple_of(row, SUBLANES), size))
                row = row + (n_pad & size)
            return carry
        lax.fori_loop(0, N_EXPERTS, per_expert, 0)

    for_each_piece(lambda copy: copy.start())
    for_each_piece(lambda copy: copy.wait())


def _zero_pad_slots(counts, pad_start, xb):
    grid_spec = pltpu.PrefetchScalarGridSpec(
        num_scalar_prefetch=2,
        grid=(1,),
        in_specs=[pl.BlockSpec(memory_space=pl.ANY)],
        out_specs=pl.BlockSpec(memory_space=pl.ANY),
        scratch_shapes=[pltpu.VMEM((MOE_ROWS // 2, SC_ROW_W), U32), pltpu.SemaphoreType.DMA],
    )
    return pl.pallas_call(
        _zero_pad_kernel,
        out_shape=jax.ShapeDtypeStruct(xb.shape, xb.dtype),
        grid_spec=grid_spec,
        input_output_aliases={2: 0},
        compiler_params=_params("arbitrary"),
        name="moe_zero_pad",
    )(counts, pad_start, xb)


def kernel(x, c, ctx, c_ctx, w_ada, b_ada, norm_g, w_in, conv_w, a_log, dt_bias, gdn_norm_g,
           sgu_ln_g, sgu_ln_b, sgu_w, sgu_b, w_out, router_w, router_b, w_gu, b_gu, w_down, b_down):
    b, l, d = x.shape
    lc = ctx.shape[1]
    t = b * l
    assert d == D_MODEL and l % ROW_TILE == 0 and l % GDN_BLOCK == 0 and lc % GDN_BLOCK == 0
    assert w_ada.shape[0] == 1, "single-layer block"

    cs = jnp.concatenate([c, c_ctx[None], jnp.zeros((8 - b - 1, d), F32)], axis=0)
    mod_all = _ada(cs, w_ada[0], b_ada[0][None])
    mod = mod_all[:b].reshape(b, 6, d)
    mod_c = mod_all[b:b + 1].reshape(1, 6, d)
    ng = norm_g[0]

    w = w_in[0]
    wqkv = w[:, :QKV_COLS].astype(BF16)
    wba = jnp.pad(w[:, QKV_COLS:QKV_COLS + N_GATE_COLS], ((0, 0), (0, LANES - N_GATE_COLS)))
    wzuv = w[:, QKV_COLS + N_GATE_COLS:].astype(BF16)

    x2d = x.reshape(t, d)
    qkv, z, ug, vg, ba, bat = _inproj(x2d, mod, l, ng[0:1], wqkv, wzuv, wba)
    ctx2d = ctx.reshape(b * lc, d)
    qkv_c, _, _, _, ba_c, bat_c = _inproj(ctx2d, mod_c, b * lc, ng[0:1], wqkv, wzuv, wba)

    alog = a_log[0].reshape(-1)
    dtb = dt_bias[0].reshape(-1)
    pc = _gdn_prep(qkv_c.reshape(b, lc, QKV_COLS), conv_w[0], ba_c.reshape(b, lc, N_GATE_COLS),
                   bat_c, alog, dtb, lc)
    s_zero = jnp.zeros((b, N_CHAINS, HEAD_DIM, HEAD_DIM), F32)
    _, _, s_ctx = _gdn_scan(*pc, s_zero)
    pp = _gdn_prep(qkv.reshape(b, l, QKV_COLS), conv_w[0], ba.reshape(b, l, N_GATE_COLS),
                   bat, alog, dtb, GRID_W)
    o_f, o_b, _ = _gdn_scan(*pp, s_ctx)

    b_s_full = jnp.broadcast_to(sgu_b[0][:, :, None], (SGU_GROUPS, SGU_CHUNK, SGU_GROUP))
    x1, h2p, logits_t = _outproj(o_f.reshape(t, GDN_W), o_b.reshape(t, GDN_W), z, ug, vg,
                                 gdn_norm_g, sgu_ln_g[0], sgu_ln_b[0], sgu_w[0], b_s_full,
                                 x2d, mod, l, ng, w_out[0].astype(BF16),
                                 router_w[0].T, router_b[0][:, None])

    eidx, gates_t, rank, counts = _route(logits_t)
    counts = counts[:, 0]
    n_blocks = -(-(t * TOP_K) // MOE_ROWS) + N_EXPERTS
    dest, pad_start, block_e, n_used = _slot_tables(eidx, rank, counts, n_blocks)
    xb = _sc_scatter_rows(h2p, dest, n_blocks * MOE_ROWS)
    xb = _zero_pad_slots(counts, pad_start, xb)
    yb = _moe_ffn(block_e, n_used, xb, w_gu[0], b_gu[0][:, None, :], w_down[0],
                  b_down[0][:, None, :])
    yg = _sc_gather_rows(yb, dest.reshape(-1))
    out = _combine(yg, gates_t.T, x1, mod, l, ng)
    return out.reshape(b, l, d)
```

```python
import functools
import math

import jax
import jax.numpy as jnp
from jax import lax
from jax.experimental import pallas as pl
from jax.experimental.pallas import tpu as pltpu
from jax.experimental.pallas import tpu_sc as plsc

F32 = jnp.float32
BF16 = jnp.bfloat16

D_MODEL = 1024
GDN_HEADS = 4
HEAD_DIM = 128
GDN_W = GDN_HEADS * HEAD_DIM
SGU_GROUPS = 4
SGU_GROUP = 128
SGU_W = SGU_GROUPS * SGU_GROUP
SGU_CHUNK = 128
DELTA_CHUNK = 64
GRID_W = 64
N_EXPERTS = 32
TOP_K = 4
D_FF = 1024
SWIGLU_LIMIT = 7.0
SWIGLU_ALPHA = 1.702
NORM_EPS = 1e-6
QKV_COLS = 3 * GDN_W
N_CHAINS = 2 * GDN_HEADS
N_GATE_COLS = 2 * N_CHAINS

ROW_TILE = 1024
SUB_ROWS = 256
COMBINE_TILE = 512
GDN_BLOCK = 256
CHUNKS_PER_BLOCK = GDN_BLOCK // DELTA_CHUNK
N_PAIR_LEVELS = DELTA_CHUNK.bit_length() - 1
MOE_ROWS = 512
U32 = jnp.uint32
LANES = 128
SUBLANES = 8
PACKED_W = D_MODEL // 2
ROUTE_TILE = 512
SC_CORES = 2
SC_SUBCORES = 16
SC_WINDOW = 128
N_PLANES = 2
SC_ROW_W = PACKED_W // N_PLANES
VMEM_LIMIT = 56 * 1024 * 1024


def _params(*sem):
    return pltpu.CompilerParams(dimension_semantics=sem, vmem_limit_bytes=VMEM_LIMIT)


def _dot(a, b):
    return jnp.dot(a, b, preferred_element_type=F32)


def _dot_nt(a, b):
    return lax.dot_general(a, b, (((1,), (1,)), ((), ())), preferred_element_type=F32)


def _dot_tn(a, b):
    return lax.dot_general(a, b, (((0,), (0,)), ((), ())), preferred_element_type=F32)


def _split2(a):
    hi = a.astype(BF16)
    lo = (a - hi.astype(F32)).astype(BF16)
    return hi, lo


def _split3(a):
    hi = a.astype(BF16)
    r = a - hi.astype(F32)
    mid = r.astype(BF16)
    lo = (r - mid.astype(F32)).astype(BF16)
    return hi, mid, lo


def _pack_rows(x32):
    xb = x32.astype(BF16).astype(F32)
    hi = lax.bitcast_convert_type(xb[:, :PACKED_W], U32)
    lo = lax.bitcast_convert_type(xb[:, PACKED_W:], U32)
    return hi | (lo >> 16)


def _store_planes(ref, packed, rows=slice(None)):
    for p in range(N_PLANES):
        ref[p, rows] = packed[:, p * SC_ROW_W:(p + 1) * SC_ROW_W]


def _load_planes(ref, rows=slice(None)):
    return jnp.concatenate([ref[p, rows] for p in range(N_PLANES)], axis=1)


def _sub_tiles(n_rows):
    return [slice(r, r + SUB_ROWS) for r in range(0, n_rows, SUB_ROWS)]


def _unpack_rows(w):
    hi = lax.bitcast_convert_type(w & jnp.uint32(0xFFFF0000), F32)
    lo = lax.bitcast_convert_type(w << 16, F32)
    return jnp.concatenate([hi, lo], axis=1)


def _rms(x32, g):
    return x32 * lax.rsqrt(jnp.mean(x32 * x32, -1, keepdims=True) + NORM_EPS) * g


def _gelu_tanh(x):
    c = math.sqrt(2.0 / math.pi)
    return 0.5 * x * (1.0 + jnp.tanh(c * (x + 0.044715 * (x * x * x))))


def _sigmoid(x):
    return 1.0 / (1.0 + jnp.exp(-x))


def _softplus(x):
    return jnp.maximum(x, 0.0) + jnp.log(1.0 + jnp.exp(-jnp.abs(x)))


def _ada_kernel(c_ref, w_ref, b_ref, o_ref):
    c = c_ref[...]
    s = c * _sigmoid(c)
    s_hi, s_lo = _split2(s)
    w_hi, w_lo = _split2(w_ref[...])
    o_ref[...] = _dot(s_hi, w_hi) + _dot(s_lo, w_hi) + _dot(s_hi, w_lo) + b_ref[...]


def _ada(cs, w_ada, b_ada):
    n = w_ada.shape[1]
    bn = D_MODEL
    return pl.pallas_call(
        _ada_kernel,
        out_shape=jax.ShapeDtypeStruct((cs.shape[0], n), F32),
        grid=(n // bn,),
        in_specs=[pl.BlockSpec(cs.shape, lambda j: (0, 0)),
                  pl.BlockSpec((D_MODEL, bn), lambda j: (0, j)),
                  pl.BlockSpec((1, bn), lambda j: (0, j))],
        out_specs=pl.BlockSpec((cs.shape[0], bn), lambda j: (0, j)),
        compiler_params=_params("parallel"),
        name="ada_mod",
    )(cs, w_ada, b_ada)


def _inproj_kernel(x_ref, mod_ref, g_ref, wqkv_ref, wzuv_ref, wba_ref,
                   qkv_ref, z_ref, u_ref, v_ref, ba_ref, bat_ref):
    mod = mod_ref[0]
    subs = _sub_tiles(x_ref.shape[0])
    hs = [_split2(_rms(x_ref[sl], g_ref[...]) * (1.0 + mod[1:2]) + mod[0:1]) for sl in subs]
    qkvs = [_dot(h_hi, wqkv_ref[...]) for h_hi, _ in hs]
    zuvs = [_dot(h_hi, wzuv_ref[...]) for h_hi, _ in hs]
    w_hi, w_lo = _split2(wba_ref[...])
    w_hi_lo = jnp.concatenate([w_hi, w_lo], axis=1)
    bas = []
    for h_hi, h_lo in hs:
        both = _dot(h_hi, w_hi_lo)
        bas.append(both[:, :LANES] + both[:, LANES:] + _dot(h_lo, w_hi))
    for sl, qkv, zuv, ba in zip(subs, qkvs, zuvs, bas):
        qkv_ref[sl] = qkv.astype(BF16)
        z_ref[sl] = zuv[:, :GDN_W].astype(BF16)
        u_ref[sl] = _gelu_tanh(zuv[:, GDN_W:GDN_W + SGU_W]).astype(BF16)
        v_ref[sl] = _gelu_tanh(zuv[:, GDN_W + SGU_W:]).astype(BF16)
        ba_ref[sl] = ba[:, :N_GATE_COLS]
        bat_ref[:, sl] = ba.T[:N_GATE_COLS]


def _inproj(x2d, mod, rows_per_mod, ng0, wqkv, wzuv, wba):
    t = x2d.shape[0]
    tm = min(ROW_TILE, t)
    tiles_per_mod = rows_per_mod // tm
    row = lambda i: (i, 0)
    const = lambda i: (0, 0)
    return pl.pallas_call(
        _inproj_kernel,
        out_shape=(jax.ShapeDtypeStruct((t, QKV_COLS), BF16),
                   jax.ShapeDtypeStruct((t, GDN_W), BF16),
                   jax.ShapeDtypeStruct((t, SGU_W), BF16),
                   jax.ShapeDtypeStruct((t, SGU_W), BF16),
                   jax.ShapeDtypeStruct((t, N_GATE_COLS), F32),
                   jax.ShapeDtypeStruct((N_GATE_COLS, t), F32)),
        grid=(t // tm,),
        in_specs=[pl.BlockSpec((tm, D_MODEL), row),
                  pl.BlockSpec((1, 6, D_MODEL), lambda i: (i // tiles_per_mod, 0, 0)),
                  pl.BlockSpec((1, D_MODEL), const),
                  pl.BlockSpec(wqkv.shape, const),
                  pl.BlockSpec(wzuv.shape, const),
                  pl.BlockSpec(wba.shape, const)],
        out_specs=(pl.BlockSpec((tm, QKV_COLS), row),
                   pl.BlockSpec((tm, GDN_W), row),
                   pl.BlockSpec((tm, SGU_W), row),
                   pl.BlockSpec((tm, SGU_W), row),
                   pl.BlockSpec((tm, N_GATE_COLS), row),
                   pl.BlockSpec((N_GATE_COLS, tm), lambda i: (0, i))),
        compiler_params=_params("parallel"),
        name="in_proj",
    )(x2d, mod, ng0, wqkv, wzuv, wba)


def _gdn_prep_kernel(row_len, qkv_ref, cw_ref, ba_ref, bat_ref, alog_r_ref, dtb_r_ref,
                     alog_c_ref, dtb_c_ref, u_ref, w_ref, qd_ref, kd_ref, at_ref, gl_ref,
                     tri_scr, pair_scr, spread_scr):
    n = GDN_BLOCK
    c = DELTA_CHUNK

    def mask01(m):
        return jnp.where(m, 1.0, 0.0).astype(BF16)

    wi = lax.broadcasted_iota(jnp.int32, (c, n), 0)
    wl = lax.broadcasted_iota(jnp.int32, (c, n), 1)
    wchunk = wl // c
    wj = wl % c
    lower_w = wi >= wj
    upper_w = wi <= wj
    diag_w = wi == wj
    eye_w = jnp.where(diag_w, 1.0, 0.0)

    @pl.when(jnp.logical_and(pl.program_id(0) == 0, pl.program_id(1) == 0))
    def _():
        ri = lax.broadcasted_iota(jnp.int32, (n, n), 0)
        ci = lax.broadcasted_iota(jnp.int32, (n, n), 1)
        same = (ri // c) == (ci // c)
        tri_scr[0] = mask01(same & (ri >= ci))
        tri_scr[1] = mask01(same & (ri <= ci))
        tri_scr[2] = mask01(same)
        for m in range(N_PAIR_LEVELS):
            s = 1 << m
            pair_scr[m] = mask01(((wi // (2 * s)) == (wj // (2 * s))) & ((wi // s) != (wj // s)))
        ei = lax.broadcasted_iota(jnp.int32, (n, CHUNKS_PER_BLOCK * HEAD_DIM), 0) // c
        ej = lax.broadcasted_iota(jnp.int32, (n, CHUNKS_PER_BLOCK * HEAD_DIM), 1) // HEAD_DIM
        spread_scr[...] = mask01(ei == ej)

    lower_b = tri_scr[0]
    upper_b = tri_scr[1]
    same_b = tri_scr[2]
    pair_masks = [pair_scr[m] for m in range(N_PAIR_LEVELS)]

    def to_wide(full):
        out = full[:c]
        for k in range(1, CHUNKS_PER_BLOCK):
            out = jnp.where(wchunk == k, full[k * c:(k + 1) * c], out)
        return out

    def col_wide(col):
        out = jnp.broadcast_to(col[:c], (c, n))
        for k in range(1, CHUNKS_PER_BLOCK):
            out = jnp.where(wchunk == k, jnp.broadcast_to(col[k * c:(k + 1) * c], (c, n)), out)
        return out

    def block_diag(x_w):
        return jnp.concatenate([x_w] * CHUNKS_PER_BLOCK, axis=0) * same_b

    ba = ba_ref[0]
    bat = bat_ref[...]
    beta_c = _sigmoid(ba[:, :N_CHAINS])
    g_c = -jnp.exp(alog_r_ref[...]) * _softplus(ba[:, N_CHAINS:] + dtb_r_ref[...])
    g_r = -jnp.exp(alog_c_ref[...]) * _softplus(bat[N_CHAINS:] + dtb_c_ref[...])
    gc3 = _split3(g_c)
    gr3 = jnp.concatenate(_split3(g_r), axis=0)

    def sum3_r(m):
        return m[:N_CHAINS] + m[N_CHAINS:2 * N_CHAINS] + m[2 * N_CHAINS:]

    cum_f_c = _dot(lower_b, gc3[0]) + _dot(lower_b, gc3[1]) + _dot(lower_b, gc3[2])
    tot_c = _dot(same_b, gc3[0]) + _dot(same_b, gc3[1]) + _dot(same_b, gc3[2])
    cum_b_c = tot_c - cum_f_c + g_c
    cum_f_r = sum3_r(_dot(gr3, upper_b))
    cum_b_r = sum3_r(_dot(gr3, lower_b))
    g_last = jnp.exp(sum3_r(_dot(gr3, spread_scr[...])))
    gl_ref[0, 0, 0] = g_last[:GDN_HEADS]
    gl_ref[0, 0, 1] = g_last[GDN_HEADS:]

    pos = lax.broadcasted_iota(jnp.int32, (n, HEAD_DIM), 0) % row_len
    first = pos == 0
    last = pos == row_len - 1

    def conv_silu(col):
        x = qkv_ref[0, :, col * HEAD_DIM:(col + 1) * HEAD_DIM].astype(F32)
        cw = cw_ref[:, col * HEAD_DIM:(col + 1) * HEAD_DIM]
        xp = jnp.where(first, 0.0, pltpu.roll(x, 1, 0))
        xn = jnp.where(last, 0.0, pltpu.roll(x, n - 1, 0))
        y = xp * cw[0:1] + x * cw[1:2] + xn * cw[2:3]
        return y * _sigmoid(y)

    def l2n(x):
        return x * lax.rsqrt(jnp.sum(x * x, -1, keepdims=True) + NORM_EPS)

    a_bs, ps, rhss = [None] * N_CHAINS, [None] * N_CHAINS, [None] * N_CHAINS
    first_pairs = pair_masks[0].astype(F32)
    for h in range(GDN_HEADS):
        q = l2n(conv_silu(h)) * (HEAD_DIM ** -0.5)
        k = l2n(conv_silu(GDN_HEADS + h))
        v = conv_silu(2 * GDN_HEADS + h)
        k_b = k.astype(BF16)
        qk_kk = _dot_nt(jnp.concatenate([q.astype(BF16), k_b], axis=0), k_b)
        qk_w = to_wide(qk_kk[:n])
        kk_w = to_wide(qk_kk[n:])
        for d in range(2):
            j = d * GDN_HEADS + h
            mask_w = lower_w if d == 0 else upper_w
            cum_c = (cum_f_c if d == 0 else cum_b_c)[:, j:j + 1]
            cum_r = (cum_f_r if d == 0 else cum_b_r)[j:j + 1, :]
            b_c = beta_c[:, j:j + 1]
            decay_w = jnp.where(mask_w, jnp.exp(jnp.where(mask_w, col_wide(cum_c) - cum_r, 0.0)), 0.0)
            amat_w = jnp.where(diag_w, 0.0, kk_w * decay_w * col_wide(b_c))
            a_bs[j] = amat_w.astype(BF16)
            ps[j] = eye_w - amat_w * first_pairs
            e_c = jnp.exp(cum_c)
            rhss[j] = jnp.concatenate([(v * b_c).astype(BF16), (k * (b_c * e_c)).astype(BF16)], axis=1)
            cols = slice(j * HEAD_DIM, (j + 1) * HEAD_DIM)
            qd_ref[0, :, cols] = (q * e_c).astype(BF16)
            kd_ref[0, :, cols] = (k * jnp.exp(tot_c[:, j:j + 1] - cum_c)).astype(BF16)
            at_ref[0, 0, j * c:(j + 1) * c, :] = (qk_w * decay_w).astype(BF16)

    for pm in pair_masks[1:]:
        p_bs = [p.astype(BF16) for p in ps]
        ys = [_dot(a_bs[j] * pm, block_diag(p_bs[j])) for j in range(N_CHAINS)]
        ps = [ps[j] - _dot(p_bs[j], block_diag(ys[j].astype(BF16))) for j in range(N_CHAINS)]

    for j in range(N_CHAINS):
        uw = _dot(block_diag(ps[j].astype(BF16)), rhss[j])
        cols = slice(j * HEAD_DIM, (j + 1) * HEAD_DIM)
        u_ref[0, :, cols] = uw[:, :HEAD_DIM].astype(BF16)
        w_ref[0, :, cols] = uw[:, HEAD_DIM:].astype(BF16)


def _gdn_prep(qkv, conv_w, ba, bat, alog, dtb, row_len):
    b, l, _ = qkv.shape
    nblk = l // GDN_BLOCK
    wide = N_CHAINS * HEAD_DIM
    blk = lambda bi, i: (bi, i, 0)
    const = lambda bi, i: (0, 0)
    alog_r, dtb_r = alog.reshape(1, N_CHAINS), dtb.reshape(1, N_CHAINS)
    alog_c, dtb_c = alog.reshape(N_CHAINS, 1), dtb.reshape(N_CHAINS, 1)
    return pl.pallas_call(
        functools.partial(_gdn_prep_kernel, row_len),
        out_shape=(jax.ShapeDtypeStruct((b, l, wide), BF16),) * 4 + (
            jax.ShapeDtypeStruct((b, nblk, N_CHAINS * DELTA_CHUNK, GDN_BLOCK), BF16),
            jax.ShapeDtypeStruct((b, nblk, 2, GDN_HEADS, CHUNKS_PER_BLOCK * HEAD_DIM), F32)),
        grid=(b, nblk),
        in_specs=[pl.BlockSpec((1, GDN_BLOCK, QKV_COLS), blk),
                  pl.BlockSpec((3, QKV_COLS), const),
                  pl.BlockSpec((1, GDN_BLOCK, N_GATE_COLS), blk),
                  pl.BlockSpec((N_GATE_COLS, GDN_BLOCK), lambda bi, i: (0, bi * nblk + i)),
                  pl.BlockSpec((1, N_CHAINS), const),
                  pl.BlockSpec((1, N_CHAINS), const),
                  pl.BlockSpec((N_CHAINS, 1), const),
                  pl.BlockSpec((N_CHAINS, 1), const)],
        out_specs=(pl.BlockSpec((1, GDN_BLOCK, wide), blk),) * 4 + (
            pl.BlockSpec((1, 1, N_CHAINS * DELTA_CHUNK, GDN_BLOCK), lambda bi, i: (bi, i, 0, 0)),
            pl.BlockSpec((1, 1, 2, GDN_HEADS, CHUNKS_PER_BLOCK * HEAD_DIM),
                         lambda bi, i: (bi, i, 0, 0, 0))),
        scratch_shapes=[pltpu.VMEM((3, GDN_BLOCK, GDN_BLOCK), BF16),
                        pltpu.VMEM((N_PAIR_LEVELS, DELTA_CHUNK, GDN_BLOCK), BF16),
                        pltpu.VMEM((GDN_BLOCK, CHUNKS_PER_BLOCK * HEAD_DIM), BF16)],
        compiler_params=_params("arbitrary", "arbitrary"),
        name="gdn_prep",
    )(qkv, conv_w, ba, bat, alog_r, dtb_r, alog_c, dtb_c)


def _gdn_scan_kernel(uf, wf, qf, kf, af, gf, ub, wb, qb, kb, ab, gb, s0_ref,
                     of_ref, ob_ref, sfin_ref, s_scr):
    i = pl.program_id(0)
    c = DELTA_CHUNK
    n_batch = s0_ref.shape[0]

    @pl.when(i == 0)
    def _():
        s_scr[...] = s0_ref[...]

    ops = ((uf, wf, qf, kf, af, gf, of_ref), (ub, wb, qb, kb, ab, gb, ob_ref))
    chains = [(bi, d, h) for bi in range(n_batch) for d in range(2) for h in range(GDN_HEADS)]
    states = [s_scr[bi, d * GDN_HEADS + h] for bi, d, h in chains]
    for step in range(CHUNKS_PER_BLOCK):
        def chunk(d):
            cc = step if d == 0 else CHUNKS_PER_BLOCK - 1 - step
            return cc, slice(cc * c, (cc + 1) * c)

        xs = []
        for j, (bi, d, h) in enumerate(chains):
            _, rows = chunk(d)
            cols = slice(h * HEAD_DIM, (h + 1) * HEAD_DIM)
            wq = jnp.concatenate([ops[d][1][bi, rows, cols], ops[d][2][bi, rows, cols]], axis=0)
            xs.append(_dot(wq, states[j].astype(BF16)))
        v_news = []
        for j, (bi, d, h) in enumerate(chains):
            _, rows = chunk(d)
            cols = slice(h * HEAD_DIM, (h + 1) * HEAD_DIM)
            v_news.append((ops[d][0][bi, rows, cols].astype(F32) - xs[j][:c]).astype(BF16))
        for j, (bi, d, h) in enumerate(chains):
            cc, rows = chunk(d)
            cols = slice(h * HEAD_DIM, (h + 1) * HEAD_DIM)
            a_c = ops[d][4][bi, 0, h * c:(h + 1) * c, cc * c:(cc + 1) * c]
            ops[d][6][bi, rows, cols] = xs[j][c:] + _dot(a_c, v_news[j])
            ds = _dot_tn(ops[d][3][bi, rows, cols], v_news[j])
            g_last = ops[d][5][bi, 0, 0, h:h + 1, cc * HEAD_DIM:(cc + 1) * HEAD_DIM]
            states[j] = states[j] * g_last + ds
    for j, (bi, d, h) in enumerate(chains):
        s_scr[bi, d * GDN_HEADS + h] = states[j]

    @pl.when(i == pl.num_programs(0) - 1)
    def _():
        sfin_ref[...] = s_scr[...]


def _gdn_scan(u, w, qd, kd, at, gl, s0):
    b, l, _ = u.shape
    nblk = l // GDN_BLOCK
    half = GDN_HEADS * HEAD_DIM
    fwd = lambda i: (0, i, 0)
    bwd = lambda i: (0, nblk - 1 - i, 1)
    big = lambda m: pl.BlockSpec((b, GDN_BLOCK, half), m)
    att_shape = (b, 1, GDN_HEADS * DELTA_CHUNK, GDN_BLOCK)
    attf = pl.BlockSpec(att_shape, lambda i: (0, i, 0, 0))
    attb = pl.BlockSpec(att_shape, lambda i: (0, nblk - 1 - i, 1, 0))
    gl_shape = (b, 1, 1, GDN_HEADS, CHUNKS_PER_BLOCK * HEAD_DIM)
    glf = pl.BlockSpec(gl_shape, lambda i: (0, i, 0, 0, 0))
    glb = pl.BlockSpec(gl_shape, lambda i: (0, nblk - 1 - i, 1, 0, 0))
    state = pl.BlockSpec((b, N_CHAINS, HEAD_DIM, HEAD_DIM), lambda i: (0, 0, 0, 0))
    return pl.pallas_call(
        _gdn_scan_kernel,
        out_shape=(jax.ShapeDtypeStruct((b, l, half), F32),
                   jax.ShapeDtypeStruct((b, l, half), F32),
                   jax.ShapeDtypeStruct((b, N_CHAINS, HEAD_DIM, HEAD_DIM), F32)),
        grid=(nblk,),
        in_specs=[big(fwd), big(fwd), big(fwd), big(fwd), attf, glf,
                  big(bwd), big(bwd), big(bwd), big(bwd), attb, glb, state],
        out_specs=(pl.BlockSpec((b, GDN_BLOCK, half), fwd),
                   pl.BlockSpec((b, GDN_BLOCK, half), lambda i: (0, nblk - 1 - i, 0)),
                   state),
        scratch_shapes=[pltpu.VMEM((b, N_CHAINS, HEAD_DIM, HEAD_DIM), F32)],
        compiler_params=_params("arbitrary"),
        name="gdn_scan",
    )(u, w, qd, kd, at, gl, u, w, qd, kd, at, gl, s0)


def _mix_into(mix_ref, of_ref, ob_ref, z_ref, u_ref, v_ref, gng_ref, lng_ref, lnb_ref, ws_ref,
              bs_ref):
    tm = of_ref.shape[0]
    o = of_ref[...] + ob_ref[...]
    z = z_ref[...].astype(F32)
    for h in range(GDN_HEADS):
        cols = slice(h * HEAD_DIM, (h + 1) * HEAD_DIM)
        oh = o[:, cols]
        zh = z[:, cols]
        r = lax.rsqrt(jnp.mean(oh * oh, -1, keepdims=True) + NORM_EPS)
        mix_ref[:, cols] = (oh * r * gng_ref[...] * (zh * _sigmoid(zh))).astype(BF16)
    for g in range(SGU_GROUPS):
        cols = slice(g * SGU_GROUP, (g + 1) * SGU_GROUP)
        vg = v_ref[:, cols].astype(F32)
        vc = vg - jnp.mean(vg, -1, keepdims=True)
        vn = vc * lax.rsqrt(jnp.mean(vc * vc, -1, keepdims=True) + NORM_EPS)
        vn = (vn * lng_ref[g:g + 1] + lnb_ref[g:g + 1]).astype(BF16)
        wsg = ws_ref[g].astype(BF16)
        for n in range(tm // SGU_CHUNK):
            rows = slice(n * SGU_CHUNK, (n + 1) * SGU_CHUNK)
            s = _dot(wsg, vn[rows]) + bs_ref[g]
            mix_ref[rows, GDN_W + g * SGU_GROUP:GDN_W + (g + 1) * SGU_GROUP] = (
                u_ref[rows, cols].astype(F32) * s).astype(BF16)


def _outproj_kernel(of_ref, ob_ref, z_ref, u_ref, v_ref, gng_ref, lng_ref, lnb_ref, ws_ref, bs_ref,
                    x_ref, mod_ref, ng_ref, wout_ref, rwt_ref, rb_ref,
                    x1_ref, h2_ref, lt_ref, mix_scr):
    subs = _sub_tiles(x_ref.shape[0])
    for sl in subs:
        _mix_into(mix_scr.at[sl], of_ref.at[sl], ob_ref.at[sl], z_ref.at[sl], u_ref.at[sl],
                  v_ref.at[sl], gng_ref, lng_ref, lnb_ref, ws_ref, bs_ref)
    mod = mod_ref[0]
    ys = [_dot(mix_scr[sl], wout_ref[...]) for sl in subs]
    w_hi, w_lo = _split2(rwt_ref[...])
    for sl, y in zip(subs, ys):
        x1 = x_ref[sl] + mod[2:3] * _rms(y, ng_ref[1:2])
        x1_ref[sl] = x1
        h2 = _rms(x1, ng_ref[2:3]) * (1.0 + mod[4:5]) + mod[3:4]
        h_hi, h_lo = _split2(h2)
        _store_planes(h2_ref, _pack_rows(h2), sl)
        lt_ref[:, sl] = (_dot_nt(w_hi, h_hi) + _dot_nt(w_lo, h_hi) + _dot_nt(w_hi, h_lo)
                         + rb_ref[...])


def _outproj(o_f, o_b, z, ug, vg, gdn_norm_g, ln_g, ln_b, w_s, b_s_full,
             x2d, mod, rows_per_mod, ng, wout, rwt, rb_col):
    t = x2d.shape[0]
    tm = min(ROW_TILE, t)
    tiles_per_mod = rows_per_mod // tm
    row = lambda i: (i, 0)
    const = lambda i: (0, 0)
    c3 = lambda i: (0, 0, 0)
    return pl.pallas_call(
        _outproj_kernel,
        out_shape=(jax.ShapeDtypeStruct((t, D_MODEL), F32),
                   jax.ShapeDtypeStruct((N_PLANES, t, SC_ROW_W), U32),
                   jax.ShapeDtypeStruct((N_EXPERTS, t), F32)),
        grid=(t // tm,),
        in_specs=[pl.BlockSpec((tm, GDN_W), row), pl.BlockSpec((tm, GDN_W), row),
                  pl.BlockSpec((tm, GDN_W), row), pl.BlockSpec((tm, SGU_W), row),
                  pl.BlockSpec((tm, SGU_W), row),
                  pl.BlockSpec((1, HEAD_DIM), const),
                  pl.BlockSpec((SGU_GROUPS, SGU_GROUP), const),
                  pl.BlockSpec((SGU_GROUPS, SGU_GROUP), const),
                  pl.BlockSpec((SGU_GROUPS, SGU_CHUNK, SGU_CHUNK), c3),
                  pl.BlockSpec((SGU_GROUPS, SGU_CHUNK, SGU_GROUP), c3),
                  pl.BlockSpec((tm, D_MODEL), row),
                  pl.BlockSpec((1, 6, D_MODEL), lambda i: (i // tiles_per_mod, 0, 0)),
                  pl.BlockSpec((4, D_MODEL), const),
                  pl.BlockSpec((D_MODEL, D_MODEL), const),
                  pl.BlockSpec((N_EXPERTS, D_MODEL), const),
                  pl.BlockSpec((N_EXPERTS, 1), const)],
        out_specs=(pl.BlockSpec((tm, D_MODEL), row),
                   pl.BlockSpec((N_PLANES, tm, SC_ROW_W), lambda i: (0, i, 0)),
                   pl.BlockSpec((N_EXPERTS, tm), lambda i: (0, i))),
        compiler_params=_params("parallel"),
        scratch_shapes=[pltpu.VMEM((tm, D_MODEL), BF16)],
        name="out_proj_router",
    )(o_f, o_b, z, ug, vg, gdn_norm_g, ln_g, ln_b, w_s, b_s_full, x2d, mod, ng, wout, rwt, rb_col)


def _moe_kernel(be_ref, slot_ref, next_ref, nb_ref, xb_ref, wgu_hbm, bgu_ref, wd_hbm, bd_ref, y_ref,
                wgu_f, wd_f, wgu_b, wd_b, gut_scr, sems):
    i = pl.program_id(0)
    live = i < nb_ref[0]
    new_expert = jnp.logical_or(i == 0, be_ref[i] != be_ref[jnp.maximum(i - 1, 0)])

    def weight_copies(expert, slot):
        return (pltpu.make_async_copy(wgu_hbm.at[expert], wgu_f.at[slot], sems.at[slot, 0]),
                pltpu.make_async_copy(wd_hbm.at[expert], wd_f.at[slot], sems.at[slot, 1]))

    @pl.when(i == 0)
    def _():
        for copy in weight_copies(be_ref[0], 0):
            copy.start()

    @pl.when(jnp.logical_and(live, new_expert))
    def _():
        slot = slot_ref[i]
        for copy in weight_copies(be_ref[i], slot):
            copy.wait()

        @pl.when(next_ref[i] >= 0)
        def _():
            for copy in weight_copies(next_ref[i], 1 - slot):
                copy.start()
        wgu_b[...] = wgu_f[slot].astype(BF16)
        wd_b[...] = wd_f[slot].astype(BF16)

    @pl.when(live)
    def _():
        xb = _unpack_rows(_load_planes(xb_ref)).astype(BF16)
        gu = _dot(xb, wgu_b[...]) + bgu_ref[0]
        gu_t = gu.T
        acts = []
        for part in range(MOE_ROWS // LANES):
            part_ref = gut_scr.at[part]
            part_ref[...] = gu_t[:, part * LANES:(part + 1) * LANES]
            gate = jnp.minimum(part_ref[pl.ds(0, D_FF, stride=2), :], SWIGLU_LIMIT)
            up = jnp.clip(part_ref[pl.ds(1, D_FF, stride=2), :], -SWIGLU_LIMIT, SWIGLU_LIMIT)
            acts.append(((up + 1.0) * gate * _sigmoid(SWIGLU_ALPHA * gate)).astype(BF16))
        act_t = jnp.concatenate(acts, axis=1)
        _store_planes(y_ref, _pack_rows(_dot_tn(act_t, wd_b[...]) + bd_ref[0]))

    @pl.when(jnp.logical_not(live))
    def _():
        y_ref[...] = jnp.zeros_like(y_ref)


def _moe_ffn(block_e, n_used, xb, w_gu, b_gu, w_down, b_down):
    n_rows = xb.shape[1]
    n_blocks = n_rows // MOE_ROWS
    idx = jnp.arange(n_blocks, dtype=jnp.int32)
    live = idx < n_used[0]
    changed = jnp.concatenate([jnp.ones((1,), bool), block_e[1:] != block_e[:-1]]) & live
    slot = ((jnp.cumsum(changed.astype(jnp.int32)) - 1) % 2).astype(jnp.int32)
    change_at = jnp.where(changed, idx, n_blocks)
    next_change = lax.cummin(jnp.concatenate([change_at[1:], jnp.full((1,), n_blocks, jnp.int32)]),
                             reverse=True)
    next_e = jnp.where(next_change < n_blocks,
                       block_e[jnp.minimum(next_change, n_blocks - 1)], -1).astype(jnp.int32)

    row = lambda i, be, sl, nx, nb: (0, i, 0)
    ex3 = lambda i, be, sl, nx, nb: (be[i], 0, 0)
    live_row = lambda i, be, sl, nx, nb: (0, jnp.minimum(i, nb[0] - 1), 0)
    planes = (N_PLANES, MOE_ROWS, SC_ROW_W)
    grid_spec = pltpu.PrefetchScalarGridSpec(
        num_scalar_prefetch=4,
        grid=(n_blocks,),
        in_specs=[pl.BlockSpec(planes, live_row),
                  pl.BlockSpec(memory_space=pl.ANY),
                  pl.BlockSpec((1, 1, 2 * D_FF), ex3),
                  pl.BlockSpec(memory_space=pl.ANY),
                  pl.BlockSpec((1, 1, D_MODEL), ex3)],
        out_specs=pl.BlockSpec(planes, row),
        scratch_shapes=[pltpu.VMEM((2, D_MODEL, 2 * D_FF), F32),
                        pltpu.VMEM((2, D_FF, D_MODEL), F32),
                        pltpu.VMEM((D_MODEL, 2 * D_FF), BF16),
                        pltpu.VMEM((D_FF, D_MODEL), BF16),
                        pltpu.VMEM((MOE_ROWS // LANES, 2 * D_FF, LANES), F32),
                        pltpu.SemaphoreType.DMA((2, 2))],
    )
    return pl.pallas_call(
        _moe_kernel,
        out_shape=jax.ShapeDtypeStruct((N_PLANES, n_rows, SC_ROW_W), U32),
        grid_spec=grid_spec,
        compiler_params=_params("arbitrary"),
        name="moe_ffn",
    )(block_e, slot, next_e, n_used, xb, w_gu, b_gu, w_down, b_down)


def _combine_kernel(y0_ref, y1_ref, y2_ref, y3_ref, gt_ref, x1_ref, mod_ref, ng_ref, o_ref):
    mod = mod_ref[0]
    gt = gt_ref[...]
    y = _unpack_rows(_load_planes(y0_ref)) * gt[:, 0:1]
    for k, y_ref in ((1, y1_ref), (2, y2_ref), (3, y3_ref)):
        y = y + _unpack_rows(_load_planes(y_ref)) * gt[:, k:k + 1]
    o_ref[...] = x1_ref[...] + mod[5:6] * _rms(y, ng_ref[3:4])


def _combine(yg, gates, x1, mod, rows_per_mod, ng):
    t = x1.shape[0]
    tm = min(COMBINE_TILE, t)
    tiles_per_mod = rows_per_mod // tm
    n_tiles = t // tm
    row = lambda i: (i, 0)
    choice = lambda k: pl.BlockSpec((N_PLANES, tm, SC_ROW_W), lambda i: (0, k * n_tiles + i, 0))
    return pl.pallas_call(
        _combine_kernel,
        out_shape=jax.ShapeDtypeStruct((t, D_MODEL), F32),
        grid=(n_tiles,),
        in_specs=[choice(0), choice(1), choice(2), choice(3),
                  pl.BlockSpec((tm, TOP_K), row),
                  pl.BlockSpec((tm, D_MODEL), row),
                  pl.BlockSpec((1, 6, D_MODEL), lambda i: (i // tiles_per_mod, 0, 0)),
                  pl.BlockSpec((4, D_MODEL), lambda i: (0, 0))],
        out_specs=pl.BlockSpec((tm, D_MODEL), row),
        compiler_params=_params("parallel"),
        name="moe_combine",
    )(yg, yg, yg, yg, gates, x1, mod, ng)


def _route_kernel(lt_ref, eidx_ref, gate_ref, rank_ref, cnt_ref, carry, earlier):
    i = pl.program_id(0)
    tile = lt_ref.shape[1]

    @pl.when(i == 0)
    def _():
        carry[...] = jnp.zeros_like(carry)
        ti = lax.broadcasted_iota(jnp.int32, (tile, tile), 0)
        tj = lax.broadcasted_iota(jnp.int32, (tile, tile), 1)
        earlier[...] = jnp.where(ti < tj, 1.0, 0.0).astype(BF16)

    logits = lt_ref[...]
    eio = lax.broadcasted_iota(jnp.int32, (N_EXPERTS, tile), 0).astype(F32)
    vals, sels = [], []
    for k in range(TOP_K):
        m = jnp.max(logits, axis=0, keepdims=True)
        idx = jnp.min(jnp.where(logits == m, eio, float(N_EXPERTS)), axis=0, keepdims=True)
        sel = eio == idx
        logits = jnp.where(sel, -jnp.inf, logits)
        vals.append(m)
        sels.append(sel)
        eidx_ref[k:k + 1, :] = idx.astype(jnp.int32)
    exps = [jnp.exp(v - vals[0]) for v in vals]
    denom = exps[0] + exps[1] + exps[2] + exps[3]
    for k in range(TOP_K):
        gate_ref[k:k + 1, :] = exps[k] / denom

    member = jnp.where(sels[0] | sels[1] | sels[2] | sels[3], 1.0, 0.0)
    before = _dot(member.astype(BF16), earlier[...]) + carry[...]
    for k in range(TOP_K):
        rank_ref[k:k + 1, :] = jnp.sum(jnp.where(sels[k], before, 0.0), axis=0,
                                       keepdims=True).astype(jnp.int32)
    carry[...] = carry[...] + jnp.sum(member, axis=1, keepdims=True)
    cnt_ref[...] = carry[...].astype(jnp.int32)


def _route(logits_t):
    t = logits_t.shape[1]
    tile = min(ROUTE_TILE, t)
    blk = lambda i: (0, i)
    return pl.pallas_call(
        _route_kernel,
        out_shape=(jax.ShapeDtypeStruct((TOP_K, t), jnp.int32),
                   jax.ShapeDtypeStruct((TOP_K, t), F32),
                   jax.ShapeDtypeStruct((TOP_K, t), jnp.int32),
                   jax.ShapeDtypeStruct((N_EXPERTS, 1), jnp.int32)),
        grid=(t // tile,),
        in_specs=[pl.BlockSpec((N_EXPERTS, tile), blk)],
        out_specs=(pl.BlockSpec((TOP_K, tile), blk), pl.BlockSpec((TOP_K, tile), blk),
                   pl.BlockSpec((TOP_K, tile), blk),
                   pl.BlockSpec((N_EXPERTS, 1), lambda i: (0, 0))),
        scratch_shapes=[pltpu.VMEM((N_EXPERTS, 1), F32), pltpu.VMEM((tile, tile), BF16)],
        compiler_params=_params("arbitrary"),
        name="moe_route",
    )(logits_t)


def _slot_tables(eidx, rank, counts, n_blocks):
    padded = (counts + MOE_ROWS - 1) // MOE_ROWS * MOE_ROWS
    pad_end = jnp.cumsum(padded)
    pad_start = pad_end - padded
    experts = jnp.arange(N_EXPERTS, dtype=jnp.int32)
    dest = rank + jnp.sum(jnp.where(eidx[..., None] == experts, pad_start, 0), axis=-1)
    first_row = jnp.arange(n_blocks, dtype=jnp.int32)[:, None] * MOE_ROWS
    block_e = jnp.minimum(jnp.sum((pad_end[None, :] <= first_row).astype(jnp.int32), axis=1),
                          N_EXPERTS - 1)
    n_used = pad_end[-1:] // MOE_ROWS
    return dest.astype(jnp.int32), pad_start.astype(jnp.int32), block_e, n_used.astype(jnp.int32)


def _sc_mesh():
    return plsc.VectorSubcoreMesh(core_axis_name="c", subcore_axis_name="s",
                                  num_cores=SC_CORES, num_subcores=SC_SUBCORES)


def _plane_row_ids(rows, rows_per_plane):
    return jnp.concatenate([rows + p * rows_per_plane for p in range(N_PLANES)], axis=-1)


def _sc_gather_rows(table, rows):
    v = table.shape[1]
    idx = _plane_row_ids(rows, v)[None]
    n_all = idx.shape[1]

    @functools.partial(pl.kernel, mesh=_sc_mesh(), name="moe_gather_rows",
                       out_type=jax.ShapeDtypeStruct((n_all, SC_ROW_W), U32))
    def gather(x_hbm, i_hbm, o_hbm):
        def body(i_vmem, o_vmem):
            pltpu.sync_copy(x_hbm.at[i_vmem.at[0]], o_vmem)

        pltpu.emit_pipeline(
            body, grid=(n_all // SC_WINDOW,),
            in_specs=[pl.BlockSpec((1, SC_WINDOW), lambda i: (0, i))],
            out_specs=[pl.BlockSpec((SC_WINDOW, SC_ROW_W), lambda i: (i, 0))],
            core_axis_name=("c", "s"), dimension_semantics=(pltpu.PARALLEL,),
        )(i_hbm, o_hbm)

    return gather(table.reshape(N_PLANES * v, SC_ROW_W), idx).reshape(N_PLANES, -1, SC_ROW_W)


def _sc_scatter_rows(rows, dest, n_out):
    t = rows.shape[1]
    idx = _plane_row_ids(dest, n_out)

    @functools.partial(pl.kernel, mesh=_sc_mesh(), name="moe_scatter_rows", scratch_types=[],
                       out_type=jax.ShapeDtypeStruct((N_PLANES * n_out, SC_ROW_W), U32))
    def scatter(x_hbm, i_hbm, o_hbm):
        def body(x_vmem, i_vmem):
            for k in range(TOP_K):
                pltpu.sync_copy(x_vmem, o_hbm.at[i_vmem.at[k]])

        pltpu.emit_pipeline(
            body, grid=(N_PLANES * t // SC_WINDOW,),
            in_specs=[pl.BlockSpec((SC_WINDOW, SC_ROW_W), lambda i: (i, 0)),
                      pl.BlockSpec((TOP_K, SC_WINDOW), lambda i: (0, i))],
            out_specs=[],
            core_axis_name=("c", "s"), dimension_semantics=(pltpu.PARALLEL,),
        )(x_hbm, i_hbm)

    return scatter(rows.reshape(N_PLANES * t, SC_ROW_W), idx).reshape(N_PLANES, n_out, SC_ROW_W)


def _zero_pad_kernel(cnt_ref, start_ref, xb_in_ref, xb_ref, zero_scr, sem):
    del xb_in_ref
    zero_scr[...] = jnp.zeros_like(zero_scr)

    pieces = [SUBLANES << bit for bit in range((MOE_ROWS // SUBLANES - 1).bit_length())]

    def zero_copy(p, row, size):
        return pltpu.make_async_copy(zero_scr.at[pl.ds(0, size)], xb_ref.at[p, pl.ds(row, size)], sem)

    def for_each_piece(fn):
        def per_expert(e, carry):
            n_real = cnt_ref[e]
            n_pad = (MOE_ROWS - n_real % MOE_ROWS) % MOE_ROWS
            first = start_ref[e] + n_real
            n_single = n_pad % SUBLANES
            for j in range(SUBLANES - 1):
                @pl.when(j < n_single)
                def _():
                    for p in range(N_PLANES):
                        fn(zero_copy(p, first + j, 1))
            row = first + n_single
            for size in pieces:
                @pl.when((n_pad & size) != 0)
                def _():
                    for p in range(N_PLANES):
                        fn(zero_copy(p, pl.multiple_of(row, SUBLANES), size))
                row = row + (n_pad & size)
            return carry
        lax.fori_loop(0, N_EXPERTS, per_expert, 0)

    for_each_piece(lambda copy: copy.start())
    for_each_piece(lambda copy: copy.wait())


def _zero_pad_slots(counts, pad_start, xb):
    grid_spec = pltpu.PrefetchScalarGridSpec(
        num_scalar_prefetch=2,
        grid=(1,),
        in_specs=[pl.BlockSpec(memory_space=pl.ANY)],
        out_specs=pl.BlockSpec(memory_space=pl.ANY),
        scratch_shapes=[pltpu.VMEM((MOE_ROWS // 2, SC_ROW_W), U32), pltpu.SemaphoreType.DMA],
    )
    return pl.pallas_call(
        _zero_pad_kernel,
        out_shape=jax.ShapeDtypeStruct(xb.shape, xb.dtype),
        grid_spec=grid_spec,
        input_output_aliases={2: 0},
        compiler_params=_params("arbitrary"),
        name="moe_zero_pad",
    )(counts, pad_start, xb)


def kernel(x, c, ctx, c_ctx, w_ada, b_ada, norm_g, w_in, conv_w, a_log, dt_bias, gdn_norm_g,
           sgu_ln_g, sgu_ln_b, sgu_w, sgu_b, w_out, router_w, router_b, w_gu, b_gu, w_down, b_down):
    b, l, d = x.shape
    lc = ctx.shape[1]
    t = b * l
    assert d == D_MODEL and l % ROW_TILE == 0 and l % GDN_BLOCK == 0 and lc % GDN_BLOCK == 0
    assert w_ada.shape[0] == 1, "single-layer block"

    cs = jnp.concatenate([c, c_ctx[None], jnp.zeros((8 - b - 1, d), F32)], axis=0)
    mod_all = _ada(cs, w_ada[0], b_ada[0][None])
    mod = mod_all[:b].reshape(b, 6, d)
    mod_c = mod_all[b:b + 1].reshape(1, 6, d)
    ng = norm_g[0]

    w = w_in[0]
    wqkv = w[:, :QKV_COLS].astype(BF16)
    wba = jnp.pad(w[:, QKV_COLS:QKV_COLS + N_GATE_COLS], ((0, 0), (0, LANES - N_GATE_COLS)))
    wzuv = w[:, QKV_COLS + N_GATE_COLS:].astype(BF16)

    x2d = x.reshape(t, d)
    qkv, z, ug, vg, ba, bat = _inproj(x2d, mod, l, ng[0:1], wqkv, wzuv, wba)
    ctx2d = ctx.reshape(b * lc, d)
    qkv_c, _, _, _, ba_c, bat_c = _inproj(ctx2d, mod_c, b * lc, ng[0:1], wqkv, wzuv, wba)

    alog = a_log[0].reshape(-1)
    dtb = dt_bias[0].reshape(-1)
    pc = _gdn_prep(qkv_c.reshape(b, lc, QKV_COLS), conv_w[0], ba_c.reshape(b, lc, N_GATE_COLS),
                   bat_c, alog, dtb, lc)
    s_zero = jnp.zeros((b, N_CHAINS, HEAD_DIM, HEAD_DIM), F32)
    _, _, s_ctx = _gdn_scan(*pc, s_zero)
    pp = _gdn_prep(qkv.reshape(b, l, QKV_COLS), conv_w[0], ba.reshape(b, l, N_GATE_COLS),
                   bat, alog, dtb, GRID_W)
    o_f, o_b, _ = _gdn_scan(*pp, s_ctx)

    b_s_full = jnp.broadcast_to(sgu_b[0][:, :, None], (SGU_GROUPS, SGU_CHUNK, SGU_GROUP))
    x1, h2p, logits_t = _outproj(o_f.reshape(t, GDN_W), o_b.reshape(t, GDN_W), z, ug, vg,
                                 gdn_norm_g, sgu_ln_g[0], sgu_ln_b[0], sgu_w[0], b_s_full,
                                 x2d, mod, l, ng, w_out[0].astype(BF16),
                                 router_w[0].T, router_b[0][:, None])

    eidx, gates_t, rank, counts = _route(logits_t)
    counts = counts[:, 0]
    n_blocks = -(-(t * TOP_K) // MOE_ROWS) + N_EXPERTS
    dest, pad_start, block_e, n_used = _slot_tables(eidx, rank, counts, n_blocks)
    xb = _sc_scatter_rows(h2p, dest, n_blocks * MOE_ROWS)
    xb = _zero_pad_slots(counts, pad_start, xb)
    yb = _moe_ffn(block_e, n_used, xb, w_gu[0], b_gu[0][:, None, :], w_down[0],
                  b_down[0][:, None, :])
    yg = _sc_gather_rows(yb, dest.reshape(-1))
    out = _combine(yg, gates_t.T, x1, mod, l, ng)
    return out.reshape(b, l, d)
```

```python
import functools
import math

import jax
import jax.numpy as jnp
from jax import lax
from jax.experimental import pallas as pl
from jax.experimental.pallas import tpu as pltpu
from jax.experimental.pallas import tpu_sc as plsc

F32 = jnp.float32
BF16 = jnp.bfloat16

D_MODEL = 1024
GDN_HEADS = 4
HEAD_DIM = 128
GDN_W = GDN_HEADS * HEAD_DIM
SGU_GROUPS = 4
SGU_GROUP = 128
SGU_W = SGU_GROUPS * SGU_GROUP
SGU_CHUNK = 128
DELTA_CHUNK = 64
GRID_W = 64
N_EXPERTS = 32
TOP_K = 4
D_FF = 1024
SWIGLU_LIMIT = 7.0
SWIGLU_ALPHA = 1.702
NORM_EPS = 1e-6
QKV_COLS = 3 * GDN_W
N_CHAINS = 2 * GDN_HEADS
N_GATE_COLS = 2 * N_CHAINS

ROW_TILE = 1024
SUB_ROWS = 256
COMBINE_TILE = 512
GDN_BLOCK = 256
CHUNKS_PER_BLOCK = GDN_BLOCK // DELTA_CHUNK
N_PAIR_LEVELS = DELTA_CHUNK.bit_length() - 1
MOE_ROWS = 512
U32 = jnp.uint32
LANES = 128
SUBLANES = 8
PACKED_W = D_MODEL // 2
ROUTE_TILE = 512
SC_CORES = 2
SC_SUBCORES = 16
SC_WINDOW = 128
N_PLANES = 2
SC_ROW_W = PACKED_W // N_PLANES
VMEM_LIMIT = 56 * 1024 * 1024


def _params(*sem):
    return pltpu.CompilerParams(dimension_semantics=sem, vmem_limit_bytes=VMEM_LIMIT)


def _dot(a, b):
    return jnp.dot(a, b, preferred_element_type=F32)


def _dot_nt(a, b):
    return lax.dot_general(a, b, (((1,), (1,)), ((), ())), preferred_element_type=F32)


def _dot_tn(a, b):
    return lax.dot_general(a, b, (((0,), (0,)), ((), ())), preferred_element_type=F32)


def _split2(a):
    hi = a.astype(BF16)
    lo = (a - hi.astype(F32)).astype(BF16)
    return hi, lo


def _split3(a):
    hi = a.astype(BF16)
    r = a - hi.astype(F32)
    mid = r.astype(BF16)
    lo = (r - mid.astype(F32)).astype(BF16)
    return hi, mid, lo


def _pack_rows(x):
    xb = x.astype(BF16).astype(F32)
    hi = lax.bitcast_convert_type(xb[:, :PACKED_W], U32)
    lo = lax.bitcast_convert_type(xb[:, PACKED_W:], U32)
    return hi | (lo >> 16)


def _store_planes(ref, packed, rows=slice(None)):
    for p in range(N_PLANES):
        ref[p, rows] = packed[:, p * SC_ROW_W:(p + 1) * SC_ROW_W]


def _load_planes(ref, rows=slice(None)):
    return jnp.concatenate([ref[p, rows] for p in range(N_PLANES)], axis=1)


def _sub_tiles(n_rows):
    return [slice(r, r + SUB_ROWS) for r in range(0, n_rows, SUB_ROWS)]


def _unpack_rows(w):
    hi = lax.bitcast_convert_type(w & jnp.uint32(0xFFFF0000), F32)
    lo = lax.bitcast_convert_type(w << 16, F32)
    return jnp.concatenate([hi, lo], axis=1)


def _rms(x32, g):
    return x32 * lax.rsqrt(jnp.mean(x32 * x32, -1, keepdims=True) + NORM_EPS) * g


def _gelu_tanh(x):
    c = math.sqrt(2.0 / math.pi)
    return 0.5 * x * (1.0 + jnp.tanh(c * (x + 0.044715 * (x * x * x))))


def _sigmoid(x):
    return 1.0 / (1.0 + jnp.exp(-x))


def _softplus(x):
    return jnp.maximum(x, 0.0) + jnp.log(1.0 + jnp.exp(-jnp.abs(x)))


def _ada_kernel(c_ref, w_ref, b_ref, o_ref):
    c = c_ref[...]
    s = c * _sigmoid(c)
    s_hi, s_lo = _split2(s)
    w_hi, w_lo = _split2(w_ref[...])
    o_ref[...] = _dot(s_hi, w_hi) + _dot(s_lo, w_hi) + _dot(s_hi, w_lo) + b_ref[...]


def _ada(cs, w_ada, b_ada):
    n = w_ada.shape[1]
    bn = D_MODEL
    return pl.pallas_call(
        _ada_kernel,
        out_shape=jax.ShapeDtypeStruct((cs.shape[0], n), F32),
        grid=(n // bn,),
        in_specs=[pl.BlockSpec(cs.shape, lambda j: (0, 0)),
                  pl.BlockSpec((D_MODEL, bn), lambda j: (0, j)),
                  pl.BlockSpec((1, bn), lambda j: (0, j))],
        out_specs=pl.BlockSpec((cs.shape[0], bn), lambda j: (0, j)),
        compiler_params=_params("parallel"),
        name="ada_mod",
    )(cs, w_ada, b_ada)


def _inproj_kernel(x_ref, mod_ref, g_ref, wqkv_ref, wzuv_ref, wba_ref,
                   qkv_ref, z_ref, u_ref, v_ref, ba_ref, bat_ref):
    mod = mod_ref[0]
    subs = _sub_tiles(x_ref.shape[0])
    hs = [_split2(_rms(x_ref[sl], g_ref[...]) * (1.0 + mod[1:2]) + mod[0:1]) for sl in subs]
    qkvs = [_dot(h_hi, wqkv_ref[...]) for h_hi, _ in hs]
    zuvs = [_dot(h_hi, wzuv_ref[...]) for h_hi, _ in hs]
    w_hi, w_lo = _split2(wba_ref[...])
    w_hi_lo = jnp.concatenate([w_hi, w_lo], axis=1)
    bas = []
    for h_hi, h_lo in hs:
        both = _dot(h_hi, w_hi_lo)
        bas.append(both[:, :LANES] + both[:, LANES:] + _dot(h_lo, w_hi))
    for sl, qkv, zuv, ba in zip(subs, qkvs, zuvs, bas):
        qkv_ref[sl] = qkv.astype(BF16)
        z_ref[sl] = zuv[:, :GDN_W].astype(BF16)
        u_ref[sl] = _gelu_tanh(zuv[:, GDN_W:GDN_W + SGU_W]).astype(BF16)
        v_ref[sl] = _gelu_tanh(zuv[:, GDN_W + SGU_W:]).astype(BF16)
        ba_ref[sl] = ba[:, :N_GATE_COLS]
        bat_ref[:, sl] = ba.T[:N_GATE_COLS]


def _inproj(x2d, mod, rows_per_mod, ng0, wqkv, wzuv, wba):
    t = x2d.shape[0]
    tm = min(ROW_TILE, t)
    tiles_per_mod = rows_per_mod // tm
    row = lambda i: (i, 0)
    const = lambda i: (0, 0)
    return pl.pallas_call(
        _inproj_kernel,
        out_shape=(jax.ShapeDtypeStruct((t, QKV_COLS), BF16),
                   jax.ShapeDtypeStruct((t, GDN_W), BF16),
                   jax.ShapeDtypeStruct((t, SGU_W), BF16),
                   jax.ShapeDtypeStruct((t, SGU_W), BF16),
                   jax.ShapeDtypeStruct((t, N_GATE_COLS), F32),
                   jax.ShapeDtypeStruct((N_GATE_COLS, t), F32)),
        grid=(t // tm,),
        in_specs=[pl.BlockSpec((tm, D_MODEL), row),
                  pl.BlockSpec((1, 6, D_MODEL), lambda i: (i // tiles_per_mod, 0, 0)),
                  pl.BlockSpec((1, D_MODEL), const),
                  pl.BlockSpec(wqkv.shape, const),
                  pl.BlockSpec(wzuv.shape, const),
                  pl.BlockSpec(wba.shape, const)],
        out_specs=(pl.BlockSpec((tm, QKV_COLS), row),
                   pl.BlockSpec((tm, GDN_W), row),
                   pl.BlockSpec((tm, SGU_W), row),
                   pl.BlockSpec((tm, SGU_W), row),
                   pl.BlockSpec((tm, N_GATE_COLS), row),
                   pl.BlockSpec((N_GATE_COLS, tm), lambda i: (0, i))),
        compiler_params=_params("parallel"),
        name="in_proj",
    )(x2d, mod, ng0, wqkv, wzuv, wba)


def _gdn_prep_kernel(row_len, qkv_ref, cw_ref, ba_ref, bat_ref, alog_r_ref, dtb_r_ref,
                     alog_c_ref, dtb_c_ref, u_ref, w_ref, qd_ref, kd_ref, at_ref, gl_ref,
                     tri_scr, pair_scr, spread_scr):
    n = GDN_BLOCK
    c = DELTA_CHUNK

    def mask01(m):
        return jnp.where(m, 1.0, 0.0).astype(BF16)

    wi = lax.broadcasted_iota(jnp.int32, (c, n), 0)
    wl = lax.broadcasted_iota(jnp.int32, (c, n), 1)
    wchunk = wl // c
    wj = wl % c
    lower_w = wi >= wj
    upper_w = wi <= wj
    diag_w = wi == wj
    eye_w = jnp.where(diag_w, 1.0, 0.0)

    @pl.when(jnp.logical_and(pl.program_id(0) == 0, pl.program_id(1) == 0))
    def _():
        ri = lax.broadcasted_iota(jnp.int32, (n, n), 0)
        ci = lax.broadcasted_iota(jnp.int32, (n, n), 1)
        same = (ri // c) == (ci // c)
        tri_scr[0] = mask01(same & (ri >= ci))
        tri_scr[1] = mask01(same & (ri <= ci))
        tri_scr[2] = mask01(same)
        for m in range(N_PAIR_LEVELS):
            s = 1 << m
            pair_scr[m] = mask01(((wi // (2 * s)) == (wj // (2 * s))) & ((wi // s) != (wj // s)))
        ei = lax.broadcasted_iota(jnp.int32, (n, CHUNKS_PER_BLOCK * HEAD_DIM), 0) // c
        ej = lax.broadcasted_iota(jnp.int32, (n, CHUNKS_PER_BLOCK * HEAD_DIM), 1) // HEAD_DIM
        spread_scr[...] = mask01(ei == ej)

    lower_b = tri_scr[0]
    upper_b = tri_scr[1]
    same_b = tri_scr[2]
    pair_masks = [pair_scr[m] for m in range(N_PAIR_LEVELS)]

    def to_wide(full):
        out = full[:c]
        for k in range(1, CHUNKS_PER_BLOCK):
            out = jnp.where(wchunk == k, full[k * c:(k + 1) * c], out)
        return out

    def col_wide(col):
        out = jnp.broadcast_to(col[:c], (c, n))
        for k in range(1, CHUNKS_PER_BLOCK):
            out = jnp.where(wchunk == k, jnp.broadcast_to(col[k * c:(k + 1) * c], (c, n)), out)
        return out

    def block_diag(x_w):
        return jnp.concatenate([x_w] * CHUNKS_PER_BLOCK, axis=0) * same_b

    ba = ba_ref[0]
    bat = bat_ref[...]
    beta_c = _sigmoid(ba[:, :N_CHAINS])
    g_c = -jnp.exp(alog_r_ref[...]) * _softplus(ba[:, N_CHAINS:] + dtb_r_ref[...])
    g_r = -jnp.exp(alog_c_ref[...]) * _softplus(bat[N_CHAINS:] + dtb_c_ref[...])
    gc3 = _split3(g_c)
    gr3 = jnp.concatenate(_split3(g_r), axis=0)

    def sum3_r(m):
        return m[:N_CHAINS] + m[N_CHAINS:2 * N_CHAINS] + m[2 * N_CHAINS:]

    cum_f_c = _dot(lower_b, gc3[0]) + _dot(lower_b, gc3[1]) + _dot(lower_b, gc3[2])
    tot_c = _dot(same_b, gc3[0]) + _dot(same_b, gc3[1]) + _dot(same_b, gc3[2])
    cum_b_c = tot_c - cum_f_c + g_c
    cum_f_r = sum3_r(_dot(gr3, upper_b))
    cum_b_r = sum3_r(_dot(gr3, lower_b))
    g_last = jnp.exp(sum3_r(_dot(gr3, spread_scr[...])))
    gl_ref[0, 0, 0] = g_last[:GDN_HEADS]
    gl_ref[0, 0, 1] = g_last[GDN_HEADS:]

    pos = lax.broadcasted_iota(jnp.int32, (n, HEAD_DIM), 0) % row_len
    first = pos == 0
    last = pos == row_len - 1

    def conv_silu(col):
        x = qkv_ref[0, :, col * HEAD_DIM:(col + 1) * HEAD_DIM].astype(F32)
        cw = cw_ref[:, col * HEAD_DIM:(col + 1) * HEAD_DIM]
        xp = jnp.where(first, 0.0, pltpu.roll(x, 1, 0))
        xn = jnp.where(last, 0.0, pltpu.roll(x, n - 1, 0))
        y = xp * cw[0:1] + x * cw[1:2] + xn * cw[2:3]
        return y * _sigmoid(y)

    def l2n(x):
        return x * lax.rsqrt(jnp.sum(x * x, -1, keepdims=True) + NORM_EPS)

    a_bs, ps, rhss = [None] * N_CHAINS, [None] * N_CHAINS, [None] * N_CHAINS
    first_pairs = pair_masks[0].astype(F32)
    for h in range(GDN_HEADS):
        q = l2n(conv_silu(h)) * (HEAD_DIM ** -0.5)
        k = l2n(conv_silu(GDN_HEADS + h))
        v = conv_silu(2 * GDN_HEADS + h)
        k_b = k.astype(BF16)
        qk_kk = _dot_nt(jnp.concatenate([q.astype(BF16), k_b], axis=0), k_b)
        qk_w = to_wide(qk_kk[:n])
        kk_w = to_wide(qk_kk[n:])
        for d in range(2):
            j = d * GDN_HEADS + h
            mask_w = lower_w if d == 0 else upper_w
            cum_c = (cum_f_c if d == 0 else cum_b_c)[:, j:j + 1]
            cum_r = (cum_f_r if d == 0 else cum_b_r)[j:j + 1, :]
            b_c = beta_c[:, j:j + 1]
            decay_w = jnp.where(mask_w, jnp.exp(jnp.where(mask_w, col_wide(cum_c) - cum_r, 0.0)), 0.0)
            amat_w = jnp.where(diag_w, 0.0, kk_w * decay_w * col_wide(b_c))
            a_bs[j] = amat_w.astype(BF16)
            ps[j] = eye_w - amat_w * first_pairs
            e_c = jnp.exp(cum_c)
            rhss[j] = jnp.concatenate([(v * b_c).astype(BF16), (k * (b_c * e_c)).astype(BF16)], axis=1)
            cols = slice(j * HEAD_DIM, (j + 1) * HEAD_DIM)
            qd_ref[0, :, cols] = (q * e_c).astype(BF16)
            kd_ref[0, :, cols] = (k * jnp.exp(tot_c[:, j:j + 1] - cum_c)).astype(BF16)
            at_ref[0, 0, j * c:(j + 1) * c, :] = (qk_w * decay_w).astype(BF16)

    for pm in pair_masks[1:]:
        p_bs = [p.astype(BF16) for p in ps]
        ys = [_dot(a_bs[j] * pm, block_diag(p_bs[j])) for j in range(N_CHAINS)]
        ps = [ps[j] - _dot(p_bs[j], block_diag(ys[j].astype(BF16))) for j in range(N_CHAINS)]

    for j in range(N_CHAINS):
        uw = _dot(block_diag(ps[j].astype(BF16)), rhss[j])
        cols = slice(j * HEAD_DIM, (j + 1) * HEAD_DIM)
        u_ref[0, :, cols] = uw[:, :HEAD_DIM].astype(BF16)
        w_ref[0, :, cols] = uw[:, HEAD_DIM:].astype(BF16)


def _gdn_prep(qkv, conv_w, ba, bat, alog, dtb, row_len):
    b, l, _ = qkv.shape
    nblk = l // GDN_BLOCK
    wide = N_CHAINS * HEAD_DIM
    blk = lambda bi, i: (bi, i, 0)
    const = lambda bi, i: (0, 0)
    alog_r, dtb_r = alog.reshape(1, N_CHAINS), dtb.reshape(1, N_CHAINS)
    alog_c, dtb_c = alog.reshape(N_CHAINS, 1), dtb.reshape(N_CHAINS, 1)
    return pl.pallas_call(
        functools.partial(_gdn_prep_kernel, row_len),
        out_shape=(jax.ShapeDtypeStruct((b, l, wide), BF16),) * 4 + (
            jax.ShapeDtypeStruct((b, nblk, N_CHAINS * DELTA_CHUNK, GDN_BLOCK), BF16),
            jax.ShapeDtypeStruct((b, nblk, 2, GDN_HEADS, CHUNKS_PER_BLOCK * HEAD_DIM), F32)),
        grid=(b, nblk),
        in_specs=[pl.BlockSpec((1, GDN_BLOCK, QKV_COLS), blk),
                  pl.BlockSpec((3, QKV_COLS), const),
                  pl.BlockSpec((1, GDN_BLOCK, N_GATE_COLS), blk),
                  pl.BlockSpec((N_GATE_COLS, GDN_BLOCK), lambda bi, i: (0, bi * nblk + i)),
                  pl.BlockSpec((1, N_CHAINS), const),
                  pl.BlockSpec((1, N_CHAINS), const),
                  pl.BlockSpec((N_CHAINS, 1), const),
                  pl.BlockSpec((N_CHAINS, 1), const)],
        out_specs=(pl.BlockSpec((1, GDN_BLOCK, wide), blk),) * 4 + (
            pl.BlockSpec((1, 1, N_CHAINS * DELTA_CHUNK, GDN_BLOCK), lambda bi, i: (bi, i, 0, 0)),
            pl.BlockSpec((1, 1, 2, GDN_HEADS, CHUNKS_PER_BLOCK * HEAD_DIM),
                         lambda bi, i: (bi, i, 0, 0, 0))),
        scratch_shapes=[pltpu.VMEM((3, GDN_BLOCK, GDN_BLOCK), BF16),
                        pltpu.VMEM((N_PAIR_LEVELS, DELTA_CHUNK, GDN_BLOCK), BF16),
                        pltpu.VMEM((GDN_BLOCK, CHUNKS_PER_BLOCK * HEAD_DIM), BF16)],
        compiler_params=_params("arbitrary", "arbitrary"),
        name="gdn_prep",
    )(qkv, conv_w, ba, bat, alog_r, dtb_r, alog_c, dtb_c)


def _gdn_scan_kernel(uf, wf, qf, kf, af, gf, ub, wb, qb, kb, ab, gb, s0_ref,
                     of_ref, ob_ref, sfin_ref, s_scr):
    i = pl.program_id(0)
    c = DELTA_CHUNK
    n_batch = s0_ref.shape[0]

    @pl.when(i == 0)
    def _():
        s_scr[...] = s0_ref[...]

    ops = ((uf, wf, qf, kf, af, gf, of_ref), (ub, wb, qb, kb, ab, gb, ob_ref))
    chains = [(bi, d, h) for bi in range(n_batch) for d in range(2) for h in range(GDN_HEADS)]
    states = [s_scr[bi, d * GDN_HEADS + h] for bi, d, h in chains]
    for step in range(CHUNKS_PER_BLOCK):
        def chunk(d):
            cc = step if d == 0 else CHUNKS_PER_BLOCK - 1 - step
            return cc, slice(cc * c, (cc + 1) * c)

        xs = []
        for j, (bi, d, h) in enumerate(chains):
            _, rows = chunk(d)
            cols = slice(h * HEAD_DIM, (h + 1) * HEAD_DIM)
            wq = jnp.concatenate([ops[d][1][bi, rows, cols], ops[d][2][bi, rows, cols]], axis=0)
            xs.append(_dot(wq, states[j].astype(BF16)))
        v_news = []
        for j, (bi, d, h) in enumerate(chains):
            _, rows = chunk(d)
            cols = slice(h * HEAD_DIM, (h + 1) * HEAD_DIM)
            v_news.append((ops[d][0][bi, rows, cols].astype(F32) - xs[j][:c]).astype(BF16))
        for j, (bi, d, h) in enumerate(chains):
            cc, rows = chunk(d)
            cols = slice(h * HEAD_DIM, (h + 1) * HEAD_DIM)
            a_c = ops[d][4][bi, 0, h * c:(h + 1) * c, cc * c:(cc + 1) * c]
            ops[d][6][bi, rows, cols] = (xs[j][c:] + _dot(a_c, v_news[j])).astype(BF16)
            ds = _dot_tn(ops[d][3][bi, rows, cols], v_news[j])
            g_last = ops[d][5][bi, 0, 0, h:h + 1, cc * HEAD_DIM:(cc + 1) * HEAD_DIM]
            states[j] = states[j] * g_last + ds
    for j, (bi, d, h) in enumerate(chains):
        s_scr[bi, d * GDN_HEADS + h] = states[j]

    @pl.when(i == pl.num_programs(0) - 1)
    def _():
        sfin_ref[...] = s_scr[...]


def _gdn_scan(u, w, qd, kd, at, gl, s0):
    b, l, _ = u.shape
    nblk = l // GDN_BLOCK
    half = GDN_HEADS * HEAD_DIM
    fwd = lambda i: (0, i, 0)
    bwd = lambda i: (0, nblk - 1 - i, 1)
    big = lambda m: pl.BlockSpec((b, GDN_BLOCK, half), m)
    att_shape = (b, 1, GDN_HEADS * DELTA_CHUNK, GDN_BLOCK)
    attf = pl.BlockSpec(att_shape, lambda i: (0, i, 0, 0))
    attb = pl.BlockSpec(att_shape, lambda i: (0, nblk - 1 - i, 1, 0))
    gl_shape = (b, 1, 1, GDN_HEADS, CHUNKS_PER_BLOCK * HEAD_DIM)
    glf = pl.BlockSpec(gl_shape, lambda i: (0, i, 0, 0, 0))
    glb = pl.BlockSpec(gl_shape, lambda i: (0, nblk - 1 - i, 1, 0, 0))
    state = pl.BlockSpec((b, N_CHAINS, HEAD_DIM, HEAD_DIM), lambda i: (0, 0, 0, 0))
    return pl.pallas_call(
        _gdn_scan_kernel,
        out_shape=(jax.ShapeDtypeStruct((b, l, half), BF16),
                   jax.ShapeDtypeStruct((b, l, half), BF16),
                   jax.ShapeDtypeStruct((b, N_CHAINS, HEAD_DIM, HEAD_DIM), F32)),
        grid=(nblk,),
        in_specs=[big(fwd), big(fwd), big(fwd), big(fwd), attf, glf,
                  big(bwd), big(bwd), big(bwd), big(bwd), attb, glb, state],
        out_specs=(pl.BlockSpec((b, GDN_BLOCK, half), fwd),
                   pl.BlockSpec((b, GDN_BLOCK, half), lambda i: (0, nblk - 1 - i, 0)),
                   state),
        scratch_shapes=[pltpu.VMEM((b, N_CHAINS, HEAD_DIM, HEAD_DIM), F32)],
        compiler_params=_params("arbitrary"),
        name="gdn_scan",
    )(u, w, qd, kd, at, gl, u, w, qd, kd, at, gl, s0)


def _mix_into(mix_ref, of_ref, ob_ref, z_ref, u_ref, v_ref, gng_ref, lng_ref, lnb_ref, ws_ref,
              bs_ref):
    tm = of_ref.shape[0]
    o = of_ref[...].astype(F32) + ob_ref[...].astype(F32)
    z = z_ref[...].astype(F32)
    for h in range(GDN_HEADS):
        cols = slice(h * HEAD_DIM, (h + 1) * HEAD_DIM)
        oh = o[:, cols]
        zh = z[:, cols]
        r = lax.rsqrt(jnp.mean(oh * oh, -1, keepdims=True) + NORM_EPS)
        mix_ref[:, cols] = (oh * r * gng_ref[...] * (zh * _sigmoid(zh))).astype(BF16)
    for g in range(SGU_GROUPS):
        cols = slice(g * SGU_GROUP, (g + 1) * SGU_GROUP)
        vg = v_ref[:, cols].astype(F32)
        vc = vg - jnp.mean(vg, -1, keepdims=True)
        vn = vc * lax.rsqrt(jnp.mean(vc * vc, -1, keepdims=True) + NORM_EPS)
        vn = (vn * lng_ref[g:g + 1] + lnb_ref[g:g + 1]).astype(BF16)
        wsg = ws_ref[g].astype(BF16)
        for n in range(tm // SGU_CHUNK):
            rows = slice(n * SGU_CHUNK, (n + 1) * SGU_CHUNK)
            s = _dot(wsg, vn[rows]) + bs_ref[g]
            mix_ref[rows, GDN_W + g * SGU_GROUP:GDN_W + (g + 1) * SGU_GROUP] = (
                u_ref[rows, cols].astype(F32) * s).astype(BF16)


def _outproj_kernel(of_ref, ob_ref, z_ref, u_ref, v_ref, gng_ref, lng_ref, lnb_ref, ws_ref, bs_ref,
                    x_ref, mod_ref, ng_ref, wout_ref, rwt_ref, rb_ref,
                    x1_ref, h2_ref, lt_ref, mix_scr):
    subs = _sub_tiles(x_ref.shape[0])
    for sl in subs:
        _mix_into(mix_scr.at[sl], of_ref.at[sl], ob_ref.at[sl], z_ref.at[sl], u_ref.at[sl],
                  v_ref.at[sl], gng_ref, lng_ref, lnb_ref, ws_ref, bs_ref)
    mod = mod_ref[0]
    ys = [_dot(mix_scr[sl], wout_ref[...]) for sl in subs]
    w_hi, w_lo = _split2(rwt_ref[...])
    for sl, y in zip(subs, ys):
        x1 = x_ref[sl] + mod[2:3] * _rms(y, ng_ref[1:2])
        x1_ref[sl] = x1
        h2 = _rms(x1, ng_ref[2:3]) * (1.0 + mod[4:5]) + mod[3:4]
        h_hi, h_lo = _split2(h2)
        _store_planes(h2_ref, _pack_rows(h_hi), sl)
        lt_ref[:, sl] = (_dot_nt(w_hi, h_hi) + _dot_nt(w_lo, h_hi) + _dot_nt(w_hi, h_lo)
                         + rb_ref[...])


def _outproj(o_f, o_b, z, ug, vg, gdn_norm_g, ln_g, ln_b, w_s, b_s_full,
             x2d, mod, rows_per_mod, ng, wout, rwt, rb_col):
    t = x2d.shape[0]
    tm = min(ROW_TILE, t)
    tiles_per_mod = rows_per_mod // tm
    row = lambda i: (i, 0)
    const = lambda i: (0, 0)
    c3 = lambda i: (0, 0, 0)
    return pl.pallas_call(
        _outproj_kernel,
        out_shape=(jax.ShapeDtypeStruct((t, D_MODEL), F32),
                   jax.ShapeDtypeStruct((N_PLANES, t, SC_ROW_W), U32),
                   jax.ShapeDtypeStruct((N_EXPERTS, t), F32)),
        grid=(t // tm,),
        in_specs=[pl.BlockSpec((tm, GDN_W), row), pl.BlockSpec((tm, GDN_W), row),
                  pl.BlockSpec((tm, GDN_W), row), pl.BlockSpec((tm, SGU_W), row),
                  pl.BlockSpec((tm, SGU_W), row),
                  pl.BlockSpec((1, HEAD_DIM), const),
                  pl.BlockSpec((SGU_GROUPS, SGU_GROUP), const),
                  pl.BlockSpec((SGU_GROUPS, SGU_GROUP), const),
                  pl.BlockSpec((SGU_GROUPS, SGU_CHUNK, SGU_CHUNK), c3),
                  pl.BlockSpec((SGU_GROUPS, SGU_CHUNK, SGU_GROUP), c3),
                  pl.BlockSpec((tm, D_MODEL), row),
                  pl.BlockSpec((1, 6, D_MODEL), lambda i: (i // tiles_per_mod, 0, 0)),
                  pl.BlockSpec((4, D_MODEL), const),
                  pl.BlockSpec((D_MODEL, D_MODEL), const),
                  pl.BlockSpec((N_EXPERTS, D_MODEL), const),
                  pl.BlockSpec((N_EXPERTS, 1), const)],
        out_specs=(pl.BlockSpec((tm, D_MODEL), row),
                   pl.BlockSpec((N_PLANES, tm, SC_ROW_W), lambda i: (0, i, 0)),
                   pl.BlockSpec((N_EXPERTS, tm), lambda i: (0, i))),
        compiler_params=_params("parallel"),
        scratch_shapes=[pltpu.VMEM((tm, D_MODEL), BF16)],
        name="out_proj_router",
    )(o_f, o_b, z, ug, vg, gdn_norm_g, ln_g, ln_b, w_s, b_s_full, x2d, mod, ng, wout, rwt, rb_col)


def _moe_kernel(be_ref, slot_ref, next_ref, nb_ref, xb_ref, wgu_hbm, bgu_ref, wd_hbm, bd_ref, y_ref,
                wgu_f, wd_f, wgu_b, wd_b, gut_scr, sems):
    i = pl.program_id(0)
    live = i < nb_ref[0]
    new_expert = jnp.logical_or(i == 0, be_ref[i] != be_ref[jnp.maximum(i - 1, 0)])

    def weight_copies(expert, slot):
        return (pltpu.make_async_copy(wgu_hbm.at[expert], wgu_f.at[slot], sems.at[slot, 0]),
                pltpu.make_async_copy(wd_hbm.at[expert], wd_f.at[slot], sems.at[slot, 1]))

    @pl.when(i == 0)
    def _():
        for copy in weight_copies(be_ref[0], 0):
            copy.start()

    @pl.when(jnp.logical_and(live, new_expert))
    def _():
        slot = slot_ref[i]
        for copy in weight_copies(be_ref[i], slot):
            copy.wait()

        @pl.when(next_ref[i] >= 0)
        def _():
            for copy in weight_copies(next_ref[i], 1 - slot):
                copy.start()
        wgu_b[...] = wgu_f[slot].astype(BF16)
        wd_b[...] = wd_f[slot].astype(BF16)

    @pl.when(live)
    def _():
        xb = _unpack_rows(_load_planes(xb_ref)).astype(BF16)
        gu = _dot(xb, wgu_b[...]) + bgu_ref[0]
        gu_t = gu.T
        acts = []
        for part in range(MOE_ROWS // LANES):
            part_ref = gut_scr.at[part]
            part_ref[...] = gu_t[:, part * LANES:(part + 1) * LANES]
            gate = jnp.minimum(part_ref[pl.ds(0, D_FF, stride=2), :], SWIGLU_LIMIT)
            up = jnp.clip(part_ref[pl.ds(1, D_FF, stride=2), :], -SWIGLU_LIMIT, SWIGLU_LIMIT)
            acts.append(((up + 1.0) * gate * _sigmoid(SWIGLU_ALPHA * gate)).astype(BF16))
        act_t = jnp.concatenate(acts, axis=1)
        _store_planes(y_ref, _pack_rows(_dot_tn(act_t, wd_b[...]) + bd_ref[0]))

    @pl.when(jnp.logical_not(live))
    def _():
        y_ref[...] = jnp.zeros_like(y_ref)


def _moe_ffn(block_e, n_used, xb, w_gu, b_gu, w_down, b_down):
    n_rows = xb.shape[1]
    n_blocks = n_rows // MOE_ROWS
    idx = jnp.arange(n_blocks, dtype=jnp.int32)
    live = idx < n_used[0]
    changed = jnp.concatenate([jnp.ones((1,), bool), block_e[1:] != block_e[:-1]]) & live
    slot = ((jnp.cumsum(changed.astype(jnp.int32)) - 1) % 2).astype(jnp.int32)
    change_at = jnp.where(changed, idx, n_blocks)
    next_change = lax.cummin(jnp.concatenate([change_at[1:], jnp.full((1,), n_blocks, jnp.int32)]),
                             reverse=True)
    next_e = jnp.where(next_change < n_blocks,
                       block_e[jnp.minimum(next_change, n_blocks - 1)], -1).astype(jnp.int32)

    row = lambda i, be, sl, nx, nb: (0, i, 0)
    ex3 = lambda i, be, sl, nx, nb: (be[i], 0, 0)
    live_row = lambda i, be, sl, nx, nb: (0, jnp.minimum(i, nb[0] - 1), 0)
    planes = (N_PLANES, MOE_ROWS, SC_ROW_W)
    grid_spec = pltpu.PrefetchScalarGridSpec(
        num_scalar_prefetch=4,
        grid=(n_blocks,),
        in_specs=[pl.BlockSpec(planes, live_row),
                  pl.BlockSpec(memory_space=pl.ANY),
                  pl.BlockSpec((1, 1, 2 * D_FF), ex3),
                  pl.BlockSpec(memory_space=pl.ANY),
                  pl.BlockSpec((1, 1, D_MODEL), ex3)],
        out_specs=pl.BlockSpec(planes, row),
        scratch_shapes=[pltpu.VMEM((2, D_MODEL, 2 * D_FF), F32),
                        pltpu.VMEM((2, D_FF, D_MODEL), F32),
                        pltpu.VMEM((D_MODEL, 2 * D_FF), BF16),
                        pltpu.VMEM((D_FF, D_MODEL), BF16),
                        pltpu.VMEM((MOE_ROWS // LANES, 2 * D_FF, LANES), F32),
                        pltpu.SemaphoreType.DMA((2, 2))],
    )
    return pl.pallas_call(
        _moe_kernel,
        out_shape=jax.ShapeDtypeStruct((N_PLANES, n_rows, SC_ROW_W), U32),
        grid_spec=grid_spec,
        compiler_params=_params("arbitrary"),
        name="moe_ffn",
    )(block_e, slot, next_e, n_used, xb, w_gu, b_gu, w_down, b_down)


def _combine_kernel(y0_ref, y1_ref, y2_ref, y3_ref, gt_ref, x1_ref, mod_ref, ng_ref, o_ref):
    mod = mod_ref[0]
    gt = gt_ref[...]
    y = _unpack_rows(_load_planes(y0_ref)) * gt[:, 0:1]
    for k, y_ref in ((1, y1_ref), (2, y2_ref), (3, y3_ref)):
        y = y + _unpack_rows(_load_planes(y_ref)) * gt[:, k:k + 1]
    o_ref[...] = x1_ref[...] + mod[5:6] * _rms(y, ng_ref[3:4])


def _combine(yg, gates, x1, mod, rows_per_mod, ng):
    t = x1.shape[0]
    tm = min(COMBINE_TILE, t)
    tiles_per_mod = rows_per_mod // tm
    n_tiles = t // tm
    row = lambda i: (i, 0)
    choice = lambda k: pl.BlockSpec((N_PLANES, tm, SC_ROW_W), lambda i: (0, k * n_tiles + i, 0))
    return pl.pallas_call(
        _combine_kernel,
        out_shape=jax.ShapeDtypeStruct((t, D_MODEL), F32),
        grid=(n_tiles,),
        in_specs=[choice(0), choice(1), choice(2), choice(3),
                  pl.BlockSpec((tm, TOP_K), row),
                  pl.BlockSpec((tm, D_MODEL), row),
                  pl.BlockSpec((1, 6, D_MODEL), lambda i: (i // tiles_per_mod, 0, 0)),
                  pl.BlockSpec((4, D_MODEL), lambda i: (0, 0))],
        out_specs=pl.BlockSpec((tm, D_MODEL), row),
        compiler_params=_params("parallel"),
        name="moe_combine",
    )(yg, yg, yg, yg, gates, x1, mod, ng)


def _route_kernel(lt_ref, eidx_ref, gate_ref, rank_ref, cnt_ref, carry, earlier):
    i = pl.program_id(0)
    tile = lt_ref.shape[1]

    @pl.when(i == 0)
    def _():
        carry[...] = jnp.zeros_like(carry)
        ti = lax.broadcasted_iota(jnp.int32, (tile, tile), 0)
        tj = lax.broadcasted_iota(jnp.int32, (tile, tile), 1)
        earlier[...] = jnp.where(ti < tj, 1.0, 0.0).astype(BF16)

    logits = lt_ref[...]
    eio = lax.broadcasted_iota(jnp.int32, (N_EXPERTS, tile), 0).astype(F32)
    vals, sels = [], []
    for k in range(TOP_K):
        m = jnp.max(logits, axis=0, keepdims=True)
        idx = jnp.min(jnp.where(logits == m, eio, float(N_EXPERTS)), axis=0, keepdims=True)
        sel = eio == idx
        logits = jnp.where(sel, -jnp.inf, logits)
        vals.append(m)
        sels.append(sel)
        eidx_ref[k:k + 1, :] = idx.astype(jnp.int32)
    exps = [jnp.exp(v - vals[0]) for v in vals]
    denom = exps[0] + exps[1] + exps[2] + exps[3]
    for k in range(TOP_K):
        gate_ref[k:k + 1, :] = exps[k] / denom

    member = jnp.where(sels[0] | sels[1] | sels[2] | sels[3], 1.0, 0.0)
    before = _dot(member.astype(BF16), earlier[...]) + carry[...]
    for k in range(TOP_K):
        rank_ref[k:k + 1, :] = jnp.sum(jnp.where(sels[k], before, 0.0), axis=0,
                                       keepdims=True).astype(jnp.int32)
    carry[...] = carry[...] + jnp.sum(member, axis=1, keepdims=True)
    cnt_ref[...] = carry[...].astype(jnp.int32)


def _route(logits_t):
    t = logits_t.shape[1]
    tile = min(ROUTE_TILE, t)
    blk = lambda i: (0, i)
    return pl.pallas_call(
        _route_kernel,
        out_shape=(jax.ShapeDtypeStruct((TOP_K, t), jnp.int32),
                   jax.ShapeDtypeStruct((TOP_K, t), F32),
                   jax.ShapeDtypeStruct((TOP_K, t), jnp.int32),
                   jax.ShapeDtypeStruct((N_EXPERTS, 1), jnp.int32)),
        grid=(t // tile,),
        in_specs=[pl.BlockSpec((N_EXPERTS, tile), blk)],
        out_specs=(pl.BlockSpec((TOP_K, tile), blk), pl.BlockSpec((TOP_K, tile), blk),
                   pl.BlockSpec((TOP_K, tile), blk),
                   pl.BlockSpec((N_EXPERTS, 1), lambda i: (0, 0))),
        scratch_shapes=[pltpu.VMEM((N_EXPERTS, 1), F32), pltpu.VMEM((tile, tile), BF16)],
        compiler_params=_params("arbitrary"),
        name="moe_route",
    )(logits_t)


def _slot_tables(eidx, rank, counts, n_blocks):
    padded = (counts + MOE_ROWS - 1) // MOE_ROWS * MOE_ROWS
    pad_end = jnp.cumsum(padded)
    pad_start = pad_end - padded
    experts = jnp.arange(N_EXPERTS, dtype=jnp.int32)
    dest = rank + jnp.sum(jnp.where(eidx[..., None] == experts, pad_start, 0), axis=-1)
    first_row = jnp.arange(n_blocks, dtype=jnp.int32)[:, None] * MOE_ROWS
    block_e = jnp.minimum(jnp.sum((pad_end[None, :] <= first_row).astype(jnp.int32), axis=1),
                          N_EXPERTS - 1)
    n_used = pad_end[-1:] // MOE_ROWS
    return dest.astype(jnp.int32), pad_start.astype(jnp.int32), block_e, n_used.astype(jnp.int32)


def _sc_mesh():
    return plsc.VectorSubcoreMesh(core_axis_name="c", subcore_axis_name="s",
                                  num_cores=SC_CORES, num_subcores=SC_SUBCORES)


def _plane_row_ids(rows, rows_per_plane):
    return jnp.concatenate([rows + p * rows_per_plane for p in range(N_PLANES)], axis=-1)


def _sc_gather_rows(table, rows):
    v = table.shape[1]
    idx = _plane_row_ids(rows, v)[None]
    n_all = idx.shape[1]

    @functools.partial(pl.kernel, mesh=_sc_mesh(), name="moe_gather_rows",
                       out_type=jax.ShapeDtypeStruct((n_all, SC_ROW_W), U32))
    def gather(x_hbm, i_hbm, o_hbm):
        def body(i_vmem, o_vmem):
            pltpu.sync_copy(x_hbm.at[i_vmem.at[0]], o_vmem)

        pltpu.emit_pipeline(
            body, grid=(n_all // SC_WINDOW,),
            in_specs=[pl.BlockSpec((1, SC_WINDOW), lambda i: (0, i))],
            out_specs=[pl.BlockSpec((SC_WINDOW, SC_ROW_W), lambda i: (i, 0))],
            core_axis_name=("c", "s"), dimension_semantics=(pltpu.PARALLEL,),
        )(i_hbm, o_hbm)

    return gather(table.reshape(N_PLANES * v, SC_ROW_W), idx).reshape(N_PLANES, -1, SC_ROW_W)


def _sc_scatter_rows(rows, dest, n_out):
    t = rows.shape[1]
    idx = _plane_row_ids(dest, n_out)

    @functools.partial(pl.kernel, mesh=_sc_mesh(), name="moe_scatter_rows", scratch_types=[],
                       out_type=jax.ShapeDtypeStruct((N_PLANES * n_out, SC_ROW_W), U32))
    def scatter(x_hbm, i_hbm, o_hbm):
        def body(x_vmem, i_vmem):
            for k in range(TOP_K):
                pltpu.sync_copy(x_vmem, o_hbm.at[i_vmem.at[k]])

        pltpu.emit_pipeline(
            body, grid=(N_PLANES * t // SC_WINDOW,),
            in_specs=[pl.BlockSpec((SC_WINDOW, SC_ROW_W), lambda i: (i, 0)),
                      pl.BlockSpec((TOP_K, SC_WINDOW), lambda i: (0, i))],
            out_specs=[],
            core_axis_name=("c", "s"), dimension_semantics=(pltpu.PARALLEL,),
        )(x_hbm, i_hbm)

    return scatter(rows.reshape(N_PLANES * t, SC_ROW_W), idx).reshape(N_PLANES, n_out, SC_ROW_W)


def _zero_pad_kernel(cnt_ref, start_ref, xb_in_ref, xb_ref, zero_scr, sem):
    del xb_in_ref
    zero_scr[...] = jnp.zeros_like(zero_scr)

    pieces = [SUBLANES << bit for bit in range((MOE_ROWS // SUBLANES - 1).bit_length())]

    def zero_copy(p, row, size):
        return pltpu.make_async_copy(zero_scr.at[pl.ds(0, size)], xb_ref.at[p, pl.ds(row, size)], sem)

    def for_each_piece(fn):
        def per_expert(e, carry):
            n_real = cnt_ref[e]
            n_pad = (MOE_ROWS - n_real % MOE_ROWS) % MOE_ROWS
            first = start_ref[e] + n_real
            n_single = n_pad % SUBLANES
            for j in range(SUBLANES - 1):
                @pl.when(j < n_single)
                def _():
                    for p in range(N_PLANES):
                        fn(zero_copy(p, first + j, 1))
            row = first + n_single
            for size in pieces:
                @pl.when((n_pad & size) != 0)
                def _():
                    for p in range(N_PLANES):
                        fn(zero_copy(p, pl.multiple_of(row, SUBLANES), size))
                row = row + (n_pad & size)
            return carry
        lax.fori_loop(0, N_EXPERTS, per_expert, 0)

    for_each_piece(lambda copy: copy.start())
    for_each_piece(lambda copy: copy.wait())


def _zero_pad_slots(counts, pad_start, xb):
    grid_spec = pltpu.PrefetchScalarGridSpec(
        num_scalar_prefetch=2,
        grid=(1,),
        in_specs=[pl.BlockSpec(memory_space=pl.ANY)],
        out_specs=pl.BlockSpec(memory_space=pl.ANY),
        scratch_shapes=[pltpu.VMEM((MOE_ROWS // 2, SC_ROW_W), U32), pltpu.SemaphoreType.DMA],
    )
    return pl.pallas_call(
        _zero_pad_kernel,
        out_shape=jax.ShapeDtypeStruct(xb.shape, xb.dtype),
        grid_spec=grid_spec,
        input_output_aliases={2: 0},
        compiler_params=_params("arbitrary"),
        name="moe_zero_pad",
    )(counts, pad_start, xb)


def kernel(x, c, ctx, c_ctx, w_ada, b_ada, norm_g, w_in, conv_w, a_log, dt_bias, gdn_norm_g,
           sgu_ln_g, sgu_ln_b, sgu_w, sgu_b, w_out, router_w, router_b, w_gu, b_gu, w_down, b_down):
    b, l, d = x.shape
    lc = ctx.shape[1]
    t = b * l
    assert d == D_MODEL and l % ROW_TILE == 0 and l % GDN_BLOCK == 0 and lc % GDN_BLOCK == 0
    assert w_ada.shape[0] == 1, "single-layer block"

    cs = jnp.concatenate([c, c_ctx[None], jnp.zeros((8 - b - 1, d), F32)], axis=0)
    mod_all = _ada(cs, w_ada[0], b_ada[0][None])
    mod = mod_all[:b].reshape(b, 6, d)
    mod_c = mod_all[b:b + 1].reshape(1, 6, d)
    ng = norm_g[0]

    w = w_in[0]
    wqkv = w[:, :QKV_COLS].astype(BF16)
    wba = jnp.pad(w[:, QKV_COLS:QKV_COLS + N_GATE_COLS], ((0, 0), (0, LANES - N_GATE_COLS)))
    wzuv = w[:, QKV_COLS + N_GATE_COLS:].astype(BF16)

    x2d = x.reshape(t, d)
    qkv, z, ug, vg, ba, bat = _inproj(x2d, mod, l, ng[0:1], wqkv, wzuv, wba)
    ctx2d = ctx.reshape(b * lc, d)
    qkv_c, _, _, _, ba_c, bat_c = _inproj(ctx2d, mod_c, b * lc, ng[0:1], wqkv, wzuv, wba)

    alog = a_log[0].reshape(-1)
    dtb = dt_bias[0].reshape(-1)
    pc = _gdn_prep(qkv_c.reshape(b, lc, QKV_COLS), conv_w[0], ba_c.reshape(b, lc, N_GATE_COLS),
                   bat_c, alog, dtb, lc)
    s_zero = jnp.zeros((b, N_CHAINS, HEAD_DIM, HEAD_DIM), F32)
    _, _, s_ctx = _gdn_scan(*pc, s_zero)
    pp = _gdn_prep(qkv.reshape(b, l, QKV_COLS), conv_w[0], ba.reshape(b, l, N_GATE_COLS),
                   bat, alog, dtb, GRID_W)
    o_f, o_b, _ = _gdn_scan(*pp, s_ctx)

    b_s_full = jnp.broadcast_to(sgu_b[0][:, :, None], (SGU_GROUPS, SGU_CHUNK, SGU_GROUP))
    x1, h2p, logits_t = _outproj(o_f.reshape(t, GDN_W), o_b.reshape(t, GDN_W), z, ug, vg,
                                 gdn_norm_g, sgu_ln_g[0], sgu_ln_b[0], sgu_w[0], b_s_full,
                                 x2d, mod, l, ng, w_out[0].astype(BF16),
                                 router_w[0].T, router_b[0][:, None])

    eidx, gates_t, rank, counts = _route(logits_t)
    counts = counts[:, 0]
    n_blocks = -(-(t * TOP_K) // MOE_ROWS) + N_EXPERTS
    dest, pad_start, block_e, n_used = _slot_tables(eidx, rank, counts, n_blocks)
    xb = _sc_scatter_rows(h2p, dest, n_blocks * MOE_ROWS)
    xb = _zero_pad_slots(counts, pad_start, xb)
    yb = _moe_ffn(block_e, n_used, xb, w_gu[0], b_gu[0][:, None, :], w_down[0],
                  b_down[0][:, None, :])
    yg = _sc_gather_rows(yb, dest.reshape(-1))
    out = _combine(yg, gates_t.T, x1, mod, l, ng)
    return out.reshape(b, l, d)
```

```python
import functools
import math

import jax
import jax.numpy as jnp
from jax import lax
from jax.experimental import pallas as pl
from jax.experimental.pallas import tpu as pltpu
from jax.experimental.pallas import tpu_sc as plsc

F32 = jnp.float32
BF16 = jnp.bfloat16

D_MODEL = 1024
GDN_HEADS = 4
HEAD_DIM = 128
GDN_W = GDN_HEADS * HEAD_DIM
SGU_GROUPS = 4
SGU_GROUP = 128
SGU_W = SGU_GROUPS * SGU_GROUP
SGU_CHUNK = 128
DELTA_CHUNK = 64
GRID_W = 64
N_EXPERTS = 32
TOP_K = 4
D_FF = 1024
SWIGLU_LIMIT = 7.0
SWIGLU_ALPHA = 1.702
NORM_EPS = 1e-6
QKV_COLS = 3 * GDN_W
N_CHAINS = 2 * GDN_HEADS
N_GATE_COLS = 2 * N_CHAINS

ROW_TILE = 1024
SUB_ROWS = 256
COMBINE_TILE = 512
GDN_BLOCK = 256
CHUNKS_PER_BLOCK = GDN_BLOCK // DELTA_CHUNK
OPND_U, OPND_W, OPND_QD, OPND_KD = range(4)
N_OPNDS = 4
N_PAIR_LEVELS = DELTA_CHUNK.bit_length() - 1
MOE_ROWS = 512
U32 = jnp.uint32
LANES = 128
SUBLANES = 8
PACKED_W = D_MODEL // 2
ROUTE_TILE = 512
SC_CORES = 2
SC_SUBCORES = 16
SC_WINDOW = 128
N_PLANES = 2
SC_ROW_W = PACKED_W // N_PLANES
VMEM_LIMIT = 56 * 1024 * 1024


def _params(*sem):
    return pltpu.CompilerParams(dimension_semantics=sem, vmem_limit_bytes=VMEM_LIMIT)


def _dot(a, b):
    return jnp.dot(a, b, preferred_element_type=F32)


def _dot_nt(a, b):
    return lax.dot_general(a, b, (((1,), (1,)), ((), ())), preferred_element_type=F32)


def _dot_tn(a, b):
    return lax.dot_general(a, b, (((0,), (0,)), ((), ())), preferred_element_type=F32)


def _split2(a):
    hi = a.astype(BF16)
    lo = (a - hi.astype(F32)).astype(BF16)
    return hi, lo


def _split3(a):
    hi = a.astype(BF16)
    r = a - hi.astype(F32)
    mid = r.astype(BF16)
    lo = (r - mid.astype(F32)).astype(BF16)
    return hi, mid, lo


def _pack_rows(x):
    xb = x.astype(BF16).astype(F32)
    hi = lax.bitcast_convert_type(xb[:, :PACKED_W], U32)
    lo = lax.bitcast_convert_type(xb[:, PACKED_W:], U32)
    return hi | (lo >> 16)


def _store_planes(ref, packed, rows=slice(None)):
    for p in range(N_PLANES):
        ref[p, rows] = packed[:, p * SC_ROW_W:(p + 1) * SC_ROW_W]


def _load_planes(ref, rows=slice(None)):
    return jnp.concatenate([ref[p, rows] for p in range(N_PLANES)], axis=1)


def _sub_tiles(n_rows):
    return [slice(r, r + SUB_ROWS) for r in range(0, n_rows, SUB_ROWS)]


def _unpack_rows(w):
    hi = lax.bitcast_convert_type(w & jnp.uint32(0xFFFF0000), F32)
    lo = lax.bitcast_convert_type(w << 16, F32)
    return jnp.concatenate([hi, lo], axis=1)


def _rms(x32, g):
    return x32 * lax.rsqrt(jnp.mean(x32 * x32, -1, keepdims=True) + NORM_EPS) * g


def _gelu_tanh(x):
    c = math.sqrt(2.0 / math.pi)
    return 0.5 * x * (1.0 + jnp.tanh(c * (x + 0.044715 * (x * x * x))))


def _sigmoid(x):
    return 1.0 / (1.0 + jnp.exp(-x))


def _softplus(x):
    return jnp.maximum(x, 0.0) + jnp.log(1.0 + jnp.exp(-jnp.abs(x)))


def _ada_kernel(c_ref, w_ref, b_ref, o_ref):
    c = c_ref[...]
    s = c * _sigmoid(c)
    s_hi, s_lo = _split2(s)
    w_hi, w_lo = _split2(w_ref[...])
    o_ref[...] = _dot(s_hi, w_hi) + _dot(s_lo, w_hi) + _dot(s_hi, w_lo) + b_ref[...]


def _ada(cs, w_ada, b_ada):
    n = w_ada.shape[1]
    bn = D_MODEL
    return pl.pallas_call(
        _ada_kernel,
        out_shape=jax.ShapeDtypeStruct((cs.shape[0], n), F32),
        grid=(n // bn,),
        in_specs=[pl.BlockSpec(cs.shape, lambda j: (0, 0)),
                  pl.BlockSpec((D_MODEL, bn), lambda j: (0, j)),
                  pl.BlockSpec((1, bn), lambda j: (0, j))],
        out_specs=pl.BlockSpec((cs.shape[0], bn), lambda j: (0, j)),
        compiler_params=_params("parallel"),
        name="ada_mod",
    )(cs, w_ada, b_ada)


def _inproj_kernel(x_ref, mod_ref, g_ref, wqkv_ref, wzuv_ref, wba_ref,
                   qkv_ref, z_ref, u_ref, v_ref, ba_ref, bat_ref):
    mod = mod_ref[0]
    subs = _sub_tiles(x_ref.shape[0])
    hs = [_split2(_rms(x_ref[sl], g_ref[...]) * (1.0 + mod[1:2]) + mod[0:1]) for sl in subs]
    qkvs = [_dot(h_hi, wqkv_ref[...]) for h_hi, _ in hs]
    zuvs = [_dot(h_hi, wzuv_ref[...]) for h_hi, _ in hs]
    w_hi, w_lo = _split2(wba_ref[...])
    w_hi_lo = jnp.concatenate([w_hi, w_lo], axis=1)
    bas = []
    for h_hi, h_lo in hs:
        both = _dot(h_hi, w_hi_lo)
        bas.append(both[:, :LANES] + both[:, LANES:] + _dot(h_lo, w_hi))
    for sl, qkv, zuv, ba in zip(subs, qkvs, zuvs, bas):
        qkv_ref[sl] = qkv.astype(BF16)
        z_ref[sl] = zuv[:, :GDN_W].astype(BF16)
        u_ref[sl] = _gelu_tanh(zuv[:, GDN_W:GDN_W + SGU_W]).astype(BF16)
        v_ref[sl] = _gelu_tanh(zuv[:, GDN_W + SGU_W:]).astype(BF16)
        ba_ref[sl] = ba[:, :N_GATE_COLS]
        bat_ref[:, sl] = ba.T[:N_GATE_COLS]


def _inproj(x2d, mod, rows_per_mod, ng0, wqkv, wzuv, wba):
    t = x2d.shape[0]
    tm = min(ROW_TILE, t)
    tiles_per_mod = rows_per_mod // tm
    row = lambda i: (i, 0)
    const = lambda i: (0, 0)
    return pl.pallas_call(
        _inproj_kernel,
        out_shape=(jax.ShapeDtypeStruct((t, QKV_COLS), BF16),
                   jax.ShapeDtypeStruct((t, GDN_W), BF16),
                   jax.ShapeDtypeStruct((t, SGU_W), BF16),
                   jax.ShapeDtypeStruct((t, SGU_W), BF16),
                   jax.ShapeDtypeStruct((t, N_GATE_COLS), F32),
                   jax.ShapeDtypeStruct((N_GATE_COLS, t), F32)),
        grid=(t // tm,),
        in_specs=[pl.BlockSpec((tm, D_MODEL), row),
                  pl.BlockSpec((1, 6, D_MODEL), lambda i: (i // tiles_per_mod, 0, 0)),
                  pl.BlockSpec((1, D_MODEL), const),
                  pl.BlockSpec(wqkv.shape, const),
                  pl.BlockSpec(wzuv.shape, const),
                  pl.BlockSpec(wba.shape, const)],
        out_specs=(pl.BlockSpec((tm, QKV_COLS), row),
                   pl.BlockSpec((tm, GDN_W), row),
                   pl.BlockSpec((tm, SGU_W), row),
                   pl.BlockSpec((tm, SGU_W), row),
                   pl.BlockSpec((tm, N_GATE_COLS), row),
                   pl.BlockSpec((N_GATE_COLS, tm), lambda i: (0, i))),
        compiler_params=_params("parallel"),
        name="in_proj",
    )(x2d, mod, ng0, wqkv, wzuv, wba)


def _gdn_prep_kernel(row_len, qkv_ref, cw_ref, ba_ref, bat_ref, alog_r_ref, dtb_r_ref,
                     alog_c_ref, dtb_c_ref, opnd_ref, at_ref, gl_ref,
                     tri_scr, pair_scr, spread_scr):
    n = GDN_BLOCK
    c = DELTA_CHUNK

    def put_operand(which, j, value):
        d, h = divmod(j, GDN_HEADS)
        col = (d * N_OPNDS + which) * GDN_W + h * HEAD_DIM
        opnd_ref[0, :, col:col + HEAD_DIM] = value.astype(BF16)

    def mask01(m):
        return jnp.where(m, 1.0, 0.0).astype(BF16)

    wi = lax.broadcasted_iota(jnp.int32, (c, n), 0)
    wl = lax.broadcasted_iota(jnp.int32, (c, n), 1)
    wchunk = wl // c
    wj = wl % c
    lower_w = wi >= wj
    upper_w = wi <= wj
    diag_w = wi == wj
    eye_w = jnp.where(diag_w, 1.0, 0.0)

    @pl.when(jnp.logical_and(pl.program_id(0) == 0, pl.program_id(1) == 0))
    def _():
        ri = lax.broadcasted_iota(jnp.int32, (n, n), 0)
        ci = lax.broadcasted_iota(jnp.int32, (n, n), 1)
        same = (ri // c) == (ci // c)
        tri_scr[0] = mask01(same & (ri >= ci))
        tri_scr[1] = mask01(same & (ri <= ci))
        tri_scr[2] = mask01(same)
        for m in range(N_PAIR_LEVELS):
            s = 1 << m
            pair_scr[m] = mask01(((wi // (2 * s)) == (wj // (2 * s))) & ((wi // s) != (wj // s)))
        ei = lax.broadcasted_iota(jnp.int32, (n, CHUNKS_PER_BLOCK * HEAD_DIM), 0) // c
        ej = lax.broadcasted_iota(jnp.int32, (n, CHUNKS_PER_BLOCK * HEAD_DIM), 1) // HEAD_DIM
        spread_scr[...] = mask01(ei == ej)

    lower_b = tri_scr[0]
    upper_b = tri_scr[1]
    same_b = tri_scr[2]
    pair_masks = [pair_scr[m] for m in range(N_PAIR_LEVELS)]

    def to_wide(full):
        out = full[:c]
        for k in range(1, CHUNKS_PER_BLOCK):
            out = jnp.where(wchunk == k, full[k * c:(k + 1) * c], out)
        return out

    def col_wide(col):
        out = jnp.broadcast_to(col[:c], (c, n))
        for k in range(1, CHUNKS_PER_BLOCK):
            out = jnp.where(wchunk == k, jnp.broadcast_to(col[k * c:(k + 1) * c], (c, n)), out)
        return out

    def block_diag(x_w):
        return jnp.concatenate([x_w] * CHUNKS_PER_BLOCK, axis=0) * same_b

    ba = ba_ref[0]
    bat = bat_ref[...]
    beta_c = _sigmoid(ba[:, :N_CHAINS])
    g_c = -jnp.exp(alog_r_ref[...]) * _softplus(ba[:, N_CHAINS:] + dtb_r_ref[...])
    g_r = -jnp.exp(alog_c_ref[...]) * _softplus(bat[N_CHAINS:] + dtb_c_ref[...])
    gc3 = _split3(g_c)
    gr3 = jnp.concatenate(_split3(g_r), axis=0)

    def sum3_r(m):
        return m[:N_CHAINS] + m[N_CHAINS:2 * N_CHAINS] + m[2 * N_CHAINS:]

    cum_f_c = _dot(lower_b, gc3[0]) + _dot(lower_b, gc3[1]) + _dot(lower_b, gc3[2])
    tot_c = _dot(same_b, gc3[0]) + _dot(same_b, gc3[1]) + _dot(same_b, gc3[2])
    cum_b_c = tot_c - cum_f_c + g_c
    cum_f_r = sum3_r(_dot(gr3, upper_b))
    cum_b_r = sum3_r(_dot(gr3, lower_b))
    g_last = jnp.exp(sum3_r(_dot(gr3, spread_scr[...])))
    gl_ref[0, 0, 0] = g_last[:GDN_HEADS]
    gl_ref[0, 0, 1] = g_last[GDN_HEADS:]

    pos = lax.broadcasted_iota(jnp.int32, (n, HEAD_DIM), 0) % row_len
    first = pos == 0
    last = pos == row_len - 1

    def conv_silu(col):
        x = qkv_ref[0, :, col * HEAD_DIM:(col + 1) * HEAD_DIM].astype(F32)
        cw = cw_ref[:, col * HEAD_DIM:(col + 1) * HEAD_DIM]
        xp = jnp.where(first, 0.0, pltpu.roll(x, 1, 0))
        xn = jnp.where(last, 0.0, pltpu.roll(x, n - 1, 0))
        y = xp * cw[0:1] + x * cw[1:2] + xn * cw[2:3]
        return y * _sigmoid(y)

    def l2n(x):
        return x * lax.rsqrt(jnp.sum(x * x, -1, keepdims=True) + NORM_EPS)

    a_bs, ps, rhss = [None] * N_CHAINS, [None] * N_CHAINS, [None] * N_CHAINS
    first_pairs = pair_masks[0].astype(F32)
    for h in range(GDN_HEADS):
        q = l2n(conv_silu(h)) * (HEAD_DIM ** -0.5)
        k = l2n(conv_silu(GDN_HEADS + h))
        v = conv_silu(2 * GDN_HEADS + h)
        k_b = k.astype(BF16)
        qk_kk = _dot_nt(jnp.concatenate([q.astype(BF16), k_b], axis=0), k_b)
        qk_w = to_wide(qk_kk[:n])
        kk_w = to_wide(qk_kk[n:])
        for d in range(2):
            j = d * GDN_HEADS + h
            mask_w = lower_w if d == 0 else upper_w
            cum_c = (cum_f_c if d == 0 else cum_b_c)[:, j:j + 1]
            cum_r = (cum_f_r if d == 0 else cum_b_r)[j:j + 1, :]
            b_c = beta_c[:, j:j + 1]
            decay_w = jnp.where(mask_w, jnp.exp(jnp.where(mask_w, col_wide(cum_c) - cum_r, 0.0)), 0.0)
            amat_w = jnp.where(diag_w, 0.0, kk_w * decay_w * col_wide(b_c))
            a_bs[j] = amat_w.astype(BF16)
            ps[j] = eye_w - amat_w * first_pairs
            e_c = jnp.exp(cum_c)
            rhss[j] = jnp.concatenate([(v * b_c).astype(BF16), (k * (b_c * e_c)).astype(BF16)], axis=1)
            put_operand(OPND_QD, j, q * e_c)
            put_operand(OPND_KD, j, k * jnp.exp(tot_c[:, j:j + 1] - cum_c))
            at_ref[0, 0, j * c:(j + 1) * c, :] = (qk_w * decay_w).astype(BF16)

    for pm in pair_masks[1:]:
        p_bs = [p.astype(BF16) for p in ps]
        ys = [_dot(a_bs[j] * pm, block_diag(p_bs[j])) for j in range(N_CHAINS)]
        ps = [ps[j] - _dot(p_bs[j], block_diag(ys[j].astype(BF16))) for j in range(N_CHAINS)]

    for j in range(N_CHAINS):
        uw = _dot(block_diag(ps[j].astype(BF16)), rhss[j])
        put_operand(OPND_U, j, uw[:, :HEAD_DIM])
        put_operand(OPND_W, j, uw[:, HEAD_DIM:])


def _gdn_prep(qkv, conv_w, ba, bat, alog, dtb, row_len):
    b, l, _ = qkv.shape
    nblk = l // GDN_BLOCK
    wide = N_OPNDS * N_CHAINS * HEAD_DIM
    blk = lambda bi, i: (bi, i, 0)
    const = lambda bi, i: (0, 0)
    alog_r, dtb_r = alog.reshape(1, N_CHAINS), dtb.reshape(1, N_CHAINS)
    alog_c, dtb_c = alog.reshape(N_CHAINS, 1), dtb.reshape(N_CHAINS, 1)
    return pl.pallas_call(
        functools.partial(_gdn_prep_kernel, row_len),
        out_shape=(
            jax.ShapeDtypeStruct((b, l, wide), BF16),
            jax.ShapeDtypeStruct((b, nblk, N_CHAINS * DELTA_CHUNK, GDN_BLOCK), BF16),
            jax.ShapeDtypeStruct((b, nblk, 2, GDN_HEADS, CHUNKS_PER_BLOCK * HEAD_DIM), F32)),
        grid=(b, nblk),
        in_specs=[pl.BlockSpec((1, GDN_BLOCK, QKV_COLS), blk),
                  pl.BlockSpec((3, QKV_COLS), const),
                  pl.BlockSpec((1, GDN_BLOCK, N_GATE_COLS), blk),
                  pl.BlockSpec((N_GATE_COLS, GDN_BLOCK), lambda bi, i: (0, bi * nblk + i)),
                  pl.BlockSpec((1, N_CHAINS), const),
                  pl.BlockSpec((1, N_CHAINS), const),
                  pl.BlockSpec((N_CHAINS, 1), const),
                  pl.BlockSpec((N_CHAINS, 1), const)],
        out_specs=(
            pl.BlockSpec((1, GDN_BLOCK, wide), blk),
            pl.BlockSpec((1, 1, N_CHAINS * DELTA_CHUNK, GDN_BLOCK), lambda bi, i: (bi, i, 0, 0)),
            pl.BlockSpec((1, 1, 2, GDN_HEADS, CHUNKS_PER_BLOCK * HEAD_DIM),
                         lambda bi, i: (bi, i, 0, 0, 0))),
        scratch_shapes=[pltpu.VMEM((3, GDN_BLOCK, GDN_BLOCK), BF16),
                        pltpu.VMEM((N_PAIR_LEVELS, DELTA_CHUNK, GDN_BLOCK), BF16),
                        pltpu.VMEM((GDN_BLOCK, CHUNKS_PER_BLOCK * HEAD_DIM), BF16)],
        compiler_params=_params("arbitrary", "arbitrary"),
        name="gdn_prep",
    )(qkv, conv_w, ba, bat, alog_r, dtb_r, alog_c, dtb_c)


def _gdn_scan_kernel(xf, af, gf, xb, ab, gb, s0_ref, of_ref, ob_ref, sfin_ref, s_scr):
    i = pl.program_id(0)
    c = DELTA_CHUNK
    n_batch = s0_ref.shape[0]

    @pl.when(i == 0)
    def _():
        s_scr[...] = s0_ref[...]

    ops = ((xf, af, gf, of_ref), (xb, ab, gb, ob_ref))

    def operand(d, which, bi, rows, h):
        col = which * GDN_W + h * HEAD_DIM
        return ops[d][0][bi, rows, col:col + HEAD_DIM]
    chains = [(bi, d, h) for bi in range(n_batch) for d in range(2) for h in range(GDN_HEADS)]
    states = [s_scr[bi, d * GDN_HEADS + h] for bi, d, h in chains]
    for step in range(CHUNKS_PER_BLOCK):
        def chunk(d):
            cc = step if d == 0 else CHUNKS_PER_BLOCK - 1 - step
            return cc, slice(cc * c, (cc + 1) * c)

        xs = []
        for j, (bi, d, h) in enumerate(chains):
            _, rows = chunk(d)
            cols = slice(h * HEAD_DIM, (h + 1) * HEAD_DIM)
            wq = jnp.concatenate([operand(d, OPND_W, bi, rows, h), operand(d, OPND_QD, bi, rows, h)],
                                 axis=0)
            xs.append(_dot(wq, states[j].astype(BF16)))
        v_news = []
        for j, (bi, d, h) in enumerate(chains):
            _, rows = chunk(d)
            cols = slice(h * HEAD_DIM, (h + 1) * HEAD_DIM)
            v_news.append((operand(d, OPND_U, bi, rows, h).astype(F32) - xs[j][:c]).astype(BF16))
        for j, (bi, d, h) in enumerate(chains):
            cc, rows = chunk(d)
            cols = slice(h * HEAD_DIM, (h + 1) * HEAD_DIM)
            a_c = ops[d][1][bi, 0, h * c:(h + 1) * c, cc * c:(cc + 1) * c]
            ops[d][3][bi, rows, cols] = (xs[j][c:] + _dot(a_c, v_news[j])).astype(BF16)
            ds = _dot_tn(operand(d, OPND_KD, bi, rows, h), v_news[j])
            g_last = ops[d][2][bi, 0, 0, h:h + 1, cc * HEAD_DIM:(cc + 1) * HEAD_DIM]
            states[j] = states[j] * g_last + ds
    for j, (bi, d, h) in enumerate(chains):
        s_scr[bi, d * GDN_HEADS + h] = states[j]

    @pl.when(i == pl.num_programs(0) - 1)
    def _():
        sfin_ref[...] = s_scr[...]


def _gdn_scan(opnd, at, gl, s0):
    b, l, _ = opnd.shape
    nblk = l // GDN_BLOCK
    half = GDN_HEADS * HEAD_DIM
    fwd = lambda i: (0, i, 0)
    bwd = lambda i: (0, nblk - 1 - i, 1)
    big = lambda m: pl.BlockSpec((b, GDN_BLOCK, N_OPNDS * half), m)
    att_shape = (b, 1, GDN_HEADS * DELTA_CHUNK, GDN_BLOCK)
    attf = pl.BlockSpec(att_shape, lambda i: (0, i, 0, 0))
    attb = pl.BlockSpec(att_shape, lambda i: (0, nblk - 1 - i, 1, 0))
    gl_shape = (b, 1, 1, GDN_HEADS, CHUNKS_PER_BLOCK * HEAD_DIM)
    glf = pl.BlockSpec(gl_shape, lambda i: (0, i, 0, 0, 0))
    glb = pl.BlockSpec(gl_shape, lambda i: (0, nblk - 1 - i, 1, 0, 0))
    state = pl.BlockSpec((b, N_CHAINS, HEAD_DIM, HEAD_DIM), lambda i: (0, 0, 0, 0))
    return pl.pallas_call(
        _gdn_scan_kernel,
        out_shape=(jax.ShapeDtypeStruct((b, l, half), BF16),
                   jax.ShapeDtypeStruct((b, l, half), BF16),
                   jax.ShapeDtypeStruct((b, N_CHAINS, HEAD_DIM, HEAD_DIM), F32)),
        grid=(nblk,),
        in_specs=[big(fwd), attf, glf, big(bwd), attb, glb, state],
        out_specs=(pl.BlockSpec((b, GDN_BLOCK, half), fwd),
                   pl.BlockSpec((b, GDN_BLOCK, half), lambda i: (0, nblk - 1 - i, 0)),
                   state),
        scratch_shapes=[pltpu.VMEM((b, N_CHAINS, HEAD_DIM, HEAD_DIM), F32)],
        compiler_params=_params("arbitrary"),
        name="gdn_scan",
    )(opnd, at, gl, opnd, at, gl, s0)


def _mix_into(mix_ref, of_ref, ob_ref, z_ref, u_ref, v_ref, gng_ref, lng_ref, lnb_ref, ws_ref,
              bs_ref):
    tm = of_ref.shape[0]
    o = of_ref[...].astype(F32) + ob_ref[...].astype(F32)
    z = z_ref[...].astype(F32)
    for h in range(GDN_HEADS):
        cols = slice(h * HEAD_DIM, (h + 1) * HEAD_DIM)
        oh = o[:, cols]
        zh = z[:, cols]
        r = lax.rsqrt(jnp.mean(oh * oh, -1, keepdims=True) + NORM_EPS)
        mix_ref[:, cols] = (oh * r * gng_ref[...] * (zh * _sigmoid(zh))).astype(BF16)
    for g in range(SGU_GROUPS):
        cols = slice(g * SGU_GROUP, (g + 1) * SGU_GROUP)
        vg = v_ref[:, cols].astype(F32)
        vc = vg - jnp.mean(vg, -1, keepdims=True)
        vn = vc * lax.rsqrt(jnp.mean(vc * vc, -1, keepdims=True) + NORM_EPS)
        vn = (vn * lng_ref[g:g + 1] + lnb_ref[g:g + 1]).astype(BF16)
        wsg = ws_ref[g].astype(BF16)
        for n in range(tm // SGU_CHUNK):
            rows = slice(n * SGU_CHUNK, (n + 1) * SGU_CHUNK)
            s = _dot(wsg, vn[rows]) + bs_ref[g]
            mix_ref[rows, GDN_W + g * SGU_GROUP:GDN_W + (g + 1) * SGU_GROUP] = (
                u_ref[rows, cols].astype(F32) * s).astype(BF16)


def _outproj_kernel(of_ref, ob_ref, z_ref, u_ref, v_ref, gng_ref, lng_ref, lnb_ref, ws_ref, bs_ref,
                    x_ref, mod_ref, ng_ref, wout_ref, rwt_ref, rb_ref,
                    x1_ref, h2_ref, lt_ref, mix_scr):
    subs = _sub_tiles(x_ref.shape[0])
    for sl in subs:
        _mix_into(mix_scr.at[sl], of_ref.at[sl], ob_ref.at[sl], z_ref.at[sl], u_ref.at[sl],
                  v_ref.at[sl], gng_ref, lng_ref, lnb_ref, ws_ref, bs_ref)
    mod = mod_ref[0]
    ys = [_dot(mix_scr[sl], wout_ref[...]) for sl in subs]
    w_hi, w_lo = _split2(rwt_ref[...])
    for sl, y in zip(subs, ys):
        x1 = x_ref[sl] + mod[2:3] * _rms(y, ng_ref[1:2])
        x1_ref[sl] = x1
        h2 = _rms(x1, ng_ref[2:3]) * (1.0 + mod[4:5]) + mod[3:4]
        h_hi, h_lo = _split2(h2)
        _store_planes(h2_ref, _pack_rows(h_hi), sl)
        lt_ref[:, sl] = (_dot_nt(w_hi, h_hi) + _dot_nt(w_lo, h_hi) + _dot_nt(w_hi, h_lo)
                         + rb_ref[...])


def _outproj(o_f, o_b, z, ug, vg, gdn_norm_g, ln_g, ln_b, w_s, b_s_full,
             x2d, mod, rows_per_mod, ng, wout, rwt, rb_col):
    t = x2d.shape[0]
    tm = min(ROW_TILE, t)
    tiles_per_mod = rows_per_mod // tm
    row = lambda i: (i, 0)
    const = lambda i: (0, 0)
    c3 = lambda i: (0, 0, 0)
    return pl.pallas_call(
        _outproj_kernel,
        out_shape=(jax.ShapeDtypeStruct((t, D_MODEL), F32),
                   jax.ShapeDtypeStruct((N_PLANES, t, SC_ROW_W), U32),
                   jax.ShapeDtypeStruct((N_EXPERTS, t), F32)),
        grid=(t // tm,),
        in_specs=[pl.BlockSpec((tm, GDN_W), row), pl.BlockSpec((tm, GDN_W), row),
                  pl.BlockSpec((tm, GDN_W), row), pl.BlockSpec((tm, SGU_W), row),
                  pl.BlockSpec((tm, SGU_W), row),
                  pl.BlockSpec((1, HEAD_DIM), const),
                  pl.BlockSpec((SGU_GROUPS, SGU_GROUP), const),
                  pl.BlockSpec((SGU_GROUPS, SGU_GROUP), const),
                  pl.BlockSpec((SGU_GROUPS, SGU_CHUNK, SGU_CHUNK), c3),
                  pl.BlockSpec((SGU_GROUPS, SGU_CHUNK, SGU_GROUP), c3),
                  pl.BlockSpec((tm, D_MODEL), row),
                  pl.BlockSpec((1, 6, D_MODEL), lambda i: (i // tiles_per_mod, 0, 0)),
                  pl.BlockSpec((4, D_MODEL), const),
                  pl.BlockSpec((D_MODEL, D_MODEL), const),
                  pl.BlockSpec((N_EXPERTS, D_MODEL), const),
                  pl.BlockSpec((N_EXPERTS, 1), const)],
        out_specs=(pl.BlockSpec((tm, D_MODEL), row),
                   pl.BlockSpec((N_PLANES, tm, SC_ROW_W), lambda i: (0, i, 0)),
                   pl.BlockSpec((N_EXPERTS, tm), lambda i: (0, i))),
        compiler_params=_params("parallel"),
        scratch_shapes=[pltpu.VMEM((tm, D_MODEL), BF16)],
        name="out_proj_router",
    )(o_f, o_b, z, ug, vg, gdn_norm_g, ln_g, ln_b, w_s, b_s_full, x2d, mod, ng, wout, rwt, rb_col)


def _moe_kernel(be_ref, slot_ref, next_ref, nb_ref, xb_ref, wgu_hbm, bgu_ref, wd_hbm, bd_ref, y_ref,
                wgu_f, wd_f, wgu_b, wd_b, gut_scr, sems):
    i = pl.program_id(0)
    live = i < nb_ref[0]
    new_expert = jnp.logical_or(i == 0, be_ref[i] != be_ref[jnp.maximum(i - 1, 0)])

    def weight_copies(expert, slot):
        return (pltpu.make_async_copy(wgu_hbm.at[expert], wgu_f.at[slot], sems.at[slot, 0]),
                pltpu.make_async_copy(wd_hbm.at[expert], wd_f.at[slot], sems.at[slot, 1]))

    @pl.when(i == 0)
    def _():
        for copy in weight_copies(be_ref[0], 0):
            copy.start()

    @pl.when(jnp.logical_and(live, new_expert))
    def _():
        slot = slot_ref[i]
        for copy in weight_copies(be_ref[i], slot):
            copy.wait()

        @pl.when(next_ref[i] >= 0)
        def _():
            for copy in weight_copies(next_ref[i], 1 - slot):
                copy.start()
        wgu_b[...] = wgu_f[slot].astype(BF16)
        wd_b[...] = wd_f[slot].astype(BF16)

    @pl.when(live)
    def _():
        xb = _unpack_rows(_load_planes(xb_ref)).astype(BF16)
        gu = _dot(xb, wgu_b[...]) + bgu_ref[0]
        gu_t = gu.T
        acts = []
        for part in range(MOE_ROWS // LANES):
            part_ref = gut_scr.at[part]
            part_ref[...] = gu_t[:, part * LANES:(part + 1) * LANES]
            gate = jnp.minimum(part_ref[pl.ds(0, D_FF, stride=2), :], SWIGLU_LIMIT)
            up = jnp.clip(part_ref[pl.ds(1, D_FF, stride=2), :], -SWIGLU_LIMIT, SWIGLU_LIMIT)
            acts.append(((up + 1.0) * gate * _sigmoid(SWIGLU_ALPHA * gate)).astype(BF16))
        act_t = jnp.concatenate(acts, axis=1)
        _store_planes(y_ref, _pack_rows(_dot_tn(act_t, wd_b[...]) + bd_ref[0]))

    @pl.when(jnp.logical_not(live))
    def _():
        y_ref[...] = jnp.zeros_like(y_ref)


def _moe_ffn(block_e, n_used, xb, w_gu, b_gu, w_down, b_down):
    n_rows = xb.shape[1]
    n_blocks = n_rows // MOE_ROWS
    idx = jnp.arange(n_blocks, dtype=jnp.int32)
    live = idx < n_used[0]
    changed = jnp.concatenate([jnp.ones((1,), bool), block_e[1:] != block_e[:-1]]) & live
    slot = ((jnp.cumsum(changed.astype(jnp.int32)) - 1) % 2).astype(jnp.int32)
    change_at = jnp.where(changed, idx, n_blocks)
    next_change = lax.cummin(jnp.concatenate([change_at[1:], jnp.full((1,), n_blocks, jnp.int32)]),
                             reverse=True)
    next_e = jnp.where(next_change < n_blocks,
                       block_e[jnp.minimum(next_change, n_blocks - 1)], -1).astype(jnp.int32)

    row = lambda i, be, sl, nx, nb: (0, i, 0)
    ex3 = lambda i, be, sl, nx, nb: (be[i], 0, 0)
    live_row = lambda i, be, sl, nx, nb: (0, jnp.minimum(i, nb[0] - 1), 0)
    planes = (N_PLANES, MOE_ROWS, SC_ROW_W)
    grid_spec = pltpu.PrefetchScalarGridSpec(
        num_scalar_prefetch=4,
        grid=(n_blocks,),
        in_specs=[pl.BlockSpec(planes, live_row),
                  pl.BlockSpec(memory_space=pl.ANY),
                  pl.BlockSpec((1, 1, 2 * D_FF), ex3),
                  pl.BlockSpec(memory_space=pl.ANY),
                  pl.BlockSpec((1, 1, D_MODEL), ex3)],
        out_specs=pl.BlockSpec(planes, row),
        scratch_shapes=[pltpu.VMEM((2, D_MODEL, 2 * D_FF), F32),
                        pltpu.VMEM((2, D_FF, D_MODEL), F32),
                        pltpu.VMEM((D_MODEL, 2 * D_FF), BF16),
                        pltpu.VMEM((D_FF, D_MODEL), BF16),
                        pltpu.VMEM((MOE_ROWS // LANES, 2 * D_FF, LANES), F32),
                        pltpu.SemaphoreType.DMA((2, 2))],
    )
    return pl.pallas_call(
        _moe_kernel,
        out_shape=jax.ShapeDtypeStruct((N_PLANES, n_rows, SC_ROW_W), U32),
        grid_spec=grid_spec,
        compiler_params=_params("arbitrary"),
        name="moe_ffn",
    )(block_e, slot, next_e, n_used, xb, w_gu, b_gu, w_down, b_down)


def _combine_kernel(y0_ref, y1_ref, y2_ref, y3_ref, gt_ref, x1_ref, mod_ref, ng_ref, o_ref):
    mod = mod_ref[0]
    gt = gt_ref[...]
    y = _unpack_rows(_load_planes(y0_ref)) * gt[:, 0:1]
    for k, y_ref in ((1, y1_ref), (2, y2_ref), (3, y3_ref)):
        y = y + _unpack_rows(_load_planes(y_ref)) * gt[:, k:k + 1]
    o_ref[...] = x1_ref[...] + mod[5:6] * _rms(y, ng_ref[3:4])


def _combine(yg, gates, x1, mod, rows_per_mod, ng):
    t = x1.shape[0]
    tm = min(COMBINE_TILE, t)
    tiles_per_mod = rows_per_mod // tm
    n_tiles = t // tm
    row = lambda i: (i, 0)
    choice = lambda k: pl.BlockSpec((N_PLANES, tm, SC_ROW_W), lambda i: (0, k * n_tiles + i, 0))
    return pl.pallas_call(
        _combine_kernel,
        out_shape=jax.ShapeDtypeStruct((t, D_MODEL), F32),
        grid=(n_tiles,),
        in_specs=[choice(0), choice(1), choice(2), choice(3),
                  pl.BlockSpec((tm, TOP_K), row),
                  pl.BlockSpec((tm, D_MODEL), row),
                  pl.BlockSpec((1, 6, D_MODEL), lambda i: (i // tiles_per_mod, 0, 0)),
                  pl.BlockSpec((4, D_MODEL), lambda i: (0, 0))],
        out_specs=pl.BlockSpec((tm, D_MODEL), row),
        compiler_params=_params("parallel"),
        name="moe_combine",
    )(yg, yg, yg, yg, gates, x1, mod, ng)


def _route_kernel(lt_ref, eidx_ref, gate_ref, rank_ref, cnt_ref, carry, earlier):
    i = pl.program_id(0)
    tile = lt_ref.shape[1]

    @pl.when(i == 0)
    def _():
        carry[...] = jnp.zeros_like(carry)
        ti = lax.broadcasted_iota(jnp.int32, (tile, tile), 0)
        tj = lax.broadcasted_iota(jnp.int32, (tile, tile), 1)
        earlier[...] = jnp.where(ti < tj, 1.0, 0.0).astype(BF16)

    logits = lt_ref[...]
    eio = lax.broadcasted_iota(jnp.int32, (N_EXPERTS, tile), 0).astype(F32)
    vals, sels = [], []
    for k in range(TOP_K):
        m = jnp.max(logits, axis=0, keepdims=True)
        idx = jnp.min(jnp.where(logits == m, eio, float(N_EXPERTS)), axis=0, keepdims=True)
        sel = eio == idx
        logits = jnp.where(sel, -jnp.inf, logits)
        vals.append(m)
        sels.append(sel)
        eidx_ref[k:k + 1, :] = idx.astype(jnp.int32)
    exps = [jnp.exp(v - vals[0]) for v in vals]
    denom = exps[0] + exps[1] + exps[2] + exps[3]
    for k in range(TOP_K):
        gate_ref[k:k + 1, :] = exps[k] / denom

    member = jnp.where(sels[0] | sels[1] | sels[2] | sels[3], 1.0, 0.0)
    before = _dot(member.astype(BF16), earlier[...]) + carry[...]
    for k in range(TOP_K):
        rank_ref[k:k + 1, :] = jnp.sum(jnp.where(sels[k], before, 0.0), axis=0,
                                       keepdims=True).astype(jnp.int32)
    carry[...] = carry[...] + jnp.sum(member, axis=1, keepdims=True)
    cnt_ref[...] = carry[...].astype(jnp.int32)


def _route(logits_t):
    t = logits_t.shape[1]
    tile = min(ROUTE_TILE, t)
    blk = lambda i: (0, i)
    return pl.pallas_call(
        _route_kernel,
        out_shape=(jax.ShapeDtypeStruct((TOP_K, t), jnp.int32),
                   jax.ShapeDtypeStruct((TOP_K, t), F32),
                   jax.ShapeDtypeStruct((TOP_K, t), jnp.int32),
                   jax.ShapeDtypeStruct((N_EXPERTS, 1), jnp.int32)),
        grid=(t // tile,),
        in_specs=[pl.BlockSpec((N_EXPERTS, tile), blk)],
        out_specs=(pl.BlockSpec((TOP_K, tile), blk), pl.BlockSpec((TOP_K, tile), blk),
                   pl.BlockSpec((TOP_K, tile), blk),
                   pl.BlockSpec((N_EXPERTS, 1), lambda i: (0, 0))),
        scratch_shapes=[pltpu.VMEM((N_EXPERTS, 1), F32), pltpu.VMEM((tile, tile), BF16)],
        compiler_params=_params("arbitrary"),
        name="moe_route",
    )(logits_t)


def _slot_tables(eidx, rank, counts, n_blocks):
    padded = (counts + MOE_ROWS - 1) // MOE_ROWS * MOE_ROWS
    pad_end = jnp.cumsum(padded)
    pad_start = pad_end - padded
    experts = jnp.arange(N_EXPERTS, dtype=jnp.int32)
    dest = rank + jnp.sum(jnp.where(eidx[..., None] == experts, pad_start, 0), axis=-1)
    first_row = jnp.arange(n_blocks, dtype=jnp.int32)[:, None] * MOE_ROWS
    block_e = jnp.minimum(jnp.sum((pad_end[None, :] <= first_row).astype(jnp.int32), axis=1),
                          N_EXPERTS - 1)
    n_used = pad_end[-1:] // MOE_ROWS
    return dest.astype(jnp.int32), pad_start.astype(jnp.int32), block_e, n_used.astype(jnp.int32)


def _sc_mesh():
    return plsc.VectorSubcoreMesh(core_axis_name="c", subcore_axis_name="s",
                                  num_cores=SC_CORES, num_subcores=SC_SUBCORES)


def _plane_row_ids(rows, rows_per_plane):
    return jnp.concatenate([rows + p * rows_per_plane for p in range(N_PLANES)], axis=-1)


def _sc_gather_rows(table, rows):
    v = table.shape[1]
    idx = _plane_row_ids(rows, v)[None]
    n_all = idx.shape[1]

    @functools.partial(pl.kernel, mesh=_sc_mesh(), name="moe_gather_rows",
                       out_type=jax.ShapeDtypeStruct((n_all, SC_ROW_W), U32))
    def gather(x_hbm, i_hbm, o_hbm):
        def body(i_vmem, o_vmem):
            pltpu.sync_copy(x_hbm.at[i_vmem.at[0]], o_vmem)

        pltpu.emit_pipeline(
            body, grid=(n_all // SC_WINDOW,),
            in_specs=[pl.BlockSpec((1, SC_WINDOW), lambda i: (0, i))],
            out_specs=[pl.BlockSpec((SC_WINDOW, SC_ROW_W), lambda i: (i, 0))],
            core_axis_name=("c", "s"), dimension_semantics=(pltpu.PARALLEL,),
        )(i_hbm, o_hbm)

    return gather(table.reshape(N_PLANES * v, SC_ROW_W), idx).reshape(N_PLANES, -1, SC_ROW_W)


def _sc_scatter_rows(rows, dest, n_out):
    t = rows.shape[1]
    idx = _plane_row_ids(dest, n_out)

    @functools.partial(pl.kernel, mesh=_sc_mesh(), name="moe_scatter_rows", scratch_types=[],
                       out_type=jax.ShapeDtypeStruct((N_PLANES * n_out, SC_ROW_W), U32))
    def scatter(x_hbm, i_hbm, o_hbm):
        def body(x_vmem, i_vmem):
            for k in range(TOP_K):
                pltpu.sync_copy(x_vmem, o_hbm.at[i_vmem.at[k]])

        pltpu.emit_pipeline(
            body, grid=(N_PLANES * t // SC_WINDOW,),
            in_specs=[pl.BlockSpec((SC_WINDOW, SC_ROW_W), lambda i: (i, 0)),
                      pl.BlockSpec((TOP_K, SC_WINDOW), lambda i: (0, i))],
            out_specs=[],
            core_axis_name=("c", "s"), dimension_semantics=(pltpu.PARALLEL,),
        )(x_hbm, i_hbm)

    return scatter(rows.reshape(N_PLANES * t, SC_ROW_W), idx).reshape(N_PLANES, n_out, SC_ROW_W)


def _zero_pad_kernel(cnt_ref, start_ref, xb_in_ref, xb_ref, zero_scr, sem):
    del xb_in_ref
    zero_scr[...] = jnp.zeros_like(zero_scr)

    pieces = [SUBLANES << bit for bit in range((MOE_ROWS // SUBLANES - 1).bit_length())]

    def zero_copy(p, row, size):
        return pltpu.make_async_copy(zero_scr.at[pl.ds(0, size)], xb_ref.at[p, pl.ds(row, size)], sem)

    def for_each_piece(fn):
        def per_expert(e, carry):
            n_real = cnt_ref[e]
            n_pad = (MOE_ROWS - n_real % MOE_ROWS) % MOE_ROWS
            first = start_ref[e] + n_real
            n_single = n_pad % SUBLANES
            for j in range(SUBLANES - 1):
                @pl.when(j < n_single)
                def _():
                    for p in range(N_PLANES):
                        fn(zero_copy(p, first + j, 1))
            row = first + n_single
            for size in pieces:
                @pl.when((n_pad & size) != 0)
                def _():
                    for p in range(N_PLANES):
                        fn(zero_copy(p, pl.multiple_of(row, SUBLANES), size))
                row = row + (n_pad & size)
            return carry
        lax.fori_loop(0, N_EXPERTS, per_expert, 0)

    for_each_piece(lambda copy: copy.start())
    for_each_piece(lambda copy: copy.wait())


def _zero_pad_slots(counts, pad_start, xb):
    grid_spec = pltpu.PrefetchScalarGridSpec(
        num_scalar_prefetch=2,
        grid=(1,),
        in_specs=[pl.BlockSpec(memory_space=pl.ANY)],
        out_specs=pl.BlockSpec(memory_space=pl.ANY),
        scratch_shapes=[pltpu.VMEM((MOE_ROWS // 2, SC_ROW_W), U32), pltpu.SemaphoreType.DMA],
    )
    return pl.pallas_call(
        _zero_pad_kernel,
        out_shape=jax.ShapeDtypeStruct(xb.shape, xb.dtype),
        grid_spec=grid_spec,
        input_output_aliases={2: 0},
        compiler_params=_params("arbitrary"),
        name="moe_zero_pad",
    )(counts, pad_start, xb)


def kernel(x, c, ctx, c_ctx, w_ada, b_ada, norm_g, w_in, conv_w, a_log, dt_bias, gdn_norm_g,
           sgu_ln_g, sgu_ln_b, sgu_w, sgu_b, w_out, router_w, router_b, w_gu, b_gu, w_down, b_down):
    b, l, d = x.shape
    lc = ctx.shape[1]
    t = b * l
    assert d == D_MODEL and l % ROW_TILE == 0 and l % GDN_BLOCK == 0 and lc % GDN_BLOCK == 0
    assert w_ada.shape[0] == 1, "single-layer block"

    cs = jnp.concatenate([c, c_ctx[None], jnp.zeros((8 - b - 1, d), F32)], axis=0)
    mod_all = _ada(cs, w_ada[0], b_ada[0][None])
    mod = mod_all[:b].reshape(b, 6, d)
    mod_c = mod_all[b:b + 1].reshape(1, 6, d)
    ng = norm_g[0]

    w = w_in[0]
    wqkv = w[:, :QKV_COLS].astype(BF16)
    wba = jnp.pad(w[:, QKV_COLS:QKV_COLS + N_GATE_COLS], ((0, 0), (0, LANES - N_GATE_COLS)))
    wzuv = w[:, QKV_COLS + N_GATE_COLS:].astype(BF16)

    x2d = x.reshape(t, d)
    qkv, z, ug, vg, ba, bat = _inproj(x2d, mod, l, ng[0:1], wqkv, wzuv, wba)
    ctx2d = ctx.reshape(b * lc, d)
    qkv_c, _, _, _, ba_c, bat_c = _inproj(ctx2d, mod_c, b * lc, ng[0:1], wqkv, wzuv, wba)

    alog = a_log[0].reshape(-1)
    dtb = dt_bias[0].reshape(-1)
    pc = _gdn_prep(qkv_c.reshape(b, lc, QKV_COLS), conv_w[0], ba_c.reshape(b, lc, N_GATE_COLS),
                   bat_c, alog, dtb, lc)
    s_zero = jnp.zeros((b, N_CHAINS, HEAD_DIM, HEAD_DIM), F32)
    _, _, s_ctx = _gdn_scan(*pc, s_zero)
    pp = _gdn_prep(qkv.reshape(b, l, QKV_COLS), conv_w[0], ba.reshape(b, l, N_GATE_COLS),
                   bat, alog, dtb, GRID_W)
    o_f, o_b, _ = _gdn_scan(*pp, s_ctx)

    b_s_full = jnp.broadcast_to(sgu_b[0][:, :, None], (SGU_GROUPS, SGU_CHUNK, SGU_GROUP))
    x1, h2p, logits_t = _outproj(o_f.reshape(t, GDN_W), o_b.reshape(t, GDN_W), z, ug, vg,
                                 gdn_norm_g, sgu_ln_g[0], sgu_ln_b[0], sgu_w[0], b_s_full,
                                 x2d, mod, l, ng, w_out[0].astype(BF16),
                                 router_w[0].T, router_b[0][:, None])

    eidx, gates_t, rank, counts = _route(logits_t)
    counts = counts[:, 0]
    n_blocks = -(-(t * TOP_K) // MOE_ROWS) + N_EXPERTS
    dest, pad_start, block_e, n_used = _slot_tables(eidx, rank, counts, n_blocks)
    xb = _sc_scatter_rows(h2p, dest, n_blocks * MOE_ROWS)
    xb = _zero_pad_slots(counts, pad_start, xb)
    yb = _moe_ffn(block_e, n_used, xb, w_gu[0], b_gu[0][:, None, :], w_down[0],
                  b_down[0][:, None, :])
    yg = _sc_gather_rows(yb, dest.reshape(-1))
    out = _combine(yg, gates_t.T, x1, mod, l, ng)
    return out.reshape(b, l, d)
```

```python
import functools
import math

import jax
import jax.numpy as jnp
from jax import lax
from jax.experimental import pallas as pl
from jax.experimental.pallas import tpu as pltpu
from jax.experimental.pallas import tpu_sc as plsc

F32 = jnp.float32
BF16 = jnp.bfloat16

D_MODEL = 1024
GDN_HEADS = 4
HEAD_DIM = 128
GDN_W = GDN_HEADS * HEAD_DIM
SGU_GROUPS = 4
SGU_GROUP = 128
SGU_W = SGU_GROUPS * SGU_GROUP
SGU_CHUNK = 128
DELTA_CHUNK = 64
GRID_W = 64
N_EXPERTS = 32
TOP_K = 4
D_FF = 1024
SWIGLU_LIMIT = 7.0
SWIGLU_ALPHA = 1.702
NORM_EPS = 1e-6
QKV_COLS = 3 * GDN_W
N_CHAINS = 2 * GDN_HEADS
N_GATE_COLS = 2 * N_CHAINS

ROW_TILE = 1024
SUB_ROWS = 256
COMBINE_TILE = 512
GDN_BLOCK = 256
CHUNKS_PER_BLOCK = GDN_BLOCK // DELTA_CHUNK
N_PAIR_LEVELS = DELTA_CHUNK.bit_length() - 1
MOE_ROWS = 512
U32 = jnp.uint32
LANES = 128
SUBLANES = 8
PACKED_W = D_MODEL // 2
ROUTE_TILE = 512
SC_CORES = 2
SC_SUBCORES = 16
SC_WINDOW = 128
N_PLANES = 2
SC_ROW_W = PACKED_W // N_PLANES
VMEM_LIMIT = 56 * 1024 * 1024


def _params(*sem):
    return pltpu.CompilerParams(dimension_semantics=sem, vmem_limit_bytes=VMEM_LIMIT)


def _dot(a, b):
    return jnp.dot(a, b, preferred_element_type=F32)


def _dot_nt(a, b):
    return lax.dot_general(a, b, (((1,), (1,)), ((), ())), preferred_element_type=F32)


def _dot_tn(a, b):
    return lax.dot_general(a, b, (((0,), (0,)), ((), ())), preferred_element_type=F32)


def _split2(a):
    hi = a.astype(BF16)
    lo = (a - hi.astype(F32)).astype(BF16)
    return hi, lo


def _split3(a):
    hi = a.astype(BF16)
    r = a - hi.astype(F32)
    mid = r.astype(BF16)
    lo = (r - mid.astype(F32)).astype(BF16)
    return hi, mid, lo


def _pack_rows(x):
    xb = x.astype(BF16).astype(F32)
    hi = lax.bitcast_convert_type(xb[:, :PACKED_W], U32)
    lo = lax.bitcast_convert_type(xb[:, PACKED_W:], U32)
    return hi | (lo >> 16)


def _store_planes(ref, packed, rows=slice(None)):
    for p in range(N_PLANES):
        ref[p, rows] = packed[:, p * SC_ROW_W:(p + 1) * SC_ROW_W]


def _load_planes(ref, rows=slice(None)):
    return jnp.concatenate([ref[p, rows] for p in range(N_PLANES)], axis=1)


def _sub_tiles(n_rows):
    return [slice(r, r + SUB_ROWS) for r in range(0, n_rows, SUB_ROWS)]


def _unpack_rows(w):
    hi = lax.bitcast_convert_type(w & jnp.uint32(0xFFFF0000), F32)
    lo = lax.bitcast_convert_type(w << 16, F32)
    return jnp.concatenate([hi, lo], axis=1)


def _rms(x32, g):
    return x32 * lax.rsqrt(jnp.mean(x32 * x32, -1, keepdims=True) + NORM_EPS) * g


def _gelu_tanh(x):
    c = math.sqrt(2.0 / math.pi)
    return 0.5 * x * (1.0 + jnp.tanh(c * (x + 0.044715 * (x * x * x))))


def _sigmoid(x):
    return 1.0 / (1.0 + jnp.exp(-x))


def _softplus(x):
    return jnp.maximum(x, 0.0) + jnp.log(1.0 + jnp.exp(-jnp.abs(x)))


def _ada_kernel(c_ref, w_ref, b_ref, o_ref):
    c = c_ref[...]
    s = c * _sigmoid(c)
    s_hi, s_lo = _split2(s)
    w_hi, w_lo = _split2(w_ref[...])
    o_ref[...] = _dot(s_hi, w_hi) + _dot(s_lo, w_hi) + _dot(s_hi, w_lo) + b_ref[...]


def _ada(cs, w_ada, b_ada):
    n = w_ada.shape[1]
    bn = D_MODEL
    return pl.pallas_call(
        _ada_kernel,
        out_shape=jax.ShapeDtypeStruct((cs.shape[0], n), F32),
        grid=(n // bn,),
        in_specs=[pl.BlockSpec(cs.shape, lambda j: (0, 0)),
                  pl.BlockSpec((D_MODEL, bn), lambda j: (0, j)),
                  pl.BlockSpec((1, bn), lambda j: (0, j))],
        out_specs=pl.BlockSpec((cs.shape[0], bn), lambda j: (0, j)),
        compiler_params=_params("parallel"),
        name="ada_mod",
    )(cs, w_ada, b_ada)


def _inproj_kernel(x_ref, mod_ref, g_ref, w_hbm,
                   qkv_ref, z_ref, u_ref, v_ref, ba_ref, bat_ref,
                   w_f32, wqkv_ref, wzuv_ref, wba_ref, sem):
    @pl.when(pl.program_id(0) == 0)
    def _():
        copy = pltpu.make_async_copy(w_hbm, w_f32, sem)
        copy.start()
        copy.wait()
        wqkv_ref[...] = w_f32[:, :QKV_COLS].astype(BF16)
        wzuv_ref[...] = w_f32[:, QKV_COLS + N_GATE_COLS:].astype(BF16)
        lane = lax.broadcasted_iota(jnp.int32, (D_MODEL, LANES), 1)
        w_gate = jnp.where(lane < N_GATE_COLS, w_f32[:, QKV_COLS:QKV_COLS + LANES], 0.0)
        w_hi, w_lo = _split2(w_gate)
        wba_ref[:, :LANES] = w_hi
        wba_ref[:, LANES:] = w_lo

    mod = mod_ref[0]
    subs = _sub_tiles(x_ref.shape[0])
    hs = [_split2(_rms(x_ref[sl], g_ref[...]) * (1.0 + mod[1:2]) + mod[0:1]) for sl in subs]
    qkvs = [_dot(h_hi, wqkv_ref[...]) for h_hi, _ in hs]
    zuvs = [_dot(h_hi, wzuv_ref[...]) for h_hi, _ in hs]
    bas = []
    for h_hi, h_lo in hs:
        both = _dot(h_hi, wba_ref[...])
        bas.append(both[:, :LANES] + both[:, LANES:] + _dot(h_lo, wba_ref[:, :LANES]))
    for sl, qkv, zuv, ba in zip(subs, qkvs, zuvs, bas):
        qkv_ref[sl] = qkv.astype(BF16)
        z_ref[sl] = zuv[:, :GDN_W].astype(BF16)
        u_ref[sl] = _gelu_tanh(zuv[:, GDN_W:GDN_W + SGU_W]).astype(BF16)
        v_ref[sl] = _gelu_tanh(zuv[:, GDN_W + SGU_W:]).astype(BF16)
        ba_ref[sl] = ba[:, :N_GATE_COLS]
        bat_ref[:, sl] = ba.T[:N_GATE_COLS]


def _inproj(x2d, mod, rows_per_mod, ng0, w_in):
    t = x2d.shape[0]
    tm = min(ROW_TILE, t)
    tiles_per_mod = rows_per_mod // tm
    row = lambda i: (i, 0)
    const = lambda i: (0, 0)
    return pl.pallas_call(
        _inproj_kernel,
        out_shape=(jax.ShapeDtypeStruct((t, QKV_COLS), BF16),
                   jax.ShapeDtypeStruct((t, GDN_W), BF16),
                   jax.ShapeDtypeStruct((t, SGU_W), BF16),
                   jax.ShapeDtypeStruct((t, SGU_W), BF16),
                   jax.ShapeDtypeStruct((t, N_GATE_COLS), F32),
                   jax.ShapeDtypeStruct((N_GATE_COLS, t), F32)),
        grid=(t // tm,),
        in_specs=[pl.BlockSpec((tm, D_MODEL), row),
                  pl.BlockSpec((1, 6, D_MODEL), lambda i: (i // tiles_per_mod, 0, 0)),
                  pl.BlockSpec((1, D_MODEL), const),
                  pl.BlockSpec(memory_space=pl.ANY)],
        out_specs=(pl.BlockSpec((tm, QKV_COLS), row),
                   pl.BlockSpec((tm, GDN_W), row),
                   pl.BlockSpec((tm, SGU_W), row),
                   pl.BlockSpec((tm, SGU_W), row),
                   pl.BlockSpec((tm, N_GATE_COLS), row),
                   pl.BlockSpec((N_GATE_COLS, tm), lambda i: (0, i))),
        scratch_shapes=[pltpu.VMEM(w_in.shape, F32),
                        pltpu.VMEM((D_MODEL, QKV_COLS), BF16),
                        pltpu.VMEM((D_MODEL, w_in.shape[1] - QKV_COLS - N_GATE_COLS), BF16),
                        pltpu.VMEM((D_MODEL, 2 * LANES), BF16),
                        pltpu.SemaphoreType.DMA],
        compiler_params=_params("arbitrary"),
        name="in_proj",
    )(x2d, mod, ng0, w_in)


def _gdn_prep_kernel(row_len, qkv_ref, cw_ref, ba_ref, bat_ref, alog_r_ref, dtb_r_ref,
                     alog_c_ref, dtb_c_ref, u_ref, w_ref, qd_ref, kd_ref, at_ref, gl_ref,
                     tri_scr, pair_scr, spread_scr):
    n = GDN_BLOCK
    c = DELTA_CHUNK

    def mask01(m):
        return jnp.where(m, 1.0, 0.0).astype(BF16)

    wi = lax.broadcasted_iota(jnp.int32, (c, n), 0)
    wl = lax.broadcasted_iota(jnp.int32, (c, n), 1)
    wchunk = wl // c
    wj = wl % c
    lower_w = wi >= wj
    upper_w = wi <= wj
    diag_w = wi == wj
    eye_w = jnp.where(diag_w, 1.0, 0.0)

    @pl.when(jnp.logical_and(pl.program_id(0) == 0, pl.program_id(1) == 0))
    def _():
        ri = lax.broadcasted_iota(jnp.int32, (n, n), 0)
        ci = lax.broadcasted_iota(jnp.int32, (n, n), 1)
        same = (ri // c) == (ci // c)
        tri_scr[0] = mask01(same & (ri >= ci))
        tri_scr[1] = mask01(same & (ri <= ci))
        tri_scr[2] = mask01(same)
        for m in range(N_PAIR_LEVELS):
            s = 1 << m
            pair_scr[m] = mask01(((wi // (2 * s)) == (wj // (2 * s))) & ((wi // s) != (wj // s)))
        ei = lax.broadcasted_iota(jnp.int32, (n, CHUNKS_PER_BLOCK * HEAD_DIM), 0) // c
        ej = lax.broadcasted_iota(jnp.int32, (n, CHUNKS_PER_BLOCK * HEAD_DIM), 1) // HEAD_DIM
        spread_scr[...] = mask01(ei == ej)

    lower_b = tri_scr[0]
    upper_b = tri_scr[1]
    same_b = tri_scr[2]
    pair_masks = [pair_scr[m] for m in range(N_PAIR_LEVELS)]

    def to_wide(full):
        out = full[:c]
        for k in range(1, CHUNKS_PER_BLOCK):
            out = jnp.where(wchunk == k, full[k * c:(k + 1) * c], out)
        return out

    def col_wide(col):
        out = jnp.broadcast_to(col[:c], (c, n))
        for k in range(1, CHUNKS_PER_BLOCK):
            out = jnp.where(wchunk == k, jnp.broadcast_to(col[k * c:(k + 1) * c], (c, n)), out)
        return out

    def block_diag(x_w):
        return jnp.concatenate([x_w] * CHUNKS_PER_BLOCK, axis=0) * same_b

    ba = ba_ref[0]
    bat = bat_ref[...]
    beta_c = _sigmoid(ba[:, :N_CHAINS])
    g_c = -jnp.exp(alog_r_ref[...]) * _softplus(ba[:, N_CHAINS:] + dtb_r_ref[...])
    g_r = -jnp.exp(alog_c_ref[...]) * _softplus(bat[N_CHAINS:] + dtb_c_ref[...])
    gc3 = _split3(g_c)
    gr3 = jnp.concatenate(_split3(g_r), axis=0)

    def sum3_r(m):
        return m[:N_CHAINS] + m[N_CHAINS:2 * N_CHAINS] + m[2 * N_CHAINS:]

    cum_f_c = _dot(lower_b, gc3[0]) + _dot(lower_b, gc3[1]) + _dot(lower_b, gc3[2])
    tot_c = _dot(same_b, gc3[0]) + _dot(same_b, gc3[1]) + _dot(same_b, gc3[2])
    cum_b_c = tot_c - cum_f_c + g_c
    cum_f_r = sum3_r(_dot(gr3, upper_b))
    cum_b_r = sum3_r(_dot(gr3, lower_b))
    g_last = jnp.exp(sum3_r(_dot(gr3, spread_scr[...])))
    gl_ref[0, 0, 0] = g_last[:GDN_HEADS]
    gl_ref[0, 0, 1] = g_last[GDN_HEADS:]

    pos = lax.broadcasted_iota(jnp.int32, (n, HEAD_DIM), 0) % row_len
    first = pos == 0
    last = pos == row_len - 1

    def conv_silu(col):
        x = qkv_ref[0, :, col * HEAD_DIM:(col + 1) * HEAD_DIM].astype(F32)
        cw = cw_ref[:, col * HEAD_DIM:(col + 1) * HEAD_DIM]
        xp = jnp.where(first, 0.0, pltpu.roll(x, 1, 0))
        xn = jnp.where(last, 0.0, pltpu.roll(x, n - 1, 0))
        y = xp * cw[0:1] + x * cw[1:2] + xn * cw[2:3]
        return y * _sigmoid(y)

    def l2n(x):
        return x * lax.rsqrt(jnp.sum(x * x, -1, keepdims=True) + NORM_EPS)

    a_bs, ps, rhss = [None] * N_CHAINS, [None] * N_CHAINS, [None] * N_CHAINS
    first_pairs = pair_masks[0].astype(F32)
    for h in range(GDN_HEADS):
        q = l2n(conv_silu(h)) * (HEAD_DIM ** -0.5)
        k = l2n(conv_silu(GDN_HEADS + h))
        v = conv_silu(2 * GDN_HEADS + h)
        k_b = k.astype(BF16)
        qk_kk = _dot_nt(jnp.concatenate([q.astype(BF16), k_b], axis=0), k_b)
        qk_w = to_wide(qk_kk[:n])
        kk_w = to_wide(qk_kk[n:])
        for d in range(2):
            j = d * GDN_HEADS + h
            mask_w = lower_w if d == 0 else upper_w
            cum_c = (cum_f_c if d == 0 else cum_b_c)[:, j:j + 1]
            cum_r = (cum_f_r if d == 0 else cum_b_r)[j:j + 1, :]
            b_c = beta_c[:, j:j + 1]
            decay_w = jnp.where(mask_w, jnp.exp(jnp.where(mask_w, col_wide(cum_c) - cum_r, 0.0)), 0.0)
            amat_w = jnp.where(diag_w, 0.0, kk_w * decay_w * col_wide(b_c))
            a_bs[j] = amat_w.astype(BF16)
            ps[j] = eye_w - amat_w * first_pairs
            e_c = jnp.exp(cum_c)
            rhss[j] = jnp.concatenate([(v * b_c).astype(BF16), (k * (b_c * e_c)).astype(BF16)], axis=1)
            cols = slice(j * HEAD_DIM, (j + 1) * HEAD_DIM)
            qd_ref[0, :, cols] = (q * e_c).astype(BF16)
            kd_ref[0, :, cols] = (k * jnp.exp(tot_c[:, j:j + 1] - cum_c)).astype(BF16)
            at_ref[0, 0, j * c:(j + 1) * c, :] = (qk_w * decay_w).astype(BF16)

    for pm in pair_masks[1:]:
        p_bs = [p.astype(BF16) for p in ps]
        ys = [_dot(a_bs[j] * pm, block_diag(p_bs[j])) for j in range(N_CHAINS)]
        ps = [ps[j] - _dot(p_bs[j], block_diag(ys[j].astype(BF16))) for j in range(N_CHAINS)]

    for j in range(N_CHAINS):
        uw = _dot(block_diag(ps[j].astype(BF16)), rhss[j])
        cols = slice(j * HEAD_DIM, (j + 1) * HEAD_DIM)
        u_ref[0, :, cols] = uw[:, :HEAD_DIM].astype(BF16)
        w_ref[0, :, cols] = uw[:, HEAD_DIM:].astype(BF16)


def _gdn_prep(qkv, conv_w, ba, bat, alog, dtb, row_len):
    b, l, _ = qkv.shape
    nblk = l // GDN_BLOCK
    wide = N_CHAINS * HEAD_DIM
    blk = lambda bi, i: (bi, i, 0)
    const = lambda bi, i: (0, 0)
    alog_r, dtb_r = alog.reshape(1, N_CHAINS), dtb.reshape(1, N_CHAINS)
    alog_c, dtb_c = alog.reshape(N_CHAINS, 1), dtb.reshape(N_CHAINS, 1)
    return pl.pallas_call(
        functools.partial(_gdn_prep_kernel, row_len),
        out_shape=(jax.ShapeDtypeStruct((b, l, wide), BF16),) * 4 + (
            jax.ShapeDtypeStruct((b, nblk, N_CHAINS * DELTA_CHUNK, GDN_BLOCK), BF16),
            jax.ShapeDtypeStruct((b, nblk, 2, GDN_HEADS, CHUNKS_PER_BLOCK * HEAD_DIM), F32)),
        grid=(b, nblk),
        in_specs=[pl.BlockSpec((1, GDN_BLOCK, QKV_COLS), blk),
                  pl.BlockSpec((3, QKV_COLS), const),
                  pl.BlockSpec((1, GDN_BLOCK, N_GATE_COLS), blk),
                  pl.BlockSpec((N_GATE_COLS, GDN_BLOCK), lambda bi, i: (0, bi * nblk + i)),
                  pl.BlockSpec((1, N_CHAINS), const),
                  pl.BlockSpec((1, N_CHAINS), const),
                  pl.BlockSpec((N_CHAINS, 1), const),
                  pl.BlockSpec((N_CHAINS, 1), const)],
        out_specs=(pl.BlockSpec((1, GDN_BLOCK, wide), blk),) * 4 + (
            pl.BlockSpec((1, 1, N_CHAINS * DELTA_CHUNK, GDN_BLOCK), lambda bi, i: (bi, i, 0, 0)),
            pl.BlockSpec((1, 1, 2, GDN_HEADS, CHUNKS_PER_BLOCK * HEAD_DIM),
                         lambda bi, i: (bi, i, 0, 0, 0))),
        scratch_shapes=[pltpu.VMEM((3, GDN_BLOCK, GDN_BLOCK), BF16),
                        pltpu.VMEM((N_PAIR_LEVELS, DELTA_CHUNK, GDN_BLOCK), BF16),
                        pltpu.VMEM((GDN_BLOCK, CHUNKS_PER_BLOCK * HEAD_DIM), BF16)],
        compiler_params=_params("arbitrary", "arbitrary"),
        name="gdn_prep",
    )(qkv, conv_w, ba, bat, alog_r, dtb_r, alog_c, dtb_c)


def _gdn_scan_kernel(uf, wf, qf, kf, af, gf, ub, wb, qb, kb, ab, gb, s0_ref,
                     of_ref, ob_ref, sfin_ref, s_scr):
    i = pl.program_id(0)
    c = DELTA_CHUNK
    n_batch = s0_ref.shape[0]

    @pl.when(i == 0)
    def _():
        s_scr[...] = s0_ref[...]

    ops = ((uf, wf, qf, kf, af, gf, of_ref), (ub, wb, qb, kb, ab, gb, ob_ref))
    chains = [(bi, d, h) for bi in range(n_batch) for d in range(2) for h in range(GDN_HEADS)]
    states = [s_scr[bi, d * GDN_HEADS + h] for bi, d, h in chains]
    for step in range(CHUNKS_PER_BLOCK):
        def chunk(d):
            cc = step if d == 0 else CHUNKS_PER_BLOCK - 1 - step
            return cc, slice(cc * c, (cc + 1) * c)

        xs = []
        for j, (bi, d, h) in enumerate(chains):
            _, rows = chunk(d)
            cols = slice(h * HEAD_DIM, (h + 1) * HEAD_DIM)
            wq = jnp.concatenate([ops[d][1][bi, rows, cols], ops[d][2][bi, rows, cols]], axis=0)
            xs.append(_dot(wq, states[j].astype(BF16)))
        v_news = []
        for j, (bi, d, h) in enumerate(chains):
            _, rows = chunk(d)
            cols = slice(h * HEAD_DIM, (h + 1) * HEAD_DIM)
            v_news.append((ops[d][0][bi, rows, cols].astype(F32) - xs[j][:c]).astype(BF16))
        for j, (bi, d, h) in enumerate(chains):
            cc, rows = chunk(d)
            cols = slice(h * HEAD_DIM, (h + 1) * HEAD_DIM)
            a_c = ops[d][4][bi, 0, h * c:(h + 1) * c, cc * c:(cc + 1) * c]
            ops[d][6][bi, rows, cols] = (xs[j][c:] + _dot(a_c, v_news[j])).astype(BF16)
            ds = _dot_tn(ops[d][3][bi, rows, cols], v_news[j])
            g_last = ops[d][5][bi, 0, 0, h:h + 1, cc * HEAD_DIM:(cc + 1) * HEAD_DIM]
            states[j] = states[j] * g_last + ds
    for j, (bi, d, h) in enumerate(chains):
        s_scr[bi, d * GDN_HEADS + h] = states[j]

    @pl.when(i == pl.num_programs(0) - 1)
    def _():
        sfin_ref[...] = s_scr[...]


def _gdn_scan(u, w, qd, kd, at, gl, s0):
    b, l, _ = u.shape
    nblk = l // GDN_BLOCK
    half = GDN_HEADS * HEAD_DIM
    fwd = lambda i: (0, i, 0)
    bwd = lambda i: (0, nblk - 1 - i, 1)
    big = lambda m: pl.BlockSpec((b, GDN_BLOCK, half), m)
    att_shape = (b, 1, GDN_HEADS * DELTA_CHUNK, GDN_BLOCK)
    attf = pl.BlockSpec(att_shape, lambda i: (0, i, 0, 0))
    attb = pl.BlockSpec(att_shape, lambda i: (0, nblk - 1 - i, 1, 0))
    gl_shape = (b, 1, 1, GDN_HEADS, CHUNKS_PER_BLOCK * HEAD_DIM)
    glf = pl.BlockSpec(gl_shape, lambda i: (0, i, 0, 0, 0))
    glb = pl.BlockSpec(gl_shape, lambda i: (0, nblk - 1 - i, 1, 0, 0))
    state = pl.BlockSpec((b, N_CHAINS, HEAD_DIM, HEAD_DIM), lambda i: (0, 0, 0, 0))
    return pl.pallas_call(
        _gdn_scan_kernel,
        out_shape=(jax.ShapeDtypeStruct((b, l, half), BF16),
                   jax.ShapeDtypeStruct((b, l, half), BF16),
                   jax.ShapeDtypeStruct((b, N_CHAINS, HEAD_DIM, HEAD_DIM), F32)),
        grid=(nblk,),
        in_specs=[big(fwd), big(fwd), big(fwd), big(fwd), attf, glf,
                  big(bwd), big(bwd), big(bwd), big(bwd), attb, glb, state],
        out_specs=(pl.BlockSpec((b, GDN_BLOCK, half), fwd),
                   pl.BlockSpec((b, GDN_BLOCK, half), lambda i: (0, nblk - 1 - i, 0)),
                   state),
        scratch_shapes=[pltpu.VMEM((b, N_CHAINS, HEAD_DIM, HEAD_DIM), F32)],
        compiler_params=_params("arbitrary"),
        name="gdn_scan",
    )(u, w, qd, kd, at, gl, u, w, qd, kd, at, gl, s0)


def _mix_into(mix_ref, of_ref, ob_ref, z_ref, u_ref, v_ref, gng_ref, lng_ref, lnb_ref, ws_ref,
              bs_ref):
    tm = of_ref.shape[0]
    o = of_ref[...].astype(F32) + ob_ref[...].astype(F32)
    z = z_ref[...].astype(F32)
    for h in range(GDN_HEADS):
        cols = slice(h * HEAD_DIM, (h + 1) * HEAD_DIM)
        oh = o[:, cols]
        zh = z[:, cols]
        r = lax.rsqrt(jnp.mean(oh * oh, -1, keepdims=True) + NORM_EPS)
        mix_ref[:, cols] = (oh * r * gng_ref[...] * (zh * _sigmoid(zh))).astype(BF16)
    for g in range(SGU_GROUPS):
        cols = slice(g * SGU_GROUP, (g + 1) * SGU_GROUP)
        vg = v_ref[:, cols].astype(F32)
        vc = vg - jnp.mean(vg, -1, keepdims=True)
        vn = vc * lax.rsqrt(jnp.mean(vc * vc, -1, keepdims=True) + NORM_EPS)
        vn = (vn * lng_ref[g:g + 1] + lnb_ref[g:g + 1]).astype(BF16)
        wsg = ws_ref[g].astype(BF16)
        for n in range(tm // SGU_CHUNK):
            rows = slice(n * SGU_CHUNK, (n + 1) * SGU_CHUNK)
            s = _dot(wsg, vn[rows]) + bs_ref[g]
            mix_ref[rows, GDN_W + g * SGU_GROUP:GDN_W + (g + 1) * SGU_GROUP] = (
                u_ref[rows, cols].astype(F32) * s).astype(BF16)


def _outproj_kernel(of_ref, ob_ref, z_ref, u_ref, v_ref, gng_ref, lng_ref, lnb_ref, ws_ref, bs_ref,
                    x_ref, mod_ref, ng_ref, wout_ref, rwt_ref, rb_ref,
                    x1_ref, h2_ref, lt_ref, mix_scr):
    subs = _sub_tiles(x_ref.shape[0])
    for sl in subs:
        _mix_into(mix_scr.at[sl], of_ref.at[sl], ob_ref.at[sl], z_ref.at[sl], u_ref.at[sl],
                  v_ref.at[sl], gng_ref, lng_ref, lnb_ref, ws_ref, bs_ref)
    mod = mod_ref[0]
    ys = [_dot(mix_scr[sl], wout_ref[...]) for sl in subs]
    w_hi, w_lo = _split2(rwt_ref[...])
    for sl, y in zip(subs, ys):
        x1 = x_ref[sl] + mod[2:3] * _rms(y, ng_ref[1:2])
        x1_ref[sl] = x1
        h2 = _rms(x1, ng_ref[2:3]) * (1.0 + mod[4:5]) + mod[3:4]
        h_hi, h_lo = _split2(h2)
        _store_planes(h2_ref, _pack_rows(h_hi), sl)
        lt_ref[:, sl] = (_dot_nt(w_hi, h_hi) + _dot_nt(w_lo, h_hi) + _dot_nt(w_hi, h_lo)
                         + rb_ref[...])


def _outproj(o_f, o_b, z, ug, vg, gdn_norm_g, ln_g, ln_b, w_s, b_s_full,
             x2d, mod, rows_per_mod, ng, wout, rwt, rb_col):
    t = x2d.shape[0]
    tm = min(ROW_TILE, t)
    tiles_per_mod = rows_per_mod // tm
    row = lambda i: (i, 0)
    const = lambda i: (0, 0)
    c3 = lambda i: (0, 0, 0)
    return pl.pallas_call(
        _outproj_kernel,
        out_shape=(jax.ShapeDtypeStruct((t, D_MODEL), F32),
                   jax.ShapeDtypeStruct((N_PLANES, t, SC_ROW_W), U32),
                   jax.ShapeDtypeStruct((N_EXPERTS, t), F32)),
        grid=(t // tm,),
        in_specs=[pl.BlockSpec((tm, GDN_W), row), pl.BlockSpec((tm, GDN_W), row),
                  pl.BlockSpec((tm, GDN_W), row), pl.BlockSpec((tm, SGU_W), row),
                  pl.BlockSpec((tm, SGU_W), row),
                  pl.BlockSpec((1, HEAD_DIM), const),
                  pl.BlockSpec((SGU_GROUPS, SGU_GROUP), const),
                  pl.BlockSpec((SGU_GROUPS, SGU_GROUP), const),
                  pl.BlockSpec((SGU_GROUPS, SGU_CHUNK, SGU_CHUNK), c3),
                  pl.BlockSpec((SGU_GROUPS, SGU_CHUNK, SGU_GROUP), c3),
                  pl.BlockSpec((tm, D_MODEL), row),
                  pl.BlockSpec((1, 6, D_MODEL), lambda i: (i // tiles_per_mod, 0, 0)),
                  pl.BlockSpec((4, D_MODEL), const),
                  pl.BlockSpec((D_MODEL, D_MODEL), const),
                  pl.BlockSpec((N_EXPERTS, D_MODEL), const),
                  pl.BlockSpec((N_EXPERTS, 1), const)],
        out_specs=(pl.BlockSpec((tm, D_MODEL), row),
                   pl.BlockSpec((N_PLANES, tm, SC_ROW_W), lambda i: (0, i, 0)),
                   pl.BlockSpec((N_EXPERTS, tm), lambda i: (0, i))),
        compiler_params=_params("parallel"),
        scratch_shapes=[pltpu.VMEM((tm, D_MODEL), BF16)],
        name="out_proj_router",
    )(o_f, o_b, z, ug, vg, gdn_norm_g, ln_g, ln_b, w_s, b_s_full, x2d, mod, ng, wout, rwt, rb_col)


def _moe_kernel(be_ref, slot_ref, next_ref, nb_ref, xb_ref, wgu_hbm, bgu_ref, wd_hbm, bd_ref, y_ref,
                wgu_f, wd_f, wgu_b, wd_b, gut_scr, sems):
    i = pl.program_id(0)
    live = i < nb_ref[0]
    new_expert = jnp.logical_or(i == 0, be_ref[i] != be_ref[jnp.maximum(i - 1, 0)])

    def weight_copies(expert, slot):
        return (pltpu.make_async_copy(wgu_hbm.at[expert], wgu_f.at[slot], sems.at[slot, 0]),
                pltpu.make_async_copy(wd_hbm.at[expert], wd_f.at[slot], sems.at[slot, 1]))

    @pl.when(i == 0)
    def _():
        for copy in weight_copies(be_ref[0], 0):
            copy.start()

    @pl.when(jnp.logical_and(live, new_expert))
    def _():
        slot = slot_ref[i]
        for copy in weight_copies(be_ref[i], slot):
            copy.wait()

        @pl.when(next_ref[i] >= 0)
        def _():
            for copy in weight_copies(next_ref[i], 1 - slot):
                copy.start()
        wgu_b[...] = wgu_f[slot].astype(BF16)
        wd_b[...] = wd_f[slot].astype(BF16)

    @pl.when(live)
    def _():
        xb = _unpack_rows(_load_planes(xb_ref)).astype(BF16)
        gu = _dot(xb, wgu_b[...]) + bgu_ref[0]
        gu_t = gu.T
        acts = []
        for part in range(MOE_ROWS // LANES):
            part_ref = gut_scr.at[part]
            part_ref[...] = gu_t[:, part * LANES:(part + 1) * LANES]
            gate = jnp.minimum(part_ref[pl.ds(0, D_FF, stride=2), :], SWIGLU_LIMIT)
            up = jnp.clip(part_ref[pl.ds(1, D_FF, stride=2), :], -SWIGLU_LIMIT, SWIGLU_LIMIT)
            acts.append(((up + 1.0) * gate * _sigmoid(SWIGLU_ALPHA * gate)).astype(BF16))
        act_t = jnp.concatenate(acts, axis=1)
        _store_planes(y_ref, _pack_rows(_dot_tn(act_t, wd_b[...]) + bd_ref[0]))

    @pl.when(jnp.logical_not(live))
    def _():
        y_ref[...] = jnp.zeros_like(y_ref)


def _moe_ffn(block_e, n_used, xb, w_gu, b_gu, w_down, b_down):
    n_rows = xb.shape[1]
    n_blocks = n_rows // MOE_ROWS
    idx = jnp.arange(n_blocks, dtype=jnp.int32)
    live = idx < n_used[0]
    changed = jnp.concatenate([jnp.ones((1,), bool), block_e[1:] != block_e[:-1]]) & live
    slot = ((jnp.cumsum(changed.astype(jnp.int32)) - 1) % 2).astype(jnp.int32)
    change_at = jnp.where(changed, idx, n_blocks)
    next_change = lax.cummin(jnp.concatenate([change_at[1:], jnp.full((1,), n_blocks, jnp.int32)]),
                             reverse=True)
    next_e = jnp.where(next_change < n_blocks,
                       block_e[jnp.minimum(next_change, n_blocks - 1)], -1).astype(jnp.int32)

    row = lambda i, be, sl, nx, nb: (0, i, 0)
    ex3 = lambda i, be, sl, nx, nb: (be[i], 0, 0)
    live_row = lambda i, be, sl, nx, nb: (0, jnp.minimum(i, nb[0] - 1), 0)
    planes = (N_PLANES, MOE_ROWS, SC_ROW_W)
    grid_spec = pltpu.PrefetchScalarGridSpec(
        num_scalar_prefetch=4,
        grid=(n_blocks,),
        in_specs=[pl.BlockSpec(planes, live_row),
                  pl.BlockSpec(memory_space=pl.ANY),
                  pl.BlockSpec((1, 1, 2 * D_FF), ex3),
                  pl.BlockSpec(memory_space=pl.ANY),
                  pl.BlockSpec((1, 1, D_MODEL), ex3)],
        out_specs=pl.BlockSpec(planes, row),
        scratch_shapes=[pltpu.VMEM((2, D_MODEL, 2 * D_FF), F32),
                        pltpu.VMEM((2, D_FF, D_MODEL), F32),
                        pltpu.VMEM((D_MODEL, 2 * D_FF), BF16),
                        pltpu.VMEM((D_FF, D_MODEL), BF16),
                        pltpu.VMEM((MOE_ROWS // LANES, 2 * D_FF, LANES), F32),
                        pltpu.SemaphoreType.DMA((2, 2))],
    )
    return pl.pallas_call(
        _moe_kernel,
        out_shape=jax.ShapeDtypeStruct((N_PLANES, n_rows, SC_ROW_W), U32),
        grid_spec=grid_spec,
        compiler_params=_params("arbitrary"),
        name="moe_ffn",
    )(block_e, slot, next_e, n_used, xb, w_gu, b_gu, w_down, b_down)


def _combine_kernel(y0_ref, y1_ref, y2_ref, y3_ref, gt_ref, x1_ref, mod_ref, ng_ref, o_ref):
    mod = mod_ref[0]
    tm = o_ref.shape[0]
    gt = jnp.concatenate([gt_ref[...], jnp.zeros((LANES - SUBLANES, tm), F32)], axis=0).T
    y = _unpack_rows(_load_planes(y0_ref)) * gt[:, 0:1]
    for k, y_ref in ((1, y1_ref), (2, y2_ref), (3, y3_ref)):
        y = y + _unpack_rows(_load_planes(y_ref)) * gt[:, k:k + 1]
    o_ref[...] = x1_ref[...] + mod[5:6] * _rms(y, ng_ref[3:4])


def _combine(yg, gates, x1, mod, rows_per_mod, ng):
    t = x1.shape[0]
    tm = min(COMBINE_TILE, t)
    tiles_per_mod = rows_per_mod // tm
    n_tiles = t // tm
    row = lambda i: (i, 0)
    choice = lambda k: pl.BlockSpec((N_PLANES, tm, SC_ROW_W), lambda i: (0, k * n_tiles + i, 0))
    return pl.pallas_call(
        _combine_kernel,
        out_shape=jax.ShapeDtypeStruct((t, D_MODEL), F32),
        grid=(n_tiles,),
        in_specs=[choice(0), choice(1), choice(2), choice(3),
                  pl.BlockSpec((SUBLANES, tm), lambda i: (0, i)),
                  pl.BlockSpec((tm, D_MODEL), row),
                  pl.BlockSpec((1, 6, D_MODEL), lambda i: (i // tiles_per_mod, 0, 0)),
                  pl.BlockSpec((4, D_MODEL), lambda i: (0, 0))],
        out_specs=pl.BlockSpec((tm, D_MODEL), row),
        compiler_params=_params("parallel"),
        name="moe_combine",
    )(yg, yg, yg, yg, gates, x1, mod, ng)


def _route_kernel(lt_ref, eidx_ref, gate_ref, rank_ref, cnt_ref, carry, earlier):
    i = pl.program_id(0)
    tile = lt_ref.shape[1]

    @pl.when(i == 0)
    def _():
        carry[...] = jnp.zeros_like(carry)
        ti = lax.broadcasted_iota(jnp.int32, (tile, tile), 0)
        tj = lax.broadcasted_iota(jnp.int32, (tile, tile), 1)
        earlier[...] = jnp.where(ti < tj, 1.0, 0.0).astype(BF16)

    logits = lt_ref[...]
    eio = lax.broadcasted_iota(jnp.int32, (N_EXPERTS, tile), 0).astype(F32)
    vals, sels = [], []
    for k in range(TOP_K):
        m = jnp.max(logits, axis=0, keepdims=True)
        idx = jnp.min(jnp.where(logits == m, eio, float(N_EXPERTS)), axis=0, keepdims=True)
        sel = eio == idx
        logits = jnp.where(sel, -jnp.inf, logits)
        vals.append(m)
        sels.append(sel)
        eidx_ref[k:k + 1, :] = idx.astype(jnp.int32)
    exps = [jnp.exp(v - vals[0]) for v in vals]
    denom = exps[0] + exps[1] + exps[2] + exps[3]
    for k in range(TOP_K):
        gate_ref[k:k + 1, :] = exps[k] / denom
    gate_ref[TOP_K:, :] = jnp.zeros((SUBLANES - TOP_K, tile), F32)

    member = jnp.where(sels[0] | sels[1] | sels[2] | sels[3], 1.0, 0.0)
    before = _dot(member.astype(BF16), earlier[...]) + carry[...]
    for k in range(TOP_K):
        rank_ref[k:k + 1, :] = jnp.sum(jnp.where(sels[k], before, 0.0), axis=0,
                                       keepdims=True).astype(jnp.int32)
    carry[...] = carry[...] + jnp.sum(member, axis=1, keepdims=True)
    cnt_ref[...] = carry[...].astype(jnp.int32)


def _route(logits_t):
    t = logits_t.shape[1]
    tile = min(ROUTE_TILE, t)
    blk = lambda i: (0, i)
    return pl.pallas_call(
        _route_kernel,
        out_shape=(jax.ShapeDtypeStruct((TOP_K, t), jnp.int32),
                   jax.ShapeDtypeStruct((SUBLANES, t), F32),
                   jax.ShapeDtypeStruct((TOP_K, t), jnp.int32),
                   jax.ShapeDtypeStruct((N_EXPERTS, 1), jnp.int32)),
        grid=(t // tile,),
        in_specs=[pl.BlockSpec((N_EXPERTS, tile), blk)],
        out_specs=(pl.BlockSpec((TOP_K, tile), blk), pl.BlockSpec((SUBLANES, tile), blk),
                   pl.BlockSpec((TOP_K, tile), blk),
                   pl.BlockSpec((N_EXPERTS, 1), lambda i: (0, 0))),
        scratch_shapes=[pltpu.VMEM((N_EXPERTS, 1), F32), pltpu.VMEM((tile, tile), BF16)],
        compiler_params=_params("arbitrary"),
        name="moe_route",
    )(logits_t)


def _slot_tables(eidx, rank, counts, n_blocks):
    padded = (counts + MOE_ROWS - 1) // MOE_ROWS * MOE_ROWS
    pad_end = jnp.cumsum(padded)
    pad_start = pad_end - padded
    experts = jnp.arange(N_EXPERTS, dtype=jnp.int32)
    dest = rank + jnp.sum(jnp.where(eidx[..., None] == experts, pad_start, 0), axis=-1)
    first_row = jnp.arange(n_blocks, dtype=jnp.int32)[:, None] * MOE_ROWS
    block_e = jnp.minimum(jnp.sum((pad_end[None, :] <= first_row).astype(jnp.int32), axis=1),
                          N_EXPERTS - 1)
    n_used = pad_end[-1:] // MOE_ROWS
    return dest.astype(jnp.int32), pad_start.astype(jnp.int32), block_e, n_used.astype(jnp.int32)


def _sc_mesh():
    return plsc.VectorSubcoreMesh(core_axis_name="c", subcore_axis_name="s",
                                  num_cores=SC_CORES, num_subcores=SC_SUBCORES)


def _plane_row_ids(rows, rows_per_plane):
    return jnp.concatenate([rows + p * rows_per_plane for p in range(N_PLANES)], axis=-1)


def _sc_gather_rows(table, rows):
    v = table.shape[1]
    idx = _plane_row_ids(rows, v)[None]
    n_all = idx.shape[1]

    @functools.partial(pl.kernel, mesh=_sc_mesh(), name="moe_gather_rows",
                       out_type=jax.ShapeDtypeStruct((n_all, SC_ROW_W), U32))
    def gather(x_hbm, i_hbm, o_hbm):
        def body(i_vmem, o_vmem):
            pltpu.sync_copy(x_hbm.at[i_vmem.at[0]], o_vmem)

        pltpu.emit_pipeline(
            body, grid=(n_all // SC_WINDOW,),
            in_specs=[pl.BlockSpec((1, SC_WINDOW), lambda i: (0, i))],
            out_specs=[pl.BlockSpec((SC_WINDOW, SC_ROW_W), lambda i: (i, 0))],
            core_axis_name=("c", "s"), dimension_semantics=(pltpu.PARALLEL,),
        )(i_hbm, o_hbm)

    return gather(table.reshape(N_PLANES * v, SC_ROW_W), idx).reshape(N_PLANES, -1, SC_ROW_W)


def _sc_scatter_rows(rows, dest, n_out):
    t = rows.shape[1]
    idx = _plane_row_ids(dest, n_out)

    @functools.partial(pl.kernel, mesh=_sc_mesh(), name="moe_scatter_rows", scratch_types=[],
                       out_type=jax.ShapeDtypeStruct((N_PLANES * n_out, SC_ROW_W), U32))
    def scatter(x_hbm, i_hbm, o_hbm):
        def body(x_vmem, i_vmem):
            for k in range(TOP_K):
                pltpu.sync_copy(x_vmem, o_hbm.at[i_vmem.at[k]])

        pltpu.emit_pipeline(
            body, grid=(N_PLANES * t // SC_WINDOW,),
            in_specs=[pl.BlockSpec((SC_WINDOW, SC_ROW_W), lambda i: (i, 0)),
                      pl.BlockSpec((TOP_K, SC_WINDOW), lambda i: (0, i))],
            out_specs=[],
            core_axis_name=("c", "s"), dimension_semantics=(pltpu.PARALLEL,),
        )(x_hbm, i_hbm)

    return scatter(rows.reshape(N_PLANES * t, SC_ROW_W), idx).reshape(N_PLANES, n_out, SC_ROW_W)


def _zero_pad_kernel(cnt_ref, start_ref, xb_in_ref, xb_ref, zero_scr, sem):
    del xb_in_ref
    zero_scr[...] = jnp.zeros_like(zero_scr)

    pieces = [SUBLANES << bit for bit in range((MOE_ROWS // SUBLANES - 1).bit_length())]

    def zero_copy(p, row, size):
        return pltpu.make_async_copy(zero_scr.at[pl.ds(0, size)], xb_ref.at[p, pl.ds(row, size)], sem)

    def for_each_piece(fn):
        def per_expert(e, carry):
            n_real = cnt_ref[e]
            n_pad = (MOE_ROWS - n_real % MOE_ROWS) % MOE_ROWS
            first = start_ref[e] + n_real
            n_single = n_pad % SUBLANES
            for j in range(SUBLANES - 1):
                @pl.when(j < n_single)
                def _():
                    for p in range(N_PLANES):
                        fn(zero_copy(p, first + j, 1))
            row = first + n_single
            for size in pieces:
                @pl.when((n_pad & size) != 0)
                def _():
                    for p in range(N_PLANES):
                        fn(zero_copy(p, pl.multiple_of(row, SUBLANES), size))
                row = row + (n_pad & size)
            return carry
        lax.fori_loop(0, N_EXPERTS, per_expert, 0)

    for_each_piece(lambda copy: copy.start())
    for_each_piece(lambda copy: copy.wait())


def _zero_pad_slots(counts, pad_start, xb):
    grid_spec = pltpu.PrefetchScalarGridSpec(
        num_scalar_prefetch=2,
        grid=(1,),
        in_specs=[pl.BlockSpec(memory_space=pl.ANY)],
        out_specs=pl.BlockSpec(memory_space=pl.ANY),
        scratch_shapes=[pltpu.VMEM((MOE_ROWS // 2, SC_ROW_W), U32), pltpu.SemaphoreType.DMA],
    )
    return pl.pallas_call(
        _zero_pad_kernel,
        out_shape=jax.ShapeDtypeStruct(xb.shape, xb.dtype),
        grid_spec=grid_spec,
        input_output_aliases={2: 0},
        compiler_params=_params("arbitrary"),
        name="moe_zero_pad",
    )(counts, pad_start, xb)


def kernel(x, c, ctx, c_ctx, w_ada, b_ada, norm_g, w_in, conv_w, a_log, dt_bias, gdn_norm_g,
           sgu_ln_g, sgu_ln_b, sgu_w, sgu_b, w_out, router_w, router_b, w_gu, b_gu, w_down, b_down):
    b, l, d = x.shape
    lc = ctx.shape[1]
    t = b * l
    assert d == D_MODEL and l % ROW_TILE == 0 and l % GDN_BLOCK == 0 and lc % GDN_BLOCK == 0
    assert w_ada.shape[0] == 1, "single-layer block"

    cs = jnp.concatenate([c, c_ctx[None], jnp.zeros((8 - b - 1, d), F32)], axis=0)
    mod_all = _ada(cs, w_ada[0], b_ada[0][None])
    mod = mod_all[:b].reshape(b, 6, d)
    mod_c = mod_all[b:b + 1].reshape(1, 6, d)
    ng = norm_g[0]


    x2d = x.reshape(t, d)
    qkv, z, ug, vg, ba, bat = _inproj(x2d, mod, l, ng[0:1], w_in[0])
    ctx2d = ctx.reshape(b * lc, d)
    qkv_c, _, _, _, ba_c, bat_c = _inproj(ctx2d, mod_c, b * lc, ng[0:1], w_in[0])

    alog = a_log[0].reshape(-1)
    dtb = dt_bias[0].reshape(-1)
    pc = _gdn_prep(qkv_c.reshape(b, lc, QKV_COLS), conv_w[0], ba_c.reshape(b, lc, N_GATE_COLS),
                   bat_c, alog, dtb, lc)
    s_zero = jnp.zeros((b, N_CHAINS, HEAD_DIM, HEAD_DIM), F32)
    _, _, s_ctx = _gdn_scan(*pc, s_zero)
    pp = _gdn_prep(qkv.reshape(b, l, QKV_COLS), conv_w[0], ba.reshape(b, l, N_GATE_COLS),
                   bat, alog, dtb, GRID_W)
    o_f, o_b, _ = _gdn_scan(*pp, s_ctx)

    b_s_full = jnp.broadcast_to(sgu_b[0][:, :, None], (SGU_GROUPS, SGU_CHUNK, SGU_GROUP))
    x1, h2p, logits_t = _outproj(o_f.reshape(t, GDN_W), o_b.reshape(t, GDN_W), z, ug, vg,
                                 gdn_norm_g, sgu_ln_g[0], sgu_ln_b[0], sgu_w[0], b_s_full,
                                 x2d, mod, l, ng, w_out[0].astype(BF16),
                                 router_w[0].T, router_b[0][:, None])

    eidx, gates_t, rank, counts = _route(logits_t)
    counts = counts[:, 0]
    n_blocks = -(-(t * TOP_K) // MOE_ROWS) + N_EXPERTS
    dest, pad_start, block_e, n_used = _slot_tables(eidx, rank, counts, n_blocks)
    xb = _sc_scatter_rows(h2p, dest, n_blocks * MOE_ROWS)
    xb = _zero_pad_slots(counts, pad_start, xb)
    yb = _moe_ffn(block_e, n_used, xb, w_gu[0], b_gu[0][:, None, :], w_down[0],
                  b_down[0][:, None, :])
    yg = _sc_gather_rows(yb, dest.reshape(-1))
    out = _combine(yg, gates_t, x1, mod, l, ng)
    return out.reshape(b, l, d)
```

```python
import functools
import math

import jax
import jax.numpy as jnp
from jax import lax
from jax.experimental import pallas as pl
from jax.experimental.pallas import tpu as pltpu
from jax.experimental.pallas import tpu_sc as plsc

F32 = jnp.float32
BF16 = jnp.bfloat16

D_MODEL = 1024
GDN_HEADS = 4
HEAD_DIM = 128
GDN_W = GDN_HEADS * HEAD_DIM
SGU_GROUPS = 4
SGU_GROUP = 128
SGU_W = SGU_GROUPS * SGU_GROUP
SGU_CHUNK = 128
DELTA_CHUNK = 64
GRID_W = 64
N_EXPERTS = 32
TOP_K = 4
D_FF = 1024
SWIGLU_LIMIT = 7.0
SWIGLU_ALPHA = 1.702
NORM_EPS = 1e-6
QKV_COLS = 3 * GDN_W
N_CHAINS = 2 * GDN_HEADS
N_GATE_COLS = 2 * N_CHAINS

ROW_TILE = 1024
SUB_ROWS = 256
COMBINE_TILE = 512
GDN_BLOCK = 256
CHUNKS_PER_BLOCK = GDN_BLOCK // DELTA_CHUNK
N_PAIR_LEVELS = DELTA_CHUNK.bit_length() - 1
MOE_ROWS = 512
U32 = jnp.uint32
LANES = 128
SUBLANES = 8
PACKED_W = D_MODEL // 2
ROUTE_TILE = 512
SC_CORES = 2
SC_SUBCORES = 16
SC_WINDOW = 128
N_PLANES = 2
SC_ROW_W = PACKED_W // N_PLANES
VMEM_LIMIT = 56 * 1024 * 1024


def _params(*sem):
    return pltpu.CompilerParams(dimension_semantics=sem, vmem_limit_bytes=VMEM_LIMIT)


def _dot(a, b):
    return jnp.dot(a, b, preferred_element_type=F32)


def _dot_nt(a, b):
    return lax.dot_general(a, b, (((1,), (1,)), ((), ())), preferred_element_type=F32)


def _dot_tn(a, b):
    return lax.dot_general(a, b, (((0,), (0,)), ((), ())), preferred_element_type=F32)


def _split2(a):
    hi = a.astype(BF16)
    lo = (a - hi.astype(F32)).astype(BF16)
    return hi, lo


def _split3(a):
    hi = a.astype(BF16)
    r = a - hi.astype(F32)
    mid = r.astype(BF16)
    lo = (r - mid.astype(F32)).astype(BF16)
    return hi, mid, lo


def _pack_rows(x):
    xb = x.astype(BF16).astype(F32)
    hi = lax.bitcast_convert_type(xb[:, :PACKED_W], U32)
    lo = lax.bitcast_convert_type(xb[:, PACKED_W:], U32)
    return hi | (lo >> 16)


def _store_planes(ref, packed, rows=slice(None)):
    for p in range(N_PLANES):
        ref[p, rows] = packed[:, p * SC_ROW_W:(p + 1) * SC_ROW_W]


def _load_planes(ref, rows=slice(None)):
    return jnp.concatenate([ref[p, rows] for p in range(N_PLANES)], axis=1)


def _sub_tiles(n_rows):
    return [slice(r, r + SUB_ROWS) for r in range(0, n_rows, SUB_ROWS)]


def _unpack_rows(w):
    hi = lax.bitcast_convert_type(w & jnp.uint32(0xFFFF0000), F32)
    lo = lax.bitcast_convert_type(w << 16, F32)
    return jnp.concatenate([hi, lo], axis=1)


def _rms(x32, g):
    return x32 * lax.rsqrt(jnp.mean(x32 * x32, -1, keepdims=True) + NORM_EPS) * g


def _gelu_tanh(x):
    c = math.sqrt(2.0 / math.pi)
    return 0.5 * x * (1.0 + jnp.tanh(c * (x + 0.044715 * (x * x * x))))


def _sigmoid(x):
    return 1.0 / (1.0 + jnp.exp(-x))


def _softplus(x):
    return jnp.maximum(x, 0.0) + jnp.log(1.0 + jnp.exp(-jnp.abs(x)))


def _ada_kernel(c_ref, w_ref, b_ref, o_ref):
    c = c_ref[...]
    s = c * _sigmoid(c)
    s_hi, s_lo = _split2(s)
    w_hi, w_lo = _split2(w_ref[...])
    o_ref[...] = _dot(s_hi, w_hi) + _dot(s_lo, w_hi) + _dot(s_hi, w_lo) + b_ref[...]


def _ada(cs, w_ada, b_ada):
    n = w_ada.shape[1]
    bn = D_MODEL
    return pl.pallas_call(
        _ada_kernel,
        out_shape=jax.ShapeDtypeStruct((cs.shape[0], n), F32),
        grid=(n // bn,),
        in_specs=[pl.BlockSpec(cs.shape, lambda j: (0, 0)),
                  pl.BlockSpec((D_MODEL, bn), lambda j: (0, j)),
                  pl.BlockSpec((1, bn), lambda j: (0, j))],
        out_specs=pl.BlockSpec((cs.shape[0], bn), lambda j: (0, j)),
        compiler_params=_params("parallel"),
        name="ada_mod",
    )(cs, w_ada, b_ada)


def _inproj_kernel(x_ref, mod_ref, g_ref, w_hbm,
                   qkv_ref, z_ref, u_ref, v_ref, ba_ref, bat_ref,
                   w_f32, wqkv_ref, wzuv_ref, wba_ref, sem):
    @pl.when(pl.program_id(0) == 0)
    def _():
        copy = pltpu.make_async_copy(w_hbm, w_f32, sem)
        copy.start()
        copy.wait()
        wqkv_ref[...] = w_f32[:, :QKV_COLS].astype(BF16)
        wzuv_ref[...] = w_f32[:, QKV_COLS + N_GATE_COLS:].astype(BF16)
        lane = lax.broadcasted_iota(jnp.int32, (D_MODEL, LANES), 1)
        w_gate = jnp.where(lane < N_GATE_COLS, w_f32[:, QKV_COLS:QKV_COLS + LANES], 0.0)
        w_hi, w_lo = _split2(w_gate)
        wba_ref[:, :LANES] = w_hi
        wba_ref[:, LANES:] = w_lo

    mod = mod_ref[0]
    subs = _sub_tiles(x_ref.shape[0])
    hs = [_split2(_rms(x_ref[sl], g_ref[...]) * (1.0 + mod[1:2]) + mod[0:1]) for sl in subs]
    qkvs = [_dot(h_hi, wqkv_ref[...]) for h_hi, _ in hs]
    zuvs = [_dot(h_hi, wzuv_ref[...]) for h_hi, _ in hs]
    bas = []
    for h_hi, h_lo in hs:
        both = _dot(h_hi, wba_ref[...])
        bas.append(both[:, :LANES] + both[:, LANES:] + _dot(h_lo, wba_ref[:, :LANES]))
    for sl, qkv, zuv, ba in zip(subs, qkvs, zuvs, bas):
        qkv_ref[sl] = qkv.astype(BF16)
        z_ref[sl] = zuv[:, :GDN_W].astype(BF16)
        u_ref[sl] = _gelu_tanh(zuv[:, GDN_W:GDN_W + SGU_W]).astype(BF16)
        v_ref[sl] = _gelu_tanh(zuv[:, GDN_W + SGU_W:]).astype(BF16)
        ba_ref[sl] = ba[:, :N_GATE_COLS]
        bat_ref[:, sl] = ba.T[:N_GATE_COLS]


def _inproj(x2d, mod, rows_per_mod, ng0, w_in):
    t = x2d.shape[0]
    tm = min(ROW_TILE, t)
    tiles_per_mod = rows_per_mod // tm
    row = lambda i: (i, 0)
    const = lambda i: (0, 0)
    return pl.pallas_call(
        _inproj_kernel,
        out_shape=(jax.ShapeDtypeStruct((t, QKV_COLS), BF16),
                   jax.ShapeDtypeStruct((t, GDN_W), BF16),
                   jax.ShapeDtypeStruct((t, SGU_W), BF16),
                   jax.ShapeDtypeStruct((t, SGU_W), BF16),
                   jax.ShapeDtypeStruct((t, N_GATE_COLS), F32),
                   jax.ShapeDtypeStruct((N_GATE_COLS, t), F32)),
        grid=(t // tm,),
        in_specs=[pl.BlockSpec((tm, D_MODEL), row),
                  pl.BlockSpec((1, 6, D_MODEL), lambda i: (i // tiles_per_mod, 0, 0)),
                  pl.BlockSpec((1, D_MODEL), const),
                  pl.BlockSpec(memory_space=pl.ANY)],
        out_specs=(pl.BlockSpec((tm, QKV_COLS), row),
                   pl.BlockSpec((tm, GDN_W), row),
                   pl.BlockSpec((tm, SGU_W), row),
                   pl.BlockSpec((tm, SGU_W), row),
                   pl.BlockSpec((tm, N_GATE_COLS), row),
                   pl.BlockSpec((N_GATE_COLS, tm), lambda i: (0, i))),
        scratch_shapes=[pltpu.VMEM(w_in.shape, F32),
                        pltpu.VMEM((D_MODEL, QKV_COLS), BF16),
                        pltpu.VMEM((D_MODEL, w_in.shape[1] - QKV_COLS - N_GATE_COLS), BF16),
                        pltpu.VMEM((D_MODEL, 2 * LANES), BF16),
                        pltpu.SemaphoreType.DMA],
        compiler_params=_params("arbitrary"),
        name="in_proj",
    )(x2d, mod, ng0, w_in)


def _gdn_prep_kernel(row_len, qkv_ref, cw_ref, ba_ref, bat_ref, alog_r_ref, dtb_r_ref,
                     alog_c_ref, dtb_c_ref, u_ref, w_ref, qd_ref, kd_ref, at_ref, gl_ref,
                     tri_scr, pair_scr, spread_scr):
    n = GDN_BLOCK
    c = DELTA_CHUNK

    def mask01(m):
        return jnp.where(m, 1.0, 0.0).astype(BF16)

    wi = lax.broadcasted_iota(jnp.int32, (c, n), 0)
    wl = lax.broadcasted_iota(jnp.int32, (c, n), 1)
    wchunk = wl // c
    wj = wl % c
    lower_w = wi >= wj
    upper_w = wi <= wj
    diag_w = wi == wj
    eye_w = jnp.where(diag_w, 1.0, 0.0)

    @pl.when(jnp.logical_and(pl.program_id(0) == 0, pl.program_id(1) == 0))
    def _():
        ri = lax.broadcasted_iota(jnp.int32, (n, n), 0)
        ci = lax.broadcasted_iota(jnp.int32, (n, n), 1)
        same = (ri // c) == (ci // c)
        tri_scr[0] = mask01(same & (ri >= ci))
        tri_scr[1] = mask01(same & (ri <= ci))
        tri_scr[2] = mask01(same)
        for m in range(N_PAIR_LEVELS):
            s = 1 << m
            pair_scr[m] = mask01(((wi // (2 * s)) == (wj // (2 * s))) & ((wi // s) != (wj // s)))
        ei = lax.broadcasted_iota(jnp.int32, (n, CHUNKS_PER_BLOCK * HEAD_DIM), 0) // c
        ej = lax.broadcasted_iota(jnp.int32, (n, CHUNKS_PER_BLOCK * HEAD_DIM), 1) // HEAD_DIM
        spread_scr[...] = mask01(ei == ej)

    lower_b = tri_scr[0]
    upper_b = tri_scr[1]
    same_b = tri_scr[2]
    pair_masks = [pair_scr[m] for m in range(N_PAIR_LEVELS)]

    def to_wide(full):
        out = full[:c]
        for k in range(1, CHUNKS_PER_BLOCK):
            out = jnp.where(wchunk == k, full[k * c:(k + 1) * c], out)
        return out

    def col_wide(col):
        out = jnp.broadcast_to(col[:c], (c, n))
        for k in range(1, CHUNKS_PER_BLOCK):
            out = jnp.where(wchunk == k, jnp.broadcast_to(col[k * c:(k + 1) * c], (c, n)), out)
        return out

    def block_diag(x_w):
        return jnp.concatenate([x_w] * CHUNKS_PER_BLOCK, axis=0) * same_b

    ba = ba_ref[0]
    bat = bat_ref[...]
    beta_c = _sigmoid(ba[:, :N_CHAINS])
    g_c = -jnp.exp(alog_r_ref[...]) * _softplus(ba[:, N_CHAINS:] + dtb_r_ref[...])
    g_r = -jnp.exp(alog_c_ref[...]) * _softplus(bat[N_CHAINS:] + dtb_c_ref[...])
    gc3 = _split3(g_c)
    gr3 = jnp.concatenate(_split3(g_r), axis=0)

    def sum3_r(m):
        return m[:N_CHAINS] + m[N_CHAINS:2 * N_CHAINS] + m[2 * N_CHAINS:]

    cum_f_c = _dot(lower_b, gc3[0]) + _dot(lower_b, gc3[1]) + _dot(lower_b, gc3[2])
    tot_c = _dot(same_b, gc3[0]) + _dot(same_b, gc3[1]) + _dot(same_b, gc3[2])
    cum_b_c = tot_c - cum_f_c + g_c
    cum_f_r = sum3_r(_dot(gr3, upper_b))
    cum_b_r = sum3_r(_dot(gr3, lower_b))
    g_last = jnp.exp(sum3_r(_dot(gr3, spread_scr[...])))
    gl_ref[0, 0, 0] = g_last[:GDN_HEADS]
    gl_ref[0, 0, 1] = g_last[GDN_HEADS:]

    pos = lax.broadcasted_iota(jnp.int32, (n, HEAD_DIM), 0) % row_len
    first = pos == 0
    last = pos == row_len - 1

    def conv_silu(col):
        x = qkv_ref[0, :, col * HEAD_DIM:(col + 1) * HEAD_DIM].astype(F32)
        cw = cw_ref[:, col * HEAD_DIM:(col + 1) * HEAD_DIM]
        xp = jnp.where(first, 0.0, pltpu.roll(x, 1, 0))
        xn = jnp.where(last, 0.0, pltpu.roll(x, n - 1, 0))
        y = xp * cw[0:1] + x * cw[1:2] + xn * cw[2:3]
        return y * _sigmoid(y)

    def l2n(x):
        return x * lax.rsqrt(jnp.sum(x * x, -1, keepdims=True) + NORM_EPS)

    a_bs, ps, rhss = [None] * N_CHAINS, [None] * N_CHAINS, [None] * N_CHAINS
    first_pairs = pair_masks[0].astype(F32)
    for h in range(GDN_HEADS):
        q = l2n(conv_silu(h)) * (HEAD_DIM ** -0.5)
        k = l2n(conv_silu(GDN_HEADS + h))
        v = conv_silu(2 * GDN_HEADS + h)
        k_b = k.astype(BF16)
        qk_kk = _dot_nt(jnp.concatenate([q.astype(BF16), k_b], axis=0), k_b)
        qk_w = to_wide(qk_kk[:n])
        kk_w = to_wide(qk_kk[n:])
        for d in range(2):
            j = d * GDN_HEADS + h
            mask_w = lower_w if d == 0 else upper_w
            cum_c = (cum_f_c if d == 0 else cum_b_c)[:, j:j + 1]
            cum_r = (cum_f_r if d == 0 else cum_b_r)[j:j + 1, :]
            b_c = beta_c[:, j:j + 1]
            decay_w = jnp.where(mask_w, jnp.exp(jnp.where(mask_w, col_wide(cum_c) - cum_r, 0.0)), 0.0)
            amat_w = jnp.where(diag_w, 0.0, kk_w * decay_w * col_wide(b_c))
            a_bs[j] = amat_w.astype(BF16)
            ps[j] = eye_w - amat_w * first_pairs
            e_c = jnp.exp(cum_c)
            rhss[j] = jnp.concatenate([(v * b_c).astype(BF16), (k * (b_c * e_c)).astype(BF16)], axis=1)
            cols = slice(j * HEAD_DIM, (j + 1) * HEAD_DIM)
            qd_ref[0, :, cols] = (q * e_c).astype(BF16)
            kd_ref[0, :, cols] = (k * jnp.exp(tot_c[:, j:j + 1] - cum_c)).astype(BF16)
            at_ref[0, 0, j * c:(j + 1) * c, :] = (qk_w * decay_w).astype(BF16)

    for pm in pair_masks[1:]:
        p_bs = [p.astype(BF16) for p in ps]
        ys = [_dot(a_bs[j] * pm, block_diag(p_bs[j])) for j in range(N_CHAINS)]
        ps = [ps[j] - _dot(p_bs[j], block_diag(ys[j].astype(BF16))) for j in range(N_CHAINS)]

    for j in range(N_CHAINS):
        uw = _dot(block_diag(ps[j].astype(BF16)), rhss[j])
        cols = slice(j * HEAD_DIM, (j + 1) * HEAD_DIM)
        u_ref[0, :, cols] = uw[:, :HEAD_DIM].astype(BF16)
        w_ref[0, :, cols] = uw[:, HEAD_DIM:].astype(BF16)


def _gdn_prep(qkv, conv_w, ba, bat, alog, dtb, row_len):
    b, l, _ = qkv.shape
    nblk = l // GDN_BLOCK
    wide = N_CHAINS * HEAD_DIM
    blk = lambda bi, i: (bi, i, 0)
    const = lambda bi, i: (0, 0)
    alog_r, dtb_r = alog.reshape(1, N_CHAINS), dtb.reshape(1, N_CHAINS)
    alog_c, dtb_c = alog.reshape(N_CHAINS, 1), dtb.reshape(N_CHAINS, 1)
    return pl.pallas_call(
        functools.partial(_gdn_prep_kernel, row_len),
        out_shape=(jax.ShapeDtypeStruct((b, l, wide), BF16),) * 4 + (
            jax.ShapeDtypeStruct((b, nblk, N_CHAINS * DELTA_CHUNK, GDN_BLOCK), BF16),
            jax.ShapeDtypeStruct((b, nblk, 2, GDN_HEADS, CHUNKS_PER_BLOCK * HEAD_DIM), F32)),
        grid=(b, nblk),
        in_specs=[pl.BlockSpec((1, GDN_BLOCK, QKV_COLS), blk),
                  pl.BlockSpec((3, QKV_COLS), const),
                  pl.BlockSpec((1, GDN_BLOCK, N_GATE_COLS), blk),
                  pl.BlockSpec((N_GATE_COLS, GDN_BLOCK), lambda bi, i: (0, bi * nblk + i)),
                  pl.BlockSpec((1, N_CHAINS), const),
                  pl.BlockSpec((1, N_CHAINS), const),
                  pl.BlockSpec((N_CHAINS, 1), const),
                  pl.BlockSpec((N_CHAINS, 1), const)],
        out_specs=(pl.BlockSpec((1, GDN_BLOCK, wide), blk),) * 4 + (
            pl.BlockSpec((1, 1, N_CHAINS * DELTA_CHUNK, GDN_BLOCK), lambda bi, i: (bi, i, 0, 0)),
            pl.BlockSpec((1, 1, 2, GDN_HEADS, CHUNKS_PER_BLOCK * HEAD_DIM),
                         lambda bi, i: (bi, i, 0, 0, 0))),
        scratch_shapes=[pltpu.VMEM((3, GDN_BLOCK, GDN_BLOCK), BF16),
                        pltpu.VMEM((N_PAIR_LEVELS, DELTA_CHUNK, GDN_BLOCK), BF16),
                        pltpu.VMEM((GDN_BLOCK, CHUNKS_PER_BLOCK * HEAD_DIM), BF16)],
        compiler_params=_params("arbitrary", "arbitrary"),
        name="gdn_prep",
    )(qkv, conv_w, ba, bat, alog_r, dtb_r, alog_c, dtb_c)


def _gdn_scan_kernel(uf, wf, qf, kf, af, gf, ub, wb, qb, kb, ab, gb, s0_ref,
                     of_ref, ob_ref, sfin_ref, s_scr):
    i = pl.program_id(0)
    c = DELTA_CHUNK
    n_batch = s0_ref.shape[0]

    @pl.when(i == 0)
    def _():
        s_scr[...] = s0_ref[...]

    ops = ((uf, wf, qf, kf, af, gf, of_ref), (ub, wb, qb, kb, ab, gb, ob_ref))
    chains = [(bi, d, h) for bi in range(n_batch) for d in range(2) for h in range(GDN_HEADS)]
    states = [s_scr[bi, d * GDN_HEADS + h] for bi, d, h in chains]
    for step in range(CHUNKS_PER_BLOCK):
        def chunk(d):
            cc = step if d == 0 else CHUNKS_PER_BLOCK - 1 - step
            return cc, slice(cc * c, (cc + 1) * c)

        xs = []
        for j, (bi, d, h) in enumerate(chains):
            _, rows = chunk(d)
            cols = slice(h * HEAD_DIM, (h + 1) * HEAD_DIM)
            wq = jnp.concatenate([ops[d][1][bi, rows, cols], ops[d][2][bi, rows, cols]], axis=0)
            xs.append(_dot(wq, states[j].astype(BF16)))
        v_news = []
        for j, (bi, d, h) in enumerate(chains):
            _, rows = chunk(d)
            cols = slice(h * HEAD_DIM, (h + 1) * HEAD_DIM)
            v_news.append((ops[d][0][bi, rows, cols].astype(F32) - xs[j][:c]).astype(BF16))
        for j, (bi, d, h) in enumerate(chains):
            cc, rows = chunk(d)
            cols = slice(h * HEAD_DIM, (h + 1) * HEAD_DIM)
            a_c = ops[d][4][bi, 0, h * c:(h + 1) * c, cc * c:(cc + 1) * c]
            ops[d][6][bi, rows, cols] = (xs[j][c:] + _dot(a_c, v_news[j])).astype(BF16)
            ds = _dot_tn(ops[d][3][bi, rows, cols], v_news[j])
            g_last = ops[d][5][bi, 0, 0, h:h + 1, cc * HEAD_DIM:(cc + 1) * HEAD_DIM]
            states[j] = states[j] * g_last + ds
    for j, (bi, d, h) in enumerate(chains):
        s_scr[bi, d * GDN_HEADS + h] = states[j]

    @pl.when(i == pl.num_programs(0) - 1)
    def _():
        sfin_ref[...] = s_scr[...]


def _gdn_scan(u, w, qd, kd, at, gl, s0):
    b, l, _ = u.shape
    nblk = l // GDN_BLOCK
    half = GDN_HEADS * HEAD_DIM
    fwd = lambda i: (0, i, 0)
    bwd = lambda i: (0, nblk - 1 - i, 1)
    big = lambda m: pl.BlockSpec((b, GDN_BLOCK, half), m)
    att_shape = (b, 1, GDN_HEADS * DELTA_CHUNK, GDN_BLOCK)
    attf = pl.BlockSpec(att_shape, lambda i: (0, i, 0, 0))
    attb = pl.BlockSpec(att_shape, lambda i: (0, nblk - 1 - i, 1, 0))
    gl_shape = (b, 1, 1, GDN_HEADS, CHUNKS_PER_BLOCK * HEAD_DIM)
    glf = pl.BlockSpec(gl_shape, lambda i: (0, i, 0, 0, 0))
    glb = pl.BlockSpec(gl_shape, lambda i: (0, nblk - 1 - i, 1, 0, 0))
    state = pl.BlockSpec((b, N_CHAINS, HEAD_DIM, HEAD_DIM), lambda i: (0, 0, 0, 0))
    return pl.pallas_call(
        _gdn_scan_kernel,
        out_shape=(jax.ShapeDtypeStruct((b, l, half), BF16),
                   jax.ShapeDtypeStruct((b, l, half), BF16),
                   jax.ShapeDtypeStruct((b, N_CHAINS, HEAD_DIM, HEAD_DIM), F32)),
        grid=(nblk,),
        in_specs=[big(fwd), big(fwd), big(fwd), big(fwd), attf, glf,
                  big(bwd), big(bwd), big(bwd), big(bwd), attb, glb, state],
        out_specs=(pl.BlockSpec((b, GDN_BLOCK, half), fwd),
                   pl.BlockSpec((b, GDN_BLOCK, half), lambda i: (0, nblk - 1 - i, 0)),
                   state),
        scratch_shapes=[pltpu.VMEM((b, N_CHAINS, HEAD_DIM, HEAD_DIM), F32)],
        compiler_params=_params("arbitrary"),
        name="gdn_scan",
    )(u, w, qd, kd, at, gl, u, w, qd, kd, at, gl, s0)


def _mix_into(mix_ref, of_ref, ob_ref, z_ref, u_ref, v_ref, gng_ref, lng_ref, lnb_ref, ws_ref,
              bs_ref):
    tm = of_ref.shape[0]
    o = of_ref[...].astype(F32) + ob_ref[...].astype(F32)
    z = z_ref[...].astype(F32)
    for h in range(GDN_HEADS):
        cols = slice(h * HEAD_DIM, (h + 1) * HEAD_DIM)
        oh = o[:, cols]
        zh = z[:, cols]
        r = lax.rsqrt(jnp.mean(oh * oh, -1, keepdims=True) + NORM_EPS)
        mix_ref[:, cols] = (oh * r * gng_ref[...] * (zh * _sigmoid(zh))).astype(BF16)
    for g in range(SGU_GROUPS):
        cols = slice(g * SGU_GROUP, (g + 1) * SGU_GROUP)
        vg = v_ref[:, cols].astype(F32)
        vc = vg - jnp.mean(vg, -1, keepdims=True)
        vn = vc * lax.rsqrt(jnp.mean(vc * vc, -1, keepdims=True) + NORM_EPS)
        vn = (vn * lng_ref[g:g + 1] + lnb_ref[g:g + 1]).astype(BF16)
        wsg = ws_ref[g].astype(BF16)
        for n in range(tm // SGU_CHUNK):
            rows = slice(n * SGU_CHUNK, (n + 1) * SGU_CHUNK)
            s = _dot(wsg, vn[rows]) + bs_ref[g]
            mix_ref[rows, GDN_W + g * SGU_GROUP:GDN_W + (g + 1) * SGU_GROUP] = (
                u_ref[rows, cols].astype(F32) * s).astype(BF16)


def _outproj_kernel(of_ref, ob_ref, z_ref, u_ref, v_ref, gng_ref, lng_ref, lnb_ref, ws_ref, bs_ref,
                    x_ref, mod_ref, ng_ref, wout_ref, rwt_ref, rb_ref,
                    x1_ref, h2_ref, lt_ref, mix_scr):
    subs = _sub_tiles(x_ref.shape[0])
    for sl in subs:
        _mix_into(mix_scr.at[sl], of_ref.at[sl], ob_ref.at[sl], z_ref.at[sl], u_ref.at[sl],
                  v_ref.at[sl], gng_ref, lng_ref, lnb_ref, ws_ref, bs_ref)
    mod = mod_ref[0]
    ys = [_dot(mix_scr[sl], wout_ref[...]) for sl in subs]
    w_hi, w_lo = _split2(rwt_ref[...])
    for sl, y in zip(subs, ys):
        x1 = x_ref[sl] + mod[2:3] * _rms(y, ng_ref[1:2])
        x1_ref[sl] = x1
        h2 = _rms(x1, ng_ref[2:3]) * (1.0 + mod[4:5]) + mod[3:4]
        h_hi, h_lo = _split2(h2)
        _store_planes(h2_ref, _pack_rows(h_hi), sl)
        lt_ref[:, sl] = (_dot_nt(w_hi, h_hi) + _dot_nt(w_lo, h_hi) + _dot_nt(w_hi, h_lo)
                         + rb_ref[...])


def _outproj(o_f, o_b, z, ug, vg, gdn_norm_g, ln_g, ln_b, w_s, b_s_full,
             x2d, mod, rows_per_mod, ng, wout, rwt, rb_col):
    t = x2d.shape[0]
    tm = min(ROW_TILE, t)
    tiles_per_mod = rows_per_mod // tm
    row = lambda i: (i, 0)
    const = lambda i: (0, 0)
    c3 = lambda i: (0, 0, 0)
    return pl.pallas_call(
        _outproj_kernel,
        out_shape=(jax.ShapeDtypeStruct((t, D_MODEL), F32),
                   jax.ShapeDtypeStruct((N_PLANES, t, SC_ROW_W), U32),
                   jax.ShapeDtypeStruct((N_EXPERTS, t), F32)),
        grid=(t // tm,),
        in_specs=[pl.BlockSpec((tm, GDN_W), row), pl.BlockSpec((tm, GDN_W), row),
                  pl.BlockSpec((tm, GDN_W), row), pl.BlockSpec((tm, SGU_W), row),
                  pl.BlockSpec((tm, SGU_W), row),
                  pl.BlockSpec((1, HEAD_DIM), const),
                  pl.BlockSpec((SGU_GROUPS, SGU_GROUP), const),
                  pl.BlockSpec((SGU_GROUPS, SGU_GROUP), const),
                  pl.BlockSpec((SGU_GROUPS, SGU_CHUNK, SGU_CHUNK), c3),
                  pl.BlockSpec((SGU_GROUPS, SGU_CHUNK, SGU_GROUP), c3),
                  pl.BlockSpec((tm, D_MODEL), row),
                  pl.BlockSpec((1, 6, D_MODEL), lambda i: (i // tiles_per_mod, 0, 0)),
                  pl.BlockSpec((4, D_MODEL), const),
                  pl.BlockSpec((D_MODEL, D_MODEL), const),
                  pl.BlockSpec((N_EXPERTS, D_MODEL), const),
                  pl.BlockSpec((N_EXPERTS, 1), const)],
        out_specs=(pl.BlockSpec((tm, D_MODEL), row),
                   pl.BlockSpec((N_PLANES, tm, SC_ROW_W), lambda i: (0, i, 0)),
                   pl.BlockSpec((N_EXPERTS, tm), lambda i: (0, i))),
        compiler_params=_params("parallel"),
        scratch_shapes=[pltpu.VMEM((tm, D_MODEL), BF16)],
        name="out_proj_router",
    )(o_f, o_b, z, ug, vg, gdn_norm_g, ln_g, ln_b, w_s, b_s_full, x2d, mod, ng, wout, rwt, rb_col)


def _moe_kernel(be_ref, slot_ref, next_ref, short_ref, nb_ref, xb_ref, wgu_hbm, bgu_ref, wd_hbm,
                bd_ref, y_ref, wgu_f, wd_f, wgu_b, wd_b, gut_scr, sems):
    i = pl.program_id(0)
    live = i < nb_ref[0]
    new_expert = jnp.logical_or(i == 0, be_ref[i] != be_ref[jnp.maximum(i - 1, 0)])

    def weight_copies(expert, slot):
        return (pltpu.make_async_copy(wgu_hbm.at[expert], wgu_f.at[slot], sems.at[slot, 0]),
                pltpu.make_async_copy(wd_hbm.at[expert], wd_f.at[slot], sems.at[slot, 1]))

    @pl.when(i == 0)
    def _():
        for copy in weight_copies(be_ref[0], 0):
            copy.start()

    @pl.when(jnp.logical_and(live, new_expert))
    def _():
        slot = slot_ref[i]
        for copy in weight_copies(be_ref[i], slot):
            copy.wait()

        @pl.when(next_ref[i] >= 0)
        def _():
            for copy in weight_copies(next_ref[i], 1 - slot):
                copy.start()
        wgu_b[...] = wgu_f[slot].astype(BF16)
        wd_b[...] = wd_f[slot].astype(BF16)

    def ffn(n_rows):
        rows = slice(0, n_rows)
        xb = _unpack_rows(_load_planes(xb_ref, rows)).astype(BF16)
        gu = _dot(xb, wgu_b[...]) + bgu_ref[0]
        gu_t = gu.T
        acts = []
        for part in range(n_rows // LANES):
            part_ref = gut_scr.at[part]
            part_ref[...] = gu_t[:, part * LANES:(part + 1) * LANES]
            gate = jnp.minimum(part_ref[pl.ds(0, D_FF, stride=2), :], SWIGLU_LIMIT)
            up = jnp.clip(part_ref[pl.ds(1, D_FF, stride=2), :], -SWIGLU_LIMIT, SWIGLU_LIMIT)
            acts.append(((up + 1.0) * gate * _sigmoid(SWIGLU_ALPHA * gate)).astype(BF16))
        act_t = jnp.concatenate(acts, axis=1)
        _store_planes(y_ref, _pack_rows(_dot_tn(act_t, wd_b[...]) + bd_ref[0]), rows)

    short = short_ref[i] != 0

    @pl.when(jnp.logical_and(live, jnp.logical_not(short)))
    def _():
        ffn(MOE_ROWS)

    @pl.when(jnp.logical_and(live, short))
    def _():
        ffn(MOE_ROWS // 2)
        for p in range(N_PLANES):
            y_ref[p, MOE_ROWS // 2:] = jnp.zeros((MOE_ROWS // 2, SC_ROW_W), U32)

    @pl.when(jnp.logical_not(live))
    def _():
        y_ref[...] = jnp.zeros_like(y_ref)


def _moe_ffn(block_e, n_used, real_end, xb, w_gu, b_gu, w_down, b_down):
    n_rows = xb.shape[1]
    n_blocks = n_rows // MOE_ROWS
    idx = jnp.arange(n_blocks, dtype=jnp.int32)
    live = idx < n_used[0]
    changed = jnp.concatenate([jnp.ones((1,), bool), block_e[1:] != block_e[:-1]]) & live
    slot = ((jnp.cumsum(changed.astype(jnp.int32)) - 1) % 2).astype(jnp.int32)
    change_at = jnp.where(changed, idx, n_blocks)
    next_change = lax.cummin(jnp.concatenate([change_at[1:], jnp.full((1,), n_blocks, jnp.int32)]),
                             reverse=True)
    next_e = jnp.where(next_change < n_blocks,
                       block_e[jnp.minimum(next_change, n_blocks - 1)], -1).astype(jnp.int32)
    n_real = jnp.clip(real_end[block_e] - idx * MOE_ROWS, 0, MOE_ROWS)
    short = (n_real <= MOE_ROWS // 2).astype(jnp.int32)

    row = lambda i, be, sl, nx, sh, nb: (0, i, 0)
    ex3 = lambda i, be, sl, nx, sh, nb: (be[i], 0, 0)
    live_row = lambda i, be, sl, nx, sh, nb: (0, jnp.minimum(i, nb[0] - 1), 0)
    planes = (N_PLANES, MOE_ROWS, SC_ROW_W)
    grid_spec = pltpu.PrefetchScalarGridSpec(
        num_scalar_prefetch=5,
        grid=(n_blocks,),
        in_specs=[pl.BlockSpec(planes, live_row),
                  pl.BlockSpec(memory_space=pl.ANY),
                  pl.BlockSpec((1, 1, 2 * D_FF), ex3),
                  pl.BlockSpec(memory_space=pl.ANY),
                  pl.BlockSpec((1, 1, D_MODEL), ex3)],
        out_specs=pl.BlockSpec(planes, row),
        scratch_shapes=[pltpu.VMEM((2, D_MODEL, 2 * D_FF), F32),
                        pltpu.VMEM((2, D_FF, D_MODEL), F32),
                        pltpu.VMEM((D_MODEL, 2 * D_FF), BF16),
                        pltpu.VMEM((D_FF, D_MODEL), BF16),
                        pltpu.VMEM((MOE_ROWS // LANES, 2 * D_FF, LANES), F32),
                        pltpu.SemaphoreType.DMA((2, 2))],
    )
    return pl.pallas_call(
        _moe_kernel,
        out_shape=jax.ShapeDtypeStruct((N_PLANES, n_rows, SC_ROW_W), U32),
        grid_spec=grid_spec,
        compiler_params=_params("arbitrary"),
        name="moe_ffn",
    )(block_e, slot, next_e, short, n_used, xb, w_gu, b_gu, w_down, b_down)


def _combine_kernel(y0_ref, y1_ref, y2_ref, y3_ref, gt_ref, x1_ref, mod_ref, ng_ref, o_ref):
    mod = mod_ref[0]
    tm = o_ref.shape[0]
    gt = jnp.concatenate([gt_ref[...], jnp.zeros((LANES - SUBLANES, tm), F32)], axis=0).T
    y = _unpack_rows(_load_planes(y0_ref)) * gt[:, 0:1]
    for k, y_ref in ((1, y1_ref), (2, y2_ref), (3, y3_ref)):
        y = y + _unpack_rows(_load_planes(y_ref)) * gt[:, k:k + 1]
    o_ref[...] = x1_ref[...] + mod[5:6] * _rms(y, ng_ref[3:4])


def _combine(yg, gates, x1, mod, rows_per_mod, ng):
    t = x1.shape[0]
    tm = min(COMBINE_TILE, t)
    tiles_per_mod = rows_per_mod // tm
    n_tiles = t // tm
    row = lambda i: (i, 0)
    choice = lambda k: pl.BlockSpec((N_PLANES, tm, SC_ROW_W), lambda i: (0, k * n_tiles + i, 0))
    return pl.pallas_call(
        _combine_kernel,
        out_shape=jax.ShapeDtypeStruct((t, D_MODEL), F32),
        grid=(n_tiles,),
        in_specs=[choice(0), choice(1), choice(2), choice(3),
                  pl.BlockSpec((SUBLANES, tm), lambda i: (0, i)),
                  pl.BlockSpec((tm, D_MODEL), row),
                  pl.BlockSpec((1, 6, D_MODEL), lambda i: (i // tiles_per_mod, 0, 0)),
                  pl.BlockSpec((4, D_MODEL), lambda i: (0, 0))],
        out_specs=pl.BlockSpec((tm, D_MODEL), row),
        compiler_params=_params("parallel"),
        name="moe_combine",
    )(yg, yg, yg, yg, gates, x1, mod, ng)


def _route_kernel(lt_ref, eidx_ref, gate_ref, rank_ref, cnt_ref, carry, earlier):
    i = pl.program_id(0)
    tile = lt_ref.shape[1]

    @pl.when(i == 0)
    def _():
        carry[...] = jnp.zeros_like(carry)
        ti = lax.broadcasted_iota(jnp.int32, (tile, tile), 0)
        tj = lax.broadcasted_iota(jnp.int32, (tile, tile), 1)
        earlier[...] = jnp.where(ti < tj, 1.0, 0.0).astype(BF16)

    logits = lt_ref[...]
    eio = lax.broadcasted_iota(jnp.int32, (N_EXPERTS, tile), 0).astype(F32)
    vals, sels = [], []
    for k in range(TOP_K):
        m = jnp.max(logits, axis=0, keepdims=True)
        idx = jnp.min(jnp.where(logits == m, eio, float(N_EXPERTS)), axis=0, keepdims=True)
        sel = eio == idx
        logits = jnp.where(sel, -jnp.inf, logits)
        vals.append(m)
        sels.append(sel)
        eidx_ref[k:k + 1, :] = idx.astype(jnp.int32)
    exps = [jnp.exp(v - vals[0]) for v in vals]
    denom = exps[0] + exps[1] + exps[2] + exps[3]
    for k in range(TOP_K):
        gate_ref[k:k + 1, :] = exps[k] / denom
    gate_ref[TOP_K:, :] = jnp.zeros((SUBLANES - TOP_K, tile), F32)

    member = jnp.where(sels[0] | sels[1] | sels[2] | sels[3], 1.0, 0.0)
    before = _dot(member.astype(BF16), earlier[...]) + carry[...]
    for k in range(TOP_K):
        rank_ref[k:k + 1, :] = jnp.sum(jnp.where(sels[k], before, 0.0), axis=0,
                                       keepdims=True).astype(jnp.int32)
    carry[...] = carry[...] + jnp.sum(member, axis=1, keepdims=True)
    cnt_ref[...] = carry[...].astype(jnp.int32)


def _route(logits_t):
    t = logits_t.shape[1]
    tile = min(ROUTE_TILE, t)
    blk = lambda i: (0, i)
    return pl.pallas_call(
        _route_kernel,
        out_shape=(jax.ShapeDtypeStruct((TOP_K, t), jnp.int32),
                   jax.ShapeDtypeStruct((SUBLANES, t), F32),
                   jax.ShapeDtypeStruct((TOP_K, t), jnp.int32),
                   jax.ShapeDtypeStruct((N_EXPERTS, 1), jnp.int32)),
        grid=(t // tile,),
        in_specs=[pl.BlockSpec((N_EXPERTS, tile), blk)],
        out_specs=(pl.BlockSpec((TOP_K, tile), blk), pl.BlockSpec((SUBLANES, tile), blk),
                   pl.BlockSpec((TOP_K, tile), blk),
                   pl.BlockSpec((N_EXPERTS, 1), lambda i: (0, 0))),
        scratch_shapes=[pltpu.VMEM((N_EXPERTS, 1), F32), pltpu.VMEM((tile, tile), BF16)],
        compiler_params=_params("arbitrary"),
        name="moe_route",
    )(logits_t)


def _slot_tables(eidx, rank, counts, n_blocks):
    padded = (counts + MOE_ROWS - 1) // MOE_ROWS * MOE_ROWS
    pad_end = jnp.cumsum(padded)
    pad_start = pad_end - padded
    experts = jnp.arange(N_EXPERTS, dtype=jnp.int32)
    dest = rank + jnp.sum(jnp.where(eidx[..., None] == experts, pad_start, 0), axis=-1)
    first_row = jnp.arange(n_blocks, dtype=jnp.int32)[:, None] * MOE_ROWS
    block_e = jnp.minimum(jnp.sum((pad_end[None, :] <= first_row).astype(jnp.int32), axis=1),
                          N_EXPERTS - 1)
    n_used = pad_end[-1:] // MOE_ROWS
    real_end = (pad_start + counts).astype(jnp.int32)
    return (dest.astype(jnp.int32), pad_start.astype(jnp.int32), block_e, n_used.astype(jnp.int32),
            real_end)


def _sc_mesh():
    return plsc.VectorSubcoreMesh(core_axis_name="c", subcore_axis_name="s",
                                  num_cores=SC_CORES, num_subcores=SC_SUBCORES)


def _plane_row_ids(rows, rows_per_plane):
    return jnp.concatenate([rows + p * rows_per_plane for p in range(N_PLANES)], axis=-1)


def _sc_gather_rows(table, rows):
    v = table.shape[1]
    idx = _plane_row_ids(rows, v)[None]
    n_all = idx.shape[1]

    @functools.partial(pl.kernel, mesh=_sc_mesh(), name="moe_gather_rows",
                       out_type=jax.ShapeDtypeStruct((n_all, SC_ROW_W), U32))
    def gather(x_hbm, i_hbm, o_hbm):
        def body(i_vmem, o_vmem):
            pltpu.sync_copy(x_hbm.at[i_vmem.at[0]], o_vmem)

        pltpu.emit_pipeline(
            body, grid=(n_all // SC_WINDOW,),
            in_specs=[pl.BlockSpec((1, SC_WINDOW), lambda i: (0, i))],
            out_specs=[pl.BlockSpec((SC_WINDOW, SC_ROW_W), lambda i: (i, 0))],
            core_axis_name=("c", "s"), dimension_semantics=(pltpu.PARALLEL,),
        )(i_hbm, o_hbm)

    return gather(table.reshape(N_PLANES * v, SC_ROW_W), idx).reshape(N_PLANES, -1, SC_ROW_W)


def _sc_scatter_rows(rows, dest, n_out):
    t = rows.shape[1]
    idx = _plane_row_ids(dest, n_out)

    @functools.partial(pl.kernel, mesh=_sc_mesh(), name="moe_scatter_rows", scratch_types=[],
                       out_type=jax.ShapeDtypeStruct((N_PLANES * n_out, SC_ROW_W), U32))
    def scatter(x_hbm, i_hbm, o_hbm):
        def body(x_vmem, i_vmem):
            for k in range(TOP_K):
                pltpu.sync_copy(x_vmem, o_hbm.at[i_vmem.at[k]])

        pltpu.emit_pipeline(
            body, grid=(N_PLANES * t // SC_WINDOW,),
            in_specs=[pl.BlockSpec((SC_WINDOW, SC_ROW_W), lambda i: (i, 0)),
                      pl.BlockSpec((TOP_K, SC_WINDOW), lambda i: (0, i))],
            out_specs=[],
            core_axis_name=("c", "s"), dimension_semantics=(pltpu.PARALLEL,),
        )(x_hbm, i_hbm)

    return scatter(rows.reshape(N_PLANES * t, SC_ROW_W), idx).reshape(N_PLANES, n_out, SC_ROW_W)


def _zero_pad_kernel(cnt_ref, start_ref, xb_in_ref, xb_ref, zero_scr, sem):
    del xb_in_ref
    zero_scr[...] = jnp.zeros_like(zero_scr)

    pieces = [SUBLANES << bit for bit in range((MOE_ROWS // SUBLANES - 1).bit_length())]

    def zero_copy(p, row, size):
        return pltpu.make_async_copy(zero_scr.at[pl.ds(0, size)], xb_ref.at[p, pl.ds(row, size)], sem)

    def for_each_piece(fn):
        def per_expert(e, carry):
            n_real = cnt_ref[e]
            n_pad = (MOE_ROWS - n_real % MOE_ROWS) % MOE_ROWS
            first = start_ref[e] + n_real
            n_single = n_pad % SUBLANES
            for j in range(SUBLANES - 1):
                @pl.when(j < n_single)
                def _():
                    for p in range(N_PLANES):
                        fn(zero_copy(p, first + j, 1))
            row = first + n_single
            for size in pieces:
                @pl.when((n_pad & size) != 0)
                def _():
                    for p in range(N_PLANES):
                        fn(zero_copy(p, pl.multiple_of(row, SUBLANES), size))
                row = row + (n_pad & size)
            return carry
        lax.fori_loop(0, N_EXPERTS, per_expert, 0)

    for_each_piece(lambda copy: copy.start())
    for_each_piece(lambda copy: copy.wait())


def _zero_pad_slots(counts, pad_start, xb):
    grid_spec = pltpu.PrefetchScalarGridSpec(
        num_scalar_prefetch=2,
        grid=(1,),
        in_specs=[pl.BlockSpec(memory_space=pl.ANY)],
        out_specs=pl.BlockSpec(memory_space=pl.ANY),
        scratch_shapes=[pltpu.VMEM((MOE_ROWS // 2, SC_ROW_W), U32), pltpu.SemaphoreType.DMA],
    )
    return pl.pallas_call(
        _zero_pad_kernel,
        out_shape=jax.ShapeDtypeStruct(xb.shape, xb.dtype),
        grid_spec=grid_spec,
        input_output_aliases={2: 0},
        compiler_params=_params("arbitrary"),
        name="moe_zero_pad",
    )(counts, pad_start, xb)


def kernel(x, c, ctx, c_ctx, w_ada, b_ada, norm_g, w_in, conv_w, a_log, dt_bias, gdn_norm_g,
           sgu_ln_g, sgu_ln_b, sgu_w, sgu_b, w_out, router_w, router_b, w_gu, b_gu, w_down, b_down):
    b, l, d = x.shape
    lc = ctx.shape[1]
    t = b * l
    assert d == D_MODEL and l % ROW_TILE == 0 and l % GDN_BLOCK == 0 and lc % GDN_BLOCK == 0
    assert w_ada.shape[0] == 1, "single-layer block"

    cs = jnp.concatenate([c, c_ctx[None], jnp.zeros((8 - b - 1, d), F32)], axis=0)
    mod_all = _ada(cs, w_ada[0], b_ada[0][None])
    mod = mod_all[:b].reshape(b, 6, d)
    mod_c = mod_all[b:b + 1].reshape(1, 6, d)
    ng = norm_g[0]


    x2d = x.reshape(t, d)
    qkv, z, ug, vg, ba, bat = _inproj(x2d, mod, l, ng[0:1], w_in[0])
    ctx2d = ctx.reshape(b * lc, d)
    qkv_c, _, _, _, ba_c, bat_c = _inproj(ctx2d, mod_c, b * lc, ng[0:1], w_in[0])

    alog = a_log[0].reshape(-1)
    dtb = dt_bias[0].reshape(-1)
    pc = _gdn_prep(qkv_c.reshape(b, lc, QKV_COLS), conv_w[0], ba_c.reshape(b, lc, N_GATE_COLS),
                   bat_c, alog, dtb, lc)
    s_zero = jnp.zeros((b, N_CHAINS, HEAD_DIM, HEAD_DIM), F32)
    _, _, s_ctx = _gdn_scan(*pc, s_zero)
    pp = _gdn_prep(qkv.reshape(b, l, QKV_COLS), conv_w[0], ba.reshape(b, l, N_GATE_COLS),
                   bat, alog, dtb, GRID_W)
    o_f, o_b, _ = _gdn_scan(*pp, s_ctx)

    b_s_full = jnp.broadcast_to(sgu_b[0][:, :, None], (SGU_GROUPS, SGU_CHUNK, SGU_GROUP))
    x1, h2p, logits_t = _outproj(o_f.reshape(t, GDN_W), o_b.reshape(t, GDN_W), z, ug, vg,
                                 gdn_norm_g, sgu_ln_g[0], sgu_ln_b[0], sgu_w[0], b_s_full,
                                 x2d, mod, l, ng, w_out[0].astype(BF16),
                                 router_w[0].T, router_b[0][:, None])

    eidx, gates_t, rank, counts = _route(logits_t)
    counts = counts[:, 0]
    n_blocks = -(-(t * TOP_K) // MOE_ROWS) + N_EXPERTS
    dest, pad_start, block_e, n_used, real_end = _slot_tables(eidx, rank, counts, n_blocks)
    xb = _sc_scatter_rows(h2p, dest, n_blocks * MOE_ROWS)
    xb = _zero_pad_slots(counts, pad_start, xb)
    yb = _moe_ffn(block_e, n_used, real_end, xb, w_gu[0], b_gu[0][:, None, :], w_down[0],
                  b_down[0][:, None, :])
    yg = _sc_gather_rows(yb, dest.reshape(-1))
    out = _combine(yg, gates_t, x1, mod, l, ng)
    return out.reshape(b, l, d)
```

```python
import functools
import math

import jax
import jax.numpy as jnp
from jax import lax
from jax.experimental import pallas as pl
from jax.experimental.pallas import tpu as pltpu
from jax.experimental.pallas import tpu_sc as plsc

F32 = jnp.float32
BF16 = jnp.bfloat16

D_MODEL = 1024
GDN_HEADS = 4
HEAD_DIM = 128
GDN_W = GDN_HEADS * HEAD_DIM
SGU_GROUPS = 4
SGU_GROUP = 128
SGU_W = SGU_GROUPS * SGU_GROUP
SGU_CHUNK = 128
DELTA_CHUNK = 64
GRID_W = 64
N_EXPERTS = 32
TOP_K = 4
D_FF = 1024
SWIGLU_LIMIT = 7.0
SWIGLU_ALPHA = 1.702
NORM_EPS = 1e-6
QKV_COLS = 3 * GDN_W
N_CHAINS = 2 * GDN_HEADS
N_GATE_COLS = 2 * N_CHAINS

ROW_TILE = 1024
SUB_ROWS = 256
COMBINE_TILE = 512
GDN_BLOCK = 256
CHUNKS_PER_BLOCK = GDN_BLOCK // DELTA_CHUNK
N_PAIR_LEVELS = DELTA_CHUNK.bit_length() - 1
MOE_ROWS = 512
U32 = jnp.uint32
LANES = 128
SUBLANES = 8
PACKED_W = D_MODEL // 2
ROUTE_BLOCK = 2048
ROUTE_TILE = 512
SC_CORES = 2
SC_SUBCORES = 16
SC_WINDOW = 128
N_PLANES = 2
SC_ROW_W = PACKED_W // N_PLANES
VMEM_LIMIT = 56 * 1024 * 1024


def _params(*sem):
    return pltpu.CompilerParams(dimension_semantics=sem, vmem_limit_bytes=VMEM_LIMIT)


def _dot(a, b):
    return jnp.dot(a, b, preferred_element_type=F32)


def _dot_nt(a, b):
    return lax.dot_general(a, b, (((1,), (1,)), ((), ())), preferred_element_type=F32)


def _dot_tn(a, b):
    return lax.dot_general(a, b, (((0,), (0,)), ((), ())), preferred_element_type=F32)


def _split2(a):
    hi = a.astype(BF16)
    lo = (a - hi.astype(F32)).astype(BF16)
    return hi, lo


def _split3(a):
    hi = a.astype(BF16)
    r = a - hi.astype(F32)
    mid = r.astype(BF16)
    lo = (r - mid.astype(F32)).astype(BF16)
    return hi, mid, lo


def _pack_rows(x):
    xb = x.astype(BF16).astype(F32)
    hi = lax.bitcast_convert_type(xb[:, :PACKED_W], U32)
    lo = lax.bitcast_convert_type(xb[:, PACKED_W:], U32)
    return hi | (lo >> 16)


def _store_planes(ref, packed, rows=slice(None)):
    for p in range(N_PLANES):
        ref[p, rows] = packed[:, p * SC_ROW_W:(p + 1) * SC_ROW_W]


def _load_planes(ref, rows=slice(None)):
    return jnp.concatenate([ref[p, rows] for p in range(N_PLANES)], axis=1)


def _sub_tiles(n_rows):
    return [slice(r, r + SUB_ROWS) for r in range(0, n_rows, SUB_ROWS)]


def _unpack_rows(w):
    hi = lax.bitcast_convert_type(w & jnp.uint32(0xFFFF0000), F32)
    lo = lax.bitcast_convert_type(w << 16, F32)
    return jnp.concatenate([hi, lo], axis=1)


def _rms(x32, g):
    return x32 * lax.rsqrt(jnp.mean(x32 * x32, -1, keepdims=True) + NORM_EPS) * g


def _gelu_tanh(x):
    c = math.sqrt(2.0 / math.pi)
    return 0.5 * x * (1.0 + jnp.tanh(c * (x + 0.044715 * (x * x * x))))


def _sigmoid(x):
    return 1.0 / (1.0 + jnp.exp(-x))


def _softplus(x):
    return jnp.maximum(x, 0.0) + jnp.log(1.0 + jnp.exp(-jnp.abs(x)))


def _ada_kernel(c_ref, w_ref, b_ref, o_ref):
    c = c_ref[...]
    s = c * _sigmoid(c)
    s_hi, s_lo = _split2(s)
    w_hi, w_lo = _split2(w_ref[...])
    o_ref[...] = _dot(s_hi, w_hi) + _dot(s_lo, w_hi) + _dot(s_hi, w_lo) + b_ref[...]


def _ada(cs, w_ada, b_ada):
    n = w_ada.shape[1]
    bn = D_MODEL
    return pl.pallas_call(
        _ada_kernel,
        out_shape=jax.ShapeDtypeStruct((cs.shape[0], n), F32),
        grid=(n // bn,),
        in_specs=[pl.BlockSpec(cs.shape, lambda j: (0, 0)),
                  pl.BlockSpec((D_MODEL, bn), lambda j: (0, j)),
                  pl.BlockSpec((1, bn), lambda j: (0, j))],
        out_specs=pl.BlockSpec((cs.shape[0], bn), lambda j: (0, j)),
        compiler_params=_params("parallel"),
        name="ada_mod",
    )(cs, w_ada, b_ada)


def _inproj_kernel(x_ref, mod_ref, g_ref, w_hbm,
                   qkv_ref, z_ref, u_ref, v_ref, ba_ref, bat_ref,
                   w_f32, wqkv_ref, wzuv_ref, wba_ref, sem):
    @pl.when(pl.program_id(0) == 0)
    def _():
        copy = pltpu.make_async_copy(w_hbm, w_f32, sem)
        copy.start()
        copy.wait()
        wqkv_ref[...] = w_f32[:, :QKV_COLS].astype(BF16)
        wzuv_ref[...] = w_f32[:, QKV_COLS + N_GATE_COLS:].astype(BF16)
        lane = lax.broadcasted_iota(jnp.int32, (D_MODEL, LANES), 1)
        w_gate = jnp.where(lane < N_GATE_COLS, w_f32[:, QKV_COLS:QKV_COLS + LANES], 0.0)
        w_hi, w_lo = _split2(w_gate)
        wba_ref[:, :LANES] = w_hi
        wba_ref[:, LANES:] = w_lo

    mod = mod_ref[0]
    subs = _sub_tiles(x_ref.shape[0])
    hs = [_split2(_rms(x_ref[sl], g_ref[...]) * (1.0 + mod[1:2]) + mod[0:1]) for sl in subs]
    qkvs = [_dot(h_hi, wqkv_ref[...]) for h_hi, _ in hs]
    zuvs = [_dot(h_hi, wzuv_ref[...]) for h_hi, _ in hs]
    bas = []
    for h_hi, h_lo in hs:
        both = _dot(h_hi, wba_ref[...])
        bas.append(both[:, :LANES] + both[:, LANES:] + _dot(h_lo, wba_ref[:, :LANES]))
    for sl, qkv, zuv, ba in zip(subs, qkvs, zuvs, bas):
        qkv_ref[sl] = qkv.astype(BF16)
        z_ref[sl] = zuv[:, :GDN_W].astype(BF16)
        u_ref[sl] = _gelu_tanh(zuv[:, GDN_W:GDN_W + SGU_W]).astype(BF16)
        v_ref[sl] = _gelu_tanh(zuv[:, GDN_W + SGU_W:]).astype(BF16)
        ba_ref[sl] = ba[:, :N_GATE_COLS]
        bat_ref[:, sl] = ba.T[:N_GATE_COLS]


def _inproj(x2d, mod, rows_per_mod, ng0, w_in):
    t = x2d.shape[0]
    tm = min(ROW_TILE, t)
    tiles_per_mod = rows_per_mod // tm
    row = lambda i: (i, 0)
    const = lambda i: (0, 0)
    return pl.pallas_call(
        _inproj_kernel,
        out_shape=(jax.ShapeDtypeStruct((t, QKV_COLS), BF16),
                   jax.ShapeDtypeStruct((t, GDN_W), BF16),
                   jax.ShapeDtypeStruct((t, SGU_W), BF16),
                   jax.ShapeDtypeStruct((t, SGU_W), BF16),
                   jax.ShapeDtypeStruct((t, N_GATE_COLS), F32),
                   jax.ShapeDtypeStruct((N_GATE_COLS, t), F32)),
        grid=(t // tm,),
        in_specs=[pl.BlockSpec((tm, D_MODEL), row),
                  pl.BlockSpec((1, 6, D_MODEL), lambda i: (i // tiles_per_mod, 0, 0)),
                  pl.BlockSpec((1, D_MODEL), const),
                  pl.BlockSpec(memory_space=pl.ANY)],
        out_specs=(pl.BlockSpec((tm, QKV_COLS), row),
                   pl.BlockSpec((tm, GDN_W), row),
                   pl.BlockSpec((tm, SGU_W), row),
                   pl.BlockSpec((tm, SGU_W), row),
                   pl.BlockSpec((tm, N_GATE_COLS), row),
                   pl.BlockSpec((N_GATE_COLS, tm), lambda i: (0, i))),
        scratch_shapes=[pltpu.VMEM(w_in.shape, F32),
                        pltpu.VMEM((D_MODEL, QKV_COLS), BF16),
                        pltpu.VMEM((D_MODEL, w_in.shape[1] - QKV_COLS - N_GATE_COLS), BF16),
                        pltpu.VMEM((D_MODEL, 2 * LANES), BF16),
                        pltpu.SemaphoreType.DMA],
        compiler_params=_params("arbitrary"),
        name="in_proj",
    )(x2d, mod, ng0, w_in)


def _gdn_prep_kernel(row_len, qkv_ref, cw_ref, ba_ref, bat_ref, alog_r_ref, dtb_r_ref,
                     alog_c_ref, dtb_c_ref, u_ref, w_ref, qd_ref, kd_ref, at_ref, gl_ref,
                     tri_scr, pair_scr, spread_scr):
    n = GDN_BLOCK
    c = DELTA_CHUNK

    def mask01(m):
        return jnp.where(m, 1.0, 0.0).astype(BF16)

    wi = lax.broadcasted_iota(jnp.int32, (c, n), 0)
    wl = lax.broadcasted_iota(jnp.int32, (c, n), 1)
    wchunk = wl // c
    wj = wl % c
    lower_w = wi >= wj
    upper_w = wi <= wj
    diag_w = wi == wj
    eye_w = jnp.where(diag_w, 1.0, 0.0)

    @pl.when(jnp.logical_and(pl.program_id(0) == 0, pl.program_id(1) == 0))
    def _():
        ri = lax.broadcasted_iota(jnp.int32, (n, n), 0)
        ci = lax.broadcasted_iota(jnp.int32, (n, n), 1)
        same = (ri // c) == (ci // c)
        tri_scr[0] = mask01(same & (ri >= ci))
        tri_scr[1] = mask01(same & (ri <= ci))
        tri_scr[2] = mask01(same)
        for m in range(N_PAIR_LEVELS):
            s = 1 << m
            pair_scr[m] = mask01(((wi // (2 * s)) == (wj // (2 * s))) & ((wi // s) != (wj // s)))
        ei = lax.broadcasted_iota(jnp.int32, (n, CHUNKS_PER_BLOCK * HEAD_DIM), 0) // c
        ej = lax.broadcasted_iota(jnp.int32, (n, CHUNKS_PER_BLOCK * HEAD_DIM), 1) // HEAD_DIM
        spread_scr[...] = mask01(ei == ej)

    lower_b = tri_scr[0]
    upper_b = tri_scr[1]
    same_b = tri_scr[2]
    pair_masks = [pair_scr[m] for m in range(N_PAIR_LEVELS)]

    def to_wide(full):
        out = full[:c]
        for k in range(1, CHUNKS_PER_BLOCK):
            out = jnp.where(wchunk == k, full[k * c:(k + 1) * c], out)
        return out

    def col_wide(col):
        out = jnp.broadcast_to(col[:c], (c, n))
        for k in range(1, CHUNKS_PER_BLOCK):
            out = jnp.where(wchunk == k, jnp.broadcast_to(col[k * c:(k + 1) * c], (c, n)), out)
        return out

    def block_diag(x_w):
        return jnp.concatenate([x_w] * CHUNKS_PER_BLOCK, axis=0) * same_b

    ba = ba_ref[0]
    bat = bat_ref[...]
    beta_c = _sigmoid(ba[:, :N_CHAINS])
    g_c = -jnp.exp(alog_r_ref[...]) * _softplus(ba[:, N_CHAINS:] + dtb_r_ref[...])
    g_r = -jnp.exp(alog_c_ref[...]) * _softplus(bat[N_CHAINS:] + dtb_c_ref[...])
    gc3 = _split3(g_c)
    gr3 = jnp.concatenate(_split3(g_r), axis=0)

    def sum3_r(m):
        return m[:N_CHAINS] + m[N_CHAINS:2 * N_CHAINS] + m[2 * N_CHAINS:]

    cum_f_c = _dot(lower_b, gc3[0]) + _dot(lower_b, gc3[1]) + _dot(lower_b, gc3[2])
    tot_c = _dot(same_b, gc3[0]) + _dot(same_b, gc3[1]) + _dot(same_b, gc3[2])
    cum_b_c = tot_c - cum_f_c + g_c
    cum_f_r = sum3_r(_dot(gr3, upper_b))
    cum_b_r = sum3_r(_dot(gr3, lower_b))
    g_last = jnp.exp(sum3_r(_dot(gr3, spread_scr[...])))
    gl_ref[0, 0, 0] = g_last[:GDN_HEADS]
    gl_ref[0, 0, 1] = g_last[GDN_HEADS:]

    pos = lax.broadcasted_iota(jnp.int32, (n, HEAD_DIM), 0) % row_len
    first = pos == 0
    last = pos == row_len - 1

    def conv_silu(col):
        x = qkv_ref[0, :, col * HEAD_DIM:(col + 1) * HEAD_DIM].astype(F32)
        cw = cw_ref[:, col * HEAD_DIM:(col + 1) * HEAD_DIM]
        xp = jnp.where(first, 0.0, pltpu.roll(x, 1, 0))
        xn = jnp.where(last, 0.0, pltpu.roll(x, n - 1, 0))
        y = xp * cw[0:1] + x * cw[1:2] + xn * cw[2:3]
        return y * _sigmoid(y)

    def l2n(x):
        return x * lax.rsqrt(jnp.sum(x * x, -1, keepdims=True) + NORM_EPS)

    a_bs, ps, rhss = [None] * N_CHAINS, [None] * N_CHAINS, [None] * N_CHAINS
    first_pairs = pair_masks[0].astype(F32)
    for h in range(GDN_HEADS):
        q = l2n(conv_silu(h)) * (HEAD_DIM ** -0.5)
        k = l2n(conv_silu(GDN_HEADS + h))
        v = conv_silu(2 * GDN_HEADS + h)
        k_b = k.astype(BF16)
        qk_kk = _dot_nt(jnp.concatenate([q.astype(BF16), k_b], axis=0), k_b)
        qk_w = to_wide(qk_kk[:n])
        kk_w = to_wide(qk_kk[n:])
        for d in range(2):
            j = d * GDN_HEADS + h
            mask_w = lower_w if d == 0 else upper_w
            cum_c = (cum_f_c if d == 0 else cum_b_c)[:, j:j + 1]
            cum_r = (cum_f_r if d == 0 else cum_b_r)[j:j + 1, :]
            b_c = beta_c[:, j:j + 1]
            decay_w = jnp.where(mask_w, jnp.exp(jnp.where(mask_w, col_wide(cum_c) - cum_r, 0.0)), 0.0)
            amat_w = jnp.where(diag_w, 0.0, kk_w * decay_w * col_wide(b_c))
            a_bs[j] = amat_w.astype(BF16)
            ps[j] = eye_w - amat_w * first_pairs
            e_c = jnp.exp(cum_c)
            rhss[j] = jnp.concatenate([(v * b_c).astype(BF16), (k * (b_c * e_c)).astype(BF16)], axis=1)
            cols = slice(j * HEAD_DIM, (j + 1) * HEAD_DIM)
            qd_ref[0, :, cols] = (q * e_c).astype(BF16)
            kd_ref[0, :, cols] = (k * jnp.exp(tot_c[:, j:j + 1] - cum_c)).astype(BF16)
            at_ref[0, 0, j * c:(j + 1) * c, :] = (qk_w * decay_w).astype(BF16)

    for pm in pair_masks[1:]:
        p_bs = [p.astype(BF16) for p in ps]
        ys = [_dot(a_bs[j] * pm, block_diag(p_bs[j])) for j in range(N_CHAINS)]
        ps = [ps[j] - _dot(p_bs[j], block_diag(ys[j].astype(BF16))) for j in range(N_CHAINS)]

    for j in range(N_CHAINS):
        uw = _dot(block_diag(ps[j].astype(BF16)), rhss[j])
        cols = slice(j * HEAD_DIM, (j + 1) * HEAD_DIM)
        u_ref[0, :, cols] = uw[:, :HEAD_DIM].astype(BF16)
        w_ref[0, :, cols] = uw[:, HEAD_DIM:].astype(BF16)


def _gdn_prep(qkv, conv_w, ba, bat, alog, dtb, row_len):
    b, l, _ = qkv.shape
    nblk = l // GDN_BLOCK
    wide = N_CHAINS * HEAD_DIM
    blk = lambda bi, i: (bi, i, 0)
    const = lambda bi, i: (0, 0)
    alog_r, dtb_r = alog.reshape(1, N_CHAINS), dtb.reshape(1, N_CHAINS)
    alog_c, dtb_c = alog.reshape(N_CHAINS, 1), dtb.reshape(N_CHAINS, 1)
    return pl.pallas_call(
        functools.partial(_gdn_prep_kernel, row_len),
        out_shape=(jax.ShapeDtypeStruct((b, l, wide), BF16),) * 4 + (
            jax.ShapeDtypeStruct((b, nblk, N_CHAINS * DELTA_CHUNK, GDN_BLOCK), BF16),
            jax.ShapeDtypeStruct((b, nblk, 2, GDN_HEADS, CHUNKS_PER_BLOCK * HEAD_DIM), F32)),
        grid=(b, nblk),
        in_specs=[pl.BlockSpec((1, GDN_BLOCK, QKV_COLS), blk),
                  pl.BlockSpec((3, QKV_COLS), const),
                  pl.BlockSpec((1, GDN_BLOCK, N_GATE_COLS), blk),
                  pl.BlockSpec((N_GATE_COLS, GDN_BLOCK), lambda bi, i: (0, bi * nblk + i)),
                  pl.BlockSpec((1, N_CHAINS), const),
                  pl.BlockSpec((1, N_CHAINS), const),
                  pl.BlockSpec((N_CHAINS, 1), const),
                  pl.BlockSpec((N_CHAINS, 1), const)],
        out_specs=(pl.BlockSpec((1, GDN_BLOCK, wide), blk),) * 4 + (
            pl.BlockSpec((1, 1, N_CHAINS * DELTA_CHUNK, GDN_BLOCK), lambda bi, i: (bi, i, 0, 0)),
            pl.BlockSpec((1, 1, 2, GDN_HEADS, CHUNKS_PER_BLOCK * HEAD_DIM),
                         lambda bi, i: (bi, i, 0, 0, 0))),
        scratch_shapes=[pltpu.VMEM((3, GDN_BLOCK, GDN_BLOCK), BF16),
                        pltpu.VMEM((N_PAIR_LEVELS, DELTA_CHUNK, GDN_BLOCK), BF16),
                        pltpu.VMEM((GDN_BLOCK, CHUNKS_PER_BLOCK * HEAD_DIM), BF16)],
        compiler_params=_params("arbitrary", "arbitrary"),
        name="gdn_prep",
    )(qkv, conv_w, ba, bat, alog_r, dtb_r, alog_c, dtb_c)


def _gdn_scan_kernel(uf, wf, qf, kf, af, gf, ub, wb, qb, kb, ab, gb, s0_ref,
                     of_ref, ob_ref, sfin_ref, s_scr):
    i = pl.program_id(0)
    c = DELTA_CHUNK
    n_batch = s0_ref.shape[0]

    @pl.when(i == 0)
    def _():
        s_scr[...] = s0_ref[...]

    ops = ((uf, wf, qf, kf, af, gf, of_ref), (ub, wb, qb, kb, ab, gb, ob_ref))
    chains = [(bi, d, h) for bi in range(n_batch) for d in range(2) for h in range(GDN_HEADS)]
    states = [s_scr[bi, d * GDN_HEADS + h] for bi, d, h in chains]
    for step in range(CHUNKS_PER_BLOCK):
        def chunk(d):
            cc = step if d == 0 else CHUNKS_PER_BLOCK - 1 - step
            return cc, slice(cc * c, (cc + 1) * c)

        xs = []
        for j, (bi, d, h) in enumerate(chains):
            _, rows = chunk(d)
            cols = slice(h * HEAD_DIM, (h + 1) * HEAD_DIM)
            wq = jnp.concatenate([ops[d][1][bi, rows, cols], ops[d][2][bi, rows, cols]], axis=0)
            xs.append(_dot(wq, states[j].astype(BF16)))
        v_news = []
        for j, (bi, d, h) in enumerate(chains):
            _, rows = chunk(d)
            cols = slice(h * HEAD_DIM, (h + 1) * HEAD_DIM)
            v_news.append((ops[d][0][bi, rows, cols].astype(F32) - xs[j][:c]).astype(BF16))
        for j, (bi, d, h) in enumerate(chains):
            cc, rows = chunk(d)
            cols = slice(h * HEAD_DIM, (h + 1) * HEAD_DIM)
            a_c = ops[d][4][bi, 0, h * c:(h + 1) * c, cc * c:(cc + 1) * c]
            ops[d][6][bi, rows, cols] = (xs[j][c:] + _dot(a_c, v_news[j])).astype(BF16)
            ds = _dot_tn(ops[d][3][bi, rows, cols], v_news[j])
            g_last = ops[d][5][bi, 0, 0, h:h + 1, cc * HEAD_DIM:(cc + 1) * HEAD_DIM]
            states[j] = states[j] * g_last + ds
    for j, (bi, d, h) in enumerate(chains):
        s_scr[bi, d * GDN_HEADS + h] = states[j]

    @pl.when(i == pl.num_programs(0) - 1)
    def _():
        sfin_ref[...] = s_scr[...]


def _gdn_scan(u, w, qd, kd, at, gl, s0):
    b, l, _ = u.shape
    nblk = l // GDN_BLOCK
    half = GDN_HEADS * HEAD_DIM
    fwd = lambda i: (0, i, 0)
    bwd = lambda i: (0, nblk - 1 - i, 1)
    big = lambda m: pl.BlockSpec((b, GDN_BLOCK, half), m)
    att_shape = (b, 1, GDN_HEADS * DELTA_CHUNK, GDN_BLOCK)
    attf = pl.BlockSpec(att_shape, lambda i: (0, i, 0, 0))
    attb = pl.BlockSpec(att_shape, lambda i: (0, nblk - 1 - i, 1, 0))
    gl_shape = (b, 1, 1, GDN_HEADS, CHUNKS_PER_BLOCK * HEAD_DIM)
    glf = pl.BlockSpec(gl_shape, lambda i: (0, i, 0, 0, 0))
    glb = pl.BlockSpec(gl_shape, lambda i: (0, nblk - 1 - i, 1, 0, 0))
    state = pl.BlockSpec((b, N_CHAINS, HEAD_DIM, HEAD_DIM), lambda i: (0, 0, 0, 0))
    return pl.pallas_call(
        _gdn_scan_kernel,
        out_shape=(jax.ShapeDtypeStruct((b, l, half), BF16),
                   jax.ShapeDtypeStruct((b, l, half), BF16),
                   jax.ShapeDtypeStruct((b, N_CHAINS, HEAD_DIM, HEAD_DIM), F32)),
        grid=(nblk,),
        in_specs=[big(fwd), big(fwd), big(fwd), big(fwd), attf, glf,
                  big(bwd), big(bwd), big(bwd), big(bwd), attb, glb, state],
        out_specs=(pl.BlockSpec((b, GDN_BLOCK, half), fwd),
                   pl.BlockSpec((b, GDN_BLOCK, half), lambda i: (0, nblk - 1 - i, 0)),
                   state),
        scratch_shapes=[pltpu.VMEM((b, N_CHAINS, HEAD_DIM, HEAD_DIM), F32)],
        compiler_params=_params("arbitrary"),
        name="gdn_scan",
    )(u, w, qd, kd, at, gl, u, w, qd, kd, at, gl, s0)


def _mix_into(mix_ref, of_ref, ob_ref, z_ref, u_ref, v_ref, gng_ref, lng_ref, lnb_ref, ws_ref,
              bs_ref):
    tm = of_ref.shape[0]
    o = of_ref[...].astype(F32) + ob_ref[...].astype(F32)
    z = z_ref[...].astype(F32)
    for h in range(GDN_HEADS):
        cols = slice(h * HEAD_DIM, (h + 1) * HEAD_DIM)
        oh = o[:, cols]
        zh = z[:, cols]
        r = lax.rsqrt(jnp.mean(oh * oh, -1, keepdims=True) + NORM_EPS)
        mix_ref[:, cols] = (oh * r * gng_ref[...] * (zh * _sigmoid(zh))).astype(BF16)
    for g in range(SGU_GROUPS):
        cols = slice(g * SGU_GROUP, (g + 1) * SGU_GROUP)
        vg = v_ref[:, cols].astype(F32)
        vc = vg - jnp.mean(vg, -1, keepdims=True)
        vn = vc * lax.rsqrt(jnp.mean(vc * vc, -1, keepdims=True) + NORM_EPS)
        vn = (vn * lng_ref[g:g + 1] + lnb_ref[g:g + 1]).astype(BF16)
        wsg = ws_ref[g].astype(BF16)
        for n in range(tm // SGU_CHUNK):
            rows = slice(n * SGU_CHUNK, (n + 1) * SGU_CHUNK)
            s = _dot(wsg, vn[rows]) + bs_ref[g]
            mix_ref[rows, GDN_W + g * SGU_GROUP:GDN_W + (g + 1) * SGU_GROUP] = (
                u_ref[rows, cols].astype(F32) * s).astype(BF16)


def _outproj_kernel(of_ref, ob_ref, z_ref, u_ref, v_ref, gng_ref, lng_ref, lnb_ref, ws_ref, bs_ref,
                    x_ref, mod_ref, ng_ref, wout_ref, rwt_ref, rb_ref,
                    x1_ref, h2_ref, lt_ref, mix_scr):
    subs = _sub_tiles(x_ref.shape[0])
    for sl in subs:
        _mix_into(mix_scr.at[sl], of_ref.at[sl], ob_ref.at[sl], z_ref.at[sl], u_ref.at[sl],
                  v_ref.at[sl], gng_ref, lng_ref, lnb_ref, ws_ref, bs_ref)
    mod = mod_ref[0]
    ys = [_dot(mix_scr[sl], wout_ref[...]) for sl in subs]
    w_hi, w_lo = _split2(rwt_ref[...])
    for sl, y in zip(subs, ys):
        x1 = x_ref[sl] + mod[2:3] * _rms(y, ng_ref[1:2])
        x1_ref[sl] = x1
        h2 = _rms(x1, ng_ref[2:3]) * (1.0 + mod[4:5]) + mod[3:4]
        h_hi, h_lo = _split2(h2)
        _store_planes(h2_ref, _pack_rows(h_hi), sl)
        lt_ref[:, sl] = (_dot_nt(w_hi, h_hi) + _dot_nt(w_lo, h_hi) + _dot_nt(w_hi, h_lo)
                         + rb_ref[...])


def _outproj(o_f, o_b, z, ug, vg, gdn_norm_g, ln_g, ln_b, w_s, b_s_full,
             x2d, mod, rows_per_mod, ng, wout, rwt, rb_col):
    t = x2d.shape[0]
    tm = min(ROW_TILE, t)
    tiles_per_mod = rows_per_mod // tm
    row = lambda i: (i, 0)
    const = lambda i: (0, 0)
    c3 = lambda i: (0, 0, 0)
    return pl.pallas_call(
        _outproj_kernel,
        out_shape=(jax.ShapeDtypeStruct((t, D_MODEL), F32),
                   jax.ShapeDtypeStruct((N_PLANES, t, SC_ROW_W), U32),
                   jax.ShapeDtypeStruct((N_EXPERTS, t), F32)),
        grid=(t // tm,),
        in_specs=[pl.BlockSpec((tm, GDN_W), row), pl.BlockSpec((tm, GDN_W), row),
                  pl.BlockSpec((tm, GDN_W), row), pl.BlockSpec((tm, SGU_W), row),
                  pl.BlockSpec((tm, SGU_W), row),
                  pl.BlockSpec((1, HEAD_DIM), const),
                  pl.BlockSpec((SGU_GROUPS, SGU_GROUP), const),
                  pl.BlockSpec((SGU_GROUPS, SGU_GROUP), const),
                  pl.BlockSpec((SGU_GROUPS, SGU_CHUNK, SGU_CHUNK), c3),
                  pl.BlockSpec((SGU_GROUPS, SGU_CHUNK, SGU_GROUP), c3),
                  pl.BlockSpec((tm, D_MODEL), row),
                  pl.BlockSpec((1, 6, D_MODEL), lambda i: (i // tiles_per_mod, 0, 0)),
                  pl.BlockSpec((4, D_MODEL), const),
                  pl.BlockSpec((D_MODEL, D_MODEL), const),
                  pl.BlockSpec((N_EXPERTS, D_MODEL), const),
                  pl.BlockSpec((N_EXPERTS, 1), const)],
        out_specs=(pl.BlockSpec((tm, D_MODEL), row),
                   pl.BlockSpec((N_PLANES, tm, SC_ROW_W), lambda i: (0, i, 0)),
                   pl.BlockSpec((N_EXPERTS, tm), lambda i: (0, i))),
        compiler_params=_params("parallel"),
        scratch_shapes=[pltpu.VMEM((tm, D_MODEL), BF16)],
        name="out_proj_router",
    )(o_f, o_b, z, ug, vg, gdn_norm_g, ln_g, ln_b, w_s, b_s_full, x2d, mod, ng, wout, rwt, rb_col)


def _moe_kernel(be_ref, slot_ref, next_ref, short_ref, nb_ref, xb_ref, wgu_hbm, bgu_ref, wd_hbm,
                bd_ref, y_ref, wgu_f, wd_f, wgu_b, wd_b, gut_scr, sems):
    i = pl.program_id(0)
    live = i < nb_ref[0]
    new_expert = jnp.logical_or(i == 0, be_ref[i] != be_ref[jnp.maximum(i - 1, 0)])

    def weight_copies(expert, slot):
        return (pltpu.make_async_copy(wgu_hbm.at[expert], wgu_f.at[slot], sems.at[slot, 0]),
                pltpu.make_async_copy(wd_hbm.at[expert], wd_f.at[slot], sems.at[slot, 1]))

    @pl.when(i == 0)
    def _():
        for copy in weight_copies(be_ref[0], 0):
            copy.start()

    @pl.when(jnp.logical_and(live, new_expert))
    def _():
        slot = slot_ref[i]
        for copy in weight_copies(be_ref[i], slot):
            copy.wait()

        @pl.when(next_ref[i] >= 0)
        def _():
            for copy in weight_copies(next_ref[i], 1 - slot):
                copy.start()
        wgu_b[...] = wgu_f[slot].astype(BF16)
        wd_b[...] = wd_f[slot].astype(BF16)

    def ffn(n_rows):
        rows = slice(0, n_rows)
        xb = _unpack_rows(_load_planes(xb_ref, rows)).astype(BF16)
        gu = _dot(xb, wgu_b[...]) + bgu_ref[0]
        gu_t = gu.T
        acts = []
        for part in range(n_rows // LANES):
            part_ref = gut_scr.at[part]
            part_ref[...] = gu_t[:, part * LANES:(part + 1) * LANES]
            gate = jnp.minimum(part_ref[pl.ds(0, D_FF, stride=2), :], SWIGLU_LIMIT)
            up = jnp.clip(part_ref[pl.ds(1, D_FF, stride=2), :], -SWIGLU_LIMIT, SWIGLU_LIMIT)
            acts.append(((up + 1.0) * gate * _sigmoid(SWIGLU_ALPHA * gate)).astype(BF16))
        act_t = jnp.concatenate(acts, axis=1)
        _store_planes(y_ref, _pack_rows(_dot_tn(act_t, wd_b[...]) + bd_ref[0]), rows)

    short = short_ref[i] != 0

    @pl.when(jnp.logical_and(live, jnp.logical_not(short)))
    def _():
        ffn(MOE_ROWS)

    @pl.when(jnp.logical_and(live, short))
    def _():
        ffn(MOE_ROWS // 2)
        for p in range(N_PLANES):
            y_ref[p, MOE_ROWS // 2:] = jnp.zeros((MOE_ROWS // 2, SC_ROW_W), U32)

    @pl.when(jnp.logical_not(live))
    def _():
        y_ref[...] = jnp.zeros_like(y_ref)


def _moe_ffn(block_e, n_used, real_end, xb, w_gu, b_gu, w_down, b_down):
    n_rows = xb.shape[1]
    n_blocks = n_rows // MOE_ROWS
    idx = jnp.arange(n_blocks, dtype=jnp.int32)
    live = idx < n_used[0]
    changed = jnp.concatenate([jnp.ones((1,), bool), block_e[1:] != block_e[:-1]]) & live
    slot = ((jnp.cumsum(changed.astype(jnp.int32)) - 1) % 2).astype(jnp.int32)
    change_at = jnp.where(changed, idx, n_blocks)
    next_change = lax.cummin(jnp.concatenate([change_at[1:], jnp.full((1,), n_blocks, jnp.int32)]),
                             reverse=True)
    next_e = jnp.where(next_change < n_blocks,
                       block_e[jnp.minimum(next_change, n_blocks - 1)], -1).astype(jnp.int32)
    n_real = jnp.clip(real_end[block_e] - idx * MOE_ROWS, 0, MOE_ROWS)
    short = (n_real <= MOE_ROWS // 2).astype(jnp.int32)

    row = lambda i, be, sl, nx, sh, nb: (0, i, 0)
    ex3 = lambda i, be, sl, nx, sh, nb: (be[i], 0, 0)
    live_row = lambda i, be, sl, nx, sh, nb: (0, jnp.minimum(i, nb[0] - 1), 0)
    planes = (N_PLANES, MOE_ROWS, SC_ROW_W)
    grid_spec = pltpu.PrefetchScalarGridSpec(
        num_scalar_prefetch=5,
        grid=(n_blocks,),
        in_specs=[pl.BlockSpec(planes, live_row),
                  pl.BlockSpec(memory_space=pl.ANY),
                  pl.BlockSpec((1, 1, 2 * D_FF), ex3),
                  pl.BlockSpec(memory_space=pl.ANY),
                  pl.BlockSpec((1, 1, D_MODEL), ex3)],
        out_specs=pl.BlockSpec(planes, row),
        scratch_shapes=[pltpu.VMEM((2, D_MODEL, 2 * D_FF), F32),
                        pltpu.VMEM((2, D_FF, D_MODEL), F32),
                        pltpu.VMEM((D_MODEL, 2 * D_FF), BF16),
                        pltpu.VMEM((D_FF, D_MODEL), BF16),
                        pltpu.VMEM((MOE_ROWS // LANES, 2 * D_FF, LANES), F32),
                        pltpu.SemaphoreType.DMA((2, 2))],
    )
    return pl.pallas_call(
        _moe_kernel,
        out_shape=jax.ShapeDtypeStruct((N_PLANES, n_rows, SC_ROW_W), U32),
        grid_spec=grid_spec,
        compiler_params=_params("arbitrary"),
        name="moe_ffn",
    )(block_e, slot, next_e, short, n_used, xb, w_gu, b_gu, w_down, b_down)


def _combine_kernel(y0_ref, y1_ref, y2_ref, y3_ref, gt_ref, x1_ref, mod_ref, ng_ref, o_ref):
    mod = mod_ref[0]
    tm = o_ref.shape[0]
    gt = jnp.concatenate([gt_ref[...], jnp.zeros((LANES - SUBLANES, tm), F32)], axis=0).T
    y = _unpack_rows(_load_planes(y0_ref)) * gt[:, 0:1]
    for k, y_ref in ((1, y1_ref), (2, y2_ref), (3, y3_ref)):
        y = y + _unpack_rows(_load_planes(y_ref)) * gt[:, k:k + 1]
    o_ref[...] = x1_ref[...] + mod[5:6] * _rms(y, ng_ref[3:4])


def _combine(yg, gates, x1, mod, rows_per_mod, ng):
    t = x1.shape[0]
    tm = min(COMBINE_TILE, t)
    tiles_per_mod = rows_per_mod // tm
    n_tiles = t // tm
    row = lambda i: (i, 0)
    choice = lambda k: pl.BlockSpec((N_PLANES, tm, SC_ROW_W), lambda i: (0, k * n_tiles + i, 0))
    return pl.pallas_call(
        _combine_kernel,
        out_shape=jax.ShapeDtypeStruct((t, D_MODEL), F32),
        grid=(n_tiles,),
        in_specs=[choice(0), choice(1), choice(2), choice(3),
                  pl.BlockSpec((SUBLANES, tm), lambda i: (0, i)),
                  pl.BlockSpec((tm, D_MODEL), row),
                  pl.BlockSpec((1, 6, D_MODEL), lambda i: (i // tiles_per_mod, 0, 0)),
                  pl.BlockSpec((4, D_MODEL), lambda i: (0, 0))],
        out_specs=pl.BlockSpec((tm, D_MODEL), row),
        compiler_params=_params("parallel"),
        name="moe_combine",
    )(yg, yg, yg, yg, gates, x1, mod, ng)


def _route_kernel(lt_ref, eidx_ref, gate_ref, rank_ref, cnt_ref, carry, earlier):
    i = pl.program_id(0)
    tile = earlier.shape[0]

    @pl.when(i == 0)
    def _():
        carry[...] = jnp.zeros_like(carry)
        ti = lax.broadcasted_iota(jnp.int32, (tile, tile), 0)
        tj = lax.broadcasted_iota(jnp.int32, (tile, tile), 1)
        earlier[...] = jnp.where(ti < tj, 1.0, 0.0).astype(BF16)

    eio = lax.broadcasted_iota(jnp.int32, (N_EXPERTS, tile), 0).astype(F32)
    seen = carry[...]
    for sub in range(lt_ref.shape[1] // tile):
        cols = slice(sub * tile, (sub + 1) * tile)
        logits = lt_ref[:, cols]
        vals, sels = [], []
        for k in range(TOP_K):
            m = jnp.max(logits, axis=0, keepdims=True)
            idx = jnp.min(jnp.where(logits == m, eio, float(N_EXPERTS)), axis=0, keepdims=True)
            sel = eio == idx
            logits = jnp.where(sel, -jnp.inf, logits)
            vals.append(m)
            sels.append(sel)
            eidx_ref[k:k + 1, cols] = idx.astype(jnp.int32)
        exps = [jnp.exp(v - vals[0]) for v in vals]
        denom = exps[0] + exps[1] + exps[2] + exps[3]
        for k in range(TOP_K):
            gate_ref[k:k + 1, cols] = exps[k] / denom
        gate_ref[TOP_K:, cols] = jnp.zeros((SUBLANES - TOP_K, tile), F32)

        member = jnp.where(sels[0] | sels[1] | sels[2] | sels[3], 1.0, 0.0)
        before = _dot(member.astype(BF16), earlier[...]) + seen
        for k in range(TOP_K):
            rank_ref[k:k + 1, cols] = jnp.sum(jnp.where(sels[k], before, 0.0), axis=0,
                                              keepdims=True).astype(jnp.int32)
        seen = seen + jnp.sum(member, axis=1, keepdims=True)
    carry[...] = seen
    cnt_ref[...] = seen.astype(jnp.int32)


def _route(logits_t):
    t = logits_t.shape[1]
    block = min(ROUTE_BLOCK, t)
    tile = min(ROUTE_TILE, block)
    blk = lambda i: (0, i)
    return pl.pallas_call(
        _route_kernel,
        out_shape=(jax.ShapeDtypeStruct((TOP_K, t), jnp.int32),
                   jax.ShapeDtypeStruct((SUBLANES, t), F32),
                   jax.ShapeDtypeStruct((TOP_K, t), jnp.int32),
                   jax.ShapeDtypeStruct((N_EXPERTS, 1), jnp.int32)),
        grid=(t // block,),
        in_specs=[pl.BlockSpec((N_EXPERTS, block), blk)],
        out_specs=(pl.BlockSpec((TOP_K, block), blk), pl.BlockSpec((SUBLANES, block), blk),
                   pl.BlockSpec((TOP_K, block), blk),
                   pl.BlockSpec((N_EXPERTS, 1), lambda i: (0, 0))),
        scratch_shapes=[pltpu.VMEM((N_EXPERTS, 1), F32), pltpu.VMEM((tile, tile), BF16)],
        compiler_params=_params("arbitrary"),
        name="moe_route",
    )(logits_t)


def _slot_tables(eidx, rank, counts, n_blocks):
    padded = (counts + MOE_ROWS - 1) // MOE_ROWS * MOE_ROWS
    pad_end = jnp.cumsum(padded)
    pad_start = pad_end - padded
    experts = jnp.arange(N_EXPERTS, dtype=jnp.int32)
    dest = rank + jnp.sum(jnp.where(eidx[..., None] == experts, pad_start, 0), axis=-1)
    first_row = jnp.arange(n_blocks, dtype=jnp.int32)[:, None] * MOE_ROWS
    block_e = jnp.minimum(jnp.sum((pad_end[None, :] <= first_row).astype(jnp.int32), axis=1),
                          N_EXPERTS - 1)
    n_used = pad_end[-1:] // MOE_ROWS
    real_end = (pad_start + counts).astype(jnp.int32)
    return (dest.astype(jnp.int32), pad_start.astype(jnp.int32), block_e, n_used.astype(jnp.int32),
            real_end)


def _sc_mesh():
    return plsc.VectorSubcoreMesh(core_axis_name="c", subcore_axis_name="s",
                                  num_cores=SC_CORES, num_subcores=SC_SUBCORES)


def _plane_row_ids(rows, rows_per_plane):
    return jnp.concatenate([rows + p * rows_per_plane for p in range(N_PLANES)], axis=-1)


def _sc_gather_rows(table, rows):
    v = table.shape[1]
    idx = _plane_row_ids(rows, v)[None]
    n_all = idx.shape[1]

    @functools.partial(pl.kernel, mesh=_sc_mesh(), name="moe_gather_rows",
                       out_type=jax.ShapeDtypeStruct((n_all, SC_ROW_W), U32))
    def gather(x_hbm, i_hbm, o_hbm):
        def body(i_vmem, o_vmem):
            pltpu.sync_copy(x_hbm.at[i_vmem.at[0]], o_vmem)

        pltpu.emit_pipeline(
            body, grid=(n_all // SC_WINDOW,),
            in_specs=[pl.BlockSpec((1, SC_WINDOW), lambda i: (0, i))],
            out_specs=[pl.BlockSpec((SC_WINDOW, SC_ROW_W), lambda i: (i, 0))],
            core_axis_name=("c", "s"), dimension_semantics=(pltpu.PARALLEL,),
        )(i_hbm, o_hbm)

    return gather(table.reshape(N_PLANES * v, SC_ROW_W), idx).reshape(N_PLANES, -1, SC_ROW_W)


def _sc_scatter_rows(rows, dest, n_out):
    t = rows.shape[1]
    idx = _plane_row_ids(dest, n_out)

    @functools.partial(pl.kernel, mesh=_sc_mesh(), name="moe_scatter_rows", scratch_types=[],
                       out_type=jax.ShapeDtypeStruct((N_PLANES * n_out, SC_ROW_W), U32))
    def scatter(x_hbm, i_hbm, o_hbm):
        def body(x_vmem, i_vmem):
            for k in range(TOP_K):
                pltpu.sync_copy(x_vmem, o_hbm.at[i_vmem.at[k]])

        pltpu.emit_pipeline(
            body, grid=(N_PLANES * t // SC_WINDOW,),
            in_specs=[pl.BlockSpec((SC_WINDOW, SC_ROW_W), lambda i: (i, 0)),
                      pl.BlockSpec((TOP_K, SC_WINDOW), lambda i: (0, i))],
            out_specs=[],
            core_axis_name=("c", "s"), dimension_semantics=(pltpu.PARALLEL,),
        )(x_hbm, i_hbm)

    return scatter(rows.reshape(N_PLANES * t, SC_ROW_W), idx).reshape(N_PLANES, n_out, SC_ROW_W)


def _zero_pad_kernel(cnt_ref, start_ref, xb_in_ref, xb_ref, zero_scr, sem):
    del xb_in_ref
    zero_scr[...] = jnp.zeros_like(zero_scr)

    pieces = [SUBLANES << bit for bit in range((MOE_ROWS // SUBLANES - 1).bit_length())]

    def zero_copy(p, row, size):
        return pltpu.make_async_copy(zero_scr.at[pl.ds(0, size)], xb_ref.at[p, pl.ds(row, size)], sem)

    def for_each_piece(fn):
        def per_expert(e, carry):
            n_real = cnt_ref[e]
            n_pad = (MOE_ROWS - n_real % MOE_ROWS) % MOE_ROWS
            first = start_ref[e] + n_real
            n_single = n_pad % SUBLANES
            for j in range(SUBLANES - 1):
                @pl.when(j < n_single)
                def _():
                    for p in range(N_PLANES):
                        fn(zero_copy(p, first + j, 1))
            row = first + n_single
            for size in pieces:
                @pl.when((n_pad & size) != 0)
                def _():
                    for p in range(N_PLANES):
                        fn(zero_copy(p, pl.multiple_of(row, SUBLANES), size))
                row = row + (n_pad & size)
            return carry
        lax.fori_loop(0, N_EXPERTS, per_expert, 0)

    for_each_piece(lambda copy: copy.start())
    for_each_piece(lambda copy: copy.wait())


def _zero_pad_slots(counts, pad_start, xb):
    grid_spec = pltpu.PrefetchScalarGridSpec(
        num_scalar_prefetch=2,
        grid=(1,),
        in_specs=[pl.BlockSpec(memory_space=pl.ANY)],
        out_specs=pl.BlockSpec(memory_space=pl.ANY),
        scratch_shapes=[pltpu.VMEM((MOE_ROWS // 2, SC_ROW_W), U32), pltpu.SemaphoreType.DMA],
    )
    return pl.pallas_call(
        _zero_pad_kernel,
        out_shape=jax.ShapeDtypeStruct(xb.shape, xb.dtype),
        grid_spec=grid_spec,
        input_output_aliases={2: 0},
        compiler_params=_params("arbitrary"),
        name="moe_zero_pad",
    )(counts, pad_start, xb)


def kernel(x, c, ctx, c_ctx, w_ada, b_ada, norm_g, w_in, conv_w, a_log, dt_bias, gdn_norm_g,
           sgu_ln_g, sgu_ln_b, sgu_w, sgu_b, w_out, router_w, router_b, w_gu, b_gu, w_down, b_down):
    b, l, d = x.shape
    lc = ctx.shape[1]
    t = b * l
    assert d == D_MODEL and l % ROW_TILE == 0 and l % GDN_BLOCK == 0 and lc % GDN_BLOCK == 0
    assert w_ada.shape[0] == 1, "single-layer block"

    cs = jnp.concatenate([c, c_ctx[None], jnp.zeros((8 - b - 1, d), F32)], axis=0)
    mod_all = _ada(cs, w_ada[0], b_ada[0][None])
    mod = mod_all[:b].reshape(b, 6, d)
    mod_c = mod_all[b:b + 1].reshape(1, 6, d)
    ng = norm_g[0]


    x2d = x.reshape(t, d)
    qkv, z, ug, vg, ba, bat = _inproj(x2d, mod, l, ng[0:1], w_in[0])
    ctx2d = ctx.reshape(b * lc, d)
    qkv_c, _, _, _, ba_c, bat_c = _inproj(ctx2d, mod_c, b * lc, ng[0:1], w_in[0])

    alog = a_log[0].reshape(-1)
    dtb = dt_bias[0].reshape(-1)
    pc = _gdn_prep(qkv_c.reshape(b, lc, QKV_COLS), conv_w[0], ba_c.reshape(b, lc, N_GATE_COLS),
                   bat_c, alog, dtb, lc)
    s_zero = jnp.zeros((b, N_CHAINS, HEAD_DIM, HEAD_DIM), F32)
    _, _, s_ctx = _gdn_scan(*pc, s_zero)
    pp = _gdn_prep(qkv.reshape(b, l, QKV_COLS), conv_w[0], ba.reshape(b, l, N_GATE_COLS),
                   bat, alog, dtb, GRID_W)
    o_f, o_b, _ = _gdn_scan(*pp, s_ctx)

    b_s_full = jnp.broadcast_to(sgu_b[0][:, :, None], (SGU_GROUPS, SGU_CHUNK, SGU_GROUP))
    x1, h2p, logits_t = _outproj(o_f.reshape(t, GDN_W), o_b.reshape(t, GDN_W), z, ug, vg,
                                 gdn_norm_g, sgu_ln_g[0], sgu_ln_b[0], sgu_w[0], b_s_full,
                                 x2d, mod, l, ng, w_out[0].astype(BF16),
                                 router_w[0].T, router_b[0][:, None])

    eidx, gates_t, rank, counts = _route(logits_t)
    counts = counts[:, 0]
    n_blocks = -(-(t * TOP_K) // MOE_ROWS) + N_EXPERTS
    dest, pad_start, block_e, n_used, real_end = _slot_tables(eidx, rank, counts, n_blocks)
    xb = _sc_scatter_rows(h2p, dest, n_blocks * MOE_ROWS)
    xb = _zero_pad_slots(counts, pad_start, xb)
    yb = _moe_ffn(block_e, n_used, real_end, xb, w_gu[0], b_gu[0][:, None, :], w_down[0],
                  b_down[0][:, None, :])
    yg = _sc_gather_rows(yb, dest.reshape(-1))
    out = _combine(yg, gates_t, x1, mod, l, ng)
    return out.reshape(b, l, d)
```

```python
import functools
import math

import jax
import jax.numpy as jnp
from jax import lax
from jax.experimental import pallas as pl
from jax.experimental.pallas import tpu as pltpu
from jax.experimental.pallas import tpu_sc as plsc

F32 = jnp.float32
BF16 = jnp.bfloat16

D_MODEL = 1024
GDN_HEADS = 4
HEAD_DIM = 128
GDN_W = GDN_HEADS * HEAD_DIM
SGU_GROUPS = 4
SGU_GROUP = 128
SGU_W = SGU_GROUPS * SGU_GROUP
SGU_CHUNK = 128
DELTA_CHUNK = 64
GRID_W = 64
N_EXPERTS = 32
TOP_K = 4
D_FF = 1024
SWIGLU_LIMIT = 7.0
SWIGLU_ALPHA = 1.702
NORM_EPS = 1e-6
QKV_COLS = 3 * GDN_W
N_CHAINS = 2 * GDN_HEADS
N_GATE_COLS = 2 * N_CHAINS

ROW_TILE = 1024
SUB_ROWS = 256
COMBINE_TILE = 1024
ADA_COLS = 3 * D_MODEL
GDN_BLOCK = 256
CHUNKS_PER_BLOCK = GDN_BLOCK // DELTA_CHUNK
N_PAIR_LEVELS = DELTA_CHUNK.bit_length() - 1
MOE_ROWS = 512
U32 = jnp.uint32
LANES = 128
SUBLANES = 8
PACKED_W = D_MODEL // 2
ROUTE_BLOCK = 2048
ROUTE_TILE = 512
SC_CORES = 2
SC_SUBCORES = 16
SC_WINDOW = 128
N_PLANES = 2
SC_ROW_W = PACKED_W // N_PLANES
VMEM_LIMIT = 56 * 1024 * 1024


def _params(*sem):
    return pltpu.CompilerParams(dimension_semantics=sem, vmem_limit_bytes=VMEM_LIMIT)


def _dot(a, b):
    return jnp.dot(a, b, preferred_element_type=F32)


def _dot_nt(a, b):
    return lax.dot_general(a, b, (((1,), (1,)), ((), ())), preferred_element_type=F32)


def _dot_tn(a, b):
    return lax.dot_general(a, b, (((0,), (0,)), ((), ())), preferred_element_type=F32)


def _split2(a):
    hi = a.astype(BF16)
    lo = (a - hi.astype(F32)).astype(BF16)
    return hi, lo


def _split3(a):
    hi = a.astype(BF16)
    r = a - hi.astype(F32)
    mid = r.astype(BF16)
    lo = (r - mid.astype(F32)).astype(BF16)
    return hi, mid, lo


def _pack_rows(x):
    xb = x.astype(BF16).astype(F32)
    hi = lax.bitcast_convert_type(xb[:, :PACKED_W], U32)
    lo = lax.bitcast_convert_type(xb[:, PACKED_W:], U32)
    return hi | (lo >> 16)


def _store_planes(ref, packed, rows=slice(None)):
    for p in range(N_PLANES):
        ref[p, rows] = packed[:, p * SC_ROW_W:(p + 1) * SC_ROW_W]


def _load_planes(ref, rows=slice(None)):
    return jnp.concatenate([ref[p, rows] for p in range(N_PLANES)], axis=1)


def _sub_tiles(n_rows):
    return [slice(r, r + SUB_ROWS) for r in range(0, n_rows, SUB_ROWS)]


def _unpack_rows(w):
    hi = lax.bitcast_convert_type(w & jnp.uint32(0xFFFF0000), F32)
    lo = lax.bitcast_convert_type(w << 16, F32)
    return jnp.concatenate([hi, lo], axis=1)


def _rms(x32, g):
    return x32 * lax.rsqrt(jnp.mean(x32 * x32, -1, keepdims=True) + NORM_EPS) * g


def _gelu_tanh(x):
    c = math.sqrt(2.0 / math.pi)
    return 0.5 * x * (1.0 + jnp.tanh(c * (x + 0.044715 * (x * x * x))))


def _sigmoid(x):
    return 1.0 / (1.0 + jnp.exp(-x))


def _softplus(x):
    return jnp.maximum(x, 0.0) + jnp.log(1.0 + jnp.exp(-jnp.abs(x)))


def _ada_kernel(c_ref, w_ref, b_ref, o_ref):
    c = c_ref[...]
    s = c * _sigmoid(c)
    s_hi, s_lo = _split2(s)
    w_hi, w_lo = _split2(w_ref[...])
    o_ref[...] = _dot(s_hi, w_hi) + _dot(s_lo, w_hi) + _dot(s_hi, w_lo) + b_ref[...]


def _ada(cs, w_ada, b_ada):
    n = w_ada.shape[1]
    bn = ADA_COLS
    return pl.pallas_call(
        _ada_kernel,
        out_shape=jax.ShapeDtypeStruct((cs.shape[0], n), F32),
        grid=(n // bn,),
        in_specs=[pl.BlockSpec(cs.shape, lambda j: (0, 0)),
                  pl.BlockSpec((D_MODEL, bn), lambda j: (0, j)),
                  pl.BlockSpec((1, bn), lambda j: (0, j))],
        out_specs=pl.BlockSpec((cs.shape[0], bn), lambda j: (0, j)),
        compiler_params=_params("parallel"),
        name="ada_mod",
    )(cs, w_ada, b_ada)


def _inproj_kernel(x_ref, mod_ref, g_ref, w_hbm,
                   qkv_ref, z_ref, u_ref, v_ref, ba_ref, bat_ref,
                   w_f32, wqkv_ref, wzuv_ref, wba_ref, sem):
    @pl.when(pl.program_id(0) == 0)
    def _():
        copy = pltpu.make_async_copy(w_hbm, w_f32, sem)
        copy.start()
        copy.wait()
        wqkv_ref[...] = w_f32[:, :QKV_COLS].astype(BF16)
        wzuv_ref[...] = w_f32[:, QKV_COLS + N_GATE_COLS:].astype(BF16)
        lane = lax.broadcasted_iota(jnp.int32, (D_MODEL, LANES), 1)
        w_gate = jnp.where(lane < N_GATE_COLS, w_f32[:, QKV_COLS:QKV_COLS + LANES], 0.0)
        w_hi, w_lo = _split2(w_gate)
        wba_ref[:, :LANES] = w_hi
        wba_ref[:, LANES:] = w_lo

    mod = mod_ref[0]
    subs = _sub_tiles(x_ref.shape[0])
    hs = [_split2(_rms(x_ref[sl], g_ref[...]) * (1.0 + mod[1:2]) + mod[0:1]) for sl in subs]
    qkvs = [_dot(h_hi, wqkv_ref[...]) for h_hi, _ in hs]
    zuvs = [_dot(h_hi, wzuv_ref[...]) for h_hi, _ in hs]
    bas = []
    for h_hi, h_lo in hs:
        both = _dot(h_hi, wba_ref[...])
        bas.append(both[:, :LANES] + both[:, LANES:] + _dot(h_lo, wba_ref[:, :LANES]))
    for sl, qkv, zuv, ba in zip(subs, qkvs, zuvs, bas):
        qkv_ref[sl] = qkv.astype(BF16)
        z_ref[sl] = zuv[:, :GDN_W].astype(BF16)
        u_ref[sl] = _gelu_tanh(zuv[:, GDN_W:GDN_W + SGU_W]).astype(BF16)
        v_ref[sl] = _gelu_tanh(zuv[:, GDN_W + SGU_W:]).astype(BF16)
        ba_ref[sl] = ba[:, :N_GATE_COLS]
        bat_ref[:, sl] = ba.T[:N_GATE_COLS]


def _inproj(x2d, mod, rows_per_mod, ng0, w_in):
    t = x2d.shape[0]
    tm = min(ROW_TILE, t)
    tiles_per_mod = rows_per_mod // tm
    row = lambda i: (i, 0)
    const = lambda i: (0, 0)
    return pl.pallas_call(
        _inproj_kernel,
        out_shape=(jax.ShapeDtypeStruct((t, QKV_COLS), BF16),
                   jax.ShapeDtypeStruct((t, GDN_W), BF16),
                   jax.ShapeDtypeStruct((t, SGU_W), BF16),
                   jax.ShapeDtypeStruct((t, SGU_W), BF16),
                   jax.ShapeDtypeStruct((t, N_GATE_COLS), F32),
                   jax.ShapeDtypeStruct((N_GATE_COLS, t), F32)),
        grid=(t // tm,),
        in_specs=[pl.BlockSpec((tm, D_MODEL), row),
                  pl.BlockSpec((1, 6, D_MODEL), lambda i: (i // tiles_per_mod, 0, 0)),
                  pl.BlockSpec((1, D_MODEL), const),
                  pl.BlockSpec(memory_space=pl.ANY)],
        out_specs=(pl.BlockSpec((tm, QKV_COLS), row),
                   pl.BlockSpec((tm, GDN_W), row),
                   pl.BlockSpec((tm, SGU_W), row),
                   pl.BlockSpec((tm, SGU_W), row),
                   pl.BlockSpec((tm, N_GATE_COLS), row),
                   pl.BlockSpec((N_GATE_COLS, tm), lambda i: (0, i))),
        scratch_shapes=[pltpu.VMEM(w_in.shape, F32),
                        pltpu.VMEM((D_MODEL, QKV_COLS), BF16),
                        pltpu.VMEM((D_MODEL, w_in.shape[1] - QKV_COLS - N_GATE_COLS), BF16),
                        pltpu.VMEM((D_MODEL, 2 * LANES), BF16),
                        pltpu.SemaphoreType.DMA],
        compiler_params=_params("arbitrary"),
        name="in_proj",
    )(x2d, mod, ng0, w_in)


def _gdn_prep_kernel(row_len, qkv_ref, cw_ref, ba_ref, bat_ref, alog_r_ref, dtb_r_ref,
                     alog_c_ref, dtb_c_ref, u_ref, w_ref, qd_ref, kd_ref, at_ref, gl_ref,
                     tri_scr, pair_scr, spread_scr):
    n = GDN_BLOCK
    c = DELTA_CHUNK

    def mask01(m):
        return jnp.where(m, 1.0, 0.0).astype(BF16)

    wi = lax.broadcasted_iota(jnp.int32, (c, n), 0)
    wl = lax.broadcasted_iota(jnp.int32, (c, n), 1)
    wchunk = wl // c
    wj = wl % c
    lower_w = wi >= wj
    upper_w = wi <= wj
    diag_w = wi == wj
    eye_w = jnp.where(diag_w, 1.0, 0.0)

    @pl.when(jnp.logical_and(pl.program_id(0) == 0, pl.program_id(1) == 0))
    def _():
        ri = lax.broadcasted_iota(jnp.int32, (n, n), 0)
        ci = lax.broadcasted_iota(jnp.int32, (n, n), 1)
        same = (ri // c) == (ci // c)
        tri_scr[0] = mask01(same & (ri >= ci))
        tri_scr[1] = mask01(same & (ri <= ci))
        tri_scr[2] = mask01(same)
        for m in range(N_PAIR_LEVELS):
            s = 1 << m
            pair_scr[m] = mask01(((wi // (2 * s)) == (wj // (2 * s))) & ((wi // s) != (wj // s)))
        ei = lax.broadcasted_iota(jnp.int32, (n, CHUNKS_PER_BLOCK * HEAD_DIM), 0) // c
        ej = lax.broadcasted_iota(jnp.int32, (n, CHUNKS_PER_BLOCK * HEAD_DIM), 1) // HEAD_DIM
        spread_scr[...] = mask01(ei == ej)

    lower_b = tri_scr[0]
    upper_b = tri_scr[1]
    same_b = tri_scr[2]
    pair_masks = [pair_scr[m] for m in range(N_PAIR_LEVELS)]

    def to_wide(full):
        out = full[:c]
        for k in range(1, CHUNKS_PER_BLOCK):
            out = jnp.where(wchunk == k, full[k * c:(k + 1) * c], out)
        return out

    def col_wide(col):
        out = jnp.broadcast_to(col[:c], (c, n))
        for k in range(1, CHUNKS_PER_BLOCK):
            out = jnp.where(wchunk == k, jnp.broadcast_to(col[k * c:(k + 1) * c], (c, n)), out)
        return out

    def block_diag(x_w):
        return jnp.concatenate([x_w] * CHUNKS_PER_BLOCK, axis=0) * same_b

    ba = ba_ref[0]
    bat = bat_ref[...]
    beta_c = _sigmoid(ba[:, :N_CHAINS])
    g_c = -jnp.exp(alog_r_ref[...]) * _softplus(ba[:, N_CHAINS:] + dtb_r_ref[...])
    g_r = -jnp.exp(alog_c_ref[...]) * _softplus(bat[N_CHAINS:] + dtb_c_ref[...])
    gc3 = _split3(g_c)
    gr3 = jnp.concatenate(_split3(g_r), axis=0)

    def sum3_r(m):
        return m[:N_CHAINS] + m[N_CHAINS:2 * N_CHAINS] + m[2 * N_CHAINS:]

    cum_f_c = _dot(lower_b, gc3[0]) + _dot(lower_b, gc3[1]) + _dot(lower_b, gc3[2])
    tot_c = _dot(same_b, gc3[0]) + _dot(same_b, gc3[1]) + _dot(same_b, gc3[2])
    cum_b_c = tot_c - cum_f_c + g_c
    cum_f_r = sum3_r(_dot(gr3, upper_b))
    cum_b_r = sum3_r(_dot(gr3, lower_b))
    g_last = jnp.exp(sum3_r(_dot(gr3, spread_scr[...])))
    gl_ref[0, 0, 0] = g_last[:GDN_HEADS]
    gl_ref[0, 0, 1] = g_last[GDN_HEADS:]

    pos = lax.broadcasted_iota(jnp.int32, (n, HEAD_DIM), 0) % row_len
    first = pos == 0
    last = pos == row_len - 1

    def conv_silu(col):
        x = qkv_ref[0, :, col * HEAD_DIM:(col + 1) * HEAD_DIM].astype(F32)
        cw = cw_ref[:, col * HEAD_DIM:(col + 1) * HEAD_DIM]
        xp = jnp.where(first, 0.0, pltpu.roll(x, 1, 0))
        xn = jnp.where(last, 0.0, pltpu.roll(x, n - 1, 0))
        y = xp * cw[0:1] + x * cw[1:2] + xn * cw[2:3]
        return y * _sigmoid(y)

    def l2n(x):
        return x * lax.rsqrt(jnp.sum(x * x, -1, keepdims=True) + NORM_EPS)

    a_bs, ps, rhss = [None] * N_CHAINS, [None] * N_CHAINS, [None] * N_CHAINS
    first_pairs = pair_masks[0].astype(F32)
    for h in range(GDN_HEADS):
        q = l2n(conv_silu(h)) * (HEAD_DIM ** -0.5)
        k = l2n(conv_silu(GDN_HEADS + h))
        v = conv_silu(2 * GDN_HEADS + h)
        k_b = k.astype(BF16)
        qk_kk = _dot_nt(jnp.concatenate([q.astype(BF16), k_b], axis=0), k_b)
        qk_w = to_wide(qk_kk[:n])
        kk_w = to_wide(qk_kk[n:])
        for d in range(2):
            j = d * GDN_HEADS + h
            mask_w = lower_w if d == 0 else upper_w
            cum_c = (cum_f_c if d == 0 else cum_b_c)[:, j:j + 1]
            cum_r = (cum_f_r if d == 0 else cum_b_r)[j:j + 1, :]
            b_c = beta_c[:, j:j + 1]
            decay_w = jnp.where(mask_w, jnp.exp(jnp.where(mask_w, col_wide(cum_c) - cum_r, 0.0)), 0.0)
            amat_w = jnp.where(diag_w, 0.0, kk_w * decay_w * col_wide(b_c))
            a_bs[j] = amat_w.astype(BF16)
            ps[j] = eye_w - amat_w * first_pairs
            e_c = jnp.exp(cum_c)
            rhss[j] = jnp.concatenate([(v * b_c).astype(BF16), (k * (b_c * e_c)).astype(BF16)], axis=1)
            cols = slice(j * HEAD_DIM, (j + 1) * HEAD_DIM)
            qd_ref[0, :, cols] = (q * e_c).astype(BF16)
            kd_ref[0, :, cols] = (k * jnp.exp(tot_c[:, j:j + 1] - cum_c)).astype(BF16)
            at_ref[0, 0, j * c:(j + 1) * c, :] = (qk_w * decay_w).astype(BF16)

    for pm in pair_masks[1:]:
        p_bs = [p.astype(BF16) for p in ps]
        ys = [_dot(a_bs[j] * pm, block_diag(p_bs[j])) for j in range(N_CHAINS)]
        ps = [ps[j] - _dot(p_bs[j], block_diag(ys[j].astype(BF16))) for j in range(N_CHAINS)]

    for j in range(N_CHAINS):
        uw = _dot(block_diag(ps[j].astype(BF16)), rhss[j])
        cols = slice(j * HEAD_DIM, (j + 1) * HEAD_DIM)
        u_ref[0, :, cols] = uw[:, :HEAD_DIM].astype(BF16)
        w_ref[0, :, cols] = uw[:, HEAD_DIM:].astype(BF16)


def _gdn_prep(qkv, conv_w, ba, bat, alog, dtb, row_len):
    b, l, _ = qkv.shape
    nblk = l // GDN_BLOCK
    wide = N_CHAINS * HEAD_DIM
    blk = lambda bi, i: (bi, i, 0)
    const = lambda bi, i: (0, 0)
    alog_r, dtb_r = alog.reshape(1, N_CHAINS), dtb.reshape(1, N_CHAINS)
    alog_c, dtb_c = alog.reshape(N_CHAINS, 1), dtb.reshape(N_CHAINS, 1)
    return pl.pallas_call(
        functools.partial(_gdn_prep_kernel, row_len),
        out_shape=(jax.ShapeDtypeStruct((b, l, wide), BF16),) * 4 + (
            jax.ShapeDtypeStruct((b, nblk, N_CHAINS * DELTA_CHUNK, GDN_BLOCK), BF16),
            jax.ShapeDtypeStruct((b, nblk, 2, GDN_HEADS, CHUNKS_PER_BLOCK * HEAD_DIM), F32)),
        grid=(b, nblk),
        in_specs=[pl.BlockSpec((1, GDN_BLOCK, QKV_COLS), blk),
                  pl.BlockSpec((3, QKV_COLS), const),
                  pl.BlockSpec((1, GDN_BLOCK, N_GATE_COLS), blk),
                  pl.BlockSpec((N_GATE_COLS, GDN_BLOCK), lambda bi, i: (0, bi * nblk + i)),
                  pl.BlockSpec((1, N_CHAINS), const),
                  pl.BlockSpec((1, N_CHAINS), const),
                  pl.BlockSpec((N_CHAINS, 1), const),
                  pl.BlockSpec((N_CHAINS, 1), const)],
        out_specs=(pl.BlockSpec((1, GDN_BLOCK, wide), blk),) * 4 + (
            pl.BlockSpec((1, 1, N_CHAINS * DELTA_CHUNK, GDN_BLOCK), lambda bi, i: (bi, i, 0, 0)),
            pl.BlockSpec((1, 1, 2, GDN_HEADS, CHUNKS_PER_BLOCK * HEAD_DIM),
                         lambda bi, i: (bi, i, 0, 0, 0))),
        scratch_shapes=[pltpu.VMEM((3, GDN_BLOCK, GDN_BLOCK), BF16),
                        pltpu.VMEM((N_PAIR_LEVELS, DELTA_CHUNK, GDN_BLOCK), BF16),
                        pltpu.VMEM((GDN_BLOCK, CHUNKS_PER_BLOCK * HEAD_DIM), BF16)],
        compiler_params=_params("arbitrary", "arbitrary"),
        name="gdn_prep",
    )(qkv, conv_w, ba, bat, alog_r, dtb_r, alog_c, dtb_c)


def _gdn_scan_kernel(uf, wf, qf, kf, af, gf, ub, wb, qb, kb, ab, gb, s0_ref,
                     of_ref, ob_ref, sfin_ref, s_scr):
    i = pl.program_id(0)
    c = DELTA_CHUNK
    n_batch = s0_ref.shape[0]

    @pl.when(i == 0)
    def _():
        s_scr[...] = s0_ref[...]

    ops = ((uf, wf, qf, kf, af, gf, of_ref), (ub, wb, qb, kb, ab, gb, ob_ref))
    chains = [(bi, d, h) for bi in range(n_batch) for d in range(2) for h in range(GDN_HEADS)]
    states = [s_scr[bi, d * GDN_HEADS + h] for bi, d, h in chains]
    for step in range(CHUNKS_PER_BLOCK):
        def chunk(d):
            cc = step if d == 0 else CHUNKS_PER_BLOCK - 1 - step
            return cc, slice(cc * c, (cc + 1) * c)

        xs = []
        for j, (bi, d, h) in enumerate(chains):
            _, rows = chunk(d)
            cols = slice(h * HEAD_DIM, (h + 1) * HEAD_DIM)
            wq = jnp.concatenate([ops[d][1][bi, rows, cols], ops[d][2][bi, rows, cols]], axis=0)
            xs.append(_dot(wq, states[j].astype(BF16)))
        v_news = []
        for j, (bi, d, h) in enumerate(chains):
            _, rows = chunk(d)
            cols = slice(h * HEAD_DIM, (h + 1) * HEAD_DIM)
            v_news.append((ops[d][0][bi, rows, cols].astype(F32) - xs[j][:c]).astype(BF16))
        for j, (bi, d, h) in enumerate(chains):
            cc, rows = chunk(d)
            cols = slice(h * HEAD_DIM, (h + 1) * HEAD_DIM)
            a_c = ops[d][4][bi, 0, h * c:(h + 1) * c, cc * c:(cc + 1) * c]
            ops[d][6][bi, rows, cols] = (xs[j][c:] + _dot(a_c, v_news[j])).astype(BF16)
            ds = _dot_tn(ops[d][3][bi, rows, cols], v_news[j])
            g_last = ops[d][5][bi, 0, 0, h:h + 1, cc * HEAD_DIM:(cc + 1) * HEAD_DIM]
            states[j] = states[j] * g_last + ds
    for j, (bi, d, h) in enumerate(chains):
        s_scr[bi, d * GDN_HEADS + h] = states[j]

    @pl.when(i == pl.num_programs(0) - 1)
    def _():
        sfin_ref[...] = s_scr[...]


def _gdn_scan(u, w, qd, kd, at, gl, s0):
    b, l, _ = u.shape
    nblk = l // GDN_BLOCK
    half = GDN_HEADS * HEAD_DIM
    fwd = lambda i: (0, i, 0)
    bwd = lambda i: (0, nblk - 1 - i, 1)
    big = lambda m: pl.BlockSpec((b, GDN_BLOCK, half), m)
    att_shape = (b, 1, GDN_HEADS * DELTA_CHUNK, GDN_BLOCK)
    attf = pl.BlockSpec(att_shape, lambda i: (0, i, 0, 0))
    attb = pl.BlockSpec(att_shape, lambda i: (0, nblk - 1 - i, 1, 0))
    gl_shape = (b, 1, 1, GDN_HEADS, CHUNKS_PER_BLOCK * HEAD_DIM)
    glf = pl.BlockSpec(gl_shape, lambda i: (0, i, 0, 0, 0))
    glb = pl.BlockSpec(gl_shape, lambda i: (0, nblk - 1 - i, 1, 0, 0))
    state = pl.BlockSpec((b, N_CHAINS, HEAD_DIM, HEAD_DIM), lambda i: (0, 0, 0, 0))
    return pl.pallas_call(
        _gdn_scan_kernel,
        out_shape=(jax.ShapeDtypeStruct((b, l, half), BF16),
                   jax.ShapeDtypeStruct((b, l, half), BF16),
                   jax.ShapeDtypeStruct((b, N_CHAINS, HEAD_DIM, HEAD_DIM), F32)),
        grid=(nblk,),
        in_specs=[big(fwd), big(fwd), big(fwd), big(fwd), attf, glf,
                  big(bwd), big(bwd), big(bwd), big(bwd), attb, glb, state],
        out_specs=(pl.BlockSpec((b, GDN_BLOCK, half), fwd),
                   pl.BlockSpec((b, GDN_BLOCK, half), lambda i: (0, nblk - 1 - i, 0)),
                   state),
        scratch_shapes=[pltpu.VMEM((b, N_CHAINS, HEAD_DIM, HEAD_DIM), F32)],
        compiler_params=_params("arbitrary"),
        name="gdn_scan",
    )(u, w, qd, kd, at, gl, u, w, qd, kd, at, gl, s0)


def _mix_into(mix_ref, of_ref, ob_ref, z_ref, u_ref, v_ref, gng_ref, lng_ref, lnb_ref, ws_ref,
              bs_ref):
    tm = of_ref.shape[0]
    o = of_ref[...].astype(F32) + ob_ref[...].astype(F32)
    z = z_ref[...].astype(F32)
    for h in range(GDN_HEADS):
        cols = slice(h * HEAD_DIM, (h + 1) * HEAD_DIM)
        oh = o[:, cols]
        zh = z[:, cols]
        r = lax.rsqrt(jnp.mean(oh * oh, -1, keepdims=True) + NORM_EPS)
        mix_ref[:, cols] = (oh * r * gng_ref[...] * (zh * _sigmoid(zh))).astype(BF16)
    for g in range(SGU_GROUPS):
        cols = slice(g * SGU_GROUP, (g + 1) * SGU_GROUP)
        vg = v_ref[:, cols].astype(F32)
        vc = vg - jnp.mean(vg, -1, keepdims=True)
        vn = vc * lax.rsqrt(jnp.mean(vc * vc, -1, keepdims=True) + NORM_EPS)
        vn = (vn * lng_ref[g:g + 1] + lnb_ref[g:g + 1]).astype(BF16)
        wsg = ws_ref[g].astype(BF16)
        for n in range(tm // SGU_CHUNK):
            rows = slice(n * SGU_CHUNK, (n + 1) * SGU_CHUNK)
            s = _dot(wsg, vn[rows]) + bs_ref[g]
            mix_ref[rows, GDN_W + g * SGU_GROUP:GDN_W + (g + 1) * SGU_GROUP] = (
                u_ref[rows, cols].astype(F32) * s).astype(BF16)


def _outproj_kernel(of_ref, ob_ref, z_ref, u_ref, v_ref, gng_ref, lng_ref, lnb_ref, ws_ref, bs_ref,
                    x_ref, mod_ref, ng_ref, wout_ref, rwt_ref, rb_ref,
                    x1_ref, h2_ref, lt_ref, mix_scr):
    subs = _sub_tiles(x_ref.shape[0])
    for sl in subs:
        _mix_into(mix_scr.at[sl], of_ref.at[sl], ob_ref.at[sl], z_ref.at[sl], u_ref.at[sl],
                  v_ref.at[sl], gng_ref, lng_ref, lnb_ref, ws_ref, bs_ref)
    mod = mod_ref[0]
    ys = [_dot(mix_scr[sl], wout_ref[...]) for sl in subs]
    w_hi, w_lo = _split2(rwt_ref[...])
    for sl, y in zip(subs, ys):
        x1 = x_ref[sl] + mod[2:3] * _rms(y, ng_ref[1:2])
        x1_ref[sl] = x1
        h2 = _rms(x1, ng_ref[2:3]) * (1.0 + mod[4:5]) + mod[3:4]
        h_hi, h_lo = _split2(h2)
        _store_planes(h2_ref, _pack_rows(h_hi), sl)
        lt_ref[:, sl] = (_dot_nt(w_hi, h_hi) + _dot_nt(w_lo, h_hi) + _dot_nt(w_hi, h_lo)
                         + rb_ref[...])


def _outproj(o_f, o_b, z, ug, vg, gdn_norm_g, ln_g, ln_b, w_s, b_s_full,
             x2d, mod, rows_per_mod, ng, wout, rwt, rb_col):
    t = x2d.shape[0]
    tm = min(ROW_TILE, t)
    tiles_per_mod = rows_per_mod // tm
    row = lambda i: (i, 0)
    const = lambda i: (0, 0)
    c3 = lambda i: (0, 0, 0)
    return pl.pallas_call(
        _outproj_kernel,
        out_shape=(jax.ShapeDtypeStruct((t, D_MODEL), F32),
                   jax.ShapeDtypeStruct((N_PLANES, t, SC_ROW_W), U32),
                   jax.ShapeDtypeStruct((N_EXPERTS, t), F32)),
        grid=(t // tm,),
        in_specs=[pl.BlockSpec((tm, GDN_W), row), pl.BlockSpec((tm, GDN_W), row),
                  pl.BlockSpec((tm, GDN_W), row), pl.BlockSpec((tm, SGU_W), row),
                  pl.BlockSpec((tm, SGU_W), row),
                  pl.BlockSpec((1, HEAD_DIM), const),
                  pl.BlockSpec((SGU_GROUPS, SGU_GROUP), const),
                  pl.BlockSpec((SGU_GROUPS, SGU_GROUP), const),
                  pl.BlockSpec((SGU_GROUPS, SGU_CHUNK, SGU_CHUNK), c3),
                  pl.BlockSpec((SGU_GROUPS, SGU_CHUNK, SGU_GROUP), c3),
                  pl.BlockSpec((tm, D_MODEL), row),
                  pl.BlockSpec((1, 6, D_MODEL), lambda i: (i // tiles_per_mod, 0, 0)),
                  pl.BlockSpec((4, D_MODEL), const),
                  pl.BlockSpec((D_MODEL, D_MODEL), const),
                  pl.BlockSpec((N_EXPERTS, D_MODEL), const),
                  pl.BlockSpec((N_EXPERTS, 1), const)],
        out_specs=(pl.BlockSpec((tm, D_MODEL), row),
                   pl.BlockSpec((N_PLANES, tm, SC_ROW_W), lambda i: (0, i, 0)),
                   pl.BlockSpec((N_EXPERTS, tm), lambda i: (0, i))),
        compiler_params=_params("parallel"),
        scratch_shapes=[pltpu.VMEM((tm, D_MODEL), BF16)],
        name="out_proj_router",
    )(o_f, o_b, z, ug, vg, gdn_norm_g, ln_g, ln_b, w_s, b_s_full, x2d, mod, ng, wout, rwt, rb_col)


def _moe_kernel(be_ref, slot_ref, next_ref, short_ref, nb_ref, xb_ref, wgu_hbm, bgu_ref, wd_hbm,
                bd_ref, y_ref, wgu_f, wd_f, wgu_b, wd_b, gut_scr, sems):
    i = pl.program_id(0)
    live = i < nb_ref[0]
    new_expert = jnp.logical_or(i == 0, be_ref[i] != be_ref[jnp.maximum(i - 1, 0)])

    def weight_copies(expert, slot):
        return (pltpu.make_async_copy(wgu_hbm.at[expert], wgu_f.at[slot], sems.at[slot, 0]),
                pltpu.make_async_copy(wd_hbm.at[expert], wd_f.at[slot], sems.at[slot, 1]))

    @pl.when(i == 0)
    def _():
        for copy in weight_copies(be_ref[0], 0):
            copy.start()

    @pl.when(jnp.logical_and(live, new_expert))
    def _():
        slot = slot_ref[i]
        for copy in weight_copies(be_ref[i], slot):
            copy.wait()

        @pl.when(next_ref[i] >= 0)
        def _():
            for copy in weight_copies(next_ref[i], 1 - slot):
                copy.start()
        wgu_b[...] = wgu_f[slot].astype(BF16)
        wd_b[...] = wd_f[slot].astype(BF16)

    def ffn(n_rows):
        rows = slice(0, n_rows)
        xb = _unpack_rows(_load_planes(xb_ref, rows)).astype(BF16)
        gu = _dot(xb, wgu_b[...]) + bgu_ref[0]
        gu_t = gu.T
        acts = []
        for part in range(n_rows // LANES):
            part_ref = gut_scr.at[part]
            part_ref[...] = gu_t[:, part * LANES:(part + 1) * LANES]
            gate = jnp.minimum(part_ref[pl.ds(0, D_FF, stride=2), :], SWIGLU_LIMIT)
            up = jnp.clip(part_ref[pl.ds(1, D_FF, stride=2), :], -SWIGLU_LIMIT, SWIGLU_LIMIT)
            acts.append(((up + 1.0) * gate * _sigmoid(SWIGLU_ALPHA * gate)).astype(BF16))
        act_t = jnp.concatenate(acts, axis=1)
        _store_planes(y_ref, _pack_rows(_dot_tn(act_t, wd_b[...]) + bd_ref[0]), rows)

    short = short_ref[i] != 0

    @pl.when(jnp.logical_and(live, jnp.logical_not(short)))
    def _():
        ffn(MOE_ROWS)

    @pl.when(jnp.logical_and(live, short))
    def _():
        ffn(MOE_ROWS // 2)
        for p in range(N_PLANES):
            y_ref[p, MOE_ROWS // 2:] = jnp.zeros((MOE_ROWS // 2, SC_ROW_W), U32)

    @pl.when(jnp.logical_not(live))
    def _():
        y_ref[...] = jnp.zeros_like(y_ref)


def _moe_ffn(block_e, n_used, real_end, xb, w_gu, b_gu, w_down, b_down):
    n_rows = xb.shape[1]
    n_blocks = n_rows // MOE_ROWS
    idx = jnp.arange(n_blocks, dtype=jnp.int32)
    live = idx < n_used[0]
    changed = jnp.concatenate([jnp.ones((1,), bool), block_e[1:] != block_e[:-1]]) & live
    slot = ((jnp.cumsum(changed.astype(jnp.int32)) - 1) % 2).astype(jnp.int32)
    change_at = jnp.where(changed, idx, n_blocks)
    next_change = lax.cummin(jnp.concatenate([change_at[1:], jnp.full((1,), n_blocks, jnp.int32)]),
                             reverse=True)
    next_e = jnp.where(next_change < n_blocks,
                       block_e[jnp.minimum(next_change, n_blocks - 1)], -1).astype(jnp.int32)
    n_real = jnp.clip(real_end[block_e] - idx * MOE_ROWS, 0, MOE_ROWS)
    short = (n_real <= MOE_ROWS // 2).astype(jnp.int32)

    row = lambda i, be, sl, nx, sh, nb: (0, i, 0)
    ex3 = lambda i, be, sl, nx, sh, nb: (be[i], 0, 0)
    live_row = lambda i, be, sl, nx, sh, nb: (0, jnp.minimum(i, nb[0] - 1), 0)
    planes = (N_PLANES, MOE_ROWS, SC_ROW_W)
    grid_spec = pltpu.PrefetchScalarGridSpec(
        num_scalar_prefetch=5,
        grid=(n_blocks,),
        in_specs=[pl.BlockSpec(planes, live_row),
                  pl.BlockSpec(memory_space=pl.ANY),
                  pl.BlockSpec((1, 1, 2 * D_FF), ex3),
                  pl.BlockSpec(memory_space=pl.ANY),
                  pl.BlockSpec((1, 1, D_MODEL), ex3)],
        out_specs=pl.BlockSpec(planes, row),
        scratch_shapes=[pltpu.VMEM((2, D_MODEL, 2 * D_FF), F32),
                        pltpu.VMEM((2, D_FF, D_MODEL), F32),
                        pltpu.VMEM((D_MODEL, 2 * D_FF), BF16),
                        pltpu.VMEM((D_FF, D_MODEL), BF16),
                        pltpu.VMEM((MOE_ROWS // LANES, 2 * D_FF, LANES), F32),
                        pltpu.SemaphoreType.DMA((2, 2))],
    )
    return pl.pallas_call(
        _moe_kernel,
        out_shape=jax.ShapeDtypeStruct((N_PLANES, n_rows, SC_ROW_W), U32),
        grid_spec=grid_spec,
        compiler_params=_params("arbitrary"),
        name="moe_ffn",
    )(block_e, slot, next_e, short, n_used, xb, w_gu, b_gu, w_down, b_down)


def _combine_kernel(y0_ref, y1_ref, y2_ref, y3_ref, gt_ref, x1_ref, mod_ref, ng_ref, o_ref):
    mod = mod_ref[0]
    tm = o_ref.shape[0]
    gt = jnp.concatenate([gt_ref[...], jnp.zeros((LANES - SUBLANES, tm), F32)], axis=0).T
    y = _unpack_rows(_load_planes(y0_ref)) * gt[:, 0:1]
    for k, y_ref in ((1, y1_ref), (2, y2_ref), (3, y3_ref)):
        y = y + _unpack_rows(_load_planes(y_ref)) * gt[:, k:k + 1]
    o_ref[...] = x1_ref[...] + mod[5:6] * _rms(y, ng_ref[3:4])


def _combine(yg, gates, x1, mod, rows_per_mod, ng):
    t = x1.shape[0]
    tm = min(COMBINE_TILE, t)
    tiles_per_mod = rows_per_mod // tm
    n_tiles = t // tm
    row = lambda i: (i, 0)
    choice = lambda k: pl.BlockSpec((N_PLANES, tm, SC_ROW_W), lambda i: (0, k * n_tiles + i, 0))
    return pl.pallas_call(
        _combine_kernel,
        out_shape=jax.ShapeDtypeStruct((t, D_MODEL), F32),
        grid=(n_tiles,),
        in_specs=[choice(0), choice(1), choice(2), choice(3),
                  pl.BlockSpec((SUBLANES, tm), lambda i: (0, i)),
                  pl.BlockSpec((tm, D_MODEL), row),
                  pl.BlockSpec((1, 6, D_MODEL), lambda i: (i // tiles_per_mod, 0, 0)),
                  pl.BlockSpec((4, D_MODEL), lambda i: (0, 0))],
        out_specs=pl.BlockSpec((tm, D_MODEL), row),
        compiler_params=_params("parallel"),
        name="moe_combine",
    )(yg, yg, yg, yg, gates, x1, mod, ng)


def _route_kernel(lt_ref, eidx_ref, gate_ref, rank_ref, cnt_ref, carry, earlier):
    i = pl.program_id(0)
    tile = earlier.shape[0]

    @pl.when(i == 0)
    def _():
        carry[...] = jnp.zeros_like(carry)
        ti = lax.broadcasted_iota(jnp.int32, (tile, tile), 0)
        tj = lax.broadcasted_iota(jnp.int32, (tile, tile), 1)
        earlier[...] = jnp.where(ti < tj, 1.0, 0.0).astype(BF16)

    eio = lax.broadcasted_iota(jnp.int32, (N_EXPERTS, tile), 0).astype(F32)
    seen = carry[...]
    for sub in range(lt_ref.shape[1] // tile):
        cols = slice(sub * tile, (sub + 1) * tile)
        logits = lt_ref[:, cols]
        vals, sels = [], []
        for k in range(TOP_K):
            m = jnp.max(logits, axis=0, keepdims=True)
            idx = jnp.min(jnp.where(logits == m, eio, float(N_EXPERTS)), axis=0, keepdims=True)
            sel = eio == idx
            logits = jnp.where(sel, -jnp.inf, logits)
            vals.append(m)
            sels.append(sel)
            eidx_ref[k:k + 1, cols] = idx.astype(jnp.int32)
        exps = [jnp.exp(v - vals[0]) for v in vals]
        denom = exps[0] + exps[1] + exps[2] + exps[3]
        for k in range(TOP_K):
            gate_ref[k:k + 1, cols] = exps[k] / denom
        gate_ref[TOP_K:, cols] = jnp.zeros((SUBLANES - TOP_K, tile), F32)

        member = jnp.where(sels[0] | sels[1] | sels[2] | sels[3], 1.0, 0.0)
        before = _dot(member.astype(BF16), earlier[...]) + seen
        for k in range(TOP_K):
            rank_ref[k:k + 1, cols] = jnp.sum(jnp.where(sels[k], before, 0.0), axis=0,
                                              keepdims=True).astype(jnp.int32)
        seen = seen + jnp.sum(member, axis=1, keepdims=True)
    carry[...] = seen
    cnt_ref[...] = seen.astype(jnp.int32)


def _route(logits_t):
    t = logits_t.shape[1]
    block = min(ROUTE_BLOCK, t)
    tile = min(ROUTE_TILE, block)
    blk = lambda i: (0, i)
    return pl.pallas_call(
        _route_kernel,
        out_shape=(jax.ShapeDtypeStruct((TOP_K, t), jnp.int32),
                   jax.ShapeDtypeStruct((SUBLANES, t), F32),
                   jax.ShapeDtypeStruct((TOP_K, t), jnp.int32),
                   jax.ShapeDtypeStruct((N_EXPERTS, 1), jnp.int32)),
        grid=(t // block,),
        in_specs=[pl.BlockSpec((N_EXPERTS, block), blk)],
        out_specs=(pl.BlockSpec((TOP_K, block), blk), pl.BlockSpec((SUBLANES, block), blk),
                   pl.BlockSpec((TOP_K, block), blk),
                   pl.BlockSpec((N_EXPERTS, 1), lambda i: (0, 0))),
        scratch_shapes=[pltpu.VMEM((N_EXPERTS, 1), F32), pltpu.VMEM((tile, tile), BF16)],
        compiler_params=_params("arbitrary"),
        name="moe_route",
    )(logits_t)


def _slot_tables(eidx, rank, counts, n_blocks):
    padded = (counts + MOE_ROWS - 1) // MOE_ROWS * MOE_ROWS
    pad_end = jnp.cumsum(padded)
    pad_start = pad_end - padded
    experts = jnp.arange(N_EXPERTS, dtype=jnp.int32)
    dest = rank + jnp.sum(jnp.where(eidx[..., None] == experts, pad_start, 0), axis=-1)
    first_row = jnp.arange(n_blocks, dtype=jnp.int32)[:, None] * MOE_ROWS
    block_e = jnp.minimum(jnp.sum((pad_end[None, :] <= first_row).astype(jnp.int32), axis=1),
                          N_EXPERTS - 1)
    n_used = pad_end[-1:] // MOE_ROWS
    real_end = (pad_start + counts).astype(jnp.int32)
    return (dest.astype(jnp.int32), pad_start.astype(jnp.int32), block_e, n_used.astype(jnp.int32),
            real_end)


def _sc_mesh():
    return plsc.VectorSubcoreMesh(core_axis_name="c", subcore_axis_name="s",
                                  num_cores=SC_CORES, num_subcores=SC_SUBCORES)


def _plane_row_ids(rows, rows_per_plane):
    return jnp.concatenate([rows + p * rows_per_plane for p in range(N_PLANES)], axis=-1)


def _sc_gather_rows(table, rows):
    v = table.shape[1]
    idx = _plane_row_ids(rows, v)[None]
    n_all = idx.shape[1]

    @functools.partial(pl.kernel, mesh=_sc_mesh(), name="moe_gather_rows",
                       out_type=jax.ShapeDtypeStruct((n_all, SC_ROW_W), U32))
    def gather(x_hbm, i_hbm, o_hbm):
        def body(i_vmem, o_vmem):
            pltpu.sync_copy(x_hbm.at[i_vmem.at[0]], o_vmem)

        pltpu.emit_pipeline(
            body, grid=(n_all // SC_WINDOW,),
            in_specs=[pl.BlockSpec((1, SC_WINDOW), lambda i: (0, i))],
            out_specs=[pl.BlockSpec((SC_WINDOW, SC_ROW_W), lambda i: (i, 0))],
            core_axis_name=("c", "s"), dimension_semantics=(pltpu.PARALLEL,),
        )(i_hbm, o_hbm)

    return gather(table.reshape(N_PLANES * v, SC_ROW_W), idx).reshape(N_PLANES, -1, SC_ROW_W)


def _sc_scatter_rows(rows, dest, n_out):
    t = rows.shape[1]
    idx = _plane_row_ids(dest, n_out)

    @functools.partial(pl.kernel, mesh=_sc_mesh(), name="moe_scatter_rows", scratch_types=[],
                       out_type=jax.ShapeDtypeStruct((N_PLANES * n_out, SC_ROW_W), U32))
    def scatter(x_hbm, i_hbm, o_hbm):
        def body(x_vmem, i_vmem):
            for k in range(TOP_K):
                pltpu.sync_copy(x_vmem, o_hbm.at[i_vmem.at[k]])

        pltpu.emit_pipeline(
            body, grid=(N_PLANES * t // SC_WINDOW,),
            in_specs=[pl.BlockSpec((SC_WINDOW, SC_ROW_W), lambda i: (i, 0)),
                      pl.BlockSpec((TOP_K, SC_WINDOW), lambda i: (0, i))],
            out_specs=[],
            core_axis_name=("c", "s"), dimension_semantics=(pltpu.PARALLEL,),
        )(x_hbm, i_hbm)

    return scatter(rows.reshape(N_PLANES * t, SC_ROW_W), idx).reshape(N_PLANES, n_out, SC_ROW_W)


def _zero_pad_kernel(cnt_ref, start_ref, xb_in_ref, xb_ref, zero_scr, sem):
    del xb_in_ref
    zero_scr[...] = jnp.zeros_like(zero_scr)

    pieces = [SUBLANES << bit for bit in range((MOE_ROWS // SUBLANES - 1).bit_length())]

    def zero_copy(p, row, size):
        return pltpu.make_async_copy(zero_scr.at[pl.ds(0, size)], xb_ref.at[p, pl.ds(row, size)], sem)

    def for_each_piece(fn):
        def per_expert(e, carry):
            n_real = cnt_ref[e]
            n_pad = (MOE_ROWS - n_real % MOE_ROWS) % MOE_ROWS
            first = start_ref[e] + n_real
            n_single = n_pad % SUBLANES
            for j in range(SUBLANES - 1):
                @pl.when(j < n_single)
                def _():
                    for p in range(N_PLANES):
                        fn(zero_copy(p, first + j, 1))
            row = first + n_single
            for size in pieces:
                @pl.when((n_pad & size) != 0)
                def _():
                    for p in range(N_PLANES):
                        fn(zero_copy(p, pl.multiple_of(row, SUBLANES), size))
                row = row + (n_pad & size)
            return carry
        lax.fori_loop(0, N_EXPERTS, per_expert, 0)

    for_each_piece(lambda copy: copy.start())
    for_each_piece(lambda copy: copy.wait())


def _zero_pad_slots(counts, pad_start, xb):
    grid_spec = pltpu.PrefetchScalarGridSpec(
        num_scalar_prefetch=2,
        grid=(1,),
        in_specs=[pl.BlockSpec(memory_space=pl.ANY)],
        out_specs=pl.BlockSpec(memory_space=pl.ANY),
        scratch_shapes=[pltpu.VMEM((MOE_ROWS // 2, SC_ROW_W), U32), pltpu.SemaphoreType.DMA],
    )
    return pl.pallas_call(
        _zero_pad_kernel,
        out_shape=jax.ShapeDtypeStruct(xb.shape, xb.dtype),
        grid_spec=grid_spec,
        input_output_aliases={2: 0},
        compiler_params=_params("arbitrary"),
        name="moe_zero_pad",
    )(counts, pad_start, xb)


def kernel(x, c, ctx, c_ctx, w_ada, b_ada, norm_g, w_in, conv_w, a_log, dt_bias, gdn_norm_g,
           sgu_ln_g, sgu_ln_b, sgu_w, sgu_b, w_out, router_w, router_b, w_gu, b_gu, w_down, b_down):
    b, l, d = x.shape
    lc = ctx.shape[1]
    t = b * l
    assert d == D_MODEL and l % ROW_TILE == 0 and l % GDN_BLOCK == 0 and lc % GDN_BLOCK == 0
    assert w_ada.shape[0] == 1, "single-layer block"

    cs = jnp.concatenate([c, c_ctx[None], jnp.zeros((8 - b - 1, d), F32)], axis=0)
    mod_all = _ada(cs, w_ada[0], b_ada[0][None])
    mod = mod_all[:b].reshape(b, 6, d)
    mod_c = mod_all[b:b + 1].reshape(1, 6, d)
    ng = norm_g[0]


    x2d = x.reshape(t, d)
    qkv, z, ug, vg, ba, bat = _inproj(x2d, mod, l, ng[0:1], w_in[0])
    ctx2d = ctx.reshape(b * lc, d)
    qkv_c, _, _, _, ba_c, bat_c = _inproj(ctx2d, mod_c, b * lc, ng[0:1], w_in[0])

    alog = a_log[0].reshape(-1)
    dtb = dt_bias[0].reshape(-1)
    pc = _gdn_prep(qkv_c.reshape(b, lc, QKV_COLS), conv_w[0], ba_c.reshape(b, lc, N_GATE_COLS),
                   bat_c, alog, dtb, lc)
    s_zero = jnp.zeros((b, N_CHAINS, HEAD_DIM, HEAD_DIM), F32)
    _, _, s_ctx = _gdn_scan(*pc, s_zero)
    pp = _gdn_prep(qkv.reshape(b, l, QKV_COLS), conv_w[0], ba.reshape(b, l, N_GATE_COLS),
                   bat, alog, dtb, GRID_W)
    o_f, o_b, _ = _gdn_scan(*pp, s_ctx)

    b_s_full = jnp.broadcast_to(sgu_b[0][:, :, None], (SGU_GROUPS, SGU_CHUNK, SGU_GROUP))
    x1, h2p, logits_t = _outproj(o_f.reshape(t, GDN_W), o_b.reshape(t, GDN_W), z, ug, vg,
                                 gdn_norm_g, sgu_ln_g[0], sgu_ln_b[0], sgu_w[0], b_s_full,
                                 x2d, mod, l, ng, w_out[0].astype(BF16),
                                 router_w[0].T, router_b[0][:, None])

    eidx, gates_t, rank, counts = _route(logits_t)
    counts = counts[:, 0]
    n_blocks = -(-(t * TOP_K) // MOE_ROWS) + N_EXPERTS
    dest, pad_start, block_e, n_used, real_end = _slot_tables(eidx, rank, counts, n_blocks)
    xb = _sc_scatter_rows(h2p, dest, n_blocks * MOE_ROWS)
    xb = _zero_pad_slots(counts, pad_start, xb)
    yb = _moe_ffn(block_e, n_used, real_end, xb, w_gu[0], b_gu[0][:, None, :], w_down[0],
                  b_down[0][:, None, :])
    yg = _sc_gather_rows(yb, dest.reshape(-1))
    out = _combine(yg, gates_t, x1, mod, l, ng)
    return out.reshape(b, l, d)
```

```python
import functools
import math

import jax
import jax.numpy as jnp
from jax import lax
from jax.experimental import pallas as pl
from jax.experimental.pallas import tpu as pltpu
from jax.experimental.pallas import tpu_sc as plsc

F32 = jnp.float32
BF16 = jnp.bfloat16

D_MODEL = 1024
GDN_HEADS = 4
HEAD_DIM = 128
GDN_W = GDN_HEADS * HEAD_DIM
SGU_GROUPS = 4
SGU_GROUP = 128
SGU_W = SGU_GROUPS * SGU_GROUP
SGU_CHUNK = 128
DELTA_CHUNK = 64
GRID_W = 64
N_EXPERTS = 32
TOP_K = 4
D_FF = 1024
SWIGLU_LIMIT = 7.0
SWIGLU_ALPHA = 1.702
NORM_EPS = 1e-6
QKV_COLS = 3 * GDN_W
N_CHAINS = 2 * GDN_HEADS
N_GATE_COLS = 2 * N_CHAINS

ROW_TILE = 1024
SUB_ROWS = 256
COMBINE_TILE = 1024
ADA_COLS = 3 * D_MODEL
GDN_BLOCK = 256
CHUNKS_PER_BLOCK = GDN_BLOCK // DELTA_CHUNK
N_PAIR_LEVELS = DELTA_CHUNK.bit_length() - 1
MOE_ROWS = 1024
FFN_ROWS = 512
U32 = jnp.uint32
LANES = 128
SUBLANES = 8
PACKED_W = D_MODEL // 2
ROUTE_BLOCK = 2048
ROUTE_TILE = 512
SC_CORES = 2
SC_SUBCORES = 16
SC_WINDOW = 128
N_PLANES = 2
SC_ROW_W = PACKED_W // N_PLANES
VMEM_LIMIT = 56 * 1024 * 1024


def _params(*sem):
    return pltpu.CompilerParams(dimension_semantics=sem, vmem_limit_bytes=VMEM_LIMIT)


def _dot(a, b):
    return jnp.dot(a, b, preferred_element_type=F32)


def _dot_nt(a, b):
    return lax.dot_general(a, b, (((1,), (1,)), ((), ())), preferred_element_type=F32)


def _dot_tn(a, b):
    return lax.dot_general(a, b, (((0,), (0,)), ((), ())), preferred_element_type=F32)


def _split2(a):
    hi = a.astype(BF16)
    lo = (a - hi.astype(F32)).astype(BF16)
    return hi, lo


def _split3(a):
    hi = a.astype(BF16)
    r = a - hi.astype(F32)
    mid = r.astype(BF16)
    lo = (r - mid.astype(F32)).astype(BF16)
    return hi, mid, lo


def _pack_rows(x):
    xb = x.astype(BF16).astype(F32)
    hi = lax.bitcast_convert_type(xb[:, :PACKED_W], U32)
    lo = lax.bitcast_convert_type(xb[:, PACKED_W:], U32)
    return hi | (lo >> 16)


def _store_planes(ref, packed, rows=slice(None)):
    for p in range(N_PLANES):
        ref[p, rows] = packed[:, p * SC_ROW_W:(p + 1) * SC_ROW_W]


def _load_planes(ref, rows=slice(None)):
    return jnp.concatenate([ref[p, rows] for p in range(N_PLANES)], axis=1)


def _sub_tiles(n_rows):
    return [slice(r, r + SUB_ROWS) for r in range(0, n_rows, SUB_ROWS)]


def _unpack_rows(w):
    hi = lax.bitcast_convert_type(w & jnp.uint32(0xFFFF0000), F32)
    lo = lax.bitcast_convert_type(w << 16, F32)
    return jnp.concatenate([hi, lo], axis=1)


def _rms(x32, g):
    return x32 * lax.rsqrt(jnp.mean(x32 * x32, -1, keepdims=True) + NORM_EPS) * g


def _gelu_tanh(x):
    c = math.sqrt(2.0 / math.pi)
    return 0.5 * x * (1.0 + jnp.tanh(c * (x + 0.044715 * (x * x * x))))


def _sigmoid(x):
    return 1.0 / (1.0 + jnp.exp(-x))


def _softplus(x):
    return jnp.maximum(x, 0.0) + jnp.log(1.0 + jnp.exp(-jnp.abs(x)))


def _ada_kernel(c_ref, w_ref, b_ref, o_ref):
    c = c_ref[...]
    s = c * _sigmoid(c)
    s_hi, s_lo = _split2(s)
    w_hi, w_lo = _split2(w_ref[...])
    o_ref[...] = _dot(s_hi, w_hi) + _dot(s_lo, w_hi) + _dot(s_hi, w_lo) + b_ref[...]


def _ada(cs, w_ada, b_ada):
    n = w_ada.shape[1]
    bn = ADA_COLS
    return pl.pallas_call(
        _ada_kernel,
        out_shape=jax.ShapeDtypeStruct((cs.shape[0], n), F32),
        grid=(n // bn,),
        in_specs=[pl.BlockSpec(cs.shape, lambda j: (0, 0)),
                  pl.BlockSpec((D_MODEL, bn), lambda j: (0, j)),
                  pl.BlockSpec((1, bn), lambda j: (0, j))],
        out_specs=pl.BlockSpec((cs.shape[0], bn), lambda j: (0, j)),
        compiler_params=_params("parallel"),
        name="ada_mod",
    )(cs, w_ada, b_ada)


def _inproj_kernel(x_ref, mod_ref, g_ref, w_hbm,
                   qkv_ref, z_ref, u_ref, v_ref, ba_ref, bat_ref,
                   w_f32, wqkv_ref, wzuv_ref, wba_ref, sem):
    @pl.when(pl.program_id(0) == 0)
    def _():
        copy = pltpu.make_async_copy(w_hbm, w_f32, sem)
        copy.start()
        copy.wait()
        wqkv_ref[...] = w_f32[:, :QKV_COLS].astype(BF16)
        wzuv_ref[...] = w_f32[:, QKV_COLS + N_GATE_COLS:].astype(BF16)
        lane = lax.broadcasted_iota(jnp.int32, (D_MODEL, LANES), 1)
        w_gate = jnp.where(lane < N_GATE_COLS, w_f32[:, QKV_COLS:QKV_COLS + LANES], 0.0)
        w_hi, w_lo = _split2(w_gate)
        wba_ref[:, :LANES] = w_hi
        wba_ref[:, LANES:] = w_lo

    mod = mod_ref[0]
    subs = _sub_tiles(x_ref.shape[0])
    hs = [_split2(_rms(x_ref[sl], g_ref[...]) * (1.0 + mod[1:2]) + mod[0:1]) for sl in subs]
    qkvs = [_dot(h_hi, wqkv_ref[...]) for h_hi, _ in hs]
    zuvs = [_dot(h_hi, wzuv_ref[...]) for h_hi, _ in hs]
    bas = []
    for h_hi, h_lo in hs:
        both = _dot(h_hi, wba_ref[...])
        bas.append(both[:, :LANES] + both[:, LANES:] + _dot(h_lo, wba_ref[:, :LANES]))
    for sl, qkv, zuv, ba in zip(subs, qkvs, zuvs, bas):
        qkv_ref[sl] = qkv.astype(BF16)
        z_ref[sl] = zuv[:, :GDN_W].astype(BF16)
        u_ref[sl] = _gelu_tanh(zuv[:, GDN_W:GDN_W + SGU_W]).astype(BF16)
        v_ref[sl] = _gelu_tanh(zuv[:, GDN_W + SGU_W:]).astype(BF16)
        ba_ref[sl] = ba[:, :N_GATE_COLS]
        bat_ref[:, sl] = ba.T[:N_GATE_COLS]


def _inproj(x2d, mod, rows_per_mod, ng0, w_in):
    t = x2d.shape[0]
    tm = min(ROW_TILE, t)
    tiles_per_mod = rows_per_mod // tm
    row = lambda i: (i, 0)
    const = lambda i: (0, 0)
    return pl.pallas_call(
        _inproj_kernel,
        out_shape=(jax.ShapeDtypeStruct((t, QKV_COLS), BF16),
                   jax.ShapeDtypeStruct((t, GDN_W), BF16),
                   jax.ShapeDtypeStruct((t, SGU_W), BF16),
                   jax.ShapeDtypeStruct((t, SGU_W), BF16),
                   jax.ShapeDtypeStruct((t, N_GATE_COLS), F32),
                   jax.ShapeDtypeStruct((N_GATE_COLS, t), F32)),
        grid=(t // tm,),
        in_specs=[pl.BlockSpec((tm, D_MODEL), row),
                  pl.BlockSpec((1, 6, D_MODEL), lambda i: (i // tiles_per_mod, 0, 0)),
                  pl.BlockSpec((1, D_MODEL), const),
                  pl.BlockSpec(memory_space=pl.ANY)],
        out_specs=(pl.BlockSpec((tm, QKV_COLS), row),
                   pl.BlockSpec((tm, GDN_W), row),
                   pl.BlockSpec((tm, SGU_W), row),
                   pl.BlockSpec((tm, SGU_W), row),
                   pl.BlockSpec((tm, N_GATE_COLS), row),
                   pl.BlockSpec((N_GATE_COLS, tm), lambda i: (0, i))),
        scratch_shapes=[pltpu.VMEM(w_in.shape, F32),
                        pltpu.VMEM((D_MODEL, QKV_COLS), BF16),
                        pltpu.VMEM((D_MODEL, w_in.shape[1] - QKV_COLS - N_GATE_COLS), BF16),
                        pltpu.VMEM((D_MODEL, 2 * LANES), BF16),
                        pltpu.SemaphoreType.DMA],
        compiler_params=_params("arbitrary"),
        name="in_proj",
    )(x2d, mod, ng0, w_in)


def _gdn_prep_kernel(row_len, qkv_ref, cw_ref, ba_ref, bat_ref, alog_r_ref, dtb_r_ref,
                     alog_c_ref, dtb_c_ref, u_ref, w_ref, qd_ref, kd_ref, at_ref, gl_ref,
                     tri_scr, pair_scr, spread_scr):
    n = GDN_BLOCK
    c = DELTA_CHUNK

    def mask01(m):
        return jnp.where(m, 1.0, 0.0).astype(BF16)

    wi = lax.broadcasted_iota(jnp.int32, (c, n), 0)
    wl = lax.broadcasted_iota(jnp.int32, (c, n), 1)
    wchunk = wl // c
    wj = wl % c
    lower_w = wi >= wj
    upper_w = wi <= wj
    diag_w = wi == wj
    eye_w = jnp.where(diag_w, 1.0, 0.0)

    @pl.when(jnp.logical_and(pl.program_id(0) == 0, pl.program_id(1) == 0))
    def _():
        ri = lax.broadcasted_iota(jnp.int32, (n, n), 0)
        ci = lax.broadcasted_iota(jnp.int32, (n, n), 1)
        same = (ri // c) == (ci // c)
        tri_scr[0] = mask01(same & (ri >= ci))
        tri_scr[1] = mask01(same & (ri <= ci))
        tri_scr[2] = mask01(same)
        for m in range(N_PAIR_LEVELS):
            s = 1 << m
            pair_scr[m] = mask01(((wi // (2 * s)) == (wj // (2 * s))) & ((wi // s) != (wj // s)))
        ei = lax.broadcasted_iota(jnp.int32, (n, CHUNKS_PER_BLOCK * HEAD_DIM), 0) // c
        ej = lax.broadcasted_iota(jnp.int32, (n, CHUNKS_PER_BLOCK * HEAD_DIM), 1) // HEAD_DIM
        spread_scr[...] = mask01(ei == ej)

    lower_b = tri_scr[0]
    upper_b = tri_scr[1]
    same_b = tri_scr[2]
    pair_masks = [pair_scr[m] for m in range(N_PAIR_LEVELS)]

    def to_wide(full):
        out = full[:c]
        for k in range(1, CHUNKS_PER_BLOCK):
            out = jnp.where(wchunk == k, full[k * c:(k + 1) * c], out)
        return out

    def col_wide(col):
        out = jnp.broadcast_to(col[:c], (c, n))
        for k in range(1, CHUNKS_PER_BLOCK):
            out = jnp.where(wchunk == k, jnp.broadcast_to(col[k * c:(k + 1) * c], (c, n)), out)
        return out

    def block_diag(x_w):
        return jnp.concatenate([x_w] * CHUNKS_PER_BLOCK, axis=0) * same_b

    ba = ba_ref[0]
    bat = bat_ref[...]
    beta_c = _sigmoid(ba[:, :N_CHAINS])
    g_c = -jnp.exp(alog_r_ref[...]) * _softplus(ba[:, N_CHAINS:] + dtb_r_ref[...])
    g_r = -jnp.exp(alog_c_ref[...]) * _softplus(bat[N_CHAINS:] + dtb_c_ref[...])
    gc3 = _split3(g_c)
    gr3 = jnp.concatenate(_split3(g_r), axis=0)

    def sum3_r(m):
        return m[:N_CHAINS] + m[N_CHAINS:2 * N_CHAINS] + m[2 * N_CHAINS:]

    cum_f_c = _dot(lower_b, gc3[0]) + _dot(lower_b, gc3[1]) + _dot(lower_b, gc3[2])
    tot_c = _dot(same_b, gc3[0]) + _dot(same_b, gc3[1]) + _dot(same_b, gc3[2])
    cum_b_c = tot_c - cum_f_c + g_c
    cum_f_r = sum3_r(_dot(gr3, upper_b))
    cum_b_r = sum3_r(_dot(gr3, lower_b))
    g_last = jnp.exp(sum3_r(_dot(gr3, spread_scr[...])))
    gl_ref[0, 0, 0] = g_last[:GDN_HEADS]
    gl_ref[0, 0, 1] = g_last[GDN_HEADS:]

    pos = lax.broadcasted_iota(jnp.int32, (n, HEAD_DIM), 0) % row_len
    first = pos == 0
    last = pos == row_len - 1

    def conv_silu(col):
        x = qkv_ref[0, :, col * HEAD_DIM:(col + 1) * HEAD_DIM].astype(F32)
        cw = cw_ref[:, col * HEAD_DIM:(col + 1) * HEAD_DIM]
        xp = jnp.where(first, 0.0, pltpu.roll(x, 1, 0))
        xn = jnp.where(last, 0.0, pltpu.roll(x, n - 1, 0))
        y = xp * cw[0:1] + x * cw[1:2] + xn * cw[2:3]
        return y * _sigmoid(y)

    def l2n(x):
        return x * lax.rsqrt(jnp.sum(x * x, -1, keepdims=True) + NORM_EPS)

    a_bs, ps, rhss = [None] * N_CHAINS, [None] * N_CHAINS, [None] * N_CHAINS
    first_pairs = pair_masks[0].astype(F32)
    for h in range(GDN_HEADS):
        q = l2n(conv_silu(h)) * (HEAD_DIM ** -0.5)
        k = l2n(conv_silu(GDN_HEADS + h))
        v = conv_silu(2 * GDN_HEADS + h)
        k_b = k.astype(BF16)
        qk_kk = _dot_nt(jnp.concatenate([q.astype(BF16), k_b], axis=0), k_b)
        qk_w = to_wide(qk_kk[:n])
        kk_w = to_wide(qk_kk[n:])
        for d in range(2):
            j = d * GDN_HEADS + h
            mask_w = lower_w if d == 0 else upper_w
            cum_c = (cum_f_c if d == 0 else cum_b_c)[:, j:j + 1]
            cum_r = (cum_f_r if d == 0 else cum_b_r)[j:j + 1, :]
            b_c = beta_c[:, j:j + 1]
            decay_w = jnp.where(mask_w, jnp.exp(jnp.where(mask_w, col_wide(cum_c) - cum_r, 0.0)), 0.0)
            amat_w = jnp.where(diag_w, 0.0, kk_w * decay_w * col_wide(b_c))
            a_bs[j] = amat_w.astype(BF16)
            ps[j] = eye_w - amat_w * first_pairs
            e_c = jnp.exp(cum_c)
            rhss[j] = jnp.concatenate([(v * b_c).astype(BF16), (k * (b_c * e_c)).astype(BF16)], axis=1)
            cols = slice(j * HEAD_DIM, (j + 1) * HEAD_DIM)
            qd_ref[0, :, cols] = (q * e_c).astype(BF16)
            kd_ref[0, :, cols] = (k * jnp.exp(tot_c[:, j:j + 1] - cum_c)).astype(BF16)
            at_ref[0, 0, j * c:(j + 1) * c, :] = (qk_w * decay_w).astype(BF16)

    for pm in pair_masks[1:]:
        p_bs = [p.astype(BF16) for p in ps]
        ys = [_dot(a_bs[j] * pm, block_diag(p_bs[j])) for j in range(N_CHAINS)]
        ps = [ps[j] - _dot(p_bs[j], block_diag(ys[j].astype(BF16))) for j in range(N_CHAINS)]

    for j in range(N_CHAINS):
        uw = _dot(block_diag(ps[j].astype(BF16)), rhss[j])
        cols = slice(j * HEAD_DIM, (j + 1) * HEAD_DIM)
        u_ref[0, :, cols] = uw[:, :HEAD_DIM].astype(BF16)
        w_ref[0, :, cols] = uw[:, HEAD_DIM:].astype(BF16)


def _gdn_prep(qkv, conv_w, ba, bat, alog, dtb, row_len):
    b, l, _ = qkv.shape
    nblk = l // GDN_BLOCK
    wide = N_CHAINS * HEAD_DIM
    blk = lambda bi, i: (bi, i, 0)
    const = lambda bi, i: (0, 0)
    alog_r, dtb_r = alog.reshape(1, N_CHAINS), dtb.reshape(1, N_CHAINS)
    alog_c, dtb_c = alog.reshape(N_CHAINS, 1), dtb.reshape(N_CHAINS, 1)
    return pl.pallas_call(
        functools.partial(_gdn_prep_kernel, row_len),
        out_shape=(jax.ShapeDtypeStruct((b, l, wide), BF16),) * 4 + (
            jax.ShapeDtypeStruct((b, nblk, N_CHAINS * DELTA_CHUNK, GDN_BLOCK), BF16),
            jax.ShapeDtypeStruct((b, nblk, 2, GDN_HEADS, CHUNKS_PER_BLOCK * HEAD_DIM), F32)),
        grid=(b, nblk),
        in_specs=[pl.BlockSpec((1, GDN_BLOCK, QKV_COLS), blk),
                  pl.BlockSpec((3, QKV_COLS), const),
                  pl.BlockSpec((1, GDN_BLOCK, N_GATE_COLS), blk),
                  pl.BlockSpec((N_GATE_COLS, GDN_BLOCK), lambda bi, i: (0, bi * nblk + i)),
                  pl.BlockSpec((1, N_CHAINS), const),
                  pl.BlockSpec((1, N_CHAINS), const),
                  pl.BlockSpec((N_CHAINS, 1), const),
                  pl.BlockSpec((N_CHAINS, 1), const)],
        out_specs=(pl.BlockSpec((1, GDN_BLOCK, wide), blk),) * 4 + (
            pl.BlockSpec((1, 1, N_CHAINS * DELTA_CHUNK, GDN_BLOCK), lambda bi, i: (bi, i, 0, 0)),
            pl.BlockSpec((1, 1, 2, GDN_HEADS, CHUNKS_PER_BLOCK * HEAD_DIM),
                         lambda bi, i: (bi, i, 0, 0, 0))),
        scratch_shapes=[pltpu.VMEM((3, GDN_BLOCK, GDN_BLOCK), BF16),
                        pltpu.VMEM((N_PAIR_LEVELS, DELTA_CHUNK, GDN_BLOCK), BF16),
                        pltpu.VMEM((GDN_BLOCK, CHUNKS_PER_BLOCK * HEAD_DIM), BF16)],
        compiler_params=_params("arbitrary", "arbitrary"),
        name="gdn_prep",
    )(qkv, conv_w, ba, bat, alog_r, dtb_r, alog_c, dtb_c)


def _gdn_scan_kernel(uf, wf, qf, kf, af, gf, ub, wb, qb, kb, ab, gb, s0_ref,
                     of_ref, ob_ref, sfin_ref, s_scr):
    i = pl.program_id(0)
    c = DELTA_CHUNK
    n_batch = s0_ref.shape[0]

    @pl.when(i == 0)
    def _():
        s_scr[...] = s0_ref[...]

    ops = ((uf, wf, qf, kf, af, gf, of_ref), (ub, wb, qb, kb, ab, gb, ob_ref))
    chains = [(bi, d, h) for bi in range(n_batch) for d in range(2) for h in range(GDN_HEADS)]
    states = [s_scr[bi, d * GDN_HEADS + h] for bi, d, h in chains]
    for step in range(CHUNKS_PER_BLOCK):
        def chunk(d):
            cc = step if d == 0 else CHUNKS_PER_BLOCK - 1 - step
            return cc, slice(cc * c, (cc + 1) * c)

        xs = []
        for j, (bi, d, h) in enumerate(chains):
            _, rows = chunk(d)
            cols = slice(h * HEAD_DIM, (h + 1) * HEAD_DIM)
            wq = jnp.concatenate([ops[d][1][bi, rows, cols], ops[d][2][bi, rows, cols]], axis=0)
            xs.append(_dot(wq, states[j].astype(BF16)))
        v_news = []
        for j, (bi, d, h) in enumerate(chains):
            _, rows = chunk(d)
            cols = slice(h * HEAD_DIM, (h + 1) * HEAD_DIM)
            v_news.append((ops[d][0][bi, rows, cols].astype(F32) - xs[j][:c]).astype(BF16))
        for j, (bi, d, h) in enumerate(chains):
            cc, rows = chunk(d)
            cols = slice(h * HEAD_DIM, (h + 1) * HEAD_DIM)
            a_c = ops[d][4][bi, 0, h * c:(h + 1) * c, cc * c:(cc + 1) * c]
            ops[d][6][bi, rows, cols] = (xs[j][c:] + _dot(a_c, v_news[j])).astype(BF16)
            ds = _dot_tn(ops[d][3][bi, rows, cols], v_news[j])
            g_last = ops[d][5][bi, 0, 0, h:h + 1, cc * HEAD_DIM:(cc + 1) * HEAD_DIM]
            states[j] = states[j] * g_last + ds
    for j, (bi, d, h) in enumerate(chains):
        s_scr[bi, d * GDN_HEADS + h] = states[j]

    @pl.when(i == pl.num_programs(0) - 1)
    def _():
        sfin_ref[...] = s_scr[...]


def _gdn_scan(u, w, qd, kd, at, gl, s0):
    b, l, _ = u.shape
    nblk = l // GDN_BLOCK
    half = GDN_HEADS * HEAD_DIM
    fwd = lambda i: (0, i, 0)
    bwd = lambda i: (0, nblk - 1 - i, 1)
    big = lambda m: pl.BlockSpec((b, GDN_BLOCK, half), m)
    att_shape = (b, 1, GDN_HEADS * DELTA_CHUNK, GDN_BLOCK)
    attf = pl.BlockSpec(att_shape, lambda i: (0, i, 0, 0))
    attb = pl.BlockSpec(att_shape, lambda i: (0, nblk - 1 - i, 1, 0))
    gl_shape = (b, 1, 1, GDN_HEADS, CHUNKS_PER_BLOCK * HEAD_DIM)
    glf = pl.BlockSpec(gl_shape, lambda i: (0, i, 0, 0, 0))
    glb = pl.BlockSpec(gl_shape, lambda i: (0, nblk - 1 - i, 1, 0, 0))
    state = pl.BlockSpec((b, N_CHAINS, HEAD_DIM, HEAD_DIM), lambda i: (0, 0, 0, 0))
    return pl.pallas_call(
        _gdn_scan_kernel,
        out_shape=(jax.ShapeDtypeStruct((b, l, half), BF16),
                   jax.ShapeDtypeStruct((b, l, half), BF16),
                   jax.ShapeDtypeStruct((b, N_CHAINS, HEAD_DIM, HEAD_DIM), F32)),
        grid=(nblk,),
        in_specs=[big(fwd), big(fwd), big(fwd), big(fwd), attf, glf,
                  big(bwd), big(bwd), big(bwd), big(bwd), attb, glb, state],
        out_specs=(pl.BlockSpec((b, GDN_BLOCK, half), fwd),
                   pl.BlockSpec((b, GDN_BLOCK, half), lambda i: (0, nblk - 1 - i, 0)),
                   state),
        scratch_shapes=[pltpu.VMEM((b, N_CHAINS, HEAD_DIM, HEAD_DIM), F32)],
        compiler_params=_params("arbitrary"),
        name="gdn_scan",
    )(u, w, qd, kd, at, gl, u, w, qd, kd, at, gl, s0)


def _mix_into(mix_ref, of_ref, ob_ref, z_ref, u_ref, v_ref, gng_ref, lng_ref, lnb_ref, ws_ref,
              bs_ref):
    tm = of_ref.shape[0]
    o = of_ref[...].astype(F32) + ob_ref[...].astype(F32)
    z = z_ref[...].astype(F32)
    for h in range(GDN_HEADS):
        cols = slice(h * HEAD_DIM, (h + 1) * HEAD_DIM)
        oh = o[:, cols]
        zh = z[:, cols]
        r = lax.rsqrt(jnp.mean(oh * oh, -1, keepdims=True) + NORM_EPS)
        mix_ref[:, cols] = (oh * r * gng_ref[...] * (zh * _sigmoid(zh))).astype(BF16)
    for g in range(SGU_GROUPS):
        cols = slice(g * SGU_GROUP, (g + 1) * SGU_GROUP)
        vg = v_ref[:, cols].astype(F32)
        vc = vg - jnp.mean(vg, -1, keepdims=True)
        vn = vc * lax.rsqrt(jnp.mean(vc * vc, -1, keepdims=True) + NORM_EPS)
        vn = (vn * lng_ref[g:g + 1] + lnb_ref[g:g + 1]).astype(BF16)
        wsg = ws_ref[g].astype(BF16)
        for n in range(tm // SGU_CHUNK):
            rows = slice(n * SGU_CHUNK, (n + 1) * SGU_CHUNK)
            s = _dot(wsg, vn[rows]) + bs_ref[g]
            mix_ref[rows, GDN_W + g * SGU_GROUP:GDN_W + (g + 1) * SGU_GROUP] = (
                u_ref[rows, cols].astype(F32) * s).astype(BF16)


def _outproj_kernel(of_ref, ob_ref, z_ref, u_ref, v_ref, gng_ref, lng_ref, lnb_ref, ws_ref, bs_ref,
                    x_ref, mod_ref, ng_ref, wout_ref, rwt_ref, rb_ref,
                    x1_ref, h2_ref, lt_ref, mix_scr):
    subs = _sub_tiles(x_ref.shape[0])
    for sl in subs:
        _mix_into(mix_scr.at[sl], of_ref.at[sl], ob_ref.at[sl], z_ref.at[sl], u_ref.at[sl],
                  v_ref.at[sl], gng_ref, lng_ref, lnb_ref, ws_ref, bs_ref)
    mod = mod_ref[0]
    ys = [_dot(mix_scr[sl], wout_ref[...]) for sl in subs]
    w_hi, w_lo = _split2(rwt_ref[...])
    for sl, y in zip(subs, ys):
        x1 = x_ref[sl] + mod[2:3] * _rms(y, ng_ref[1:2])
        x1_ref[sl] = x1
        h2 = _rms(x1, ng_ref[2:3]) * (1.0 + mod[4:5]) + mod[3:4]
        h_hi, h_lo = _split2(h2)
        _store_planes(h2_ref, _pack_rows(h_hi), sl)
        lt_ref[:, sl] = (_dot_nt(w_hi, h_hi) + _dot_nt(w_lo, h_hi) + _dot_nt(w_hi, h_lo)
                         + rb_ref[...])


def _outproj(o_f, o_b, z, ug, vg, gdn_norm_g, ln_g, ln_b, w_s, b_s_full,
             x2d, mod, rows_per_mod, ng, wout, rwt, rb_col):
    t = x2d.shape[0]
    tm = min(ROW_TILE, t)
    tiles_per_mod = rows_per_mod // tm
    row = lambda i: (i, 0)
    const = lambda i: (0, 0)
    c3 = lambda i: (0, 0, 0)
    return pl.pallas_call(
        _outproj_kernel,
        out_shape=(jax.ShapeDtypeStruct((t, D_MODEL), F32),
                   jax.ShapeDtypeStruct((N_PLANES, t, SC_ROW_W), U32),
                   jax.ShapeDtypeStruct((N_EXPERTS, t), F32)),
        grid=(t // tm,),
        in_specs=[pl.BlockSpec((tm, GDN_W), row), pl.BlockSpec((tm, GDN_W), row),
                  pl.BlockSpec((tm, GDN_W), row), pl.BlockSpec((tm, SGU_W), row),
                  pl.BlockSpec((tm, SGU_W), row),
                  pl.BlockSpec((1, HEAD_DIM), const),
                  pl.BlockSpec((SGU_GROUPS, SGU_GROUP), const),
                  pl.BlockSpec((SGU_GROUPS, SGU_GROUP), const),
                  pl.BlockSpec((SGU_GROUPS, SGU_CHUNK, SGU_CHUNK), c3),
                  pl.BlockSpec((SGU_GROUPS, SGU_CHUNK, SGU_GROUP), c3),
                  pl.BlockSpec((tm, D_MODEL), row),
                  pl.BlockSpec((1, 6, D_MODEL), lambda i: (i // tiles_per_mod, 0, 0)),
                  pl.BlockSpec((4, D_MODEL), const),
                  pl.BlockSpec((D_MODEL, D_MODEL), const),
                  pl.BlockSpec((N_EXPERTS, D_MODEL), const),
                  pl.BlockSpec((N_EXPERTS, 1), const)],
        out_specs=(pl.BlockSpec((tm, D_MODEL), row),
                   pl.BlockSpec((N_PLANES, tm, SC_ROW_W), lambda i: (0, i, 0)),
                   pl.BlockSpec((N_EXPERTS, tm), lambda i: (0, i))),
        compiler_params=_params("parallel"),
        scratch_shapes=[pltpu.VMEM((tm, D_MODEL), BF16)],
        name="out_proj_router",
    )(o_f, o_b, z, ug, vg, gdn_norm_g, ln_g, ln_b, w_s, b_s_full, x2d, mod, ng, wout, rwt, rb_col)


def _moe_kernel(be_ref, slot_ref, next_ref, nreal_ref, nb_ref, xb_ref, wgu_hbm, bgu_ref, wd_hbm,
                bd_ref, y_ref, wgu_f, wd_f, wgu_b, wd_b, gut_scr, sems):
    i = pl.program_id(0)
    live = i < nb_ref[0]
    new_expert = jnp.logical_or(i == 0, be_ref[i] != be_ref[jnp.maximum(i - 1, 0)])

    def weight_copies(expert, slot):
        return (pltpu.make_async_copy(wgu_hbm.at[expert], wgu_f.at[slot], sems.at[slot, 0]),
                pltpu.make_async_copy(wd_hbm.at[expert], wd_f.at[slot], sems.at[slot, 1]))

    @pl.when(i == 0)
    def _():
        for copy in weight_copies(be_ref[0], 0):
            copy.start()

    @pl.when(jnp.logical_and(live, new_expert))
    def _():
        slot = slot_ref[i]
        for copy in weight_copies(be_ref[i], slot):
            copy.wait()

        @pl.when(next_ref[i] >= 0)
        def _():
            for copy in weight_copies(next_ref[i], 1 - slot):
                copy.start()
        wgu_b[...] = wgu_f[slot].astype(BF16)
        wd_b[...] = wd_f[slot].astype(BF16)

    def ffn(first_row, n_rows):
        rows = slice(first_row, first_row + n_rows)
        xb = _unpack_rows(_load_planes(xb_ref, rows)).astype(BF16)
        gu = _dot(xb, wgu_b[...]) + bgu_ref[0]
        gu_t = gu.T
        acts = []
        for part in range(n_rows // LANES):
            part_ref = gut_scr.at[part]
            part_ref[...] = gu_t[:, part * LANES:(part + 1) * LANES]
            gate = jnp.minimum(part_ref[pl.ds(0, D_FF, stride=2), :], SWIGLU_LIMIT)
            up = jnp.clip(part_ref[pl.ds(1, D_FF, stride=2), :], -SWIGLU_LIMIT, SWIGLU_LIMIT)
            acts.append(((up + 1.0) * gate * _sigmoid(SWIGLU_ALPHA * gate)).astype(BF16))
        act_t = jnp.concatenate(acts, axis=1)
        _store_planes(y_ref, _pack_rows(_dot_tn(act_t, wd_b[...]) + bd_ref[0]), rows)

    def zero_rows(first_row, n_rows):
        for p in range(N_PLANES):
            y_ref[p, first_row:first_row + n_rows] = jnp.zeros((n_rows, SC_ROW_W), U32)

    for base in range(0, MOE_ROWS, FFN_ROWS):
        left = nreal_ref[i] - base
        half = FFN_ROWS // 2

        @pl.when(jnp.logical_and(live, left > half))
        def _():
            ffn(base, FFN_ROWS)

        @pl.when(jnp.logical_and(live, jnp.logical_and(left > 0, left <= half)))
        def _():
            ffn(base, half)
            zero_rows(base + half, half)

        @pl.when(jnp.logical_and(live, left <= 0))
        def _():
            zero_rows(base, FFN_ROWS)

    @pl.when(jnp.logical_not(live))
    def _():
        y_ref[...] = jnp.zeros_like(y_ref)


def _moe_ffn(block_e, n_used, real_end, xb, w_gu, b_gu, w_down, b_down):
    n_rows = xb.shape[1]
    n_blocks = n_rows // MOE_ROWS
    idx = jnp.arange(n_blocks, dtype=jnp.int32)
    live = idx < n_used[0]
    changed = jnp.concatenate([jnp.ones((1,), bool), block_e[1:] != block_e[:-1]]) & live
    slot = ((jnp.cumsum(changed.astype(jnp.int32)) - 1) % 2).astype(jnp.int32)
    change_at = jnp.where(changed, idx, n_blocks)
    next_change = lax.cummin(jnp.concatenate([change_at[1:], jnp.full((1,), n_blocks, jnp.int32)]),
                             reverse=True)
    next_e = jnp.where(next_change < n_blocks,
                       block_e[jnp.minimum(next_change, n_blocks - 1)], -1).astype(jnp.int32)
    n_real = jnp.clip(real_end[block_e] - idx * MOE_ROWS, 0, MOE_ROWS).astype(jnp.int32)

    row = lambda i, be, sl, nx, sh, nb: (0, i, 0)
    ex3 = lambda i, be, sl, nx, sh, nb: (be[i], 0, 0)
    live_row = lambda i, be, sl, nx, sh, nb: (0, jnp.minimum(i, nb[0] - 1), 0)
    planes = (N_PLANES, MOE_ROWS, SC_ROW_W)
    grid_spec = pltpu.PrefetchScalarGridSpec(
        num_scalar_prefetch=5,
        grid=(n_blocks,),
        in_specs=[pl.BlockSpec(planes, live_row),
                  pl.BlockSpec(memory_space=pl.ANY),
                  pl.BlockSpec((1, 1, 2 * D_FF), ex3),
                  pl.BlockSpec(memory_space=pl.ANY),
                  pl.BlockSpec((1, 1, D_MODEL), ex3)],
        out_specs=pl.BlockSpec(planes, row),
        scratch_shapes=[pltpu.VMEM((2, D_MODEL, 2 * D_FF), F32),
                        pltpu.VMEM((2, D_FF, D_MODEL), F32),
                        pltpu.VMEM((D_MODEL, 2 * D_FF), BF16),
                        pltpu.VMEM((D_FF, D_MODEL), BF16),
                        pltpu.VMEM((FFN_ROWS // LANES, 2 * D_FF, LANES), F32),
                        pltpu.SemaphoreType.DMA((2, 2))],
    )
    return pl.pallas_call(
        _moe_kernel,
        out_shape=jax.ShapeDtypeStruct((N_PLANES, n_rows, SC_ROW_W), U32),
        grid_spec=grid_spec,
        compiler_params=_params("arbitrary"),
        name="moe_ffn",
    )(block_e, slot, next_e, n_real, n_used, xb, w_gu, b_gu, w_down, b_down)


def _combine_kernel(y0_ref, y1_ref, y2_ref, y3_ref, gt_ref, x1_ref, mod_ref, ng_ref, o_ref):
    mod = mod_ref[0]
    tm = o_ref.shape[0]
    gt = jnp.concatenate([gt_ref[...], jnp.zeros((LANES - SUBLANES, tm), F32)], axis=0).T
    y = _unpack_rows(_load_planes(y0_ref)) * gt[:, 0:1]
    for k, y_ref in ((1, y1_ref), (2, y2_ref), (3, y3_ref)):
        y = y + _unpack_rows(_load_planes(y_ref)) * gt[:, k:k + 1]
    o_ref[...] = x1_ref[...] + mod[5:6] * _rms(y, ng_ref[3:4])


def _combine(yg, gates, x1, mod, rows_per_mod, ng):
    t = x1.shape[0]
    tm = min(COMBINE_TILE, t)
    tiles_per_mod = rows_per_mod // tm
    n_tiles = t // tm
    row = lambda i: (i, 0)
    choice = lambda k: pl.BlockSpec((N_PLANES, tm, SC_ROW_W), lambda i: (0, k * n_tiles + i, 0))
    return pl.pallas_call(
        _combine_kernel,
        out_shape=jax.ShapeDtypeStruct((t, D_MODEL), F32),
        grid=(n_tiles,),
        in_specs=[choice(0), choice(1), choice(2), choice(3),
                  pl.BlockSpec((SUBLANES, tm), lambda i: (0, i)),
                  pl.BlockSpec((tm, D_MODEL), row),
                  pl.BlockSpec((1, 6, D_MODEL), lambda i: (i // tiles_per_mod, 0, 0)),
                  pl.BlockSpec((4, D_MODEL), lambda i: (0, 0))],
        out_specs=pl.BlockSpec((tm, D_MODEL), row),
        compiler_params=_params("parallel"),
        name="moe_combine",
    )(yg, yg, yg, yg, gates, x1, mod, ng)


def _route_kernel(lt_ref, eidx_ref, gate_ref, rank_ref, cnt_ref, carry, earlier):
    i = pl.program_id(0)
    tile = earlier.shape[0]

    @pl.when(i == 0)
    def _():
        carry[...] = jnp.zeros_like(carry)
        ti = lax.broadcasted_iota(jnp.int32, (tile, tile), 0)
        tj = lax.broadcasted_iota(jnp.int32, (tile, tile), 1)
        earlier[...] = jnp.where(ti < tj, 1.0, 0.0).astype(BF16)

    eio = lax.broadcasted_iota(jnp.int32, (N_EXPERTS, tile), 0).astype(F32)
    seen = carry[...]
    for sub in range(lt_ref.shape[1] // tile):
        cols = slice(sub * tile, (sub + 1) * tile)
        logits = lt_ref[:, cols]
        vals, sels = [], []
        for k in range(TOP_K):
            m = jnp.max(logits, axis=0, keepdims=True)
            idx = jnp.min(jnp.where(logits == m, eio, float(N_EXPERTS)), axis=0, keepdims=True)
            sel = eio == idx
            logits = jnp.where(sel, -jnp.inf, logits)
            vals.append(m)
            sels.append(sel)
            eidx_ref[k:k + 1, cols] = idx.astype(jnp.int32)
        exps = [jnp.exp(v - vals[0]) for v in vals]
        denom = exps[0] + exps[1] + exps[2] + exps[3]
        for k in range(TOP_K):
            gate_ref[k:k + 1, cols] = exps[k] / denom
        gate_ref[TOP_K:, cols] = jnp.zeros((SUBLANES - TOP_K, tile), F32)

        member = jnp.where(sels[0] | sels[1] | sels[2] | sels[3], 1.0, 0.0)
        before = _dot(member.astype(BF16), earlier[...]) + seen
        for k in range(TOP_K):
            rank_ref[k:k + 1, cols] = jnp.sum(jnp.where(sels[k], before, 0.0), axis=0,
                                              keepdims=True).astype(jnp.int32)
        seen = seen + jnp.sum(member, axis=1, keepdims=True)
    carry[...] = seen
    cnt_ref[...] = seen.astype(jnp.int32)


def _route(logits_t):
    t = logits_t.shape[1]
    block = min(ROUTE_BLOCK, t)
    tile = min(ROUTE_TILE, block)
    blk = lambda i: (0, i)
    return pl.pallas_call(
        _route_kernel,
        out_shape=(jax.ShapeDtypeStruct((TOP_K, t), jnp.int32),
                   jax.ShapeDtypeStruct((SUBLANES, t), F32),
                   jax.ShapeDtypeStruct((TOP_K, t), jnp.int32),
                   jax.ShapeDtypeStruct((N_EXPERTS, 1), jnp.int32)),
        grid=(t // block,),
        in_specs=[pl.BlockSpec((N_EXPERTS, block), blk)],
        out_specs=(pl.BlockSpec((TOP_K, block), blk), pl.BlockSpec((SUBLANES, block), blk),
                   pl.BlockSpec((TOP_K, block), blk),
                   pl.BlockSpec((N_EXPERTS, 1), lambda i: (0, 0))),
        scratch_shapes=[pltpu.VMEM((N_EXPERTS, 1), F32), pltpu.VMEM((tile, tile), BF16)],
        compiler_params=_params("arbitrary"),
        name="moe_route",
    )(logits_t)


def _slot_tables(eidx, rank, counts, n_blocks):
    padded = (counts + MOE_ROWS - 1) // MOE_ROWS * MOE_ROWS
    pad_end = jnp.cumsum(padded)
    pad_start = pad_end - padded
    experts = jnp.arange(N_EXPERTS, dtype=jnp.int32)
    dest = rank + jnp.sum(jnp.where(eidx[..., None] == experts, pad_start, 0), axis=-1)
    first_row = jnp.arange(n_blocks, dtype=jnp.int32)[:, None] * MOE_ROWS
    block_e = jnp.minimum(jnp.sum((pad_end[None, :] <= first_row).astype(jnp.int32), axis=1),
                          N_EXPERTS - 1)
    n_used = pad_end[-1:] // MOE_ROWS
    real_end = (pad_start + counts).astype(jnp.int32)
    return (dest.astype(jnp.int32), pad_start.astype(jnp.int32), block_e, n_used.astype(jnp.int32),
            real_end)


def _sc_mesh():
    return plsc.VectorSubcoreMesh(core_axis_name="c", subcore_axis_name="s",
                                  num_cores=SC_CORES, num_subcores=SC_SUBCORES)


def _plane_row_ids(rows, rows_per_plane):
    return jnp.concatenate([rows + p * rows_per_plane for p in range(N_PLANES)], axis=-1)


def _sc_gather_rows(table, rows):
    v = table.shape[1]
    idx = _plane_row_ids(rows, v)[None]
    n_all = idx.shape[1]

    @functools.partial(pl.kernel, mesh=_sc_mesh(), name="moe_gather_rows",
                       out_type=jax.ShapeDtypeStruct((n_all, SC_ROW_W), U32))
    def gather(x_hbm, i_hbm, o_hbm):
        def body(i_vmem, o_vmem):
            pltpu.sync_copy(x_hbm.at[i_vmem.at[0]], o_vmem)

        pltpu.emit_pipeline(
            body, grid=(n_all // SC_WINDOW,),
            in_specs=[pl.BlockSpec((1, SC_WINDOW), lambda i: (0, i))],
            out_specs=[pl.BlockSpec((SC_WINDOW, SC_ROW_W), lambda i: (i, 0))],
            core_axis_name=("c", "s"), dimension_semantics=(pltpu.PARALLEL,),
        )(i_hbm, o_hbm)

    return gather(table.reshape(N_PLANES * v, SC_ROW_W), idx).reshape(N_PLANES, -1, SC_ROW_W)


def _sc_scatter_rows(rows, dest, n_out):
    t = rows.shape[1]
    idx = _plane_row_ids(dest, n_out)

    @functools.partial(pl.kernel, mesh=_sc_mesh(), name="moe_scatter_rows", scratch_types=[],
                       out_type=jax.ShapeDtypeStruct((N_PLANES * n_out, SC_ROW_W), U32))
    def scatter(x_hbm, i_hbm, o_hbm):
        def body(x_vmem, i_vmem):
            for k in range(TOP_K):
                pltpu.sync_copy(x_vmem, o_hbm.at[i_vmem.at[k]])

        pltpu.emit_pipeline(
            body, grid=(N_PLANES * t // SC_WINDOW,),
            in_specs=[pl.BlockSpec((SC_WINDOW, SC_ROW_W), lambda i: (i, 0)),
                      pl.BlockSpec((TOP_K, SC_WINDOW), lambda i: (0, i))],
            out_specs=[],
            core_axis_name=("c", "s"), dimension_semantics=(pltpu.PARALLEL,),
        )(x_hbm, i_hbm)

    return scatter(rows.reshape(N_PLANES * t, SC_ROW_W), idx).reshape(N_PLANES, n_out, SC_ROW_W)


def _zero_pad_kernel(cnt_ref, start_ref, xb_in_ref, xb_ref, zero_scr, sem):
    del xb_in_ref
    zero_scr[...] = jnp.zeros_like(zero_scr)

    pieces = [SUBLANES << bit for bit in range((MOE_ROWS // SUBLANES - 1).bit_length())]

    def zero_copy(p, row, size):
        return pltpu.make_async_copy(zero_scr.at[pl.ds(0, size)], xb_ref.at[p, pl.ds(row, size)], sem)

    def for_each_piece(fn):
        def per_expert(e, carry):
            n_real = cnt_ref[e]
            n_pad = (MOE_ROWS - n_real % MOE_ROWS) % MOE_ROWS
            first = start_ref[e] + n_real
            n_single = n_pad % SUBLANES
            for j in range(SUBLANES - 1):
                @pl.when(j < n_single)
                def _():
                    for p in range(N_PLANES):
                        fn(zero_copy(p, first + j, 1))
            row = first + n_single
            for size in pieces:
                @pl.when((n_pad & size) != 0)
                def _():
                    for p in range(N_PLANES):
                        fn(zero_copy(p, pl.multiple_of(row, SUBLANES), size))
                row = row + (n_pad & size)
            return carry
        lax.fori_loop(0, N_EXPERTS, per_expert, 0)

    for_each_piece(lambda copy: copy.start())
    for_each_piece(lambda copy: copy.wait())


def _zero_pad_slots(counts, pad_start, xb):
    grid_spec = pltpu.PrefetchScalarGridSpec(
        num_scalar_prefetch=2,
        grid=(1,),
        in_specs=[pl.BlockSpec(memory_space=pl.ANY)],
        out_specs=pl.BlockSpec(memory_space=pl.ANY),
        scratch_shapes=[pltpu.VMEM((MOE_ROWS // 2, SC_ROW_W), U32), pltpu.SemaphoreType.DMA],
    )
    return pl.pallas_call(
        _zero_pad_kernel,
        out_shape=jax.ShapeDtypeStruct(xb.shape, xb.dtype),
        grid_spec=grid_spec,
        input_output_aliases={2: 0},
        compiler_params=_params("arbitrary"),
        name="moe_zero_pad",
    )(counts, pad_start, xb)


def kernel(x, c, ctx, c_ctx, w_ada, b_ada, norm_g, w_in, conv_w, a_log, dt_bias, gdn_norm_g,
           sgu_ln_g, sgu_ln_b, sgu_w, sgu_b, w_out, router_w, router_b, w_gu, b_gu, w_down, b_down):
    b, l, d = x.shape
    lc = ctx.shape[1]
    t = b * l
    assert d == D_MODEL and l % ROW_TILE == 0 and l % GDN_BLOCK == 0 and lc % GDN_BLOCK == 0
    assert w_ada.shape[0] == 1, "single-layer block"

    cs = jnp.concatenate([c, c_ctx[None], jnp.zeros((8 - b - 1, d), F32)], axis=0)
    mod_all = _ada(cs, w_ada[0], b_ada[0][None])
    mod = mod_all[:b].reshape(b, 6, d)
    mod_c = mod_all[b:b + 1].reshape(1, 6, d)
    ng = norm_g[0]


    x2d = x.reshape(t, d)
    qkv, z, ug, vg, ba, bat = _inproj(x2d, mod, l, ng[0:1], w_in[0])
    ctx2d = ctx.reshape(b * lc, d)
    qkv_c, _, _, _, ba_c, bat_c = _inproj(ctx2d, mod_c, b * lc, ng[0:1], w_in[0])

    alog = a_log[0].reshape(-1)
    dtb = dt_bias[0].reshape(-1)
    pc = _gdn_prep(qkv_c.reshape(b, lc, QKV_COLS), conv_w[0], ba_c.reshape(b, lc, N_GATE_COLS),
                   bat_c, alog, dtb, lc)
    s_zero = jnp.zeros((b, N_CHAINS, HEAD_DIM, HEAD_DIM), F32)
    _, _, s_ctx = _gdn_scan(*pc, s_zero)
    pp = _gdn_prep(qkv.reshape(b, l, QKV_COLS), conv_w[0], ba.reshape(b, l, N_GATE_COLS),
                   bat, alog, dtb, GRID_W)
    o_f, o_b, _ = _gdn_scan(*pp, s_ctx)

    b_s_full = jnp.broadcast_to(sgu_b[0][:, :, None], (SGU_GROUPS, SGU_CHUNK, SGU_GROUP))
    x1, h2p, logits_t = _outproj(o_f.reshape(t, GDN_W), o_b.reshape(t, GDN_W), z, ug, vg,
                                 gdn_norm_g, sgu_ln_g[0], sgu_ln_b[0], sgu_w[0], b_s_full,
                                 x2d, mod, l, ng, w_out[0].astype(BF16),
                                 router_w[0].T, router_b[0][:, None])

    eidx, gates_t, rank, counts = _route(logits_t)
    counts = counts[:, 0]
    n_blocks = -(-(t * TOP_K) // MOE_ROWS) + N_EXPERTS
    dest, pad_start, block_e, n_used, real_end = _slot_tables(eidx, rank, counts, n_blocks)
    xb = _sc_scatter_rows(h2p, dest, n_blocks * MOE_ROWS)
    xb = _zero_pad_slots(counts, pad_start, xb)
    yb = _moe_ffn(block_e, n_used, real_end, xb, w_gu[0], b_gu[0][:, None, :], w_down[0],
                  b_down[0][:, None, :])
    yg = _sc_gather_rows(yb, dest.reshape(-1))
    out = _combine(yg, gates_t, x1, mod, l, ng)
    return out.reshape(b, l, d)
```

```python
import functools
import math

import jax
import jax.numpy as jnp
from jax import lax
from jax.experimental import pallas as pl
from jax.experimental.pallas import tpu as pltpu
from jax.experimental.pallas import tpu_sc as plsc

F32 = jnp.float32
BF16 = jnp.bfloat16

D_MODEL = 1024
GDN_HEADS = 4
HEAD_DIM = 128
GDN_W = GDN_HEADS * HEAD_DIM
SGU_GROUPS = 4
SGU_GROUP = 128
SGU_W = SGU_GROUPS * SGU_GROUP
SGU_CHUNK = 128
DELTA_CHUNK = 64
GRID_W = 64
N_EXPERTS = 32
TOP_K = 4
D_FF = 1024
SWIGLU_LIMIT = 7.0
SWIGLU_ALPHA = 1.702
NORM_EPS = 1e-6
QKV_COLS = 3 * GDN_W
N_CHAINS = 2 * GDN_HEADS
N_GATE_COLS = 2 * N_CHAINS

ROW_TILE = 1024
SUB_ROWS = 256
COMBINE_TILE = 1024
ADA_COLS = 3 * D_MODEL
GDN_BLOCK = 256
CHUNKS_PER_BLOCK = GDN_BLOCK // DELTA_CHUNK
SCAN_BLOCKS_PER_STEP = 2
N_PAIR_LEVELS = DELTA_CHUNK.bit_length() - 1
MOE_ROWS = 512
U32 = jnp.uint32
LANES = 128
SUBLANES = 8
PACKED_W = D_MODEL // 2
ROUTE_BLOCK = 2048
ROUTE_TILE = 512
SC_CORES = 2
SC_SUBCORES = 16
SC_WINDOW = 128
N_PLANES = 2
SC_ROW_W = PACKED_W // N_PLANES
VMEM_LIMIT = 56 * 1024 * 1024


def _params(*sem):
    return pltpu.CompilerParams(dimension_semantics=sem, vmem_limit_bytes=VMEM_LIMIT)


def _dot(a, b):
    return jnp.dot(a, b, preferred_element_type=F32)


def _dot_nt(a, b):
    return lax.dot_general(a, b, (((1,), (1,)), ((), ())), preferred_element_type=F32)


def _dot_tn(a, b):
    return lax.dot_general(a, b, (((0,), (0,)), ((), ())), preferred_element_type=F32)


def _split2(a):
    hi = a.astype(BF16)
    lo = (a - hi.astype(F32)).astype(BF16)
    return hi, lo


def _split3(a):
    hi = a.astype(BF16)
    r = a - hi.astype(F32)
    mid = r.astype(BF16)
    lo = (r - mid.astype(F32)).astype(BF16)
    return hi, mid, lo


def _pack_rows(x):
    xb = x.astype(BF16).astype(F32)
    hi = lax.bitcast_convert_type(xb[:, :PACKED_W], U32)
    lo = lax.bitcast_convert_type(xb[:, PACKED_W:], U32)
    return hi | (lo >> 16)


def _store_planes(ref, packed, rows=slice(None)):
    for p in range(N_PLANES):
        ref[p, rows] = packed[:, p * SC_ROW_W:(p + 1) * SC_ROW_W]


def _load_planes(ref, rows=slice(None)):
    return jnp.concatenate([ref[p, rows] for p in range(N_PLANES)], axis=1)


def _sub_tiles(n_rows):
    return [slice(r, r + SUB_ROWS) for r in range(0, n_rows, SUB_ROWS)]


def _unpack_rows(w):
    hi = lax.bitcast_convert_type(w & jnp.uint32(0xFFFF0000), F32)
    lo = lax.bitcast_convert_type(w << 16, F32)
    return jnp.concatenate([hi, lo], axis=1)


def _rms(x32, g):
    return x32 * lax.rsqrt(jnp.mean(x32 * x32, -1, keepdims=True) + NORM_EPS) * g


def _gelu_tanh(x):
    c = math.sqrt(2.0 / math.pi)
    return 0.5 * x * (1.0 + jnp.tanh(c * (x + 0.044715 * (x * x * x))))


def _sigmoid(x):
    return 1.0 / (1.0 + jnp.exp(-x))


def _softplus(x):
    return jnp.maximum(x, 0.0) + jnp.log(1.0 + jnp.exp(-jnp.abs(x)))


def _ada_kernel(c_ref, w_ref, b_ref, o_ref):
    c = c_ref[...]
    s = c * _sigmoid(c)
    s_hi, s_lo = _split2(s)
    w_hi, w_lo = _split2(w_ref[...])
    o_ref[...] = _dot(s_hi, w_hi) + _dot(s_lo, w_hi) + _dot(s_hi, w_lo) + b_ref[...]


def _ada(cs, w_ada, b_ada):
    n = w_ada.shape[1]
    bn = ADA_COLS
    return pl.pallas_call(
        _ada_kernel,
        out_shape=jax.ShapeDtypeStruct((cs.shape[0], n), F32),
        grid=(n // bn,),
        in_specs=[pl.BlockSpec(cs.shape, lambda j: (0, 0)),
                  pl.BlockSpec((D_MODEL, bn), lambda j: (0, j)),
                  pl.BlockSpec((1, bn), lambda j: (0, j))],
        out_specs=pl.BlockSpec((cs.shape[0], bn), lambda j: (0, j)),
        compiler_params=_params("parallel"),
        name="ada_mod",
    )(cs, w_ada, b_ada)


def _inproj_kernel(x_ref, mod_ref, g_ref, w_hbm,
                   qkv_ref, z_ref, u_ref, v_ref, ba_ref, bat_ref,
                   w_f32, wqkv_ref, wzuv_ref, wba_ref, sem):
    @pl.when(pl.program_id(0) == 0)
    def _():
        copy = pltpu.make_async_copy(w_hbm, w_f32, sem)
        copy.start()
        copy.wait()
        wqkv_ref[...] = w_f32[:, :QKV_COLS].astype(BF16)
        wzuv_ref[...] = w_f32[:, QKV_COLS + N_GATE_COLS:].astype(BF16)
        lane = lax.broadcasted_iota(jnp.int32, (D_MODEL, LANES), 1)
        w_gate = jnp.where(lane < N_GATE_COLS, w_f32[:, QKV_COLS:QKV_COLS + LANES], 0.0)
        w_hi, w_lo = _split2(w_gate)
        wba_ref[:, :LANES] = w_hi
        wba_ref[:, LANES:] = w_lo

    mod = mod_ref[0]
    subs = _sub_tiles(x_ref.shape[0])
    hs = [_split2(_rms(x_ref[sl], g_ref[...]) * (1.0 + mod[1:2]) + mod[0:1]) for sl in subs]
    qkvs = [_dot(h_hi, wqkv_ref[...]) for h_hi, _ in hs]
    zuvs = [_dot(h_hi, wzuv_ref[...]) for h_hi, _ in hs]
    bas = []
    for h_hi, h_lo in hs:
        both = _dot(h_hi, wba_ref[...])
        bas.append(both[:, :LANES] + both[:, LANES:] + _dot(h_lo, wba_ref[:, :LANES]))
    for sl, qkv, zuv, ba in zip(subs, qkvs, zuvs, bas):
        qkv_ref[sl] = qkv.astype(BF16)
        z_ref[sl] = zuv[:, :GDN_W].astype(BF16)
        u_ref[sl] = _gelu_tanh(zuv[:, GDN_W:GDN_W + SGU_W]).astype(BF16)
        v_ref[sl] = _gelu_tanh(zuv[:, GDN_W + SGU_W:]).astype(BF16)
        ba_ref[sl] = ba[:, :N_GATE_COLS]
        bat_ref[:, sl] = ba.T[:N_GATE_COLS]


def _inproj(x2d, mod, rows_per_mod, ng0, w_in):
    t = x2d.shape[0]
    tm = min(ROW_TILE, t)
    tiles_per_mod = rows_per_mod // tm
    row = lambda i: (i, 0)
    const = lambda i: (0, 0)
    return pl.pallas_call(
        _inproj_kernel,
        out_shape=(jax.ShapeDtypeStruct((t, QKV_COLS), BF16),
                   jax.ShapeDtypeStruct((t, GDN_W), BF16),
                   jax.ShapeDtypeStruct((t, SGU_W), BF16),
                   jax.ShapeDtypeStruct((t, SGU_W), BF16),
                   jax.ShapeDtypeStruct((t, N_GATE_COLS), F32),
                   jax.ShapeDtypeStruct((N_GATE_COLS, t), F32)),
        grid=(t // tm,),
        in_specs=[pl.BlockSpec((tm, D_MODEL), row),
                  pl.BlockSpec((1, 6, D_MODEL), lambda i: (i // tiles_per_mod, 0, 0)),
                  pl.BlockSpec((1, D_MODEL), const),
                  pl.BlockSpec(memory_space=pl.ANY)],
        out_specs=(pl.BlockSpec((tm, QKV_COLS), row),
                   pl.BlockSpec((tm, GDN_W), row),
                   pl.BlockSpec((tm, SGU_W), row),
                   pl.BlockSpec((tm, SGU_W), row),
                   pl.BlockSpec((tm, N_GATE_COLS), row),
                   pl.BlockSpec((N_GATE_COLS, tm), lambda i: (0, i))),
        scratch_shapes=[pltpu.VMEM(w_in.shape, F32),
                        pltpu.VMEM((D_MODEL, QKV_COLS), BF16),
                        pltpu.VMEM((D_MODEL, w_in.shape[1] - QKV_COLS - N_GATE_COLS), BF16),
                        pltpu.VMEM((D_MODEL, 2 * LANES), BF16),
                        pltpu.SemaphoreType.DMA],
        compiler_params=_params("arbitrary"),
        name="in_proj",
    )(x2d, mod, ng0, w_in)


def _gdn_prep_kernel(row_len, qkv_ref, cw_ref, ba_ref, bat_ref, alog_r_ref, dtb_r_ref,
                     alog_c_ref, dtb_c_ref, u_ref, w_ref, qd_ref, kd_ref, at_ref, gl_ref,
                     tri_scr, pair_scr, spread_scr):
    n = GDN_BLOCK
    c = DELTA_CHUNK

    def mask01(m):
        return jnp.where(m, 1.0, 0.0).astype(BF16)

    wi = lax.broadcasted_iota(jnp.int32, (c, n), 0)
    wl = lax.broadcasted_iota(jnp.int32, (c, n), 1)
    wchunk = wl // c
    wj = wl % c
    lower_w = wi >= wj
    upper_w = wi <= wj
    diag_w = wi == wj
    eye_w = jnp.where(diag_w, 1.0, 0.0)

    @pl.when(jnp.logical_and(pl.program_id(0) == 0, pl.program_id(1) == 0))
    def _():
        ri = lax.broadcasted_iota(jnp.int32, (n, n), 0)
        ci = lax.broadcasted_iota(jnp.int32, (n, n), 1)
        same = (ri // c) == (ci // c)
        tri_scr[0] = mask01(same & (ri >= ci))
        tri_scr[1] = mask01(same & (ri <= ci))
        tri_scr[2] = mask01(same)
        for m in range(N_PAIR_LEVELS):
            s = 1 << m
            pair_scr[m] = mask01(((wi // (2 * s)) == (wj // (2 * s))) & ((wi // s) != (wj // s)))
        ei = lax.broadcasted_iota(jnp.int32, (n, CHUNKS_PER_BLOCK * HEAD_DIM), 0) // c
        ej = lax.broadcasted_iota(jnp.int32, (n, CHUNKS_PER_BLOCK * HEAD_DIM), 1) // HEAD_DIM
        spread_scr[...] = mask01(ei == ej)

    lower_b = tri_scr[0]
    upper_b = tri_scr[1]
    same_b = tri_scr[2]
    pair_masks = [pair_scr[m] for m in range(N_PAIR_LEVELS)]

    def to_wide(full):
        out = full[:c]
        for k in range(1, CHUNKS_PER_BLOCK):
            out = jnp.where(wchunk == k, full[k * c:(k + 1) * c], out)
        return out

    def col_wide(col):
        out = jnp.broadcast_to(col[:c], (c, n))
        for k in range(1, CHUNKS_PER_BLOCK):
            out = jnp.where(wchunk == k, jnp.broadcast_to(col[k * c:(k + 1) * c], (c, n)), out)
        return out

    def block_diag(x_w):
        return jnp.concatenate([x_w] * CHUNKS_PER_BLOCK, axis=0) * same_b

    ba = ba_ref[0]
    bat = bat_ref[...]
    beta_c = _sigmoid(ba[:, :N_CHAINS])
    g_c = -jnp.exp(alog_r_ref[...]) * _softplus(ba[:, N_CHAINS:] + dtb_r_ref[...])
    g_r = -jnp.exp(alog_c_ref[...]) * _softplus(bat[N_CHAINS:] + dtb_c_ref[...])
    gc3 = _split3(g_c)
    gr3 = jnp.concatenate(_split3(g_r), axis=0)

    def sum3_r(m):
        return m[:N_CHAINS] + m[N_CHAINS:2 * N_CHAINS] + m[2 * N_CHAINS:]

    cum_f_c = _dot(lower_b, gc3[0]) + _dot(lower_b, gc3[1]) + _dot(lower_b, gc3[2])
    tot_c = _dot(same_b, gc3[0]) + _dot(same_b, gc3[1]) + _dot(same_b, gc3[2])
    cum_b_c = tot_c - cum_f_c + g_c
    cum_f_r = sum3_r(_dot(gr3, upper_b))
    cum_b_r = sum3_r(_dot(gr3, lower_b))
    g_last = jnp.exp(sum3_r(_dot(gr3, spread_scr[...])))
    gl_ref[0, 0, 0] = g_last[:GDN_HEADS]
    gl_ref[0, 0, 1] = g_last[GDN_HEADS:]

    pos = lax.broadcasted_iota(jnp.int32, (n, HEAD_DIM), 0) % row_len
    first = pos == 0
    last = pos == row_len - 1

    def conv_silu(col):
        x = qkv_ref[0, :, col * HEAD_DIM:(col + 1) * HEAD_DIM].astype(F32)
        cw = cw_ref[:, col * HEAD_DIM:(col + 1) * HEAD_DIM]
        xp = jnp.where(first, 0.0, pltpu.roll(x, 1, 0))
        xn = jnp.where(last, 0.0, pltpu.roll(x, n - 1, 0))
        y = xp * cw[0:1] + x * cw[1:2] + xn * cw[2:3]
        return y * _sigmoid(y)

    def l2n(x):
        return x * lax.rsqrt(jnp.sum(x * x, -1, keepdims=True) + NORM_EPS)

    a_bs, ps, rhss = [None] * N_CHAINS, [None] * N_CHAINS, [None] * N_CHAINS
    first_pairs = pair_masks[0].astype(F32)
    for h in range(GDN_HEADS):
        q = l2n(conv_silu(h)) * (HEAD_DIM ** -0.5)
        k = l2n(conv_silu(GDN_HEADS + h))
        v = conv_silu(2 * GDN_HEADS + h)
        k_b = k.astype(BF16)
        qk_kk = _dot_nt(jnp.concatenate([q.astype(BF16), k_b], axis=0), k_b)
        qk_w = to_wide(qk_kk[:n])
        kk_w = to_wide(qk_kk[n:])
        for d in range(2):
            j = d * GDN_HEADS + h
            mask_w = lower_w if d == 0 else upper_w
            cum_c = (cum_f_c if d == 0 else cum_b_c)[:, j:j + 1]
            cum_r = (cum_f_r if d == 0 else cum_b_r)[j:j + 1, :]
            b_c = beta_c[:, j:j + 1]
            decay_w = jnp.where(mask_w, jnp.exp(jnp.where(mask_w, col_wide(cum_c) - cum_r, 0.0)), 0.0)
            amat_w = jnp.where(diag_w, 0.0, kk_w * decay_w * col_wide(b_c))
            a_bs[j] = amat_w.astype(BF16)
            ps[j] = eye_w - amat_w * first_pairs
            e_c = jnp.exp(cum_c)
            rhss[j] = jnp.concatenate([(v * b_c).astype(BF16), (k * (b_c * e_c)).astype(BF16)], axis=1)
            cols = slice(j * HEAD_DIM, (j + 1) * HEAD_DIM)
            qd_ref[0, :, cols] = (q * e_c).astype(BF16)
            kd_ref[0, :, cols] = (k * jnp.exp(tot_c[:, j:j + 1] - cum_c)).astype(BF16)
            at_ref[0, 0, j * c:(j + 1) * c, :] = (qk_w * decay_w).astype(BF16)

    for pm in pair_masks[1:]:
        p_bs = [p.astype(BF16) for p in ps]
        ys = [_dot(a_bs[j] * pm, block_diag(p_bs[j])) for j in range(N_CHAINS)]
        ps = [ps[j] - _dot(p_bs[j], block_diag(ys[j].astype(BF16))) for j in range(N_CHAINS)]

    for j in range(N_CHAINS):
        uw = _dot(block_diag(ps[j].astype(BF16)), rhss[j])
        cols = slice(j * HEAD_DIM, (j + 1) * HEAD_DIM)
        u_ref[0, :, cols] = uw[:, :HEAD_DIM].astype(BF16)
        w_ref[0, :, cols] = uw[:, HEAD_DIM:].astype(BF16)


def _gdn_prep(qkv, conv_w, ba, bat, alog, dtb, row_len):
    b, l, _ = qkv.shape
    nblk = l // GDN_BLOCK
    wide = N_CHAINS * HEAD_DIM
    blk = lambda bi, i: (bi, i, 0)
    const = lambda bi, i: (0, 0)
    alog_r, dtb_r = alog.reshape(1, N_CHAINS), dtb.reshape(1, N_CHAINS)
    alog_c, dtb_c = alog.reshape(N_CHAINS, 1), dtb.reshape(N_CHAINS, 1)
    return pl.pallas_call(
        functools.partial(_gdn_prep_kernel, row_len),
        out_shape=(jax.ShapeDtypeStruct((b, l, wide), BF16),) * 4 + (
            jax.ShapeDtypeStruct((b, nblk, N_CHAINS * DELTA_CHUNK, GDN_BLOCK), BF16),
            jax.ShapeDtypeStruct((b, nblk, 2, GDN_HEADS, CHUNKS_PER_BLOCK * HEAD_DIM), F32)),
        grid=(b, nblk),
        in_specs=[pl.BlockSpec((1, GDN_BLOCK, QKV_COLS), blk),
                  pl.BlockSpec((3, QKV_COLS), const),
                  pl.BlockSpec((1, GDN_BLOCK, N_GATE_COLS), blk),
                  pl.BlockSpec((N_GATE_COLS, GDN_BLOCK), lambda bi, i: (0, bi * nblk + i)),
                  pl.BlockSpec((1, N_CHAINS), const),
                  pl.BlockSpec((1, N_CHAINS), const),
                  pl.BlockSpec((N_CHAINS, 1), const),
                  pl.BlockSpec((N_CHAINS, 1), const)],
        out_specs=(pl.BlockSpec((1, GDN_BLOCK, wide), blk),) * 4 + (
            pl.BlockSpec((1, 1, N_CHAINS * DELTA_CHUNK, GDN_BLOCK), lambda bi, i: (bi, i, 0, 0)),
            pl.BlockSpec((1, 1, 2, GDN_HEADS, CHUNKS_PER_BLOCK * HEAD_DIM),
                         lambda bi, i: (bi, i, 0, 0, 0))),
        scratch_shapes=[pltpu.VMEM((3, GDN_BLOCK, GDN_BLOCK), BF16),
                        pltpu.VMEM((N_PAIR_LEVELS, DELTA_CHUNK, GDN_BLOCK), BF16),
                        pltpu.VMEM((GDN_BLOCK, CHUNKS_PER_BLOCK * HEAD_DIM), BF16)],
        compiler_params=_params("arbitrary", "arbitrary"),
        name="gdn_prep",
    )(qkv, conv_w, ba, bat, alog_r, dtb_r, alog_c, dtb_c)


def _gdn_scan_kernel(uf, wf, qf, kf, af, gf, ub, wb, qb, kb, ab, gb, s0_ref,
                     of_ref, ob_ref, sfin_ref, s_scr):
    i = pl.program_id(0)
    c = DELTA_CHUNK
    n_batch = s0_ref.shape[0]

    @pl.when(i == 0)
    def _():
        s_scr[...] = s0_ref[...]

    ops = ((uf, wf, qf, kf, af, gf, of_ref), (ub, wb, qb, kb, ab, gb, ob_ref))
    chains = [(bi, d, h) for bi in range(n_batch) for d in range(2) for h in range(GDN_HEADS)]
    states = [s_scr[bi, d * GDN_HEADS + h] for bi, d, h in chains]
    n_chunks = uf.shape[1] // c
    for step in range(n_chunks):
        def chunk(d):
            cc = step if d == 0 else n_chunks - 1 - step
            return cc, slice(cc * c, (cc + 1) * c)

        xs = []
        for j, (bi, d, h) in enumerate(chains):
            _, rows = chunk(d)
            cols = slice(h * HEAD_DIM, (h + 1) * HEAD_DIM)
            wq = jnp.concatenate([ops[d][1][bi, rows, cols], ops[d][2][bi, rows, cols]], axis=0)
            xs.append(_dot(wq, states[j].astype(BF16)))
        v_news = []
        for j, (bi, d, h) in enumerate(chains):
            _, rows = chunk(d)
            cols = slice(h * HEAD_DIM, (h + 1) * HEAD_DIM)
            v_news.append((ops[d][0][bi, rows, cols].astype(F32) - xs[j][:c]).astype(BF16))
        for j, (bi, d, h) in enumerate(chains):
            cc, rows = chunk(d)
            cols = slice(h * HEAD_DIM, (h + 1) * HEAD_DIM)
            blk, lc = divmod(cc, CHUNKS_PER_BLOCK)
            a_c = ops[d][4][bi, blk, h * c:(h + 1) * c, lc * c:(lc + 1) * c]
            ops[d][6][bi, rows, cols] = (xs[j][c:] + _dot(a_c, v_news[j])).astype(BF16)
            ds = _dot_tn(ops[d][3][bi, rows, cols], v_news[j])
            g_last = ops[d][5][bi, blk, 0, h:h + 1, lc * HEAD_DIM:(lc + 1) * HEAD_DIM]
            states[j] = states[j] * g_last + ds
    for j, (bi, d, h) in enumerate(chains):
        s_scr[bi, d * GDN_HEADS + h] = states[j]

    @pl.when(i == pl.num_programs(0) - 1)
    def _():
        sfin_ref[...] = s_scr[...]


def _gdn_scan(u, w, qd, kd, at, gl, s0):
    b, l, _ = u.shape
    n_sub = SCAN_BLOCKS_PER_STEP if (l // GDN_BLOCK) % SCAN_BLOCKS_PER_STEP == 0 else 1
    nblk = l // (n_sub * GDN_BLOCK)
    step_rows = n_sub * GDN_BLOCK
    half = GDN_HEADS * HEAD_DIM
    fwd = lambda i: (0, i, 0)
    bwd = lambda i: (0, nblk - 1 - i, 1)
    big = lambda m: pl.BlockSpec((b, step_rows, half), m)
    att_shape = (b, n_sub, GDN_HEADS * DELTA_CHUNK, GDN_BLOCK)
    attf = pl.BlockSpec(att_shape, lambda i: (0, i, 0, 0))
    attb = pl.BlockSpec(att_shape, lambda i: (0, nblk - 1 - i, 1, 0))
    gl_shape = (b, n_sub, 1, GDN_HEADS, CHUNKS_PER_BLOCK * HEAD_DIM)
    glf = pl.BlockSpec(gl_shape, lambda i: (0, i, 0, 0, 0))
    glb = pl.BlockSpec(gl_shape, lambda i: (0, nblk - 1 - i, 1, 0, 0))
    state = pl.BlockSpec((b, N_CHAINS, HEAD_DIM, HEAD_DIM), lambda i: (0, 0, 0, 0))
    return pl.pallas_call(
        _gdn_scan_kernel,
        out_shape=(jax.ShapeDtypeStruct((b, l, half), BF16),
                   jax.ShapeDtypeStruct((b, l, half), BF16),
                   jax.ShapeDtypeStruct((b, N_CHAINS, HEAD_DIM, HEAD_DIM), F32)),
        grid=(nblk,),
        in_specs=[big(fwd), big(fwd), big(fwd), big(fwd), attf, glf,
                  big(bwd), big(bwd), big(bwd), big(bwd), attb, glb, state],
        out_specs=(pl.BlockSpec((b, step_rows, half), fwd),
                   pl.BlockSpec((b, step_rows, half), lambda i: (0, nblk - 1 - i, 0)),
                   state),
        scratch_shapes=[pltpu.VMEM((b, N_CHAINS, HEAD_DIM, HEAD_DIM), F32)],
        compiler_params=_params("arbitrary"),
        name="gdn_scan",
    )(u, w, qd, kd, at, gl, u, w, qd, kd, at, gl, s0)


def _mix_into(mix_ref, of_ref, ob_ref, z_ref, u_ref, v_ref, gng_ref, lng_ref, lnb_ref, ws_ref,
              bs_ref):
    tm = of_ref.shape[0]
    o = of_ref[...].astype(F32) + ob_ref[...].astype(F32)
    z = z_ref[...].astype(F32)
    for h in range(GDN_HEADS):
        cols = slice(h * HEAD_DIM, (h + 1) * HEAD_DIM)
        oh = o[:, cols]
        zh = z[:, cols]
        r = lax.rsqrt(jnp.mean(oh * oh, -1, keepdims=True) + NORM_EPS)
        mix_ref[:, cols] = (oh * r * gng_ref[...] * (zh * _sigmoid(zh))).astype(BF16)
    for g in range(SGU_GROUPS):
        cols = slice(g * SGU_GROUP, (g + 1) * SGU_GROUP)
        vg = v_ref[:, cols].astype(F32)
        vc = vg - jnp.mean(vg, -1, keepdims=True)
        vn = vc * lax.rsqrt(jnp.mean(vc * vc, -1, keepdims=True) + NORM_EPS)
        vn = (vn * lng_ref[g:g + 1] + lnb_ref[g:g + 1]).astype(BF16)
        wsg = ws_ref[g].astype(BF16)
        for n in range(tm // SGU_CHUNK):
            rows = slice(n * SGU_CHUNK, (n + 1) * SGU_CHUNK)
            s = _dot(wsg, vn[rows]) + bs_ref[g]
            mix_ref[rows, GDN_W + g * SGU_GROUP:GDN_W + (g + 1) * SGU_GROUP] = (
                u_ref[rows, cols].astype(F32) * s).astype(BF16)


def _outproj_kernel(of_ref, ob_ref, z_ref, u_ref, v_ref, gng_ref, lng_ref, lnb_ref, ws_ref, bs_ref,
                    x_ref, mod_ref, ng_ref, wout_ref, rwt_ref, rb_ref,
                    x1_ref, h2_ref, lt_ref, mix_scr):
    subs = _sub_tiles(x_ref.shape[0])
    for sl in subs:
        _mix_into(mix_scr.at[sl], of_ref.at[sl], ob_ref.at[sl], z_ref.at[sl], u_ref.at[sl],
                  v_ref.at[sl], gng_ref, lng_ref, lnb_ref, ws_ref, bs_ref)
    mod = mod_ref[0]
    ys = [_dot(mix_scr[sl], wout_ref[...]) for sl in subs]
    w_hi, w_lo = _split2(rwt_ref[...])
    for sl, y in zip(subs, ys):
        x1 = x_ref[sl] + mod[2:3] * _rms(y, ng_ref[1:2])
        x1_ref[sl] = x1
        h2 = _rms(x1, ng_ref[2:3]) * (1.0 + mod[4:5]) + mod[3:4]
        h_hi, h_lo = _split2(h2)
        _store_planes(h2_ref, _pack_rows(h_hi), sl)
        lt_ref[:, sl] = (_dot_nt(w_hi, h_hi) + _dot_nt(w_lo, h_hi) + _dot_nt(w_hi, h_lo)
                         + rb_ref[...])


def _outproj(o_f, o_b, z, ug, vg, gdn_norm_g, ln_g, ln_b, w_s, b_s_full,
             x2d, mod, rows_per_mod, ng, wout, rwt, rb_col):
    t = x2d.shape[0]
    tm = min(ROW_TILE, t)
    tiles_per_mod = rows_per_mod // tm
    row = lambda i: (i, 0)
    const = lambda i: (0, 0)
    c3 = lambda i: (0, 0, 0)
    return pl.pallas_call(
        _outproj_kernel,
        out_shape=(jax.ShapeDtypeStruct((t, D_MODEL), F32),
                   jax.ShapeDtypeStruct((N_PLANES, t, SC_ROW_W), U32),
                   jax.ShapeDtypeStruct((N_EXPERTS, t), F32)),
        grid=(t // tm,),
        in_specs=[pl.BlockSpec((tm, GDN_W), row), pl.BlockSpec((tm, GDN_W), row),
                  pl.BlockSpec((tm, GDN_W), row), pl.BlockSpec((tm, SGU_W), row),
                  pl.BlockSpec((tm, SGU_W), row),
                  pl.BlockSpec((1, HEAD_DIM), const),
                  pl.BlockSpec((SGU_GROUPS, SGU_GROUP), const),
                  pl.BlockSpec((SGU_GROUPS, SGU_GROUP), const),
                  pl.BlockSpec((SGU_GROUPS, SGU_CHUNK, SGU_CHUNK), c3),
                  pl.BlockSpec((SGU_GROUPS, SGU_CHUNK, SGU_GROUP), c3),
                  pl.BlockSpec((tm, D_MODEL), row),
                  pl.BlockSpec((1, 6, D_MODEL), lambda i: (i // tiles_per_mod, 0, 0)),
                  pl.BlockSpec((4, D_MODEL), const),
                  pl.BlockSpec((D_MODEL, D_MODEL), const),
                  pl.BlockSpec((N_EXPERTS, D_MODEL), const),
                  pl.BlockSpec((N_EXPERTS, 1), const)],
        out_specs=(pl.BlockSpec((tm, D_MODEL), row),
                   pl.BlockSpec((N_PLANES, tm, SC_ROW_W), lambda i: (0, i, 0)),
                   pl.BlockSpec((N_EXPERTS, tm), lambda i: (0, i))),
        compiler_params=_params("parallel"),
        scratch_shapes=[pltpu.VMEM((tm, D_MODEL), BF16)],
        name="out_proj_router",
    )(o_f, o_b, z, ug, vg, gdn_norm_g, ln_g, ln_b, w_s, b_s_full, x2d, mod, ng, wout, rwt, rb_col)


def _moe_kernel(be_ref, slot_ref, next_ref, short_ref, nb_ref, xb_ref, wgu_hbm, bgu_ref, wd_hbm,
                bd_ref, y_ref, wgu_f, wd_f, wgu_b, wd_b, gut_scr, sems):
    i = pl.program_id(0)
    live = i < nb_ref[0]
    new_expert = jnp.logical_or(i == 0, be_ref[i] != be_ref[jnp.maximum(i - 1, 0)])

    def weight_copies(expert, slot):
        return (pltpu.make_async_copy(wgu_hbm.at[expert], wgu_f.at[slot], sems.at[slot, 0]),
                pltpu.make_async_copy(wd_hbm.at[expert], wd_f.at[slot], sems.at[slot, 1]))

    @pl.when(i == 0)
    def _():
        for copy in weight_copies(be_ref[0], 0):
            copy.start()

    @pl.when(jnp.logical_and(live, new_expert))
    def _():
        slot = slot_ref[i]
        for copy in weight_copies(be_ref[i], slot):
            copy.wait()

        @pl.when(next_ref[i] >= 0)
        def _():
            for copy in weight_copies(next_ref[i], 1 - slot):
                copy.start()
        wgu_b[...] = wgu_f[slot].astype(BF16)
        wd_b[...] = wd_f[slot].astype(BF16)

    def ffn(n_rows):
        rows = slice(0, n_rows)
        xb = _unpack_rows(_load_planes(xb_ref, rows)).astype(BF16)
        gu = _dot(xb, wgu_b[...]) + bgu_ref[0]
        gu_t = gu.T
        acts = []
        for part in range(n_rows // LANES):
            part_ref = gut_scr.at[part]
            part_ref[...] = gu_t[:, part * LANES:(part + 1) * LANES]
            gate = jnp.minimum(part_ref[pl.ds(0, D_FF, stride=2), :], SWIGLU_LIMIT)
            up = jnp.clip(part_ref[pl.ds(1, D_FF, stride=2), :], -SWIGLU_LIMIT, SWIGLU_LIMIT)
            acts.append(((up + 1.0) * gate * _sigmoid(SWIGLU_ALPHA * gate)).astype(BF16))
        act_t = jnp.concatenate(acts, axis=1)
        _store_planes(y_ref, _pack_rows(_dot_tn(act_t, wd_b[...]) + bd_ref[0]), rows)

    short = short_ref[i] != 0

    @pl.when(jnp.logical_and(live, jnp.logical_not(short)))
    def _():
        ffn(MOE_ROWS)

    @pl.when(jnp.logical_and(live, short))
    def _():
        ffn(MOE_ROWS // 2)
        for p in range(N_PLANES):
            y_ref[p, MOE_ROWS // 2:] = jnp.zeros((MOE_ROWS // 2, SC_ROW_W), U32)

    @pl.when(jnp.logical_not(live))
    def _():
        y_ref[...] = jnp.zeros_like(y_ref)


def _moe_ffn(block_e, n_used, real_end, xb, w_gu, b_gu, w_down, b_down):
    n_rows = xb.shape[1]
    n_blocks = n_rows // MOE_ROWS
    idx = jnp.arange(n_blocks, dtype=jnp.int32)
    live = idx < n_used[0]
    changed = jnp.concatenate([jnp.ones((1,), bool), block_e[1:] != block_e[:-1]]) & live
    slot = ((jnp.cumsum(changed.astype(jnp.int32)) - 1) % 2).astype(jnp.int32)
    change_at = jnp.where(changed, idx, n_blocks)
    next_change = lax.cummin(jnp.concatenate([change_at[1:], jnp.full((1,), n_blocks, jnp.int32)]),
                             reverse=True)
    next_e = jnp.where(next_change < n_blocks,
                       block_e[jnp.minimum(next_change, n_blocks - 1)], -1).astype(jnp.int32)
    n_real = jnp.clip(real_end[block_e] - idx * MOE_ROWS, 0, MOE_ROWS)
    short = (n_real <= MOE_ROWS // 2).astype(jnp.int32)

    row = lambda i, be, sl, nx, sh, nb: (0, i, 0)
    ex3 = lambda i, be, sl, nx, sh, nb: (be[i], 0, 0)
    live_row = lambda i, be, sl, nx, sh, nb: (0, jnp.minimum(i, nb[0] - 1), 0)
    planes = (N_PLANES, MOE_ROWS, SC_ROW_W)
    grid_spec = pltpu.PrefetchScalarGridSpec(
        num_scalar_prefetch=5,
        grid=(n_blocks,),
        in_specs=[pl.BlockSpec(planes, live_row),
                  pl.BlockSpec(memory_space=pl.ANY),
                  pl.BlockSpec((1, 1, 2 * D_FF), ex3),
                  pl.BlockSpec(memory_space=pl.ANY),
                  pl.BlockSpec((1, 1, D_MODEL), ex3)],
        out_specs=pl.BlockSpec(planes, row),
        scratch_shapes=[pltpu.VMEM((2, D_MODEL, 2 * D_FF), F32),
                        pltpu.VMEM((2, D_FF, D_MODEL), F32),
                        pltpu.VMEM((D_MODEL, 2 * D_FF), BF16),
                        pltpu.VMEM((D_FF, D_MODEL), BF16),
                        pltpu.VMEM((MOE_ROWS // LANES, 2 * D_FF, LANES), F32),
                        pltpu.SemaphoreType.DMA((2, 2))],
    )
    return pl.pallas_call(
        _moe_kernel,
        out_shape=jax.ShapeDtypeStruct((N_PLANES, n_rows, SC_ROW_W), U32),
        grid_spec=grid_spec,
        compiler_params=_params("arbitrary"),
        name="moe_ffn",
    )(block_e, slot, next_e, short, n_used, xb, w_gu, b_gu, w_down, b_down)


def _combine_kernel(y0_ref, y1_ref, y2_ref, y3_ref, gt_ref, x1_ref, mod_ref, ng_ref, o_ref):
    mod = mod_ref[0]
    tm = o_ref.shape[0]
    gt = jnp.concatenate([gt_ref[...], jnp.zeros((LANES - SUBLANES, tm), F32)], axis=0).T
    y = _unpack_rows(_load_planes(y0_ref)) * gt[:, 0:1]
    for k, y_ref in ((1, y1_ref), (2, y2_ref), (3, y3_ref)):
        y = y + _unpack_rows(_load_planes(y_ref)) * gt[:, k:k + 1]
    o_ref[...] = x1_ref[...] + mod[5:6] * _rms(y, ng_ref[3:4])


def _combine(yg, gates, x1, mod, rows_per_mod, ng):
    t = x1.shape[0]
    tm = min(COMBINE_TILE, t)
    tiles_per_mod = rows_per_mod // tm
    n_tiles = t // tm
    row = lambda i: (i, 0)
    choice = lambda k: pl.BlockSpec((N_PLANES, tm, SC_ROW_W), lambda i: (0, k * n_tiles + i, 0))
    return pl.pallas_call(
        _combine_kernel,
        out_shape=jax.ShapeDtypeStruct((t, D_MODEL), F32),
        grid=(n_tiles,),
        in_specs=[choice(0), choice(1), choice(2), choice(3),
                  pl.BlockSpec((SUBLANES, tm), lambda i: (0, i)),
                  pl.BlockSpec((tm, D_MODEL), row),
                  pl.BlockSpec((1, 6, D_MODEL), lambda i: (i // tiles_per_mod, 0, 0)),
                  pl.BlockSpec((4, D_MODEL), lambda i: (0, 0))],
        out_specs=pl.BlockSpec((tm, D_MODEL), row),
        compiler_params=_params("parallel"),
        name="moe_combine",
    )(yg, yg, yg, yg, gates, x1, mod, ng)


def _route_kernel(lt_ref, eidx_ref, gate_ref, rank_ref, cnt_ref, carry, earlier):
    i = pl.program_id(0)
    tile = earlier.shape[0]

    @pl.when(i == 0)
    def _():
        carry[...] = jnp.zeros_like(carry)
        ti = lax.broadcasted_iota(jnp.int32, (tile, tile), 0)
        tj = lax.broadcasted_iota(jnp.int32, (tile, tile), 1)
        earlier[...] = jnp.where(ti < tj, 1.0, 0.0).astype(BF16)

    eio = lax.broadcasted_iota(jnp.int32, (N_EXPERTS, tile), 0).astype(F32)
    seen = carry[...]
    for sub in range(lt_ref.shape[1] // tile):
        cols = slice(sub * tile, (sub + 1) * tile)
        logits = lt_ref[:, cols]
        vals, sels = [], []
        for k in range(TOP_K):
            m = jnp.max(logits, axis=0, keepdims=True)
            idx = jnp.min(jnp.where(logits == m, eio, float(N_EXPERTS)), axis=0, keepdims=True)
            sel = eio == idx
            logits = jnp.where(sel, -jnp.inf, logits)
            vals.append(m)
            sels.append(sel)
            eidx_ref[k:k + 1, cols] = idx.astype(jnp.int32)
        exps = [jnp.exp(v - vals[0]) for v in vals]
        denom = exps[0] + exps[1] + exps[2] + exps[3]
        for k in range(TOP_K):
            gate_ref[k:k + 1, cols] = exps[k] / denom
        gate_ref[TOP_K:, cols] = jnp.zeros((SUBLANES - TOP_K, tile), F32)

        member = jnp.where(sels[0] | sels[1] | sels[2] | sels[3], 1.0, 0.0)
        before = _dot(member.astype(BF16), earlier[...]) + seen
        for k in range(TOP_K):
            rank_ref[k:k + 1, cols] = jnp.sum(jnp.where(sels[k], before, 0.0), axis=0,
                                              keepdims=True).astype(jnp.int32)
        seen = seen + jnp.sum(member, axis=1, keepdims=True)
    carry[...] = seen
    cnt_ref[...] = seen.astype(jnp.int32)


def _route(logits_t):
    t = logits_t.shape[1]
    block = min(ROUTE_BLOCK, t)
    tile = min(ROUTE_TILE, block)
    blk = lambda i: (0, i)
    return pl.pallas_call(
        _route_kernel,
        out_shape=(jax.ShapeDtypeStruct((TOP_K, t), jnp.int32),
                   jax.ShapeDtypeStruct((SUBLANES, t), F32),
                   jax.ShapeDtypeStruct((TOP_K, t), jnp.int32),
                   jax.ShapeDtypeStruct((N_EXPERTS, 1), jnp.int32)),
        grid=(t // block,),
        in_specs=[pl.BlockSpec((N_EXPERTS, block), blk)],
        out_specs=(pl.BlockSpec((TOP_K, block), blk), pl.BlockSpec((SUBLANES, block), blk),
                   pl.BlockSpec((TOP_K, block), blk),
                   pl.BlockSpec((N_EXPERTS, 1), lambda i: (0, 0))),
        scratch_shapes=[pltpu.VMEM((N_EXPERTS, 1), F32), pltpu.VMEM((tile, tile), BF16)],
        compiler_params=_params("arbitrary"),
        name="moe_route",
    )(logits_t)


def _slot_tables(eidx, rank, counts, n_blocks):
    padded = (counts + MOE_ROWS - 1) // MOE_ROWS * MOE_ROWS
    pad_end = jnp.cumsum(padded)
    pad_start = pad_end - padded
    experts = jnp.arange(N_EXPERTS, dtype=jnp.int32)
    dest = rank + jnp.sum(jnp.where(eidx[..., None] == experts, pad_start, 0), axis=-1)
    first_row = jnp.arange(n_blocks, dtype=jnp.int32)[:, None] * MOE_ROWS
    block_e = jnp.minimum(jnp.sum((pad_end[None, :] <= first_row).astype(jnp.int32), axis=1),
                          N_EXPERTS - 1)
    n_used = pad_end[-1:] // MOE_ROWS
    real_end = (pad_start + counts).astype(jnp.int32)
    return (dest.astype(jnp.int32), pad_start.astype(jnp.int32), block_e, n_used.astype(jnp.int32),
            real_end)


def _sc_mesh():
    return plsc.VectorSubcoreMesh(core_axis_name="c", subcore_axis_name="s",
                                  num_cores=SC_CORES, num_subcores=SC_SUBCORES)


def _plane_row_ids(rows, rows_per_plane):
    return jnp.concatenate([rows + p * rows_per_plane for p in range(N_PLANES)], axis=-1)


def _sc_gather_rows(table, rows):
    v = table.shape[1]
    idx = _plane_row_ids(rows, v)[None]
    n_all = idx.shape[1]

    @functools.partial(pl.kernel, mesh=_sc_mesh(), name="moe_gather_rows",
                       out_type=jax.ShapeDtypeStruct((n_all, SC_ROW_W), U32))
    def gather(x_hbm, i_hbm, o_hbm):
        def body(i_vmem, o_vmem):
            pltpu.sync_copy(x_hbm.at[i_vmem.at[0]], o_vmem)

        pltpu.emit_pipeline(
            body, grid=(n_all // SC_WINDOW,),
            in_specs=[pl.BlockSpec((1, SC_WINDOW), lambda i: (0, i))],
            out_specs=[pl.BlockSpec((SC_WINDOW, SC_ROW_W), lambda i: (i, 0))],
            core_axis_name=("c", "s"), dimension_semantics=(pltpu.PARALLEL,),
        )(i_hbm, o_hbm)

    return gather(table.reshape(N_PLANES * v, SC_ROW_W), idx).reshape(N_PLANES, -1, SC_ROW_W)


def _sc_scatter_rows(rows, dest, n_out):
    t = rows.shape[1]
    idx = _plane_row_ids(dest, n_out)

    @functools.partial(pl.kernel, mesh=_sc_mesh(), name="moe_scatter_rows", scratch_types=[],
                       out_type=jax.ShapeDtypeStruct((N_PLANES * n_out, SC_ROW_W), U32))
    def scatter(x_hbm, i_hbm, o_hbm):
        def body(x_vmem, i_vmem):
            for k in range(TOP_K):
                pltpu.sync_copy(x_vmem, o_hbm.at[i_vmem.at[k]])

        pltpu.emit_pipeline(
            body, grid=(N_PLANES * t // SC_WINDOW,),
            in_specs=[pl.BlockSpec((SC_WINDOW, SC_ROW_W), lambda i: (i, 0)),
                      pl.BlockSpec((TOP_K, SC_WINDOW), lambda i: (0, i))],
            out_specs=[],
            core_axis_name=("c", "s"), dimension_semantics=(pltpu.PARALLEL,),
        )(x_hbm, i_hbm)

    return scatter(rows.reshape(N_PLANES * t, SC_ROW_W), idx).reshape(N_PLANES, n_out, SC_ROW_W)


def _zero_pad_kernel(cnt_ref, start_ref, xb_in_ref, xb_ref, zero_scr, sem):
    del xb_in_ref
    zero_scr[...] = jnp.zeros_like(zero_scr)

    pieces = [SUBLANES << bit for bit in range((MOE_ROWS // SUBLANES - 1).bit_length())]

    def zero_copy(p, row, size):
        return pltpu.make_async_copy(zero_scr.at[pl.ds(0, size)], xb_ref.at[p, pl.ds(row, size)], sem)

    def for_each_piece(fn):
        def per_expert(e, carry):
            n_real = cnt_ref[e]
            n_pad = (MOE_ROWS - n_real % MOE_ROWS) % MOE_ROWS
            first = start_ref[e] + n_real
            n_single = n_pad % SUBLANES
            for j in range(SUBLANES - 1):
                @pl.when(j < n_single)
                def _():
                    for p in range(N_PLANES):
                        fn(zero_copy(p, first + j, 1))
            row = first + n_single
            for size in pieces:
                @pl.when((n_pad & size) != 0)
                def _():
                    for p in range(N_PLANES):
                        fn(zero_copy(p, pl.multiple_of(row, SUBLANES), size))
                row = row + (n_pad & size)
            return carry
        lax.fori_loop(0, N_EXPERTS, per_expert, 0)

    for_each_piece(lambda copy: copy.start())
    for_each_piece(lambda copy: copy.wait())


def _zero_pad_slots(counts, pad_start, xb):
    grid_spec = pltpu.PrefetchScalarGridSpec(
        num_scalar_prefetch=2,
        grid=(1,),
        in_specs=[pl.BlockSpec(memory_space=pl.ANY)],
        out_specs=pl.BlockSpec(memory_space=pl.ANY),
        scratch_shapes=[pltpu.VMEM((MOE_ROWS // 2, SC_ROW_W), U32), pltpu.SemaphoreType.DMA],
    )
    return pl.pallas_call(
        _zero_pad_kernel,
        out_shape=jax.ShapeDtypeStruct(xb.shape, xb.dtype),
        grid_spec=grid_spec,
        input_output_aliases={2: 0},
        compiler_params=_params("arbitrary"),
        name="moe_zero_pad",
    )(counts, pad_start, xb)


def kernel(x, c, ctx, c_ctx, w_ada, b_ada, norm_g, w_in, conv_w, a_log, dt_bias, gdn_norm_g,
           sgu_ln_g, sgu_ln_b, sgu_w, sgu_b, w_out, router_w, router_b, w_gu, b_gu, w_down, b_down):
    b, l, d = x.shape
    lc = ctx.shape[1]
    t = b * l
    assert d == D_MODEL and l % ROW_TILE == 0 and l % GDN_BLOCK == 0 and lc % GDN_BLOCK == 0
    assert w_ada.shape[0] == 1, "single-layer block"

    cs = jnp.concatenate([c, c_ctx[None], jnp.zeros((8 - b - 1, d), F32)], axis=0)
    mod_all = _ada(cs, w_ada[0], b_ada[0][None])
    mod = mod_all[:b].reshape(b, 6, d)
    mod_c = mod_all[b:b + 1].reshape(1, 6, d)
    ng = norm_g[0]


    x2d = x.reshape(t, d)
    qkv, z, ug, vg, ba, bat = _inproj(x2d, mod, l, ng[0:1], w_in[0])
    ctx2d = ctx.reshape(b * lc, d)
    qkv_c, _, _, _, ba_c, bat_c = _inproj(ctx2d, mod_c, b * lc, ng[0:1], w_in[0])

    alog = a_log[0].reshape(-1)
    dtb = dt_bias[0].reshape(-1)
    pc = _gdn_prep(qkv_c.reshape(b, lc, QKV_COLS), conv_w[0], ba_c.reshape(b, lc, N_GATE_COLS),
                   bat_c, alog, dtb, lc)
    s_zero = jnp.zeros((b, N_CHAINS, HEAD_DIM, HEAD_DIM), F32)
    _, _, s_ctx = _gdn_scan(*pc, s_zero)
    pp = _gdn_prep(qkv.reshape(b, l, QKV_COLS), conv_w[0], ba.reshape(b, l, N_GATE_COLS),
                   bat, alog, dtb, GRID_W)
    o_f, o_b, _ = _gdn_scan(*pp, s_ctx)

    b_s_full = jnp.broadcast_to(sgu_b[0][:, :, None], (SGU_GROUPS, SGU_CHUNK, SGU_GROUP))
    x1, h2p, logits_t = _outproj(o_f.reshape(t, GDN_W), o_b.reshape(t, GDN_W), z, ug, vg,
                                 gdn_norm_g, sgu_ln_g[0], sgu_ln_b[0], sgu_w[0], b_s_full,
                                 x2d, mod, l, ng, w_out[0].astype(BF16),
                                 router_w[0].T, router_b[0][:, None])

    eidx, gates_t, rank, counts = _route(logits_t)
    counts = counts[:, 0]
    n_blocks = -(-(t * TOP_K) // MOE_ROWS) + N_EXPERTS
    dest, pad_start, block_e, n_used, real_end = _slot_tables(eidx, rank, counts, n_blocks)
    xb = _sc_scatter_rows(h2p, dest, n_blocks * MOE_ROWS)
    xb = _zero_pad_slots(counts, pad_start, xb)
    yb = _moe_ffn(block_e, n_used, real_end, xb, w_gu[0], b_gu[0][:, None, :], w_down[0],
                  b_down[0][:, None, :])
    yg = _sc_gather_rows(yb, dest.reshape(-1))
    out = _combine(yg, gates_t, x1, mod, l, ng)
    return out.reshape(b, l, d)
```

```python
import functools
import math

import jax
import jax.numpy as jnp
from jax import lax
from jax.experimental import pallas as pl
from jax.experimental.pallas import tpu as pltpu
from jax.experimental.pallas import tpu_sc as plsc

F32 = jnp.float32
BF16 = jnp.bfloat16

D_MODEL = 1024
GDN_HEADS = 4
HEAD_DIM = 128
GDN_W = GDN_HEADS * HEAD_DIM
SGU_GROUPS = 4
SGU_GROUP = 128
SGU_W = SGU_GROUPS * SGU_GROUP
SGU_CHUNK = 128
DELTA_CHUNK = 64
GRID_W = 64
N_EXPERTS = 32
TOP_K = 4
D_FF = 1024
SWIGLU_LIMIT = 7.0
SWIGLU_ALPHA = 1.702
NORM_EPS = 1e-6
QKV_COLS = 3 * GDN_W
N_CHAINS = 2 * GDN_HEADS
N_GATE_COLS = 2 * N_CHAINS

ROW_TILE = 1024
SUB_ROWS = 256
COMBINE_TILE = 1024
ADA_COLS = 3 * D_MODEL
GDN_BLOCK = 256
CHUNKS_PER_BLOCK = GDN_BLOCK // DELTA_CHUNK
SCAN_BLOCKS_PER_STEP = 4
N_PAIR_LEVELS = DELTA_CHUNK.bit_length() - 1
MOE_ROWS = 512
U32 = jnp.uint32
LANES = 128
SUBLANES = 8
PACKED_W = D_MODEL // 2
ROUTE_BLOCK = 2048
ROUTE_TILE = 512
SC_CORES = 2
SC_SUBCORES = 16
SC_WINDOW = 128
N_PLANES = 2
SC_ROW_W = PACKED_W // N_PLANES
VMEM_LIMIT = 56 * 1024 * 1024


def _params(*sem):
    return pltpu.CompilerParams(dimension_semantics=sem, vmem_limit_bytes=VMEM_LIMIT)


def _dot(a, b):
    return jnp.dot(a, b, preferred_element_type=F32)


def _dot_nt(a, b):
    return lax.dot_general(a, b, (((1,), (1,)), ((), ())), preferred_element_type=F32)


def _dot_tn(a, b):
    return lax.dot_general(a, b, (((0,), (0,)), ((), ())), preferred_element_type=F32)


def _split2(a):
    hi = a.astype(BF16)
    lo = (a - hi.astype(F32)).astype(BF16)
    return hi, lo


def _split3(a):
    hi = a.astype(BF16)
    r = a - hi.astype(F32)
    mid = r.astype(BF16)
    lo = (r - mid.astype(F32)).astype(BF16)
    return hi, mid, lo


def _pack_rows(x):
    xb = x.astype(BF16).astype(F32)
    hi = lax.bitcast_convert_type(xb[:, :PACKED_W], U32)
    lo = lax.bitcast_convert_type(xb[:, PACKED_W:], U32)
    return hi | (lo >> 16)


def _store_planes(ref, packed, rows=slice(None)):
    for p in range(N_PLANES):
        ref[p, rows] = packed[:, p * SC_ROW_W:(p + 1) * SC_ROW_W]


def _load_planes(ref, rows=slice(None)):
    return jnp.concatenate([ref[p, rows] for p in range(N_PLANES)], axis=1)


def _sub_tiles(n_rows):
    return [slice(r, r + SUB_ROWS) for r in range(0, n_rows, SUB_ROWS)]


def _unpack_rows(w):
    hi = lax.bitcast_convert_type(w & jnp.uint32(0xFFFF0000), F32)
    lo = lax.bitcast_convert_type(w << 16, F32)
    return jnp.concatenate([hi, lo], axis=1)


def _rms(x32, g):
    return x32 * lax.rsqrt(jnp.mean(x32 * x32, -1, keepdims=True) + NORM_EPS) * g


def _gelu_tanh(x):
    c = math.sqrt(2.0 / math.pi)
    return 0.5 * x * (1.0 + jnp.tanh(c * (x + 0.044715 * (x * x * x))))


def _sigmoid(x):
    return 1.0 / (1.0 + jnp.exp(-x))


def _softplus(x):
    return jnp.maximum(x, 0.0) + jnp.log(1.0 + jnp.exp(-jnp.abs(x)))


def _ada_kernel(c_ref, w_ref, b_ref, o_ref):
    c = c_ref[...]
    s = c * _sigmoid(c)
    s_hi, s_lo = _split2(s)
    w_hi, w_lo = _split2(w_ref[...])
    o_ref[...] = _dot(s_hi, w_hi) + _dot(s_lo, w_hi) + _dot(s_hi, w_lo) + b_ref[...]


def _ada(cs, w_ada, b_ada):
    n = w_ada.shape[1]
    bn = ADA_COLS
    return pl.pallas_call(
        _ada_kernel,
        out_shape=jax.ShapeDtypeStruct((cs.shape[0], n), F32),
        grid=(n // bn,),
        in_specs=[pl.BlockSpec(cs.shape, lambda j: (0, 0)),
                  pl.BlockSpec((D_MODEL, bn), lambda j: (0, j)),
                  pl.BlockSpec((1, bn), lambda j: (0, j))],
        out_specs=pl.BlockSpec((cs.shape[0], bn), lambda j: (0, j)),
        compiler_params=_params("parallel"),
        name="ada_mod",
    )(cs, w_ada, b_ada)


def _inproj_kernel(x_ref, mod_ref, g_ref, w_hbm,
                   qkv_ref, z_ref, u_ref, v_ref, ba_ref, bat_ref,
                   w_f32, wqkv_ref, wzuv_ref, wba_ref, sem):
    @pl.when(pl.program_id(0) == 0)
    def _():
        copy = pltpu.make_async_copy(w_hbm, w_f32, sem)
        copy.start()
        copy.wait()
        wqkv_ref[...] = w_f32[:, :QKV_COLS].astype(BF16)
        wzuv_ref[...] = w_f32[:, QKV_COLS + N_GATE_COLS:].astype(BF16)
        lane = lax.broadcasted_iota(jnp.int32, (D_MODEL, LANES), 1)
        w_gate = jnp.where(lane < N_GATE_COLS, w_f32[:, QKV_COLS:QKV_COLS + LANES], 0.0)
        w_hi, w_lo = _split2(w_gate)
        wba_ref[:, :LANES] = w_hi
        wba_ref[:, LANES:] = w_lo

    mod = mod_ref[0]
    subs = _sub_tiles(x_ref.shape[0])
    hs = [_split2(_rms(x_ref[sl], g_ref[...]) * (1.0 + mod[1:2]) + mod[0:1]) for sl in subs]
    qkvs = [_dot(h_hi, wqkv_ref[...]) for h_hi, _ in hs]
    zuvs = [_dot(h_hi, wzuv_ref[...]) for h_hi, _ in hs]
    bas = []
    for h_hi, h_lo in hs:
        both = _dot(h_hi, wba_ref[...])
        bas.append(both[:, :LANES] + both[:, LANES:] + _dot(h_lo, wba_ref[:, :LANES]))
    for sl, qkv, zuv, ba in zip(subs, qkvs, zuvs, bas):
        qkv_ref[sl] = qkv.astype(BF16)
        z_ref[sl] = zuv[:, :GDN_W].astype(BF16)
        u_ref[sl] = _gelu_tanh(zuv[:, GDN_W:GDN_W + SGU_W]).astype(BF16)
        v_ref[sl] = _gelu_tanh(zuv[:, GDN_W + SGU_W:]).astype(BF16)
        ba_ref[sl] = ba[:, :N_GATE_COLS]
        bat_ref[:, sl] = ba.T[:N_GATE_COLS]


def _inproj(x2d, mod, rows_per_mod, ng0, w_in):
    t = x2d.shape[0]
    tm = min(ROW_TILE, t)
    tiles_per_mod = rows_per_mod // tm
    row = lambda i: (i, 0)
    const = lambda i: (0, 0)
    return pl.pallas_call(
        _inproj_kernel,
        out_shape=(jax.ShapeDtypeStruct((t, QKV_COLS), BF16),
                   jax.ShapeDtypeStruct((t, GDN_W), BF16),
                   jax.ShapeDtypeStruct((t, SGU_W), BF16),
                   jax.ShapeDtypeStruct((t, SGU_W), BF16),
                   jax.ShapeDtypeStruct((t, N_GATE_COLS), F32),
                   jax.ShapeDtypeStruct((N_GATE_COLS, t), F32)),
        grid=(t // tm,),
        in_specs=[pl.BlockSpec((tm, D_MODEL), row),
                  pl.BlockSpec((1, 6, D_MODEL), lambda i: (i // tiles_per_mod, 0, 0)),
                  pl.BlockSpec((1, D_MODEL), const),
                  pl.BlockSpec(memory_space=pl.ANY)],
        out_specs=(pl.BlockSpec((tm, QKV_COLS), row),
                   pl.BlockSpec((tm, GDN_W), row),
                   pl.BlockSpec((tm, SGU_W), row),
                   pl.BlockSpec((tm, SGU_W), row),
                   pl.BlockSpec((tm, N_GATE_COLS), row),
                   pl.BlockSpec((N_GATE_COLS, tm), lambda i: (0, i))),
        scratch_shapes=[pltpu.VMEM(w_in.shape, F32),
                        pltpu.VMEM((D_MODEL, QKV_COLS), BF16),
                        pltpu.VMEM((D_MODEL, w_in.shape[1] - QKV_COLS - N_GATE_COLS), BF16),
                        pltpu.VMEM((D_MODEL, 2 * LANES), BF16),
                        pltpu.SemaphoreType.DMA],
        compiler_params=_params("arbitrary"),
        name="in_proj",
    )(x2d, mod, ng0, w_in)


def _gdn_prep_kernel(row_len, qkv_ref, cw_ref, ba_ref, bat_ref, alog_r_ref, dtb_r_ref,
                     alog_c_ref, dtb_c_ref, u_ref, w_ref, qd_ref, kd_ref, at_ref, gl_ref,
                     tri_scr, pair_scr, spread_scr):
    n = GDN_BLOCK
    c = DELTA_CHUNK

    def mask01(m):
        return jnp.where(m, 1.0, 0.0).astype(BF16)

    wi = lax.broadcasted_iota(jnp.int32, (c, n), 0)
    wl = lax.broadcasted_iota(jnp.int32, (c, n), 1)
    wchunk = wl // c
    wj = wl % c
    lower_w = wi >= wj
    upper_w = wi <= wj
    diag_w = wi == wj
    eye_w = jnp.where(diag_w, 1.0, 0.0)

    @pl.when(jnp.logical_and(pl.program_id(0) == 0, pl.program_id(1) == 0))
    def _():
        ri = lax.broadcasted_iota(jnp.int32, (n, n), 0)
        ci = lax.broadcasted_iota(jnp.int32, (n, n), 1)
        same = (ri // c) == (ci // c)
        tri_scr[0] = mask01(same & (ri >= ci))
        tri_scr[1] = mask01(same & (ri <= ci))
        tri_scr[2] = mask01(same)
        for m in range(N_PAIR_LEVELS):
            s = 1 << m
            pair_scr[m] = mask01(((wi // (2 * s)) == (wj // (2 * s))) & ((wi // s) != (wj // s)))
        ei = lax.broadcasted_iota(jnp.int32, (n, CHUNKS_PER_BLOCK * HEAD_DIM), 0) // c
        ej = lax.broadcasted_iota(jnp.int32, (n, CHUNKS_PER_BLOCK * HEAD_DIM), 1) // HEAD_DIM
        spread_scr[...] = mask01(ei == ej)

    lower_b = tri_scr[0]
    upper_b = tri_scr[1]
    same_b = tri_scr[2]
    pair_masks = [pair_scr[m] for m in range(N_PAIR_LEVELS)]

    def to_wide(full):
        out = full[:c]
        for k in range(1, CHUNKS_PER_BLOCK):
            out = jnp.where(wchunk == k, full[k * c:(k + 1) * c], out)
        return out

    def col_wide(col):
        out = jnp.broadcast_to(col[:c], (c, n))
        for k in range(1, CHUNKS_PER_BLOCK):
            out = jnp.where(wchunk == k, jnp.broadcast_to(col[k * c:(k + 1) * c], (c, n)), out)
        return out

    def block_diag(x_w):
        return jnp.concatenate([x_w] * CHUNKS_PER_BLOCK, axis=0) * same_b

    ba = ba_ref[0]
    bat = bat_ref[...]
    beta_c = _sigmoid(ba[:, :N_CHAINS])
    g_c = -jnp.exp(alog_r_ref[...]) * _softplus(ba[:, N_CHAINS:] + dtb_r_ref[...])
    g_r = -jnp.exp(alog_c_ref[...]) * _softplus(bat[N_CHAINS:] + dtb_c_ref[...])
    gc3 = _split3(g_c)
    gr3 = jnp.concatenate(_split3(g_r), axis=0)

    def sum3_r(m):
        return m[:N_CHAINS] + m[N_CHAINS:2 * N_CHAINS] + m[2 * N_CHAINS:]

    cum_f_c = _dot(lower_b, gc3[0]) + _dot(lower_b, gc3[1]) + _dot(lower_b, gc3[2])
    tot_c = _dot(same_b, gc3[0]) + _dot(same_b, gc3[1]) + _dot(same_b, gc3[2])
    cum_b_c = tot_c - cum_f_c + g_c
    cum_f_r = sum3_r(_dot(gr3, upper_b))
    cum_b_r = sum3_r(_dot(gr3, lower_b))
    g_last = jnp.exp(sum3_r(_dot(gr3, spread_scr[...])))
    gl_ref[0, 0, 0] = g_last[:GDN_HEADS]
    gl_ref[0, 0, 1] = g_last[GDN_HEADS:]

    pos = lax.broadcasted_iota(jnp.int32, (n, HEAD_DIM), 0) % row_len
    first = pos == 0
    last = pos == row_len - 1

    def conv_silu(col):
        x = qkv_ref[0, :, col * HEAD_DIM:(col + 1) * HEAD_DIM].astype(F32)
        cw = cw_ref[:, col * HEAD_DIM:(col + 1) * HEAD_DIM]
        xp = jnp.where(first, 0.0, pltpu.roll(x, 1, 0))
        xn = jnp.where(last, 0.0, pltpu.roll(x, n - 1, 0))
        y = xp * cw[0:1] + x * cw[1:2] + xn * cw[2:3]
        return y * _sigmoid(y)

    def l2n(x):
        return x * lax.rsqrt(jnp.sum(x * x, -1, keepdims=True) + NORM_EPS)

    a_bs, ps, rhss = [None] * N_CHAINS, [None] * N_CHAINS, [None] * N_CHAINS
    first_pairs = pair_masks[0].astype(F32)
    for h in range(GDN_HEADS):
        q = l2n(conv_silu(h)) * (HEAD_DIM ** -0.5)
        k = l2n(conv_silu(GDN_HEADS + h))
        v = conv_silu(2 * GDN_HEADS + h)
        k_b = k.astype(BF16)
        qk_kk = _dot_nt(jnp.concatenate([q.astype(BF16), k_b], axis=0), k_b)
        qk_w = to_wide(qk_kk[:n])
        kk_w = to_wide(qk_kk[n:])
        for d in range(2):
            j = d * GDN_HEADS + h
            mask_w = lower_w if d == 0 else upper_w
            cum_c = (cum_f_c if d == 0 else cum_b_c)[:, j:j + 1]
            cum_r = (cum_f_r if d == 0 else cum_b_r)[j:j + 1, :]
            b_c = beta_c[:, j:j + 1]
            decay_w = jnp.where(mask_w, jnp.exp(jnp.where(mask_w, col_wide(cum_c) - cum_r, 0.0)), 0.0)
            amat_w = jnp.where(diag_w, 0.0, kk_w * decay_w * col_wide(b_c))
            a_bs[j] = amat_w.astype(BF16)
            ps[j] = eye_w - amat_w * first_pairs
            e_c = jnp.exp(cum_c)
            rhss[j] = jnp.concatenate([(v * b_c).astype(BF16), (k * (b_c * e_c)).astype(BF16)], axis=1)
            cols = slice(j * HEAD_DIM, (j + 1) * HEAD_DIM)
            qd_ref[0, :, cols] = (q * e_c).astype(BF16)
            kd_ref[0, :, cols] = (k * jnp.exp(tot_c[:, j:j + 1] - cum_c)).astype(BF16)
            at_ref[0, 0, j * c:(j + 1) * c, :] = (qk_w * decay_w).astype(BF16)

    for pm in pair_masks[1:]:
        p_bs = [p.astype(BF16) for p in ps]
        ys = [_dot(a_bs[j] * pm, block_diag(p_bs[j])) for j in range(N_CHAINS)]
        ps = [ps[j] - _dot(p_bs[j], block_diag(ys[j].astype(BF16))) for j in range(N_CHAINS)]

    for j in range(N_CHAINS):
        uw = _dot(block_diag(ps[j].astype(BF16)), rhss[j])
        cols = slice(j * HEAD_DIM, (j + 1) * HEAD_DIM)
        u_ref[0, :, cols] = uw[:, :HEAD_DIM].astype(BF16)
        w_ref[0, :, cols] = uw[:, HEAD_DIM:].astype(BF16)


def _gdn_prep(qkv, conv_w, ba, bat, alog, dtb, row_len):
    b, l, _ = qkv.shape
    nblk = l // GDN_BLOCK
    wide = N_CHAINS * HEAD_DIM
    blk = lambda bi, i: (bi, i, 0)
    const = lambda bi, i: (0, 0)
    alog_r, dtb_r = alog.reshape(1, N_CHAINS), dtb.reshape(1, N_CHAINS)
    alog_c, dtb_c = alog.reshape(N_CHAINS, 1), dtb.reshape(N_CHAINS, 1)
    return pl.pallas_call(
        functools.partial(_gdn_prep_kernel, row_len),
        out_shape=(jax.ShapeDtypeStruct((b, l, wide), BF16),) * 4 + (
            jax.ShapeDtypeStruct((b, nblk, N_CHAINS * DELTA_CHUNK, GDN_BLOCK), BF16),
            jax.ShapeDtypeStruct((b, nblk, 2, GDN_HEADS, CHUNKS_PER_BLOCK * HEAD_DIM), F32)),
        grid=(b, nblk),
        in_specs=[pl.BlockSpec((1, GDN_BLOCK, QKV_COLS), blk),
                  pl.BlockSpec((3, QKV_COLS), const),
                  pl.BlockSpec((1, GDN_BLOCK, N_GATE_COLS), blk),
                  pl.BlockSpec((N_GATE_COLS, GDN_BLOCK), lambda bi, i: (0, bi * nblk + i)),
                  pl.BlockSpec((1, N_CHAINS), const),
                  pl.BlockSpec((1, N_CHAINS), const),
                  pl.BlockSpec((N_CHAINS, 1), const),
                  pl.BlockSpec((N_CHAINS, 1), const)],
        out_specs=(pl.BlockSpec((1, GDN_BLOCK, wide), blk),) * 4 + (
            pl.BlockSpec((1, 1, N_CHAINS * DELTA_CHUNK, GDN_BLOCK), lambda bi, i: (bi, i, 0, 0)),
            pl.BlockSpec((1, 1, 2, GDN_HEADS, CHUNKS_PER_BLOCK * HEAD_DIM),
                         lambda bi, i: (bi, i, 0, 0, 0))),
        scratch_shapes=[pltpu.VMEM((3, GDN_BLOCK, GDN_BLOCK), BF16),
                        pltpu.VMEM((N_PAIR_LEVELS, DELTA_CHUNK, GDN_BLOCK), BF16),
                        pltpu.VMEM((GDN_BLOCK, CHUNKS_PER_BLOCK * HEAD_DIM), BF16)],
        compiler_params=_params("arbitrary", "arbitrary"),
        name="gdn_prep",
    )(qkv, conv_w, ba, bat, alog_r, dtb_r, alog_c, dtb_c)


def _gdn_scan_kernel(uf, wf, qf, kf, af, gf, ub, wb, qb, kb, ab, gb, s0_ref,
                     of_ref, ob_ref, sfin_ref, s_scr):
    i = pl.program_id(0)
    c = DELTA_CHUNK
    n_batch = s0_ref.shape[0]

    @pl.when(i == 0)
    def _():
        s_scr[...] = s0_ref[...]

    ops = ((uf, wf, qf, kf, af, gf, of_ref), (ub, wb, qb, kb, ab, gb, ob_ref))
    chains = [(bi, d, h) for bi in range(n_batch) for d in range(2) for h in range(GDN_HEADS)]
    states = [s_scr[bi, d * GDN_HEADS + h] for bi, d, h in chains]
    n_chunks = uf.shape[1] // c
    for step in range(n_chunks):
        def chunk(d):
            cc = step if d == 0 else n_chunks - 1 - step
            return cc, slice(cc * c, (cc + 1) * c)

        xs = []
        for j, (bi, d, h) in enumerate(chains):
            _, rows = chunk(d)
            cols = slice(h * HEAD_DIM, (h + 1) * HEAD_DIM)
            wq = jnp.concatenate([ops[d][1][bi, rows, cols], ops[d][2][bi, rows, cols]], axis=0)
            xs.append(_dot(wq, states[j].astype(BF16)))
        v_news = []
        for j, (bi, d, h) in enumerate(chains):
            _, rows = chunk(d)
            cols = slice(h * HEAD_DIM, (h + 1) * HEAD_DIM)
            v_news.append((ops[d][0][bi, rows, cols].astype(F32) - xs[j][:c]).astype(BF16))
        for j, (bi, d, h) in enumerate(chains):
            cc, rows = chunk(d)
            cols = slice(h * HEAD_DIM, (h + 1) * HEAD_DIM)
            blk, lc = divmod(cc, CHUNKS_PER_BLOCK)
            a_c = ops[d][4][bi, blk, h * c:(h + 1) * c, lc * c:(lc + 1) * c]
            ops[d][6][bi, rows, cols] = (xs[j][c:] + _dot(a_c, v_news[j])).astype(BF16)
            ds = _dot_tn(ops[d][3][bi, rows, cols], v_news[j])
            g_last = ops[d][5][bi, blk, 0, h:h + 1, lc * HEAD_DIM:(lc + 1) * HEAD_DIM]
            states[j] = states[j] * g_last + ds
    for j, (bi, d, h) in enumerate(chains):
        s_scr[bi, d * GDN_HEADS + h] = states[j]

    @pl.when(i == pl.num_programs(0) - 1)
    def _():
        sfin_ref[...] = s_scr[...]


def _gdn_scan(u, w, qd, kd, at, gl, s0):
    b, l, _ = u.shape
    n_sub = SCAN_BLOCKS_PER_STEP if (l // GDN_BLOCK) % SCAN_BLOCKS_PER_STEP == 0 else 1
    nblk = l // (n_sub * GDN_BLOCK)
    step_rows = n_sub * GDN_BLOCK
    half = GDN_HEADS * HEAD_DIM
    fwd = lambda i: (0, i, 0)
    bwd = lambda i: (0, nblk - 1 - i, 1)
    big = lambda m: pl.BlockSpec((b, step_rows, half), m)
    att_shape = (b, n_sub, GDN_HEADS * DELTA_CHUNK, GDN_BLOCK)
    attf = pl.BlockSpec(att_shape, lambda i: (0, i, 0, 0))
    attb = pl.BlockSpec(att_shape, lambda i: (0, nblk - 1 - i, 1, 0))
    gl_shape = (b, n_sub, 1, GDN_HEADS, CHUNKS_PER_BLOCK * HEAD_DIM)
    glf = pl.BlockSpec(gl_shape, lambda i: (0, i, 0, 0, 0))
    glb = pl.BlockSpec(gl_shape, lambda i: (0, nblk - 1 - i, 1, 0, 0))
    state = pl.BlockSpec((b, N_CHAINS, HEAD_DIM, HEAD_DIM), lambda i: (0, 0, 0, 0))
    return pl.pallas_call(
        _gdn_scan_kernel,
        out_shape=(jax.ShapeDtypeStruct((b, l, half), BF16),
                   jax.ShapeDtypeStruct((b, l, half), BF16),
                   jax.ShapeDtypeStruct((b, N_CHAINS, HEAD_DIM, HEAD_DIM), F32)),
        grid=(nblk,),
        in_specs=[big(fwd), big(fwd), big(fwd), big(fwd), attf, glf,
                  big(bwd), big(bwd), big(bwd), big(bwd), attb, glb, state],
        out_specs=(pl.BlockSpec((b, step_rows, half), fwd),
                   pl.BlockSpec((b, step_rows, half), lambda i: (0, nblk - 1 - i, 0)),
                   state),
        scratch_shapes=[pltpu.VMEM((b, N_CHAINS, HEAD_DIM, HEAD_DIM), F32)],
        compiler_params=_params("arbitrary"),
        name="gdn_scan",
    )(u, w, qd, kd, at, gl, u, w, qd, kd, at, gl, s0)


def _mix_into(mix_ref, of_ref, ob_ref, z_ref, u_ref, v_ref, gng_ref, lng_ref, lnb_ref, ws_ref,
              bs_ref):
    tm = of_ref.shape[0]
    o = of_ref[...].astype(F32) + ob_ref[...].astype(F32)
    z = z_ref[...].astype(F32)
    for h in range(GDN_HEADS):
        cols = slice(h * HEAD_DIM, (h + 1) * HEAD_DIM)
        oh = o[:, cols]
        zh = z[:, cols]
        r = lax.rsqrt(jnp.mean(oh * oh, -1, keepdims=True) + NORM_EPS)
        mix_ref[:, cols] = (oh * r * gng_ref[...] * (zh * _sigmoid(zh))).astype(BF16)
    for g in range(SGU_GROUPS):
        cols = slice(g * SGU_GROUP, (g + 1) * SGU_GROUP)
        vg = v_ref[:, cols].astype(F32)
        vc = vg - jnp.mean(vg, -1, keepdims=True)
        vn = vc * lax.rsqrt(jnp.mean(vc * vc, -1, keepdims=True) + NORM_EPS)
        vn = (vn * lng_ref[g:g + 1] + lnb_ref[g:g + 1]).astype(BF16)
        wsg = ws_ref[g].astype(BF16)
        for n in range(tm // SGU_CHUNK):
            rows = slice(n * SGU_CHUNK, (n + 1) * SGU_CHUNK)
            s = _dot(wsg, vn[rows]) + bs_ref[g]
            mix_ref[rows, GDN_W + g * SGU_GROUP:GDN_W + (g + 1) * SGU_GROUP] = (
                u_ref[rows, cols].astype(F32) * s).astype(BF16)


def _outproj_kernel(of_ref, ob_ref, z_ref, u_ref, v_ref, gng_ref, lng_ref, lnb_ref, ws_ref, bs_ref,
                    x_ref, mod_ref, ng_ref, wout_ref, rwt_ref, rb_ref,
                    x1_ref, h2_ref, lt_ref, mix_scr):
    subs = _sub_tiles(x_ref.shape[0])
    for sl in subs:
        _mix_into(mix_scr.at[sl], of_ref.at[sl], ob_ref.at[sl], z_ref.at[sl], u_ref.at[sl],
                  v_ref.at[sl], gng_ref, lng_ref, lnb_ref, ws_ref, bs_ref)
    mod = mod_ref[0]
    ys = [_dot(mix_scr[sl], wout_ref[...]) for sl in subs]
    w_hi, w_lo = _split2(rwt_ref[...])
    for sl, y in zip(subs, ys):
        x1 = x_ref[sl] + mod[2:3] * _rms(y, ng_ref[1:2])
        x1_ref[sl] = x1
        h2 = _rms(x1, ng_ref[2:3]) * (1.0 + mod[4:5]) + mod[3:4]
        h_hi, h_lo = _split2(h2)
        _store_planes(h2_ref, _pack_rows(h_hi), sl)
        lt_ref[:, sl] = (_dot_nt(w_hi, h_hi) + _dot_nt(w_lo, h_hi) + _dot_nt(w_hi, h_lo)
                         + rb_ref[...])


def _outproj(o_f, o_b, z, ug, vg, gdn_norm_g, ln_g, ln_b, w_s, b_s_full,
             x2d, mod, rows_per_mod, ng, wout, rwt, rb_col):
    t = x2d.shape[0]
    tm = min(ROW_TILE, t)
    tiles_per_mod = rows_per_mod // tm
    row = lambda i: (i, 0)
    const = lambda i: (0, 0)
    c3 = lambda i: (0, 0, 0)
    return pl.pallas_call(
        _outproj_kernel,
        out_shape=(jax.ShapeDtypeStruct((t, D_MODEL), F32),
                   jax.ShapeDtypeStruct((N_PLANES, t, SC_ROW_W), U32),
                   jax.ShapeDtypeStruct((N_EXPERTS, t), F32)),
        grid=(t // tm,),
        in_specs=[pl.BlockSpec((tm, GDN_W), row), pl.BlockSpec((tm, GDN_W), row),
                  pl.BlockSpec((tm, GDN_W), row), pl.BlockSpec((tm, SGU_W), row),
                  pl.BlockSpec((tm, SGU_W), row),
                  pl.BlockSpec((1, HEAD_DIM), const),
                  pl.BlockSpec((SGU_GROUPS, SGU_GROUP), const),
                  pl.BlockSpec((SGU_GROUPS, SGU_GROUP), const),
                  pl.BlockSpec((SGU_GROUPS, SGU_CHUNK, SGU_CHUNK), c3),
                  pl.BlockSpec((SGU_GROUPS, SGU_CHUNK, SGU_GROUP), c3),
                  pl.BlockSpec((tm, D_MODEL), row),
                  pl.BlockSpec((1, 6, D_MODEL), lambda i: (i // tiles_per_mod, 0, 0)),
                  pl.BlockSpec((4, D_MODEL), const),
                  pl.BlockSpec((D_MODEL, D_MODEL), const),
                  pl.BlockSpec((N_EXPERTS, D_MODEL), const),
                  pl.BlockSpec((N_EXPERTS, 1), const)],
        out_specs=(pl.BlockSpec((tm, D_MODEL), row),
                   pl.BlockSpec((N_PLANES, tm, SC_ROW_W), lambda i: (0, i, 0)),
                   pl.BlockSpec((N_EXPERTS, tm), lambda i: (0, i))),
        compiler_params=_params("parallel"),
        scratch_shapes=[pltpu.VMEM((tm, D_MODEL), BF16)],
        name="out_proj_router",
    )(o_f, o_b, z, ug, vg, gdn_norm_g, ln_g, ln_b, w_s, b_s_full, x2d, mod, ng, wout, rwt, rb_col)


def _moe_kernel(be_ref, slot_ref, next_ref, short_ref, nb_ref, xb_ref, wgu_hbm, bgu_ref, wd_hbm,
                bd_ref, y_ref, wgu_f, wd_f, wgu_b, wd_b, gut_scr, sems):
    i = pl.program_id(0)
    live = i < nb_ref[0]
    new_expert = jnp.logical_or(i == 0, be_ref[i] != be_ref[jnp.maximum(i - 1, 0)])

    def weight_copies(expert, slot):
        return (pltpu.make_async_copy(wgu_hbm.at[expert], wgu_f.at[slot], sems.at[slot, 0]),
                pltpu.make_async_copy(wd_hbm.at[expert], wd_f.at[slot], sems.at[slot, 1]))

    @pl.when(i == 0)
    def _():
        for copy in weight_copies(be_ref[0], 0):
            copy.start()

    @pl.when(jnp.logical_and(live, new_expert))
    def _():
        slot = slot_ref[i]
        for copy in weight_copies(be_ref[i], slot):
            copy.wait()

        @pl.when(next_ref[i] >= 0)
        def _():
            for copy in weight_copies(next_ref[i], 1 - slot):
                copy.start()
        wgu_b[...] = wgu_f[slot].astype(BF16)
        wd_b[...] = wd_f[slot].astype(BF16)

    def ffn(n_rows):
        rows = slice(0, n_rows)
        xb = _unpack_rows(_load_planes(xb_ref, rows)).astype(BF16)
        gu = _dot(xb, wgu_b[...]) + bgu_ref[0]
        gu_t = gu.T
        acts = []
        for part in range(n_rows // LANES):
            part_ref = gut_scr.at[part]
            part_ref[...] = gu_t[:, part * LANES:(part + 1) * LANES]
            gate = jnp.minimum(part_ref[pl.ds(0, D_FF, stride=2), :], SWIGLU_LIMIT)
            up = jnp.clip(part_ref[pl.ds(1, D_FF, stride=2), :], -SWIGLU_LIMIT, SWIGLU_LIMIT)
            acts.append(((up + 1.0) * gate * _sigmoid(SWIGLU_ALPHA * gate)).astype(BF16))
        act_t = jnp.concatenate(acts, axis=1)
        _store_planes(y_ref, _pack_rows(_dot_tn(act_t, wd_b[...]) + bd_ref[0]), rows)

    short = short_ref[i] != 0

    @pl.when(jnp.logical_and(live, jnp.logical_not(short)))
    def _():
        ffn(MOE_ROWS)

    @pl.when(jnp.logical_and(live, short))
    def _():
        ffn(MOE_ROWS // 2)
        for p in range(N_PLANES):
            y_ref[p, MOE_ROWS // 2:] = jnp.zeros((MOE_ROWS // 2, SC_ROW_W), U32)

    @pl.when(jnp.logical_not(live))
    def _():
        y_ref[...] = jnp.zeros_like(y_ref)


def _moe_ffn(block_e, n_used, real_end, xb, w_gu, b_gu, w_down, b_down):
    n_rows = xb.shape[1]
    n_blocks = n_rows // MOE_ROWS
    idx = jnp.arange(n_blocks, dtype=jnp.int32)
    live = idx < n_used[0]
    changed = jnp.concatenate([jnp.ones((1,), bool), block_e[1:] != block_e[:-1]]) & live
    slot = ((jnp.cumsum(changed.astype(jnp.int32)) - 1) % 2).astype(jnp.int32)
    change_at = jnp.where(changed, idx, n_blocks)
    next_change = lax.cummin(jnp.concatenate([change_at[1:], jnp.full((1,), n_blocks, jnp.int32)]),
                             reverse=True)
    next_e = jnp.where(next_change < n_blocks,
                       block_e[jnp.minimum(next_change, n_blocks - 1)], -1).astype(jnp.int32)
    n_real = jnp.clip(real_end[block_e] - idx * MOE_ROWS, 0, MOE_ROWS)
    short = (n_real <= MOE_ROWS // 2).astype(jnp.int32)

    row = lambda i, be, sl, nx, sh, nb: (0, i, 0)
    ex3 = lambda i, be, sl, nx, sh, nb: (be[i], 0, 0)
    live_row = lambda i, be, sl, nx, sh, nb: (0, jnp.minimum(i, nb[0] - 1), 0)
    planes = (N_PLANES, MOE_ROWS, SC_ROW_W)
    grid_spec = pltpu.PrefetchScalarGridSpec(
        num_scalar_prefetch=5,
        grid=(n_blocks,),
        in_specs=[pl.BlockSpec(planes, live_row),
                  pl.BlockSpec(memory_space=pl.ANY),
                  pl.BlockSpec((1, 1, 2 * D_FF), ex3),
                  pl.BlockSpec(memory_space=pl.ANY),
                  pl.BlockSpec((1, 1, D_MODEL), ex3)],
        out_specs=pl.BlockSpec(planes, row),
        scratch_shapes=[pltpu.VMEM((2, D_MODEL, 2 * D_FF), F32),
                        pltpu.VMEM((2, D_FF, D_MODEL), F32),
                        pltpu.VMEM((D_MODEL, 2 * D_FF), BF16),
                        pltpu.VMEM((D_FF, D_MODEL), BF16),
                        pltpu.VMEM((MOE_ROWS // LANES, 2 * D_FF, LANES), F32),
                        pltpu.SemaphoreType.DMA((2, 2))],
    )
    return pl.pallas_call(
        _moe_kernel,
        out_shape=jax.ShapeDtypeStruct((N_PLANES, n_rows, SC_ROW_W), U32),
        grid_spec=grid_spec,
        compiler_params=_params("arbitrary"),
        name="moe_ffn",
    )(block_e, slot, next_e, short, n_used, xb, w_gu, b_gu, w_down, b_down)


def _combine_kernel(y0_ref, y1_ref, y2_ref, y3_ref, gt_ref, x1_ref, mod_ref, ng_ref, o_ref):
    mod = mod_ref[0]
    tm = o_ref.shape[0]
    gt = jnp.concatenate([gt_ref[...], jnp.zeros((LANES - SUBLANES, tm), F32)], axis=0).T
    y = _unpack_rows(_load_planes(y0_ref)) * gt[:, 0:1]
    for k, y_ref in ((1, y1_ref), (2, y2_ref), (3, y3_ref)):
        y = y + _unpack_rows(_load_planes(y_ref)) * gt[:, k:k + 1]
    o_ref[...] = x1_ref[...] + mod[5:6] * _rms(y, ng_ref[3:4])


def _combine(yg, gates, x1, mod, rows_per_mod, ng):
    t = x1.shape[0]
    tm = min(COMBINE_TILE, t)
    tiles_per_mod = rows_per_mod // tm
    n_tiles = t // tm
    row = lambda i: (i, 0)
    choice = lambda k: pl.BlockSpec((N_PLANES, tm, SC_ROW_W), lambda i: (0, k * n_tiles + i, 0))
    return pl.pallas_call(
        _combine_kernel,
        out_shape=jax.ShapeDtypeStruct((t, D_MODEL), F32),
        grid=(n_tiles,),
        in_specs=[choice(0), choice(1), choice(2), choice(3),
                  pl.BlockSpec((SUBLANES, tm), lambda i: (0, i)),
                  pl.BlockSpec((tm, D_MODEL), row),
                  pl.BlockSpec((1, 6, D_MODEL), lambda i: (i // tiles_per_mod, 0, 0)),
                  pl.BlockSpec((4, D_MODEL), lambda i: (0, 0))],
        out_specs=pl.BlockSpec((tm, D_MODEL), row),
        compiler_params=_params("parallel"),
        name="moe_combine",
    )(yg, yg, yg, yg, gates, x1, mod, ng)


def _route_kernel(lt_ref, eidx_ref, gate_ref, rank_ref, cnt_ref, carry, earlier):
    i = pl.program_id(0)
    tile = earlier.shape[0]

    @pl.when(i == 0)
    def _():
        carry[...] = jnp.zeros_like(carry)
        ti = lax.broadcasted_iota(jnp.int32, (tile, tile), 0)
        tj = lax.broadcasted_iota(jnp.int32, (tile, tile), 1)
        earlier[...] = jnp.where(ti < tj, 1.0, 0.0).astype(BF16)

    eio = lax.broadcasted_iota(jnp.int32, (N_EXPERTS, tile), 0).astype(F32)
    seen = carry[...]
    for sub in range(lt_ref.shape[1] // tile):
        cols = slice(sub * tile, (sub + 1) * tile)
        logits = lt_ref[:, cols]
        vals, sels = [], []
        for k in range(TOP_K):
            m = jnp.max(logits, axis=0, keepdims=True)
            idx = jnp.min(jnp.where(logits == m, eio, float(N_EXPERTS)), axis=0, keepdims=True)
            sel = eio == idx
            logits = jnp.where(sel, -jnp.inf, logits)
            vals.append(m)
            sels.append(sel)
            eidx_ref[k:k + 1, cols] = idx.astype(jnp.int32)
        exps = [jnp.exp(v - vals[0]) for v in vals]
        denom = exps[0] + exps[1] + exps[2] + exps[3]
        for k in range(TOP_K):
            gate_ref[k:k + 1, cols] = exps[k] / denom
        gate_ref[TOP_K:, cols] = jnp.zeros((SUBLANES - TOP_K, tile), F32)

        member = jnp.where(sels[0] | sels[1] | sels[2] | sels[3], 1.0, 0.0)
        before = _dot(member.astype(BF16), earlier[...]) + seen
        for k in range(TOP_K):
            rank_ref[k:k + 1, cols] = jnp.sum(jnp.where(sels[k], before, 0.0), axis=0,
                                              keepdims=True).astype(jnp.int32)
        seen = seen + jnp.sum(member, axis=1, keepdims=True)
    carry[...] = seen
    cnt_ref[...] = seen.astype(jnp.int32)


def _route(logits_t):
    t = logits_t.shape[1]
    block = min(ROUTE_BLOCK, t)
    tile = min(ROUTE_TILE, block)
    blk = lambda i: (0, i)
    return pl.pallas_call(
        _route_kernel,
        out_shape=(jax.ShapeDtypeStruct((TOP_K, t), jnp.int32),
                   jax.ShapeDtypeStruct((SUBLANES, t), F32),
                   jax.ShapeDtypeStruct((TOP_K, t), jnp.int32),
                   jax.ShapeDtypeStruct((N_EXPERTS, 1), jnp.int32)),
        grid=(t // block,),
        in_specs=[pl.BlockSpec((N_EXPERTS, block), blk)],
        out_specs=(pl.BlockSpec((TOP_K, block), blk), pl.BlockSpec((SUBLANES, block), blk),
                   pl.BlockSpec((TOP_K, block), blk),
                   pl.BlockSpec((N_EXPERTS, 1), lambda i: (0, 0))),
        scratch_shapes=[pltpu.VMEM((N_EXPERTS, 1), F32), pltpu.VMEM((tile, tile), BF16)],
        compiler_params=_params("arbitrary"),
        name="moe_route",
    )(logits_t)


def _slot_tables(eidx, rank, counts, n_blocks):
    padded = (counts + MOE_ROWS - 1) // MOE_ROWS * MOE_ROWS
    pad_end = jnp.cumsum(padded)
    pad_start = pad_end - padded
    experts = jnp.arange(N_EXPERTS, dtype=jnp.int32)
    dest = rank + jnp.sum(jnp.where(eidx[..., None] == experts, pad_start, 0), axis=-1)
    first_row = jnp.arange(n_blocks, dtype=jnp.int32)[:, None] * MOE_ROWS
    block_e = jnp.minimum(jnp.sum((pad_end[None, :] <= first_row).astype(jnp.int32), axis=1),
                          N_EXPERTS - 1)
    n_used = pad_end[-1:] // MOE_ROWS
    real_end = (pad_start + counts).astype(jnp.int32)
    return (dest.astype(jnp.int32), pad_start.astype(jnp.int32), block_e, n_used.astype(jnp.int32),
            real_end)


def _sc_mesh():
    return plsc.VectorSubcoreMesh(core_axis_name="c", subcore_axis_name="s",
                                  num_cores=SC_CORES, num_subcores=SC_SUBCORES)


def _plane_row_ids(rows, rows_per_plane):
    return jnp.concatenate([rows + p * rows_per_plane for p in range(N_PLANES)], axis=-1)


def _sc_gather_rows(table, rows):
    v = table.shape[1]
    idx = _plane_row_ids(rows, v)[None]
    n_all = idx.shape[1]

    @functools.partial(pl.kernel, mesh=_sc_mesh(), name="moe_gather_rows",
                       out_type=jax.ShapeDtypeStruct((n_all, SC_ROW_W), U32))
    def gather(x_hbm, i_hbm, o_hbm):
        def body(i_vmem, o_vmem):
            pltpu.sync_copy(x_hbm.at[i_vmem.at[0]], o_vmem)

        pltpu.emit_pipeline(
            body, grid=(n_all // SC_WINDOW,),
            in_specs=[pl.BlockSpec((1, SC_WINDOW), lambda i: (0, i))],
            out_specs=[pl.BlockSpec((SC_WINDOW, SC_ROW_W), lambda i: (i, 0))],
            core_axis_name=("c", "s"), dimension_semantics=(pltpu.PARALLEL,),
        )(i_hbm, o_hbm)

    return gather(table.reshape(N_PLANES * v, SC_ROW_W), idx).reshape(N_PLANES, -1, SC_ROW_W)


def _sc_scatter_rows(rows, dest, n_out):
    t = rows.shape[1]
    idx = _plane_row_ids(dest, n_out)

    @functools.partial(pl.kernel, mesh=_sc_mesh(), name="moe_scatter_rows", scratch_types=[],
                       out_type=jax.ShapeDtypeStruct((N_PLANES * n_out, SC_ROW_W), U32))
    def scatter(x_hbm, i_hbm, o_hbm):
        def body(x_vmem, i_vmem):
            for k in range(TOP_K):
                pltpu.sync_copy(x_vmem, o_hbm.at[i_vmem.at[k]])

        pltpu.emit_pipeline(
            body, grid=(N_PLANES * t // SC_WINDOW,),
            in_specs=[pl.BlockSpec((SC_WINDOW, SC_ROW_W), lambda i: (i, 0)),
                      pl.BlockSpec((TOP_K, SC_WINDOW), lambda i: (0, i))],
            out_specs=[],
            core_axis_name=("c", "s"), dimension_semantics=(pltpu.PARALLEL,),
        )(x_hbm, i_hbm)

    return scatter(rows.reshape(N_PLANES * t, SC_ROW_W), idx).reshape(N_PLANES, n_out, SC_ROW_W)


def _zero_pad_kernel(cnt_ref, start_ref, xb_in_ref, xb_ref, zero_scr, sem):
    del xb_in_ref
    zero_scr[...] = jnp.zeros_like(zero_scr)

    pieces = [SUBLANES << bit for bit in range((MOE_ROWS // SUBLANES - 1).bit_length())]

    def zero_copy(p, row, size):
        return pltpu.make_async_copy(zero_scr.at[pl.ds(0, size)], xb_ref.at[p, pl.ds(row, size)], sem)

    def for_each_piece(fn):
        def per_expert(e, carry):
            n_real = cnt_ref[e]
            n_pad = (MOE_ROWS - n_real % MOE_ROWS) % MOE_ROWS
            first = start_ref[e] + n_real
            n_single = n_pad % SUBLANES
            for j in range(SUBLANES - 1):
                @pl.when(j < n_single)
                def _():
                    for p in range(N_PLANES):
                        fn(zero_copy(p, first + j, 1))
            row = first + n_single
            for size in pieces:
                @pl.when((n_pad & size) != 0)
                def _():
                    for p in range(N_PLANES):
                        fn(zero_copy(p, pl.multiple_of(row, SUBLANES), size))
                row = row + (n_pad & size)
            return carry
        lax.fori_loop(0, N_EXPERTS, per_expert, 0)

    for_each_piece(lambda copy: copy.start())
    for_each_piece(lambda copy: copy.wait())


def _zero_pad_slots(counts, pad_start, xb):
    grid_spec = pltpu.PrefetchScalarGridSpec(
        num_scalar_prefetch=2,
        grid=(1,),
        in_specs=[pl.BlockSpec(memory_space=pl.ANY)],
        out_specs=pl.BlockSpec(memory_space=pl.ANY),
        scratch_shapes=[pltpu.VMEM((MOE_ROWS // 2, SC_ROW_W), U32), pltpu.SemaphoreType.DMA],
    )
    return pl.pallas_call(
        _zero_pad_kernel,
        out_shape=jax.ShapeDtypeStruct(xb.shape, xb.dtype),
        grid_spec=grid_spec,
        input_output_aliases={2: 0},
        compiler_params=_params("arbitrary"),
        name="moe_zero_pad",
    )(counts, pad_start, xb)


def kernel(x, c, ctx, c_ctx, w_ada, b_ada, norm_g, w_in, conv_w, a_log, dt_bias, gdn_norm_g,
           sgu_ln_g, sgu_ln_b, sgu_w, sgu_b, w_out, router_w, router_b, w_gu, b_gu, w_down, b_down):
    b, l, d = x.shape
    lc = ctx.shape[1]
    t = b * l
    assert d == D_MODEL and l % ROW_TILE == 0 and l % GDN_BLOCK == 0 and lc % GDN_BLOCK == 0
    assert w_ada.shape[0] == 1, "single-layer block"

    cs = jnp.concatenate([c, c_ctx[None], jnp.zeros((8 - b - 1, d), F32)], axis=0)
    mod_all = _ada(cs, w_ada[0], b_ada[0][None])
    mod = mod_all[:b].reshape(b, 6, d)
    mod_c = mod_all[b:b + 1].reshape(1, 6, d)
    ng = norm_g[0]


    x2d = x.reshape(t, d)
    qkv, z, ug, vg, ba, bat = _inproj(x2d, mod, l, ng[0:1], w_in[0])
    ctx2d = ctx.reshape(b * lc, d)
    qkv_c, _, _, _, ba_c, bat_c = _inproj(ctx2d, mod_c, b * lc, ng[0:1], w_in[0])

    alog = a_log[0].reshape(-1)
    dtb = dt_bias[0].reshape(-1)
    pc = _gdn_prep(qkv_c.reshape(b, lc, QKV_COLS), conv_w[0], ba_c.reshape(b, lc, N_GATE_COLS),
                   bat_c, alog, dtb, lc)
    s_zero = jnp.zeros((b, N_CHAINS, HEAD_DIM, HEAD_DIM), F32)
    _, _, s_ctx = _gdn_scan(*pc, s_zero)
    pp = _gdn_prep(qkv.reshape(b, l, QKV_COLS), conv_w[0], ba.reshape(b, l, N_GATE_COLS),
                   bat, alog, dtb, GRID_W)
    o_f, o_b, _ = _gdn_scan(*pp, s_ctx)

    b_s_full = jnp.broadcast_to(sgu_b[0][:, :, None], (SGU_GROUPS, SGU_CHUNK, SGU_GROUP))
    x1, h2p, logits_t = _outproj(o_f.reshape(t, GDN_W), o_b.reshape(t, GDN_W), z, ug, vg,
                                 gdn_norm_g, sgu_ln_g[0], sgu_ln_b[0], sgu_w[0], b_s_full,
                                 x2d, mod, l, ng, w_out[0].astype(BF16),
                                 router_w[0].T, router_b[0][:, None])

    eidx, gates_t, rank, counts = _route(logits_t)
    counts = counts[:, 0]
    n_blocks = -(-(t * TOP_K) // MOE_ROWS) + N_EXPERTS
    dest, pad_start, block_e, n_used, real_end = _slot_tables(eidx, rank, counts, n_blocks)
    xb = _sc_scatter_rows(h2p, dest, n_blocks * MOE_ROWS)
    xb = _zero_pad_slots(counts, pad_start, xb)
    yb = _moe_ffn(block_e, n_used, real_end, xb, w_gu[0], b_gu[0][:, None, :], w_down[0],
                  b_down[0][:, None, :])
    yg = _sc_gather_rows(yb, dest.reshape(-1))
    out = _combine(yg, gates_t, x1, mod, l, ng)
    return out.reshape(b, l, d)
```

```python
import functools
import math

import jax
import jax.numpy as jnp
from jax import lax
from jax.experimental import pallas as pl
from jax.experimental.pallas import tpu as pltpu
from jax.experimental.pallas import tpu_sc as plsc

F32 = jnp.float32
BF16 = jnp.bfloat16

D_MODEL = 1024
GDN_HEADS = 4
HEAD_DIM = 128
GDN_W = GDN_HEADS * HEAD_DIM
SGU_GROUPS = 4
SGU_GROUP = 128
SGU_W = SGU_GROUPS * SGU_GROUP
SGU_CHUNK = 128
DELTA_CHUNK = 64
GRID_W = 64
N_EXPERTS = 32
TOP_K = 4
D_FF = 1024
SWIGLU_LIMIT = 7.0
SWIGLU_ALPHA = 1.702
NORM_EPS = 1e-6
QKV_COLS = 3 * GDN_W
N_CHAINS = 2 * GDN_HEADS
N_GATE_COLS = 2 * N_CHAINS

ROW_TILE = 1024
SUB_ROWS = 256
COMBINE_TILE = 1024
ADA_COLS = 3 * D_MODEL
GDN_BLOCK = 256
CHUNKS_PER_BLOCK = GDN_BLOCK // DELTA_CHUNK
SCAN_BLOCKS_PER_STEP = 4
N_PAIR_LEVELS = DELTA_CHUNK.bit_length() - 1
MOE_ROWS = 512
U32 = jnp.uint32
LANES = 128
SUBLANES = 8
PACKED_W = D_MODEL // 2
ROUTE_BLOCK = 2048
ROUTE_TILE = 512
SC_CORES = 2
SC_SUBCORES = 16
SC_WINDOW = 128
N_PLANES = 2
SC_ROW_W = PACKED_W // N_PLANES
VMEM_LIMIT = 56 * 1024 * 1024


def _params(*sem):
    return pltpu.CompilerParams(dimension_semantics=sem, vmem_limit_bytes=VMEM_LIMIT)


def _dot(a, b):
    return jnp.dot(a, b, preferred_element_type=F32)


def _dot_nt(a, b):
    return lax.dot_general(a, b, (((1,), (1,)), ((), ())), preferred_element_type=F32)


def _dot_tn(a, b):
    return lax.dot_general(a, b, (((0,), (0,)), ((), ())), preferred_element_type=F32)


def _split2(a):
    hi = a.astype(BF16)
    lo = (a - hi.astype(F32)).astype(BF16)
    return hi, lo


def _split3(a):
    hi = a.astype(BF16)
    r = a - hi.astype(F32)
    mid = r.astype(BF16)
    lo = (r - mid.astype(F32)).astype(BF16)
    return hi, mid, lo


def _pack_rows(x):
    xb = x.astype(BF16).astype(F32)
    hi = lax.bitcast_convert_type(xb[:, :PACKED_W], U32)
    lo = lax.bitcast_convert_type(xb[:, PACKED_W:], U32)
    return hi | (lo >> 16)


def _store_planes(ref, packed, rows=slice(None)):
    for p in range(N_PLANES):
        ref[p, rows] = packed[:, p * SC_ROW_W:(p + 1) * SC_ROW_W]


def _load_planes(ref, rows=slice(None)):
    return jnp.concatenate([ref[p, rows] for p in range(N_PLANES)], axis=1)


def _sub_tiles(n_rows):
    return [slice(r, r + SUB_ROWS) for r in range(0, n_rows, SUB_ROWS)]


def _unpack_rows(w):
    hi = lax.bitcast_convert_type(w & jnp.uint32(0xFFFF0000), F32)
    lo = lax.bitcast_convert_type(w << 16, F32)
    return jnp.concatenate([hi, lo], axis=1)


def _rms(x32, g):
    return x32 * lax.rsqrt(jnp.mean(x32 * x32, -1, keepdims=True) + NORM_EPS) * g


def _gelu_tanh(x):
    c = math.sqrt(2.0 / math.pi)
    return 0.5 * x * (1.0 + jnp.tanh(c * (x + 0.044715 * (x * x * x))))


def _sigmoid(x):
    return 1.0 / (1.0 + jnp.exp(-x))


def _softplus(x):
    return jnp.maximum(x, 0.0) + jnp.log(1.0 + jnp.exp(-jnp.abs(x)))


def _ada_kernel(c_ref, w_ref, b_ref, o_ref):
    c = c_ref[...]
    s = c * _sigmoid(c)
    s_hi, s_lo = _split2(s)
    w_hi, w_lo = _split2(w_ref[...])
    o_ref[...] = _dot(s_hi, w_hi) + _dot(s_lo, w_hi) + _dot(s_hi, w_lo) + b_ref[...]


def _ada(cs, w_ada, b_ada):
    n = w_ada.shape[1]
    bn = ADA_COLS
    return pl.pallas_call(
        _ada_kernel,
        out_shape=jax.ShapeDtypeStruct((cs.shape[0], n), F32),
        grid=(n // bn,),
        in_specs=[pl.BlockSpec(cs.shape, lambda j: (0, 0)),
                  pl.BlockSpec((D_MODEL, bn), lambda j: (0, j)),
                  pl.BlockSpec((1, bn), lambda j: (0, j))],
        out_specs=pl.BlockSpec((cs.shape[0], bn), lambda j: (0, j)),
        compiler_params=_params("parallel"),
        name="ada_mod",
    )(cs, w_ada, b_ada)


def _inproj_kernel(x_ref, mod_ref, g_ref, w_hbm,
                   qkv_ref, z_ref, u_ref, v_ref, ba_ref, bat_ref,
                   w_f32, wqkv_ref, wzuv_ref, wba_ref, sem):
    @pl.when(pl.program_id(0) == 0)
    def _():
        copy = pltpu.make_async_copy(w_hbm, w_f32, sem)
        copy.start()
        copy.wait()
        wqkv_ref[...] = w_f32[:, :QKV_COLS].astype(BF16)
        wzuv_ref[...] = w_f32[:, QKV_COLS + N_GATE_COLS:].astype(BF16)
        lane = lax.broadcasted_iota(jnp.int32, (D_MODEL, LANES), 1)
        w_gate = jnp.where(lane < N_GATE_COLS, w_f32[:, QKV_COLS:QKV_COLS + LANES], 0.0)
        w_hi, w_lo = _split2(w_gate)
        wba_ref[:, :LANES] = w_hi
        wba_ref[:, LANES:] = w_lo

    mod = mod_ref[0]
    subs = _sub_tiles(x_ref.shape[0])
    hs = [_split2(_rms(x_ref[sl], g_ref[...]) * (1.0 + mod[1:2]) + mod[0:1]) for sl in subs]
    qkvs = [_dot(h_hi, wqkv_ref[...]) for h_hi, _ in hs]
    zuvs = [_dot(h_hi, wzuv_ref[...]) for h_hi, _ in hs]
    bas = []
    for h_hi, h_lo in hs:
        both = _dot(h_hi, wba_ref[...])
        bas.append(both[:, :LANES] + both[:, LANES:] + _dot(h_lo, wba_ref[:, :LANES]))
    for sl, qkv, zuv, ba in zip(subs, qkvs, zuvs, bas):
        qkv_ref[sl] = qkv.astype(BF16)
        z_ref[sl] = zuv[:, :GDN_W].astype(BF16)
        u_ref[sl] = _gelu_tanh(zuv[:, GDN_W:GDN_W + SGU_W]).astype(BF16)
        v_ref[sl] = _gelu_tanh(zuv[:, GDN_W + SGU_W:]).astype(BF16)
        ba_ref[sl] = ba[:, :N_GATE_COLS]
        bat_ref[:, sl] = ba.T[:N_GATE_COLS]


def _inproj(x2d, mod, rows_per_mod, ng0, w_in):
    t = x2d.shape[0]
    tm = min(ROW_TILE, t)
    tiles_per_mod = rows_per_mod // tm
    row = lambda i: (i, 0)
    const = lambda i: (0, 0)
    return pl.pallas_call(
        _inproj_kernel,
        out_shape=(jax.ShapeDtypeStruct((t, QKV_COLS), BF16),
                   jax.ShapeDtypeStruct((t, GDN_W), BF16),
                   jax.ShapeDtypeStruct((t, SGU_W), BF16),
                   jax.ShapeDtypeStruct((t, SGU_W), BF16),
                   jax.ShapeDtypeStruct((t, N_GATE_COLS), F32),
                   jax.ShapeDtypeStruct((N_GATE_COLS, t), F32)),
        grid=(t // tm,),
        in_specs=[pl.BlockSpec((tm, D_MODEL), row),
                  pl.BlockSpec((1, 6, D_MODEL), lambda i: (i // tiles_per_mod, 0, 0)),
                  pl.BlockSpec((1, D_MODEL), const),
                  pl.BlockSpec(memory_space=pl.ANY)],
        out_specs=(pl.BlockSpec((tm, QKV_COLS), row),
                   pl.BlockSpec((tm, GDN_W), row),
                   pl.BlockSpec((tm, SGU_W), row),
                   pl.BlockSpec((tm, SGU_W), row),
                   pl.BlockSpec((tm, N_GATE_COLS), row),
                   pl.BlockSpec((N_GATE_COLS, tm), lambda i: (0, i))),
        scratch_shapes=[pltpu.VMEM(w_in.shape, F32),
                        pltpu.VMEM((D_MODEL, QKV_COLS), BF16),
                        pltpu.VMEM((D_MODEL, w_in.shape[1] - QKV_COLS - N_GATE_COLS), BF16),
                        pltpu.VMEM((D_MODEL, 2 * LANES), BF16),
                        pltpu.SemaphoreType.DMA],
        compiler_params=_params("arbitrary"),
        name="in_proj",
    )(x2d, mod, ng0, w_in)


def _gdn_prep_kernel(row_len, qkv_ref, cw_ref, ba_ref, bat_ref, alog_r_ref, dtb_r_ref,
                     alog_c_ref, dtb_c_ref, u_ref, w_ref, qd_ref, kd_ref, at_ref, gl_ref,
                     tri_scr, pair_scr, spread_scr, first_flag):
    n = GDN_BLOCK
    c = DELTA_CHUNK

    def mask01(m):
        return jnp.where(m, 1.0, 0.0).astype(BF16)

    wi = lax.broadcasted_iota(jnp.int32, (c, n), 0)
    wl = lax.broadcasted_iota(jnp.int32, (c, n), 1)
    wchunk = wl // c
    wj = wl % c
    lower_w = wi >= wj
    upper_w = wi <= wj
    diag_w = wi == wj
    eye_w = jnp.where(diag_w, 1.0, 0.0)

    @pl.when(first_flag[0] == 1)
    def _():
        first_flag[0] = 0
        ri = lax.broadcasted_iota(jnp.int32, (n, n), 0)
        ci = lax.broadcasted_iota(jnp.int32, (n, n), 1)
        same = (ri // c) == (ci // c)
        tri_scr[0] = mask01(same & (ri >= ci))
        tri_scr[1] = mask01(same & (ri <= ci))
        tri_scr[2] = mask01(same)
        for m in range(N_PAIR_LEVELS):
            s = 1 << m
            pair_scr[m] = mask01(((wi // (2 * s)) == (wj // (2 * s))) & ((wi // s) != (wj // s)))
        ei = lax.broadcasted_iota(jnp.int32, (n, CHUNKS_PER_BLOCK * HEAD_DIM), 0) // c
        ej = lax.broadcasted_iota(jnp.int32, (n, CHUNKS_PER_BLOCK * HEAD_DIM), 1) // HEAD_DIM
        spread_scr[...] = mask01(ei == ej)

    lower_b = tri_scr[0]
    upper_b = tri_scr[1]
    same_b = tri_scr[2]
    pair_masks = [pair_scr[m] for m in range(N_PAIR_LEVELS)]

    def to_wide(full):
        out = full[:c]
        for k in range(1, CHUNKS_PER_BLOCK):
            out = jnp.where(wchunk == k, full[k * c:(k + 1) * c], out)
        return out

    def col_wide(col):
        out = jnp.broadcast_to(col[:c], (c, n))
        for k in range(1, CHUNKS_PER_BLOCK):
            out = jnp.where(wchunk == k, jnp.broadcast_to(col[k * c:(k + 1) * c], (c, n)), out)
        return out

    def block_diag(x_w):
        return jnp.concatenate([x_w] * CHUNKS_PER_BLOCK, axis=0) * same_b

    ba = ba_ref[0]
    bat = bat_ref[...]
    beta_c = _sigmoid(ba[:, :N_CHAINS])
    g_c = -jnp.exp(alog_r_ref[...]) * _softplus(ba[:, N_CHAINS:] + dtb_r_ref[...])
    g_r = -jnp.exp(alog_c_ref[...]) * _softplus(bat[N_CHAINS:] + dtb_c_ref[...])
    gc3 = _split3(g_c)
    gr3 = jnp.concatenate(_split3(g_r), axis=0)

    def sum3_r(m):
        return m[:N_CHAINS] + m[N_CHAINS:2 * N_CHAINS] + m[2 * N_CHAINS:]

    cum_f_c = _dot(lower_b, gc3[0]) + _dot(lower_b, gc3[1]) + _dot(lower_b, gc3[2])
    tot_c = _dot(same_b, gc3[0]) + _dot(same_b, gc3[1]) + _dot(same_b, gc3[2])
    cum_b_c = tot_c - cum_f_c + g_c
    cum_f_r = sum3_r(_dot(gr3, upper_b))
    cum_b_r = sum3_r(_dot(gr3, lower_b))
    g_last = jnp.exp(sum3_r(_dot(gr3, spread_scr[...])))
    gl_ref[0, 0, 0] = g_last[:GDN_HEADS]
    gl_ref[0, 0, 1] = g_last[GDN_HEADS:]

    pos = lax.broadcasted_iota(jnp.int32, (n, HEAD_DIM), 0) % row_len
    first = pos == 0
    last = pos == row_len - 1

    def conv_silu(col):
        x = qkv_ref[0, :, col * HEAD_DIM:(col + 1) * HEAD_DIM].astype(F32)
        cw = cw_ref[:, col * HEAD_DIM:(col + 1) * HEAD_DIM]
        xp = jnp.where(first, 0.0, pltpu.roll(x, 1, 0))
        xn = jnp.where(last, 0.0, pltpu.roll(x, n - 1, 0))
        y = xp * cw[0:1] + x * cw[1:2] + xn * cw[2:3]
        return y * _sigmoid(y)

    def l2n(x):
        return x * lax.rsqrt(jnp.sum(x * x, -1, keepdims=True) + NORM_EPS)

    a_bs, ps, rhss = [None] * N_CHAINS, [None] * N_CHAINS, [None] * N_CHAINS
    first_pairs = pair_masks[0].astype(F32)
    for h in range(GDN_HEADS):
        q = l2n(conv_silu(h)) * (HEAD_DIM ** -0.5)
        k = l2n(conv_silu(GDN_HEADS + h))
        v = conv_silu(2 * GDN_HEADS + h)
        k_b = k.astype(BF16)
        qk_kk = _dot_nt(jnp.concatenate([q.astype(BF16), k_b], axis=0), k_b)
        qk_w = to_wide(qk_kk[:n])
        kk_w = to_wide(qk_kk[n:])
        for d in range(2):
            j = d * GDN_HEADS + h
            mask_w = lower_w if d == 0 else upper_w
            cum_c = (cum_f_c if d == 0 else cum_b_c)[:, j:j + 1]
            cum_r = (cum_f_r if d == 0 else cum_b_r)[j:j + 1, :]
            b_c = beta_c[:, j:j + 1]
            decay_w = jnp.where(mask_w, jnp.exp(jnp.where(mask_w, col_wide(cum_c) - cum_r, 0.0)), 0.0)
            amat_w = jnp.where(diag_w, 0.0, kk_w * decay_w * col_wide(b_c))
            a_bs[j] = amat_w.astype(BF16)
            ps[j] = eye_w - amat_w * first_pairs
            e_c = jnp.exp(cum_c)
            rhss[j] = jnp.concatenate([(v * b_c).astype(BF16), (k * (b_c * e_c)).astype(BF16)], axis=1)
            cols = slice(j * HEAD_DIM, (j + 1) * HEAD_DIM)
            qd_ref[0, :, cols] = (q * e_c).astype(BF16)
            kd_ref[0, :, cols] = (k * jnp.exp(tot_c[:, j:j + 1] - cum_c)).astype(BF16)
            at_ref[0, 0, j * c:(j + 1) * c, :] = (qk_w * decay_w).astype(BF16)

    for pm in pair_masks[1:]:
        p_bs = [p.astype(BF16) for p in ps]
        ys = [_dot(a_bs[j] * pm, block_diag(p_bs[j])) for j in range(N_CHAINS)]
        ps = [ps[j] - _dot(p_bs[j], block_diag(ys[j].astype(BF16))) for j in range(N_CHAINS)]

    for j in range(N_CHAINS):
        uw = _dot(block_diag(ps[j].astype(BF16)), rhss[j])
        cols = slice(j * HEAD_DIM, (j + 1) * HEAD_DIM)
        u_ref[0, :, cols] = uw[:, :HEAD_DIM].astype(BF16)
        w_ref[0, :, cols] = uw[:, HEAD_DIM:].astype(BF16)


def _gdn_prep(qkv, conv_w, ba, bat, alog, dtb, row_len):
    b, l, _ = qkv.shape
    nblk = l // GDN_BLOCK
    wide = N_CHAINS * HEAD_DIM
    blk = lambda bi, i: (bi, i, 0)
    const = lambda bi, i: (0, 0)
    alog_r, dtb_r = alog.reshape(1, N_CHAINS), dtb.reshape(1, N_CHAINS)
    alog_c, dtb_c = alog.reshape(N_CHAINS, 1), dtb.reshape(N_CHAINS, 1)
    in_specs = [pl.BlockSpec((1, GDN_BLOCK, QKV_COLS), blk),
                pl.BlockSpec((3, QKV_COLS), const),
                pl.BlockSpec((1, GDN_BLOCK, N_GATE_COLS), blk),
                pl.BlockSpec((N_GATE_COLS, GDN_BLOCK), lambda bi, i: (0, bi * nblk + i)),
                pl.BlockSpec((1, N_CHAINS), const),
                pl.BlockSpec((1, N_CHAINS), const),
                pl.BlockSpec((N_CHAINS, 1), const),
                pl.BlockSpec((N_CHAINS, 1), const)]
    out_specs = [pl.BlockSpec((1, GDN_BLOCK, wide), blk)] * 4 + [
        pl.BlockSpec((1, 1, N_CHAINS * DELTA_CHUNK, GDN_BLOCK), lambda bi, i: (bi, i, 0, 0)),
        pl.BlockSpec((1, 1, 2, GDN_HEADS, CHUNKS_PER_BLOCK * HEAD_DIM),
                     lambda bi, i: (bi, i, 0, 0, 0))]
    n_in, n_out = len(in_specs), len(out_specs)

    def streamed(*refs):
        hbm, scratch = refs[:n_in + n_out], refs[n_in + n_out:]
        scratch[-1][0] = 1

        def body(*blocks):
            _gdn_prep_kernel(row_len, *blocks, *scratch)

        pltpu.emit_pipeline(body, grid=(b, nblk), in_specs=in_specs, out_specs=out_specs)(*hbm)

    any_spec = pl.BlockSpec(memory_space=pl.ANY)
    return pl.pallas_call(
        streamed,
        out_shape=(jax.ShapeDtypeStruct((b, l, wide), BF16),) * 4 + (
            jax.ShapeDtypeStruct((b, nblk, N_CHAINS * DELTA_CHUNK, GDN_BLOCK), BF16),
            jax.ShapeDtypeStruct((b, nblk, 2, GDN_HEADS, CHUNKS_PER_BLOCK * HEAD_DIM), F32)),
        in_specs=[any_spec] * n_in,
        out_specs=(any_spec,) * n_out,
        scratch_shapes=[pltpu.VMEM((3, GDN_BLOCK, GDN_BLOCK), BF16),
                        pltpu.VMEM((N_PAIR_LEVELS, DELTA_CHUNK, GDN_BLOCK), BF16),
                        pltpu.VMEM((GDN_BLOCK, CHUNKS_PER_BLOCK * HEAD_DIM), BF16),
                        pltpu.SMEM((1,), jnp.int32)],
        compiler_params=pltpu.CompilerParams(vmem_limit_bytes=VMEM_LIMIT),
        name="gdn_prep",
    )(qkv, conv_w, ba, bat, alog_r, dtb_r, alog_c, dtb_c)


def _gdn_scan_kernel(uf, wf, qf, kf, af, gf, ub, wb, qb, kb, ab, gb, s0_ref,
                     of_ref, ob_ref, sfin_ref, s_scr):
    i = pl.program_id(0)
    c = DELTA_CHUNK
    n_batch = s0_ref.shape[0]

    @pl.when(i == 0)
    def _():
        s_scr[...] = s0_ref[...]

    ops = ((uf, wf, qf, kf, af, gf, of_ref), (ub, wb, qb, kb, ab, gb, ob_ref))
    chains = [(bi, d, h) for bi in range(n_batch) for d in range(2) for h in range(GDN_HEADS)]
    states = [s_scr[bi, d * GDN_HEADS + h] for bi, d, h in chains]
    n_chunks = uf.shape[1] // c
    for step in range(n_chunks):
        def chunk(d):
            cc = step if d == 0 else n_chunks - 1 - step
            return cc, slice(cc * c, (cc + 1) * c)

        xs = []
        for j, (bi, d, h) in enumerate(chains):
            _, rows = chunk(d)
            cols = slice(h * HEAD_DIM, (h + 1) * HEAD_DIM)
            wq = jnp.concatenate([ops[d][1][bi, rows, cols], ops[d][2][bi, rows, cols]], axis=0)
            xs.append(_dot(wq, states[j].astype(BF16)))
        v_news = []
        for j, (bi, d, h) in enumerate(chains):
            _, rows = chunk(d)
            cols = slice(h * HEAD_DIM, (h + 1) * HEAD_DIM)
            v_news.append((ops[d][0][bi, rows, cols].astype(F32) - xs[j][:c]).astype(BF16))
        for j, (bi, d, h) in enumerate(chains):
            cc, rows = chunk(d)
            cols = slice(h * HEAD_DIM, (h + 1) * HEAD_DIM)
            blk, lc = divmod(cc, CHUNKS_PER_BLOCK)
            a_c = ops[d][4][bi, blk, h * c:(h + 1) * c, lc * c:(lc + 1) * c]
            ops[d][6][bi, rows, cols] = (xs[j][c:] + _dot(a_c, v_news[j])).astype(BF16)
            ds = _dot_tn(ops[d][3][bi, rows, cols], v_news[j])
            g_last = ops[d][5][bi, blk, 0, h:h + 1, lc * HEAD_DIM:(lc + 1) * HEAD_DIM]
            states[j] = states[j] * g_last + ds
    for j, (bi, d, h) in enumerate(chains):
        s_scr[bi, d * GDN_HEADS + h] = states[j]

    @pl.when(i == pl.num_programs(0) - 1)
    def _():
        sfin_ref[...] = s_scr[...]


def _gdn_scan(u, w, qd, kd, at, gl, s0):
    b, l, _ = u.shape
    n_sub = SCAN_BLOCKS_PER_STEP if (l // GDN_BLOCK) % SCAN_BLOCKS_PER_STEP == 0 else 1
    nblk = l // (n_sub * GDN_BLOCK)
    step_rows = n_sub * GDN_BLOCK
    half = GDN_HEADS * HEAD_DIM
    fwd = lambda i: (0, i, 0)
    bwd = lambda i: (0, nblk - 1 - i, 1)
    big = lambda m: pl.BlockSpec((b, step_rows, half), m)
    att_shape = (b, n_sub, GDN_HEADS * DELTA_CHUNK, GDN_BLOCK)
    attf = pl.BlockSpec(att_shape, lambda i: (0, i, 0, 0))
    attb = pl.BlockSpec(att_shape, lambda i: (0, nblk - 1 - i, 1, 0))
    gl_shape = (b, n_sub, 1, GDN_HEADS, CHUNKS_PER_BLOCK * HEAD_DIM)
    glf = pl.BlockSpec(gl_shape, lambda i: (0, i, 0, 0, 0))
    glb = pl.BlockSpec(gl_shape, lambda i: (0, nblk - 1 - i, 1, 0, 0))
    state = pl.BlockSpec((b, N_CHAINS, HEAD_DIM, HEAD_DIM), lambda i: (0, 0, 0, 0))
    return pl.pallas_call(
        _gdn_scan_kernel,
        out_shape=(jax.ShapeDtypeStruct((b, l, half), BF16),
                   jax.ShapeDtypeStruct((b, l, half), BF16),
                   jax.ShapeDtypeStruct((b, N_CHAINS, HEAD_DIM, HEAD_DIM), F32)),
        grid=(nblk,),
        in_specs=[big(fwd), big(fwd), big(fwd), big(fwd), attf, glf,
                  big(bwd), big(bwd), big(bwd), big(bwd), attb, glb, state],
        out_specs=(pl.BlockSpec((b, step_rows, half), fwd),
                   pl.BlockSpec((b, step_rows, half), lambda i: (0, nblk - 1 - i, 0)),
                   state),
        scratch_shapes=[pltpu.VMEM((b, N_CHAINS, HEAD_DIM, HEAD_DIM), F32)],
        compiler_params=_params("arbitrary"),
        name="gdn_scan",
    )(u, w, qd, kd, at, gl, u, w, qd, kd, at, gl, s0)


def _mix_into(mix_ref, of_ref, ob_ref, z_ref, u_ref, v_ref, gng_ref, lng_ref, lnb_ref, ws_ref,
              bs_ref):
    tm = of_ref.shape[0]
    o = of_ref[...].astype(F32) + ob_ref[...].astype(F32)
    z = z_ref[...].astype(F32)
    for h in range(GDN_HEADS):
        cols = slice(h * HEAD_DIM, (h + 1) * HEAD_DIM)
        oh = o[:, cols]
        zh = z[:, cols]
        r = lax.rsqrt(jnp.mean(oh * oh, -1, keepdims=True) + NORM_EPS)
        mix_ref[:, cols] = (oh * r * gng_ref[...] * (zh * _sigmoid(zh))).astype(BF16)
    for g in range(SGU_GROUPS):
        cols = slice(g * SGU_GROUP, (g + 1) * SGU_GROUP)
        vg = v_ref[:, cols].astype(F32)
        vc = vg - jnp.mean(vg, -1, keepdims=True)
        vn = vc * lax.rsqrt(jnp.mean(vc * vc, -1, keepdims=True) + NORM_EPS)
        vn = (vn * lng_ref[g:g + 1] + lnb_ref[g:g + 1]).astype(BF16)
        wsg = ws_ref[g].astype(BF16)
        for n in range(tm // SGU_CHUNK):
            rows = slice(n * SGU_CHUNK, (n + 1) * SGU_CHUNK)
            s = _dot(wsg, vn[rows]) + bs_ref[g]
            mix_ref[rows, GDN_W + g * SGU_GROUP:GDN_W + (g + 1) * SGU_GROUP] = (
                u_ref[rows, cols].astype(F32) * s).astype(BF16)


def _outproj_kernel(of_ref, ob_ref, z_ref, u_ref, v_ref, gng_ref, lng_ref, lnb_ref, ws_ref, bs_ref,
                    x_ref, mod_ref, ng_ref, wout_ref, rwt_ref, rb_ref,
                    x1_ref, h2_ref, lt_ref, mix_scr):
    subs = _sub_tiles(x_ref.shape[0])
    for sl in subs:
        _mix_into(mix_scr.at[sl], of_ref.at[sl], ob_ref.at[sl], z_ref.at[sl], u_ref.at[sl],
                  v_ref.at[sl], gng_ref, lng_ref, lnb_ref, ws_ref, bs_ref)
    mod = mod_ref[0]
    ys = [_dot(mix_scr[sl], wout_ref[...]) for sl in subs]
    w_hi, w_lo = _split2(rwt_ref[...])
    for sl, y in zip(subs, ys):
        x1 = x_ref[sl] + mod[2:3] * _rms(y, ng_ref[1:2])
        x1_ref[sl] = x1
        h2 = _rms(x1, ng_ref[2:3]) * (1.0 + mod[4:5]) + mod[3:4]
        h_hi, h_lo = _split2(h2)
        _store_planes(h2_ref, _pack_rows(h_hi), sl)
        lt_ref[:, sl] = (_dot_nt(w_hi, h_hi) + _dot_nt(w_lo, h_hi) + _dot_nt(w_hi, h_lo)
                         + rb_ref[...])


def _outproj(o_f, o_b, z, ug, vg, gdn_norm_g, ln_g, ln_b, w_s, b_s_full,
             x2d, mod, rows_per_mod, ng, wout, rwt, rb_col):
    t = x2d.shape[0]
    tm = min(ROW_TILE, t)
    tiles_per_mod = rows_per_mod // tm
    row = lambda i: (i, 0)
    const = lambda i: (0, 0)
    c3 = lambda i: (0, 0, 0)
    return pl.pallas_call(
        _outproj_kernel,
        out_shape=(jax.ShapeDtypeStruct((t, D_MODEL), F32),
                   jax.ShapeDtypeStruct((N_PLANES, t, SC_ROW_W), U32),
                   jax.ShapeDtypeStruct((N_EXPERTS, t), F32)),
        grid=(t // tm,),
        in_specs=[pl.BlockSpec((tm, GDN_W), row), pl.BlockSpec((tm, GDN_W), row),
                  pl.BlockSpec((tm, GDN_W), row), pl.BlockSpec((tm, SGU_W), row),
                  pl.BlockSpec((tm, SGU_W), row),
                  pl.BlockSpec((1, HEAD_DIM), const),
                  pl.BlockSpec((SGU_GROUPS, SGU_GROUP), const),
                  pl.BlockSpec((SGU_GROUPS, SGU_GROUP), const),
                  pl.BlockSpec((SGU_GROUPS, SGU_CHUNK, SGU_CHUNK), c3),
                  pl.BlockSpec((SGU_GROUPS, SGU_CHUNK, SGU_GROUP), c3),
                  pl.BlockSpec((tm, D_MODEL), row),
                  pl.BlockSpec((1, 6, D_MODEL), lambda i: (i // tiles_per_mod, 0, 0)),
                  pl.BlockSpec((4, D_MODEL), const),
                  pl.BlockSpec((D_MODEL, D_MODEL), const),
                  pl.BlockSpec((N_EXPERTS, D_MODEL), const),
                  pl.BlockSpec((N_EXPERTS, 1), const)],
        out_specs=(pl.BlockSpec((tm, D_MODEL), row),
                   pl.BlockSpec((N_PLANES, tm, SC_ROW_W), lambda i: (0, i, 0)),
                   pl.BlockSpec((N_EXPERTS, tm), lambda i: (0, i))),
        compiler_params=_params("parallel"),
        scratch_shapes=[pltpu.VMEM((tm, D_MODEL), BF16)],
        name="out_proj_router",
    )(o_f, o_b, z, ug, vg, gdn_norm_g, ln_g, ln_b, w_s, b_s_full, x2d, mod, ng, wout, rwt, rb_col)


def _moe_kernel(be_ref, slot_ref, next_ref, short_ref, nb_ref, xb_ref, wgu_hbm, bgu_ref, wd_hbm,
                bd_ref, y_ref, wgu_f, wd_f, wgu_b, wd_b, gut_scr, sems):
    i = pl.program_id(0)
    live = i < nb_ref[0]
    new_expert = jnp.logical_or(i == 0, be_ref[i] != be_ref[jnp.maximum(i - 1, 0)])

    def weight_copies(expert, slot):
        return (pltpu.make_async_copy(wgu_hbm.at[expert], wgu_f.at[slot], sems.at[slot, 0]),
                pltpu.make_async_copy(wd_hbm.at[expert], wd_f.at[slot], sems.at[slot, 1]))

    @pl.when(i == 0)
    def _():
        for copy in weight_copies(be_ref[0], 0):
            copy.start()

    @pl.when(jnp.logical_and(live, new_expert))
    def _():
        slot = slot_ref[i]
        for copy in weight_copies(be_ref[i], slot):
            copy.wait()

        @pl.when(next_ref[i] >= 0)
        def _():
            for copy in weight_copies(next_ref[i], 1 - slot):
                copy.start()
        wgu_b[...] = wgu_f[slot].astype(BF16)
        wd_b[...] = wd_f[slot].astype(BF16)

    def ffn(n_rows):
        rows = slice(0, n_rows)
        xb = _unpack_rows(_load_planes(xb_ref, rows)).astype(BF16)
        gu = _dot(xb, wgu_b[...]) + bgu_ref[0]
        gu_t = gu.T
        acts = []
        for part in range(n_rows // LANES):
            part_ref = gut_scr.at[part]
            part_ref[...] = gu_t[:, part * LANES:(part + 1) * LANES]
            gate = jnp.minimum(part_ref[pl.ds(0, D_FF, stride=2), :], SWIGLU_LIMIT)
            up = jnp.clip(part_ref[pl.ds(1, D_FF, stride=2), :], -SWIGLU_LIMIT, SWIGLU_LIMIT)
            acts.append(((up + 1.0) * gate * _sigmoid(SWIGLU_ALPHA * gate)).astype(BF16))
        act_t = jnp.concatenate(acts, axis=1)
        _store_planes(y_ref, _pack_rows(_dot_tn(act_t, wd_b[...]) + bd_ref[0]), rows)

    short = short_ref[i] != 0

    @pl.when(jnp.logical_and(live, jnp.logical_not(short)))
    def _():
        ffn(MOE_ROWS)

    @pl.when(jnp.logical_and(live, short))
    def _():
        ffn(MOE_ROWS // 2)
        for p in range(N_PLANES):
            y_ref[p, MOE_ROWS // 2:] = jnp.zeros((MOE_ROWS // 2, SC_ROW_W), U32)

    @pl.when(jnp.logical_not(live))
    def _():
        y_ref[...] = jnp.zeros_like(y_ref)


def _moe_ffn(block_e, n_used, real_end, xb, w_gu, b_gu, w_down, b_down):
    n_rows = xb.shape[1]
    n_blocks = n_rows // MOE_ROWS
    idx = jnp.arange(n_blocks, dtype=jnp.int32)
    live = idx < n_used[0]
    changed = jnp.concatenate([jnp.ones((1,), bool), block_e[1:] != block_e[:-1]]) & live
    slot = ((jnp.cumsum(changed.astype(jnp.int32)) - 1) % 2).astype(jnp.int32)
    change_at = jnp.where(changed, idx, n_blocks)
    next_change = lax.cummin(jnp.concatenate([change_at[1:], jnp.full((1,), n_blocks, jnp.int32)]),
                             reverse=True)
    next_e = jnp.where(next_change < n_blocks,
                       block_e[jnp.minimum(next_change, n_blocks - 1)], -1).astype(jnp.int32)
    n_real = jnp.clip(real_end[block_e] - idx * MOE_ROWS, 0, MOE_ROWS)
    short = (n_real <= MOE_ROWS // 2).astype(jnp.int32)

    row = lambda i, be, sl, nx, sh, nb: (0, i, 0)
    ex3 = lambda i, be, sl, nx, sh, nb: (be[i], 0, 0)
    live_row = lambda i, be, sl, nx, sh, nb: (0, jnp.minimum(i, nb[0] - 1), 0)
    planes = (N_PLANES, MOE_ROWS, SC_ROW_W)
    grid_spec = pltpu.PrefetchScalarGridSpec(
        num_scalar_prefetch=5,
        grid=(n_blocks,),
        in_specs=[pl.BlockSpec(planes, live_row),
                  pl.BlockSpec(memory_space=pl.ANY),
                  pl.BlockSpec((1, 1, 2 * D_FF), ex3),
                  pl.BlockSpec(memory_space=pl.ANY),
                  pl.BlockSpec((1, 1, D_MODEL), ex3)],
        out_specs=pl.BlockSpec(planes, row),
        scratch_shapes=[pltpu.VMEM((2, D_MODEL, 2 * D_FF), F32),
                        pltpu.VMEM((2, D_FF, D_MODEL), F32),
                        pltpu.VMEM((D_MODEL, 2 * D_FF), BF16),
                        pltpu.VMEM((D_FF, D_MODEL), BF16),
                        pltpu.VMEM((MOE_ROWS // LANES, 2 * D_FF, LANES), F32),
                        pltpu.SemaphoreType.DMA((2, 2))],
    )
    return pl.pallas_call(
        _moe_kernel,
        out_shape=jax.ShapeDtypeStruct((N_PLANES, n_rows, SC_ROW_W), U32),
        grid_spec=grid_spec,
        compiler_params=_params("arbitrary"),
        name="moe_ffn",
    )(block_e, slot, next_e, short, n_used, xb, w_gu, b_gu, w_down, b_down)


def _combine_kernel(y0_ref, y1_ref, y2_ref, y3_ref, gt_ref, x1_ref, mod_ref, ng_ref, o_ref):
    mod = mod_ref[0]
    tm = o_ref.shape[0]
    gt = jnp.concatenate([gt_ref[...], jnp.zeros((LANES - SUBLANES, tm), F32)], axis=0).T
    y = _unpack_rows(_load_planes(y0_ref)) * gt[:, 0:1]
    for k, y_ref in ((1, y1_ref), (2, y2_ref), (3, y3_ref)):
        y = y + _unpack_rows(_load_planes(y_ref)) * gt[:, k:k + 1]
    o_ref[...] = x1_ref[...] + mod[5:6] * _rms(y, ng_ref[3:4])


def _combine(yg, gates, x1, mod, rows_per_mod, ng):
    t = x1.shape[0]
    tm = min(COMBINE_TILE, t)
    tiles_per_mod = rows_per_mod // tm
    n_tiles = t // tm
    row = lambda i: (i, 0)
    choice = lambda k: pl.BlockSpec((N_PLANES, tm, SC_ROW_W), lambda i: (0, k * n_tiles + i, 0))
    return pl.pallas_call(
        _combine_kernel,
        out_shape=jax.ShapeDtypeStruct((t, D_MODEL), F32),
        grid=(n_tiles,),
        in_specs=[choice(0), choice(1), choice(2), choice(3),
                  pl.BlockSpec((SUBLANES, tm), lambda i: (0, i)),
                  pl.BlockSpec((tm, D_MODEL), row),
                  pl.BlockSpec((1, 6, D_MODEL), lambda i: (i // tiles_per_mod, 0, 0)),
                  pl.BlockSpec((4, D_MODEL), lambda i: (0, 0))],
        out_specs=pl.BlockSpec((tm, D_MODEL), row),
        compiler_params=_params("parallel"),
        name="moe_combine",
    )(yg, yg, yg, yg, gates, x1, mod, ng)


def _route_kernel(lt_ref, eidx_ref, gate_ref, rank_ref, cnt_ref, carry, earlier):
    i = pl.program_id(0)
    tile = earlier.shape[0]

    @pl.when(i == 0)
    def _():
        carry[...] = jnp.zeros_like(carry)
        ti = lax.broadcasted_iota(jnp.int32, (tile, tile), 0)
        tj = lax.broadcasted_iota(jnp.int32, (tile, tile), 1)
        earlier[...] = jnp.where(ti < tj, 1.0, 0.0).astype(BF16)

    eio = lax.broadcasted_iota(jnp.int32, (N_EXPERTS, tile), 0).astype(F32)
    seen = carry[...]
    for sub in range(lt_ref.shape[1] // tile):
        cols = slice(sub * tile, (sub + 1) * tile)
        logits = lt_ref[:, cols]
        vals, sels = [], []
        for k in range(TOP_K):
            m = jnp.max(logits, axis=0, keepdims=True)
            idx = jnp.min(jnp.where(logits == m, eio, float(N_EXPERTS)), axis=0, keepdims=True)
            sel = eio == idx
            logits = jnp.where(sel, -jnp.inf, logits)
            vals.append(m)
            sels.append(sel)
            eidx_ref[k:k + 1, cols] = idx.astype(jnp.int32)
        exps = [jnp.exp(v - vals[0]) for v in vals]
        denom = exps[0] + exps[1] + exps[2] + exps[3]
        for k in range(TOP_K):
            gate_ref[k:k + 1, cols] = exps[k] / denom
        gate_ref[TOP_K:, cols] = jnp.zeros((SUBLANES - TOP_K, tile), F32)

        member = jnp.where(sels[0] | sels[1] | sels[2] | sels[3], 1.0, 0.0)
        before = _dot(member.astype(BF16), earlier[...]) + seen
        for k in range(TOP_K):
            rank_ref[k:k + 1, cols] = jnp.sum(jnp.where(sels[k], before, 0.0), axis=0,
                                              keepdims=True).astype(jnp.int32)
        seen = seen + jnp.sum(member, axis=1, keepdims=True)
    carry[...] = seen
    cnt_ref[...] = seen.astype(jnp.int32)


def _route(logits_t):
    t = logits_t.shape[1]
    block = min(ROUTE_BLOCK, t)
    tile = min(ROUTE_TILE, block)
    blk = lambda i: (0, i)
    return pl.pallas_call(
        _route_kernel,
        out_shape=(jax.ShapeDtypeStruct((TOP_K, t), jnp.int32),
                   jax.ShapeDtypeStruct((SUBLANES, t), F32),
                   jax.ShapeDtypeStruct((TOP_K, t), jnp.int32),
                   jax.ShapeDtypeStruct((N_EXPERTS, 1), jnp.int32)),
        grid=(t // block,),
        in_specs=[pl.BlockSpec((N_EXPERTS, block), blk)],
        out_specs=(pl.BlockSpec((TOP_K, block), blk), pl.BlockSpec((SUBLANES, block), blk),
                   pl.BlockSpec((TOP_K, block), blk),
                   pl.BlockSpec((N_EXPERTS, 1), lambda i: (0, 0))),
        scratch_shapes=[pltpu.VMEM((N_EXPERTS, 1), F32), pltpu.VMEM((tile, tile), BF16)],
        compiler_params=_params("arbitrary"),
        name="moe_route",
    )(logits_t)


def _slot_tables(eidx, rank, counts, n_blocks):
    padded = (counts + MOE_ROWS - 1) // MOE_ROWS * MOE_ROWS
    pad_end = jnp.cumsum(padded)
    pad_start = pad_end - padded
    experts = jnp.arange(N_EXPERTS, dtype=jnp.int32)
    dest = rank + jnp.sum(jnp.where(eidx[..., None] == experts, pad_start, 0), axis=-1)
    first_row = jnp.arange(n_blocks, dtype=jnp.int32)[:, None] * MOE_ROWS
    block_e = jnp.minimum(jnp.sum((pad_end[None, :] <= first_row).astype(jnp.int32), axis=1),
                          N_EXPERTS - 1)
    n_used = pad_end[-1:] // MOE_ROWS
    real_end = (pad_start + counts).astype(jnp.int32)
    return (dest.astype(jnp.int32), pad_start.astype(jnp.int32), block_e, n_used.astype(jnp.int32),
            real_end)


def _sc_mesh():
    return plsc.VectorSubcoreMesh(core_axis_name="c", subcore_axis_name="s",
                                  num_cores=SC_CORES, num_subcores=SC_SUBCORES)


def _plane_row_ids(rows, rows_per_plane):
    return jnp.concatenate([rows + p * rows_per_plane for p in range(N_PLANES)], axis=-1)


def _sc_gather_rows(table, rows):
    v = table.shape[1]
    idx = _plane_row_ids(rows, v)[None]
    n_all = idx.shape[1]

    @functools.partial(pl.kernel, mesh=_sc_mesh(), name="moe_gather_rows",
                       out_type=jax.ShapeDtypeStruct((n_all, SC_ROW_W), U32))
    def gather(x_hbm, i_hbm, o_hbm):
        def body(i_vmem, o_vmem):
            pltpu.sync_copy(x_hbm.at[i_vmem.at[0]], o_vmem)

        pltpu.emit_pipeline(
            body, grid=(n_all // SC_WINDOW,),
            in_specs=[pl.BlockSpec((1, SC_WINDOW), lambda i: (0, i))],
            out_specs=[pl.BlockSpec((SC_WINDOW, SC_ROW_W), lambda i: (i, 0))],
            core_axis_name=("c", "s"), dimension_semantics=(pltpu.PARALLEL,),
        )(i_hbm, o_hbm)

    return gather(table.reshape(N_PLANES * v, SC_ROW_W), idx).reshape(N_PLANES, -1, SC_ROW_W)


def _sc_scatter_rows(rows, dest, n_out):
    t = rows.shape[1]
    idx = _plane_row_ids(dest, n_out)

    @functools.partial(pl.kernel, mesh=_sc_mesh(), name="moe_scatter_rows", scratch_types=[],
                       out_type=jax.ShapeDtypeStruct((N_PLANES * n_out, SC_ROW_W), U32))
    def scatter(x_hbm, i_hbm, o_hbm):
        def body(x_vmem, i_vmem):
            for k in range(TOP_K):
                pltpu.sync_copy(x_vmem, o_hbm.at[i_vmem.at[k]])

        pltpu.emit_pipeline(
            body, grid=(N_PLANES * t // SC_WINDOW,),
            in_specs=[pl.BlockSpec((SC_WINDOW, SC_ROW_W), lambda i: (i, 0)),
                      pl.BlockSpec((TOP_K, SC_WINDOW), lambda i: (0, i))],
            out_specs=[],
            core_axis_name=("c", "s"), dimension_semantics=(pltpu.PARALLEL,),
        )(x_hbm, i_hbm)

    return scatter(rows.reshape(N_PLANES * t, SC_ROW_W), idx).reshape(N_PLANES, n_out, SC_ROW_W)


def _zero_pad_kernel(cnt_ref, start_ref, xb_in_ref, xb_ref, zero_scr, sem):
    del xb_in_ref
    zero_scr[...] = jnp.zeros_like(zero_scr)

    pieces = [SUBLANES << bit for bit in range((MOE_ROWS // SUBLANES - 1).bit_length())]

    def zero_copy(p, row, size):
        return pltpu.make_async_copy(zero_scr.at[pl.ds(0, size)], xb_ref.at[p, pl.ds(row, size)], sem)

    def for_each_piece(fn):
        def per_expert(e, carry):
            n_real = cnt_ref[e]
            n_pad = (MOE_ROWS - n_real % MOE_ROWS) % MOE_ROWS
            first = start_ref[e] + n_real
            n_single = n_pad % SUBLANES
            for j in range(SUBLANES - 1):
                @pl.when(j < n_single)
                def _():
                    for p in range(N_PLANES):
                        fn(zero_copy(p, first + j, 1))
            row = first + n_single
            for size in pieces:
                @pl.when((n_pad & size) != 0)
                def _():
                    for p in range(N_PLANES):
                        fn(zero_copy(p, pl.multiple_of(row, SUBLANES), size))
                row = row + (n_pad & size)
            return carry
        lax.fori_loop(0, N_EXPERTS, per_expert, 0)

    for_each_piece(lambda copy: copy.start())
    for_each_piece(lambda copy: copy.wait())


def _zero_pad_slots(counts, pad_start, xb):
    grid_spec = pltpu.PrefetchScalarGridSpec(
        num_scalar_prefetch=2,
        grid=(1,),
        in_specs=[pl.BlockSpec(memory_space=pl.ANY)],
        out_specs=pl.BlockSpec(memory_space=pl.ANY),
        scratch_shapes=[pltpu.VMEM((MOE_ROWS // 2, SC_ROW_W), U32), pltpu.SemaphoreType.DMA],
    )
    return pl.pallas_call(
        _zero_pad_kernel,
        out_shape=jax.ShapeDtypeStruct(xb.shape, xb.dtype),
        grid_spec=grid_spec,
        input_output_aliases={2: 0},
        compiler_params=_params("arbitrary"),
        name="moe_zero_pad",
    )(counts, pad_start, xb)


def kernel(x, c, ctx, c_ctx, w_ada, b_ada, norm_g, w_in, conv_w, a_log, dt_bias, gdn_norm_g,
           sgu_ln_g, sgu_ln_b, sgu_w, sgu_b, w_out, router_w, router_b, w_gu, b_gu, w_down, b_down):
    b, l, d = x.shape
    lc = ctx.shape[1]
    t = b * l
    assert d == D_MODEL and l % ROW_TILE == 0 and l % GDN_BLOCK == 0 and lc % GDN_BLOCK == 0
    assert w_ada.shape[0] == 1, "single-layer block"

    cs = jnp.concatenate([c, c_ctx[None], jnp.zeros((8 - b - 1, d), F32)], axis=0)
    mod_all = _ada(cs, w_ada[0], b_ada[0][None])
    mod = mod_all[:b].reshape(b, 6, d)
    mod_c = mod_all[b:b + 1].reshape(1, 6, d)
    ng = norm_g[0]


    x2d = x.reshape(t, d)
    qkv, z, ug, vg, ba, bat = _inproj(x2d, mod, l, ng[0:1], w_in[0])
    ctx2d = ctx.reshape(b * lc, d)
    qkv_c, _, _, _, ba_c, bat_c = _inproj(ctx2d, mod_c, b * lc, ng[0:1], w_in[0])

    alog = a_log[0].reshape(-1)
    dtb = dt_bias[0].reshape(-1)
    pc = _gdn_prep(qkv_c.reshape(b, lc, QKV_COLS), conv_w[0], ba_c.reshape(b, lc, N_GATE_COLS),
                   bat_c, alog, dtb, lc)
    s_zero = jnp.zeros((b, N_CHAINS, HEAD_DIM, HEAD_DIM), F32)
    _, _, s_ctx = _gdn_scan(*pc, s_zero)
    pp = _gdn_prep(qkv.reshape(b, l, QKV_COLS), conv_w[0], ba.reshape(b, l, N_GATE_COLS),
                   bat, alog, dtb, GRID_W)
    o_f, o_b, _ = _gdn_scan(*pp, s_ctx)

    b_s_full = jnp.broadcast_to(sgu_b[0][:, :, None], (SGU_GROUPS, SGU_CHUNK, SGU_GROUP))
    x1, h2p, logits_t = _outproj(o_f.reshape(t, GDN_W), o_b.reshape(t, GDN_W), z, ug, vg,
                                 gdn_norm_g, sgu_ln_g[0], sgu_ln_b[0], sgu_w[0], b_s_full,
                                 x2d, mod, l, ng, w_out[0].astype(BF16),
                                 router_w[0].T, router_b[0][:, None])

    eidx, gates_t, rank, counts = _route(logits_t)
    counts = counts[:, 0]
    n_blocks = -(-(t * TOP_K) // MOE_ROWS) + N_EXPERTS
    dest, pad_start, block_e, n_used, real_end = _slot_tables(eidx, rank, counts, n_blocks)
    xb = _sc_scatter_rows(h2p, dest, n_blocks * MOE_ROWS)
    xb = _zero_pad_slots(counts, pad_start, xb)
    yb = _moe_ffn(block_e, n_used, real_end, xb, w_gu[0], b_gu[0][:, None, :], w_down[0],
                  b_down[0][:, None, :])
    yg = _sc_gather_rows(yb, dest.reshape(-1))
    out = _combine(yg, gates_t, x1, mod, l, ng)
    return out.reshape(b, l, d)
```
